```python
import jax, jax.numpy as jnp
from jax import lax
import numpy as np

D_MODEL = 2048
BATCH = 8
SEQ = 8192
DEPTH = 4

EPS = 1e-6
D_A = D_MODEL // 2
CONV_WIDTH = 31
D_B = D_MODEL // 2
CHUNK = 128
HEAD_DIM_B = 128
N_HEADS_B = D_B // HEAD_DIM_B
D_IN_AB = 2 * D_A + 2 * D_B
POOL_WINDOWS = (2, 4, 8, 16)
N_POOL_GROUPS = len(POOL_WINDOWS)
D_POOL_GROUP = D_MODEL // N_POOL_GROUPS
D_FF = -(-8 * D_MODEL // (3 * 256)) * 256
N_MOD = 6
N_EVEN = (DEPTH + 1) // 2
N_ODD = DEPTH // 2

kernel_name = "hybrid_conv_gmlp_pool_adaln_trunk"


def rmsnorm(x, g):
    xf = x.astype(jnp.float32)
    y = xf * lax.rsqrt(jnp.mean(xf * xf, axis=-1, keepdims=True) + EPS)
    return (y * g.astype(jnp.float32)).astype(x.dtype)


def layernorm(x, g, b):
    xf = x.astype(jnp.float32)
    mu = jnp.mean(xf, axis=-1, keepdims=True)
    xc = xf - mu
    y = xc * lax.rsqrt(jnp.mean(xc * xc, axis=-1, keepdims=True) + EPS)
    return (y * g.astype(jnp.float32) + b.astype(jnp.float32)).astype(x.dtype)


def modulate(h, shift, scale):
    return h * (1.0 + scale[:, None, :]) + shift[:, None, :]


def mixer_ab(h, w_in, conv_w, conv_b, a_g, a_b, v_g, v_b, w_s, s_bias, w_out):
    bsz, s, _ = h.shape
    proj = h @ w_in
    a_val = proj[..., :D_A]
    a_gate = proj[..., D_A:2 * D_A]
    b_u = proj[..., 2 * D_A:2 * D_A + D_B]
    b_v = proj[..., 2 * D_A + D_B:]

    a = a_val * jax.nn.sigmoid(a_gate)
    a = lax.conv_general_dilated(
        a, conv_w[:, None, :].astype(a.dtype), window_strides=(1,),
        padding=[(CONV_WIDTH - 1, 0)],
        dimension_numbers=('NWC', 'WIO', 'NWC'),
        feature_group_count=D_A) + conv_b
    a = jax.nn.silu(layernorm(a, a_g, a_b))

    v = layernorm(b_v, v_g, v_b).reshape(bsz, s // CHUNK, CHUNK, N_HEADS_B, HEAD_DIM_B)
    causal = jnp.tril(jnp.ones((CHUNK, CHUNK), dtype=bool))
    w_c = jnp.where(causal[None], w_s, jnp.zeros_like(w_s))
    v = jnp.einsum('hts,bnshd->bnthd', w_c, v) + s_bias.T[:, :, None]
    b_out = b_u * v.reshape(bsz, s, D_B)

    return jnp.concatenate([a, b_out], axis=-1) @ w_out


def mixer_c(h, pool_w, pool_scale):
    bsz, s, _ = h.shape
    hf = h.astype(jnp.float32)
    cs = jnp.cumsum(hf, axis=1)
    count_base = (jnp.arange(s) + 1).astype(jnp.float32)
    pooled = []
    for gi, w in enumerate(POOL_WINDOWS):
        sl = slice(gi * D_POOL_GROUP, (gi + 1) * D_POOL_GROUP)
        cg = cs[..., sl]
        lagged = jnp.pad(cg[:, :s - w], ((0, 0), (w, 0), (0, 0)))
        cnt = jnp.minimum(count_base, float(w))[None, :, None]
        pooled.append(((cg - lagged) / cnt - hf[..., sl]).astype(h.dtype))
    p = jnp.stack(pooled, axis=2)
    y = jnp.einsum('bsgi,gio->bsgo', p, pool_w).reshape(bsz, s, D_MODEL)
    return y * pool_scale


def swiglu(h, w1, w3, w2):
    return (jax.nn.silu(h @ w1) * (h @ w3)) @ w2


def _fwd_setup_inputs(seed: int = 0) -> dict:
    key = jax.random.key(seed)
    ks = jax.random.split(key, 24)
    f32 = jnp.float32
    nrm = lambda k, shape, s: (jax.random.normal(k, shape, f32) * s)
    gain = lambda k, shape: 1.0 + 0.02 * jax.random.normal(k, shape, f32)
    return {
        "x": nrm(ks[0], (BATCH, SEQ, D_MODEL), 1.0),
        "c": nrm(ks[1], (BATCH, D_MODEL), 1.0),
        "ada_w": nrm(ks[2], (DEPTH, D_MODEL, N_MOD * D_MODEL), 0.5 * D_MODEL ** -0.5),
        "ada_b": nrm(ks[3], (DEPTH, N_MOD * D_MODEL), 0.02),
        "norm_mix_g": gain(ks[4], (DEPTH, D_MODEL)),
        "norm_ffn_g": gain(ks[5], (DEPTH, D_MODEL)),
        "ab_w_in": nrm(ks[6], (N_EVEN, D_MODEL, D_IN_AB), D_MODEL ** -0.5),
        "a_conv_w": nrm(ks[7], (N_EVEN, CONV_WIDTH, D_A), CONV_WIDTH ** -0.5),
        "a_conv_b": nrm(ks[8], (N_EVEN, D_A), 0.02),
        "a_norm_g": gain(ks[9], (N_EVEN, D_A)),
        "a_norm_b": nrm(ks[10], (N_EVEN, D_A), 0.02),
        "b_norm_g": gain(ks[11], (N_EVEN, D_B)),
        "b_norm_b": nrm(ks[12], (N_EVEN, D_B), 0.02),
        "b_w_s": nrm(ks[13], (N_EVEN, N_HEADS_B, CHUNK, CHUNK), CHUNK ** -0.5),
        "b_bias": gain(ks[14], (N_EVEN, N_HEADS_B, CHUNK)),
        "ab_w_out": nrm(ks[15], (N_EVEN, D_A + D_B, D_MODEL), (D_A + D_B) ** -0.5),
        "pool_w": nrm(ks[16], (N_ODD, N_POOL_GROUPS, D_POOL_GROUP, D_POOL_GROUP), D_POOL_GROUP ** -0.5),
        "pool_scale": 1.0 + 0.1 * jax.random.normal(ks[17], (N_ODD, D_MODEL), f32),
        "ffn_w1": nrm(ks[18], (DEPTH, D_MODEL, D_FF), D_MODEL ** -0.5),
        "ffn_w3": nrm(ks[19], (DEPTH, D_MODEL, D_FF), D_MODEL ** -0.5),
        "ffn_w2": nrm(ks[20], (DEPTH, D_FF, D_MODEL), D_FF ** -0.5),
        "final_g": gain(ks[21], (D_MODEL,)),
    }


def _fwd_reference(x, c, ada_w, ada_b, norm_mix_g, norm_ffn_g, ab_w_in, a_conv_w, a_conv_b,
              a_norm_g, a_norm_b, b_norm_g, b_norm_b, b_w_s, b_bias, ab_w_out,
              pool_w, pool_scale, ffn_w1, ffn_w3, ffn_w2, final_g):
    cond = jax.nn.silu(c)
    for l in range(DEPTH):
        mod = cond @ ada_w[l] + ada_b[l]
        sh1, sc1, g1, sh2, sc2, g2 = jnp.split(mod, N_MOD, axis=-1)
        h = modulate(rmsnorm(x, norm_mix_g[l]), sh1, sc1)
        i = l // 2
        if l % 2 == 0:
            y = mixer_ab(h, ab_w_in[i], a_conv_w[i], a_conv_b[i], a_norm_g[i], a_norm_b[i],
                         b_norm_g[i], b_norm_b[i], b_w_s[i], b_bias[i], ab_w_out[i])
        else:
            y = mixer_c(h, pool_w[i], pool_scale[i])
        x = x + g1[:, None, :] * y
        h = modulate(rmsnorm(x, norm_ffn_g[l]), sh2, sc2)
        x = x + g2[:, None, :] * swiglu(h, ffn_w1[l], ffn_w3[l], ffn_w2[l])
    return rmsnorm(x, final_g)


import jax as _jax
import jax.numpy as _jnp

TWIN_FORMAT = 'train_step'
FWD_PARAMS = ['x', 'c', 'ada_w', 'ada_b', 'norm_mix_g', 'norm_ffn_g', 'ab_w_in', 'a_conv_w', 'a_conv_b', 'a_norm_g', 'a_norm_b', 'b_norm_g', 'b_norm_b', 'b_w_s', 'b_bias', 'ab_w_out', 'pool_w', 'pool_scale', 'ffn_w1', 'ffn_w3', 'ffn_w2', 'final_g']
TWIN_WEIGHTS = ['ada_w', 'ada_b', 'norm_mix_g', 'norm_ffn_g', 'ab_w_in', 'a_conv_w', 'a_conv_b', 'a_norm_g', 'a_norm_b', 'b_norm_g', 'b_norm_b', 'b_w_s', 'b_bias', 'ab_w_out', 'pool_w', 'pool_scale', 'ffn_w1', 'ffn_w3', 'ffn_w2', 'final_g']
TWIN_DIFF_INPUT = 'x'
TWIN_INPUTS = ['x', 'c', 'ada_w', 'ada_b', 'norm_mix_g', 'norm_ffn_g', 'ab_w_in', 'a_conv_w', 'a_conv_b', 'a_norm_g', 'a_norm_b', 'b_norm_g', 'b_norm_b', 'b_w_s', 'b_bias', 'ab_w_out', 'pool_w', 'pool_scale', 'ffn_w1', 'ffn_w3', 'ffn_w2', 'final_g', 'loss_target', 'm_ada_w', 'm_ada_b', 'm_norm_mix_g', 'm_norm_ffn_g', 'm_ab_w_in', 'm_a_conv_w', 'm_a_conv_b', 'm_a_norm_g', 'm_a_norm_b', 'm_b_norm_g', 'm_b_norm_b', 'm_b_w_s', 'm_b_bias', 'm_ab_w_out', 'm_pool_w', 'm_pool_scale', 'm_ffn_w1', 'm_ffn_w3', 'm_ffn_w2', 'm_final_g', 'v_ada_w', 'v_ada_b', 'v_norm_mix_g', 'v_norm_ffn_g', 'v_ab_w_in', 'v_a_conv_w', 'v_a_conv_b', 'v_a_norm_g', 'v_a_norm_b', 'v_b_norm_g', 'v_b_norm_b', 'v_b_w_s', 'v_b_bias', 'v_ab_w_out', 'v_pool_w', 'v_pool_scale', 'v_ffn_w1', 'v_ffn_w3', 'v_ffn_w2', 'v_final_g']
TWIN_OUTPUTS = ['loss', 'grad_x', 'grad_ada_w', 'grad_ada_b', 'grad_norm_mix_g', 'grad_norm_ffn_g', 'grad_ab_w_in', 'grad_a_conv_w', 'grad_a_conv_b', 'grad_a_norm_g', 'grad_a_norm_b', 'grad_b_norm_g', 'grad_b_norm_b', 'grad_b_w_s', 'grad_b_bias', 'grad_ab_w_out', 'grad_pool_w', 'grad_pool_scale', 'grad_ffn_w1', 'grad_ffn_w3', 'grad_ffn_w2', 'grad_final_g', 'delta_ada_w', 'delta_ada_b', 'delta_norm_mix_g', 'delta_norm_ffn_g', 'delta_ab_w_in', 'delta_a_conv_w', 'delta_a_conv_b', 'delta_a_norm_g', 'delta_a_norm_b', 'delta_b_norm_g', 'delta_b_norm_b', 'delta_b_w_s', 'delta_b_bias', 'delta_ab_w_out', 'delta_pool_w', 'delta_pool_scale', 'delta_ffn_w1', 'delta_ffn_w3', 'delta_ffn_w2', 'delta_final_g', 'new_m_ada_w', 'new_m_ada_b', 'new_m_norm_mix_g', 'new_m_norm_ffn_g', 'new_m_ab_w_in', 'new_m_a_conv_w', 'new_m_a_conv_b', 'new_m_a_norm_g', 'new_m_a_norm_b', 'new_m_b_norm_g', 'new_m_b_norm_b', 'new_m_b_w_s', 'new_m_b_bias', 'new_m_ab_w_out', 'new_m_pool_w', 'new_m_pool_scale', 'new_m_ffn_w1', 'new_m_ffn_w3', 'new_m_ffn_w2', 'new_m_final_g', 'new_v_ada_w', 'new_v_ada_b', 'new_v_norm_mix_g', 'new_v_norm_ffn_g', 'new_v_ab_w_in', 'new_v_a_conv_w', 'new_v_a_conv_b', 'new_v_a_norm_g', 'new_v_a_norm_b', 'new_v_b_norm_g', 'new_v_b_norm_b', 'new_v_b_w_s', 'new_v_b_bias', 'new_v_ab_w_out', 'new_v_pool_w', 'new_v_pool_scale', 'new_v_ffn_w1', 'new_v_ffn_w3', 'new_v_ffn_w2', 'new_v_final_g']
TWIN_LEAF_KINDS = {'loss': 'loss', 'grad_x': 'grad_x', 'grad_ada_w': 'grad_w', 'grad_ada_b': 'grad_w', 'grad_norm_mix_g': 'grad_w', 'grad_norm_ffn_g': 'grad_w', 'grad_ab_w_in': 'grad_w', 'grad_a_conv_w': 'grad_w', 'grad_a_conv_b': 'grad_w', 'grad_a_norm_g': 'grad_w', 'grad_a_norm_b': 'grad_w', 'grad_b_norm_g': 'grad_w', 'grad_b_norm_b': 'grad_w', 'grad_b_w_s': 'grad_w', 'grad_b_bias': 'grad_w', 'grad_ab_w_out': 'grad_w', 'grad_pool_w': 'grad_w', 'grad_pool_scale': 'grad_w', 'grad_ffn_w1': 'grad_w', 'grad_ffn_w3': 'grad_w', 'grad_ffn_w2': 'grad_w', 'grad_final_g': 'grad_w', 'delta_ada_w': 'delta_w', 'delta_ada_b': 'delta_w', 'delta_norm_mix_g': 'delta_w', 'delta_norm_ffn_g': 'delta_w', 'delta_ab_w_in': 'delta_w', 'delta_a_conv_w': 'delta_w', 'delta_a_conv_b': 'delta_w', 'delta_a_norm_g': 'delta_w', 'delta_a_norm_b': 'delta_w', 'delta_b_norm_g': 'delta_w', 'delta_b_norm_b': 'delta_w', 'delta_b_w_s': 'delta_w', 'delta_b_bias': 'delta_w', 'delta_ab_w_out': 'delta_w', 'delta_pool_w': 'delta_w', 'delta_pool_scale': 'delta_w', 'delta_ffn_w1': 'delta_w', 'delta_ffn_w3': 'delta_w', 'delta_ffn_w2': 'delta_w', 'delta_final_g': 'delta_w', 'new_m_ada_w': 'new_m', 'new_m_ada_b': 'new_m', 'new_m_norm_mix_g': 'new_m', 'new_m_norm_ffn_g': 'new_m', 'new_m_ab_w_in': 'new_m', 'new_m_a_conv_w': 'new_m', 'new_m_a_conv_b': 'new_m', 'new_m_a_norm_g': 'new_m', 'new_m_a_norm_b': 'new_m', 'new_m_b_norm_g': 'new_m', 'new_m_b_norm_b': 'new_m', 'new_m_b_w_s': 'new_m', 'new_m_b_bias': 'new_m', 'new_m_ab_w_out': 'new_m', 'new_m_pool_w': 'new_m', 'new_m_pool_scale': 'new_m', 'new_m_ffn_w1': 'new_m', 'new_m_ffn_w3': 'new_m', 'new_m_ffn_w2': 'new_m', 'new_m_final_g': 'new_m', 'new_v_ada_w': 'new_v', 'new_v_ada_b': 'new_v', 'new_v_norm_mix_g': 'new_v', 'new_v_norm_ffn_g': 'new_v', 'new_v_ab_w_in': 'new_v', 'new_v_a_conv_w': 'new_v', 'new_v_a_conv_b': 'new_v', 'new_v_a_norm_g': 'new_v', 'new_v_a_norm_b': 'new_v', 'new_v_b_norm_g': 'new_v', 'new_v_b_norm_b': 'new_v', 'new_v_b_w_s': 'new_v', 'new_v_b_bias': 'new_v', 'new_v_ab_w_out': 'new_v', 'new_v_pool_w': 'new_v', 'new_v_pool_scale': 'new_v', 'new_v_ffn_w1': 'new_v', 'new_v_ffn_w3': 'new_v', 'new_v_ffn_w2': 'new_v', 'new_v_final_g': 'new_v'}


def _forward(args):
    return _fwd_reference(*[args[k] for k in FWD_PARAMS])


def _output_shape():
    def fwd():
        inp = _fwd_setup_inputs(0)
        return _fwd_reference(*[inp[k] for k in FWD_PARAMS])
    out = _jax.eval_shape(fwd)
    return out.shape, out.dtype

N_MICROBATCH = 1
ADAM_LR = 0.001
ADAM_B1 = 0.9
ADAM_B2 = 0.999
ADAM_EPS = 1e-08
ADAM_WD = 0.01
ADAM_STEP = 10
PER_EXAMPLE_BATCH_AXIS = {'x': 0, 'c': 0, 'loss_target': 0}
SHARED_INPUTS = []
_WEIGHT_DTYPES = {'ada_w': _jnp.float32, 'ada_b': _jnp.float32, 'norm_mix_g': _jnp.float32, 'norm_ffn_g': _jnp.float32, 'ab_w_in': _jnp.float32, 'a_conv_w': _jnp.float32, 'a_conv_b': _jnp.float32, 'a_norm_g': _jnp.float32, 'a_norm_b': _jnp.float32, 'b_norm_g': _jnp.float32, 'b_norm_b': _jnp.float32, 'b_w_s': _jnp.float32, 'b_bias': _jnp.float32, 'ab_w_out': _jnp.float32, 'pool_w': _jnp.float32, 'pool_scale': _jnp.float32, 'ffn_w1': _jnp.float32, 'ffn_w3': _jnp.float32, 'ffn_w2': _jnp.float32, 'final_g': _jnp.float32}
MOMENT_SCALE = {'ada_w': 3.993357e-02, 'ada_b': 7.008677e-02, 'norm_mix_g': 3.687286e-02, 'norm_ffn_g': 3.371231e-02, 'ab_w_in': 3.009053e-02, 'a_conv_w': 2.237948e-02, 'a_conv_b': 4.142274e-02, 'a_norm_g': 2.527329e-02, 'a_norm_b': 2.272317e-02, 'b_norm_g': 2.842222e-02, 'b_norm_b': 2.791503e-02, 'b_w_s': 2.741951e-02, 'b_bias': 3.905151e-02, 'ab_w_out': 3.699203e-02, 'pool_w': 3.120319e-02, 'pool_scale': 5.660385e-02, 'ffn_w1': 1.503763e-02, 'ffn_w3': 1.455348e-02, 'ffn_w2': 2.412613e-02, 'final_g': 3.210456e+01}


def _to_microbatches(a, axis):
    t = _jnp.moveaxis(a, axis, 0)
    t = t.reshape((N_MICROBATCH, t.shape[0] // N_MICROBATCH) + t.shape[1:])
    return _jnp.moveaxis(t, 1, axis + 1)


def setup_inputs(seed: int = 0) -> dict:
    inp = _fwd_setup_inputs(seed)
    key = _jax.random.fold_in(_jax.random.key(seed), 7919)
    shape, _ = _output_shape()
    out = dict(inp)
    out["loss_target"] = _jax.random.normal(_jax.random.fold_in(key, 0), shape, _jnp.float32)
    for i, name in enumerate(TWIN_WEIGHTS):
        w = inp[name].astype(_jnp.float32)
        if MOMENT_SCALE is None:
            s = _jnp.sqrt(_jnp.mean(_jnp.square(w)) + 1e-30)
        else:
            s = MOMENT_SCALE[name]
        km, kv = _jax.random.split(_jax.random.fold_in(key, i + 1))
        out[name] = w
        out["m_" + name] = s * _jax.random.normal(km, w.shape, _jnp.float32)
        out["v_" + name] = (s * s) * _jax.random.uniform(kv, w.shape, _jnp.float32, 0.5, 1.5)
    if N_MICROBATCH > 1:
        for name, axis in PER_EXAMPLE_BATCH_AXIS.items():
            out[name] = _to_microbatches(out[name], axis)
    return {'x': out['x'], 'c': out['c'], 'ada_w': out['ada_w'], 'ada_b': out['ada_b'], 'norm_mix_g': out['norm_mix_g'], 'norm_ffn_g': out['norm_ffn_g'], 'ab_w_in': out['ab_w_in'], 'a_conv_w': out['a_conv_w'], 'a_conv_b': out['a_conv_b'], 'a_norm_g': out['a_norm_g'], 'a_norm_b': out['a_norm_b'], 'b_norm_g': out['b_norm_g'], 'b_norm_b': out['b_norm_b'], 'b_w_s': out['b_w_s'], 'b_bias': out['b_bias'], 'ab_w_out': out['ab_w_out'], 'pool_w': out['pool_w'], 'pool_scale': out['pool_scale'], 'ffn_w1': out['ffn_w1'], 'ffn_w3': out['ffn_w3'], 'ffn_w2': out['ffn_w2'], 'final_g': out['final_g'], 'loss_target': out['loss_target'], 'm_ada_w': out['m_ada_w'], 'm_ada_b': out['m_ada_b'], 'm_norm_mix_g': out['m_norm_mix_g'], 'm_norm_ffn_g': out['m_norm_ffn_g'], 'm_ab_w_in': out['m_ab_w_in'], 'm_a_conv_w': out['m_a_conv_w'], 'm_a_conv_b': out['m_a_conv_b'], 'm_a_norm_g': out['m_a_norm_g'], 'm_a_norm_b': out['m_a_norm_b'], 'm_b_norm_g': out['m_b_norm_g'], 'm_b_norm_b': out['m_b_norm_b'], 'm_b_w_s': out['m_b_w_s'], 'm_b_bias': out['m_b_bias'], 'm_ab_w_out': out['m_ab_w_out'], 'm_pool_w': out['m_pool_w'], 'm_pool_scale': out['m_pool_scale'], 'm_ffn_w1': out['m_ffn_w1'], 'm_ffn_w3': out['m_ffn_w3'], 'm_ffn_w2': out['m_ffn_w2'], 'm_final_g': out['m_final_g'], 'v_ada_w': out['v_ada_w'], 'v_ada_b': out['v_ada_b'], 'v_norm_mix_g': out['v_norm_mix_g'], 'v_norm_ffn_g': out['v_norm_ffn_g'], 'v_ab_w_in': out['v_ab_w_in'], 'v_a_conv_w': out['v_a_conv_w'], 'v_a_conv_b': out['v_a_conv_b'], 'v_a_norm_g': out['v_a_norm_g'], 'v_a_norm_b': out['v_a_norm_b'], 'v_b_norm_g': out['v_b_norm_g'], 'v_b_norm_b': out['v_b_norm_b'], 'v_b_w_s': out['v_b_w_s'], 'v_b_bias': out['v_b_bias'], 'v_ab_w_out': out['v_ab_w_out'], 'v_pool_w': out['v_pool_w'], 'v_pool_scale': out['v_pool_scale'], 'v_ffn_w1': out['v_ffn_w1'], 'v_ffn_w3': out['v_ffn_w3'], 'v_ffn_w2': out['v_ffn_w2'], 'v_final_g': out['v_final_g']}


def _loss(weights, diff, rest, loss_target):
    with _jax.named_scope("forward"):
        args = {**rest, TWIN_DIFF_INPUT: diff, **{k: w.astype(_WEIGHT_DTYPES[k]) for k, w in weights.items()}}
        y = _forward(args)
    with _jax.named_scope("loss_head"):
        err = _jnp.square(y.astype(_jnp.float32) - loss_target)
        return 0.5 * _jnp.sum(_jnp.mean(err, axis=-1)) if err.ndim else 0.5 * err


def _adamw(w, g, m, v):
    m = ADAM_B1 * m + (1.0 - ADAM_B1) * g
    v = ADAM_B2 * v + (1.0 - ADAM_B2) * _jnp.square(g)
    m_hat = m / (1.0 - ADAM_B1 ** ADAM_STEP)
    v_hat = v / (1.0 - ADAM_B2 ** ADAM_STEP)
    delta = -ADAM_LR * (m_hat / (_jnp.sqrt(v_hat) + ADAM_EPS) + ADAM_WD * w)
    return delta, m, v


def reference(x, c, ada_w, ada_b, norm_mix_g, norm_ffn_g, ab_w_in, a_conv_w, a_conv_b, a_norm_g, a_norm_b, b_norm_g, b_norm_b, b_w_s, b_bias, ab_w_out, pool_w, pool_scale, ffn_w1, ffn_w3, ffn_w2, final_g, loss_target, m_ada_w, m_ada_b, m_norm_mix_g, m_norm_ffn_g, m_ab_w_in, m_a_conv_w, m_a_conv_b, m_a_norm_g, m_a_norm_b, m_b_norm_g, m_b_norm_b, m_b_w_s, m_b_bias, m_ab_w_out, m_pool_w, m_pool_scale, m_ffn_w1, m_ffn_w3, m_ffn_w2, m_final_g, v_ada_w, v_ada_b, v_norm_mix_g, v_norm_ffn_g, v_ab_w_in, v_a_conv_w, v_a_conv_b, v_a_norm_g, v_a_norm_b, v_b_norm_g, v_b_norm_b, v_b_w_s, v_b_bias, v_ab_w_out, v_pool_w, v_pool_scale, v_ffn_w1, v_ffn_w3, v_ffn_w2, v_final_g):
    given = dict(x=x, c=c, ada_w=ada_w, ada_b=ada_b, norm_mix_g=norm_mix_g, norm_ffn_g=norm_ffn_g, ab_w_in=ab_w_in, a_conv_w=a_conv_w, a_conv_b=a_conv_b, a_norm_g=a_norm_g, a_norm_b=a_norm_b, b_norm_g=b_norm_g, b_norm_b=b_norm_b, b_w_s=b_w_s, b_bias=b_bias, ab_w_out=ab_w_out, pool_w=pool_w, pool_scale=pool_scale, ffn_w1=ffn_w1, ffn_w3=ffn_w3, ffn_w2=ffn_w2, final_g=final_g, loss_target=loss_target, m_ada_w=m_ada_w, m_ada_b=m_ada_b, m_norm_mix_g=m_norm_mix_g, m_norm_ffn_g=m_norm_ffn_g, m_ab_w_in=m_ab_w_in, m_a_conv_w=m_a_conv_w, m_a_conv_b=m_a_conv_b, m_a_norm_g=m_a_norm_g, m_a_norm_b=m_a_norm_b, m_b_norm_g=m_b_norm_g, m_b_norm_b=m_b_norm_b, m_b_w_s=m_b_w_s, m_b_bias=m_b_bias, m_ab_w_out=m_ab_w_out, m_pool_w=m_pool_w, m_pool_scale=m_pool_scale, m_ffn_w1=m_ffn_w1, m_ffn_w3=m_ffn_w3, m_ffn_w2=m_ffn_w2, m_final_g=m_final_g, v_ada_w=v_ada_w, v_ada_b=v_ada_b, v_norm_mix_g=v_norm_mix_g, v_norm_ffn_g=v_norm_ffn_g, v_ab_w_in=v_ab_w_in, v_a_conv_w=v_a_conv_w, v_a_conv_b=v_a_conv_b, v_a_norm_g=v_a_norm_g, v_a_norm_b=v_a_norm_b, v_b_norm_g=v_b_norm_g, v_b_norm_b=v_b_norm_b, v_b_w_s=v_b_w_s, v_b_bias=v_b_bias, v_ab_w_out=v_ab_w_out, v_pool_w=v_pool_w, v_pool_scale=v_pool_scale, v_ffn_w1=v_ffn_w1, v_ffn_w3=v_ffn_w3, v_ffn_w2=v_ffn_w2, v_final_g=v_final_g)
    weights = {n: given[n] for n in TWIN_WEIGHTS}
    shared = {n: given[n] for n in SHARED_INPUTS}
    per_example = {n: given[n] for n in ['x', 'c']}
    grad_fn = _jax.value_and_grad(_loss, argnums=(0, 1))

    def one_microbatch(ex, loss_target):
        ex = dict(ex)
        diff = ex.pop(TWIN_DIFF_INPUT)
        return grad_fn(weights, diff, {**shared, **ex}, loss_target)

    if N_MICROBATCH == 1:
        loss, (grad_w, grad_x) = one_microbatch(per_example, given["loss_target"])
    else:
        def body(carry, xs):
            loss_sum, grad_sum = carry
            l_k, (gw_k, gx_k) = one_microbatch(xs[0], xs[1])
            with _jax.named_scope("update"):
                return (loss_sum + l_k, _jax.tree.map(_jnp.add, grad_sum, gw_k)), gx_k

        init = (_jnp.zeros((), _jnp.float32), _jax.tree.map(_jnp.zeros_like, weights))
        (loss, grad_w), grad_x = _jax.lax.scan(body, init, (per_example, given["loss_target"]))
    with _jax.named_scope("update"):
        delta_w, new_m, new_v = {}, {}, {}
        for n in TWIN_WEIGHTS:
            delta_w[n], new_m[n], new_v[n] = _adamw(weights[n], grad_w[n], given["m_" + n], given["v_" + n])
    return (loss, grad_x, *[grad_w[n] for n in TWIN_WEIGHTS], *[delta_w[n] for n in TWIN_WEIGHTS],
            *[new_m[n] for n in TWIN_WEIGHTS], *[new_v[n] for n in TWIN_WEIGHTS])
```

```python
import functools

import jax
import jax.numpy as jnp
from jax import lax
from jax.experimental import pallas as pl
from jax.experimental.pallas import tpu as pltpu

F32 = jnp.float32
BF16 = jnp.bfloat16
EPS = 1e-6
N_DEV = 8
N_CHIP = 4
MESH_AXES = ("x", "y", "c")
MESH = pl.DeviceIdType.MESH
V7X_VMEM_LIMIT_BYTES = 56 * 1024 * 1024
CONV_HALO = 32
POOL_HALO = 16
POOL_WINDOWS = (2, 4, 8, 16)
CHUNK = 128
ADAM_LR, ADAM_B1, ADAM_B2, ADAM_EPS, ADAM_WD, ADAM_STEP = 0.001, 0.9, 0.999, 1e-08, 0.01, 10
HBM = pl.BlockSpec(memory_space=pltpu.HBM)


def _params(sem=None):
    return pltpu.CompilerParams(dimension_semantics=sem, vmem_limit_bytes=V7X_VMEM_LIMIT_BYTES)


def _pick(n, prefs):
    for p in prefs:
        if p <= n and n % p == 0:
            return p
    return n


def _row_tile(rows, cols, target):
    if rows * cols <= target:
        return rows
    best = None
    for d in range(16, rows, 16):
        if rows % d == 0 and d * cols <= target:
            best = d
    return best if best is not None else rows


def _sds(shape, dtype):
    return jax.ShapeDtypeStruct(tuple(shape), dtype)


def _mm_core(name, grid, nk, a_list, b_list, a_spec, b_spec, dn, extras, extra_specs, out_shapes, out_specs,
             acc_shape, epilogue):
    n, n_ex, n_out = len(a_list), len(extras), len(out_shapes)

    def body(*refs):
        a_refs, b_refs = refs[:n], refs[n:2 * n]
        ex = refs[2 * n:2 * n + n_ex]
        outs = refs[2 * n + n_ex:2 * n + n_ex + n_out]
        accs = refs[2 * n + n_ex + n_out:]
        ps = [lax.dot_general(a[...], b[...], dn, preferred_element_type=F32) for a, b in zip(a_refs, b_refs)]
        if nk == 1:
            epilogue(ps, ex, outs)
            return
        k = pl.program_id(2)

        @pl.when(k == 0)
        def _():
            for acc, p in zip(accs, ps):
                acc[...] = p

        @pl.when(k > 0)
        def _():
            for acc, p in zip(accs, ps):
                acc[...] += p

        @pl.when(k == nk - 1)
        def _():
            epilogue([acc[...] for acc in accs], ex, outs)

    scratch = [] if nk == 1 else [pltpu.VMEM(acc_shape, F32) for _ in range(n)]
    return pl.pallas_call(
        body, grid=grid, name=name,
        in_specs=[a_spec] * n + [b_spec] * n + list(extra_specs),
        out_specs=list(out_specs), out_shape=list(out_shapes), scratch_shapes=scratch,
        compiler_params=_params(("parallel", "parallel", "arbitrary")),
    )(*a_list, *b_list, *extras)


def _mm(name, a_list, b_list, out_dtypes, epilogue, *, trans_a=False, trans_b=False, extras=(), extra_kinds=(),
        tm=1024, tn=1024, tk=1024):
    a0, b0 = a_list[0], b_list[0]
    m, kk = (a0.shape[1], a0.shape[0]) if trans_a else a0.shape
    nn = b0.shape[0] if trans_b else b0.shape[1]
    tm, tn, tk = _pick(m, (tm, 512, 256, 128)), _pick(nn, (tn, 512, 256, 128)), _pick(kk, (tk, 512, 256, 128))
    nk = kk // tk
    a_spec = pl.BlockSpec((tk, tm), lambda i, j, k: (k, i)) if trans_a else pl.BlockSpec((tm, tk), lambda i, j, k: (i, k))
    b_spec = pl.BlockSpec((tn, tk), lambda i, j, k: (j, k)) if trans_b else pl.BlockSpec((tk, tn), lambda i, j, k: (k, j))
    dn = (((0 if trans_a else 1,), (1 if trans_b else 0,)), ((), ()))
    tile = pl.BlockSpec((tm, tn), lambda i, j, k: (i, j))
    row = pl.BlockSpec((1, tn), lambda i, j, k: (0, j))
    return _mm_core(name, (m // tm, nn // tn, nk), nk, a_list, b_list, a_spec, b_spec, dn, extras,
                    [tile if kd == "tile" else row for kd in extra_kinds],
                    [_sds((m, nn), dt) for dt in out_dtypes], [tile] * len(out_dtypes), (tm, tn), epilogue)


def _ep_store(ps, ex, outs):
    outs[0][...] = ps[0].astype(outs[0].dtype)


def _ep_sum(ps, ex, outs):
    outs[0][...] = (ps[0] + ps[1]).astype(outs[0].dtype)


def _ep_swiglu(ps, ex, outs):
    u, t = ps
    outs[0][...] = u.astype(BF16)
    outs[1][...] = t.astype(BF16)
    outs[2][...] = (u * jax.nn.sigmoid(u) * t).astype(BF16)


def _ep_residual(ps, ex, outs):
    x_ref, gate_ref = ex
    y = ps[0]
    outs[0][...] = x_ref[...] + gate_ref[...] * y
    outs[1][...] = y.astype(BF16)


def _ep_swiglu_bwd(ps, ex, outs):
    u = ex[0][...].astype(F32)
    t = ex[1][...].astype(F32)
    dz = ps[0]
    sg = jax.nn.sigmoid(u)
    outs[0][...] = (dz * t * (sg * (1.0 + u * (1.0 - sg)))).astype(BF16)
    outs[1][...] = (dz * (u * sg)).astype(BF16)


def _grouped_fwd(p, w, x, gate, tm):
    s, d = p.shape
    g, kg, ng = w.shape
    tm = _pick(s, (tm, 512, 256, 128))
    tile_a = pl.BlockSpec((tm, kg), lambda i, j, k: (i, j))
    tile_o = pl.BlockSpec((tm, ng), lambda i, j, k: (i, j))
    return _mm_core("pool_mm_fwd", (s // tm, g, 1), 1, [p], [w], tile_a,
                    pl.BlockSpec((None, kg, ng), lambda i, j, k: (j, 0, 0)), (((1,), (0,)), ((), ())),
                    [x, gate], [tile_o, pl.BlockSpec((1, ng), lambda i, j, k: (0, j))],
                    [_sds((s, g * ng), F32), _sds((s, g * ng), BF16)], [tile_o, tile_o], None, _ep_residual)


def _grouped_dx(dy, w, tm):
    s, _ = dy.shape
    g, kg, ng = w.shape
    tm = _pick(s, (tm, 512, 256, 128))
    return _mm_core("pool_mm_dx", (s // tm, g, 1), 1, [dy], [w], pl.BlockSpec((tm, ng), lambda i, j, k: (i, j)),
                    pl.BlockSpec((None, kg, ng), lambda i, j, k: (j, 0, 0)), (((1,), (1,)), ((), ())), [], [],
                    [_sds((s, g * kg), F32)], [pl.BlockSpec((tm, kg), lambda i, j, k: (i, j))], None, _ep_store)[0]


def _grouped_dw(p, dy, groups, tk):
    s, d = p.shape
    kg = d // groups
    ng = dy.shape[1] // groups
    tk = _pick(s, (tk, 512, 256, 128))
    nk = s // tk

    def ep(ps, ex, outs):
        outs[0][...] = ps[0].astype(BF16)

    return _mm_core("pool_mm_dw", (groups, 1, nk), nk, [p], [dy], pl.BlockSpec((tk, kg), lambda i, j, k: (k, i)),
                    pl.BlockSpec((tk, ng), lambda i, j, k: (k, i)), (((0,), (0,)), ((), ())), [], [],
                    [_sds((groups, kg, ng), BF16)], [pl.BlockSpec((None, kg, ng), lambda i, j, k: (i, 0, 0))],
                    (kg, ng), ep)[0]


def _rms_rstd(xv):
    return lax.rsqrt(jnp.mean(xv * xv, axis=-1, keepdims=True) + EPS)


def _norm_mod_math(xv, g, sh, sc):
    return ((xv * _rms_rstd(xv)) * g) * (1.0 + sc) + sh


def _norm_mod_bwd_math(dh, xv, dxo, g, sc, acc_ref):
    r = _rms_rstd(xv)
    xhat = xv * r
    acc_ref[0:1, :] += jnp.sum(dh, axis=0, keepdims=True)
    acc_ref[1:2, :] += jnp.sum(dh * (xhat * g), axis=0, keepdims=True)
    dhn = dh * (1.0 + sc)
    acc_ref[2:3, :] += jnp.sum(dhn * xhat, axis=0, keepdims=True)
    dxh = dhn * g
    return dxo + r * (dxh - xhat * jnp.mean(dxh * xhat, axis=-1, keepdims=True))


def _vec_spec(d):
    return pl.BlockSpec((1, d), lambda i: (0, 0))


def _acc_spec(d):
    return pl.BlockSpec((8, d), lambda i: (0, 0))


def _norm_mod(x, g, sh, sc, tm):
    s, d = x.shape

    def body(x_ref, g_ref, sh_ref, sc_ref, h_ref):
        h_ref[...] = _norm_mod_math(x_ref[...], g_ref[...], sh_ref[...], sc_ref[...]).astype(BF16)

    row = pl.BlockSpec((tm, d), lambda i: (i, 0))
    return pl.pallas_call(body, grid=(s // tm,), name="norm_mod", in_specs=[row] + [_vec_spec(d)] * 3, out_specs=row,
                          out_shape=_sds((s, d), BF16), compiler_params=_params(("parallel",)))(x, g, sh, sc)


def _norm_mod_bwd(dh, x, dxo, g, sc, tm):
    s, d = x.shape

    def body(dh_ref, x_ref, dxo_ref, g_ref, sc_ref, dx_ref, acc_ref):
        @pl.when(pl.program_id(0) == 0)
        def _():
            acc_ref[...] = jnp.zeros_like(acc_ref)

        dx_ref[...] = _norm_mod_bwd_math(dh_ref[...], x_ref[...], dxo_ref[...], g_ref[...], sc_ref[...], acc_ref)

    row = pl.BlockSpec((tm, d), lambda i: (i, 0))
    return pl.pallas_call(body, grid=(s // tm,), name="norm_mod_bwd", in_specs=[row, row, row, _vec_spec(d), _vec_spec(d)],
                          out_specs=[row, _acc_spec(d)], out_shape=[_sds((s, d), F32), _sds((8, d), F32)],
                          compiler_params=_params(("arbitrary",)))(dh, x, dxo, g, sc)


def _gate_bwd(dx, y, gate, tm):
    s, d = dx.shape

    def body(dx_ref, y_ref, gate_ref, dy_ref, acc_ref):
        @pl.when(pl.program_id(0) == 0)
        def _():
            acc_ref[...] = jnp.zeros_like(acc_ref)

        dxv = dx_ref[...]
        dy_ref[...] = (dxv * gate_ref[...]).astype(BF16)
        acc_ref[0:1, :] += jnp.sum(dxv * y_ref[...].astype(F32), axis=0, keepdims=True)

    row = pl.BlockSpec((tm, d), lambda i: (i, 0))
    return pl.pallas_call(body, grid=(s // tm,), name="gate_bwd", in_specs=[row, row, _vec_spec(d)],
                          out_specs=[row, _acc_spec(d)], out_shape=[_sds((s, d), BF16), _sds((8, d), F32)],
                          compiler_params=_params(("arbitrary",)))(dx, y, gate)


def _final_loss_bwd(x, g, target, tm):
    s, d = x.shape

    def body(x_ref, g_ref, t_ref, dx_ref, acc_ref, loss_ref):
        @pl.when(pl.program_id(0) == 0)
        def _():
            acc_ref[...] = jnp.zeros_like(acc_ref)
            loss_ref[...] = jnp.zeros_like(loss_ref)

        xv = x_ref[...]
        gv = g_ref[...]
        r = _rms_rstd(xv)
        xhat = xv * r
        err = xhat * gv - t_ref[...]
        loss_ref[...] += (0.5 / d) * jnp.sum(err * err)
        dy = err * (1.0 / d)
        acc_ref[0:1, :] += jnp.sum(dy * xhat, axis=0, keepdims=True)
        dxh = dy * gv
        dx_ref[...] = r * (dxh - xhat * jnp.mean(dxh * xhat, axis=-1, keepdims=True))

    row = pl.BlockSpec((tm, d), lambda i: (i, 0))
    return pl.pallas_call(body, grid=(s // tm,), name="final_loss_bwd", in_specs=[row, _vec_spec(d), row],
                          out_specs=[row, _acc_spec(d), pl.BlockSpec((8, 128), lambda i: (0, 0))],
                          out_shape=[_sds((s, d), F32), _sds((8, d), F32), _sds((8, 128), F32)],
                          compiler_params=_params(("arbitrary",)))(x, g, target)


def _chunks(tm, width, rb, cb):
    rb, cb = min(rb, tm), min(cb, width)
    return [(r0, c0, rb, cb) for r0 in range(0, tm, rb) for c0 in range(0, width, cb)]


def _prev_halo_map(tm, halo):
    return lambda i: (jnp.maximum(i * (tm // halo) - 1, 0), 0)


def _next_halo_map(tm, halo, s):
    return lambda i: (jnp.minimum((i + 1) * (tm // halo), s // halo - 1), 0)


def _pool_fwd(x, g, sh, sc, tm):
    s, d = x.shape
    dg = d // len(POOL_WINDOWS)

    def body(x_ref, xh_ref, g_ref, sh_ref, sc_ref, p_ref, ext_ref):
        i = pl.program_id(0)
        gv, shv, scv = g_ref[...], sh_ref[...], sc_ref[...]
        ext_ref[POOL_HALO:, :] = _norm_mod_math(x_ref[...], gv, shv, scv)
        ext_ref[0:POOL_HALO, :] = jnp.where(i == 0, 0.0, _norm_mod_math(xh_ref[...], gv, shv, scv))
        for gi, w in enumerate(POOL_WINDOWS):
            for r0, c0, rb, cb in _chunks(tm, dg, 64, 256):
                cols = pl.ds(gi * dg + c0, cb)
                tok = ext_ref[pl.ds(POOL_HALO + r0, rb), cols]
                acc = tok
                for j in range(1, w):
                    acc = acc + ext_ref[pl.ds(POOL_HALO + r0 - j, rb), cols]
                t_glob = i * tm + r0 + lax.broadcasted_iota(jnp.int32, (rb, 1), 0)
                cnt = jnp.minimum(t_glob + 1, w).astype(F32)
                p_ref[pl.ds(r0, rb), cols] = (acc / cnt - tok).astype(BF16)

    row = pl.BlockSpec((tm, d), lambda i: (i, 0))
    halo = pl.BlockSpec((POOL_HALO, d), _prev_halo_map(tm, POOL_HALO))
    return pl.pallas_call(body, grid=(s // tm,), name="pool_fwd", in_specs=[row, halo] + [_vec_spec(d)] * 3, out_specs=row,
                          out_shape=_sds((s, d), BF16), scratch_shapes=[pltpu.VMEM((tm + POOL_HALO, d), F32)],
                          compiler_params=_params(("parallel",)))(x, x, g, sh, sc)


def _pool_bwd(dp, x, dxo, g, sc, tm):
    s, d = x.shape
    dg = d // len(POOL_WINDOWS)
    n_tiles = s // tm

    def body(dp_ref, dph_ref, x_ref, dxo_ref, g_ref, sc_ref, dx_ref, acc_ref, ext_ref, dh_ref):
        i = pl.program_id(0)

        @pl.when(i == 0)
        def _():
            acc_ref[...] = jnp.zeros_like(acc_ref)

        for gi, w in enumerate(POOL_WINDOWS):
            cols = pl.ds(gi * dg, dg)
            t_main = i * tm + lax.broadcasted_iota(jnp.int32, (tm, 1), 0)
            ext_ref[0:tm, cols] = dp_ref[:, cols] / jnp.minimum(t_main + 1, w).astype(F32)
            ext_ref[tm:, cols] = jnp.where(i == n_tiles - 1, 0.0, dph_ref[:, cols] * (1.0 / w))
            for r0, c0, rb, cb in _chunks(tm, dg, 64, 256):
                cc = pl.ds(gi * dg + c0, cb)
                acc = ext_ref[pl.ds(r0, rb), cc]
                for j in range(1, w):
                    acc = acc + ext_ref[pl.ds(r0 + j, rb), cc]
                dh_ref[pl.ds(r0, rb), cc] = acc - dp_ref[pl.ds(r0, rb), cc]
        dx_ref[...] = _norm_mod_bwd_math(dh_ref[...], x_ref[...], dxo_ref[...], g_ref[...], sc_ref[...], acc_ref)

    row = pl.BlockSpec((tm, d), lambda i: (i, 0))
    halo = pl.BlockSpec((POOL_HALO, d), _next_halo_map(tm, POOL_HALO, s))
    return pl.pallas_call(body, grid=(n_tiles,), name="pool_bwd", in_specs=[row, halo, row, row, _vec_spec(d), _vec_spec(d)],
                          out_specs=[row, _acc_spec(d)], out_shape=[_sds((s, d), F32), _sds((8, d), F32)],
                          scratch_shapes=[pltpu.VMEM((tm + POOL_HALO, d), F32), pltpu.VMEM((tm, d), F32)],
                          compiler_params=_params(("arbitrary",)))(dp, dp, x, dxo, g, sc)


def _layernorm_fwd(v, g, b):
    mu = jnp.mean(v, axis=-1, keepdims=True)
    xc = v - mu
    rstd = lax.rsqrt(jnp.mean(xc * xc, axis=-1, keepdims=True) + EPS)
    yn = xc * rstd
    return yn * g + b, yn, rstd


def _layernorm_bwd(dz, yn, rstd, g):
    dyn = dz * g
    dv = rstd * (dyn - jnp.mean(dyn, axis=-1, keepdims=True) - yn * jnp.mean(dyn * yn, axis=-1, keepdims=True))
    return dv, jnp.sum(dz * yn, axis=0, keepdims=True), jnp.sum(dz, axis=0, keepdims=True)


def _ab_mid_fwd(proj, conv_w, conv_b, a_g, a_b, v_g, v_b, w_c, bias_full, tm):
    s = proj.shape[0]
    da = conv_b.shape[1]
    db = v_g.shape[1]
    nh = w_c.shape[0]
    kw = conv_w.shape[0]
    lead = CONV_HALO - (kw - 1)

    def body(p_ref, ph_ref, cw_ref, cb_ref, ag_ref, ab_ref, vg_ref, vb_ref, wc_ref, bias_ref, cat_ref, a1_ref, ext_ref,
             a1s_ref):
        i = pl.program_id(0)
        val = p_ref[:, 0:da].astype(F32)
        gat = p_ref[:, da:2 * da].astype(F32)
        ext_ref[CONV_HALO:, :] = val * jax.nn.sigmoid(gat)
        hv = ph_ref[:, 0:da].astype(F32)
        hg = ph_ref[:, da:2 * da].astype(F32)
        ext_ref[0:CONV_HALO, :] = jnp.where(i == 0, 0.0, hv * jax.nn.sigmoid(hg))
        for r0, c0, rb, cb in _chunks(tm, da, 64, 256):
            cols = pl.ds(c0, cb)
            acc = jnp.broadcast_to(cb_ref[:, cols], (rb, cb))
            for k in range(kw):
                acc = acc + cw_ref[k:k + 1, cols] * ext_ref[pl.ds(r0 + lead + k, rb), cols]
            a1s_ref[pl.ds(r0, rb), cols] = acc
        a1 = a1s_ref[...]
        a1_ref[...] = a1.astype(BF16)
        z, _, _ = _layernorm_fwd(a1, ag_ref[...], ab_ref[...])
        cat_ref[:, 0:da] = (z * jax.nn.sigmoid(z)).astype(BF16)

        bu = p_ref[:, 2 * da:2 * da + db].astype(F32)
        bv = p_ref[:, 2 * da + db:].astype(F32)
        vn, _, _ = _layernorm_fwd(bv, vg_ref[...], vb_ref[...])
        vnb = vn.astype(BF16)
        for n in range(tm // CHUNK):
            rows = slice(n * CHUNK, (n + 1) * CHUNK)
            for h in range(nh):
                hc = slice(h * CHUNK, (h + 1) * CHUNK)
                vo = jnp.dot(wc_ref[h], vnb[rows, hc], preferred_element_type=F32) + bias_ref[:, hc]
                cat_ref[rows, da + h * CHUNK:da + (h + 1) * CHUNK] = (bu[rows, hc] * vo).astype(BF16)

    full = lambda a: pl.BlockSpec(a.shape, lambda i: (0,) * a.ndim)
    return pl.pallas_call(
        body, grid=(s // tm,), name="ab_mid_fwd",
        in_specs=[pl.BlockSpec((tm, 2 * da + 2 * db), lambda i: (i, 0)),
                  pl.BlockSpec((CONV_HALO, 2 * da), _prev_halo_map(tm, CONV_HALO)),
                  full(conv_w), full(conv_b), full(a_g), full(a_b), full(v_g), full(v_b), full(w_c), full(bias_full)],
        out_specs=[pl.BlockSpec((tm, da + db), lambda i: (i, 0)), pl.BlockSpec((tm, da), lambda i: (i, 0))],
        out_shape=[_sds((s, da + db), BF16), _sds((s, da), BF16)],
        scratch_shapes=[pltpu.VMEM((tm + CONV_HALO, da), F32), pltpu.VMEM((tm, da), F32)],
        compiler_params=_params(("parallel",)),
    )(proj, proj, conv_w, conv_b, a_g, a_b, v_g, v_b, w_c, bias_full)


def _ab_mid_bwd(dcat, proj, a1, conv_w, a_g, a_b, v_g, v_b, w_c, bias_full, tm):
    s = proj.shape[0]
    da = a_g.shape[1]
    db = v_g.shape[1]
    nh = w_c.shape[0]
    kw = conv_w.shape[0]
    lead = CONV_HALO - (kw - 1)
    n_tiles = s // tm

    def body(dc_ref, dch_ref, p_ref, ph_ref, a1_ref, a1h_ref, cw_ref, ag_ref, ab_ref, vg_ref, vb_ref, wc_ref, bias_ref,
             dp_ref, dcw_ref, vec_ref, dws_ref, dbias_ref, ext_ref, dext_ref, dcw_acc, dvn_ref):
        i = pl.program_id(0)

        @pl.when(i == 0)
        def _():
            dcw_acc[...] = jnp.zeros_like(dcw_acc)
            vec_ref[...] = jnp.zeros_like(vec_ref)
            dws_ref[...] = jnp.zeros_like(dws_ref)
            dbias_ref[...] = jnp.zeros_like(dbias_ref)

        agv, abv = ag_ref[...], ab_ref[...]

        def silu_ln_bwd(a1v, d_a2):
            z, yn, rstd = _layernorm_fwd(a1v, agv, abv)
            sg = jax.nn.sigmoid(z)
            return _layernorm_bwd(d_a2 * (sg * (1.0 + z * (1.0 - sg))), yn, rstd, agv)

        d_a1, dga, dba = silu_ln_bwd(a1_ref[...].astype(F32), dc_ref[:, 0:da])
        dext_ref[0:tm, :] = d_a1
        d_a1h, _, _ = silu_ln_bwd(a1h_ref[...].astype(F32), dch_ref[...])
        dext_ref[tm:, :] = jnp.where(i == n_tiles - 1, 0.0, d_a1h)
        vec_ref[0:1, 0:da] += dga
        vec_ref[1:2, 0:da] += dba
        vec_ref[2:3, 0:da] += jnp.sum(d_a1, axis=0, keepdims=True)

        val = p_ref[:, 0:da].astype(F32)
        sgg = jax.nn.sigmoid(p_ref[:, da:2 * da].astype(F32))
        ext_ref[CONV_HALO:, :] = val * sgg
        hv = ph_ref[:, 0:da].astype(F32)
        hg = ph_ref[:, da:2 * da].astype(F32)
        ext_ref[0:CONV_HALO, :] = jnp.where(i == 0, 0.0, hv * jax.nn.sigmoid(hg))

        for r0, c0, rb, cb in _chunks(tm, da, 64, 256):
            cols = pl.ds(c0, cb)
            rows = pl.ds(r0, rb)
            d1 = dext_ref[rows, cols]
            acc = jnp.zeros((rb, cb), F32)
            for k in range(kw):
                acc = acc + cw_ref[k:k + 1, cols] * dext_ref[pl.ds(r0 + (kw - 1) - k, rb), cols]
                prod = d1 * ext_ref[pl.ds(r0 + lead + k, rb), cols]
                dcw_acc[k, :, cols] += jnp.sum(prod.reshape(rb // 8, 8, cb), axis=0)
            v = p_ref[rows, pl.ds(c0, cb)].astype(F32)
            sg = jax.nn.sigmoid(p_ref[rows, pl.ds(da + c0, cb)].astype(F32))
            dp_ref[rows, pl.ds(c0, cb)] = (acc * sg).astype(BF16)
            dp_ref[rows, pl.ds(da + c0, cb)] = (acc * v * sg * (1.0 - sg)).astype(BF16)

        vgv = vg_ref[...]
        bu = p_ref[:, 2 * da:2 * da + db].astype(F32)
        bv = p_ref[:, 2 * da + db:].astype(F32)
        vn, yn_v, rstd_v = _layernorm_fwd(bv, vgv, vb_ref[...])
        vnb = vn.astype(BF16)
        for n in range(tm // CHUNK):
            rows = slice(n * CHUNK, (n + 1) * CHUNK)
            for h in range(nh):
                hc = slice(h * CHUNK, (h + 1) * CHUNK)
                wch = wc_ref[h]
                blk = vnb[rows, hc]
                vo = jnp.dot(wch, blk, preferred_element_type=F32) + bias_ref[:, hc]
                d_bout = dc_ref[rows, da + h * CHUNK:da + (h + 1) * CHUNK]
                dp_ref[rows, 2 * da + h * CHUNK:2 * da + (h + 1) * CHUNK] = (d_bout * vo).astype(BF16)
                d_vo = d_bout * bu[rows, hc]
                dbias_ref[h] += jnp.sum(d_vo, axis=1, keepdims=True)
                d_vob = d_vo.astype(BF16)
                dws_ref[h] += lax.dot_general(d_vob, blk, (((1,), (1,)), ((), ())), preferred_element_type=F32)
                dvn_ref[rows, hc] = lax.dot_general(wch, d_vob, (((0,), (0,)), ((), ())), preferred_element_type=F32)
        d_bv, dgv, dbv = _layernorm_bwd(dvn_ref[...], yn_v, rstd_v, vgv)
        dp_ref[:, 2 * da + db:] = d_bv.astype(BF16)
        vec_ref[3:4, 0:db] += dgv
        vec_ref[4:5, 0:db] += dbv

        @pl.when(i == n_tiles - 1)
        def _():
            dcw_ref[...] = jnp.sum(dcw_acc[...], axis=1)
            causal = lax.broadcasted_iota(jnp.int32, (CHUNK, CHUNK), 0) >= lax.broadcasted_iota(jnp.int32, (CHUNK, CHUNK), 1)
            for h in range(nh):
                dws_ref[h] = jnp.where(causal, dws_ref[h], 0.0)

    full = lambda a: pl.BlockSpec(a.shape, lambda i: (0,) * a.ndim)
    wide = max(da, db)
    return pl.pallas_call(
        body, grid=(n_tiles,), name="ab_mid_bwd",
        in_specs=[pl.BlockSpec((tm, da + db), lambda i: (i, 0)),
                  pl.BlockSpec((CONV_HALO, da), _next_halo_map(tm, CONV_HALO, s)),
                  pl.BlockSpec((tm, 2 * da + 2 * db), lambda i: (i, 0)),
                  pl.BlockSpec((CONV_HALO, 2 * da), _prev_halo_map(tm, CONV_HALO)),
                  pl.BlockSpec((tm, da), lambda i: (i, 0)),
                  pl.BlockSpec((CONV_HALO, da), _next_halo_map(tm, CONV_HALO, s)),
                  full(conv_w), full(a_g), full(a_b), full(v_g), full(v_b), full(w_c), full(bias_full)],
        out_specs=[pl.BlockSpec((tm, 2 * da + 2 * db), lambda i: (i, 0)),
                   pl.BlockSpec((kw, da), lambda i: (0, 0)),
                   pl.BlockSpec((8, wide), lambda i: (0, 0)),
                   pl.BlockSpec((nh, CHUNK, CHUNK), lambda i: (0, 0, 0)),
                   pl.BlockSpec((nh, CHUNK, 1), lambda i: (0, 0, 0))],
        out_shape=[_sds((s, 2 * da + 2 * db), BF16), _sds((kw, da), F32), _sds((8, wide), F32),
                   _sds((nh, CHUNK, CHUNK), F32), _sds((nh, CHUNK, 1), F32)],
        scratch_shapes=[pltpu.VMEM((tm + CONV_HALO, da), F32), pltpu.VMEM((tm + CONV_HALO, da), F32),
                        pltpu.VMEM((kw, 8, da), F32), pltpu.VMEM((tm, db), F32)],
        compiler_params=_params(("arbitrary",)),
    )(dcat, dcat, proj, proj, a1, a1, conv_w, a_g, a_b, v_g, v_b, w_c, bias_full)


def _ada_fwd(c_all, w, b):
    nl, d, n = w.shape
    tn = _pick(n, (512, 256, 128))

    def body(c_ref, w_ref, b_ref, o_ref):
        cv = c_ref[...]
        cond = (cv * jax.nn.sigmoid(cv)).astype(BF16)
        o_ref[...] = jnp.dot(cond, w_ref[...].astype(BF16), preferred_element_type=F32) + b_ref[...]

    return pl.pallas_call(
        body, grid=(nl, n // tn), name="ada_fwd",
        in_specs=[pl.BlockSpec(c_all.shape, lambda l, j: (0, 0)), pl.BlockSpec((None, d, tn), lambda l, j: (l, 0, j)),
                  pl.BlockSpec((None, 1, tn), lambda l, j: (l, 0, j))],
        out_specs=pl.BlockSpec((None, c_all.shape[0], tn), lambda l, j: (l, 0, j)),
        out_shape=_sds((nl, c_all.shape[0], n), F32), compiler_params=_params(("parallel", "parallel")),
    )(c_all, w, b)


def _ada_bwd(c_all, dmod):
    nl, nb, n = dmod.shape
    d = c_all.shape[1]
    tn = _pick(n, (512, 256, 128))

    def body(c_ref, g_ref, o_ref):
        cv = c_ref[...]
        cond = (cv * jax.nn.sigmoid(cv)).astype(BF16)
        o_ref[...] = lax.dot_general(cond, g_ref[...].astype(BF16), (((0,), (0,)), ((), ())), preferred_element_type=F32)

    return pl.pallas_call(
        body, grid=(nl, n // tn), name="ada_bwd",
        in_specs=[pl.BlockSpec(c_all.shape, lambda l, j: (0, 0)), pl.BlockSpec((None, nb, tn), lambda l, j: (l, 0, j))],
        out_specs=pl.BlockSpec((None, d, tn), lambda l, j: (l, 0, j)),
        out_shape=_sds((nl, d, n), F32), compiler_params=_params(("parallel", "parallel")),
    )(c_all, dmod)


def _sum_leading(a, out_dtype=F32, name="sum_leading"):
    n, r, c = a.shape
    tr = _row_tile(r, c, 256 * 1024)

    def body(a_ref, o_ref):
        acc = a_ref[0].astype(F32)
        for k in range(1, n):
            acc = acc + a_ref[k].astype(F32)
        o_ref[...] = acc.astype(out_dtype)

    return pl.pallas_call(body, grid=(r // tr,), name=name, in_specs=[pl.BlockSpec((n, tr, c), lambda i: (0, i, 0))],
                          out_specs=pl.BlockSpec((tr, c), lambda i: (i, 0)), out_shape=_sds((r, c), out_dtype),
                          compiler_params=_params(("parallel",)))(a)


def _pair_add(dw, sib, kind, c_vec):
    r, c = sib.shape
    tr = _row_tile(r, c, 512 * 1024)
    nb = r // tr
    if kind == "col":
        dw_spec = pl.BlockSpec((tr, c), lambda i, cv: (cv[0] * nb + i, 0))
    else:
        dw_spec = pl.BlockSpec((tr, c), lambda i, cv: (i, cv[0]))

    def body(cv_ref, dw_ref, sib_ref, o_ref):
        o_ref[...] = (dw_ref[...].astype(F32) + sib_ref[...].astype(F32)).astype(BF16)

    return pl.pallas_call(
        body, name="pair_add",
        grid_spec=pltpu.PrefetchScalarGridSpec(num_scalar_prefetch=1, grid=(nb,), in_specs=[dw_spec, pl.BlockSpec((tr, c), lambda i, cv: (i, 0))],
                                               out_specs=pl.BlockSpec((tr, c), lambda i, cv: (i, 0))),
        out_shape=_sds((r, c), BF16), compiler_params=_params(("parallel",)),
    )(c_vec, dw, sib)


def _adamw(w, g, m, v):
    shape = w.shape
    cols = shape[-1]
    rows = w.size // cols
    tr = _row_tile(rows, cols, 256 * 1024)
    bc1 = 1.0 - ADAM_B1 ** ADAM_STEP
    bc2 = 1.0 - ADAM_B2 ** ADAM_STEP

    def body(w_ref, g_ref, m_ref, v_ref, d_ref, mo_ref, vo_ref):
        gv = g_ref[...]
        mn = ADAM_B1 * m_ref[...] + (1.0 - ADAM_B1) * gv
        vn = ADAM_B2 * v_ref[...] + (1.0 - ADAM_B2) * (gv * gv)
        d_ref[...] = -ADAM_LR * ((mn / bc1) / (jnp.sqrt(vn / bc2) + ADAM_EPS) + ADAM_WD * w_ref[...])
        mo_ref[...] = mn
        vo_ref[...] = vn

    spec = pl.BlockSpec((tr, cols), lambda i: (i, 0))
    outs = pl.pallas_call(body, grid=(rows // tr,), name="adamw", in_specs=[spec] * 4, out_specs=[spec] * 3,
                          out_shape=[_sds((rows, cols), F32)] * 3, compiler_params=_params(("parallel",)))(
        *[a.reshape(rows, cols) for a in (w, g, m, v)])
    return [o.reshape(shape) for o in outs]


def _mesh_pos():
    return lax.axis_index("x"), lax.axis_index("y"), lax.axis_index("c")


def _other_chips(x, y):
    return [(1 - x, y), (x, 1 - y), (1 - x, 1 - y)]


def _allgather_small(a):
    r, c = a.shape

    def body(x_ref, out_ref, send_sems, recv_sems, local_sem):
        x, y, cc = _mesh_pos()
        me, sibling = (x, y, cc), (x, y, 1 - cc)
        chips = _other_chips(x, y)

        def slab(px, py, pc):
            return out_ref.at[4 * px + 2 * py + pc]

        def copy(k, block, to, src=None):
            return pltpu.make_async_remote_copy(src_ref=slab(*block) if src is None else src, dst_ref=slab(*block),
                                                send_sem=send_sems.at[k], recv_sem=recv_sems.at[k], device_id=to,
                                                device_id_type=MESH)

        mine = pltpu.make_async_copy(x_ref, slab(*me), local_sem)
        mine.start()
        first = [copy(0, me, sibling, src=x_ref)]
        first += [copy(1 + j, me, (*chip, cc), src=x_ref) for j, chip in enumerate(chips)]
        for cp in first:
            cp.start()
        passed = [copy(4 + j, (*chip, cc), sibling) for j, chip in enumerate(chips)]
        for j, chip in enumerate(chips):
            copy(1 + j, (*chip, cc), me).wait_recv()
            passed[j].start()
        copy(0, sibling, me).wait_recv()
        for j, chip in enumerate(chips):
            copy(4 + j, (*chip, 1 - cc), me).wait_recv()
        for cp in first + passed:
            cp.wait_send()
        mine.wait()

    return pl.pallas_call(
        body, name="allgather_small", out_shape=_sds((N_DEV, r, c), F32),
        in_specs=[pl.BlockSpec(memory_space=pltpu.VMEM)], out_specs=pl.BlockSpec(memory_space=pltpu.VMEM),
        scratch_shapes=[pltpu.SemaphoreType.DMA((7,)), pltpu.SemaphoreType.DMA((7,)), pltpu.SemaphoreType.DMA],
        compiler_params=pltpu.CompilerParams(vmem_limit_bytes=V7X_VMEM_LIMIT_BYTES),
    )(a)


def _idx(ref, rows=None, cols=None):
    lead = (slice(None),) * (len(ref.shape) - 2)
    return ref.at[lead + (slice(None) if rows is None else rows, slice(None) if cols is None else cols)]


def _half(ref, kind, c):
    r, cdim = ref.shape[-2:]
    if kind == "col":
        return _idx(ref, rows=pl.ds(c * (r // 2), r // 2))
    return _idx(ref, cols=pl.ds(c * (cdim // 2), cdim // 2))


def _shard_region(full, kind, shard_shape, s):
    r, cdim = shard_shape[-2:]
    if kind == "col":
        return _idx(full, cols=pl.ds(s * cdim, cdim))
    return _idx(full, rows=pl.ds(s * r, r))


def _allgather_weights(shards, kinds):
    nt = len(shards)
    full_shapes = []
    for sh, kind in zip(shards, kinds):
        shp = list(sh.shape)
        shp[-1 if kind == "col" else -2] *= N_CHIP
        full_shapes.append(tuple(shp))

    def body(*refs):
        sh, fu = refs[:nt], refs[nt:2 * nt]
        send_sems, recv_sems, local_sems = refs[2 * nt:]
        x, y, c = _mesh_pos()
        chips = _other_chips(x, y)

        def part(t, chip, cc):
            return _half(_shard_region(fu[t], kinds[t], sh[t].shape, 2 * chip[0] + chip[1]), kinds[t], cc)

        def copy(t, k, src, dst, to):
            return pltpu.make_async_remote_copy(src_ref=src, dst_ref=dst, send_sem=send_sems.at[6 * t + k],
                                                recv_sem=recv_sems.at[6 * t + k], device_id=to, device_id_type=MESH)

        mines, first, passed = [], [], []
        for t in range(nt):
            mine = pltpu.make_async_copy(sh[t], _shard_region(fu[t], kinds[t], sh[t].shape, 2 * x + y), local_sems.at[t])
            mine.start()
            mines.append(mine)
            for j, chip in enumerate(chips):
                cp = copy(t, j, _half(sh[t], kinds[t], c), part(t, (x, y), c), (*chip, c))
                cp.start()
                first.append(cp)
        for t in range(nt):
            for j, chip in enumerate(chips):
                copy(t, j, part(t, chip, c), part(t, chip, c), (x, y, c)).wait_recv()
                fw = copy(t, 3 + j, part(t, chip, c), part(t, chip, c), (x, y, 1 - c))
                fw.start()
                passed.append(fw)
        for t in range(nt):
            for j, chip in enumerate(chips):
                copy(t, 3 + j, part(t, chip, 1 - c), part(t, chip, 1 - c), (x, y, c)).wait_recv()
        for cp in first + passed:
            cp.wait_send()
        for mine in mines:
            mine.wait()

    return pl.pallas_call(
        body, name="allgather_weights", out_shape=[_sds(shp, BF16) for shp in full_shapes],
        in_specs=[HBM] * nt, out_specs=[HBM] * nt,
        scratch_shapes=[pltpu.SemaphoreType.DMA((6 * nt,)), pltpu.SemaphoreType.DMA((6 * nt,)),
                        pltpu.SemaphoreType.DMA((nt,))],
    )(*shards)


def _pair_exchange(dws, kinds):
    nt = len(dws)
    out_shapes = []
    for dw, kind in zip(dws, kinds):
        shp = list(dw.shape)
        shp[-2 if kind == "col" else -1] //= 2
        out_shapes.append(tuple(shp))

    def body(*refs):
        src, dst = refs[:nt], refs[nt:2 * nt]
        send_sems, recv_sems = refs[2 * nt:]
        x, y, c = _mesh_pos()
        copies = [pltpu.make_async_remote_copy(src_ref=_half(src[t], kinds[t], 1 - c), dst_ref=dst[t], send_sem=send_sems.at[t],
                                               recv_sem=recv_sems.at[t], device_id=(x, y, 1 - c), device_id_type=MESH)
                  for t in range(nt)]
        for cp in copies:
            cp.start()
        for cp in copies:
            cp.wait()

    return pl.pallas_call(
        body, name="grad_pair_exchange", out_shape=[_sds(shp, BF16) for shp in out_shapes], in_specs=[HBM] * nt,
        out_specs=[HBM] * nt, scratch_shapes=[pltpu.SemaphoreType.DMA((nt,)), pltpu.SemaphoreType.DMA((nt,))],
    )(*dws)


def _chip_block(ref, kind, s):
    r, cdim = ref.shape[-2:]
    if kind == "col":
        return _idx(ref, cols=pl.ds(s * (cdim // N_CHIP), cdim // N_CHIP))
    return _idx(ref, rows=pl.ds(s * (r // N_CHIP), r // N_CHIP))


def _chip_exchange(cps, kinds):
    nt = len(cps)
    out_shapes = []
    for cp, kind in zip(cps, kinds):
        shp = list(cp.shape)
        shp[-1 if kind == "col" else -2] //= N_CHIP
        out_shapes.append((N_CHIP,) + tuple(shp))

    def body(*refs):
        src, dst = refs[:nt], refs[nt:2 * nt]
        send_sems, recv_sems, local_sems = refs[2 * nt:]
        x, y, c = _mesh_pos()
        s = 2 * x + y
        chips = _other_chips(x, y)
        sends, locals_ = [], []
        for t in range(nt):
            own = pltpu.make_async_copy(_chip_block(src[t], kinds[t], s), dst[t].at[s], local_sems.at[t])
            own.start()
            locals_.append(own)
            for j, chip in enumerate(chips):
                cp = pltpu.make_async_remote_copy(src_ref=_chip_block(src[t], kinds[t], 2 * chip[0] + chip[1]), dst_ref=dst[t].at[s],
                                                  send_sem=send_sems.at[3 * t + j], recv_sem=recv_sems.at[3 * t + j],
                                                  device_id=(*chip, c), device_id_type=MESH)
                cp.start()
                sends.append(cp)
        for t in range(nt):
            for j, chip in enumerate(chips):
                landing = dst[t].at[2 * chip[0] + chip[1]]
                pltpu.make_async_remote_copy(src_ref=landing, dst_ref=landing, send_sem=send_sems.at[3 * t + j],
                                             recv_sem=recv_sems.at[3 * t + j], device_id=(x, y, c), device_id_type=MESH).wait_recv()
        for cp in sends:
            cp.wait_send()
        for own in locals_:
            own.wait()

    return pl.pallas_call(
        body, name="grad_chip_exchange", out_shape=[_sds(shp, BF16) for shp in out_shapes], in_specs=[HBM] * nt,
        out_specs=[HBM] * nt,
        scratch_shapes=[pltpu.SemaphoreType.DMA((3 * nt,)), pltpu.SemaphoreType.DMA((3 * nt,)), pltpu.SemaphoreType.DMA((nt,))],
    )(*cps)


def _pair_assemble(halves, kinds, groups, out_shapes):
    nt = len(halves)
    n_out = len(out_shapes)

    def body(*refs):
        src, dst = refs[:nt], refs[nt:nt + n_out]
        send_sems, recv_sems, local_sems = refs[nt + n_out:]
        x, y, c = _mesh_pos()
        copies, locals_ = [], []
        for t in range(nt):
            o, l = groups[t]
            own = pltpu.make_async_copy(src[t], _half(dst[o].at[l], kinds[t], c), local_sems.at[t])
            own.start()
            locals_.append(own)
            cp = pltpu.make_async_remote_copy(src_ref=src[t], dst_ref=_half(dst[o].at[l], kinds[t], c), send_sem=send_sems.at[t],
                                              recv_sem=recv_sems.at[t], device_id=(x, y, 1 - c), device_id_type=MESH)
            cp.start()
            copies.append(cp)
        for t in range(nt):
            o, l = groups[t]
            landing = _half(dst[o].at[l], kinds[t], 1 - c)
            pltpu.make_async_remote_copy(src_ref=landing, dst_ref=landing, send_sem=send_sems.at[t], recv_sem=recv_sems.at[t],
                                         device_id=(x, y, c), device_id_type=MESH).wait_recv()
        for cp in copies:
            cp.wait_send()
        for own in locals_:
            own.wait()

    return pl.pallas_call(
        body, name="grad_pair_assemble", out_shape=[_sds(shp, F32) for shp in out_shapes], in_specs=[HBM] * nt,
        out_specs=[HBM] * n_out,
        scratch_shapes=[pltpu.SemaphoreType.DMA((nt,)), pltpu.SemaphoreType.DMA((nt,)), pltpu.SemaphoreType.DMA((nt,))],
    )(*halves)


def _pack(arrays, width):
    flat = jnp.concatenate([a.reshape(-1) for a in arrays])
    pad = (-flat.size) % (8 * width)
    return jnp.pad(flat, (0, pad)).reshape(-1, width)


def _unpack(packed, shapes):
    flat = packed.reshape(-1)
    out, off = [], 0
    for shp in shapes:
        n = 1
        for dim in shp:
            n *= dim
        out.append(flat[off:off + n].reshape(shp))
        off += n
    return out


def kernel(x, c, ada_w, ada_b, norm_mix_g, norm_ffn_g, ab_w_in, a_conv_w, a_conv_b, a_norm_g, a_norm_b, b_norm_g, b_norm_b, b_w_s, b_bias, ab_w_out, pool_w, pool_scale, ffn_w1, ffn_w3, ffn_w2, final_g, loss_target, m_ada_w, m_ada_b, m_norm_mix_g, m_norm_ffn_g, m_ab_w_in, m_a_conv_w, m_a_conv_b, m_a_norm_g, m_a_norm_b, m_b_norm_g, m_b_norm_b, m_b_w_s, m_b_bias, m_ab_w_out, m_pool_w, m_pool_scale, m_ffn_w1, m_ffn_w3, m_ffn_w2, m_final_g, v_ada_w, v_ada_b, v_norm_mix_g, v_norm_ffn_g, v_ab_w_in, v_a_conv_w, v_a_conv_b, v_a_norm_g, v_a_norm_b, v_b_norm_g, v_b_norm_b, v_b_w_s, v_b_bias, v_ab_w_out, v_pool_w, v_pool_scale, v_ffn_w1, v_ffn_w3, v_ffn_w2, v_final_g):
    mx, my, mc = _mesh_pos()
    chip = 2 * mx + my
    dev = 4 * mx + 2 * my + mc
    x2 = x[0]
    target = loss_target[0]
    s, d = x2.shape
    depth = ada_w.shape[0]
    n_mod = ada_b.shape[1] // d
    n_even = ab_w_in.shape[0]
    da, db = a_conv_b.shape[1], b_norm_g.shape[1]
    nh = b_w_s.shape[1]
    kw = a_conv_w.shape[1]
    n_pool = pool_w.shape[1]
    tm_row = _pick(s, (256, 128))

    kinds = ["col", "row", "row", "col", "col", "row"]
    full = _allgather_weights([w.astype(BF16) for w in (ab_w_in, ab_w_out, pool_w, ffn_w1, ffn_w3, ffn_w2)], kinds)
    w_in_f, w_out_f, pool_f, w1_f, w3_f, w2_f = full

    pre = _allgather_small(_pack([c, a_conv_w, pool_scale], 128)).reshape(N_DEV, -1)
    n_cw, n_ps = a_conv_w.size, pool_scale.size
    c_all = pre[:, :d]
    cw_chips = pre[0::2, d:d + n_cw].reshape((N_CHIP,) + a_conv_w.shape)
    conv_w_full = jnp.concatenate([cw_chips[k] for k in range(N_CHIP)], axis=-1)
    ps_chips = pre[0::2, d + n_cw:d + n_cw + n_ps].reshape((N_CHIP,) + pool_scale.shape)
    pool_scale_full = jnp.concatenate([ps_chips[k] for k in range(N_CHIP)], axis=-1)
    c_pad = jnp.pad(c_all, ((0, 8), (0, 0)))
    n_ada = ada_w.shape[2]
    ada_b_mine = lax.dynamic_slice_in_dim(ada_b, chip * n_ada, n_ada, axis=1)[:, None, :]
    mod_part = _ada_fwd(c_pad, ada_w, ada_b_mine)[:, :N_DEV, :]
    mod_all = _allgather_small(mod_part.reshape(depth * N_DEV, n_ada))
    mod_chips = mod_all[0::2].reshape(N_CHIP, depth, N_DEV, n_ada)
    mod_mine = lax.dynamic_index_in_dim(mod_chips, dev, axis=2, keepdims=False)
    mod = jnp.transpose(mod_mine, (1, 0, 2)).reshape(depth, n_mod, 1, d)

    causal = jnp.tril(jnp.ones((CHUNK, CHUNK), dtype=bool))
    w_c = jnp.where(causal[None, None], b_w_s, 0.0).astype(BF16)
    bias_full = jnp.repeat(jnp.swapaxes(b_bias, 1, 2), CHUNK, axis=2)

    saved = []
    xs = x2
    for l in range(depth):
        sh1, sc1, g1, sh2, sc2, g2 = [mod[l, k] for k in range(n_mod)]
        i = l // 2
        st = {"x1": xs}
        if l % 2 == 0:
            h = _norm_mod(xs, norm_mix_g[l][None], sh1, sc1, tm_row)
            proj = _mm("ab_proj", [h], [w_in_f[i]], [BF16], _ep_store, tk=2048)[0]
            cat, a1 = _ab_mid_fwd(proj, conv_w_full[i], a_conv_b[i][None], a_norm_g[i][None], a_norm_b[i][None],
                                  b_norm_g[i][None], b_norm_b[i][None], w_c[i], bias_full[i], tm_row)
            xs, y1 = _mm("ab_out", [cat], [w_out_f[i]], [F32, BF16], _ep_residual, extras=[xs, g1], extra_kinds=["tile", "row"], tk=2048)
            st.update(h=h, proj=proj, a1=a1, cat=cat, y=y1)
        else:
            p = _pool_fwd(xs, norm_mix_g[l][None], sh1, sc1, tm_row)
            gate = g1 * pool_scale_full[i][None]
            xs, ymm = _grouped_fwd(p, pool_f[i], xs, gate, 1024)
            st.update(p=p, y=ymm, gate=gate)
        st["x2"] = xs
        h2 = _norm_mod(xs, norm_ffn_g[l][None], sh2, sc2, tm_row)
        u, t, z = _mm("ffn_up", [h2, h2], [w1_f[l], w3_f[l]], [BF16, BF16, BF16], _ep_swiglu, tn=512, tk=2048)
        xs, y2 = _mm("ffn_down", [z], [w2_f[l]], [F32, BF16], _ep_residual, extras=[xs, g2], extra_kinds=["tile", "row"], tk=512)
        st.update(h2=h2, u=u, t=t, z=z, y2=y2)
        saved.append(st)

    dx, fin_acc, loss_blk = _final_loss_bwd(xs, final_g[None], target, tm_row)
    loss = lax.psum(loss_blk[0, 0], MESH_AXES)
    d_final_g = fin_acc[0]
    dmod_rows = [None] * depth
    d_norm_mix, d_norm_ffn = [None] * depth, [None] * depth
    even_small = [None] * n_even
    d_pool_scale = [None] * (depth // 2)
    big = {}
    for l in reversed(range(depth)):
        sh1, sc1, g1, sh2, sc2, g2 = [mod[l, k] for k in range(n_mod)]
        st = saved[l]
        i = l // 2
        dyb, gacc = _gate_bwd(dx, st["y2"], g2, tm_row)
        d_g2 = gacc[0]
        du, dt = _mm("ffn_dz", [dyb], [w2_f[l]], [BF16, BF16], _ep_swiglu_bwd, trans_b=True, extras=[st["u"], st["t"]],
                     extra_kinds=["tile", "tile"], tn=512, tk=2048)
        big[("w2", l)] = _mm("ffn_dw2", [st["z"]], [dyb], [BF16], _ep_store, trans_a=True, tm=512)[0]
        big[("w1", l)] = _mm("ffn_dw1", [st["h2"]], [du], [BF16], _ep_store, trans_a=True, tn=512)[0]
        big[("w3", l)] = _mm("ffn_dw3", [st["h2"]], [dt], [BF16], _ep_store, trans_a=True, tn=512)[0]
        dh = _mm("ffn_dh", [du, dt], [w1_f[l], w3_f[l]], [F32], _ep_sum, trans_b=True, tk=512)[0]
        dx, nacc = _norm_mod_bwd(dh, st["x2"], dx, norm_ffn_g[l][None], sc2, tm_row)
        d_sh2, d_sc2, d_norm_ffn[l] = nacc[0], nacc[1], nacc[2]
        if l % 2 == 0:
            dyb, gacc = _gate_bwd(dx, st["y"], g1, tm_row)
            d_g1 = gacc[0]
            big[("w_out", i)] = _mm("ab_dw_out", [st["cat"]], [dyb], [BF16], _ep_store, trans_a=True)[0]
            dcat = _mm("ab_dcat", [dyb], [w_out_f[i]], [F32], _ep_store, trans_b=True, tk=2048)[0]
            dproj, dcw, vecs, dws, dbias = _ab_mid_bwd(dcat, st["proj"], st["a1"], conv_w_full[i], a_norm_g[i][None],
                                                       a_norm_b[i][None], b_norm_g[i][None], b_norm_b[i][None], w_c[i],
                                                       bias_full[i], tm_row)
            even_small[i] = dict(conv_w=dcw, a_norm_g=vecs[0, :da], a_norm_b=vecs[1, :da], conv_b=vecs[2, :da],
                                 b_norm_g=vecs[3, :db], b_norm_b=vecs[4, :db], w_s=dws, bias=dbias[:, :, 0])
            big[("w_in", i)] = _mm("ab_dw_in", [st["h"]], [dproj], [BF16], _ep_store, trans_a=True)[0]
            dh = _mm("ab_dh", [dproj], [w_in_f[i]], [F32], _ep_store, trans_b=True, tk=1024)[0]
            dx, nacc = _norm_mod_bwd(dh, st["x1"], dx, norm_mix_g[l][None], sc1, tm_row)
        else:
            dyb, gacc = _gate_bwd(dx, st["y"], st["gate"], tm_row)
            d_g1 = gacc[0] * pool_scale_full[i]
            d_pool_scale[i] = gacc[0] * g1[0]
            big[("pool", i)] = _grouped_dw(st["p"], dyb, n_pool, 1024)
            dp = _grouped_dx(dyb, pool_f[i], 1024)
            dx, nacc = _pool_bwd(dp, st["x1"], dx, norm_mix_g[l][None], sc1, tm_row)
        d_sh1, d_sc1, d_norm_mix[l] = nacc[0], nacc[1], nacc[2]
        dmod_rows[l] = jnp.concatenate([d_sh1, d_sc1, d_g1, d_sh2, d_sc2, d_g2])
    grad_x = dx[None]

    dmod = jnp.stack(dmod_rows)
    small = [dmod, jnp.stack(d_norm_mix), jnp.stack(d_norm_ffn),
             jnp.stack([e["conv_w"] for e in even_small]), jnp.stack([e["conv_b"] for e in even_small]),
             jnp.stack([e["a_norm_g"] for e in even_small]), jnp.stack([e["a_norm_b"] for e in even_small]),
             jnp.stack([e["b_norm_g"] for e in even_small]), jnp.stack([e["b_norm_b"] for e in even_small]),
             jnp.stack([e["w_s"] for e in even_small]), jnp.stack([e["bias"] for e in even_small]),
             jnp.stack(d_pool_scale), d_final_g]
    small_shapes = [a.shape for a in small]
    width = 1024 if d >= 1024 else 128
    gathered = _allgather_small(_pack(small, width))
    summed = _unpack(_sum_leading(gathered), small_shapes)
    (g_ada_b, g_norm_mix, g_norm_ffn, g_conv_w_full, g_conv_b, g_a_norm_g, g_a_norm_b, g_b_norm_g, g_b_norm_b, g_w_s, g_bias,
     g_pool_scale_full, g_final_g) = summed
    cw_cols = a_conv_w.shape[2]
    g_conv_w = lax.dynamic_slice_in_dim(g_conv_w_full, chip * cw_cols, cw_cols, axis=2)
    ps_cols = pool_scale.shape[1]
    g_pool_scale = lax.dynamic_slice_in_dim(g_pool_scale_full, chip * ps_cols, ps_cols, axis=1)

    dmod_all = gathered.reshape(N_DEV, -1)[:, :dmod.size].reshape(N_DEV, depth, n_mod * d)
    dmod_cols = lax.dynamic_slice_in_dim(dmod_all, chip * n_ada, n_ada, axis=2)
    dmod_cols = jnp.pad(jnp.transpose(dmod_cols, (1, 0, 2)), ((0, 0), (0, 8), (0, 0)))
    g_ada_w = _ada_bwd(c_pad, dmod_cols)

    order = ([("w_in", i, "col") for i in range(n_even)] + [("w_out", i, "row") for i in range(n_even)]
             + [("pool", i, "row") for i in range(depth // 2)] + [("w1", l, "col") for l in range(depth)]
             + [("w3", l, "col") for l in range(depth)] + [("w2", l, "row") for l in range(depth)])
    rs_kinds = [kd for _, _, kd in order]
    dws = [big[(nm, l)] for nm, l, _ in order]
    sibs = _pair_exchange(dws, rs_kinds)
    c_vec = jnp.reshape(mc, (1,)).astype(jnp.int32)
    cps = []
    for dw, sib, kind in zip(dws, sibs, rs_kinds):
        cp = _pair_add(dw.reshape(-1, dw.shape[-1]), sib.reshape(-1, sib.shape[-1]), kind, c_vec)
        cps.append(cp.reshape(sib.shape))
    slabs = _chip_exchange(cps, rs_kinds)
    halves = []
    for sl in slabs:
        tot = _sum_leading(sl.reshape(N_CHIP, -1, sl.shape[-1]), name="chip_sum")
        halves.append(tot.reshape(sl.shape[1:]))
    out_names = ["w_in", "w_out", "pool", "w1", "w3", "w2"]
    out_shapes = [ab_w_in.shape, ab_w_out.shape, pool_w.shape, ffn_w1.shape, ffn_w3.shape, ffn_w2.shape]
    groups = [(out_names.index(nm), l) for nm, l, _ in order]
    g_w_in, g_w_out, g_pool, g_w1, g_w3, g_w2 = _pair_assemble(halves, rs_kinds, groups, out_shapes)

    grads = [g_ada_w, g_ada_b, g_norm_mix, g_norm_ffn, g_w_in, g_conv_w, g_conv_b, g_a_norm_g, g_a_norm_b, g_b_norm_g,
             g_b_norm_b, g_w_s, g_bias, g_w_out, g_pool, g_pool_scale, g_w1, g_w3, g_w2, g_final_g]
    weights = [ada_w, ada_b, norm_mix_g, norm_ffn_g, ab_w_in, a_conv_w, a_conv_b, a_norm_g, a_norm_b, b_norm_g, b_norm_b, b_w_s,
               b_bias, ab_w_out, pool_w, pool_scale, ffn_w1, ffn_w3, ffn_w2, final_g]
    ms = [m_ada_w, m_ada_b, m_norm_mix_g, m_norm_ffn_g, m_ab_w_in, m_a_conv_w, m_a_conv_b, m_a_norm_g, m_a_norm_b, m_b_norm_g,
          m_b_norm_b, m_b_w_s, m_b_bias, m_ab_w_out, m_pool_w, m_pool_scale, m_ffn_w1, m_ffn_w3, m_ffn_w2, m_final_g]
    vs = [v_ada_w, v_ada_b, v_norm_mix_g, v_norm_ffn_g, v_ab_w_in, v_a_conv_w, v_a_conv_b, v_a_norm_g, v_a_norm_b, v_b_norm_g,
          v_b_norm_b, v_b_w_s, v_b_bias, v_ab_w_out, v_pool_w, v_pool_scale, v_ffn_w1, v_ffn_w3, v_ffn_w2, v_final_g]
    deltas, new_ms, new_vs = [], [], []
    for w, g, m, v in zip(weights, grads, ms, vs):
        g = g.reshape(w.shape)
        dl, nm_, nv_ = _adamw(w, g, m, v)
        deltas.append(dl)
        new_ms.append(nm_)
        new_vs.append(nv_)
    grads = [g.reshape(w.shape) for g, w in zip(grads, weights)]
    return (loss, grad_x, *grads, *deltas, *new_ms, *new_vs)
```

```python
import functools

import jax
import jax.numpy as jnp
from jax import lax
from jax.experimental import pallas as pl
from jax.experimental.pallas import tpu as pltpu

F32 = jnp.float32
BF16 = jnp.bfloat16
EPS = 1e-6
N_DEV = 8
N_CHIP = 4
MESH_AXES = ("x", "y", "c")
MESH = pl.DeviceIdType.MESH
V7X_VMEM_LIMIT_BYTES = 56 * 1024 * 1024
CONV_HALO = 32
POOL_HALO = 16
POOL_WINDOWS = (2, 4, 8, 16)
CHUNK = 128
ADAM_LR, ADAM_B1, ADAM_B2, ADAM_EPS, ADAM_WD, ADAM_STEP = 0.001, 0.9, 0.999, 1e-08, 0.01, 10
HBM = pl.BlockSpec(memory_space=pltpu.HBM)


def _params(sem=None):
    return pltpu.CompilerParams(dimension_semantics=sem, vmem_limit_bytes=V7X_VMEM_LIMIT_BYTES)


def _pick(n, prefs):
    for p in prefs:
        if p <= n and n % p == 0:
            return p
    return n


def _row_tile(rows, cols, target):
    if rows * cols <= target:
        return rows
    best = None
    for d in range(16, rows, 16):
        if rows % d == 0 and d * cols <= target:
            best = d
    return best if best is not None else rows


def _sds(shape, dtype):
    return jax.ShapeDtypeStruct(tuple(shape), dtype)


def _mm_core(name, grid, nk, a_list, b_list, a_spec, b_spec, dn, extras, extra_specs, out_shapes, out_specs,
             acc_shape, epilogue):
    n, n_ex, n_out = len(a_list), len(extras), len(out_shapes)

    def body(*refs):
        a_refs, b_refs = refs[:n], refs[n:2 * n]
        ex = refs[2 * n:2 * n + n_ex]
        outs = refs[2 * n + n_ex:2 * n + n_ex + n_out]
        accs = refs[2 * n + n_ex + n_out:]
        ps = [lax.dot_general(a[...], b[...], dn, preferred_element_type=F32) for a, b in zip(a_refs, b_refs)]
        if nk == 1:
            epilogue(ps, ex, outs)
            return
        k = pl.program_id(2)

        @pl.when(k == 0)
        def _():
            for acc, p in zip(accs, ps):
                acc[...] = p

        @pl.when(k > 0)
        def _():
            for acc, p in zip(accs, ps):
                acc[...] += p

        @pl.when(k == nk - 1)
        def _():
            epilogue([acc[...] for acc in accs], ex, outs)

    scratch = [] if nk == 1 else [pltpu.VMEM(acc_shape, F32) for _ in range(n)]
    return pl.pallas_call(
        body, grid=grid, name=name,
        in_specs=[a_spec] * n + [b_spec] * n + list(extra_specs),
        out_specs=list(out_specs), out_shape=list(out_shapes), scratch_shapes=scratch,
        compiler_params=_params(("parallel", "parallel", "arbitrary")),
    )(*a_list, *b_list, *extras)


def _mm(name, a_list, b_list, out_dtypes, epilogue, *, trans_a=False, trans_b=False, extras=(), extra_kinds=(),
        tm=1024, tn=1024, tk=1024):
    layer = None
    if isinstance(b_list[0], tuple):
        layer = b_list[0][1]
        b_list = [b for b, _ in b_list]
    a0 = a_list[0]
    b_shape = b_list[0].shape[-2:]
    m, kk = (a0.shape[1], a0.shape[0]) if trans_a else a0.shape
    nn = b_shape[0] if trans_b else b_shape[1]
    tm, tn, tk = _pick(m, (tm, 512, 256, 128)), _pick(nn, (tn, 512, 256, 128)), _pick(kk, (tk, 512, 256, 128))
    nk = kk // tk
    a_spec = pl.BlockSpec((tk, tm), lambda i, j, k: (k, i)) if trans_a else pl.BlockSpec((tm, tk), lambda i, j, k: (i, k))
    if layer is None:
        b_spec = pl.BlockSpec((tn, tk), lambda i, j, k: (j, k)) if trans_b else pl.BlockSpec((tk, tn), lambda i, j, k: (k, j))
    elif trans_b:
        b_spec = pl.BlockSpec((None, tn, tk), lambda i, j, k: (layer, j, k))
    else:
        b_spec = pl.BlockSpec((None, tk, tn), lambda i, j, k: (layer, k, j))
    dn = (((0 if trans_a else 1,), (1 if trans_b else 0,)), ((), ()))
    tile = pl.BlockSpec((tm, tn), lambda i, j, k: (i, j))
    row = pl.BlockSpec((1, tn), lambda i, j, k: (0, j))
    return _mm_core(name, (m // tm, nn // tn, nk), nk, a_list, b_list, a_spec, b_spec, dn, extras,
                    [tile if kd == "tile" else row for kd in extra_kinds],
                    [_sds((m, nn), dt) for dt in out_dtypes], [tile] * len(out_dtypes), (tm, tn), epilogue)


def _ep_store(ps, ex, outs):
    outs[0][...] = ps[0].astype(outs[0].dtype)


def _ep_sum(ps, ex, outs):
    outs[0][...] = (ps[0] + ps[1]).astype(outs[0].dtype)


def _ep_swiglu(ps, ex, outs):
    u, t = ps
    outs[0][...] = u.astype(BF16)
    outs[1][...] = t.astype(BF16)
    outs[2][...] = (u * jax.nn.sigmoid(u) * t).astype(BF16)


def _ep_residual(ps, ex, outs):
    x_ref, gate_ref = ex
    y = ps[0]
    outs[0][...] = x_ref[...] + gate_ref[...] * y
    outs[1][...] = y.astype(BF16)


def _ep_swiglu_bwd(ps, ex, outs):
    u = ex[0][...].astype(F32)
    t = ex[1][...].astype(F32)
    dz = ps[0]
    sg = jax.nn.sigmoid(u)
    outs[0][...] = (dz * t * (sg * (1.0 + u * (1.0 - sg)))).astype(BF16)
    outs[1][...] = (dz * (u * sg)).astype(BF16)


def _grouped_fwd(p, w, layer, x, gate, tm):
    s, d = p.shape
    _, g, kg, ng = w.shape
    tm = _pick(s, (tm, 512, 256, 128))
    tile_a = pl.BlockSpec((tm, kg), lambda i, j, k: (i, j))
    tile_o = pl.BlockSpec((tm, ng), lambda i, j, k: (i, j))
    return _mm_core("pool_mm_fwd", (s // tm, g, 1), 1, [p], [w], tile_a,
                    pl.BlockSpec((None, None, kg, ng), lambda i, j, k: (layer, j, 0, 0)), (((1,), (0,)), ((), ())),
                    [x, gate], [tile_o, pl.BlockSpec((1, ng), lambda i, j, k: (0, j))],
                    [_sds((s, g * ng), F32), _sds((s, g * ng), BF16)], [tile_o, tile_o], None, _ep_residual)


def _grouped_dx(dy, w, layer, tm):
    s, _ = dy.shape
    _, g, kg, ng = w.shape
    tm = _pick(s, (tm, 512, 256, 128))
    return _mm_core("pool_mm_dx", (s // tm, g, 1), 1, [dy], [w], pl.BlockSpec((tm, ng), lambda i, j, k: (i, j)),
                    pl.BlockSpec((None, None, kg, ng), lambda i, j, k: (layer, j, 0, 0)), (((1,), (1,)), ((), ())), [], [],
                    [_sds((s, g * kg), F32)], [pl.BlockSpec((tm, kg), lambda i, j, k: (i, j))], None, _ep_store)[0]


def _grouped_dw(p, dy, groups, tk):
    s, d = p.shape
    kg = d // groups
    ng = dy.shape[1] // groups
    tk = _pick(s, (tk, 512, 256, 128))
    nk = s // tk

    def ep(ps, ex, outs):
        outs[0][...] = ps[0].astype(BF16)

    return _mm_core("pool_mm_dw", (groups, 1, nk), nk, [p], [dy], pl.BlockSpec((tk, kg), lambda i, j, k: (k, i)),
                    pl.BlockSpec((tk, ng), lambda i, j, k: (k, i)), (((0,), (0,)), ((), ())), [], [],
                    [_sds((groups, kg, ng), BF16)], [pl.BlockSpec((None, kg, ng), lambda i, j, k: (i, 0, 0))],
                    (kg, ng), ep)[0]


def _rms_rstd(xv):
    return lax.rsqrt(jnp.mean(xv * xv, axis=-1, keepdims=True) + EPS)


def _norm_mod_math(xv, g, sh, sc):
    return ((xv * _rms_rstd(xv)) * g) * (1.0 + sc) + sh


def _norm_mod_bwd_math(dh, xv, dxo, g, sc, acc_ref):
    r = _rms_rstd(xv)
    xhat = xv * r
    acc_ref[0:1, :] += jnp.sum(dh, axis=0, keepdims=True)
    acc_ref[1:2, :] += jnp.sum(dh * (xhat * g), axis=0, keepdims=True)
    dhn = dh * (1.0 + sc)
    acc_ref[2:3, :] += jnp.sum(dhn * xhat, axis=0, keepdims=True)
    dxh = dhn * g
    return dxo + r * (dxh - xhat * jnp.mean(dxh * xhat, axis=-1, keepdims=True))


def _vec_spec(d):
    return pl.BlockSpec((1, d), lambda i: (0, 0))


def _acc_spec(d):
    return pl.BlockSpec((8, d), lambda i: (0, 0))


def _norm_mod(x, g, sh, sc, tm):
    s, d = x.shape

    def body(x_ref, g_ref, sh_ref, sc_ref, h_ref):
        h_ref[...] = _norm_mod_math(x_ref[...], g_ref[...], sh_ref[...], sc_ref[...]).astype(BF16)

    row = pl.BlockSpec((tm, d), lambda i: (i, 0))
    return pl.pallas_call(body, grid=(s // tm,), name="norm_mod", in_specs=[row] + [_vec_spec(d)] * 3, out_specs=row,
                          out_shape=_sds((s, d), BF16), compiler_params=_params(("parallel",)))(x, g, sh, sc)


def _norm_mod_bwd(dh, x, dxo, g, sc, tm):
    s, d = x.shape

    def body(dh_ref, x_ref, dxo_ref, g_ref, sc_ref, dx_ref, acc_ref):
        @pl.when(pl.program_id(0) == 0)
        def _():
            acc_ref[...] = jnp.zeros_like(acc_ref)

        dx_ref[...] = _norm_mod_bwd_math(dh_ref[...], x_ref[...], dxo_ref[...], g_ref[...], sc_ref[...], acc_ref)

    row = pl.BlockSpec((tm, d), lambda i: (i, 0))
    return pl.pallas_call(body, grid=(s // tm,), name="norm_mod_bwd", in_specs=[row, row, row, _vec_spec(d), _vec_spec(d)],
                          out_specs=[row, _acc_spec(d)], out_shape=[_sds((s, d), F32), _sds((8, d), F32)],
                          compiler_params=_params(("arbitrary",)))(dh, x, dxo, g, sc)


def _gate_bwd(dx, y, gate, tm):
    s, d = dx.shape

    def body(dx_ref, y_ref, gate_ref, dy_ref, acc_ref):
        @pl.when(pl.program_id(0) == 0)
        def _():
            acc_ref[...] = jnp.zeros_like(acc_ref)

        dxv = dx_ref[...]
        dy_ref[...] = (dxv * gate_ref[...]).astype(BF16)
        acc_ref[0:1, :] += jnp.sum(dxv * y_ref[...].astype(F32), axis=0, keepdims=True)

    row = pl.BlockSpec((tm, d), lambda i: (i, 0))
    return pl.pallas_call(body, grid=(s // tm,), name="gate_bwd", in_specs=[row, row, _vec_spec(d)],
                          out_specs=[row, _acc_spec(d)], out_shape=[_sds((s, d), BF16), _sds((8, d), F32)],
                          compiler_params=_params(("arbitrary",)))(dx, y, gate)


def _final_loss_bwd(x, g, target, tm):
    s, d = x.shape

    def body(x_ref, g_ref, t_ref, dx_ref, acc_ref, loss_ref):
        @pl.when(pl.program_id(0) == 0)
        def _():
            acc_ref[...] = jnp.zeros_like(acc_ref)
            loss_ref[...] = jnp.zeros_like(loss_ref)

        xv = x_ref[...]
        gv = g_ref[...]
        r = _rms_rstd(xv)
        xhat = xv * r
        err = xhat * gv - t_ref[...]
        loss_ref[...] += (0.5 / d) * jnp.sum(err * err)
        dy = err * (1.0 / d)
        acc_ref[0:1, :] += jnp.sum(dy * xhat, axis=0, keepdims=True)
        dxh = dy * gv
        dx_ref[...] = r * (dxh - xhat * jnp.mean(dxh * xhat, axis=-1, keepdims=True))

    row = pl.BlockSpec((tm, d), lambda i: (i, 0))
    return pl.pallas_call(body, grid=(s // tm,), name="final_loss_bwd", in_specs=[row, _vec_spec(d), row],
                          out_specs=[row, _acc_spec(d), pl.BlockSpec((8, 128), lambda i: (0, 0))],
                          out_shape=[_sds((s, d), F32), _sds((8, d), F32), _sds((8, 128), F32)],
                          compiler_params=_params(("arbitrary",)))(x, g, target)


def _chunks(tm, width, rb, cb):
    rb, cb = min(rb, tm), min(cb, width)
    return [(r0, c0, rb, cb) for r0 in range(0, tm, rb) for c0 in range(0, width, cb)]


def _prev_halo_map(tm, halo):
    return lambda i: (jnp.maximum(i * (tm // halo) - 1, 0), 0)


def _next_halo_map(tm, halo, s):
    return lambda i: (jnp.minimum((i + 1) * (tm // halo), s // halo - 1), 0)


def _pool_fwd(x, g, sh, sc, tm):
    s, d = x.shape
    dg = d // len(POOL_WINDOWS)

    def body(x_ref, xh_ref, g_ref, sh_ref, sc_ref, p_ref, ext_ref):
        i = pl.program_id(0)
        gv, shv, scv = g_ref[...], sh_ref[...], sc_ref[...]
        ext_ref[POOL_HALO:, :] = _norm_mod_math(x_ref[...], gv, shv, scv)
        ext_ref[0:POOL_HALO, :] = jnp.where(i == 0, 0.0, _norm_mod_math(xh_ref[...], gv, shv, scv))
        for gi, w in enumerate(POOL_WINDOWS):
            for r0, c0, rb, cb in _chunks(tm, dg, 64, 256):
                cols = pl.ds(gi * dg + c0, cb)
                tok = ext_ref[pl.ds(POOL_HALO + r0, rb), cols]
                acc = tok
                for j in range(1, w):
                    acc = acc + ext_ref[pl.ds(POOL_HALO + r0 - j, rb), cols]
                t_glob = i * tm + r0 + lax.broadcasted_iota(jnp.int32, (rb, 1), 0)
                cnt = jnp.minimum(t_glob + 1, w).astype(F32)
                p_ref[pl.ds(r0, rb), cols] = (acc / cnt - tok).astype(BF16)

    row = pl.BlockSpec((tm, d), lambda i: (i, 0))
    halo = pl.BlockSpec((POOL_HALO, d), _prev_halo_map(tm, POOL_HALO))
    return pl.pallas_call(body, grid=(s // tm,), name="pool_fwd", in_specs=[row, halo] + [_vec_spec(d)] * 3, out_specs=row,
                          out_shape=_sds((s, d), BF16), scratch_shapes=[pltpu.VMEM((tm + POOL_HALO, d), F32)],
                          compiler_params=_params(("parallel",)))(x, x, g, sh, sc)


def _pool_bwd(dp, x, dxo, g, sc, tm):
    s, d = x.shape
    dg = d // len(POOL_WINDOWS)
    n_tiles = s // tm

    def body(dp_ref, dph_ref, x_ref, dxo_ref, g_ref, sc_ref, dx_ref, acc_ref, ext_ref, dh_ref):
        i = pl.program_id(0)

        @pl.when(i == 0)
        def _():
            acc_ref[...] = jnp.zeros_like(acc_ref)

        for gi, w in enumerate(POOL_WINDOWS):
            cols = pl.ds(gi * dg, dg)
            t_main = i * tm + lax.broadcasted_iota(jnp.int32, (tm, 1), 0)
            ext_ref[0:tm, cols] = dp_ref[:, cols] / jnp.minimum(t_main + 1, w).astype(F32)
            ext_ref[tm:, cols] = jnp.where(i == n_tiles - 1, 0.0, dph_ref[:, cols] * (1.0 / w))
            for r0, c0, rb, cb in _chunks(tm, dg, 64, 256):
                cc = pl.ds(gi * dg + c0, cb)
                acc = ext_ref[pl.ds(r0, rb), cc]
                for j in range(1, w):
                    acc = acc + ext_ref[pl.ds(r0 + j, rb), cc]
                dh_ref[pl.ds(r0, rb), cc] = acc - dp_ref[pl.ds(r0, rb), cc]
        dx_ref[...] = _norm_mod_bwd_math(dh_ref[...], x_ref[...], dxo_ref[...], g_ref[...], sc_ref[...], acc_ref)

    row = pl.BlockSpec((tm, d), lambda i: (i, 0))
    halo = pl.BlockSpec((POOL_HALO, d), _next_halo_map(tm, POOL_HALO, s))
    return pl.pallas_call(body, grid=(n_tiles,), name="pool_bwd", in_specs=[row, halo, row, row, _vec_spec(d), _vec_spec(d)],
                          out_specs=[row, _acc_spec(d)], out_shape=[_sds((s, d), F32), _sds((8, d), F32)],
                          scratch_shapes=[pltpu.VMEM((tm + POOL_HALO, d), F32), pltpu.VMEM((tm, d), F32)],
                          compiler_params=_params(("arbitrary",)))(dp, dp, x, dxo, g, sc)


def _layernorm_fwd(v, g, b):
    mu = jnp.mean(v, axis=-1, keepdims=True)
    xc = v - mu
    rstd = lax.rsqrt(jnp.mean(xc * xc, axis=-1, keepdims=True) + EPS)
    yn = xc * rstd
    return yn * g + b, yn, rstd


def _layernorm_bwd(dz, yn, rstd, g):
    dyn = dz * g
    dv = rstd * (dyn - jnp.mean(dyn, axis=-1, keepdims=True) - yn * jnp.mean(dyn * yn, axis=-1, keepdims=True))
    return dv, jnp.sum(dz * yn, axis=0, keepdims=True), jnp.sum(dz, axis=0, keepdims=True)


def _ab_mid_fwd(proj, conv_w, conv_b, a_g, a_b, v_g, v_b, w_c, bias_full, tm):
    s = proj.shape[0]
    da = conv_b.shape[1]
    db = v_g.shape[1]
    nh = w_c.shape[0]
    kw = conv_w.shape[0]
    lead = CONV_HALO - (kw - 1)

    def body(p_ref, ph_ref, cw_ref, cb_ref, ag_ref, ab_ref, vg_ref, vb_ref, wc_ref, bias_ref, cat_ref, a1_ref, ext_ref,
             a1s_ref):
        i = pl.program_id(0)
        val = p_ref[:, 0:da].astype(F32)
        gat = p_ref[:, da:2 * da].astype(F32)
        ext_ref[CONV_HALO:, :] = val * jax.nn.sigmoid(gat)
        hv = ph_ref[:, 0:da].astype(F32)
        hg = ph_ref[:, da:2 * da].astype(F32)
        ext_ref[0:CONV_HALO, :] = jnp.where(i == 0, 0.0, hv * jax.nn.sigmoid(hg))
        for r0, c0, rb, cb in _chunks(tm, da, 64, 256):
            cols = pl.ds(c0, cb)
            acc = jnp.broadcast_to(cb_ref[:, cols], (rb, cb))
            for k in range(kw):
                acc = acc + cw_ref[k:k + 1, cols] * ext_ref[pl.ds(r0 + lead + k, rb), cols]
            a1s_ref[pl.ds(r0, rb), cols] = acc
        a1 = a1s_ref[...]
        a1_ref[...] = a1.astype(BF16)
        z, _, _ = _layernorm_fwd(a1, ag_ref[...], ab_ref[...])
        cat_ref[:, 0:da] = (z * jax.nn.sigmoid(z)).astype(BF16)

        bu = p_ref[:, 2 * da:2 * da + db].astype(F32)
        bv = p_ref[:, 2 * da + db:].astype(F32)
        vn, _, _ = _layernorm_fwd(bv, vg_ref[...], vb_ref[...])
        vnb = vn.astype(BF16)
        for n in range(tm // CHUNK):
            rows = slice(n * CHUNK, (n + 1) * CHUNK)
            for h in range(nh):
                hc = slice(h * CHUNK, (h + 1) * CHUNK)
                vo = jnp.dot(wc_ref[h], vnb[rows, hc], preferred_element_type=F32) + bias_ref[:, hc]
                cat_ref[rows, da + h * CHUNK:da + (h + 1) * CHUNK] = (bu[rows, hc] * vo).astype(BF16)

    full = lambda a: pl.BlockSpec(a.shape, lambda i: (0,) * a.ndim)
    return pl.pallas_call(
        body, grid=(s // tm,), name="ab_mid_fwd",
        in_specs=[pl.BlockSpec((tm, 2 * da + 2 * db), lambda i: (i, 0)),
                  pl.BlockSpec((CONV_HALO, 2 * da), _prev_halo_map(tm, CONV_HALO)),
                  full(conv_w), full(conv_b), full(a_g), full(a_b), full(v_g), full(v_b), full(w_c), full(bias_full)],
        out_specs=[pl.BlockSpec((tm, da + db), lambda i: (i, 0)), pl.BlockSpec((tm, da), lambda i: (i, 0))],
        out_shape=[_sds((s, da + db), BF16), _sds((s, da), BF16)],
        scratch_shapes=[pltpu.VMEM((tm + CONV_HALO, da), F32), pltpu.VMEM((tm, da), F32)],
        compiler_params=_params(("parallel",)),
    )(proj, proj, conv_w, conv_b, a_g, a_b, v_g, v_b, w_c, bias_full)


def _ab_mid_bwd(dcat, proj, a1, conv_w, a_g, a_b, v_g, v_b, w_c, bias_full, tm):
    s = proj.shape[0]
    da = a_g.shape[1]
    db = v_g.shape[1]
    nh = w_c.shape[0]
    kw = conv_w.shape[0]
    lead = CONV_HALO - (kw - 1)
    n_tiles = s // tm

    def body(dc_ref, dch_ref, p_ref, ph_ref, a1_ref, a1h_ref, cw_ref, ag_ref, ab_ref, vg_ref, vb_ref, wc_ref, bias_ref,
             dp_ref, dcw_ref, vec_ref, dws_ref, dbias_ref, ext_ref, dext_ref, dcw_acc, dvn_ref):
        i = pl.program_id(0)

        @pl.when(i == 0)
        def _():
            dcw_acc[...] = jnp.zeros_like(dcw_acc)
            vec_ref[...] = jnp.zeros_like(vec_ref)
            dws_ref[...] = jnp.zeros_like(dws_ref)
            dbias_ref[...] = jnp.zeros_like(dbias_ref)

        agv, abv = ag_ref[...], ab_ref[...]

        def silu_ln_bwd(a1v, d_a2):
            z, yn, rstd = _layernorm_fwd(a1v, agv, abv)
            sg = jax.nn.sigmoid(z)
            return _layernorm_bwd(d_a2 * (sg * (1.0 + z * (1.0 - sg))), yn, rstd, agv)

        d_a1, dga, dba = silu_ln_bwd(a1_ref[...].astype(F32), dc_ref[:, 0:da])
        dext_ref[0:tm, :] = d_a1
        d_a1h, _, _ = silu_ln_bwd(a1h_ref[...].astype(F32), dch_ref[...])
        dext_ref[tm:, :] = jnp.where(i == n_tiles - 1, 0.0, d_a1h)
        vec_ref[0:1, 0:da] += dga
        vec_ref[1:2, 0:da] += dba
        vec_ref[2:3, 0:da] += jnp.sum(d_a1, axis=0, keepdims=True)

        val = p_ref[:, 0:da].astype(F32)
        sgg = jax.nn.sigmoid(p_ref[:, da:2 * da].astype(F32))
        ext_ref[CONV_HALO:, :] = val * sgg
        hv = ph_ref[:, 0:da].astype(F32)
        hg = ph_ref[:, da:2 * da].astype(F32)
        ext_ref[0:CONV_HALO, :] = jnp.where(i == 0, 0.0, hv * jax.nn.sigmoid(hg))

        for r0, c0, rb, cb in _chunks(tm, da, 64, 256):
            cols = pl.ds(c0, cb)
            rows = pl.ds(r0, rb)
            d1 = dext_ref[rows, cols]
            acc = jnp.zeros((rb, cb), F32)
            for k in range(kw):
                acc = acc + cw_ref[k:k + 1, cols] * dext_ref[pl.ds(r0 + (kw - 1) - k, rb), cols]
                prod = d1 * ext_ref[pl.ds(r0 + lead + k, rb), cols]
                dcw_acc[k, :, cols] += jnp.sum(prod.reshape(rb // 8, 8, cb), axis=0)
            v = p_ref[rows, pl.ds(c0, cb)].astype(F32)
            sg = jax.nn.sigmoid(p_ref[rows, pl.ds(da + c0, cb)].astype(F32))
            dp_ref[rows, pl.ds(c0, cb)] = (acc * sg).astype(BF16)
            dp_ref[rows, pl.ds(da + c0, cb)] = (acc * v * sg * (1.0 - sg)).astype(BF16)

        vgv = vg_ref[...]
        bu = p_ref[:, 2 * da:2 * da + db].astype(F32)
        bv = p_ref[:, 2 * da + db:].astype(F32)
        vn, yn_v, rstd_v = _layernorm_fwd(bv, vgv, vb_ref[...])
        vnb = vn.astype(BF16)
        for n in range(tm // CHUNK):
            rows = slice(n * CHUNK, (n + 1) * CHUNK)
            for h in range(nh):
                hc = slice(h * CHUNK, (h + 1) * CHUNK)
                wch = wc_ref[h]
                blk = vnb[rows, hc]
                vo = jnp.dot(wch, blk, preferred_element_type=F32) + bias_ref[:, hc]
                d_bout = dc_ref[rows, da + h * CHUNK:da + (h + 1) * CHUNK]
                dp_ref[rows, 2 * da + h * CHUNK:2 * da + (h + 1) * CHUNK] = (d_bout * vo).astype(BF16)
                d_vo = d_bout * bu[rows, hc]
                dbias_ref[h] += jnp.sum(d_vo, axis=1, keepdims=True)
                d_vob = d_vo.astype(BF16)
                dws_ref[h] += lax.dot_general(d_vob, blk, (((1,), (1,)), ((), ())), preferred_element_type=F32)
                dvn_ref[rows, hc] = lax.dot_general(wch, d_vob, (((0,), (0,)), ((), ())), preferred_element_type=F32)
        d_bv, dgv, dbv = _layernorm_bwd(dvn_ref[...], yn_v, rstd_v, vgv)
        dp_ref[:, 2 * da + db:] = d_bv.astype(BF16)
        vec_ref[3:4, 0:db] += dgv
        vec_ref[4:5, 0:db] += dbv

        @pl.when(i == n_tiles - 1)
        def _():
            dcw_ref[...] = jnp.sum(dcw_acc[...], axis=1)
            causal = lax.broadcasted_iota(jnp.int32, (CHUNK, CHUNK), 0) >= lax.broadcasted_iota(jnp.int32, (CHUNK, CHUNK), 1)
            for h in range(nh):
                dws_ref[h] = jnp.where(causal, dws_ref[h], 0.0)

    full = lambda a: pl.BlockSpec(a.shape, lambda i: (0,) * a.ndim)
    wide = max(da, db)
    return pl.pallas_call(
        body, grid=(n_tiles,), name="ab_mid_bwd",
        in_specs=[pl.BlockSpec((tm, da + db), lambda i: (i, 0)),
                  pl.BlockSpec((CONV_HALO, da), _next_halo_map(tm, CONV_HALO, s)),
                  pl.BlockSpec((tm, 2 * da + 2 * db), lambda i: (i, 0)),
                  pl.BlockSpec((CONV_HALO, 2 * da), _prev_halo_map(tm, CONV_HALO)),
                  pl.BlockSpec((tm, da), lambda i: (i, 0)),
                  pl.BlockSpec((CONV_HALO, da), _next_halo_map(tm, CONV_HALO, s)),
                  full(conv_w), full(a_g), full(a_b), full(v_g), full(v_b), full(w_c), full(bias_full)],
        out_specs=[pl.BlockSpec((tm, 2 * da + 2 * db), lambda i: (i, 0)),
                   pl.BlockSpec((kw, da), lambda i: (0, 0)),
                   pl.BlockSpec((8, wide), lambda i: (0, 0)),
                   pl.BlockSpec((nh, CHUNK, CHUNK), lambda i: (0, 0, 0)),
                   pl.BlockSpec((nh, CHUNK, 1), lambda i: (0, 0, 0))],
        out_shape=[_sds((s, 2 * da + 2 * db), BF16), _sds((kw, da), F32), _sds((8, wide), F32),
                   _sds((nh, CHUNK, CHUNK), F32), _sds((nh, CHUNK, 1), F32)],
        scratch_shapes=[pltpu.VMEM((tm + CONV_HALO, da), F32), pltpu.VMEM((tm + CONV_HALO, da), F32),
                        pltpu.VMEM((kw, 8, da), F32), pltpu.VMEM((tm, db), F32)],
        compiler_params=_params(("arbitrary",)),
    )(dcat, dcat, proj, proj, a1, a1, conv_w, a_g, a_b, v_g, v_b, w_c, bias_full)


def _ada_fwd(c_all, w, b):
    nl, d, n = w.shape
    tn = _pick(n, (512, 256, 128))

    def body(c_ref, w_ref, b_ref, o_ref):
        cv = c_ref[...]
        cond = (cv * jax.nn.sigmoid(cv)).astype(BF16)
        o_ref[...] = jnp.dot(cond, w_ref[...].astype(BF16), preferred_element_type=F32) + b_ref[...]

    return pl.pallas_call(
        body, grid=(nl, n // tn), name="ada_fwd",
        in_specs=[pl.BlockSpec(c_all.shape, lambda l, j: (0, 0)), pl.BlockSpec((None, d, tn), lambda l, j: (l, 0, j)),
                  pl.BlockSpec((None, 1, tn), lambda l, j: (l, 0, j))],
        out_specs=pl.BlockSpec((None, c_all.shape[0], tn), lambda l, j: (l, 0, j)),
        out_shape=_sds((nl, c_all.shape[0], n), F32), compiler_params=_params(("parallel", "parallel")),
    )(c_all, w, b)


def _ada_bwd(c_all, dmod):
    nl, nb, n = dmod.shape
    d = c_all.shape[1]
    tn = _pick(n, (512, 256, 128))

    def body(c_ref, g_ref, o_ref):
        cv = c_ref[...]
        cond = (cv * jax.nn.sigmoid(cv)).astype(BF16)
        o_ref[...] = lax.dot_general(cond, g_ref[...].astype(BF16), (((0,), (0,)), ((), ())), preferred_element_type=F32)

    return pl.pallas_call(
        body, grid=(nl, n // tn), name="ada_bwd",
        in_specs=[pl.BlockSpec(c_all.shape, lambda l, j: (0, 0)), pl.BlockSpec((None, nb, tn), lambda l, j: (l, 0, j))],
        out_specs=pl.BlockSpec((None, d, tn), lambda l, j: (l, 0, j)),
        out_shape=_sds((nl, d, n), F32), compiler_params=_params(("parallel", "parallel")),
    )(c_all, dmod)


def _sum_leading(a, out_dtype=F32, name="sum_leading"):
    n, r, c = a.shape
    tr = _row_tile(r, c, 256 * 1024)

    def body(a_ref, o_ref):
        acc = a_ref[0].astype(F32)
        for k in range(1, n):
            acc = acc + a_ref[k].astype(F32)
        o_ref[...] = acc.astype(out_dtype)

    return pl.pallas_call(body, grid=(r // tr,), name=name, in_specs=[pl.BlockSpec((n, tr, c), lambda i: (0, i, 0))],
                          out_specs=pl.BlockSpec((tr, c), lambda i: (i, 0)), out_shape=_sds((r, c), out_dtype),
                          compiler_params=_params(("parallel",)))(a)


def _cast_into_full(w, kind, s_vec):
    nl, r, c = w.shape
    tr = _row_tile(r, c, 512 * 1024)
    nb = r // tr
    if kind == "col":
        out_shape, out_spec = (nl, r, N_CHIP * c), pl.BlockSpec((None, tr, c), lambda l, i, sv: (l, i, sv[0]))
    else:
        out_shape, out_spec = (nl, N_CHIP * r, c), pl.BlockSpec((None, tr, c), lambda l, i, sv: (l, sv[0] * nb + i, 0))

    def body(sv_ref, w_ref, o_ref):
        o_ref[...] = w_ref[...].astype(BF16)

    return pl.pallas_call(
        body, name="cast_into_full",
        grid_spec=pltpu.PrefetchScalarGridSpec(num_scalar_prefetch=1, grid=(nl, nb),
                                               in_specs=[pl.BlockSpec((None, tr, c), lambda l, i, sv: (l, i, 0))], out_specs=out_spec),
        out_shape=_sds(out_shape, BF16), compiler_params=_params(("parallel", "parallel")),
    )(s_vec, w)


def _chip_sum_into(slab, g, g_shape, layer, kind, c_vec):
    n, r, c = slab.shape
    tr = _row_tile(r, c, 256 * 1024)
    nb = r // tr
    if kind == "col":
        out_spec = pl.BlockSpec((None, tr, c), lambda i, cv: (layer, cv[0] * nb + i, 0))
    else:
        out_spec = pl.BlockSpec((None, tr, c), lambda i, cv: (layer, i, cv[0]))
    in_specs = [pl.BlockSpec((n, tr, c), lambda i, cv: (0, i, 0))]
    args = [c_vec, slab]
    aliases = {}
    if g is not None:
        in_specs.append(pl.BlockSpec(memory_space=pl.ANY))
        args.append(g)
        aliases = {2: 0}

    def body(cv_ref, s_ref, *rest):
        o_ref = rest[-1]
        acc = s_ref[0].astype(F32)
        for k in range(1, n):
            acc = acc + s_ref[k].astype(F32)
        o_ref[...] = acc

    return pl.pallas_call(
        body, name="chip_sum",
        grid_spec=pltpu.PrefetchScalarGridSpec(num_scalar_prefetch=1, grid=(nb,), in_specs=in_specs, out_specs=out_spec),
        out_shape=_sds(g_shape, F32), input_output_aliases=aliases, compiler_params=_params(("parallel",)),
    )(*args)


def _pair_add(dw, sib, kind, c_vec):
    r, c = sib.shape
    tr = _row_tile(r, c, 512 * 1024)
    nb = r // tr
    if kind == "col":
        dw_spec = pl.BlockSpec((tr, c), lambda i, cv: (cv[0] * nb + i, 0))
    else:
        dw_spec = pl.BlockSpec((tr, c), lambda i, cv: (i, cv[0]))

    def body(cv_ref, dw_ref, sib_ref, o_ref):
        o_ref[...] = (dw_ref[...].astype(F32) + sib_ref[...].astype(F32)).astype(BF16)

    return pl.pallas_call(
        body, name="pair_add",
        grid_spec=pltpu.PrefetchScalarGridSpec(num_scalar_prefetch=1, grid=(nb,), in_specs=[dw_spec, pl.BlockSpec((tr, c), lambda i, cv: (i, 0))],
                                               out_specs=pl.BlockSpec((tr, c), lambda i, cv: (i, 0))),
        out_shape=_sds((r, c), BF16), compiler_params=_params(("parallel",)),
    )(c_vec, dw, sib)


def _adamw(w, g, m, v):
    shape = w.shape
    cols = shape[-1]
    rows = w.size // cols
    tr = _row_tile(rows, cols, 256 * 1024)
    bc1 = 1.0 - ADAM_B1 ** ADAM_STEP
    bc2 = 1.0 - ADAM_B2 ** ADAM_STEP

    def body(w_ref, g_ref, m_ref, v_ref, d_ref, mo_ref, vo_ref):
        gv = g_ref[...]
        mn = ADAM_B1 * m_ref[...] + (1.0 - ADAM_B1) * gv
        vn = ADAM_B2 * v_ref[...] + (1.0 - ADAM_B2) * (gv * gv)
        d_ref[...] = -ADAM_LR * ((mn / bc1) / (jnp.sqrt(vn / bc2) + ADAM_EPS) + ADAM_WD * w_ref[...])
        mo_ref[...] = mn
        vo_ref[...] = vn

    spec = pl.BlockSpec((tr, cols), lambda i: (i, 0))
    outs = pl.pallas_call(body, grid=(rows // tr,), name="adamw", in_specs=[spec] * 4, out_specs=[spec] * 3,
                          out_shape=[_sds((rows, cols), F32)] * 3, compiler_params=_params(("parallel",)))(
        *[a.reshape(rows, cols) for a in (w, g, m, v)])
    return [o.reshape(shape) for o in outs]


def _mesh_pos():
    return lax.axis_index("x"), lax.axis_index("y"), lax.axis_index("c")


def _other_chips(x, y):
    return [(1 - x, y), (x, 1 - y), (1 - x, 1 - y)]


def _allgather_small(a):
    r, c = a.shape

    def body(x_ref, out_ref, send_sems, recv_sems, local_sem):
        x, y, cc = _mesh_pos()
        me, sibling = (x, y, cc), (x, y, 1 - cc)
        chips = _other_chips(x, y)

        def slab(px, py, pc):
            return out_ref.at[4 * px + 2 * py + pc]

        def copy(k, block, to, src=None):
            return pltpu.make_async_remote_copy(src_ref=slab(*block) if src is None else src, dst_ref=slab(*block),
                                                send_sem=send_sems.at[k], recv_sem=recv_sems.at[k], device_id=to,
                                                device_id_type=MESH)

        mine = pltpu.make_async_copy(x_ref, slab(*me), local_sem)
        mine.start()
        first = [copy(0, me, sibling, src=x_ref)]
        first += [copy(1 + j, me, (*chip, cc), src=x_ref) for j, chip in enumerate(chips)]
        for cp in first:
            cp.start()
        passed = [copy(4 + j, (*chip, cc), sibling) for j, chip in enumerate(chips)]
        for j, chip in enumerate(chips):
            copy(1 + j, (*chip, cc), me).wait_recv()
            passed[j].start()
        copy(0, sibling, me).wait_recv()
        for j, chip in enumerate(chips):
            copy(4 + j, (*chip, 1 - cc), me).wait_recv()
        for cp in first + passed:
            cp.wait_send()
        mine.wait()

    return pl.pallas_call(
        body, name="allgather_small", out_shape=_sds((N_DEV, r, c), F32),
        in_specs=[pl.BlockSpec(memory_space=pltpu.VMEM)], out_specs=pl.BlockSpec(memory_space=pltpu.VMEM),
        scratch_shapes=[pltpu.SemaphoreType.DMA((7,)), pltpu.SemaphoreType.DMA((7,)), pltpu.SemaphoreType.DMA],
        compiler_params=pltpu.CompilerParams(vmem_limit_bytes=V7X_VMEM_LIMIT_BYTES),
    )(a)


def _idx(ref, rows=None, cols=None):
    lead = (slice(None),) * (len(ref.shape) - 2)
    return ref.at[lead + (slice(None) if rows is None else rows, slice(None) if cols is None else cols)]


def _half(ref, kind, c):
    r, cdim = ref.shape[-2:]
    if kind == "col":
        return _idx(ref, rows=pl.ds(c * (r // 2), r // 2))
    return _idx(ref, cols=pl.ds(c * (cdim // 2), cdim // 2))


def _shard_region(full, kind, shard_shape, s):
    r, cdim = shard_shape[-2:]
    if kind == "col":
        return _idx(full, cols=pl.ds(s * cdim, cdim))
    return _idx(full, rows=pl.ds(s * r, r))


def _allgather_weights(fulls, kinds, shard_shapes):
    nt = len(fulls)

    def body(*refs):
        fu = refs[nt:2 * nt]
        send_sems, recv_sems = refs[2 * nt:]
        x, y, c = _mesh_pos()
        chips = _other_chips(x, y)

        def part(t, chip, cc):
            return _half(_shard_region(fu[t], kinds[t], shard_shapes[t], 2 * chip[0] + chip[1]), kinds[t], cc)

        def copy(t, k, blk, to):
            return pltpu.make_async_remote_copy(src_ref=blk, dst_ref=blk, send_sem=send_sems.at[6 * t + k],
                                                recv_sem=recv_sems.at[6 * t + k], device_id=to, device_id_type=MESH)

        first, passed = [], []
        for t in range(nt):
            for j, chip in enumerate(chips):
                cp = copy(t, j, part(t, (x, y), c), (*chip, c))
                cp.start()
                first.append(cp)
        for t in range(nt):
            for j, chip in enumerate(chips):
                copy(t, j, part(t, chip, c), (x, y, c)).wait_recv()
                fw = copy(t, 3 + j, part(t, chip, c), (x, y, 1 - c))
                fw.start()
                passed.append(fw)
        for t in range(nt):
            for j, chip in enumerate(chips):
                copy(t, 3 + j, part(t, chip, 1 - c), (x, y, c)).wait_recv()
        for cp in first + passed:
            cp.wait_send()

    return pl.pallas_call(
        body, name="allgather_weights", out_shape=[_sds(f.shape, BF16) for f in fulls],
        in_specs=[HBM] * nt, out_specs=[HBM] * nt, input_output_aliases={t: t for t in range(nt)},
        scratch_shapes=[pltpu.SemaphoreType.DMA((6 * nt,)), pltpu.SemaphoreType.DMA((6 * nt,))],
    )(*fulls)


def _pair_exchange(dws, kinds):
    nt = len(dws)
    out_shapes = []
    for dw, kind in zip(dws, kinds):
        shp = list(dw.shape)
        shp[-2 if kind == "col" else -1] //= 2
        out_shapes.append(tuple(shp))

    def body(*refs):
        src, dst = refs[:nt], refs[nt:2 * nt]
        send_sems, recv_sems = refs[2 * nt:]
        x, y, c = _mesh_pos()
        copies = [pltpu.make_async_remote_copy(src_ref=_half(src[t], kinds[t], 1 - c), dst_ref=dst[t], send_sem=send_sems.at[t],
                                               recv_sem=recv_sems.at[t], device_id=(x, y, 1 - c), device_id_type=MESH)
                  for t in range(nt)]
        for cp in copies:
            cp.start()
        for cp in copies:
            cp.wait()

    return pl.pallas_call(
        body, name="grad_pair_exchange", out_shape=[_sds(shp, BF16) for shp in out_shapes], in_specs=[HBM] * nt,
        out_specs=[HBM] * nt, scratch_shapes=[pltpu.SemaphoreType.DMA((nt,)), pltpu.SemaphoreType.DMA((nt,))],
    )(*dws)


def _chip_block(ref, kind, s):
    r, cdim = ref.shape[-2:]
    if kind == "col":
        return _idx(ref, cols=pl.ds(s * (cdim // N_CHIP), cdim // N_CHIP))
    return _idx(ref, rows=pl.ds(s * (r // N_CHIP), r // N_CHIP))


def _chip_exchange(cps, kinds):
    nt = len(cps)
    out_shapes = []
    for cp, kind in zip(cps, kinds):
        shp = list(cp.shape)
        shp[-1 if kind == "col" else -2] //= N_CHIP
        out_shapes.append((N_CHIP,) + tuple(shp))

    def body(*refs):
        src, dst = refs[:nt], refs[nt:2 * nt]
        send_sems, recv_sems, local_sems = refs[2 * nt:]
        x, y, c = _mesh_pos()
        s = 2 * x + y
        chips = _other_chips(x, y)
        sends, locals_ = [], []
        for t in range(nt):
            own = pltpu.make_async_copy(_chip_block(src[t], kinds[t], s), dst[t].at[s], local_sems.at[t])
            own.start()
            locals_.append(own)
            for j, chip in enumerate(chips):
                cp = pltpu.make_async_remote_copy(src_ref=_chip_block(src[t], kinds[t], 2 * chip[0] + chip[1]), dst_ref=dst[t].at[s],
                                                  send_sem=send_sems.at[3 * t + j], recv_sem=recv_sems.at[3 * t + j],
                                                  device_id=(*chip, c), device_id_type=MESH)
                cp.start()
                sends.append(cp)
        for t in range(nt):
            for j, chip in enumerate(chips):
                landing = dst[t].at[2 * chip[0] + chip[1]]
                pltpu.make_async_remote_copy(src_ref=landing, dst_ref=landing, send_sem=send_sems.at[3 * t + j],
                                             recv_sem=recv_sems.at[3 * t + j], device_id=(x, y, c), device_id_type=MESH).wait_recv()
        for cp in sends:
            cp.wait_send()
        for own in locals_:
            own.wait()

    return pl.pallas_call(
        body, name="grad_chip_exchange", out_shape=[_sds(shp, BF16) for shp in out_shapes], in_specs=[HBM] * nt,
        out_specs=[HBM] * nt,
        scratch_shapes=[pltpu.SemaphoreType.DMA((3 * nt,)), pltpu.SemaphoreType.DMA((3 * nt,)), pltpu.SemaphoreType.DMA((nt,))],
    )(*cps)


def _pair_assemble(gs, kinds):
    nt = len(gs)

    def body(*refs):
        g = refs[nt:2 * nt]
        send_sems, recv_sems = refs[2 * nt:]
        x, y, c = _mesh_pos()
        copies = []
        for t in range(nt):
            mine = _half(g[t], kinds[t], c)
            cp = pltpu.make_async_remote_copy(src_ref=mine, dst_ref=mine, send_sem=send_sems.at[t], recv_sem=recv_sems.at[t],
                                              device_id=(x, y, 1 - c), device_id_type=MESH)
            cp.start()
            copies.append(cp)
        for t in range(nt):
            landing = _half(g[t], kinds[t], 1 - c)
            pltpu.make_async_remote_copy(src_ref=landing, dst_ref=landing, send_sem=send_sems.at[t], recv_sem=recv_sems.at[t],
                                         device_id=(x, y, c), device_id_type=MESH).wait_recv()
        for cp in copies:
            cp.wait_send()

    return pl.pallas_call(
        body, name="grad_pair_assemble", out_shape=[_sds(a.shape, F32) for a in gs], in_specs=[HBM] * nt,
        out_specs=[HBM] * nt, input_output_aliases={t: t for t in range(nt)},
        scratch_shapes=[pltpu.SemaphoreType.DMA((nt,)), pltpu.SemaphoreType.DMA((nt,))],
    )(*gs)


def _pack(arrays, width):
    flat = jnp.concatenate([a.reshape(-1) for a in arrays])
    pad = (-flat.size) % (8 * width)
    return jnp.pad(flat, (0, pad)).reshape(-1, width)


def _unpack(packed, shapes):
    flat = packed.reshape(-1)
    out, off = [], 0
    for shp in shapes:
        n = 1
        for dim in shp:
            n *= dim
        out.append(flat[off:off + n].reshape(shp))
        off += n
    return out


def kernel(x, c, ada_w, ada_b, norm_mix_g, norm_ffn_g, ab_w_in, a_conv_w, a_conv_b, a_norm_g, a_norm_b, b_norm_g, b_norm_b, b_w_s, b_bias, ab_w_out, pool_w, pool_scale, ffn_w1, ffn_w3, ffn_w2, final_g, loss_target, m_ada_w, m_ada_b, m_norm_mix_g, m_norm_ffn_g, m_ab_w_in, m_a_conv_w, m_a_conv_b, m_a_norm_g, m_a_norm_b, m_b_norm_g, m_b_norm_b, m_b_w_s, m_b_bias, m_ab_w_out, m_pool_w, m_pool_scale, m_ffn_w1, m_ffn_w3, m_ffn_w2, m_final_g, v_ada_w, v_ada_b, v_norm_mix_g, v_norm_ffn_g, v_ab_w_in, v_a_conv_w, v_a_conv_b, v_a_norm_g, v_a_norm_b, v_b_norm_g, v_b_norm_b, v_b_w_s, v_b_bias, v_ab_w_out, v_pool_w, v_pool_scale, v_ffn_w1, v_ffn_w3, v_ffn_w2, v_final_g):
    mx, my, mc = _mesh_pos()
    chip = 2 * mx + my
    dev = 4 * mx + 2 * my + mc
    x2 = x[0]
    target = loss_target[0]
    s, d = x2.shape
    depth = ada_w.shape[0]
    n_mod = ada_b.shape[1] // d
    n_even = ab_w_in.shape[0]
    da, db = a_conv_b.shape[1], b_norm_g.shape[1]
    nh = b_w_s.shape[1]
    kw = a_conv_w.shape[1]
    n_pool = pool_w.shape[1]
    tm_row = _pick(s, (256, 128))

    kinds = ["col", "row", "row", "col", "col", "row"]
    s_vec = jnp.reshape(chip, (1,)).astype(jnp.int32)
    c_vec = jnp.reshape(mc, (1,)).astype(jnp.int32)
    pool_w3 = pool_w.reshape((-1,) + pool_w.shape[2:])
    shards = [ab_w_in, ab_w_out, pool_w3, ffn_w1, ffn_w3, ffn_w2]
    full = _allgather_weights([_cast_into_full(w, kd, s_vec) for w, kd in zip(shards, kinds)], kinds, [w.shape for w in shards])
    w_in_f, w_out_f, pool_f, w1_f, w3_f, w2_f = full
    pool_f = pool_f.reshape(pool_w.shape[:2] + pool_f.shape[1:])

    pre = _allgather_small(_pack([c, a_conv_w, pool_scale], 128)).reshape(N_DEV, -1)
    n_cw, n_ps = a_conv_w.size, pool_scale.size
    c_all = pre[:, :d]
    cw_chips = pre[0::2, d:d + n_cw].reshape((N_CHIP,) + a_conv_w.shape)
    conv_w_full = jnp.concatenate([cw_chips[k] for k in range(N_CHIP)], axis=-1)
    ps_chips = pre[0::2, d + n_cw:d + n_cw + n_ps].reshape((N_CHIP,) + pool_scale.shape)
    pool_scale_full = jnp.concatenate([ps_chips[k] for k in range(N_CHIP)], axis=-1)
    c_pad = jnp.pad(c_all, ((0, 8), (0, 0)))
    n_ada = ada_w.shape[2]
    ada_b_mine = lax.dynamic_slice_in_dim(ada_b, chip * n_ada, n_ada, axis=1)[:, None, :]
    mod_part = _ada_fwd(c_pad, ada_w, ada_b_mine)[:, :N_DEV, :]
    mod_all = _allgather_small(mod_part.reshape(depth * N_DEV, n_ada))
    mod_chips = mod_all[0::2].reshape(N_CHIP, depth, N_DEV, n_ada)
    mod_mine = lax.dynamic_index_in_dim(mod_chips, dev, axis=2, keepdims=False)
    mod = jnp.transpose(mod_mine, (1, 0, 2)).reshape(depth, n_mod, 1, d)

    causal = jnp.tril(jnp.ones((CHUNK, CHUNK), dtype=bool))
    w_c = jnp.where(causal[None, None], b_w_s, 0.0).astype(BF16)
    bias_full = jnp.repeat(jnp.swapaxes(b_bias, 1, 2), CHUNK, axis=2)

    saved = []
    xs = x2
    for l in range(depth):
        sh1, sc1, g1, sh2, sc2, g2 = [mod[l, k] for k in range(n_mod)]
        i = l // 2
        st = {"x1": xs}
        if l % 2 == 0:
            h = _norm_mod(xs, norm_mix_g[l][None], sh1, sc1, tm_row)
            proj = _mm("ab_proj", [h], [(w_in_f, i)], [BF16], _ep_store, tk=2048)[0]
            cat, a1 = _ab_mid_fwd(proj, conv_w_full[i], a_conv_b[i][None], a_norm_g[i][None], a_norm_b[i][None],
                                  b_norm_g[i][None], b_norm_b[i][None], w_c[i], bias_full[i], tm_row)
            xs, y1 = _mm("ab_out", [cat], [(w_out_f, i)], [F32, BF16], _ep_residual, extras=[xs, g1], extra_kinds=["tile", "row"], tk=2048)
            st.update(h=h, proj=proj, a1=a1, cat=cat, y=y1)
        else:
            p = _pool_fwd(xs, norm_mix_g[l][None], sh1, sc1, tm_row)
            gate = g1 * pool_scale_full[i][None]
            xs, ymm = _grouped_fwd(p, pool_f, i, xs, gate, 1024)
            st.update(p=p, y=ymm, gate=gate)
        st["x2"] = xs
        h2 = _norm_mod(xs, norm_ffn_g[l][None], sh2, sc2, tm_row)
        u, t, z = _mm("ffn_up", [h2, h2], [(w1_f, l), (w3_f, l)], [BF16, BF16, BF16], _ep_swiglu, tn=512, tk=2048)
        xs, y2 = _mm("ffn_down", [z], [(w2_f, l)], [F32, BF16], _ep_residual, extras=[xs, g2], extra_kinds=["tile", "row"],
                     tn=256, tk=ffn_w2.shape[1] * N_CHIP)
        st.update(h2=h2, u=u, t=t, z=z, y2=y2)
        saved.append(st)

    dx, fin_acc, loss_blk = _final_loss_bwd(xs, final_g[None], target, tm_row)
    loss = lax.psum(loss_blk[0, 0], MESH_AXES)
    d_final_g = fin_acc[0]
    dmod_rows = [None] * depth
    d_norm_mix, d_norm_ffn = [None] * depth, [None] * depth
    even_small = [None] * n_even
    d_pool_scale = [None] * (depth // 2)
    big = {}
    for l in reversed(range(depth)):
        sh1, sc1, g1, sh2, sc2, g2 = [mod[l, k] for k in range(n_mod)]
        st = saved[l]
        i = l // 2
        dyb, gacc = _gate_bwd(dx, st["y2"], g2, tm_row)
        d_g2 = gacc[0]
        du, dt = _mm("ffn_dz", [dyb], [(w2_f, l)], [BF16, BF16], _ep_swiglu_bwd, trans_b=True, extras=[st["u"], st["t"]],
                     extra_kinds=["tile", "tile"], tn=512, tk=2048)
        big[("w2", l)] = _mm("ffn_dw2", [st["z"]], [dyb], [BF16], _ep_store, trans_a=True, tm=512, tn=512, tk=s)[0]
        big[("w1", l)] = _mm("ffn_dw1", [st["h2"]], [du], [BF16], _ep_store, trans_a=True, tm=512, tn=512, tk=s)[0]
        big[("w3", l)] = _mm("ffn_dw3", [st["h2"]], [dt], [BF16], _ep_store, trans_a=True, tm=512, tn=512, tk=s)[0]
        dh = _mm("ffn_dh", [du, dt], [(w1_f, l), (w3_f, l)], [F32], _ep_sum, trans_b=True, tn=512, tk=ffn_w2.shape[1] * 2)[0]
        dx, nacc = _norm_mod_bwd(dh, st["x2"], dx, norm_ffn_g[l][None], sc2, tm_row)
        d_sh2, d_sc2, d_norm_ffn[l] = nacc[0], nacc[1], nacc[2]
        if l % 2 == 0:
            dyb, gacc = _gate_bwd(dx, st["y"], g1, tm_row)
            d_g1 = gacc[0]
            big[("w_out", i)] = _mm("ab_dw_out", [st["cat"]], [dyb], [BF16], _ep_store, trans_a=True, tm=512, tn=512, tk=s)[0]
            dcat = _mm("ab_dcat", [dyb], [(w_out_f, i)], [F32], _ep_store, trans_b=True, tk=2048)[0]
            dproj, dcw, vecs, dws, dbias = _ab_mid_bwd(dcat, st["proj"], st["a1"], conv_w_full[i], a_norm_g[i][None],
                                                       a_norm_b[i][None], b_norm_g[i][None], b_norm_b[i][None], w_c[i],
                                                       bias_full[i], tm_row)
            even_small[i] = dict(conv_w=dcw, a_norm_g=vecs[0, :da], a_norm_b=vecs[1, :da], conv_b=vecs[2, :da],
                                 b_norm_g=vecs[3, :db], b_norm_b=vecs[4, :db], w_s=dws, bias=dbias[:, :, 0])
            big[("w_in", i)] = _mm("ab_dw_in", [st["h"]], [dproj], [BF16], _ep_store, trans_a=True, tm=512, tn=512, tk=s)[0]
            dh = _mm("ab_dh", [dproj], [(w_in_f, i)], [F32], _ep_store, trans_b=True, tn=512, tk=2 * da + 2 * db)[0]
            dx, nacc = _norm_mod_bwd(dh, st["x1"], dx, norm_mix_g[l][None], sc1, tm_row)
        else:
            dyb, gacc = _gate_bwd(dx, st["y"], st["gate"], tm_row)
            d_g1 = gacc[0] * pool_scale_full[i]
            d_pool_scale[i] = gacc[0] * g1[0]
            big[("pool", i)] = _grouped_dw(st["p"], dyb, n_pool, 1024)
            dp = _grouped_dx(dyb, pool_f, i, 1024)
            dx, nacc = _pool_bwd(dp, st["x1"], dx, norm_mix_g[l][None], sc1, tm_row)
        d_sh1, d_sc1, d_norm_mix[l] = nacc[0], nacc[1], nacc[2]
        dmod_rows[l] = jnp.concatenate([d_sh1, d_sc1, d_g1, d_sh2, d_sc2, d_g2])
    grad_x = dx[None]

    dmod = jnp.stack(dmod_rows)
    small = [dmod, jnp.stack(d_norm_mix), jnp.stack(d_norm_ffn),
             jnp.stack([e["conv_w"] for e in even_small]), jnp.stack([e["conv_b"] for e in even_small]),
             jnp.stack([e["a_norm_g"] for e in even_small]), jnp.stack([e["a_norm_b"] for e in even_small]),
             jnp.stack([e["b_norm_g"] for e in even_small]), jnp.stack([e["b_norm_b"] for e in even_small]),
             jnp.stack([e["w_s"] for e in even_small]), jnp.stack([e["bias"] for e in even_small]),
             jnp.stack(d_pool_scale), d_final_g]
    small_shapes = [a.shape for a in small]
    width = 1024 if d >= 1024 else 128
    gathered = _allgather_small(_pack(small, width))
    summed = _unpack(_sum_leading(gathered), small_shapes)
    (g_ada_b, g_norm_mix, g_norm_ffn, g_conv_w_full, g_conv_b, g_a_norm_g, g_a_norm_b, g_b_norm_g, g_b_norm_b, g_w_s, g_bias,
     g_pool_scale_full, g_final_g) = summed
    cw_cols = a_conv_w.shape[2]
    g_conv_w = lax.dynamic_slice_in_dim(g_conv_w_full, chip * cw_cols, cw_cols, axis=2)
    ps_cols = pool_scale.shape[1]
    g_pool_scale = lax.dynamic_slice_in_dim(g_pool_scale_full, chip * ps_cols, ps_cols, axis=1)

    dmod_all = gathered.reshape(N_DEV, -1)[:, :dmod.size].reshape(N_DEV, depth, n_mod * d)
    dmod_cols = lax.dynamic_slice_in_dim(dmod_all, chip * n_ada, n_ada, axis=2)
    dmod_cols = jnp.pad(jnp.transpose(dmod_cols, (1, 0, 2)), ((0, 0), (0, 8), (0, 0)))
    g_ada_w = _ada_bwd(c_pad, dmod_cols)

    order = ([("w_in", i, "col") for i in range(n_even)] + [("w_out", i, "row") for i in range(n_even)]
             + [("pool", i, "row") for i in range(depth // 2)] + [("w1", l, "col") for l in range(depth)]
             + [("w3", l, "col") for l in range(depth)] + [("w2", l, "row") for l in range(depth)])
    rs_kinds = [kd for _, _, kd in order]
    dws = [big[(nm, l)] for nm, l, _ in order]
    sibs = _pair_exchange(dws, rs_kinds)
    cps = []
    for dw, sib, kind in zip(dws, sibs, rs_kinds):
        cp = _pair_add(dw.reshape(-1, dw.shape[-1]), sib.reshape(-1, sib.shape[-1]), kind, c_vec)
        cps.append(cp.reshape(sib.shape))
    slabs = _chip_exchange(cps, rs_kinds)
    out_names = ["w_in", "w_out", "pool", "w1", "w3", "w2"]
    g_shapes = [ab_w_in.shape, ab_w_out.shape, (pool_w.shape[0], n_pool * pool_w.shape[2], pool_w.shape[3]), ffn_w1.shape,
                ffn_w3.shape, ffn_w2.shape]
    gs = [None] * len(out_names)
    for (nm, l, kind), sl in zip(order, slabs):
        o = out_names.index(nm)
        gs[o] = _chip_sum_into(sl.reshape(N_CHIP, -1, sl.shape[-1]), gs[o], g_shapes[o], l, kind, c_vec)
    g_w_in, g_w_out, g_pool, g_w1, g_w3, g_w2 = _pair_assemble(gs, kinds)

    grads = [g_ada_w, g_ada_b, g_norm_mix, g_norm_ffn, g_w_in, g_conv_w, g_conv_b, g_a_norm_g, g_a_norm_b, g_b_norm_g,
             g_b_norm_b, g_w_s, g_bias, g_w_out, g_pool, g_pool_scale, g_w1, g_w3, g_w2, g_final_g]
    weights = [ada_w, ada_b, norm_mix_g, norm_ffn_g, ab_w_in, a_conv_w, a_conv_b, a_norm_g, a_norm_b, b_norm_g, b_norm_b, b_w_s,
               b_bias, ab_w_out, pool_w, pool_scale, ffn_w1, ffn_w3, ffn_w2, final_g]
    ms = [m_ada_w, m_ada_b, m_norm_mix_g, m_norm_ffn_g, m_ab_w_in, m_a_conv_w, m_a_conv_b, m_a_norm_g, m_a_norm_b, m_b_norm_g,
          m_b_norm_b, m_b_w_s, m_b_bias, m_ab_w_out, m_pool_w, m_pool_scale, m_ffn_w1, m_ffn_w3, m_ffn_w2, m_final_g]
    vs = [v_ada_w, v_ada_b, v_norm_mix_g, v_norm_ffn_g, v_ab_w_in, v_a_conv_w, v_a_conv_b, v_a_norm_g, v_a_norm_b, v_b_norm_g,
          v_b_norm_b, v_b_w_s, v_b_bias, v_ab_w_out, v_pool_w, v_pool_scale, v_ffn_w1, v_ffn_w3, v_ffn_w2, v_final_g]
    deltas, new_ms, new_vs = [], [], []
    for w, g, m, v in zip(weights, grads, ms, vs):
        g = g.reshape(w.shape)
        dl, nm_, nv_ = _adamw(w, g, m, v)
        deltas.append(dl)
        new_ms.append(nm_)
        new_vs.append(nv_)
    grads = [g.reshape(w.shape) for g, w in zip(grads, weights)]
    return (loss, grad_x, *grads, *deltas, *new_ms, *new_vs)
```

```python
import functools

import jax
import jax.numpy as jnp
from jax import lax
from jax.experimental import pallas as pl
from jax.experimental.pallas import tpu as pltpu

F32 = jnp.float32
BF16 = jnp.bfloat16
EPS = 1e-6
N_DEV = 8
N_CHIP = 4
MESH_AXES = ("x", "y", "c")
MESH = pl.DeviceIdType.MESH
V7X_VMEM_LIMIT_BYTES = 56 * 1024 * 1024
CONV_HALO = 32
POOL_HALO = 16
POOL_WINDOWS = (2, 4, 8, 16)
CHUNK = 128
ADAM_LR, ADAM_B1, ADAM_B2, ADAM_EPS, ADAM_WD, ADAM_STEP = 0.001, 0.9, 0.999, 1e-08, 0.01, 10
HBM = pl.BlockSpec(memory_space=pltpu.HBM)


def _params(sem=None):
    return pltpu.CompilerParams(dimension_semantics=sem, vmem_limit_bytes=V7X_VMEM_LIMIT_BYTES)


def _pick(n, prefs):
    for p in prefs:
        if p <= n and n % p == 0:
            return p
    return n


def _row_tile(rows, cols, target):
    if rows * cols <= target:
        return rows
    best = None
    for d in range(16, rows, 16):
        if rows % d == 0 and d * cols <= target:
            best = d
    return best if best is not None else rows


def _sds(shape, dtype):
    return jax.ShapeDtypeStruct(tuple(shape), dtype)


def _mm_core(name, grid, nk, a_list, b_list, a_spec, b_spec, dn, extras, extra_specs, out_shapes, out_specs,
             acc_shape, epilogue):
    n, n_ex, n_out = len(a_list), len(extras), len(out_shapes)

    def body(*refs):
        a_refs, b_refs = refs[:n], refs[n:2 * n]
        ex = refs[2 * n:2 * n + n_ex]
        outs = refs[2 * n + n_ex:2 * n + n_ex + n_out]
        accs = refs[2 * n + n_ex + n_out:]
        ps = [lax.dot_general(a[...], b[...], dn, preferred_element_type=F32) for a, b in zip(a_refs, b_refs)]
        if nk == 1:
            epilogue(ps, ex, outs)
            return
        k = pl.program_id(2)

        @pl.when(k == 0)
        def _():
            for acc, p in zip(accs, ps):
                acc[...] = p

        @pl.when(k > 0)
        def _():
            for acc, p in zip(accs, ps):
                acc[...] += p

        @pl.when(k == nk - 1)
        def _():
            epilogue([acc[...] for acc in accs], ex, outs)

    scratch = [] if nk == 1 else [pltpu.VMEM(acc_shape, F32) for _ in range(n)]
    return pl.pallas_call(
        body, grid=grid, name=name,
        in_specs=[a_spec] * n + [b_spec] * n + list(extra_specs),
        out_specs=list(out_specs), out_shape=list(out_shapes), scratch_shapes=scratch,
        compiler_params=_params(("parallel", "parallel", "arbitrary")),
    )(*a_list, *b_list, *extras)


def _mm(name, a_list, b_list, out_dtypes, epilogue, *, trans_a=False, trans_b=False, extras=(), extra_kinds=(),
        tm=1024, tn=1024, tk=1024):
    layer = None
    if isinstance(b_list[0], tuple):
        layer = b_list[0][1]
        b_list = [b for b, _ in b_list]
    a0 = a_list[0]
    b_shape = b_list[0].shape[-2:]
    m, kk = (a0.shape[1], a0.shape[0]) if trans_a else a0.shape
    nn = b_shape[0] if trans_b else b_shape[1]
    tm, tn, tk = _pick(m, (tm, 512, 256, 128)), _pick(nn, (tn, 512, 256, 128)), _pick(kk, (tk, 512, 256, 128))
    nk = kk // tk
    a_spec = pl.BlockSpec((tk, tm), lambda i, j, k: (k, i)) if trans_a else pl.BlockSpec((tm, tk), lambda i, j, k: (i, k))
    if layer is None:
        b_spec = pl.BlockSpec((tn, tk), lambda i, j, k: (j, k)) if trans_b else pl.BlockSpec((tk, tn), lambda i, j, k: (k, j))
    elif trans_b:
        b_spec = pl.BlockSpec((None, tn, tk), lambda i, j, k: (layer, j, k))
    else:
        b_spec = pl.BlockSpec((None, tk, tn), lambda i, j, k: (layer, k, j))
    dn = (((0 if trans_a else 1,), (1 if trans_b else 0,)), ((), ()))
    tile = pl.BlockSpec((tm, tn), lambda i, j, k: (i, j))
    row = pl.BlockSpec((1, tn), lambda i, j, k: (0, j))
    return _mm_core(name, (m // tm, nn // tn, nk), nk, a_list, b_list, a_spec, b_spec, dn, extras,
                    [tile if kd == "tile" else row for kd in extra_kinds],
                    [_sds((m, nn), dt) for dt in out_dtypes], [tile] * len(out_dtypes), (tm, tn), epilogue)


def _ep_store(ps, ex, outs):
    outs[0][...] = ps[0].astype(outs[0].dtype)


def _ep_sum(ps, ex, outs):
    outs[0][...] = (ps[0] + ps[1]).astype(outs[0].dtype)


def _ep_swiglu(ps, ex, outs):
    u, t = ps
    outs[0][...] = u.astype(BF16)
    outs[1][...] = t.astype(BF16)
    outs[2][...] = (u * jax.nn.sigmoid(u) * t).astype(BF16)


def _ep_residual(ps, ex, outs):
    x_ref, gate_ref = ex
    y = ps[0]
    outs[0][...] = x_ref[...] + gate_ref[...] * y
    outs[1][...] = y.astype(BF16)


def _ep_swiglu_bwd(ps, ex, outs):
    u = ex[0][...].astype(F32)
    t = ex[1][...].astype(F32)
    dz = ps[0]
    sg = jax.nn.sigmoid(u)
    outs[0][...] = (dz * t * (sg * (1.0 + u * (1.0 - sg)))).astype(BF16)
    outs[1][...] = (dz * (u * sg)).astype(BF16)


def _grouped_fwd(p, w, layer, x, gate, tm):
    s, d = p.shape
    _, g, kg, ng = w.shape
    tm = _pick(s, (tm, 512, 256, 128))
    tile_a = pl.BlockSpec((tm, kg), lambda i, j, k: (i, j))
    tile_o = pl.BlockSpec((tm, ng), lambda i, j, k: (i, j))
    return _mm_core("pool_mm_fwd", (s // tm, g, 1), 1, [p], [w], tile_a,
                    pl.BlockSpec((None, None, kg, ng), lambda i, j, k: (layer, j, 0, 0)), (((1,), (0,)), ((), ())),
                    [x, gate], [tile_o, pl.BlockSpec((1, ng), lambda i, j, k: (0, j))],
                    [_sds((s, g * ng), F32), _sds((s, g * ng), BF16)], [tile_o, tile_o], None, _ep_residual)


def _grouped_dx(dy, w, layer, tm):
    s, _ = dy.shape
    _, g, kg, ng = w.shape
    tm = _pick(s, (tm, 512, 256, 128))
    return _mm_core("pool_mm_dx", (s // tm, g, 1), 1, [dy], [w], pl.BlockSpec((tm, ng), lambda i, j, k: (i, j)),
                    pl.BlockSpec((None, None, kg, ng), lambda i, j, k: (layer, j, 0, 0)), (((1,), (1,)), ((), ())), [], [],
                    [_sds((s, g * kg), F32)], [pl.BlockSpec((tm, kg), lambda i, j, k: (i, j))], None, _ep_store)[0]


def _grouped_dw(p, dy, groups, tk):
    s, d = p.shape
    kg = d // groups
    ng = dy.shape[1] // groups
    tk = _pick(s, (tk, 512, 256, 128))
    nk = s // tk

    def ep(ps, ex, outs):
        outs[0][...] = ps[0].astype(BF16)

    return _mm_core("pool_mm_dw", (groups, 1, nk), nk, [p], [dy], pl.BlockSpec((tk, kg), lambda i, j, k: (k, i)),
                    pl.BlockSpec((tk, ng), lambda i, j, k: (k, i)), (((0,), (0,)), ((), ())), [], [],
                    [_sds((groups, kg, ng), BF16)], [pl.BlockSpec((None, kg, ng), lambda i, j, k: (i, 0, 0))],
                    (kg, ng), ep)[0]


def _rms_rstd(xv):
    return lax.rsqrt(jnp.mean(xv * xv, axis=-1, keepdims=True) + EPS)


def _norm_mod_math(xv, g, sh, sc):
    return ((xv * _rms_rstd(xv)) * g) * (1.0 + sc) + sh


def _norm_mod_bwd_math(dh, xv, dxo, g, sc, acc_ref):
    r = _rms_rstd(xv)
    xhat = xv * r
    acc_ref[0:1, :] += jnp.sum(dh, axis=0, keepdims=True)
    acc_ref[1:2, :] += jnp.sum(dh * (xhat * g), axis=0, keepdims=True)
    dhn = dh * (1.0 + sc)
    acc_ref[2:3, :] += jnp.sum(dhn * xhat, axis=0, keepdims=True)
    dxh = dhn * g
    return dxo + r * (dxh - xhat * jnp.mean(dxh * xhat, axis=-1, keepdims=True))


def _vec_spec(d):
    return pl.BlockSpec((1, d), lambda i: (0, 0))


def _acc_spec(d):
    return pl.BlockSpec((8, d), lambda i: (0, 0))


def _norm_mod(x, g, sh, sc, tm):
    s, d = x.shape

    def body(x_ref, g_ref, sh_ref, sc_ref, h_ref):
        h_ref[...] = _norm_mod_math(x_ref[...], g_ref[...], sh_ref[...], sc_ref[...]).astype(BF16)

    row = pl.BlockSpec((tm, d), lambda i: (i, 0))
    return pl.pallas_call(body, grid=(s // tm,), name="norm_mod", in_specs=[row] + [_vec_spec(d)] * 3, out_specs=row,
                          out_shape=_sds((s, d), BF16), compiler_params=_params(("parallel",)))(x, g, sh, sc)


def _norm_mod_bwd(dh, x, dxo, g, sc, tm):
    s, d = x.shape

    def body(dh_ref, x_ref, dxo_ref, g_ref, sc_ref, dx_ref, acc_ref):
        @pl.when(pl.program_id(0) == 0)
        def _():
            acc_ref[...] = jnp.zeros_like(acc_ref)

        dx_ref[...] = _norm_mod_bwd_math(dh_ref[...], x_ref[...], dxo_ref[...], g_ref[...], sc_ref[...], acc_ref)

    row = pl.BlockSpec((tm, d), lambda i: (i, 0))
    return pl.pallas_call(body, grid=(s // tm,), name="norm_mod_bwd", in_specs=[row, row, row, _vec_spec(d), _vec_spec(d)],
                          out_specs=[row, _acc_spec(d)], out_shape=[_sds((s, d), F32), _sds((8, d), F32)],
                          compiler_params=_params(("arbitrary",)))(dh, x, dxo, g, sc)


def _gate_bwd(dx, y, gate, tm):
    s, d = dx.shape

    def body(dx_ref, y_ref, gate_ref, dy_ref, acc_ref):
        @pl.when(pl.program_id(0) == 0)
        def _():
            acc_ref[...] = jnp.zeros_like(acc_ref)

        dxv = dx_ref[...]
        dy_ref[...] = (dxv * gate_ref[...]).astype(BF16)
        acc_ref[0:1, :] += jnp.sum(dxv * y_ref[...].astype(F32), axis=0, keepdims=True)

    row = pl.BlockSpec((tm, d), lambda i: (i, 0))
    return pl.pallas_call(body, grid=(s // tm,), name="gate_bwd", in_specs=[row, row, _vec_spec(d)],
                          out_specs=[row, _acc_spec(d)], out_shape=[_sds((s, d), BF16), _sds((8, d), F32)],
                          compiler_params=_params(("arbitrary",)))(dx, y, gate)


def _final_loss_bwd(x, g, target, tm):
    s, d = x.shape

    def body(x_ref, g_ref, t_ref, dx_ref, acc_ref, loss_ref):
        @pl.when(pl.program_id(0) == 0)
        def _():
            acc_ref[...] = jnp.zeros_like(acc_ref)
            loss_ref[...] = jnp.zeros_like(loss_ref)

        xv = x_ref[...]
        gv = g_ref[...]
        r = _rms_rstd(xv)
        xhat = xv * r
        err = xhat * gv - t_ref[...]
        loss_ref[...] += (0.5 / d) * jnp.sum(err * err)
        dy = err * (1.0 / d)
        acc_ref[0:1, :] += jnp.sum(dy * xhat, axis=0, keepdims=True)
        dxh = dy * gv
        dx_ref[...] = r * (dxh - xhat * jnp.mean(dxh * xhat, axis=-1, keepdims=True))

    row = pl.BlockSpec((tm, d), lambda i: (i, 0))
    return pl.pallas_call(body, grid=(s // tm,), name="final_loss_bwd", in_specs=[row, _vec_spec(d), row],
                          out_specs=[row, _acc_spec(d), pl.BlockSpec((8, 128), lambda i: (0, 0))],
                          out_shape=[_sds((s, d), F32), _sds((8, d), F32), _sds((8, 128), F32)],
                          compiler_params=_params(("arbitrary",)))(x, g, target)


def _chunks(tm, width, rb, cb):
    rb, cb = min(rb, tm), min(cb, width)
    return [(r0, c0, rb, cb) for r0 in range(0, tm, rb) for c0 in range(0, width, cb)]


def _prev_halo_map(tm, halo):
    return lambda i: (jnp.maximum(i * (tm // halo) - 1, 0), 0)


def _next_halo_map(tm, halo, s):
    return lambda i: (jnp.minimum((i + 1) * (tm // halo), s // halo - 1), 0)


def _pool_fwd(x, g, sh, sc, tm):
    s, d = x.shape
    dg = d // len(POOL_WINDOWS)

    def body(x_ref, xh_ref, g_ref, sh_ref, sc_ref, p_ref, ext_ref):
        i = pl.program_id(0)
        gv, shv, scv = g_ref[...], sh_ref[...], sc_ref[...]
        ext_ref[POOL_HALO:, :] = _norm_mod_math(x_ref[...], gv, shv, scv)
        ext_ref[0:POOL_HALO, :] = jnp.where(i == 0, 0.0, _norm_mod_math(xh_ref[...], gv, shv, scv))
        for gi, w in enumerate(POOL_WINDOWS):
            for r0, c0, rb, cb in _chunks(tm, dg, 64, 256):
                cols = pl.ds(gi * dg + c0, cb)
                tok = ext_ref[pl.ds(POOL_HALO + r0, rb), cols]
                acc = tok
                for j in range(1, w):
                    acc = acc + ext_ref[pl.ds(POOL_HALO + r0 - j, rb), cols]
                t_glob = i * tm + r0 + lax.broadcasted_iota(jnp.int32, (rb, 1), 0)
                cnt = jnp.minimum(t_glob + 1, w).astype(F32)
                p_ref[pl.ds(r0, rb), cols] = (acc / cnt - tok).astype(BF16)

    row = pl.BlockSpec((tm, d), lambda i: (i, 0))
    halo = pl.BlockSpec((POOL_HALO, d), _prev_halo_map(tm, POOL_HALO))
    return pl.pallas_call(body, grid=(s // tm,), name="pool_fwd", in_specs=[row, halo] + [_vec_spec(d)] * 3, out_specs=row,
                          out_shape=_sds((s, d), BF16), scratch_shapes=[pltpu.VMEM((tm + POOL_HALO, d), F32)],
                          compiler_params=_params(("parallel",)))(x, x, g, sh, sc)


def _pool_bwd(dp, x, dxo, g, sc, tm):
    s, d = x.shape
    dg = d // len(POOL_WINDOWS)
    n_tiles = s // tm

    def body(dp_ref, dph_ref, x_ref, dxo_ref, g_ref, sc_ref, dx_ref, acc_ref, ext_ref, dh_ref):
        i = pl.program_id(0)

        @pl.when(i == 0)
        def _():
            acc_ref[...] = jnp.zeros_like(acc_ref)

        for gi, w in enumerate(POOL_WINDOWS):
            cols = pl.ds(gi * dg, dg)
            t_main = i * tm + lax.broadcasted_iota(jnp.int32, (tm, 1), 0)
            ext_ref[0:tm, cols] = dp_ref[:, cols] / jnp.minimum(t_main + 1, w).astype(F32)
            ext_ref[tm:, cols] = jnp.where(i == n_tiles - 1, 0.0, dph_ref[:, cols] * (1.0 / w))
            for r0, c0, rb, cb in _chunks(tm, dg, 64, 256):
                cc = pl.ds(gi * dg + c0, cb)
                acc = ext_ref[pl.ds(r0, rb), cc]
                for j in range(1, w):
                    acc = acc + ext_ref[pl.ds(r0 + j, rb), cc]
                dh_ref[pl.ds(r0, rb), cc] = acc - dp_ref[pl.ds(r0, rb), cc]
        dx_ref[...] = _norm_mod_bwd_math(dh_ref[...], x_ref[...], dxo_ref[...], g_ref[...], sc_ref[...], acc_ref)

    row = pl.BlockSpec((tm, d), lambda i: (i, 0))
    halo = pl.BlockSpec((POOL_HALO, d), _next_halo_map(tm, POOL_HALO, s))
    return pl.pallas_call(body, grid=(n_tiles,), name="pool_bwd", in_specs=[row, halo, row, row, _vec_spec(d), _vec_spec(d)],
                          out_specs=[row, _acc_spec(d)], out_shape=[_sds((s, d), F32), _sds((8, d), F32)],
                          scratch_shapes=[pltpu.VMEM((tm + POOL_HALO, d), F32), pltpu.VMEM((tm, d), F32)],
                          compiler_params=_params(("arbitrary",)))(dp, dp, x, dxo, g, sc)


def _layernorm_fwd(v, g, b):
    mu = jnp.mean(v, axis=-1, keepdims=True)
    xc = v - mu
    rstd = lax.rsqrt(jnp.mean(xc * xc, axis=-1, keepdims=True) + EPS)
    yn = xc * rstd
    return yn * g + b, yn, rstd


def _layernorm_bwd(dz, yn, rstd, g):
    dyn = dz * g
    dv = rstd * (dyn - jnp.mean(dyn, axis=-1, keepdims=True) - yn * jnp.mean(dyn * yn, axis=-1, keepdims=True))
    return dv, jnp.sum(dz * yn, axis=0, keepdims=True), jnp.sum(dz, axis=0, keepdims=True)


def _ab_mid_fwd(proj, conv_w, conv_b, a_g, a_b, v_g, v_b, w_c, bias_full, tm):
    s = proj.shape[0]
    da = conv_b.shape[1]
    db = v_g.shape[1]
    nh = w_c.shape[0]
    kw = conv_w.shape[0]
    lead = CONV_HALO - (kw - 1)

    def body(p_ref, ph_ref, cw_ref, cb_ref, ag_ref, ab_ref, vg_ref, vb_ref, wc_ref, bias_ref, cat_ref, a1_ref, ext_ref,
             a1s_ref):
        i = pl.program_id(0)
        val = p_ref[:, 0:da].astype(F32)
        gat = p_ref[:, da:2 * da].astype(F32)
        ext_ref[CONV_HALO:, :] = val * jax.nn.sigmoid(gat)
        hv = ph_ref[:, 0:da].astype(F32)
        hg = ph_ref[:, da:2 * da].astype(F32)
        ext_ref[0:CONV_HALO, :] = jnp.where(i == 0, 0.0, hv * jax.nn.sigmoid(hg))
        for r0, c0, rb, cb in _chunks(tm, da, 64, 256):
            cols = pl.ds(c0, cb)
            acc = jnp.broadcast_to(cb_ref[:, cols], (rb, cb))
            for k in range(kw):
                acc = acc + cw_ref[k:k + 1, cols] * ext_ref[pl.ds(r0 + lead + k, rb), cols]
            a1s_ref[pl.ds(r0, rb), cols] = acc
        a1 = a1s_ref[...]
        a1_ref[...] = a1.astype(BF16)
        z, _, _ = _layernorm_fwd(a1, ag_ref[...], ab_ref[...])
        cat_ref[:, 0:da] = (z * jax.nn.sigmoid(z)).astype(BF16)

        bu = p_ref[:, 2 * da:2 * da + db].astype(F32)
        bv = p_ref[:, 2 * da + db:].astype(F32)
        vn, _, _ = _layernorm_fwd(bv, vg_ref[...], vb_ref[...])
        vnb = vn.astype(BF16)
        for n in range(tm // CHUNK):
            rows = slice(n * CHUNK, (n + 1) * CHUNK)
            for h in range(nh):
                hc = slice(h * CHUNK, (h + 1) * CHUNK)
                vo = jnp.dot(wc_ref[h], vnb[rows, hc], preferred_element_type=F32) + bias_ref[:, hc]
                cat_ref[rows, da + h * CHUNK:da + (h + 1) * CHUNK] = (bu[rows, hc] * vo).astype(BF16)

    full = lambda a: pl.BlockSpec(a.shape, lambda i: (0,) * a.ndim)
    return pl.pallas_call(
        body, grid=(s // tm,), name="ab_mid_fwd",
        in_specs=[pl.BlockSpec((tm, 2 * da + 2 * db), lambda i: (i, 0)),
                  pl.BlockSpec((CONV_HALO, 2 * da), _prev_halo_map(tm, CONV_HALO)),
                  full(conv_w), full(conv_b), full(a_g), full(a_b), full(v_g), full(v_b), full(w_c), full(bias_full)],
        out_specs=[pl.BlockSpec((tm, da + db), lambda i: (i, 0)), pl.BlockSpec((tm, da), lambda i: (i, 0))],
        out_shape=[_sds((s, da + db), BF16), _sds((s, da), BF16)],
        scratch_shapes=[pltpu.VMEM((tm + CONV_HALO, da), F32), pltpu.VMEM((tm, da), F32)],
        compiler_params=_params(("parallel",)),
    )(proj, proj, conv_w, conv_b, a_g, a_b, v_g, v_b, w_c, bias_full)


def _ab_mid_bwd(dcat, proj, a1, conv_w, a_g, a_b, v_g, v_b, w_c, bias_full, tm):
    s = proj.shape[0]
    da = a_g.shape[1]
    db = v_g.shape[1]
    nh = w_c.shape[0]
    kw = conv_w.shape[0]
    lead = CONV_HALO - (kw - 1)
    n_tiles = s // tm

    def body(dc_ref, dch_ref, p_ref, ph_ref, a1_ref, a1h_ref, cw_ref, ag_ref, ab_ref, vg_ref, vb_ref, wc_ref, bias_ref,
             dp_ref, dcw_ref, vec_ref, dws_ref, dbias_ref, ext_ref, dext_ref, dcw_acc, dvn_ref):
        i = pl.program_id(0)

        @pl.when(i == 0)
        def _():
            dcw_acc[...] = jnp.zeros_like(dcw_acc)
            vec_ref[...] = jnp.zeros_like(vec_ref)
            dws_ref[...] = jnp.zeros_like(dws_ref)
            dbias_ref[...] = jnp.zeros_like(dbias_ref)

        agv, abv = ag_ref[...], ab_ref[...]

        def silu_ln_bwd(a1v, d_a2):
            z, yn, rstd = _layernorm_fwd(a1v, agv, abv)
            sg = jax.nn.sigmoid(z)
            return _layernorm_bwd(d_a2 * (sg * (1.0 + z * (1.0 - sg))), yn, rstd, agv)

        d_a1, dga, dba = silu_ln_bwd(a1_ref[...].astype(F32), dc_ref[:, 0:da])
        dext_ref[0:tm, :] = d_a1
        d_a1h, _, _ = silu_ln_bwd(a1h_ref[...].astype(F32), dch_ref[...])
        dext_ref[tm:, :] = jnp.where(i == n_tiles - 1, 0.0, d_a1h)
        vec_ref[0:1, 0:da] += dga
        vec_ref[1:2, 0:da] += dba
        vec_ref[2:3, 0:da] += jnp.sum(d_a1, axis=0, keepdims=True)

        val = p_ref[:, 0:da].astype(F32)
        sgg = jax.nn.sigmoid(p_ref[:, da:2 * da].astype(F32))
        ext_ref[CONV_HALO:, :] = val * sgg
        hv = ph_ref[:, 0:da].astype(F32)
        hg = ph_ref[:, da:2 * da].astype(F32)
        ext_ref[0:CONV_HALO, :] = jnp.where(i == 0, 0.0, hv * jax.nn.sigmoid(hg))

        for r0, c0, rb, cb in _chunks(tm, da, 64, 256):
            cols = pl.ds(c0, cb)
            rows = pl.ds(r0, rb)
            d1 = dext_ref[rows, cols]
            acc = jnp.zeros((rb, cb), F32)
            for k in range(kw):
                acc = acc + cw_ref[k:k + 1, cols] * dext_ref[pl.ds(r0 + (kw - 1) - k, rb), cols]
                prod = d1 * ext_ref[pl.ds(r0 + lead + k, rb), cols]
                dcw_acc[k, :, cols] += jnp.sum(prod.reshape(rb // 8, 8, cb), axis=0)
            v = p_ref[rows, pl.ds(c0, cb)].astype(F32)
            sg = jax.nn.sigmoid(p_ref[rows, pl.ds(da + c0, cb)].astype(F32))
            dp_ref[rows, pl.ds(c0, cb)] = (acc * sg).astype(BF16)
            dp_ref[rows, pl.ds(da + c0, cb)] = (acc * v * sg * (1.0 - sg)).astype(BF16)

        vgv = vg_ref[...]
        bu = p_ref[:, 2 * da:2 * da + db].astype(F32)
        bv = p_ref[:, 2 * da + db:].astype(F32)
        vn, yn_v, rstd_v = _layernorm_fwd(bv, vgv, vb_ref[...])
        vnb = vn.astype(BF16)
        for n in range(tm // CHUNK):
            rows = slice(n * CHUNK, (n + 1) * CHUNK)
            for h in range(nh):
                hc = slice(h * CHUNK, (h + 1) * CHUNK)
                wch = wc_ref[h]
                blk = vnb[rows, hc]
                vo = jnp.dot(wch, blk, preferred_element_type=F32) + bias_ref[:, hc]
                d_bout = dc_ref[rows, da + h * CHUNK:da + (h + 1) * CHUNK]
                dp_ref[rows, 2 * da + h * CHUNK:2 * da + (h + 1) * CHUNK] = (d_bout * vo).astype(BF16)
                d_vo = d_bout * bu[rows, hc]
                dbias_ref[h] += jnp.sum(d_vo, axis=1, keepdims=True)
                d_vob = d_vo.astype(BF16)
                dws_ref[h] += lax.dot_general(d_vob, blk, (((1,), (1,)), ((), ())), preferred_element_type=F32)
                dvn_ref[rows, hc] = lax.dot_general(wch, d_vob, (((0,), (0,)), ((), ())), preferred_element_type=F32)
        d_bv, dgv, dbv = _layernorm_bwd(dvn_ref[...], yn_v, rstd_v, vgv)
        dp_ref[:, 2 * da + db:] = d_bv.astype(BF16)
        vec_ref[3:4, 0:db] += dgv
        vec_ref[4:5, 0:db] += dbv

        @pl.when(i == n_tiles - 1)
        def _():
            dcw_ref[...] = jnp.sum(dcw_acc[...], axis=1)
            causal = lax.broadcasted_iota(jnp.int32, (CHUNK, CHUNK), 0) >= lax.broadcasted_iota(jnp.int32, (CHUNK, CHUNK), 1)
            for h in range(nh):
                dws_ref[h] = jnp.where(causal, dws_ref[h], 0.0)

    full = lambda a: pl.BlockSpec(a.shape, lambda i: (0,) * a.ndim)
    wide = max(da, db)
    return pl.pallas_call(
        body, grid=(n_tiles,), name="ab_mid_bwd",
        in_specs=[pl.BlockSpec((tm, da + db), lambda i: (i, 0)),
                  pl.BlockSpec((CONV_HALO, da), _next_halo_map(tm, CONV_HALO, s)),
                  pl.BlockSpec((tm, 2 * da + 2 * db), lambda i: (i, 0)),
                  pl.BlockSpec((CONV_HALO, 2 * da), _prev_halo_map(tm, CONV_HALO)),
                  pl.BlockSpec((tm, da), lambda i: (i, 0)),
                  pl.BlockSpec((CONV_HALO, da), _next_halo_map(tm, CONV_HALO, s)),
                  full(conv_w), full(a_g), full(a_b), full(v_g), full(v_b), full(w_c), full(bias_full)],
        out_specs=[pl.BlockSpec((tm, 2 * da + 2 * db), lambda i: (i, 0)),
                   pl.BlockSpec((kw, da), lambda i: (0, 0)),
                   pl.BlockSpec((8, wide), lambda i: (0, 0)),
                   pl.BlockSpec((nh, CHUNK, CHUNK), lambda i: (0, 0, 0)),
                   pl.BlockSpec((nh, CHUNK, 1), lambda i: (0, 0, 0))],
        out_shape=[_sds((s, 2 * da + 2 * db), BF16), _sds((kw, da), F32), _sds((8, wide), F32),
                   _sds((nh, CHUNK, CHUNK), F32), _sds((nh, CHUNK, 1), F32)],
        scratch_shapes=[pltpu.VMEM((tm + CONV_HALO, da), F32), pltpu.VMEM((tm + CONV_HALO, da), F32),
                        pltpu.VMEM((kw, 8, da), F32), pltpu.VMEM((tm, db), F32)],
        compiler_params=_params(("arbitrary",)),
    )(dcat, dcat, proj, proj, a1, a1, conv_w, a_g, a_b, v_g, v_b, w_c, bias_full)


def _ada_fwd(c_all, w, b):
    nl, d, n = w.shape
    tn = _pick(n, (512, 256, 128))

    def body(c_ref, w_ref, b_ref, o_ref):
        cv = c_ref[...]
        cond = (cv * jax.nn.sigmoid(cv)).astype(BF16)
        o_ref[...] = jnp.dot(cond, w_ref[...].astype(BF16), preferred_element_type=F32) + b_ref[...]

    return pl.pallas_call(
        body, grid=(nl, n // tn), name="ada_fwd",
        in_specs=[pl.BlockSpec(c_all.shape, lambda l, j: (0, 0)), pl.BlockSpec((None, d, tn), lambda l, j: (l, 0, j)),
                  pl.BlockSpec((None, 1, tn), lambda l, j: (l, 0, j))],
        out_specs=pl.BlockSpec((None, c_all.shape[0], tn), lambda l, j: (l, 0, j)),
        out_shape=_sds((nl, c_all.shape[0], n), F32), compiler_params=_params(("parallel", "parallel")),
    )(c_all, w, b)


def _ada_bwd(c_all, dmod):
    nl, nb, n = dmod.shape
    d = c_all.shape[1]
    tn = _pick(n, (512, 256, 128))

    def body(c_ref, g_ref, o_ref):
        cv = c_ref[...]
        cond = (cv * jax.nn.sigmoid(cv)).astype(BF16)
        o_ref[...] = lax.dot_general(cond, g_ref[...].astype(BF16), (((0,), (0,)), ((), ())), preferred_element_type=F32)

    return pl.pallas_call(
        body, grid=(nl, n // tn), name="ada_bwd",
        in_specs=[pl.BlockSpec(c_all.shape, lambda l, j: (0, 0)), pl.BlockSpec((None, nb, tn), lambda l, j: (l, 0, j))],
        out_specs=pl.BlockSpec((None, d, tn), lambda l, j: (l, 0, j)),
        out_shape=_sds((nl, d, n), F32), compiler_params=_params(("parallel", "parallel")),
    )(c_all, dmod)


def _sum_leading(a, out_dtype=F32, name="sum_leading"):
    n, r, c = a.shape
    tr = _row_tile(r, c, 256 * 1024)

    def body(a_ref, o_ref):
        acc = a_ref[0].astype(F32)
        for k in range(1, n):
            acc = acc + a_ref[k].astype(F32)
        o_ref[...] = acc.astype(out_dtype)

    return pl.pallas_call(body, grid=(r // tr,), name=name, in_specs=[pl.BlockSpec((n, tr, c), lambda i: (0, i, 0))],
                          out_specs=pl.BlockSpec((tr, c), lambda i: (i, 0)), out_shape=_sds((r, c), out_dtype),
                          compiler_params=_params(("parallel",)))(a)


def _cast_into_full(w, kind, s_vec, l0, nl):
    _, r, c = w.shape
    tr = _row_tile(r, c, 512 * 1024)
    nb = r // tr
    if kind == "col":
        out_shape, out_spec = (nl, r, N_CHIP * c), pl.BlockSpec((None, tr, c), lambda l, i, sv: (l, i, sv[0]))
    else:
        out_shape, out_spec = (nl, N_CHIP * r, c), pl.BlockSpec((None, tr, c), lambda l, i, sv: (l, sv[0] * nb + i, 0))

    def body(sv_ref, w_ref, o_ref):
        o_ref[...] = w_ref[...].astype(BF16)

    return pl.pallas_call(
        body, name="cast_into_full",
        grid_spec=pltpu.PrefetchScalarGridSpec(num_scalar_prefetch=1, grid=(nl, nb),
                                               in_specs=[pl.BlockSpec((None, tr, c), lambda l, i, sv: (l0 + l, i, 0))], out_specs=out_spec),
        out_shape=_sds(out_shape, BF16), compiler_params=_params(("parallel", "parallel")),
    )(s_vec, w)


def _chip_sum_into(slab, cp, g, g_shape, layer, kind, sc_vec, rows_per_group=None):
    n, r, c = slab.shape
    rg = r if rows_per_group is None else rows_per_group
    tr = _row_tile(rg, c, 256 * 1024)
    groups = r // rg
    nbg = rg // tr
    nb = groups * nbg
    n_sc = len(sc_vec)
    if kind == "col":
        out_spec = pl.BlockSpec((None, tr, c), lambda gi, i, *sc: (layer, sc[1][0] * nb + gi * nbg + i, 0))
        own_spec = pl.BlockSpec((tr, c), lambda gi, i, *sc: (gi * nbg + i, sc[0][0]))
    else:
        out_spec = pl.BlockSpec((None, tr, c), lambda gi, i, *sc: (layer, gi * nbg + i, sc[1][0]))
        own_spec = pl.BlockSpec((tr, c), lambda gi, i, *sc: (gi * (n * nbg) + sc[0][0] * nbg + i, 0))

    def other(k):
        return pl.BlockSpec((None, tr, c), lambda gi, i, *sc: (sc[1 + k][0], gi * nbg + i, 0))

    in_specs = [own_spec] + [other(k) for k in range(1, n)]
    args = list(sc_vec) + [cp] + [slab] * (n - 1)
    aliases = {}
    if g is not None:
        in_specs.append(pl.BlockSpec(memory_space=pl.ANY))
        args.append(g)
        aliases = {len(args) - 1: 0}

    def body(*refs):
        own_ref, rest = refs[n_sc], refs[n_sc + 1:]
        o_ref = rest[-1]
        acc = own_ref[...].astype(F32)
        for k in range(n - 1):
            acc = acc + rest[k][...].astype(F32)
        o_ref[...] = acc

    return pl.pallas_call(
        body, name="chip_sum",
        grid_spec=pltpu.PrefetchScalarGridSpec(num_scalar_prefetch=n_sc, grid=(groups, nbg), in_specs=in_specs, out_specs=out_spec),
        out_shape=_sds(g_shape, F32), input_output_aliases=aliases, compiler_params=_params(("parallel", "parallel")),
    )(*args)


def _pair_add(dw, sib, kind, c_vec):
    r, c = sib.shape
    tr = _row_tile(r, c, 512 * 1024)
    nb = r // tr
    if kind == "col":
        dw_spec = pl.BlockSpec((tr, c), lambda i, cv: (cv[0] * nb + i, 0))
    else:
        dw_spec = pl.BlockSpec((tr, c), lambda i, cv: (i, cv[0]))

    def body(cv_ref, dw_ref, sib_ref, o_ref):
        o_ref[...] = (dw_ref[...].astype(F32) + sib_ref[...].astype(F32)).astype(BF16)

    return pl.pallas_call(
        body, name="pair_add",
        grid_spec=pltpu.PrefetchScalarGridSpec(num_scalar_prefetch=1, grid=(nb,), in_specs=[dw_spec, pl.BlockSpec((tr, c), lambda i, cv: (i, 0))],
                                               out_specs=pl.BlockSpec((tr, c), lambda i, cv: (i, 0))),
        out_shape=_sds((r, c), BF16), compiler_params=_params(("parallel",)),
    )(c_vec, dw, sib)


def _adamw(w, g, m, v):
    shape = w.shape
    cols = shape[-1]
    rows = w.size // cols
    tr = _row_tile(rows, cols, 256 * 1024)
    bc1 = 1.0 - ADAM_B1 ** ADAM_STEP
    bc2 = 1.0 - ADAM_B2 ** ADAM_STEP

    def body(w_ref, g_ref, m_ref, v_ref, d_ref, mo_ref, vo_ref):
        gv = g_ref[...]
        mn = ADAM_B1 * m_ref[...] + (1.0 - ADAM_B1) * gv
        vn = ADAM_B2 * v_ref[...] + (1.0 - ADAM_B2) * (gv * gv)
        d_ref[...] = -ADAM_LR * ((mn / bc1) / (jnp.sqrt(vn / bc2) + ADAM_EPS) + ADAM_WD * w_ref[...])
        mo_ref[...] = mn
        vo_ref[...] = vn

    spec = pl.BlockSpec((tr, cols), lambda i: (i, 0))
    outs = pl.pallas_call(body, grid=(rows // tr,), name="adamw", in_specs=[spec] * 4, out_specs=[spec] * 3,
                          out_shape=[_sds((rows, cols), F32)] * 3, compiler_params=_params(("parallel",)))(
        *[a.reshape(rows, cols) for a in (w, g, m, v)])
    return [o.reshape(shape) for o in outs]


def _mesh_pos():
    return lax.axis_index("x"), lax.axis_index("y"), lax.axis_index("c")


def _other_chips(x, y):
    return [(1 - x, y), (x, 1 - y), (1 - x, 1 - y)]


def _allgather_small(a):
    r, c = a.shape

    def body(x_ref, out_ref, send_sems, recv_sems, local_sem):
        x, y, cc = _mesh_pos()
        me, sibling = (x, y, cc), (x, y, 1 - cc)
        chips = _other_chips(x, y)

        def slab(px, py, pc):
            return out_ref.at[4 * px + 2 * py + pc]

        def copy(k, block, to, src=None):
            return pltpu.make_async_remote_copy(src_ref=slab(*block) if src is None else src, dst_ref=slab(*block),
                                                send_sem=send_sems.at[k], recv_sem=recv_sems.at[k], device_id=to,
                                                device_id_type=MESH)

        mine = pltpu.make_async_copy(x_ref, slab(*me), local_sem)
        mine.start()
        first = [copy(0, me, sibling, src=x_ref)]
        first += [copy(1 + j, me, (*chip, cc), src=x_ref) for j, chip in enumerate(chips)]
        for cp in first:
            cp.start()
        passed = [copy(4 + j, (*chip, cc), sibling) for j, chip in enumerate(chips)]
        for j, chip in enumerate(chips):
            copy(1 + j, (*chip, cc), me).wait_recv()
            passed[j].start()
        copy(0, sibling, me).wait_recv()
        for j, chip in enumerate(chips):
            copy(4 + j, (*chip, 1 - cc), me).wait_recv()
        for cp in first + passed:
            cp.wait_send()
        mine.wait()

    return pl.pallas_call(
        body, name="allgather_small", out_shape=_sds((N_DEV, r, c), F32),
        in_specs=[pl.BlockSpec(memory_space=pltpu.VMEM)], out_specs=pl.BlockSpec(memory_space=pltpu.VMEM),
        scratch_shapes=[pltpu.SemaphoreType.DMA((7,)), pltpu.SemaphoreType.DMA((7,)), pltpu.SemaphoreType.DMA],
        compiler_params=pltpu.CompilerParams(vmem_limit_bytes=V7X_VMEM_LIMIT_BYTES),
    )(a)


def _idx(ref, rows=None, cols=None):
    lead = (slice(None),) * (len(ref.shape) - 2)
    return ref.at[lead + (slice(None) if rows is None else rows, slice(None) if cols is None else cols)]


def _half(ref, kind, c):
    r, cdim = ref.shape[-2:]
    if kind == "col":
        return _idx(ref, rows=pl.ds(c * (r // 2), r // 2))
    return _idx(ref, cols=pl.ds(c * (cdim // 2), cdim // 2))


def _shard_region(full, kind, shard_shape, s):
    r, cdim = shard_shape[-2:]
    if kind == "col":
        return _idx(full, cols=pl.ds(s * cdim, cdim))
    return _idx(full, rows=pl.ds(s * r, r))


def _allgather_weights(fulls, kinds, shard_shapes):
    nt = len(fulls)

    def body(*refs):
        fu = refs[nt:2 * nt]
        send_sems, recv_sems = refs[2 * nt:]
        x, y, c = _mesh_pos()
        chips = _other_chips(x, y)

        def part(t, chip, cc):
            return _half(_shard_region(fu[t], kinds[t], shard_shapes[t], 2 * chip[0] + chip[1]), kinds[t], cc)

        def copy(t, k, blk, to):
            return pltpu.make_async_remote_copy(src_ref=blk, dst_ref=blk, send_sem=send_sems.at[6 * t + k],
                                                recv_sem=recv_sems.at[6 * t + k], device_id=to, device_id_type=MESH)

        first, passed = [], []
        for t in range(nt):
            for j, chip in enumerate(chips):
                cp = copy(t, j, part(t, (x, y), c), (*chip, c))
                cp.start()
                first.append(cp)
        for t in range(nt):
            for j, chip in enumerate(chips):
                copy(t, j, part(t, chip, c), (x, y, c)).wait_recv()
                fw = copy(t, 3 + j, part(t, chip, c), (x, y, 1 - c))
                fw.start()
                passed.append(fw)
        for t in range(nt):
            for j, chip in enumerate(chips):
                copy(t, 3 + j, part(t, chip, 1 - c), (x, y, c)).wait_recv()
        for cp in first + passed:
            cp.wait_send()

    return pl.pallas_call(
        body, name="allgather_weights", out_shape=[_sds(f.shape, BF16) for f in fulls],
        in_specs=[HBM] * nt, out_specs=[HBM] * nt, input_output_aliases={t: t for t in range(nt)},
        scratch_shapes=[pltpu.SemaphoreType.DMA((6 * nt,)), pltpu.SemaphoreType.DMA((6 * nt,))],
    )(*fulls)


SEM = pl.BlockSpec(memory_space=pltpu.SEMAPHORE)
ANY = pl.BlockSpec(memory_space=pl.ANY)
EFFECT = pltpu.SideEffectType.DATAFLOW_SIDE_EFFECTING


def _gather_start(name, fulls, kinds, shard_shapes, prev):
    nt = len(fulls)

    def body(*refs):
        send_sems, recv_sems = refs[nt + 1], refs[nt + 2]
        fu = refs[nt + 3:2 * nt + 3]
        token = refs[2 * nt + 3]
        x, y, c = _mesh_pos()
        for t in range(nt):
            mine = _half(_shard_region(fu[t], kinds[t], shard_shapes[t], 2 * x + y), kinds[t], c)
            for j, chip in enumerate(_other_chips(x, y)):
                for e in range(2):
                    pltpu.make_async_remote_copy(src_ref=mine, dst_ref=mine, send_sem=send_sems.at[6 * t + 2 * j + e],
                                                 recv_sem=recv_sems.at[6 * t + 2 * j + c], device_id=(*chip, e),
                                                 device_id_type=MESH).start()
        token[...] = jnp.zeros_like(token)

    outs = pl.pallas_call(
        body, name=name,
        out_shape=(pltpu.SemaphoreType.DMA((6 * nt,)), pltpu.SemaphoreType.DMA((6 * nt,)), *[pltpu.HBM(f.shape, f.dtype) for f in fulls],
                   _sds((8, 128), F32)),
        in_specs=[HBM] * nt + [ANY], out_specs=(SEM, SEM, *[HBM] * nt, pl.BlockSpec(memory_space=pltpu.VMEM)),
        input_output_aliases={t: 2 + t for t in range(nt)}, compiler_params=pltpu.CompilerParams(has_side_effects=EFFECT),
    )(*fulls, prev)
    return outs[0], outs[1], list(outs[2:2 + nt]), outs[2 + nt]


def _gather_wait(name, send_sems, recv_sems, fulls, kinds, shard_shapes, after):
    nt = len(fulls)

    def body(*refs):
        fu = refs[:nt]
        send_sems, recv_sems = refs[nt], refs[nt + 1]
        x, y, c = _mesh_pos()

        def part(t, chip, cc):
            return _half(_shard_region(fu[t], kinds[t], shard_shapes[t], 2 * chip[0] + chip[1]), kinds[t], cc)

        for t in range(nt):
            for j, chip in enumerate(_other_chips(x, y)):
                for e in range(2):
                    mine = part(t, (x, y), c)
                    pltpu.make_async_remote_copy(src_ref=mine, dst_ref=mine, send_sem=send_sems.at[6 * t + 2 * j + e],
                                                 recv_sem=recv_sems.at[6 * t + 2 * j + e], device_id=(x, y, c),
                                                 device_id_type=MESH).wait_send()
                    landed = part(t, chip, e)
                    pltpu.make_async_remote_copy(src_ref=landed, dst_ref=landed, send_sem=send_sems.at[6 * t + 2 * j + e],
                                                 recv_sem=recv_sems.at[6 * t + 2 * j + e], device_id=(x, y, c),
                                                 device_id_type=MESH).wait_recv()

    return pl.pallas_call(
        body, name=name, out_shape=[pltpu.HBM(f.shape, f.dtype) for f in fulls], in_specs=[HBM] * nt + [SEM, SEM, ANY],
        out_specs=[HBM] * nt, input_output_aliases={t: t for t in range(nt)},
        compiler_params=pltpu.CompilerParams(has_side_effects=EFFECT),
    )(*fulls, send_sems, recv_sems, after)


def _reduce_start(name, cps, kinds, prev):
    nt = len(cps)
    slab_shapes = []
    for cp, kind in zip(cps, kinds):
        shp = list(cp.shape)
        shp[-1 if kind == "col" else -2] //= N_CHIP
        slab_shapes.append((N_CHIP,) + tuple(shp))

    def body(*refs):
        send_sems, recv_sems = refs[nt + 1], refs[nt + 2]
        src = refs[nt + 3:2 * nt + 3]
        dst = refs[2 * nt + 3:3 * nt + 3]
        token = refs[3 * nt + 3]
        x, y, c = _mesh_pos()
        s = 2 * x + y
        for t in range(nt):
            for j, chip in enumerate(_other_chips(x, y)):
                pltpu.make_async_remote_copy(src_ref=_chip_block(src[t], kinds[t], 2 * chip[0] + chip[1]), dst_ref=dst[t].at[s],
                                             send_sem=send_sems.at[3 * t + j], recv_sem=recv_sems.at[3 * t + j],
                                             device_id=(*chip, c), device_id_type=MESH).start()
        token[...] = jnp.zeros_like(token)

    outs = pl.pallas_call(
        body, name=name,
        out_shape=(pltpu.SemaphoreType.DMA((3 * nt,)), pltpu.SemaphoreType.DMA((3 * nt,)), *[pltpu.HBM(a.shape, a.dtype) for a in cps],
                   *[pltpu.HBM(shp, BF16) for shp in slab_shapes], _sds((8, 128), F32)),
        in_specs=[HBM] * nt + [ANY], out_specs=(SEM, SEM, *[HBM] * (2 * nt), pl.BlockSpec(memory_space=pltpu.VMEM)),
        input_output_aliases={t: 2 + t for t in range(nt)}, compiler_params=pltpu.CompilerParams(has_side_effects=EFFECT),
    )(*cps, prev)
    return outs[0], outs[1], list(outs[2:2 + nt]), list(outs[2 + nt:2 + 2 * nt]), outs[2 + 2 * nt]


def _reduce_wait(name, send_sems, recv_sems, cps, slabs, kinds, after):
    nt = len(cps)

    def body(*refs):
        src, dst = refs[:nt], refs[nt:2 * nt]
        send_sems, recv_sems = refs[2 * nt], refs[2 * nt + 1]
        x, y, c = _mesh_pos()
        for t in range(nt):
            for j, chip in enumerate(_other_chips(x, y)):
                sj = 2 * chip[0] + chip[1]
                pltpu.make_async_remote_copy(src_ref=_chip_block(src[t], kinds[t], sj), dst_ref=dst[t].at[sj],
                                             send_sem=send_sems.at[3 * t + j], recv_sem=recv_sems.at[3 * t + j],
                                             device_id=(x, y, c), device_id_type=MESH).wait()

    outs = pl.pallas_call(
        body, name=name, out_shape=[pltpu.HBM(a.shape, a.dtype) for a in cps] + [pltpu.HBM(a.shape, a.dtype) for a in slabs],
        in_specs=[HBM] * (2 * nt) + [SEM, SEM, ANY], out_specs=[HBM] * (2 * nt), input_output_aliases={t: t for t in range(2 * nt)},
        compiler_params=pltpu.CompilerParams(has_side_effects=EFFECT),
    )(*cps, *slabs, send_sems, recv_sems, after)
    return list(outs[:nt]), list(outs[nt:])


def _pair_exchange(dws, kinds):
    nt = len(dws)
    out_shapes = []
    for dw, kind in zip(dws, kinds):
        shp = list(dw.shape)
        shp[-2 if kind == "col" else -1] //= 2
        out_shapes.append(tuple(shp))

    def body(*refs):
        src, dst = refs[:nt], refs[nt:2 * nt]
        send_sems, recv_sems = refs[2 * nt:]
        x, y, c = _mesh_pos()
        copies = [pltpu.make_async_remote_copy(src_ref=_half(src[t], kinds[t], 1 - c), dst_ref=dst[t], send_sem=send_sems.at[t],
                                               recv_sem=recv_sems.at[t], device_id=(x, y, 1 - c), device_id_type=MESH)
                  for t in range(nt)]
        for cp in copies:
            cp.start()
        for cp in copies:
            cp.wait()

    return pl.pallas_call(
        body, name="grad_pair_exchange", out_shape=[_sds(shp, BF16) for shp in out_shapes], in_specs=[HBM] * nt,
        out_specs=[HBM] * nt, scratch_shapes=[pltpu.SemaphoreType.DMA((nt,)), pltpu.SemaphoreType.DMA((nt,))],
    )(*dws)


def _chip_block(ref, kind, s):
    r, cdim = ref.shape[-2:]
    if kind == "col":
        return _idx(ref, cols=pl.ds(s * (cdim // N_CHIP), cdim // N_CHIP))
    return _idx(ref, rows=pl.ds(s * (r // N_CHIP), r // N_CHIP))


def _chip_exchange(cps, kinds):
    nt = len(cps)
    out_shapes = []
    for cp, kind in zip(cps, kinds):
        shp = list(cp.shape)
        shp[-1 if kind == "col" else -2] //= N_CHIP
        out_shapes.append((N_CHIP,) + tuple(shp))

    def body(*refs):
        src, dst = refs[:nt], refs[nt:2 * nt]
        send_sems, recv_sems, local_sems = refs[2 * nt:]
        x, y, c = _mesh_pos()
        s = 2 * x + y
        chips = _other_chips(x, y)
        sends, locals_ = [], []
        for t in range(nt):
            own = pltpu.make_async_copy(_chip_block(src[t], kinds[t], s), dst[t].at[s], local_sems.at[t])
            own.start()
            locals_.append(own)
            for j, chip in enumerate(chips):
                cp = pltpu.make_async_remote_copy(src_ref=_chip_block(src[t], kinds[t], 2 * chip[0] + chip[1]), dst_ref=dst[t].at[s],
                                                  send_sem=send_sems.at[3 * t + j], recv_sem=recv_sems.at[3 * t + j],
                                                  device_id=(*chip, c), device_id_type=MESH)
                cp.start()
                sends.append(cp)
        for t in range(nt):
            for j, chip in enumerate(chips):
                landing = dst[t].at[2 * chip[0] + chip[1]]
                pltpu.make_async_remote_copy(src_ref=landing, dst_ref=landing, send_sem=send_sems.at[3 * t + j],
                                             recv_sem=recv_sems.at[3 * t + j], device_id=(x, y, c), device_id_type=MESH).wait_recv()
        for cp in sends:
            cp.wait_send()
        for own in locals_:
            own.wait()

    return pl.pallas_call(
        body, name="grad_chip_exchange", out_shape=[_sds(shp, BF16) for shp in out_shapes], in_specs=[HBM] * nt,
        out_specs=[HBM] * nt,
        scratch_shapes=[pltpu.SemaphoreType.DMA((3 * nt,)), pltpu.SemaphoreType.DMA((3 * nt,)), pltpu.SemaphoreType.DMA((nt,))],
    )(*cps)


def _pair_assemble(gs, kinds):
    nt = len(gs)

    def body(*refs):
        g = refs[nt:2 * nt]
        send_sems, recv_sems = refs[2 * nt:]
        x, y, c = _mesh_pos()
        copies = []
        for t in range(nt):
            mine = _half(g[t], kinds[t], c)
            cp = pltpu.make_async_remote_copy(src_ref=mine, dst_ref=mine, send_sem=send_sems.at[t], recv_sem=recv_sems.at[t],
                                              device_id=(x, y, 1 - c), device_id_type=MESH)
            cp.start()
            copies.append(cp)
        for t in range(nt):
            landing = _half(g[t], kinds[t], 1 - c)
            pltpu.make_async_remote_copy(src_ref=landing, dst_ref=landing, send_sem=send_sems.at[t], recv_sem=recv_sems.at[t],
                                         device_id=(x, y, c), device_id_type=MESH).wait_recv()
        for cp in copies:
            cp.wait_send()

    return pl.pallas_call(
        body, name="grad_pair_assemble", out_shape=[_sds(a.shape, F32) for a in gs], in_specs=[HBM] * nt,
        out_specs=[HBM] * nt, input_output_aliases={t: t for t in range(nt)},
        scratch_shapes=[pltpu.SemaphoreType.DMA((nt,)), pltpu.SemaphoreType.DMA((nt,))],
    )(*gs)


def _pack(arrays, width):
    flat = jnp.concatenate([a.reshape(-1) for a in arrays])
    pad = (-flat.size) % (8 * width)
    return jnp.pad(flat, (0, pad)).reshape(-1, width)


def _unpack(packed, shapes):
    flat = packed.reshape(-1)
    out, off = [], 0
    for shp in shapes:
        n = 1
        for dim in shp:
            n *= dim
        out.append(flat[off:off + n].reshape(shp))
        off += n
    return out


def kernel(x, c, ada_w, ada_b, norm_mix_g, norm_ffn_g, ab_w_in, a_conv_w, a_conv_b, a_norm_g, a_norm_b, b_norm_g, b_norm_b, b_w_s, b_bias, ab_w_out, pool_w, pool_scale, ffn_w1, ffn_w3, ffn_w2, final_g, loss_target, m_ada_w, m_ada_b, m_norm_mix_g, m_norm_ffn_g, m_ab_w_in, m_a_conv_w, m_a_conv_b, m_a_norm_g, m_a_norm_b, m_b_norm_g, m_b_norm_b, m_b_w_s, m_b_bias, m_ab_w_out, m_pool_w, m_pool_scale, m_ffn_w1, m_ffn_w3, m_ffn_w2, m_final_g, v_ada_w, v_ada_b, v_norm_mix_g, v_norm_ffn_g, v_ab_w_in, v_a_conv_w, v_a_conv_b, v_a_norm_g, v_a_norm_b, v_b_norm_g, v_b_norm_b, v_b_w_s, v_b_bias, v_ab_w_out, v_pool_w, v_pool_scale, v_ffn_w1, v_ffn_w3, v_ffn_w2, v_final_g):
    mx, my, mc = _mesh_pos()
    chip = 2 * mx + my
    dev = 4 * mx + 2 * my + mc
    x2 = x[0]
    target = loss_target[0]
    s, d = x2.shape
    depth = ada_w.shape[0]
    n_mod = ada_b.shape[1] // d
    n_even = ab_w_in.shape[0]
    da, db = a_conv_b.shape[1], b_norm_g.shape[1]
    nh = b_w_s.shape[1]
    kw = a_conv_w.shape[1]
    n_pool = pool_w.shape[1]
    tm_row = _pick(s, (256, 128))

    s_vec = jnp.reshape(chip, (1,)).astype(jnp.int32)
    c_vec = jnp.reshape(mc, (1,)).astype(jnp.int32)
    sc_vec = [s_vec, c_vec] + [jnp.reshape(v, (1,)).astype(jnp.int32)
                               for v in (2 * mx + (1 - my), 2 * (1 - mx) + my, 2 * (1 - mx) + (1 - my))]
    pool_w3 = pool_w.reshape((-1,) + pool_w.shape[2:])
    shard_of = {"w_in": (ab_w_in, "col"), "w_out": (ab_w_out, "row"), "pool": (pool_w3, "row"), "w1": (ffn_w1, "col"),
                "w3": (ffn_w3, "col"), "w2": (ffn_w2, "row")}

    def layer_names(l):
        return (["w_in", "w_out"] if l % 2 == 0 else ["pool"]) + ["w1", "w3", "w2"]

    def layer_span(nm, l):
        if nm in ("w_in", "w_out"):
            return l // 2, 1
        if nm == "pool":
            return (l // 2) * n_pool, n_pool
        return l, 1

    owned, w_kinds, w_shapes = [], [], []
    for l in range(depth):
        names = layer_names(l)
        owned.append([_cast_into_full(shard_of[nm][0], shard_of[nm][1], s_vec, *layer_span(nm, l)) for nm in names])
        w_kinds.append([shard_of[nm][1] for nm in names])
        w_shapes.append([(layer_span(nm, l)[1],) + shard_of[nm][0].shape[1:] for nm in names])
    layer_w = [None] * depth
    layer_w[0] = dict(zip(layer_names(0), _allgather_weights(owned[0], w_kinds[0], w_shapes[0])))
    gather_prev = layer_w[0]["w2"]

    pre = _allgather_small(_pack([c, a_conv_w, pool_scale], 128)).reshape(N_DEV, -1)
    n_cw, n_ps = a_conv_w.size, pool_scale.size
    c_all = pre[:, :d]
    cw_chips = pre[0::2, d:d + n_cw].reshape((N_CHIP,) + a_conv_w.shape)
    conv_w_full = jnp.concatenate([cw_chips[k] for k in range(N_CHIP)], axis=-1)
    ps_chips = pre[0::2, d + n_cw:d + n_cw + n_ps].reshape((N_CHIP,) + pool_scale.shape)
    pool_scale_full = jnp.concatenate([ps_chips[k] for k in range(N_CHIP)], axis=-1)
    c_pad = jnp.pad(c_all, ((0, 8), (0, 0)))
    n_ada = ada_w.shape[2]
    ada_b_mine = lax.dynamic_slice_in_dim(ada_b, chip * n_ada, n_ada, axis=1)[:, None, :]
    mod_part = _ada_fwd(c_pad, ada_w, ada_b_mine)[:, :N_DEV, :]
    mod_all = _allgather_small(mod_part.reshape(depth * N_DEV, n_ada))
    mod_chips = mod_all[0::2].reshape(N_CHIP, depth, N_DEV, n_ada)
    mod_mine = lax.dynamic_index_in_dim(mod_chips, dev, axis=2, keepdims=False)
    mod = jnp.transpose(mod_mine, (1, 0, 2)).reshape(depth, n_mod, 1, d)

    causal = jnp.tril(jnp.ones((CHUNK, CHUNK), dtype=bool))
    w_c = jnp.where(causal[None, None], b_w_s, 0.0).astype(BF16)
    bias_full = jnp.repeat(jnp.swapaxes(b_bias, 1, 2), CHUNK, axis=2)

    saved = []
    xs = x2
    for l in range(depth):
        sh1, sc1, g1, sh2, sc2, g2 = [mod[l, k] for k in range(n_mod)]
        i = l // 2
        st = {"x1": xs}
        gain1 = norm_mix_g[l][None]
        if l + 1 < depth:
            in_flight = _gather_start(f"gather_start_{l + 1}", owned[l + 1], w_kinds[l + 1], w_shapes[l + 1], gather_prev)
            gain1 = gain1 + in_flight[3][0:1, 0:1]
        wl = layer_w[l]
        if l % 2 == 0:
            h = _norm_mod(xs, gain1, sh1, sc1, tm_row)
            proj = _mm("ab_proj", [h], [(wl["w_in"], 0)], [BF16], _ep_store, tk=2048)[0]
            cat, a1 = _ab_mid_fwd(proj, conv_w_full[i], a_conv_b[i][None], a_norm_g[i][None], a_norm_b[i][None],
                                  b_norm_g[i][None], b_norm_b[i][None], w_c[i], bias_full[i], tm_row)
            xs, y1 = _mm("ab_out", [cat], [(wl["w_out"], 0)], [F32, BF16], _ep_residual, extras=[xs, g1], extra_kinds=["tile", "row"], tk=2048)
            st.update(h=h, proj=proj, a1=a1, cat=cat, y=y1)
        else:
            p = _pool_fwd(xs, gain1, sh1, sc1, tm_row)
            gate = g1 * pool_scale_full[i][None]
            xs, ymm = _grouped_fwd(p, wl["pool"][None], 0, xs, gate, 1024)
            st.update(p=p, y=ymm, gate=gate)
        st["x2"] = xs
        h2 = _norm_mod(xs, norm_ffn_g[l][None], sh2, sc2, tm_row)
        u, t, z = _mm("ffn_up", [h2, h2], [(wl["w1"], 0), (wl["w3"], 0)], [BF16, BF16, BF16], _ep_swiglu, tn=512, tk=2048)
        xs, y2 = _mm("ffn_down", [z], [(wl["w2"], 0)], [F32, BF16], _ep_residual, extras=[xs, g2], extra_kinds=["tile", "row"],
                     tn=256, tk=ffn_w2.shape[1] * N_CHIP)
        if l + 1 < depth:
            landed = _gather_wait(f"gather_wait_{l + 1}", in_flight[0], in_flight[1], in_flight[2], w_kinds[l + 1], w_shapes[l + 1], xs)
            layer_w[l + 1] = dict(zip(layer_names(l + 1), landed))
            gather_prev = landed[-1]
        st.update(h2=h2, u=u, t=t, z=z, y2=y2)
        saved.append(st)

    dx, fin_acc, loss_blk = _final_loss_bwd(xs, final_g[None], target, tm_row)
    loss = lax.psum(loss_blk[0, 0], MESH_AXES)
    d_final_g = fin_acc[0]
    dmod_rows = [None] * depth
    d_norm_mix, d_norm_ffn = [None] * depth, [None] * depth
    even_small = [None] * n_even
    d_pool_scale = [None] * (depth // 2)
    grad_names = ["w_in", "w_out", "pool", "w1", "w3", "w2"]
    g_shapes = {"w_in": ab_w_in.shape, "w_out": ab_w_out.shape, "pool": (pool_w.shape[0], n_pool * pool_w.shape[2], pool_w.shape[3]),
                "w1": ffn_w1.shape, "w3": ffn_w3.shape, "w2": ffn_w2.shape}
    shard_grads = {nm: None for nm in grad_names}
    reduce_prev = dx
    pending = None

    def finish_reduce(pend, after):
        pl_, names_, kinds_, sends, recvs, cps_f, slabs_f = pend
        cps_d, slabs_d = _reduce_wait(f"reduce_wait_{pl_}", sends, recvs, cps_f, slabs_f, kinds_, after)
        for nm, kind, cp, sl in zip(names_, kinds_, cps_d, slabs_d):
            rpg = pool_w.shape[2] if nm == "pool" else None
            cp2 = cp.reshape(-1, cp.shape[-1])
            shard_grads[nm] = _chip_sum_into(sl.reshape(N_CHIP, -1, sl.shape[-1]), cp2, shard_grads[nm], g_shapes[nm],
                                             layer_span(nm, pl_)[0] // (n_pool if nm == "pool" else 1), kind, sc_vec, rpg)
        return slabs_d[-1]

    for l in reversed(range(depth)):
        sh1, sc1, g1, sh2, sc2, g2 = [mod[l, k] for k in range(n_mod)]
        st = saved[l]
        wl = layer_w[l]
        i = l // 2
        big = {}
        gate2 = g2 if pending is None else g2 + pending_token[0:1, 0:1]
        dyb, gacc = _gate_bwd(dx, st["y2"], gate2, tm_row)
        d_g2 = gacc[0]
        du, dt = _mm("ffn_dz", [dyb], [(wl["w2"], 0)], [BF16, BF16], _ep_swiglu_bwd, trans_b=True, extras=[st["u"], st["t"]],
                     extra_kinds=["tile", "tile"], tn=512, tk=2048)
        big["w2"] = _mm("ffn_dw2", [st["z"]], [dyb], [BF16], _ep_store, trans_a=True, tm=512, tn=512, tk=s)[0]
        big["w1"] = _mm("ffn_dw1", [st["h2"]], [du], [BF16], _ep_store, trans_a=True, tm=512, tn=512, tk=s)[0]
        big["w3"] = _mm("ffn_dw3", [st["h2"]], [dt], [BF16], _ep_store, trans_a=True, tm=512, tn=512, tk=s)[0]
        dh = _mm("ffn_dh", [du, dt], [(wl["w1"], 0), (wl["w3"], 0)], [F32], _ep_sum, trans_b=True, tn=512, tk=ffn_w2.shape[1] * 2)[0]
        dx, nacc = _norm_mod_bwd(dh, st["x2"], dx, norm_ffn_g[l][None], sc2, tm_row)
        d_sh2, d_sc2, d_norm_ffn[l] = nacc[0], nacc[1], nacc[2]
        if l % 2 == 0:
            dyb, gacc = _gate_bwd(dx, st["y"], g1, tm_row)
            d_g1 = gacc[0]
            big["w_out"] = _mm("ab_dw_out", [st["cat"]], [dyb], [BF16], _ep_store, trans_a=True, tm=512, tn=512, tk=s)[0]
            dcat = _mm("ab_dcat", [dyb], [(wl["w_out"], 0)], [F32], _ep_store, trans_b=True, tk=2048)[0]
            dproj, dcw, vecs, dws, dbias = _ab_mid_bwd(dcat, st["proj"], st["a1"], conv_w_full[i], a_norm_g[i][None],
                                                       a_norm_b[i][None], b_norm_g[i][None], b_norm_b[i][None], w_c[i],
                                                       bias_full[i], tm_row)
            even_small[i] = dict(conv_w=dcw, a_norm_g=vecs[0, :da], a_norm_b=vecs[1, :da], conv_b=vecs[2, :da],
                                 b_norm_g=vecs[3, :db], b_norm_b=vecs[4, :db], w_s=dws, bias=dbias[:, :, 0])
            big["w_in"] = _mm("ab_dw_in", [st["h"]], [dproj], [BF16], _ep_store, trans_a=True, tm=512, tn=512, tk=s)[0]
            dh = _mm("ab_dh", [dproj], [(wl["w_in"], 0)], [F32], _ep_store, trans_b=True, tn=512, tk=2 * da + 2 * db)[0]
            dx, nacc = _norm_mod_bwd(dh, st["x1"], dx, norm_mix_g[l][None], sc1, tm_row)
        else:
            dyb, gacc = _gate_bwd(dx, st["y"], st["gate"], tm_row)
            d_g1 = gacc[0] * pool_scale_full[i]
            d_pool_scale[i] = gacc[0] * g1[0]
            big["pool"] = _grouped_dw(st["p"], dyb, n_pool, 1024)
            dp = _grouped_dx(dyb, wl["pool"][None], 0, 1024)
            dx, nacc = _pool_bwd(dp, st["x1"], dx, norm_mix_g[l][None], sc1, tm_row)
        d_sh1, d_sc1, d_norm_mix[l] = nacc[0], nacc[1], nacc[2]
        dmod_rows[l] = jnp.concatenate([d_sh1, d_sc1, d_g1, d_sh2, d_sc2, d_g2])
        if pending is not None:
            reduce_prev = finish_reduce(pending, dx)
        names = layer_names(l)
        kinds_l = [shard_of[nm][1] for nm in names]
        dws = [big[nm] for nm in names]
        sibs = _pair_exchange(dws, kinds_l)
        cps = []
        for dw, sib, kind in zip(dws, sibs, kinds_l):
            cp = _pair_add(dw.reshape(-1, dw.shape[-1]), sib.reshape(-1, sib.shape[-1]), kind, c_vec)
            cps.append(cp.reshape(sib.shape))
        sends, recvs, cps_f, slabs_f, pending_token = _reduce_start(f"reduce_start_{l}", cps, kinds_l, reduce_prev)
        pending = (l, names, kinds_l, sends, recvs, cps_f, slabs_f)
    finish_reduce(pending, dx)
    grad_x = dx[None]

    dmod = jnp.stack(dmod_rows)
    small = [dmod, jnp.stack(d_norm_mix), jnp.stack(d_norm_ffn),
             jnp.stack([e["conv_w"] for e in even_small]), jnp.stack([e["conv_b"] for e in even_small]),
             jnp.stack([e["a_norm_g"] for e in even_small]), jnp.stack([e["a_norm_b"] for e in even_small]),
             jnp.stack([e["b_norm_g"] for e in even_small]), jnp.stack([e["b_norm_b"] for e in even_small]),
             jnp.stack([e["w_s"] for e in even_small]), jnp.stack([e["bias"] for e in even_small]),
             jnp.stack(d_pool_scale), d_final_g]
    small_shapes = [a.shape for a in small]
    width = 1024 if d >= 1024 else 128
    gathered = _allgather_small(_pack(small, width))
    summed = _unpack(_sum_leading(gathered), small_shapes)
    (g_ada_b, g_norm_mix, g_norm_ffn, g_conv_w_full, g_conv_b, g_a_norm_g, g_a_norm_b, g_b_norm_g, g_b_norm_b, g_w_s, g_bias,
     g_pool_scale_full, g_final_g) = summed
    cw_cols = a_conv_w.shape[2]
    g_conv_w = lax.dynamic_slice_in_dim(g_conv_w_full, chip * cw_cols, cw_cols, axis=2)
    ps_cols = pool_scale.shape[1]
    g_pool_scale = lax.dynamic_slice_in_dim(g_pool_scale_full, chip * ps_cols, ps_cols, axis=1)

    dmod_all = gathered.reshape(N_DEV, -1)[:, :dmod.size].reshape(N_DEV, depth, n_mod * d)
    dmod_cols = lax.dynamic_slice_in_dim(dmod_all, chip * n_ada, n_ada, axis=2)
    dmod_cols = jnp.pad(jnp.transpose(dmod_cols, (1, 0, 2)), ((0, 0), (0, 8), (0, 0)))
    g_ada_w = _ada_bwd(c_pad, dmod_cols)

    g_w_in, g_w_out, g_pool, g_w1, g_w3, g_w2 = _pair_assemble([shard_grads[nm] for nm in grad_names],
                                                              [shard_of[nm][1] for nm in grad_names])

    grads = [g_ada_w, g_ada_b, g_norm_mix, g_norm_ffn, g_w_in, g_conv_w, g_conv_b, g_a_norm_g, g_a_norm_b, g_b_norm_g,
             g_b_norm_b, g_w_s, g_bias, g_w_out, g_pool, g_pool_scale, g_w1, g_w3, g_w2, g_final_g]
    weights = [ada_w, ada_b, norm_mix_g, norm_ffn_g, ab_w_in, a_conv_w, a_conv_b, a_norm_g, a_norm_b, b_norm_g, b_norm_b, b_w_s,
               b_bias, ab_w_out, pool_w, pool_scale, ffn_w1, ffn_w3, ffn_w2, final_g]
    ms = [m_ada_w, m_ada_b, m_norm_mix_g, m_norm_ffn_g, m_ab_w_in, m_a_conv_w, m_a_conv_b, m_a_norm_g, m_a_norm_b, m_b_norm_g,
          m_b_norm_b, m_b_w_s, m_b_bias, m_ab_w_out, m_pool_w, m_pool_scale, m_ffn_w1, m_ffn_w3, m_ffn_w2, m_final_g]
    vs = [v_ada_w, v_ada_b, v_norm_mix_g, v_norm_ffn_g, v_ab_w_in, v_a_conv_w, v_a_conv_b, v_a_norm_g, v_a_norm_b, v_b_norm_g,
          v_b_norm_b, v_b_w_s, v_b_bias, v_ab_w_out, v_pool_w, v_pool_scale, v_ffn_w1, v_ffn_w3, v_ffn_w2, v_final_g]
    deltas, new_ms, new_vs = [], [], []
    for w, g, m, v in zip(weights, grads, ms, vs):
        g = g.reshape(w.shape)
        dl, nm_, nv_ = _adamw(w, g, m, v)
        deltas.append(dl)
        new_ms.append(nm_)
        new_vs.append(nv_)
    grads = [g.reshape(w.shape) for g, w in zip(grads, weights)]
    return (loss, grad_x, *grads, *deltas, *new_ms, *new_vs)
```

```python
import functools

import jax
import jax.numpy as jnp
from jax import lax
from jax.experimental import pallas as pl
from jax.experimental.pallas import tpu as pltpu

F32 = jnp.float32
BF16 = jnp.bfloat16
EPS = 1e-6
N_DEV = 8
N_CHIP = 4
MESH_AXES = ("x", "y", "c")
MESH = pl.DeviceIdType.MESH
V7X_VMEM_LIMIT_BYTES = 56 * 1024 * 1024
SUBLANES = 8
CONV_HALO = 32
POOL_HALO = 16
POOL_WINDOWS = (2, 4, 8, 16)
CHUNK = 128
ADAM_LR, ADAM_B1, ADAM_B2, ADAM_EPS, ADAM_WD, ADAM_STEP = 0.001, 0.9, 0.999, 1e-08, 0.01, 10
HBM = pl.BlockSpec(memory_space=pltpu.HBM)


def _params(sem=None):
    return pltpu.CompilerParams(dimension_semantics=sem, vmem_limit_bytes=V7X_VMEM_LIMIT_BYTES)


def _pick(n, prefs):
    for p in prefs:
        if p <= n and n % p == 0:
            return p
    return n


def _row_tile(rows, cols, target):
    if rows * cols <= target:
        return rows
    best = None
    for d in range(16, rows, 16):
        if rows % d == 0 and d * cols <= target:
            best = d
    return best if best is not None else rows


def _sds(shape, dtype):
    return jax.ShapeDtypeStruct(tuple(shape), dtype)


def _mm_core(name, grid, nk, a_list, b_list, a_spec, b_spec, dn, extras, extra_specs, out_shapes, out_specs,
             acc_shape, epilogue):
    n, n_ex, n_out = len(a_list), len(extras), len(out_shapes)

    def body(*refs):
        a_refs, b_refs = refs[:n], refs[n:2 * n]
        ex = refs[2 * n:2 * n + n_ex]
        outs = refs[2 * n + n_ex:2 * n + n_ex + n_out]
        accs = refs[2 * n + n_ex + n_out:]
        ps = [lax.dot_general(a[...], b[...], dn, preferred_element_type=F32) for a, b in zip(a_refs, b_refs)]
        if nk == 1:
            epilogue(ps, ex, outs)
            return
        k = pl.program_id(2)

        @pl.when(k == 0)
        def _():
            for acc, p in zip(accs, ps):
                acc[...] = p

        @pl.when(k > 0)
        def _():
            for acc, p in zip(accs, ps):
                acc[...] += p

        @pl.when(k == nk - 1)
        def _():
            epilogue([acc[...] for acc in accs], ex, outs)

    scratch = [] if nk == 1 else [pltpu.VMEM(acc_shape, F32) for _ in range(n)]
    return pl.pallas_call(
        body, grid=grid, name=name,
        in_specs=[a_spec] * n + [b_spec] * n + list(extra_specs),
        out_specs=list(out_specs), out_shape=list(out_shapes), scratch_shapes=scratch,
        compiler_params=_params(("parallel", "parallel", "arbitrary")),
    )(*a_list, *b_list, *extras)


def _mm(name, a_list, b_list, out_dtypes, epilogue, *, trans_a=False, trans_b=False, extras=(), extra_kinds=(),
        tm=1024, tn=1024, tk=1024):
    layer = None
    if isinstance(b_list[0], tuple):
        layer = b_list[0][1]
        b_list = [b for b, _ in b_list]
    a0 = a_list[0]
    b_shape = b_list[0].shape[-2:]
    m, kk = (a0.shape[1], a0.shape[0]) if trans_a else a0.shape
    nn = b_shape[0] if trans_b else b_shape[1]
    tm, tn, tk = _pick(m, (tm, 512, 256, 128)), _pick(nn, (tn, 512, 256, 128)), _pick(kk, (tk, 512, 256, 128))
    nk = kk // tk
    a_spec = pl.BlockSpec((tk, tm), lambda i, j, k: (k, i)) if trans_a else pl.BlockSpec((tm, tk), lambda i, j, k: (i, k))
    if layer is None:
        b_spec = pl.BlockSpec((tn, tk), lambda i, j, k: (j, k)) if trans_b else pl.BlockSpec((tk, tn), lambda i, j, k: (k, j))
    elif trans_b:
        b_spec = pl.BlockSpec((None, tn, tk), lambda i, j, k: (layer, j, k))
    else:
        b_spec = pl.BlockSpec((None, tk, tn), lambda i, j, k: (layer, k, j))
    dn = (((0 if trans_a else 1,), (1 if trans_b else 0,)), ((), ()))
    tile = pl.BlockSpec((tm, tn), lambda i, j, k: (i, j))
    row = pl.BlockSpec((1, tn), lambda i, j, k: (0, j))
    return _mm_core(name, (m // tm, nn // tn, nk), nk, a_list, b_list, a_spec, b_spec, dn, extras,
                    [tile if kd == "tile" else row for kd in extra_kinds],
                    [_sds((m, nn), dt) for dt in out_dtypes], [tile] * len(out_dtypes), (tm, tn), epilogue)


def _ep_store(ps, ex, outs):
    outs[0][...] = ps[0].astype(outs[0].dtype)


def _ep_sum(ps, ex, outs):
    outs[0][...] = (ps[0] + ps[1]).astype(outs[0].dtype)


def _ep_swiglu(ps, ex, outs):
    u, t = ps
    sg = jax.nn.sigmoid(u)
    su = u * sg
    outs[0][...] = (t * (sg * (1.0 + u * (1.0 - sg)))).astype(BF16)
    outs[1][...] = su.astype(BF16)
    outs[2][...] = (su * t).astype(BF16)


def _ep_residual(ps, ex, outs):
    x_ref, gate_ref = ex
    y = ps[0]
    outs[0][...] = x_ref[...] + gate_ref[...] * y
    outs[1][...] = y.astype(BF16)


def _ep_swiglu_bwd(ps, ex, outs):
    dz = ps[0]
    outs[0][...] = (dz * ex[0][...].astype(F32)).astype(BF16)
    outs[1][...] = (dz * ex[1][...].astype(F32)).astype(BF16)


def _grouped_fwd(p, w, layer, x, gate, tm):
    s, d = p.shape
    _, g, kg, ng = w.shape
    tm = _pick(s, (tm, 512, 256, 128))
    tile_a = pl.BlockSpec((tm, kg), lambda i, j, k: (i, j))
    tile_o = pl.BlockSpec((tm, ng), lambda i, j, k: (i, j))
    return _mm_core("pool_mm_fwd", (s // tm, g, 1), 1, [p], [w], tile_a,
                    pl.BlockSpec((None, None, kg, ng), lambda i, j, k: (layer, j, 0, 0)), (((1,), (0,)), ((), ())),
                    [x, gate], [tile_o, pl.BlockSpec((1, ng), lambda i, j, k: (0, j))],
                    [_sds((s, g * ng), F32), _sds((s, g * ng), BF16)], [tile_o, tile_o], None, _ep_residual)


def _grouped_dx(dy, w, layer, tm):
    s, _ = dy.shape
    _, g, kg, ng = w.shape
    tm = _pick(s, (tm, 512, 256, 128))
    return _mm_core("pool_mm_dx", (s // tm, g, 1), 1, [dy], [w], pl.BlockSpec((tm, ng), lambda i, j, k: (i, j)),
                    pl.BlockSpec((None, None, kg, ng), lambda i, j, k: (layer, j, 0, 0)), (((1,), (1,)), ((), ())), [], [],
                    [_sds((s, g * kg), F32)], [pl.BlockSpec((tm, kg), lambda i, j, k: (i, j))], None, _ep_store)[0]


def _grouped_dw(p, dy, groups, tk):
    s, d = p.shape
    kg = d // groups
    ng = dy.shape[1] // groups
    tk = _pick(s, (tk, 512, 256, 128))
    nk = s // tk

    def ep(ps, ex, outs):
        outs[0][...] = ps[0].astype(BF16)

    return _mm_core("pool_mm_dw", (groups, 1, nk), nk, [p], [dy], pl.BlockSpec((tk, kg), lambda i, j, k: (k, i)),
                    pl.BlockSpec((tk, ng), lambda i, j, k: (k, i)), (((0,), (0,)), ((), ())), [], [],
                    [_sds((groups, kg, ng), BF16)], [pl.BlockSpec((None, kg, ng), lambda i, j, k: (i, 0, 0))],
                    (kg, ng), ep)[0]


def _rms_rstd(xv):
    return lax.rsqrt(jnp.mean(xv * xv, axis=-1, keepdims=True) + EPS)


def _norm_mod_math(xv, g, sh, sc):
    return ((xv * _rms_rstd(xv)) * g) * (1.0 + sc) + sh


def _norm_mod_bwd_math(dh, xv, dxo, g, sc, acc_ref):
    r = _rms_rstd(xv)
    xhat = xv * r
    acc_ref[0:1, :] += jnp.sum(dh, axis=0, keepdims=True)
    acc_ref[1:2, :] += jnp.sum(dh * (xhat * g), axis=0, keepdims=True)
    dhn = dh * (1.0 + sc)
    acc_ref[2:3, :] += jnp.sum(dhn * xhat, axis=0, keepdims=True)
    dxh = dhn * g
    return dxo + r * (dxh - xhat * jnp.mean(dxh * xhat, axis=-1, keepdims=True))


def _vec_spec(d):
    return pl.BlockSpec((1, d), lambda i: (0, 0))


def _acc_spec(d):
    return pl.BlockSpec((8, d), lambda i: (0, 0))


def _norm_mod(x, g, sh, sc, tm):
    s, d = x.shape

    def body(x_ref, g_ref, sh_ref, sc_ref, h_ref):
        h_ref[...] = _norm_mod_math(x_ref[...], g_ref[...], sh_ref[...], sc_ref[...]).astype(BF16)

    row = pl.BlockSpec((tm, d), lambda i: (i, 0))
    return pl.pallas_call(body, grid=(s // tm,), name="norm_mod", in_specs=[row] + [_vec_spec(d)] * 3, out_specs=row,
                          out_shape=_sds((s, d), BF16), compiler_params=_params(("parallel",)))(x, g, sh, sc)


def _norm_mod_bwd(dh, x, dxo, g, sc, tm):
    s, d = x.shape

    def body(dh_ref, x_ref, dxo_ref, g_ref, sc_ref, dx_ref, acc_ref):
        @pl.when(pl.program_id(0) == 0)
        def _():
            acc_ref[...] = jnp.zeros_like(acc_ref)

        dx_ref[...] = _norm_mod_bwd_math(dh_ref[...], x_ref[...], dxo_ref[...], g_ref[...], sc_ref[...], acc_ref)

    row = pl.BlockSpec((tm, d), lambda i: (i, 0))
    return pl.pallas_call(body, grid=(s // tm,), name="norm_mod_bwd", in_specs=[row, row, row, _vec_spec(d), _vec_spec(d)],
                          out_specs=[row, _acc_spec(d)], out_shape=[_sds((s, d), F32), _sds((8, d), F32)],
                          compiler_params=_params(("arbitrary",)))(dh, x, dxo, g, sc)


def _gate_bwd(dx, y, gate, tm):
    s, d = dx.shape

    def body(dx_ref, y_ref, gate_ref, dy_ref, acc_ref):
        @pl.when(pl.program_id(0) == 0)
        def _():
            acc_ref[...] = jnp.zeros_like(acc_ref)

        dxv = dx_ref[...]
        dy_ref[...] = (dxv * gate_ref[...]).astype(BF16)
        acc_ref[0:1, :] += jnp.sum(dxv * y_ref[...].astype(F32), axis=0, keepdims=True)

    row = pl.BlockSpec((tm, d), lambda i: (i, 0))
    return pl.pallas_call(body, grid=(s // tm,), name="gate_bwd", in_specs=[row, row, _vec_spec(d)],
                          out_specs=[row, _acc_spec(d)], out_shape=[_sds((s, d), BF16), _sds((8, d), F32)],
                          compiler_params=_params(("arbitrary",)))(dx, y, gate)


def _final_loss_bwd(x, g, target, tm):
    s, d = x.shape

    def body(x_ref, g_ref, t_ref, dx_ref, acc_ref, loss_ref):
        @pl.when(pl.program_id(0) == 0)
        def _():
            acc_ref[...] = jnp.zeros_like(acc_ref)
            loss_ref[...] = jnp.zeros_like(loss_ref)

        xv = x_ref[...]
        gv = g_ref[...]
        r = _rms_rstd(xv)
        xhat = xv * r
        err = xhat * gv - t_ref[...]
        loss_ref[...] += (0.5 / d) * jnp.sum(err * err)
        dy = err * (1.0 / d)
        acc_ref[0:1, :] += jnp.sum(dy * xhat, axis=0, keepdims=True)
        dxh = dy * gv
        dx_ref[...] = r * (dxh - xhat * jnp.mean(dxh * xhat, axis=-1, keepdims=True))

    row = pl.BlockSpec((tm, d), lambda i: (i, 0))
    return pl.pallas_call(body, grid=(s // tm,), name="final_loss_bwd", in_specs=[row, _vec_spec(d), row],
                          out_specs=[row, _acc_spec(d), pl.BlockSpec((8, 128), lambda i: (0, 0))],
                          out_shape=[_sds((s, d), F32), _sds((8, d), F32), _sds((8, 128), F32)],
                          compiler_params=_params(("arbitrary",)))(x, g, target)


def _chunks(tm, width, rb, cb):
    rb, cb = min(rb, tm), min(cb, width)
    return [(r0, c0, rb, cb) for r0 in range(0, tm, rb) for c0 in range(0, width, cb)]


def _prev_halo_map(tm, halo):
    return lambda i: (jnp.maximum(i * (tm // halo) - 1, 0), 0)


def _next_halo_map(tm, halo, s):
    return lambda i: (jnp.minimum((i + 1) * (tm // halo), s // halo - 1), 0)


def _pool_fwd(x, g, sh, sc, tm):
    s, d = x.shape
    dg = d // len(POOL_WINDOWS)

    def body(x_ref, xh_ref, g_ref, sh_ref, sc_ref, p_ref, ext_ref):
        i = pl.program_id(0)
        gv, shv, scv = g_ref[...], sh_ref[...], sc_ref[...]
        ext_ref[POOL_HALO:, :] = _norm_mod_math(x_ref[...], gv, shv, scv)
        ext_ref[0:POOL_HALO, :] = jnp.where(i == 0, 0.0, _norm_mod_math(xh_ref[...], gv, shv, scv))
        for gi, w in enumerate(POOL_WINDOWS):
            for r0, c0, rb, cb in _chunks(tm, dg, 64, 256):
                cols = pl.ds(gi * dg + c0, cb)
                tok = ext_ref[pl.ds(POOL_HALO + r0, rb), cols]
                acc = tok
                for j in range(1, w):
                    acc = acc + ext_ref[pl.ds(POOL_HALO + r0 - j, rb), cols]
                t_glob = i * tm + r0 + lax.broadcasted_iota(jnp.int32, (rb, 1), 0)
                cnt = jnp.minimum(t_glob + 1, w).astype(F32)
                p_ref[pl.ds(r0, rb), cols] = (acc / cnt - tok).astype(BF16)

    row = pl.BlockSpec((tm, d), lambda i: (i, 0))
    halo = pl.BlockSpec((POOL_HALO, d), _prev_halo_map(tm, POOL_HALO))
    return pl.pallas_call(body, grid=(s // tm,), name="pool_fwd", in_specs=[row, halo] + [_vec_spec(d)] * 3, out_specs=row,
                          out_shape=_sds((s, d), BF16), scratch_shapes=[pltpu.VMEM((tm + POOL_HALO, d), F32)],
                          compiler_params=_params(("parallel",)))(x, x, g, sh, sc)


def _pool_bwd(dp, x, dxo, g, sc, tm):
    s, d = x.shape
    dg = d // len(POOL_WINDOWS)
    n_tiles = s // tm

    def body(dp_ref, dph_ref, x_ref, dxo_ref, g_ref, sc_ref, dx_ref, acc_ref, ext_ref, dh_ref):
        i = pl.program_id(0)

        @pl.when(i == 0)
        def _():
            acc_ref[...] = jnp.zeros_like(acc_ref)

        for gi, w in enumerate(POOL_WINDOWS):
            cols = pl.ds(gi * dg, dg)
            t_main = i * tm + lax.broadcasted_iota(jnp.int32, (tm, 1), 0)
            ext_ref[0:tm, cols] = dp_ref[:, cols] / jnp.minimum(t_main + 1, w).astype(F32)
            ext_ref[tm:, cols] = jnp.where(i == n_tiles - 1, 0.0, dph_ref[:, cols] * (1.0 / w))
            for r0, c0, rb, cb in _chunks(tm, dg, 64, 256):
                cc = pl.ds(gi * dg + c0, cb)
                acc = ext_ref[pl.ds(r0, rb), cc]
                for j in range(1, w):
                    acc = acc + ext_ref[pl.ds(r0 + j, rb), cc]
                dh_ref[pl.ds(r0, rb), cc] = acc - dp_ref[pl.ds(r0, rb), cc]
        dx_ref[...] = _norm_mod_bwd_math(dh_ref[...], x_ref[...], dxo_ref[...], g_ref[...], sc_ref[...], acc_ref)

    row = pl.BlockSpec((tm, d), lambda i: (i, 0))
    halo = pl.BlockSpec((POOL_HALO, d), _next_halo_map(tm, POOL_HALO, s))
    return pl.pallas_call(body, grid=(n_tiles,), name="pool_bwd", in_specs=[row, halo, row, row, _vec_spec(d), _vec_spec(d)],
                          out_specs=[row, _acc_spec(d)], out_shape=[_sds((s, d), F32), _sds((8, d), F32)],
                          scratch_shapes=[pltpu.VMEM((tm + POOL_HALO, d), F32), pltpu.VMEM((tm, d), F32)],
                          compiler_params=_params(("arbitrary",)))(dp, dp, x, dxo, g, sc)


def _layernorm_fwd(v, g, b):
    mu = jnp.mean(v, axis=-1, keepdims=True)
    xc = v - mu
    rstd = lax.rsqrt(jnp.mean(xc * xc, axis=-1, keepdims=True) + EPS)
    yn = xc * rstd
    return yn * g + b, yn, rstd


def _layernorm_bwd(dz, yn, rstd, g):
    dyn = dz * g
    dv = rstd * (dyn - jnp.mean(dyn, axis=-1, keepdims=True) - yn * jnp.mean(dyn * yn, axis=-1, keepdims=True))
    return dv, jnp.sum(dz * yn, axis=0, keepdims=True), jnp.sum(dz, axis=0, keepdims=True)


def _fill_shifted(ext_ref, sh_ref):
    n = ext_ref.shape[0]
    for r in range(1, SUBLANES):
        sh_ref[r - 1, 0:n - SUBLANES, :] = ext_ref[pl.ds(r, n - SUBLANES), :]


def _shifted(ext_ref, sh_ref, start, rows, cols):
    q, r = divmod(start, SUBLANES)
    if r == 0:
        return ext_ref[pl.ds(start, rows), cols]
    return sh_ref[r - 1, pl.ds(q * SUBLANES, rows), cols]


def _ab_mid_fwd(proj, conv_w, conv_b, a_g, a_b, v_g, v_b, w_c, bias_full, tm):
    s = proj.shape[0]
    da = conv_b.shape[1]
    db = v_g.shape[1]
    nh = w_c.shape[0]
    kw = conv_w.shape[0]
    lead = CONV_HALO - (kw - 1)

    def body(p_ref, ph_ref, cw_ref, cb_ref, ag_ref, ab_ref, vg_ref, vb_ref, wc_ref, bias_ref, cat_ref, a1_ref, ext_ref,
             a1s_ref, sh_ref):
        i = pl.program_id(0)
        val = p_ref[:, 0:da].astype(F32)
        gat = p_ref[:, da:2 * da].astype(F32)
        ext_ref[CONV_HALO:, :] = val * jax.nn.sigmoid(gat)
        hv = ph_ref[:, 0:da].astype(F32)
        hg = ph_ref[:, da:2 * da].astype(F32)
        ext_ref[0:CONV_HALO, :] = jnp.where(i == 0, 0.0, hv * jax.nn.sigmoid(hg))
        _fill_shifted(ext_ref, sh_ref)
        for r0, c0, rb, cb in _chunks(tm, da, 64, 256):
            cols = pl.ds(c0, cb)
            acc = jnp.broadcast_to(cb_ref[:, cols], (rb, cb))
            for k in range(kw):
                acc = acc + cw_ref[k:k + 1, cols] * _shifted(ext_ref, sh_ref, r0 + lead + k, rb, cols)
            a1s_ref[pl.ds(r0, rb), cols] = acc
        a1 = a1s_ref[...]
        a1_ref[...] = a1.astype(BF16)
        z, _, _ = _layernorm_fwd(a1, ag_ref[...], ab_ref[...])
        cat_ref[:, 0:da] = (z * jax.nn.sigmoid(z)).astype(BF16)

        bu = p_ref[:, 2 * da:2 * da + db].astype(F32)
        bv = p_ref[:, 2 * da + db:].astype(F32)
        vn, _, _ = _layernorm_fwd(bv, vg_ref[...], vb_ref[...])
        vnb = vn.astype(BF16)
        for n in range(tm // CHUNK):
            rows = slice(n * CHUNK, (n + 1) * CHUNK)
            for h in range(nh):
                hc = slice(h * CHUNK, (h + 1) * CHUNK)
                vo = jnp.dot(wc_ref[h], vnb[rows, hc], preferred_element_type=F32) + bias_ref[:, hc]
                cat_ref[rows, da + h * CHUNK:da + (h + 1) * CHUNK] = (bu[rows, hc] * vo).astype(BF16)

    full = lambda a: pl.BlockSpec(a.shape, lambda i: (0,) * a.ndim)
    return pl.pallas_call(
        body, grid=(s // tm,), name="ab_mid_fwd",
        in_specs=[pl.BlockSpec((tm, 2 * da + 2 * db), lambda i: (i, 0)),
                  pl.BlockSpec((CONV_HALO, 2 * da), _prev_halo_map(tm, CONV_HALO)),
                  full(conv_w), full(conv_b), full(a_g), full(a_b), full(v_g), full(v_b), full(w_c), full(bias_full)],
        out_specs=[pl.BlockSpec((tm, da + db), lambda i: (i, 0)), pl.BlockSpec((tm, da), lambda i: (i, 0))],
        out_shape=[_sds((s, da + db), BF16), _sds((s, da), BF16)],
        scratch_shapes=[pltpu.VMEM((tm + CONV_HALO, da), F32), pltpu.VMEM((tm, da), F32),
                        pltpu.VMEM((SUBLANES - 1, tm + CONV_HALO, da), F32)],
        compiler_params=_params(("parallel",)),
    )(proj, proj, conv_w, conv_b, a_g, a_b, v_g, v_b, w_c, bias_full)


def _ab_mid_bwd(dcat, proj, a1, conv_w, a_g, a_b, v_g, v_b, w_c, bias_full, tm):
    s = proj.shape[0]
    da = a_g.shape[1]
    db = v_g.shape[1]
    nh = w_c.shape[0]
    kw = conv_w.shape[0]
    lead = CONV_HALO - (kw - 1)
    n_tiles = s // tm

    def body(dc_ref, dch_ref, p_ref, ph_ref, a1_ref, a1h_ref, cw_ref, ag_ref, ab_ref, vg_ref, vb_ref, wc_ref, bias_ref,
             dp_ref, dcw_ref, vec_ref, dws_ref, dbias_ref, ext_ref, dext_ref, dcw_acc, dvn_ref, sh_ref, dsh_ref):
        i = pl.program_id(0)

        @pl.when(i == 0)
        def _():
            dcw_acc[...] = jnp.zeros_like(dcw_acc)
            vec_ref[...] = jnp.zeros_like(vec_ref)
            dws_ref[...] = jnp.zeros_like(dws_ref)
            dbias_ref[...] = jnp.zeros_like(dbias_ref)

        agv, abv = ag_ref[...], ab_ref[...]

        def silu_ln_bwd(a1v, d_a2):
            z, yn, rstd = _layernorm_fwd(a1v, agv, abv)
            sg = jax.nn.sigmoid(z)
            return _layernorm_bwd(d_a2 * (sg * (1.0 + z * (1.0 - sg))), yn, rstd, agv)

        d_a1, dga, dba = silu_ln_bwd(a1_ref[...].astype(F32), dc_ref[:, 0:da])
        dext_ref[0:tm, :] = d_a1
        d_a1h, _, _ = silu_ln_bwd(a1h_ref[...].astype(F32), dch_ref[...])
        dext_ref[tm:, :] = jnp.where(i == n_tiles - 1, 0.0, d_a1h)
        vec_ref[0:1, 0:da] += dga
        vec_ref[1:2, 0:da] += dba
        vec_ref[2:3, 0:da] += jnp.sum(d_a1, axis=0, keepdims=True)

        val = p_ref[:, 0:da].astype(F32)
        sgg = jax.nn.sigmoid(p_ref[:, da:2 * da].astype(F32))
        ext_ref[CONV_HALO:, :] = val * sgg
        hv = ph_ref[:, 0:da].astype(F32)
        hg = ph_ref[:, da:2 * da].astype(F32)
        ext_ref[0:CONV_HALO, :] = jnp.where(i == 0, 0.0, hv * jax.nn.sigmoid(hg))

        _fill_shifted(ext_ref, sh_ref)
        _fill_shifted(dext_ref, dsh_ref)
        for r0, c0, rb, cb in _chunks(tm, da, 64, 256):
            cols = pl.ds(c0, cb)
            rows = pl.ds(r0, rb)
            d1 = dext_ref[rows, cols]
            acc = jnp.zeros((rb, cb), F32)
            for k in range(kw):
                acc = acc + cw_ref[k:k + 1, cols] * _shifted(dext_ref, dsh_ref, r0 + (kw - 1) - k, rb, cols)
                prod = d1 * _shifted(ext_ref, sh_ref, r0 + lead + k, rb, cols)
                dcw_acc[k, :, cols] += jnp.sum(prod.reshape(rb // SUBLANES, SUBLANES, cb), axis=0)
            v = p_ref[rows, pl.ds(c0, cb)].astype(F32)
            sg = jax.nn.sigmoid(p_ref[rows, pl.ds(da + c0, cb)].astype(F32))
            dp_ref[rows, pl.ds(c0, cb)] = (acc * sg).astype(BF16)
            dp_ref[rows, pl.ds(da + c0, cb)] = (acc * v * sg * (1.0 - sg)).astype(BF16)

        vgv = vg_ref[...]
        bu = p_ref[:, 2 * da:2 * da + db].astype(F32)
        bv = p_ref[:, 2 * da + db:].astype(F32)
        vn, yn_v, rstd_v = _layernorm_fwd(bv, vgv, vb_ref[...])
        vnb = vn.astype(BF16)
        for n in range(tm // CHUNK):
            rows = slice(n * CHUNK, (n + 1) * CHUNK)
            for h in range(nh):
                hc = slice(h * CHUNK, (h + 1) * CHUNK)
                wch = wc_ref[h]
                blk = vnb[rows, hc]
                vo = jnp.dot(wch, blk, preferred_element_type=F32) + bias_ref[:, hc]
                d_bout = dc_ref[rows, da + h * CHUNK:da + (h + 1) * CHUNK]
                dp_ref[rows, 2 * da + h * CHUNK:2 * da + (h + 1) * CHUNK] = (d_bout * vo).astype(BF16)
                d_vo = d_bout * bu[rows, hc]
                dbias_ref[h] += jnp.sum(d_vo, axis=1, keepdims=True)
                d_vob = d_vo.astype(BF16)
                dws_ref[h] += lax.dot_general(d_vob, blk, (((1,), (1,)), ((), ())), preferred_element_type=F32)
                dvn_ref[rows, hc] = lax.dot_general(wch, d_vob, (((0,), (0,)), ((), ())), preferred_element_type=F32)
        d_bv, dgv, dbv = _layernorm_bwd(dvn_ref[...], yn_v, rstd_v, vgv)
        dp_ref[:, 2 * da + db:] = d_bv.astype(BF16)
        vec_ref[3:4, 0:db] += dgv
        vec_ref[4:5, 0:db] += dbv

        @pl.when(i == n_tiles - 1)
        def _():
            dcw_ref[...] = jnp.sum(dcw_acc[...], axis=1)
            causal = lax.broadcasted_iota(jnp.int32, (CHUNK, CHUNK), 0) >= lax.broadcasted_iota(jnp.int32, (CHUNK, CHUNK), 1)
            for h in range(nh):
                dws_ref[h] = jnp.where(causal, dws_ref[h], 0.0)

    full = lambda a: pl.BlockSpec(a.shape, lambda i: (0,) * a.ndim)
    wide = max(da, db)
    return pl.pallas_call(
        body, grid=(n_tiles,), name="ab_mid_bwd",
        in_specs=[pl.BlockSpec((tm, da + db), lambda i: (i, 0)),
                  pl.BlockSpec((CONV_HALO, da), _next_halo_map(tm, CONV_HALO, s)),
                  pl.BlockSpec((tm, 2 * da + 2 * db), lambda i: (i, 0)),
                  pl.BlockSpec((CONV_HALO, 2 * da), _prev_halo_map(tm, CONV_HALO)),
                  pl.BlockSpec((tm, da), lambda i: (i, 0)),
                  pl.BlockSpec((CONV_HALO, da), _next_halo_map(tm, CONV_HALO, s)),
                  full(conv_w), full(a_g), full(a_b), full(v_g), full(v_b), full(w_c), full(bias_full)],
        out_specs=[pl.BlockSpec((tm, 2 * da + 2 * db), lambda i: (i, 0)),
                   pl.BlockSpec((kw, da), lambda i: (0, 0)),
                   pl.BlockSpec((8, wide), lambda i: (0, 0)),
                   pl.BlockSpec((nh, CHUNK, CHUNK), lambda i: (0, 0, 0)),
                   pl.BlockSpec((nh, CHUNK, 1), lambda i: (0, 0, 0))],
        out_shape=[_sds((s, 2 * da + 2 * db), BF16), _sds((kw, da), F32), _sds((8, wide), F32),
                   _sds((nh, CHUNK, CHUNK), F32), _sds((nh, CHUNK, 1), F32)],
        scratch_shapes=[pltpu.VMEM((tm + CONV_HALO, da), F32), pltpu.VMEM((tm + CONV_HALO, da), F32),
                        pltpu.VMEM((kw, SUBLANES, da), F32), pltpu.VMEM((tm, db), F32),
                        pltpu.VMEM((SUBLANES - 1, tm + CONV_HALO, da), F32), pltpu.VMEM((SUBLANES - 1, tm + CONV_HALO, da), F32)],
        compiler_params=_params(("arbitrary",)),
    )(dcat, dcat, proj, proj, a1, a1, conv_w, a_g, a_b, v_g, v_b, w_c, bias_full)


def _ada_fwd(c_all, w, b):
    nl, d, n = w.shape
    tn = _pick(n, (512, 256, 128))

    def body(c_ref, w_ref, b_ref, o_ref):
        cv = c_ref[...]
        cond = (cv * jax.nn.sigmoid(cv)).astype(BF16)
        o_ref[...] = jnp.dot(cond, w_ref[...].astype(BF16), preferred_element_type=F32) + b_ref[...]

    return pl.pallas_call(
        body, grid=(nl, n // tn), name="ada_fwd",
        in_specs=[pl.BlockSpec(c_all.shape, lambda l, j: (0, 0)), pl.BlockSpec((None, d, tn), lambda l, j: (l, 0, j)),
                  pl.BlockSpec((None, 1, tn), lambda l, j: (l, 0, j))],
        out_specs=pl.BlockSpec((None, c_all.shape[0], tn), lambda l, j: (l, 0, j)),
        out_shape=_sds((nl, c_all.shape[0], n), F32), compiler_params=_params(("parallel", "parallel")),
    )(c_all, w, b)


def _ada_bwd(c_all, dmod):
    nl, nb, n = dmod.shape
    d = c_all.shape[1]
    tn = _pick(n, (512, 256, 128))

    def body(c_ref, g_ref, o_ref):
        cv = c_ref[...]
        cond = (cv * jax.nn.sigmoid(cv)).astype(BF16)
        o_ref[...] = lax.dot_general(cond, g_ref[...].astype(BF16), (((0,), (0,)), ((), ())), preferred_element_type=F32)

    return pl.pallas_call(
        body, grid=(nl, n // tn), name="ada_bwd",
        in_specs=[pl.BlockSpec(c_all.shape, lambda l, j: (0, 0)), pl.BlockSpec((None, nb, tn), lambda l, j: (l, 0, j))],
        out_specs=pl.BlockSpec((None, d, tn), lambda l, j: (l, 0, j)),
        out_shape=_sds((nl, d, n), F32), compiler_params=_params(("parallel", "parallel")),
    )(c_all, dmod)


def _sum_leading(a, out_dtype=F32, name="sum_leading"):
    n, r, c = a.shape
    tr = _row_tile(r, c, 256 * 1024)

    def body(a_ref, o_ref):
        acc = a_ref[0].astype(F32)
        for k in range(1, n):
            acc = acc + a_ref[k].astype(F32)
        o_ref[...] = acc.astype(out_dtype)

    return pl.pallas_call(body, grid=(r // tr,), name=name, in_specs=[pl.BlockSpec((n, tr, c), lambda i: (0, i, 0))],
                          out_specs=pl.BlockSpec((tr, c), lambda i: (i, 0)), out_shape=_sds((r, c), out_dtype),
                          compiler_params=_params(("parallel",)))(a)


def _cast_into_full(w, kind, s_vec, l0, nl):
    _, r, c = w.shape
    tr = _row_tile(r, c, 512 * 1024)
    nb = r // tr
    if kind == "col":
        out_shape, out_spec = (nl, r, N_CHIP * c), pl.BlockSpec((None, tr, c), lambda l, i, sv: (l, i, sv[0]))
    else:
        out_shape, out_spec = (nl, N_CHIP * r, c), pl.BlockSpec((None, tr, c), lambda l, i, sv: (l, sv[0] * nb + i, 0))

    def body(sv_ref, w_ref, o_ref):
        o_ref[...] = w_ref[...].astype(BF16)

    return pl.pallas_call(
        body, name="cast_into_full",
        grid_spec=pltpu.PrefetchScalarGridSpec(num_scalar_prefetch=1, grid=(nl, nb),
                                               in_specs=[pl.BlockSpec((None, tr, c), lambda l, i, sv: (l0 + l, i, 0))], out_specs=out_spec),
        out_shape=_sds(out_shape, BF16), compiler_params=_params(("parallel", "parallel")),
    )(s_vec, w)


def _chip_sum_into(slab, cp, g, g_shape, layer, kind, sc_vec, rows_per_group=None):
    n, r, c = slab.shape
    rg = r if rows_per_group is None else rows_per_group
    tr = _row_tile(rg, c, 256 * 1024)
    groups = r // rg
    nbg = rg // tr
    nb = groups * nbg
    n_sc = len(sc_vec)
    if kind == "col":
        out_spec = pl.BlockSpec((None, tr, c), lambda gi, i, *sc: (layer, sc[1][0] * nb + gi * nbg + i, 0))
        own_spec = pl.BlockSpec((tr, c), lambda gi, i, *sc: (gi * nbg + i, sc[0][0]))
    else:
        out_spec = pl.BlockSpec((None, tr, c), lambda gi, i, *sc: (layer, gi * nbg + i, sc[1][0]))
        own_spec = pl.BlockSpec((tr, c), lambda gi, i, *sc: (gi * (n * nbg) + sc[0][0] * nbg + i, 0))

    def other(k):
        return pl.BlockSpec((None, tr, c), lambda gi, i, *sc: (sc[1 + k][0], gi * nbg + i, 0))

    in_specs = [own_spec] + [other(k) for k in range(1, n)]
    args = list(sc_vec) + [cp] + [slab] * (n - 1)
    aliases = {}
    if g is not None:
        in_specs.append(pl.BlockSpec(memory_space=pl.ANY))
        args.append(g)
        aliases = {len(args) - 1: 0}

    def body(*refs):
        own_ref, rest = refs[n_sc], refs[n_sc + 1:]
        o_ref = rest[-1]
        acc = own_ref[...].astype(F32)
        for k in range(n - 1):
            acc = acc + rest[k][...].astype(F32)
        o_ref[...] = acc

    return pl.pallas_call(
        body, name="chip_sum",
        grid_spec=pltpu.PrefetchScalarGridSpec(num_scalar_prefetch=n_sc, grid=(groups, nbg), in_specs=in_specs, out_specs=out_spec),
        out_shape=_sds(g_shape, F32), input_output_aliases=aliases, compiler_params=_params(("parallel", "parallel")),
    )(*args)


def _pair_add(dw, sib, kind, c_vec):
    r, c = sib.shape
    tr = _row_tile(r, c, 512 * 1024)
    nb = r // tr
    if kind == "col":
        dw_spec = pl.BlockSpec((tr, c), lambda i, cv: (cv[0] * nb + i, 0))
    else:
        dw_spec = pl.BlockSpec((tr, c), lambda i, cv: (i, cv[0]))

    def body(cv_ref, dw_ref, sib_ref, o_ref):
        o_ref[...] = (dw_ref[...].astype(F32) + sib_ref[...].astype(F32)).astype(BF16)

    return pl.pallas_call(
        body, name="pair_add",
        grid_spec=pltpu.PrefetchScalarGridSpec(num_scalar_prefetch=1, grid=(nb,), in_specs=[dw_spec, pl.BlockSpec((tr, c), lambda i, cv: (i, 0))],
                                               out_specs=pl.BlockSpec((tr, c), lambda i, cv: (i, 0))),
        out_shape=_sds((r, c), BF16), compiler_params=_params(("parallel",)),
    )(c_vec, dw, sib)


def _adamw(w, g, m, v):
    shape = w.shape
    cols = shape[-1]
    rows = w.size // cols
    tr = _row_tile(rows, cols, 256 * 1024)
    bc1 = 1.0 - ADAM_B1 ** ADAM_STEP
    bc2 = 1.0 - ADAM_B2 ** ADAM_STEP

    def body(w_ref, g_ref, m_ref, v_ref, d_ref, mo_ref, vo_ref):
        gv = g_ref[...]
        mn = ADAM_B1 * m_ref[...] + (1.0 - ADAM_B1) * gv
        vn = ADAM_B2 * v_ref[...] + (1.0 - ADAM_B2) * (gv * gv)
        d_ref[...] = -ADAM_LR * ((mn / bc1) / (jnp.sqrt(vn / bc2) + ADAM_EPS) + ADAM_WD * w_ref[...])
        mo_ref[...] = mn
        vo_ref[...] = vn

    spec = pl.BlockSpec((tr, cols), lambda i: (i, 0))
    outs = pl.pallas_call(body, grid=(rows // tr,), name="adamw", in_specs=[spec] * 4, out_specs=[spec] * 3,
                          out_shape=[_sds((rows, cols), F32)] * 3, compiler_params=_params(("parallel",)))(
        *[a.reshape(rows, cols) for a in (w, g, m, v)])
    return [o.reshape(shape) for o in outs]


def _mesh_pos():
    return lax.axis_index("x"), lax.axis_index("y"), lax.axis_index("c")


def _other_chips(x, y):
    return [(1 - x, y), (x, 1 - y), (1 - x, 1 - y)]


def _allgather_small(a):
    r, c = a.shape

    def body(x_ref, out_ref, send_sems, recv_sems, local_sem):
        x, y, cc = _mesh_pos()
        me, sibling = (x, y, cc), (x, y, 1 - cc)
        chips = _other_chips(x, y)

        def slab(px, py, pc):
            return out_ref.at[4 * px + 2 * py + pc]

        def copy(k, block, to, src=None):
            return pltpu.make_async_remote_copy(src_ref=slab(*block) if src is None else src, dst_ref=slab(*block),
                                                send_sem=send_sems.at[k], recv_sem=recv_sems.at[k], device_id=to,
                                                device_id_type=MESH)

        mine = pltpu.make_async_copy(x_ref, slab(*me), local_sem)
        mine.start()
        first = [copy(0, me, sibling, src=x_ref)]
        first += [copy(1 + j, me, (*chip, cc), src=x_ref) for j, chip in enumerate(chips)]
        for cp in first:
            cp.start()
        passed = [copy(4 + j, (*chip, cc), sibling) for j, chip in enumerate(chips)]
        for j, chip in enumerate(chips):
            copy(1 + j, (*chip, cc), me).wait_recv()
            passed[j].start()
        copy(0, sibling, me).wait_recv()
        for j, chip in enumerate(chips):
            copy(4 + j, (*chip, 1 - cc), me).wait_recv()
        for cp in first + passed:
            cp.wait_send()
        mine.wait()

    return pl.pallas_call(
        body, name="allgather_small", out_shape=_sds((N_DEV, r, c), F32),
        in_specs=[pl.BlockSpec(memory_space=pltpu.VMEM)], out_specs=pl.BlockSpec(memory_space=pltpu.VMEM),
        scratch_shapes=[pltpu.SemaphoreType.DMA((7,)), pltpu.SemaphoreType.DMA((7,)), pltpu.SemaphoreType.DMA],
        compiler_params=pltpu.CompilerParams(vmem_limit_bytes=V7X_VMEM_LIMIT_BYTES),
    )(a)


def _idx(ref, rows=None, cols=None):
    lead = (slice(None),) * (len(ref.shape) - 2)
    return ref.at[lead + (slice(None) if rows is None else rows, slice(None) if cols is None else cols)]


def _half(ref, kind, c):
    r, cdim = ref.shape[-2:]
    if kind == "col":
        return _idx(ref, rows=pl.ds(c * (r // 2), r // 2))
    return _idx(ref, cols=pl.ds(c * (cdim // 2), cdim // 2))


def _shard_region(full, kind, shard_shape, s):
    r, cdim = shard_shape[-2:]
    if kind == "col":
        return _idx(full, cols=pl.ds(s * cdim, cdim))
    return _idx(full, rows=pl.ds(s * r, r))


def _allgather_weights(fulls, kinds, shard_shapes):
    nt = len(fulls)

    def body(*refs):
        fu = refs[nt:2 * nt]
        send_sems, recv_sems = refs[2 * nt:]
        x, y, c = _mesh_pos()
        chips = _other_chips(x, y)

        def part(t, chip, cc):
            return _half(_shard_region(fu[t], kinds[t], shard_shapes[t], 2 * chip[0] + chip[1]), kinds[t], cc)

        def copy(t, k, blk, to):
            return pltpu.make_async_remote_copy(src_ref=blk, dst_ref=blk, send_sem=send_sems.at[6 * t + k],
                                                recv_sem=recv_sems.at[6 * t + k], device_id=to, device_id_type=MESH)

        first, passed = [], []
        for t in range(nt):
            for j, chip in enumerate(chips):
                cp = copy(t, j, part(t, (x, y), c), (*chip, c))
                cp.start()
                first.append(cp)
        for t in range(nt):
            for j, chip in enumerate(chips):
                copy(t, j, part(t, chip, c), (x, y, c)).wait_recv()
                fw = copy(t, 3 + j, part(t, chip, c), (x, y, 1 - c))
                fw.start()
                passed.append(fw)
        for t in range(nt):
            for j, chip in enumerate(chips):
                copy(t, 3 + j, part(t, chip, 1 - c), (x, y, c)).wait_recv()
        for cp in first + passed:
            cp.wait_send()

    return pl.pallas_call(
        body, name="allgather_weights", out_shape=[_sds(f.shape, BF16) for f in fulls],
        in_specs=[HBM] * nt, out_specs=[HBM] * nt, input_output_aliases={t: t for t in range(nt)},
        scratch_shapes=[pltpu.SemaphoreType.DMA((6 * nt,)), pltpu.SemaphoreType.DMA((6 * nt,))],
    )(*fulls)


SEM = pl.BlockSpec(memory_space=pltpu.SEMAPHORE)
ANY = pl.BlockSpec(memory_space=pl.ANY)
EFFECT = pltpu.SideEffectType.DATAFLOW_SIDE_EFFECTING


def _gather_start(name, fulls, kinds, shard_shapes, prev):
    nt = len(fulls)

    def body(*refs):
        send_sems, recv_sems = refs[nt + 1], refs[nt + 2]
        fu = refs[nt + 3:2 * nt + 3]
        token = refs[2 * nt + 3]
        x, y, c = _mesh_pos()
        for t in range(nt):
            mine = _half(_shard_region(fu[t], kinds[t], shard_shapes[t], 2 * x + y), kinds[t], c)
            for j, chip in enumerate(_other_chips(x, y)):
                for e in range(2):
                    pltpu.make_async_remote_copy(src_ref=mine, dst_ref=mine, send_sem=send_sems.at[6 * t + 2 * j + e],
                                                 recv_sem=recv_sems.at[6 * t + 2 * j + c], device_id=(*chip, e),
                                                 device_id_type=MESH).start()
        token[...] = jnp.zeros_like(token)

    outs = pl.pallas_call(
        body, name=name,
        out_shape=(pltpu.SemaphoreType.DMA((6 * nt,)), pltpu.SemaphoreType.DMA((6 * nt,)), *[pltpu.HBM(f.shape, f.dtype) for f in fulls],
                   _sds((8, 128), F32)),
        in_specs=[HBM] * nt + [ANY], out_specs=(SEM, SEM, *[HBM] * nt, pl.BlockSpec(memory_space=pltpu.VMEM)),
        input_output_aliases={t: 2 + t for t in range(nt)}, compiler_params=pltpu.CompilerParams(has_side_effects=EFFECT),
    )(*fulls, prev)
    return outs[0], outs[1], list(outs[2:2 + nt]), outs[2 + nt]


def _gather_wait(name, send_sems, recv_sems, fulls, kinds, shard_shapes, after):
    nt = len(fulls)

    def body(*refs):
        fu = refs[:nt]
        send_sems, recv_sems = refs[nt], refs[nt + 1]
        x, y, c = _mesh_pos()

        def part(t, chip, cc):
            return _half(_shard_region(fu[t], kinds[t], shard_shapes[t], 2 * chip[0] + chip[1]), kinds[t], cc)

        for t in range(nt):
            for j, chip in enumerate(_other_chips(x, y)):
                for e in range(2):
                    mine = part(t, (x, y), c)
                    pltpu.make_async_remote_copy(src_ref=mine, dst_ref=mine, send_sem=send_sems.at[6 * t + 2 * j + e],
                                                 recv_sem=recv_sems.at[6 * t + 2 * j + e], device_id=(x, y, c),
                                                 device_id_type=MESH).wait_send()
                    landed = part(t, chip, e)
                    pltpu.make_async_remote_copy(src_ref=landed, dst_ref=landed, send_sem=send_sems.at[6 * t + 2 * j + e],
                                                 recv_sem=recv_sems.at[6 * t + 2 * j + e], device_id=(x, y, c),
                                                 device_id_type=MESH).wait_recv()

    return pl.pallas_call(
        body, name=name, out_shape=[pltpu.HBM(f.shape, f.dtype) for f in fulls], in_specs=[HBM] * nt + [SEM, SEM, ANY],
        out_specs=[HBM] * nt, input_output_aliases={t: t for t in range(nt)},
        compiler_params=pltpu.CompilerParams(has_side_effects=EFFECT),
    )(*fulls, send_sems, recv_sems, after)


def _reduce_start(name, cps, kinds, prev):
    nt = len(cps)
    slab_shapes = []
    for cp, kind in zip(cps, kinds):
        shp = list(cp.shape)
        shp[-1 if kind == "col" else -2] //= N_CHIP
        slab_shapes.append((N_CHIP,) + tuple(shp))

    def body(*refs):
        send_sems, recv_sems = refs[nt + 1], refs[nt + 2]
        src = refs[nt + 3:2 * nt + 3]
        dst = refs[2 * nt + 3:3 * nt + 3]
        token = refs[3 * nt + 3]
        x, y, c = _mesh_pos()
        s = 2 * x + y
        for t in range(nt):
            for j, chip in enumerate(_other_chips(x, y)):
                pltpu.make_async_remote_copy(src_ref=_chip_block(src[t], kinds[t], 2 * chip[0] + chip[1]), dst_ref=dst[t].at[s],
                                             send_sem=send_sems.at[3 * t + j], recv_sem=recv_sems.at[3 * t + j],
                                             device_id=(*chip, c), device_id_type=MESH).start()
        token[...] = jnp.zeros_like(token)

    outs = pl.pallas_call(
        body, name=name,
        out_shape=(pltpu.SemaphoreType.DMA((3 * nt,)), pltpu.SemaphoreType.DMA((3 * nt,)), *[pltpu.HBM(a.shape, a.dtype) for a in cps],
                   *[pltpu.HBM(shp, BF16) for shp in slab_shapes], _sds((8, 128), F32)),
        in_specs=[HBM] * nt + [ANY], out_specs=(SEM, SEM, *[HBM] * (2 * nt), pl.BlockSpec(memory_space=pltpu.VMEM)),
        input_output_aliases={t: 2 + t for t in range(nt)}, compiler_params=pltpu.CompilerParams(has_side_effects=EFFECT),
    )(*cps, prev)
    return outs[0], outs[1], list(outs[2:2 + nt]), list(outs[2 + nt:2 + 2 * nt]), outs[2 + 2 * nt]


def _reduce_wait(name, send_sems, recv_sems, cps, slabs, kinds, after):
    nt = len(cps)

    def body(*refs):
        src, dst = refs[:nt], refs[nt:2 * nt]
        send_sems, recv_sems = refs[2 * nt], refs[2 * nt + 1]
        x, y, c = _mesh_pos()
        for t in range(nt):
            for j, chip in enumerate(_other_chips(x, y)):
                sj = 2 * chip[0] + chip[1]
                pltpu.make_async_remote_copy(src_ref=_chip_block(src[t], kinds[t], sj), dst_ref=dst[t].at[sj],
                                             send_sem=send_sems.at[3 * t + j], recv_sem=recv_sems.at[3 * t + j],
                                             device_id=(x, y, c), device_id_type=MESH).wait()

    outs = pl.pallas_call(
        body, name=name, out_shape=[pltpu.HBM(a.shape, a.dtype) for a in cps] + [pltpu.HBM(a.shape, a.dtype) for a in slabs],
        in_specs=[HBM] * (2 * nt) + [SEM, SEM, ANY], out_specs=[HBM] * (2 * nt), input_output_aliases={t: t for t in range(2 * nt)},
        compiler_params=pltpu.CompilerParams(has_side_effects=EFFECT),
    )(*cps, *slabs, send_sems, recv_sems, after)
    return list(outs[:nt]), list(outs[nt:])


def _pair_exchange(dws, kinds):
    nt = len(dws)
    out_shapes = []
    for dw, kind in zip(dws, kinds):
        shp = list(dw.shape)
        shp[-2 if kind == "col" else -1] //= 2
        out_shapes.append(tuple(shp))

    def body(*refs):
        src, dst = refs[:nt], refs[nt:2 * nt]
        send_sems, recv_sems = refs[2 * nt:]
        x, y, c = _mesh_pos()
        copies = [pltpu.make_async_remote_copy(src_ref=_half(src[t], kinds[t], 1 - c), dst_ref=dst[t], send_sem=send_sems.at[t],
                                               recv_sem=recv_sems.at[t], device_id=(x, y, 1 - c), device_id_type=MESH)
                  for t in range(nt)]
        for cp in copies:
            cp.start()
        for cp in copies:
            cp.wait()

    return pl.pallas_call(
        body, name="grad_pair_exchange", out_shape=[_sds(shp, BF16) for shp in out_shapes], in_specs=[HBM] * nt,
        out_specs=[HBM] * nt, scratch_shapes=[pltpu.SemaphoreType.DMA((nt,)), pltpu.SemaphoreType.DMA((nt,))],
    )(*dws)


def _chip_block(ref, kind, s):
    r, cdim = ref.shape[-2:]
    if kind == "col":
        return _idx(ref, cols=pl.ds(s * (cdim // N_CHIP), cdim // N_CHIP))
    return _idx(ref, rows=pl.ds(s * (r // N_CHIP), r // N_CHIP))


def _chip_exchange(cps, kinds):
    nt = len(cps)
    out_shapes = []
    for cp, kind in zip(cps, kinds):
        shp = list(cp.shape)
        shp[-1 if kind == "col" else -2] //= N_CHIP
        out_shapes.append((N_CHIP,) + tuple(shp))

    def body(*refs):
        src, dst = refs[:nt], refs[nt:2 * nt]
        send_sems, recv_sems, local_sems = refs[2 * nt:]
        x, y, c = _mesh_pos()
        s = 2 * x + y
        chips = _other_chips(x, y)
        sends, locals_ = [], []
        for t in range(nt):
            own = pltpu.make_async_copy(_chip_block(src[t], kinds[t], s), dst[t].at[s], local_sems.at[t])
            own.start()
            locals_.append(own)
            for j, chip in enumerate(chips):
                cp = pltpu.make_async_remote_copy(src_ref=_chip_block(src[t], kinds[t], 2 * chip[0] + chip[1]), dst_ref=dst[t].at[s],
                                                  send_sem=send_sems.at[3 * t + j], recv_sem=recv_sems.at[3 * t + j],
                                                  device_id=(*chip, c), device_id_type=MESH)
                cp.start()
                sends.append(cp)
        for t in range(nt):
            for j, chip in enumerate(chips):
                landing = dst[t].at[2 * chip[0] + chip[1]]
                pltpu.make_async_remote_copy(src_ref=landing, dst_ref=landing, send_sem=send_sems.at[3 * t + j],
                                             recv_sem=recv_sems.at[3 * t + j], device_id=(x, y, c), device_id_type=MESH).wait_recv()
        for cp in sends:
            cp.wait_send()
        for own in locals_:
            own.wait()

    return pl.pallas_call(
        body, name="grad_chip_exchange", out_shape=[_sds(shp, BF16) for shp in out_shapes], in_specs=[HBM] * nt,
        out_specs=[HBM] * nt,
        scratch_shapes=[pltpu.SemaphoreType.DMA((3 * nt,)), pltpu.SemaphoreType.DMA((3 * nt,)), pltpu.SemaphoreType.DMA((nt,))],
    )(*cps)


def _pair_assemble(gs, kinds):
    nt = len(gs)

    def body(*refs):
        g = refs[nt:2 * nt]
        send_sems, recv_sems = refs[2 * nt:]
        x, y, c = _mesh_pos()
        copies = []
        for t in range(nt):
            mine = _half(g[t], kinds[t], c)
            cp = pltpu.make_async_remote_copy(src_ref=mine, dst_ref=mine, send_sem=send_sems.at[t], recv_sem=recv_sems.at[t],
                                              device_id=(x, y, 1 - c), device_id_type=MESH)
            cp.start()
            copies.append(cp)
        for t in range(nt):
            landing = _half(g[t], kinds[t], 1 - c)
            pltpu.make_async_remote_copy(src_ref=landing, dst_ref=landing, send_sem=send_sems.at[t], recv_sem=recv_sems.at[t],
                                         device_id=(x, y, c), device_id_type=MESH).wait_recv()
        for cp in copies:
            cp.wait_send()

    return pl.pallas_call(
        body, name="grad_pair_assemble", out_shape=[_sds(a.shape, F32) for a in gs], in_specs=[HBM] * nt,
        out_specs=[HBM] * nt, input_output_aliases={t: t for t in range(nt)},
        scratch_shapes=[pltpu.SemaphoreType.DMA((nt,)), pltpu.SemaphoreType.DMA((nt,))],
    )(*gs)


def _pack(arrays, width):
    flat = jnp.concatenate([a.reshape(-1) for a in arrays])
    pad = (-flat.size) % (8 * width)
    return jnp.pad(flat, (0, pad)).reshape(-1, width)


def _unpack(packed, shapes):
    flat = packed.reshape(-1)
    out, off = [], 0
    for shp in shapes:
        n = 1
        for dim in shp:
            n *= dim
        out.append(flat[off:off + n].reshape(shp))
        off += n
    return out


def kernel(x, c, ada_w, ada_b, norm_mix_g, norm_ffn_g, ab_w_in, a_conv_w, a_conv_b, a_norm_g, a_norm_b, b_norm_g, b_norm_b, b_w_s, b_bias, ab_w_out, pool_w, pool_scale, ffn_w1, ffn_w3, ffn_w2, final_g, loss_target, m_ada_w, m_ada_b, m_norm_mix_g, m_norm_ffn_g, m_ab_w_in, m_a_conv_w, m_a_conv_b, m_a_norm_g, m_a_norm_b, m_b_norm_g, m_b_norm_b, m_b_w_s, m_b_bias, m_ab_w_out, m_pool_w, m_pool_scale, m_ffn_w1, m_ffn_w3, m_ffn_w2, m_final_g, v_ada_w, v_ada_b, v_norm_mix_g, v_norm_ffn_g, v_ab_w_in, v_a_conv_w, v_a_conv_b, v_a_norm_g, v_a_norm_b, v_b_norm_g, v_b_norm_b, v_b_w_s, v_b_bias, v_ab_w_out, v_pool_w, v_pool_scale, v_ffn_w1, v_ffn_w3, v_ffn_w2, v_final_g):
    mx, my, mc = _mesh_pos()
    chip = 2 * mx + my
    dev = 4 * mx + 2 * my + mc
    x2 = x[0]
    target = loss_target[0]
    s, d = x2.shape
    depth = ada_w.shape[0]
    n_mod = ada_b.shape[1] // d
    n_even = ab_w_in.shape[0]
    da, db = a_conv_b.shape[1], b_norm_g.shape[1]
    nh = b_w_s.shape[1]
    kw = a_conv_w.shape[1]
    n_pool = pool_w.shape[1]
    tm_row = _pick(s, (256, 128))

    s_vec = jnp.reshape(chip, (1,)).astype(jnp.int32)
    c_vec = jnp.reshape(mc, (1,)).astype(jnp.int32)
    sc_vec = [s_vec, c_vec] + [jnp.reshape(v, (1,)).astype(jnp.int32)
                               for v in (2 * mx + (1 - my), 2 * (1 - mx) + my, 2 * (1 - mx) + (1 - my))]
    pool_w3 = pool_w.reshape((-1,) + pool_w.shape[2:])
    shard_of = {"w_in": (ab_w_in, "col"), "w_out": (ab_w_out, "row"), "pool": (pool_w3, "row"), "w1": (ffn_w1, "col"),
                "w3": (ffn_w3, "col"), "w2": (ffn_w2, "row")}

    def layer_names(l):
        return (["w_in", "w_out"] if l % 2 == 0 else ["pool"]) + ["w1", "w3", "w2"]

    def layer_span(nm, l):
        if nm in ("w_in", "w_out"):
            return l // 2, 1
        if nm == "pool":
            return (l // 2) * n_pool, n_pool
        return l, 1

    owned, w_kinds, w_shapes = [], [], []
    for l in range(depth):
        names = layer_names(l)
        owned.append([_cast_into_full(shard_of[nm][0], shard_of[nm][1], s_vec, *layer_span(nm, l)) for nm in names])
        w_kinds.append([shard_of[nm][1] for nm in names])
        w_shapes.append([(layer_span(nm, l)[1],) + shard_of[nm][0].shape[1:] for nm in names])
    layer_w = [None] * depth
    layer_w[0] = dict(zip(layer_names(0), _allgather_weights(owned[0], w_kinds[0], w_shapes[0])))
    gather_prev = layer_w[0]["w2"]

    pre = _allgather_small(_pack([c, a_conv_w, pool_scale], 128)).reshape(N_DEV, -1)
    n_cw, n_ps = a_conv_w.size, pool_scale.size
    c_all = pre[:, :d]
    cw_chips = pre[0::2, d:d + n_cw].reshape((N_CHIP,) + a_conv_w.shape)
    conv_w_full = jnp.concatenate([cw_chips[k] for k in range(N_CHIP)], axis=-1)
    ps_chips = pre[0::2, d + n_cw:d + n_cw + n_ps].reshape((N_CHIP,) + pool_scale.shape)
    pool_scale_full = jnp.concatenate([ps_chips[k] for k in range(N_CHIP)], axis=-1)
    c_pad = jnp.pad(c_all, ((0, 8), (0, 0)))
    n_ada = ada_w.shape[2]
    ada_b_mine = lax.dynamic_slice_in_dim(ada_b, chip * n_ada, n_ada, axis=1)[:, None, :]
    mod_part = _ada_fwd(c_pad, ada_w, ada_b_mine)[:, :N_DEV, :]
    mod_all = _allgather_small(mod_part.reshape(depth * N_DEV, n_ada))
    mod_chips = mod_all[0::2].reshape(N_CHIP, depth, N_DEV, n_ada)
    mod_mine = lax.dynamic_index_in_dim(mod_chips, dev, axis=2, keepdims=False)
    mod = jnp.transpose(mod_mine, (1, 0, 2)).reshape(depth, n_mod, 1, d)

    causal = jnp.tril(jnp.ones((CHUNK, CHUNK), dtype=bool))
    w_c = jnp.where(causal[None, None], b_w_s, 0.0).astype(BF16)
    bias_full = jnp.repeat(jnp.swapaxes(b_bias, 1, 2), CHUNK, axis=2)

    saved = []
    xs = x2
    for l in range(depth):
        sh1, sc1, g1, sh2, sc2, g2 = [mod[l, k] for k in range(n_mod)]
        i = l // 2
        st = {"x1": xs}
        gain1 = norm_mix_g[l][None]
        if l + 1 < depth:
            in_flight = _gather_start(f"gather_start_{l + 1}", owned[l + 1], w_kinds[l + 1], w_shapes[l + 1], gather_prev)
            gain1 = gain1 + in_flight[3][0:1, 0:1]
        wl = layer_w[l]
        if l % 2 == 0:
            h = _norm_mod(xs, gain1, sh1, sc1, tm_row)
            proj = _mm("ab_proj", [h], [(wl["w_in"], 0)], [BF16], _ep_store, tk=2048)[0]
            cat, a1 = _ab_mid_fwd(proj, conv_w_full[i], a_conv_b[i][None], a_norm_g[i][None], a_norm_b[i][None],
                                  b_norm_g[i][None], b_norm_b[i][None], w_c[i], bias_full[i], tm_row)
            xs, y1 = _mm("ab_out", [cat], [(wl["w_out"], 0)], [F32, BF16], _ep_residual, extras=[xs, g1], extra_kinds=["tile", "row"], tk=2048)
            st.update(h=h, proj=proj, a1=a1, cat=cat, y=y1)
        else:
            p = _pool_fwd(xs, gain1, sh1, sc1, tm_row)
            gate = g1 * pool_scale_full[i][None]
            xs, ymm = _grouped_fwd(p, wl["pool"][None], 0, xs, gate, 1024)
            st.update(p=p, y=ymm, gate=gate)
        st["x2"] = xs
        h2 = _norm_mod(xs, norm_ffn_g[l][None], sh2, sc2, tm_row)
        u, t, z = _mm("ffn_up", [h2, h2], [(wl["w1"], 0), (wl["w3"], 0)], [BF16, BF16, BF16], _ep_swiglu, tn=512, tk=2048)
        xs, y2 = _mm("ffn_down", [z], [(wl["w2"], 0)], [F32, BF16], _ep_residual, extras=[xs, g2], extra_kinds=["tile", "row"],
                     tn=256, tk=ffn_w2.shape[1] * N_CHIP)
        if l + 1 < depth:
            landed = _gather_wait(f"gather_wait_{l + 1}", in_flight[0], in_flight[1], in_flight[2], w_kinds[l + 1], w_shapes[l + 1], xs)
            layer_w[l + 1] = dict(zip(layer_names(l + 1), landed))
            gather_prev = landed[-1]
        st.update(h2=h2, u=u, t=t, z=z, y2=y2)
        saved.append(st)

    dx, fin_acc, loss_blk = _final_loss_bwd(xs, final_g[None], target, tm_row)
    loss = lax.psum(loss_blk[0, 0], MESH_AXES)
    d_final_g = fin_acc[0]
    dmod_rows = [None] * depth
    d_norm_mix, d_norm_ffn = [None] * depth, [None] * depth
    even_small = [None] * n_even
    d_pool_scale = [None] * (depth // 2)
    grad_names = ["w_in", "w_out", "pool", "w1", "w3", "w2"]
    g_shapes = {"w_in": ab_w_in.shape, "w_out": ab_w_out.shape, "pool": (pool_w.shape[0], n_pool * pool_w.shape[2], pool_w.shape[3]),
                "w1": ffn_w1.shape, "w3": ffn_w3.shape, "w2": ffn_w2.shape}
    shard_grads = {nm: None for nm in grad_names}
    flight = {"pending": None, "prev": dx, "token": None}

    def finish_reduce(after):
        tag, pl_, names_, kinds_, sends, recvs, cps_f, slabs_f = flight["pending"]
        cps_d, slabs_d = _reduce_wait(f"reduce_wait_{tag}", sends, recvs, cps_f, slabs_f, kinds_, after)
        for nm, kind, cp, sl in zip(names_, kinds_, cps_d, slabs_d):
            rpg = pool_w.shape[2] if nm == "pool" else None
            cp2 = cp.reshape(-1, cp.shape[-1])
            shard_grads[nm] = _chip_sum_into(sl.reshape(N_CHIP, -1, sl.shape[-1]), cp2, shard_grads[nm], g_shapes[nm],
                                             layer_span(nm, pl_)[0] // (n_pool if nm == "pool" else 1), kind, sc_vec, rpg)
        flight["pending"] = None
        flight["prev"] = slabs_d[-1]

    def launch_reduce(tag, lyr, names, big, after):
        if flight["pending"] is not None:
            finish_reduce(after)
        kinds_l = [shard_of[nm][1] for nm in names]
        dws = [big[nm] for nm in names]
        sibs = _pair_exchange(dws, kinds_l)
        cps = []
        for dw, sib, kind in zip(dws, sibs, kinds_l):
            cp = _pair_add(dw.reshape(-1, dw.shape[-1]), sib.reshape(-1, sib.shape[-1]), kind, c_vec)
            cps.append(cp.reshape(sib.shape))
        sends, recvs, cps_f, slabs_f, token = _reduce_start(f"reduce_start_{tag}", cps, kinds_l, flight["prev"])
        flight["pending"] = (tag, lyr, names, kinds_l, sends, recvs, cps_f, slabs_f)
        flight["token"] = token

    def behind(gate):
        return gate if flight["token"] is None else gate + flight["token"][0:1, 0:1]

    for l in reversed(range(depth)):
        sh1, sc1, g1, sh2, sc2, g2 = [mod[l, k] for k in range(n_mod)]
        st = saved[l]
        wl = layer_w[l]
        i = l // 2
        big = {}
        dyb, gacc = _gate_bwd(dx, st["y2"], behind(g2), tm_row)
        d_g2 = gacc[0]
        du, dt = _mm("ffn_dz", [dyb], [(wl["w2"], 0)], [BF16, BF16], _ep_swiglu_bwd, trans_b=True, extras=[st["u"], st["t"]],
                     extra_kinds=["tile", "tile"], tn=512, tk=2048)
        big["w2"] = _mm("ffn_dw2", [st["z"]], [dyb], [BF16], _ep_store, trans_a=True, tm=512, tn=512, tk=s)[0]
        big["w1"] = _mm("ffn_dw1", [st["h2"]], [du], [BF16], _ep_store, trans_a=True, tm=512, tn=512, tk=s)[0]
        big["w3"] = _mm("ffn_dw3", [st["h2"]], [dt], [BF16], _ep_store, trans_a=True, tm=512, tn=512, tk=s)[0]
        dh = _mm("ffn_dh", [du, dt], [(wl["w1"], 0), (wl["w3"], 0)], [F32], _ep_sum, trans_b=True, tn=512, tk=ffn_w2.shape[1] * 2)[0]
        dx, nacc = _norm_mod_bwd(dh, st["x2"], dx, norm_ffn_g[l][None], sc2, tm_row)
        d_sh2, d_sc2, d_norm_ffn[l] = nacc[0], nacc[1], nacc[2]
        launch_reduce(f"ffn{l}", l, ["w1", "w3", "w2"], big, dx)
        if l % 2 == 0:
            dyb, gacc = _gate_bwd(dx, st["y"], behind(g1), tm_row)
            d_g1 = gacc[0]
            big["w_out"] = _mm("ab_dw_out", [st["cat"]], [dyb], [BF16], _ep_store, trans_a=True, tm=512, tn=512, tk=s)[0]
            dcat = _mm("ab_dcat", [dyb], [(wl["w_out"], 0)], [F32], _ep_store, trans_b=True, tk=2048)[0]
            dproj, dcw, vecs, dws, dbias = _ab_mid_bwd(dcat, st["proj"], st["a1"], conv_w_full[i], a_norm_g[i][None],
                                                       a_norm_b[i][None], b_norm_g[i][None], b_norm_b[i][None], w_c[i],
                                                       bias_full[i], tm_row)
            even_small[i] = dict(conv_w=dcw, a_norm_g=vecs[0, :da], a_norm_b=vecs[1, :da], conv_b=vecs[2, :da],
                                 b_norm_g=vecs[3, :db], b_norm_b=vecs[4, :db], w_s=dws, bias=dbias[:, :, 0])
            big["w_in"] = _mm("ab_dw_in", [st["h"]], [dproj], [BF16], _ep_store, trans_a=True, tm=512, tn=512, tk=s)[0]
            dh = _mm("ab_dh", [dproj], [(wl["w_in"], 0)], [F32], _ep_store, trans_b=True, tn=512, tk=2 * da + 2 * db)[0]
            dx, nacc = _norm_mod_bwd(dh, st["x1"], dx, norm_mix_g[l][None], sc1, tm_row)
        else:
            dyb, gacc = _gate_bwd(dx, st["y"], behind(st["gate"]), tm_row)
            d_g1 = gacc[0] * pool_scale_full[i]
            d_pool_scale[i] = gacc[0] * g1[0]
            big["pool"] = _grouped_dw(st["p"], dyb, n_pool, 1024)
            dp = _grouped_dx(dyb, wl["pool"][None], 0, 1024)
            dx, nacc = _pool_bwd(dp, st["x1"], dx, norm_mix_g[l][None], sc1, tm_row)
        d_sh1, d_sc1, d_norm_mix[l] = nacc[0], nacc[1], nacc[2]
        dmod_rows[l] = jnp.concatenate([d_sh1, d_sc1, d_g1, d_sh2, d_sc2, d_g2])
        launch_reduce(f"mix{l}", l, ["w_in", "w_out"] if l % 2 == 0 else ["pool"], big, dx)
    grad_x = dx[None]

    dmod = jnp.stack(dmod_rows)
    small = [dmod, jnp.stack(d_norm_mix), jnp.stack(d_norm_ffn),
             jnp.stack([e["conv_w"] for e in even_small]), jnp.stack([e["conv_b"] for e in even_small]),
             jnp.stack([e["a_norm_g"] for e in even_small]), jnp.stack([e["a_norm_b"] for e in even_small]),
             jnp.stack([e["b_norm_g"] for e in even_small]), jnp.stack([e["b_norm_b"] for e in even_small]),
             jnp.stack([e["w_s"] for e in even_small]), jnp.stack([e["bias"] for e in even_small]),
             jnp.stack(d_pool_scale), d_final_g]
    small_shapes = [a.shape for a in small]
    width = 1024 if d >= 1024 else 128
    gathered = _allgather_small(_pack(small, width))
    summed = _unpack(_sum_leading(gathered), small_shapes)
    (g_ada_b, g_norm_mix, g_norm_ffn, g_conv_w_full, g_conv_b, g_a_norm_g, g_a_norm_b, g_b_norm_g, g_b_norm_b, g_w_s, g_bias,
     g_pool_scale_full, g_final_g) = summed
    cw_cols = a_conv_w.shape[2]
    g_conv_w = lax.dynamic_slice_in_dim(g_conv_w_full, chip * cw_cols, cw_cols, axis=2)
    ps_cols = pool_scale.shape[1]
    g_pool_scale = lax.dynamic_slice_in_dim(g_pool_scale_full, chip * ps_cols, ps_cols, axis=1)

    dmod_all = gathered.reshape(N_DEV, -1)[:, :dmod.size].reshape(N_DEV, depth, n_mod * d)
    dmod_cols = lax.dynamic_slice_in_dim(dmod_all, chip * n_ada, n_ada, axis=2)
    dmod_cols = jnp.pad(jnp.transpose(dmod_cols, (1, 0, 2)), ((0, 0), (0, 8), (0, 0)))
    g_ada_w = _ada_bwd(c_pad, dmod_cols)

    finish_reduce(g_ada_w)
    g_w_in, g_w_out, g_pool, g_w1, g_w3, g_w2 = _pair_assemble([shard_grads[nm] for nm in grad_names],
                                                              [shard_of[nm][1] for nm in grad_names])

    grads = [g_ada_w, g_ada_b, g_norm_mix, g_norm_ffn, g_w_in, g_conv_w, g_conv_b, g_a_norm_g, g_a_norm_b, g_b_norm_g,
             g_b_norm_b, g_w_s, g_bias, g_w_out, g_pool, g_pool_scale, g_w1, g_w3, g_w2, g_final_g]
    weights = [ada_w, ada_b, norm_mix_g, norm_ffn_g, ab_w_in, a_conv_w, a_conv_b, a_norm_g, a_norm_b, b_norm_g, b_norm_b, b_w_s,
               b_bias, ab_w_out, pool_w, pool_scale, ffn_w1, ffn_w3, ffn_w2, final_g]
    ms = [m_ada_w, m_ada_b, m_norm_mix_g, m_norm_ffn_g, m_ab_w_in, m_a_conv_w, m_a_conv_b, m_a_norm_g, m_a_norm_b, m_b_norm_g,
          m_b_norm_b, m_b_w_s, m_b_bias, m_ab_w_out, m_pool_w, m_pool_scale, m_ffn_w1, m_ffn_w3, m_ffn_w2, m_final_g]
    vs = [v_ada_w, v_ada_b, v_norm_mix_g, v_norm_ffn_g, v_ab_w_in, v_a_conv_w, v_a_conv_b, v_a_norm_g, v_a_norm_b, v_b_norm_g,
          v_b_norm_b, v_b_w_s, v_b_bias, v_ab_w_out, v_pool_w, v_pool_scale, v_ffn_w1, v_ffn_w3, v_ffn_w2, v_final_g]
    deltas, new_ms, new_vs = [], [], []
    for w, g, m, v in zip(weights, grads, ms, vs):
        g = g.reshape(w.shape)
        dl, nm_, nv_ = _adamw(w, g, m, v)
        deltas.append(dl)
        new_ms.append(nm_)
        new_vs.append(nv_)
    grads = [g.reshape(w.shape) for g, w in zip(grads, weights)]
    return (loss, grad_x, *grads, *deltas, *new_ms, *new_vs)
```

```python
import functools

import jax
import jax.numpy as jnp
from jax import lax
from jax.experimental import pallas as pl
from jax.experimental.pallas import tpu as pltpu

F32 = jnp.float32
BF16 = jnp.bfloat16
EPS = 1e-6
N_DEV = 8
N_CHIP = 4
MESH_AXES = ("x", "y", "c")
MESH = pl.DeviceIdType.MESH
V7X_VMEM_LIMIT_BYTES = 56 * 1024 * 1024
SUBLANES = 8
CONV_HALO = 32
POOL_HALO = 16
POOL_WINDOWS = (2, 4, 8, 16)
CHUNK = 128
ADAM_LR, ADAM_B1, ADAM_B2, ADAM_EPS, ADAM_WD, ADAM_STEP = 0.001, 0.9, 0.999, 1e-08, 0.01, 10
HBM = pl.BlockSpec(memory_space=pltpu.HBM)


def _params(sem=None):
    return pltpu.CompilerParams(dimension_semantics=sem, vmem_limit_bytes=V7X_VMEM_LIMIT_BYTES)


def _pick(n, prefs):
    for p in prefs:
        if p <= n and n % p == 0:
            return p
    return n


def _row_tile(rows, cols, target):
    if rows * cols <= target:
        return rows
    best = None
    for d in range(16, rows, 16):
        if rows % d == 0 and d * cols <= target:
            best = d
    return best if best is not None else rows


def _sds(shape, dtype):
    return jax.ShapeDtypeStruct(tuple(shape), dtype)


def _mm_core(name, grid, nk, a_list, b_list, a_spec, b_spec, dn, extras, extra_specs, out_shapes, out_specs,
             acc_shape, epilogue):
    n, n_ex, n_out = len(a_list), len(extras), len(out_shapes)

    def body(*refs):
        a_refs, b_refs = refs[:n], refs[n:2 * n]
        ex = refs[2 * n:2 * n + n_ex]
        outs = refs[2 * n + n_ex:2 * n + n_ex + n_out]
        accs = refs[2 * n + n_ex + n_out:]
        ps = [lax.dot_general(a[...], b[...], dn, preferred_element_type=F32) for a, b in zip(a_refs, b_refs)]
        if nk == 1:
            epilogue(ps, ex, outs)
            return
        k = pl.program_id(2)

        @pl.when(k == 0)
        def _():
            for acc, p in zip(accs, ps):
                acc[...] = p

        @pl.when(k > 0)
        def _():
            for acc, p in zip(accs, ps):
                acc[...] += p

        @pl.when(k == nk - 1)
        def _():
            epilogue([acc[...] for acc in accs], ex, outs)

    scratch = [] if nk == 1 else [pltpu.VMEM(acc_shape, F32) for _ in range(n)]
    return pl.pallas_call(
        body, grid=grid, name=name,
        in_specs=[a_spec] * n + [b_spec] * n + list(extra_specs),
        out_specs=list(out_specs), out_shape=list(out_shapes), scratch_shapes=scratch,
        compiler_params=_params(("parallel", "parallel", "arbitrary")),
    )(*a_list, *b_list, *extras)


def _mm(name, a_list, b_list, out_dtypes, epilogue, *, trans_a=False, trans_b=False, extras=(), extra_kinds=(),
        tm=1024, tn=1024, tk=1024):
    layer = None
    if isinstance(b_list[0], tuple):
        layer = b_list[0][1]
        b_list = [b for b, _ in b_list]
    a0 = a_list[0]
    b_shape = b_list[0].shape[-2:]
    m, kk = (a0.shape[1], a0.shape[0]) if trans_a else a0.shape
    nn = b_shape[0] if trans_b else b_shape[1]
    tm, tn, tk = _pick(m, (tm, 512, 256, 128)), _pick(nn, (tn, 512, 256, 128)), _pick(kk, (tk, 512, 256, 128))
    nk = kk // tk
    a_spec = pl.BlockSpec((tk, tm), lambda i, j, k: (k, i)) if trans_a else pl.BlockSpec((tm, tk), lambda i, j, k: (i, k))
    if layer is None:
        b_spec = pl.BlockSpec((tn, tk), lambda i, j, k: (j, k)) if trans_b else pl.BlockSpec((tk, tn), lambda i, j, k: (k, j))
    elif trans_b:
        b_spec = pl.BlockSpec((None, tn, tk), lambda i, j, k: (layer, j, k))
    else:
        b_spec = pl.BlockSpec((None, tk, tn), lambda i, j, k: (layer, k, j))
    dn = (((0 if trans_a else 1,), (1 if trans_b else 0,)), ((), ()))
    tile = pl.BlockSpec((tm, tn), lambda i, j, k: (i, j))
    row = pl.BlockSpec((1, tn), lambda i, j, k: (0, j))
    return _mm_core(name, (m // tm, nn // tn, nk), nk, a_list, b_list, a_spec, b_spec, dn, extras,
                    [tile if kd == "tile" else row for kd in extra_kinds],
                    [_sds((m, nn), dt) for dt in out_dtypes], [tile] * len(out_dtypes), (tm, tn), epilogue)


def _ep_store(ps, ex, outs):
    outs[0][...] = ps[0].astype(outs[0].dtype)


def _ep_sum(ps, ex, outs):
    outs[0][...] = (ps[0] + ps[1]).astype(outs[0].dtype)


def _ep_swiglu(ps, ex, outs):
    u, t = ps
    sg = jax.nn.sigmoid(u)
    su = u * sg
    outs[0][...] = (t * (sg * (1.0 + u * (1.0 - sg)))).astype(BF16)
    outs[1][...] = su.astype(BF16)
    outs[2][...] = (su * t).astype(BF16)


def _ep_residual(ps, ex, outs):
    x_ref, gate_ref = ex
    y = ps[0]
    outs[0][...] = x_ref[...] + gate_ref[...] * y
    outs[1][...] = y.astype(BF16)


def _ep_swiglu_bwd(ps, ex, outs):
    dz = ps[0]
    outs[0][...] = (dz * ex[0][...].astype(F32)).astype(BF16)
    outs[1][...] = (dz * ex[1][...].astype(F32)).astype(BF16)


def _grouped_fwd(p, w, layer, x, gate, tm):
    s, d = p.shape
    _, g, kg, ng = w.shape
    tm = _pick(s, (tm, 512, 256, 128))
    tile_a = pl.BlockSpec((tm, kg), lambda i, j, k: (i, j))
    tile_o = pl.BlockSpec((tm, ng), lambda i, j, k: (i, j))
    return _mm_core("pool_mm_fwd", (s // tm, g, 1), 1, [p], [w], tile_a,
                    pl.BlockSpec((None, None, kg, ng), lambda i, j, k: (layer, j, 0, 0)), (((1,), (0,)), ((), ())),
                    [x, gate], [tile_o, pl.BlockSpec((1, ng), lambda i, j, k: (0, j))],
                    [_sds((s, g * ng), F32), _sds((s, g * ng), BF16)], [tile_o, tile_o], None, _ep_residual)


def _grouped_dx(dy, w, layer, tm):
    s, _ = dy.shape
    _, g, kg, ng = w.shape
    tm = _pick(s, (tm, 512, 256, 128))
    return _mm_core("pool_mm_dx", (s // tm, g, 1), 1, [dy], [w], pl.BlockSpec((tm, ng), lambda i, j, k: (i, j)),
                    pl.BlockSpec((None, None, kg, ng), lambda i, j, k: (layer, j, 0, 0)), (((1,), (1,)), ((), ())), [], [],
                    [_sds((s, g * kg), F32)], [pl.BlockSpec((tm, kg), lambda i, j, k: (i, j))], None, _ep_store)[0]


def _grouped_dw(p, dy, groups, tk):
    s, d = p.shape
    kg = d // groups
    ng = dy.shape[1] // groups
    tk = _pick(s, (tk, 512, 256, 128))
    nk = s // tk

    def ep(ps, ex, outs):
        outs[0][...] = ps[0].astype(BF16)

    return _mm_core("pool_mm_dw", (groups, 1, nk), nk, [p], [dy], pl.BlockSpec((tk, kg), lambda i, j, k: (k, i)),
                    pl.BlockSpec((tk, ng), lambda i, j, k: (k, i)), (((0,), (0,)), ((), ())), [], [],
                    [_sds((groups, kg, ng), BF16)], [pl.BlockSpec((None, kg, ng), lambda i, j, k: (i, 0, 0))],
                    (kg, ng), ep)[0]


def _rms_rstd(xv):
    return lax.rsqrt(jnp.mean(xv * xv, axis=-1, keepdims=True) + EPS)


def _norm_mod_math(xv, g, sh, sc):
    return ((xv * _rms_rstd(xv)) * g) * (1.0 + sc) + sh


def _norm_mod_bwd_math(dh, xv, dxo, g, sc, acc_ref):
    r = _rms_rstd(xv)
    xhat = xv * r
    acc_ref[0:1, :] += jnp.sum(dh, axis=0, keepdims=True)
    acc_ref[1:2, :] += jnp.sum(dh * (xhat * g), axis=0, keepdims=True)
    dhn = dh * (1.0 + sc)
    acc_ref[2:3, :] += jnp.sum(dhn * xhat, axis=0, keepdims=True)
    dxh = dhn * g
    return dxo + r * (dxh - xhat * jnp.mean(dxh * xhat, axis=-1, keepdims=True))


def _vec_spec(d):
    return pl.BlockSpec((1, d), lambda i: (0, 0))


def _acc_spec(d):
    return pl.BlockSpec((8, d), lambda i: (0, 0))


def _norm_mod(x, g, sh, sc, tm):
    s, d = x.shape

    def body(x_ref, g_ref, sh_ref, sc_ref, h_ref):
        h_ref[...] = _norm_mod_math(x_ref[...], g_ref[...], sh_ref[...], sc_ref[...]).astype(BF16)

    row = pl.BlockSpec((tm, d), lambda i: (i, 0))
    return pl.pallas_call(body, grid=(s // tm,), name="norm_mod", in_specs=[row] + [_vec_spec(d)] * 3, out_specs=row,
                          out_shape=_sds((s, d), BF16), compiler_params=_params(("parallel",)))(x, g, sh, sc)


def _gate_step(dxv, y_ref, gate_ref, dy_ref, gacc_ref):
    dy_ref[...] = (dxv * gate_ref[...]).astype(BF16)
    gacc_ref[0:1, :] += jnp.sum(dxv * y_ref[...].astype(F32), axis=0, keepdims=True)


def _norm_mod_bwd(dh, x, dxo, g, sc, tm, below=None):
    s, d = x.shape

    def body(dh_ref, x_ref, dxo_ref, g_ref, sc_ref, *rest):
        if below is None:
            dx_ref, acc_ref = rest
        else:
            y_ref, gate_ref, dx_ref, acc_ref, dy_ref, gacc_ref = rest

        @pl.when(pl.program_id(0) == 0)
        def _():
            acc_ref[...] = jnp.zeros_like(acc_ref)
            if below is not None:
                gacc_ref[...] = jnp.zeros_like(gacc_ref)

        dxv = _norm_mod_bwd_math(dh_ref[...], x_ref[...], dxo_ref[...], g_ref[...], sc_ref[...], acc_ref)
        dx_ref[...] = dxv
        if below is not None:
            _gate_step(dxv, y_ref, gate_ref, dy_ref, gacc_ref)

    row = pl.BlockSpec((tm, d), lambda i: (i, 0))
    extra_in = [] if below is None else [row, _vec_spec(d)]
    extra_out = [] if below is None else [row, _acc_spec(d)]
    extra_shape = [] if below is None else [_sds((s, d), BF16), _sds((8, d), F32)]
    return pl.pallas_call(body, grid=(s // tm,), name="norm_mod_bwd", in_specs=[row, row, row, _vec_spec(d), _vec_spec(d)] + extra_in,
                          out_specs=[row, _acc_spec(d)] + extra_out, out_shape=[_sds((s, d), F32), _sds((8, d), F32)] + extra_shape,
                          compiler_params=_params(("arbitrary",)))(dh, x, dxo, g, sc, *([] if below is None else below))


def _gate_bwd(dx, y, gate, tm):
    s, d = dx.shape

    def body(dx_ref, y_ref, gate_ref, dy_ref, acc_ref):
        @pl.when(pl.program_id(0) == 0)
        def _():
            acc_ref[...] = jnp.zeros_like(acc_ref)

        dxv = dx_ref[...]
        dy_ref[...] = (dxv * gate_ref[...]).astype(BF16)
        acc_ref[0:1, :] += jnp.sum(dxv * y_ref[...].astype(F32), axis=0, keepdims=True)

    row = pl.BlockSpec((tm, d), lambda i: (i, 0))
    return pl.pallas_call(body, grid=(s // tm,), name="gate_bwd", in_specs=[row, row, _vec_spec(d)],
                          out_specs=[row, _acc_spec(d)], out_shape=[_sds((s, d), BF16), _sds((8, d), F32)],
                          compiler_params=_params(("arbitrary",)))(dx, y, gate)


def _final_loss_bwd(x, g, target, tm, below):
    s, d = x.shape

    def body(x_ref, g_ref, t_ref, y_ref, gate_ref, dx_ref, acc_ref, loss_ref, dy_ref, gacc_ref):
        @pl.when(pl.program_id(0) == 0)
        def _():
            acc_ref[...] = jnp.zeros_like(acc_ref)
            loss_ref[...] = jnp.zeros_like(loss_ref)
            gacc_ref[...] = jnp.zeros_like(gacc_ref)

        xv = x_ref[...]
        gv = g_ref[...]
        r = _rms_rstd(xv)
        xhat = xv * r
        err = xhat * gv - t_ref[...]
        loss_ref[...] += (0.5 / d) * jnp.sum(err * err)
        dy = err * (1.0 / d)
        acc_ref[0:1, :] += jnp.sum(dy * xhat, axis=0, keepdims=True)
        dxh = dy * gv
        dxv = r * (dxh - xhat * jnp.mean(dxh * xhat, axis=-1, keepdims=True))
        dx_ref[...] = dxv
        _gate_step(dxv, y_ref, gate_ref, dy_ref, gacc_ref)

    row = pl.BlockSpec((tm, d), lambda i: (i, 0))
    return pl.pallas_call(body, grid=(s // tm,), name="final_loss_bwd", in_specs=[row, _vec_spec(d), row, row, _vec_spec(d)],
                          out_specs=[row, _acc_spec(d), pl.BlockSpec((8, 128), lambda i: (0, 0)), row, _acc_spec(d)],
                          out_shape=[_sds((s, d), F32), _sds((8, d), F32), _sds((8, 128), F32), _sds((s, d), BF16), _sds((8, d), F32)],
                          compiler_params=_params(("arbitrary",)))(x, g, target, *below)


def _chunks(tm, width, rb, cb):
    rb, cb = min(rb, tm), min(cb, width)
    return [(r0, c0, rb, cb) for r0 in range(0, tm, rb) for c0 in range(0, width, cb)]


def _prev_halo_map(tm, halo):
    return lambda i: (jnp.maximum(i * (tm // halo) - 1, 0), 0)


def _next_halo_map(tm, halo, s):
    return lambda i: (jnp.minimum((i + 1) * (tm // halo), s // halo - 1), 0)


def _pool_fwd(x, g, sh, sc, tm):
    s, d = x.shape
    dg = d // len(POOL_WINDOWS)

    def body(x_ref, xh_ref, g_ref, sh_ref, sc_ref, p_ref, ext_ref):
        i = pl.program_id(0)
        gv, shv, scv = g_ref[...], sh_ref[...], sc_ref[...]
        ext_ref[POOL_HALO:, :] = _norm_mod_math(x_ref[...], gv, shv, scv)
        ext_ref[0:POOL_HALO, :] = jnp.where(i == 0, 0.0, _norm_mod_math(xh_ref[...], gv, shv, scv))
        for gi, w in enumerate(POOL_WINDOWS):
            for r0, c0, rb, cb in _chunks(tm, dg, 64, 256):
                cols = pl.ds(gi * dg + c0, cb)
                tok = ext_ref[pl.ds(POOL_HALO + r0, rb), cols]
                acc = tok
                for j in range(1, w):
                    acc = acc + ext_ref[pl.ds(POOL_HALO + r0 - j, rb), cols]
                t_glob = i * tm + r0 + lax.broadcasted_iota(jnp.int32, (rb, 1), 0)
                cnt = jnp.minimum(t_glob + 1, w).astype(F32)
                p_ref[pl.ds(r0, rb), cols] = (acc / cnt - tok).astype(BF16)

    row = pl.BlockSpec((tm, d), lambda i: (i, 0))
    halo = pl.BlockSpec((POOL_HALO, d), _prev_halo_map(tm, POOL_HALO))
    return pl.pallas_call(body, grid=(s // tm,), name="pool_fwd", in_specs=[row, halo] + [_vec_spec(d)] * 3, out_specs=row,
                          out_shape=_sds((s, d), BF16), scratch_shapes=[pltpu.VMEM((tm + POOL_HALO, d), F32)],
                          compiler_params=_params(("parallel",)))(x, x, g, sh, sc)


def _pool_bwd(dp, x, dxo, g, sc, tm, below):
    s, d = x.shape
    dg = d // len(POOL_WINDOWS)
    n_tiles = s // tm

    def body(dp_ref, dph_ref, x_ref, dxo_ref, g_ref, sc_ref, y_ref, gate_ref, dx_ref, acc_ref, dy_ref, gacc_ref, ext_ref, dh_ref):
        i = pl.program_id(0)

        @pl.when(i == 0)
        def _():
            acc_ref[...] = jnp.zeros_like(acc_ref)
            gacc_ref[...] = jnp.zeros_like(gacc_ref)

        for gi, w in enumerate(POOL_WINDOWS):
            cols = pl.ds(gi * dg, dg)
            t_main = i * tm + lax.broadcasted_iota(jnp.int32, (tm, 1), 0)
            ext_ref[0:tm, cols] = dp_ref[:, cols] / jnp.minimum(t_main + 1, w).astype(F32)
            ext_ref[tm:, cols] = jnp.where(i == n_tiles - 1, 0.0, dph_ref[:, cols] * (1.0 / w))
            for r0, c0, rb, cb in _chunks(tm, dg, 64, 256):
                cc = pl.ds(gi * dg + c0, cb)
                acc = ext_ref[pl.ds(r0, rb), cc]
                for j in range(1, w):
                    acc = acc + ext_ref[pl.ds(r0 + j, rb), cc]
                dh_ref[pl.ds(r0, rb), cc] = acc - dp_ref[pl.ds(r0, rb), cc]
        dxv = _norm_mod_bwd_math(dh_ref[...], x_ref[...], dxo_ref[...], g_ref[...], sc_ref[...], acc_ref)
        dx_ref[...] = dxv
        _gate_step(dxv, y_ref, gate_ref, dy_ref, gacc_ref)

    row = pl.BlockSpec((tm, d), lambda i: (i, 0))
    halo = pl.BlockSpec((POOL_HALO, d), _next_halo_map(tm, POOL_HALO, s))
    return pl.pallas_call(body, grid=(n_tiles,), name="pool_bwd",
                          in_specs=[row, halo, row, row, _vec_spec(d), _vec_spec(d), row, _vec_spec(d)],
                          out_specs=[row, _acc_spec(d), row, _acc_spec(d)],
                          out_shape=[_sds((s, d), F32), _sds((8, d), F32), _sds((s, d), BF16), _sds((8, d), F32)],
                          scratch_shapes=[pltpu.VMEM((tm + POOL_HALO, d), F32), pltpu.VMEM((tm, d), F32)],
                          compiler_params=_params(("arbitrary",)))(dp, dp, x, dxo, g, sc, *below)


def _layernorm_fwd(v, g, b):
    mu = jnp.mean(v, axis=-1, keepdims=True)
    xc = v - mu
    rstd = lax.rsqrt(jnp.mean(xc * xc, axis=-1, keepdims=True) + EPS)
    yn = xc * rstd
    return yn * g + b, yn, rstd


def _layernorm_bwd(dz, yn, rstd, g):
    dyn = dz * g
    dv = rstd * (dyn - jnp.mean(dyn, axis=-1, keepdims=True) - yn * jnp.mean(dyn * yn, axis=-1, keepdims=True))
    return dv, jnp.sum(dz * yn, axis=0, keepdims=True), jnp.sum(dz, axis=0, keepdims=True)


def _fill_shifted(ext_ref, sh_ref):
    n = ext_ref.shape[0]
    for r in range(1, SUBLANES):
        sh_ref[r - 1, 0:n - SUBLANES, :] = ext_ref[pl.ds(r, n - SUBLANES), :]


def _shifted(ext_ref, sh_ref, start, rows, cols):
    q, r = divmod(start, SUBLANES)
    if r == 0:
        return ext_ref[pl.ds(start, rows), cols]
    return sh_ref[r - 1, pl.ds(q * SUBLANES, rows), cols]


def _ab_mid_fwd(proj, conv_w, conv_b, a_g, a_b, v_g, v_b, w_c, bias_full, tm):
    s = proj.shape[0]
    da = conv_b.shape[1]
    db = v_g.shape[1]
    nh = w_c.shape[0]
    kw = conv_w.shape[0]
    lead = CONV_HALO - (kw - 1)

    def body(p_ref, ph_ref, cw_ref, cb_ref, ag_ref, ab_ref, vg_ref, vb_ref, wc_ref, bias_ref, cat_ref, a1_ref, ext_ref,
             a1s_ref, sh_ref):
        i = pl.program_id(0)
        val = p_ref[:, 0:da].astype(F32)
        gat = p_ref[:, da:2 * da].astype(F32)
        ext_ref[CONV_HALO:, :] = val * jax.nn.sigmoid(gat)
        hv = ph_ref[:, 0:da].astype(F32)
        hg = ph_ref[:, da:2 * da].astype(F32)
        ext_ref[0:CONV_HALO, :] = jnp.where(i == 0, 0.0, hv * jax.nn.sigmoid(hg))
        _fill_shifted(ext_ref, sh_ref)
        for r0, c0, rb, cb in _chunks(tm, da, 64, 256):
            cols = pl.ds(c0, cb)
            acc = jnp.broadcast_to(cb_ref[:, cols], (rb, cb))
            for k in range(kw):
                acc = acc + cw_ref[k:k + 1, cols] * _shifted(ext_ref, sh_ref, r0 + lead + k, rb, cols)
            a1s_ref[pl.ds(r0, rb), cols] = acc
        a1 = a1s_ref[...]
        a1_ref[...] = a1.astype(BF16)
        z, _, _ = _layernorm_fwd(a1, ag_ref[...], ab_ref[...])
        cat_ref[:, 0:da] = (z * jax.nn.sigmoid(z)).astype(BF16)

        bu = p_ref[:, 2 * da:2 * da + db].astype(F32)
        bv = p_ref[:, 2 * da + db:].astype(F32)
        vn, _, _ = _layernorm_fwd(bv, vg_ref[...], vb_ref[...])
        vnb = vn.astype(BF16)
        for n in range(tm // CHUNK):
            rows = slice(n * CHUNK, (n + 1) * CHUNK)
            for h in range(nh):
                hc = slice(h * CHUNK, (h + 1) * CHUNK)
                vo = jnp.dot(wc_ref[h], vnb[rows, hc], preferred_element_type=F32) + bias_ref[:, hc]
                cat_ref[rows, da + h * CHUNK:da + (h + 1) * CHUNK] = (bu[rows, hc] * vo).astype(BF16)

    full = lambda a: pl.BlockSpec(a.shape, lambda i: (0,) * a.ndim)
    return pl.pallas_call(
        body, grid=(s // tm,), name="ab_mid_fwd",
        in_specs=[pl.BlockSpec((tm, 2 * da + 2 * db), lambda i: (i, 0)),
                  pl.BlockSpec((CONV_HALO, 2 * da), _prev_halo_map(tm, CONV_HALO)),
                  full(conv_w), full(conv_b), full(a_g), full(a_b), full(v_g), full(v_b), full(w_c), full(bias_full)],
        out_specs=[pl.BlockSpec((tm, da + db), lambda i: (i, 0)), pl.BlockSpec((tm, da), lambda i: (i, 0))],
        out_shape=[_sds((s, da + db), BF16), _sds((s, da), BF16)],
        scratch_shapes=[pltpu.VMEM((tm + CONV_HALO, da), F32), pltpu.VMEM((tm, da), F32),
                        pltpu.VMEM((SUBLANES - 1, tm + CONV_HALO, da), F32)],
        compiler_params=_params(("parallel",)),
    )(proj, proj, conv_w, conv_b, a_g, a_b, v_g, v_b, w_c, bias_full)


def _ab_mid_bwd(dcat, proj, a1, conv_w, a_g, a_b, v_g, v_b, w_c, bias_full, tm):
    s = proj.shape[0]
    da = a_g.shape[1]
    db = v_g.shape[1]
    nh = w_c.shape[0]
    kw = conv_w.shape[0]
    lead = CONV_HALO - (kw - 1)
    n_tiles = s // tm

    def body(dc_ref, dch_ref, p_ref, ph_ref, a1_ref, a1h_ref, cw_ref, ag_ref, ab_ref, vg_ref, vb_ref, wc_ref, bias_ref,
             dp_ref, dcw_ref, vec_ref, dws_ref, dbias_ref, ext_ref, dext_ref, dcw_acc, dvn_ref, sh_ref, dsh_ref):
        i = pl.program_id(0)

        @pl.when(i == 0)
        def _():
            dcw_acc[...] = jnp.zeros_like(dcw_acc)
            vec_ref[...] = jnp.zeros_like(vec_ref)
            dws_ref[...] = jnp.zeros_like(dws_ref)
            dbias_ref[...] = jnp.zeros_like(dbias_ref)

        agv, abv = ag_ref[...], ab_ref[...]

        def silu_ln_bwd(a1v, d_a2):
            z, yn, rstd = _layernorm_fwd(a1v, agv, abv)
            sg = jax.nn.sigmoid(z)
            return _layernorm_bwd(d_a2 * (sg * (1.0 + z * (1.0 - sg))), yn, rstd, agv)

        d_a1, dga, dba = silu_ln_bwd(a1_ref[...].astype(F32), dc_ref[:, 0:da])
        dext_ref[0:tm, :] = d_a1
        d_a1h, _, _ = silu_ln_bwd(a1h_ref[...].astype(F32), dch_ref[...])
        dext_ref[tm:, :] = jnp.where(i == n_tiles - 1, 0.0, d_a1h)
        vec_ref[0:1, 0:da] += dga
        vec_ref[1:2, 0:da] += dba
        vec_ref[2:3, 0:da] += jnp.sum(d_a1, axis=0, keepdims=True)

        val = p_ref[:, 0:da].astype(F32)
        sgg = jax.nn.sigmoid(p_ref[:, da:2 * da].astype(F32))
        ext_ref[CONV_HALO:, :] = val * sgg
        hv = ph_ref[:, 0:da].astype(F32)
        hg = ph_ref[:, da:2 * da].astype(F32)
        ext_ref[0:CONV_HALO, :] = jnp.where(i == 0, 0.0, hv * jax.nn.sigmoid(hg))

        _fill_shifted(ext_ref, sh_ref)
        _fill_shifted(dext_ref, dsh_ref)
        for r0, c0, rb, cb in _chunks(tm, da, 64, 256):
            cols = pl.ds(c0, cb)
            rows = pl.ds(r0, rb)
            d1 = dext_ref[rows, cols]
            acc = jnp.zeros((rb, cb), F32)
            for k in range(kw):
                acc = acc + cw_ref[k:k + 1, cols] * _shifted(dext_ref, dsh_ref, r0 + (kw - 1) - k, rb, cols)
                prod = d1 * _shifted(ext_ref, sh_ref, r0 + lead + k, rb, cols)
                dcw_acc[k, :, cols] += jnp.sum(prod.reshape(rb // SUBLANES, SUBLANES, cb), axis=0)
            v = p_ref[rows, pl.ds(c0, cb)].astype(F32)
            sg = jax.nn.sigmoid(p_ref[rows, pl.ds(da + c0, cb)].astype(F32))
            dp_ref[rows, pl.ds(c0, cb)] = (acc * sg).astype(BF16)
            dp_ref[rows, pl.ds(da + c0, cb)] = (acc * v * sg * (1.0 - sg)).astype(BF16)

        vgv = vg_ref[...]
        bu = p_ref[:, 2 * da:2 * da + db].astype(F32)
        bv = p_ref[:, 2 * da + db:].astype(F32)
        vn, yn_v, rstd_v = _layernorm_fwd(bv, vgv, vb_ref[...])
        vnb = vn.astype(BF16)
        for n in range(tm // CHUNK):
            rows = slice(n * CHUNK, (n + 1) * CHUNK)
            for h in range(nh):
                hc = slice(h * CHUNK, (h + 1) * CHUNK)
                wch = wc_ref[h]
                blk = vnb[rows, hc]
                vo = jnp.dot(wch, blk, preferred_element_type=F32) + bias_ref[:, hc]
                d_bout = dc_ref[rows, da + h * CHUNK:da + (h + 1) * CHUNK]
                dp_ref[rows, 2 * da + h * CHUNK:2 * da + (h + 1) * CHUNK] = (d_bout * vo).astype(BF16)
                d_vo = d_bout * bu[rows, hc]
                dbias_ref[h] += jnp.sum(d_vo, axis=1, keepdims=True)
                d_vob = d_vo.astype(BF16)
                dws_ref[h] += lax.dot_general(d_vob, blk, (((1,), (1,)), ((), ())), preferred_element_type=F32)
                dvn_ref[rows, hc] = lax.dot_general(wch, d_vob, (((0,), (0,)), ((), ())), preferred_element_type=F32)
        d_bv, dgv, dbv = _layernorm_bwd(dvn_ref[...], yn_v, rstd_v, vgv)
        dp_ref[:, 2 * da + db:] = d_bv.astype(BF16)
        vec_ref[3:4, 0:db] += dgv
        vec_ref[4:5, 0:db] += dbv

        @pl.when(i == n_tiles - 1)
        def _():
            dcw_ref[...] = jnp.sum(dcw_acc[...], axis=1)
            causal = lax.broadcasted_iota(jnp.int32, (CHUNK, CHUNK), 0) >= lax.broadcasted_iota(jnp.int32, (CHUNK, CHUNK), 1)
            for h in range(nh):
                dws_ref[h] = jnp.where(causal, dws_ref[h], 0.0)

    full = lambda a: pl.BlockSpec(a.shape, lambda i: (0,) * a.ndim)
    wide = max(da, db)
    return pl.pallas_call(
        body, grid=(n_tiles,), name="ab_mid_bwd",
        in_specs=[pl.BlockSpec((tm, da + db), lambda i: (i, 0)),
                  pl.BlockSpec((CONV_HALO, da), _next_halo_map(tm, CONV_HALO, s)),
                  pl.BlockSpec((tm, 2 * da + 2 * db), lambda i: (i, 0)),
                  pl.BlockSpec((CONV_HALO, 2 * da), _prev_halo_map(tm, CONV_HALO)),
                  pl.BlockSpec((tm, da), lambda i: (i, 0)),
                  pl.BlockSpec((CONV_HALO, da), _next_halo_map(tm, CONV_HALO, s)),
                  full(conv_w), full(a_g), full(a_b), full(v_g), full(v_b), full(w_c), full(bias_full)],
        out_specs=[pl.BlockSpec((tm, 2 * da + 2 * db), lambda i: (i, 0)),
                   pl.BlockSpec((kw, da), lambda i: (0, 0)),
                   pl.BlockSpec((8, wide), lambda i: (0, 0)),
                   pl.BlockSpec((nh, CHUNK, CHUNK), lambda i: (0, 0, 0)),
                   pl.BlockSpec((nh, CHUNK, 1), lambda i: (0, 0, 0))],
        out_shape=[_sds((s, 2 * da + 2 * db), BF16), _sds((kw, da), F32), _sds((8, wide), F32),
                   _sds((nh, CHUNK, CHUNK), F32), _sds((nh, CHUNK, 1), F32)],
        scratch_shapes=[pltpu.VMEM((tm + CONV_HALO, da), F32), pltpu.VMEM((tm + CONV_HALO, da), F32),
                        pltpu.VMEM((kw, SUBLANES, da), F32), pltpu.VMEM((tm, db), F32),
                        pltpu.VMEM((SUBLANES - 1, tm + CONV_HALO, da), F32), pltpu.VMEM((SUBLANES - 1, tm + CONV_HALO, da), F32)],
        compiler_params=_params(("arbitrary",)),
    )(dcat, dcat, proj, proj, a1, a1, conv_w, a_g, a_b, v_g, v_b, w_c, bias_full)


def _ada_fwd(c_all, w, b):
    nl, d, n = w.shape
    tn = _pick(n, (512, 256, 128))

    def body(c_ref, w_ref, b_ref, o_ref):
        cv = c_ref[...]
        cond = (cv * jax.nn.sigmoid(cv)).astype(BF16)
        o_ref[...] = jnp.dot(cond, w_ref[...].astype(BF16), preferred_element_type=F32) + b_ref[...]

    return pl.pallas_call(
        body, grid=(nl, n // tn), name="ada_fwd",
        in_specs=[pl.BlockSpec(c_all.shape, lambda l, j: (0, 0)), pl.BlockSpec((None, d, tn), lambda l, j: (l, 0, j)),
                  pl.BlockSpec((None, 1, tn), lambda l, j: (l, 0, j))],
        out_specs=pl.BlockSpec((None, c_all.shape[0], tn), lambda l, j: (l, 0, j)),
        out_shape=_sds((nl, c_all.shape[0], n), F32), compiler_params=_params(("parallel", "parallel")),
    )(c_all, w, b)


def _ada_bwd(c_all, dmod):
    nl, nb, n = dmod.shape
    d = c_all.shape[1]
    tn = _pick(n, (512, 256, 128))

    def body(c_ref, g_ref, o_ref):
        cv = c_ref[...]
        cond = (cv * jax.nn.sigmoid(cv)).astype(BF16)
        o_ref[...] = lax.dot_general(cond, g_ref[...].astype(BF16), (((0,), (0,)), ((), ())), preferred_element_type=F32)

    return pl.pallas_call(
        body, grid=(nl, n // tn), name="ada_bwd",
        in_specs=[pl.BlockSpec(c_all.shape, lambda l, j: (0, 0)), pl.BlockSpec((None, nb, tn), lambda l, j: (l, 0, j))],
        out_specs=pl.BlockSpec((None, d, tn), lambda l, j: (l, 0, j)),
        out_shape=_sds((nl, d, n), F32), compiler_params=_params(("parallel", "parallel")),
    )(c_all, dmod)


def _sum_leading(a, out_dtype=F32, name="sum_leading"):
    n, r, c = a.shape
    tr = _row_tile(r, c, 256 * 1024)

    def body(a_ref, o_ref):
        acc = a_ref[0].astype(F32)
        for k in range(1, n):
            acc = acc + a_ref[k].astype(F32)
        o_ref[...] = acc.astype(out_dtype)

    return pl.pallas_call(body, grid=(r // tr,), name=name, in_specs=[pl.BlockSpec((n, tr, c), lambda i: (0, i, 0))],
                          out_specs=pl.BlockSpec((tr, c), lambda i: (i, 0)), out_shape=_sds((r, c), out_dtype),
                          compiler_params=_params(("parallel",)))(a)


def _cast_into_full(w, kind, s_vec, l0, nl):
    _, r, c = w.shape
    tr = _row_tile(r, c, 512 * 1024)
    nb = r // tr
    if kind == "col":
        out_shape, out_spec = (nl, r, N_CHIP * c), pl.BlockSpec((None, tr, c), lambda l, i, sv: (l, i, sv[0]))
    else:
        out_shape, out_spec = (nl, N_CHIP * r, c), pl.BlockSpec((None, tr, c), lambda l, i, sv: (l, sv[0] * nb + i, 0))

    def body(sv_ref, w_ref, o_ref):
        o_ref[...] = w_ref[...].astype(BF16)

    return pl.pallas_call(
        body, name="cast_into_full",
        grid_spec=pltpu.PrefetchScalarGridSpec(num_scalar_prefetch=1, grid=(nl, nb),
                                               in_specs=[pl.BlockSpec((None, tr, c), lambda l, i, sv: (l0 + l, i, 0))], out_specs=out_spec),
        out_shape=_sds(out_shape, BF16), compiler_params=_params(("parallel", "parallel")),
    )(s_vec, w)


def _chip_sum_into(slab, cp, g, g_shape, layer, kind, sc_vec, rows_per_group=None):
    n, r, c = slab.shape
    rg = r if rows_per_group is None else rows_per_group
    tr = _row_tile(rg, c, 256 * 1024)
    groups = r // rg
    nbg = rg // tr
    nb = groups * nbg
    n_sc = len(sc_vec)
    if kind == "col":
        out_spec = pl.BlockSpec((None, tr, c), lambda gi, i, *sc: (layer, sc[1][0] * nb + gi * nbg + i, 0))
        own_spec = pl.BlockSpec((tr, c), lambda gi, i, *sc: (gi * nbg + i, sc[0][0]))
    else:
        out_spec = pl.BlockSpec((None, tr, c), lambda gi, i, *sc: (layer, gi * nbg + i, sc[1][0]))
        own_spec = pl.BlockSpec((tr, c), lambda gi, i, *sc: (gi * (n * nbg) + sc[0][0] * nbg + i, 0))

    def other(k):
        return pl.BlockSpec((None, tr, c), lambda gi, i, *sc: (sc[1 + k][0], gi * nbg + i, 0))

    in_specs = [own_spec] + [other(k) for k in range(1, n)]
    args = list(sc_vec) + [cp] + [slab] * (n - 1)
    aliases = {}
    if g is not None:
        in_specs.append(pl.BlockSpec(memory_space=pl.ANY))
        args.append(g)
        aliases = {len(args) - 1: 0}

    def body(*refs):
        own_ref, rest = refs[n_sc], refs[n_sc + 1:]
        o_ref = rest[-1]
        acc = own_ref[...].astype(F32)
        for k in range(n - 1):
            acc = acc + rest[k][...].astype(F32)
        o_ref[...] = acc

    return pl.pallas_call(
        body, name="chip_sum",
        grid_spec=pltpu.PrefetchScalarGridSpec(num_scalar_prefetch=n_sc, grid=(groups, nbg), in_specs=in_specs, out_specs=out_spec),
        out_shape=_sds(g_shape, F32), input_output_aliases=aliases, compiler_params=_params(("parallel", "parallel")),
    )(*args)


def _pair_add(dw, sib, kind, c_vec):
    r, c = sib.shape
    tr = _row_tile(r, c, 512 * 1024)
    nb = r // tr
    if kind == "col":
        dw_spec = pl.BlockSpec((tr, c), lambda i, cv: (cv[0] * nb + i, 0))
    else:
        dw_spec = pl.BlockSpec((tr, c), lambda i, cv: (i, cv[0]))

    def body(cv_ref, dw_ref, sib_ref, o_ref):
        o_ref[...] = (dw_ref[...].astype(F32) + sib_ref[...].astype(F32)).astype(BF16)

    return pl.pallas_call(
        body, name="pair_add",
        grid_spec=pltpu.PrefetchScalarGridSpec(num_scalar_prefetch=1, grid=(nb,), in_specs=[dw_spec, pl.BlockSpec((tr, c), lambda i, cv: (i, 0))],
                                               out_specs=pl.BlockSpec((tr, c), lambda i, cv: (i, 0))),
        out_shape=_sds((r, c), BF16), compiler_params=_params(("parallel",)),
    )(c_vec, dw, sib)


def _adamw(w, g, m, v):
    shape = w.shape
    cols = shape[-1]
    rows = w.size // cols
    tr = _row_tile(rows, cols, 256 * 1024)
    bc1 = 1.0 - ADAM_B1 ** ADAM_STEP
    bc2 = 1.0 - ADAM_B2 ** ADAM_STEP

    def body(w_ref, g_ref, m_ref, v_ref, d_ref, mo_ref, vo_ref):
        gv = g_ref[...]
        mn = ADAM_B1 * m_ref[...] + (1.0 - ADAM_B1) * gv
        vn = ADAM_B2 * v_ref[...] + (1.0 - ADAM_B2) * (gv * gv)
        d_ref[...] = -ADAM_LR * ((mn / bc1) / (jnp.sqrt(vn / bc2) + ADAM_EPS) + ADAM_WD * w_ref[...])
        mo_ref[...] = mn
        vo_ref[...] = vn

    spec = pl.BlockSpec((tr, cols), lambda i: (i, 0))
    outs = pl.pallas_call(body, grid=(rows // tr,), name="adamw", in_specs=[spec] * 4, out_specs=[spec] * 3,
                          out_shape=[_sds((rows, cols), F32)] * 3, compiler_params=_params(("parallel",)))(
        *[a.reshape(rows, cols) for a in (w, g, m, v)])
    return [o.reshape(shape) for o in outs]


def _mesh_pos():
    return lax.axis_index("x"), lax.axis_index("y"), lax.axis_index("c")


def _other_chips(x, y):
    return [(1 - x, y), (x, 1 - y), (1 - x, 1 - y)]


def _allgather_small(a):
    r, c = a.shape

    def body(x_ref, out_ref, send_sems, recv_sems, local_sem):
        x, y, cc = _mesh_pos()
        me, sibling = (x, y, cc), (x, y, 1 - cc)
        chips = _other_chips(x, y)

        def slab(px, py, pc):
            return out_ref.at[4 * px + 2 * py + pc]

        def copy(k, block, to, src=None):
            return pltpu.make_async_remote_copy(src_ref=slab(*block) if src is None else src, dst_ref=slab(*block),
                                                send_sem=send_sems.at[k], recv_sem=recv_sems.at[k], device_id=to,
                                                device_id_type=MESH)

        mine = pltpu.make_async_copy(x_ref, slab(*me), local_sem)
        mine.start()
        first = [copy(0, me, sibling, src=x_ref)]
        first += [copy(1 + j, me, (*chip, cc), src=x_ref) for j, chip in enumerate(chips)]
        for cp in first:
            cp.start()
        passed = [copy(4 + j, (*chip, cc), sibling) for j, chip in enumerate(chips)]
        for j, chip in enumerate(chips):
            copy(1 + j, (*chip, cc), me).wait_recv()
            passed[j].start()
        copy(0, sibling, me).wait_recv()
        for j, chip in enumerate(chips):
            copy(4 + j, (*chip, 1 - cc), me).wait_recv()
        for cp in first + passed:
            cp.wait_send()
        mine.wait()

    return pl.pallas_call(
        body, name="allgather_small", out_shape=_sds((N_DEV, r, c), F32),
        in_specs=[pl.BlockSpec(memory_space=pltpu.VMEM)], out_specs=pl.BlockSpec(memory_space=pltpu.VMEM),
        scratch_shapes=[pltpu.SemaphoreType.DMA((7,)), pltpu.SemaphoreType.DMA((7,)), pltpu.SemaphoreType.DMA],
        compiler_params=pltpu.CompilerParams(vmem_limit_bytes=V7X_VMEM_LIMIT_BYTES),
    )(a)


def _idx(ref, rows=None, cols=None):
    lead = (slice(None),) * (len(ref.shape) - 2)
    return ref.at[lead + (slice(None) if rows is None else rows, slice(None) if cols is None else cols)]


def _half(ref, kind, c):
    r, cdim = ref.shape[-2:]
    if kind == "col":
        return _idx(ref, rows=pl.ds(c * (r // 2), r // 2))
    return _idx(ref, cols=pl.ds(c * (cdim // 2), cdim // 2))


def _shard_region(full, kind, shard_shape, s):
    r, cdim = shard_shape[-2:]
    if kind == "col":
        return _idx(full, cols=pl.ds(s * cdim, cdim))
    return _idx(full, rows=pl.ds(s * r, r))


def _allgather_weights(fulls, kinds, shard_shapes):
    nt = len(fulls)

    def body(*refs):
        fu = refs[nt:2 * nt]
        send_sems, recv_sems = refs[2 * nt:]
        x, y, c = _mesh_pos()
        chips = _other_chips(x, y)

        def part(t, chip, cc):
            return _half(_shard_region(fu[t], kinds[t], shard_shapes[t], 2 * chip[0] + chip[1]), kinds[t], cc)

        def copy(t, k, blk, to):
            return pltpu.make_async_remote_copy(src_ref=blk, dst_ref=blk, send_sem=send_sems.at[6 * t + k],
                                                recv_sem=recv_sems.at[6 * t + k], device_id=to, device_id_type=MESH)

        first, passed = [], []
        for t in range(nt):
            for j, chip in enumerate(chips):
                cp = copy(t, j, part(t, (x, y), c), (*chip, c))
                cp.start()
                first.append(cp)
        for t in range(nt):
            for j, chip in enumerate(chips):
                copy(t, j, part(t, chip, c), (x, y, c)).wait_recv()
                fw = copy(t, 3 + j, part(t, chip, c), (x, y, 1 - c))
                fw.start()
                passed.append(fw)
        for t in range(nt):
            for j, chip in enumerate(chips):
                copy(t, 3 + j, part(t, chip, 1 - c), (x, y, c)).wait_recv()
        for cp in first + passed:
            cp.wait_send()

    return pl.pallas_call(
        body, name="allgather_weights", out_shape=[_sds(f.shape, BF16) for f in fulls],
        in_specs=[HBM] * nt, out_specs=[HBM] * nt, input_output_aliases={t: t for t in range(nt)},
        scratch_shapes=[pltpu.SemaphoreType.DMA((6 * nt,)), pltpu.SemaphoreType.DMA((6 * nt,))],
    )(*fulls)


SEM = pl.BlockSpec(memory_space=pltpu.SEMAPHORE)
ANY = pl.BlockSpec(memory_space=pl.ANY)
EFFECT = pltpu.SideEffectType.DATAFLOW_SIDE_EFFECTING


def _gather_start(name, fulls, kinds, shard_shapes, prev):
    nt = len(fulls)

    def body(*refs):
        send_sems, recv_sems = refs[nt + 1], refs[nt + 2]
        fu = refs[nt + 3:2 * nt + 3]
        token = refs[2 * nt + 3]
        x, y, c = _mesh_pos()
        for t in range(nt):
            mine = _half(_shard_region(fu[t], kinds[t], shard_shapes[t], 2 * x + y), kinds[t], c)
            for j, chip in enumerate(_other_chips(x, y)):
                for e in range(2):
                    pltpu.make_async_remote_copy(src_ref=mine, dst_ref=mine, send_sem=send_sems.at[6 * t + 2 * j + e],
                                                 recv_sem=recv_sems.at[6 * t + 2 * j + c], device_id=(*chip, e),
                                                 device_id_type=MESH).start()
        token[...] = jnp.zeros_like(token)

    outs = pl.pallas_call(
        body, name=name,
        out_shape=(pltpu.SemaphoreType.DMA((6 * nt,)), pltpu.SemaphoreType.DMA((6 * nt,)), *[pltpu.HBM(f.shape, f.dtype) for f in fulls],
                   _sds((8, 128), F32)),
        in_specs=[HBM] * nt + [ANY], out_specs=(SEM, SEM, *[HBM] * nt, pl.BlockSpec(memory_space=pltpu.VMEM)),
        input_output_aliases={t: 2 + t for t in range(nt)}, compiler_params=pltpu.CompilerParams(has_side_effects=EFFECT),
    )(*fulls, prev)
    return outs[0], outs[1], list(outs[2:2 + nt]), outs[2 + nt]


def _gather_wait(name, send_sems, recv_sems, fulls, kinds, shard_shapes, after):
    nt = len(fulls)

    def body(*refs):
        fu = refs[:nt]
        send_sems, recv_sems = refs[nt], refs[nt + 1]
        x, y, c = _mesh_pos()

        def part(t, chip, cc):
            return _half(_shard_region(fu[t], kinds[t], shard_shapes[t], 2 * chip[0] + chip[1]), kinds[t], cc)

        for t in range(nt):
            for j, chip in enumerate(_other_chips(x, y)):
                for e in range(2):
                    mine = part(t, (x, y), c)
                    pltpu.make_async_remote_copy(src_ref=mine, dst_ref=mine, send_sem=send_sems.at[6 * t + 2 * j + e],
                                                 recv_sem=recv_sems.at[6 * t + 2 * j + e], device_id=(x, y, c),
                                                 device_id_type=MESH).wait_send()
                    landed = part(t, chip, e)
                    pltpu.make_async_remote_copy(src_ref=landed, dst_ref=landed, send_sem=send_sems.at[6 * t + 2 * j + e],
                                                 recv_sem=recv_sems.at[6 * t + 2 * j + e], device_id=(x, y, c),
                                                 device_id_type=MESH).wait_recv()

    return pl.pallas_call(
        body, name=name, out_shape=[pltpu.HBM(f.shape, f.dtype) for f in fulls], in_specs=[HBM] * nt + [SEM, SEM, ANY],
        out_specs=[HBM] * nt, input_output_aliases={t: t for t in range(nt)},
        compiler_params=pltpu.CompilerParams(has_side_effects=EFFECT),
    )(*fulls, send_sems, recv_sems, after)


def _reduce_start(name, cps, kinds, prev):
    nt = len(cps)
    slab_shapes = []
    for cp, kind in zip(cps, kinds):
        shp = list(cp.shape)
        shp[-1 if kind == "col" else -2] //= N_CHIP
        slab_shapes.append((N_CHIP,) + tuple(shp))

    def body(*refs):
        send_sems, recv_sems = refs[nt + 1], refs[nt + 2]
        src = refs[nt + 3:2 * nt + 3]
        dst = refs[2 * nt + 3:3 * nt + 3]
        token = refs[3 * nt + 3]
        x, y, c = _mesh_pos()
        s = 2 * x + y
        for t in range(nt):
            for j, chip in enumerate(_other_chips(x, y)):
                pltpu.make_async_remote_copy(src_ref=_chip_block(src[t], kinds[t], 2 * chip[0] + chip[1]), dst_ref=dst[t].at[s],
                                             send_sem=send_sems.at[3 * t + j], recv_sem=recv_sems.at[3 * t + j],
                                             device_id=(*chip, c), device_id_type=MESH).start()
        token[...] = jnp.zeros_like(token)

    outs = pl.pallas_call(
        body, name=name,
        out_shape=(pltpu.SemaphoreType.DMA((3 * nt,)), pltpu.SemaphoreType.DMA((3 * nt,)), *[pltpu.HBM(a.shape, a.dtype) for a in cps],
                   *[pltpu.HBM(shp, BF16) for shp in slab_shapes], _sds((8, 128), F32)),
        in_specs=[HBM] * nt + [ANY], out_specs=(SEM, SEM, *[HBM] * (2 * nt), pl.BlockSpec(memory_space=pltpu.VMEM)),
        input_output_aliases={t: 2 + t for t in range(nt)}, compiler_params=pltpu.CompilerParams(has_side_effects=EFFECT),
    )(*cps, prev)
    return outs[0], outs[1], list(outs[2:2 + nt]), list(outs[2 + nt:2 + 2 * nt]), outs[2 + 2 * nt]


def _reduce_wait(name, send_sems, recv_sems, cps, slabs, kinds, after):
    nt = len(cps)

    def body(*refs):
        src, dst = refs[:nt], refs[nt:2 * nt]
        send_sems, recv_sems = refs[2 * nt], refs[2 * nt + 1]
        x, y, c = _mesh_pos()
        for t in range(nt):
            for j, chip in enumerate(_other_chips(x, y)):
                sj = 2 * chip[0] + chip[1]
                pltpu.make_async_remote_copy(src_ref=_chip_block(src[t], kinds[t], sj), dst_ref=dst[t].at[sj],
                                             send_sem=send_sems.at[3 * t + j], recv_sem=recv_sems.at[3 * t + j],
                                             device_id=(x, y, c), device_id_type=MESH).wait()

    outs = pl.pallas_call(
        body, name=name, out_shape=[pltpu.HBM(a.shape, a.dtype) for a in cps] + [pltpu.HBM(a.shape, a.dtype) for a in slabs],
        in_specs=[HBM] * (2 * nt) + [SEM, SEM, ANY], out_specs=[HBM] * (2 * nt), input_output_aliases={t: t for t in range(2 * nt)},
        compiler_params=pltpu.CompilerParams(has_side_effects=EFFECT),
    )(*cps, *slabs, send_sems, recv_sems, after)
    return list(outs[:nt]), list(outs[nt:])


def _pair_start(name, dws, kinds, prev):
    nt = len(dws)
    sib_shapes = []
    for dw, kind in zip(dws, kinds):
        shp = list(dw.shape)
        shp[-2 if kind == "col" else -1] //= 2
        sib_shapes.append(tuple(shp))

    def body(*refs):
        send_sems, recv_sems = refs[nt + 1], refs[nt + 2]
        src = refs[nt + 3:2 * nt + 3]
        dst = refs[2 * nt + 3:3 * nt + 3]
        token = refs[3 * nt + 3]
        x, y, c = _mesh_pos()
        for t in range(nt):
            pltpu.make_async_remote_copy(src_ref=_half(src[t], kinds[t], 1 - c), dst_ref=dst[t], send_sem=send_sems.at[t],
                                         recv_sem=recv_sems.at[t], device_id=(x, y, 1 - c), device_id_type=MESH).start()
        token[...] = jnp.zeros_like(token)

    outs = pl.pallas_call(
        body, name=name,
        out_shape=(pltpu.SemaphoreType.DMA((nt,)), pltpu.SemaphoreType.DMA((nt,)), *[pltpu.HBM(a.shape, a.dtype) for a in dws],
                   *[pltpu.HBM(shp, BF16) for shp in sib_shapes], _sds((8, 128), F32)),
        in_specs=[HBM] * nt + [ANY], out_specs=(SEM, SEM, *[HBM] * (2 * nt), pl.BlockSpec(memory_space=pltpu.VMEM)),
        input_output_aliases={t: 2 + t for t in range(nt)}, compiler_params=pltpu.CompilerParams(has_side_effects=EFFECT),
    )(*dws, prev)
    return outs[0], outs[1], list(outs[2:2 + nt]), list(outs[2 + nt:2 + 2 * nt]), outs[2 + 2 * nt]


def _pair_wait(name, send_sems, recv_sems, dws, sibs, kinds, after):
    nt = len(dws)

    def body(*refs):
        src, dst = refs[:nt], refs[nt:2 * nt]
        send_sems, recv_sems = refs[2 * nt], refs[2 * nt + 1]
        x, y, c = _mesh_pos()
        for t in range(nt):
            pltpu.make_async_remote_copy(src_ref=_half(src[t], kinds[t], 1 - c), dst_ref=dst[t], send_sem=send_sems.at[t],
                                         recv_sem=recv_sems.at[t], device_id=(x, y, c), device_id_type=MESH).wait()

    outs = pl.pallas_call(
        body, name=name, out_shape=[pltpu.HBM(a.shape, a.dtype) for a in dws] + [pltpu.HBM(a.shape, a.dtype) for a in sibs],
        in_specs=[HBM] * (2 * nt) + [SEM, SEM, ANY], out_specs=[HBM] * (2 * nt), input_output_aliases={t: t for t in range(2 * nt)},
        compiler_params=pltpu.CompilerParams(has_side_effects=EFFECT),
    )(*dws, *sibs, send_sems, recv_sems, after)
    return list(outs[:nt]), list(outs[nt:])


def _pair_exchange(dws, kinds):
    nt = len(dws)
    out_shapes = []
    for dw, kind in zip(dws, kinds):
        shp = list(dw.shape)
        shp[-2 if kind == "col" else -1] //= 2
        out_shapes.append(tuple(shp))

    def body(*refs):
        src, dst = refs[:nt], refs[nt:2 * nt]
        send_sems, recv_sems = refs[2 * nt:]
        x, y, c = _mesh_pos()
        copies = [pltpu.make_async_remote_copy(src_ref=_half(src[t], kinds[t], 1 - c), dst_ref=dst[t], send_sem=send_sems.at[t],
                                               recv_sem=recv_sems.at[t], device_id=(x, y, 1 - c), device_id_type=MESH)
                  for t in range(nt)]
        for cp in copies:
            cp.start()
        for cp in copies:
            cp.wait()

    return pl.pallas_call(
        body, name="grad_pair_exchange", out_shape=[_sds(shp, BF16) for shp in out_shapes], in_specs=[HBM] * nt,
        out_specs=[HBM] * nt, scratch_shapes=[pltpu.SemaphoreType.DMA((nt,)), pltpu.SemaphoreType.DMA((nt,))],
    )(*dws)


def _chip_block(ref, kind, s):
    r, cdim = ref.shape[-2:]
    if kind == "col":
        return _idx(ref, cols=pl.ds(s * (cdim // N_CHIP), cdim // N_CHIP))
    return _idx(ref, rows=pl.ds(s * (r // N_CHIP), r // N_CHIP))


def _chip_exchange(cps, kinds):
    nt = len(cps)
    out_shapes = []
    for cp, kind in zip(cps, kinds):
        shp = list(cp.shape)
        shp[-1 if kind == "col" else -2] //= N_CHIP
        out_shapes.append((N_CHIP,) + tuple(shp))

    def body(*refs):
        src, dst = refs[:nt], refs[nt:2 * nt]
        send_sems, recv_sems, local_sems = refs[2 * nt:]
        x, y, c = _mesh_pos()
        s = 2 * x + y
        chips = _other_chips(x, y)
        sends, locals_ = [], []
        for t in range(nt):
            own = pltpu.make_async_copy(_chip_block(src[t], kinds[t], s), dst[t].at[s], local_sems.at[t])
            own.start()
            locals_.append(own)
            for j, chip in enumerate(chips):
                cp = pltpu.make_async_remote_copy(src_ref=_chip_block(src[t], kinds[t], 2 * chip[0] + chip[1]), dst_ref=dst[t].at[s],
                                                  send_sem=send_sems.at[3 * t + j], recv_sem=recv_sems.at[3 * t + j],
                                                  device_id=(*chip, c), device_id_type=MESH)
                cp.start()
                sends.append(cp)
        for t in range(nt):
            for j, chip in enumerate(chips):
                landing = dst[t].at[2 * chip[0] + chip[1]]
                pltpu.make_async_remote_copy(src_ref=landing, dst_ref=landing, send_sem=send_sems.at[3 * t + j],
                                             recv_sem=recv_sems.at[3 * t + j], device_id=(x, y, c), device_id_type=MESH).wait_recv()
        for cp in sends:
            cp.wait_send()
        for own in locals_:
            own.wait()

    return pl.pallas_call(
        body, name="grad_chip_exchange", out_shape=[_sds(shp, BF16) for shp in out_shapes], in_specs=[HBM] * nt,
        out_specs=[HBM] * nt,
        scratch_shapes=[pltpu.SemaphoreType.DMA((3 * nt,)), pltpu.SemaphoreType.DMA((3 * nt,)), pltpu.SemaphoreType.DMA((nt,))],
    )(*cps)


def _pair_assemble(gs, kinds):
    nt = len(gs)

    def body(*refs):
        g = refs[nt:2 * nt]
        send_sems, recv_sems = refs[2 * nt:]
        x, y, c = _mesh_pos()
        copies = []
        for t in range(nt):
            mine = _half(g[t], kinds[t], c)
            cp = pltpu.make_async_remote_copy(src_ref=mine, dst_ref=mine, send_sem=send_sems.at[t], recv_sem=recv_sems.at[t],
                                              device_id=(x, y, 1 - c), device_id_type=MESH)
            cp.start()
            copies.append(cp)
        for t in range(nt):
            landing = _half(g[t], kinds[t], 1 - c)
            pltpu.make_async_remote_copy(src_ref=landing, dst_ref=landing, send_sem=send_sems.at[t], recv_sem=recv_sems.at[t],
                                         device_id=(x, y, c), device_id_type=MESH).wait_recv()
        for cp in copies:
            cp.wait_send()

    return pl.pallas_call(
        body, name="grad_pair_assemble", out_shape=[_sds(a.shape, F32) for a in gs], in_specs=[HBM] * nt,
        out_specs=[HBM] * nt, input_output_aliases={t: t for t in range(nt)},
        scratch_shapes=[pltpu.SemaphoreType.DMA((nt,)), pltpu.SemaphoreType.DMA((nt,))],
    )(*gs)


def _pack(arrays, width):
    flat = jnp.concatenate([a.reshape(-1) for a in arrays])
    pad = (-flat.size) % (8 * width)
    return jnp.pad(flat, (0, pad)).reshape(-1, width)


def _unpack(packed, shapes):
    flat = packed.reshape(-1)
    out, off = [], 0
    for shp in shapes:
        n = 1
        for dim in shp:
            n *= dim
        out.append(flat[off:off + n].reshape(shp))
        off += n
    return out


def kernel(x, c, ada_w, ada_b, norm_mix_g, norm_ffn_g, ab_w_in, a_conv_w, a_conv_b, a_norm_g, a_norm_b, b_norm_g, b_norm_b, b_w_s, b_bias, ab_w_out, pool_w, pool_scale, ffn_w1, ffn_w3, ffn_w2, final_g, loss_target, m_ada_w, m_ada_b, m_norm_mix_g, m_norm_ffn_g, m_ab_w_in, m_a_conv_w, m_a_conv_b, m_a_norm_g, m_a_norm_b, m_b_norm_g, m_b_norm_b, m_b_w_s, m_b_bias, m_ab_w_out, m_pool_w, m_pool_scale, m_ffn_w1, m_ffn_w3, m_ffn_w2, m_final_g, v_ada_w, v_ada_b, v_norm_mix_g, v_norm_ffn_g, v_ab_w_in, v_a_conv_w, v_a_conv_b, v_a_norm_g, v_a_norm_b, v_b_norm_g, v_b_norm_b, v_b_w_s, v_b_bias, v_ab_w_out, v_pool_w, v_pool_scale, v_ffn_w1, v_ffn_w3, v_ffn_w2, v_final_g):
    mx, my, mc = _mesh_pos()
    chip = 2 * mx + my
    dev = 4 * mx + 2 * my + mc
    x2 = x[0]
    target = loss_target[0]
    s, d = x2.shape
    depth = ada_w.shape[0]
    n_mod = ada_b.shape[1] // d
    n_even = ab_w_in.shape[0]
    da, db = a_conv_b.shape[1], b_norm_g.shape[1]
    nh = b_w_s.shape[1]
    kw = a_conv_w.shape[1]
    n_pool = pool_w.shape[1]
    tm_row = _pick(s, (256, 128))

    s_vec = jnp.reshape(chip, (1,)).astype(jnp.int32)
    c_vec = jnp.reshape(mc, (1,)).astype(jnp.int32)
    sc_vec = [s_vec, c_vec] + [jnp.reshape(v, (1,)).astype(jnp.int32)
                               for v in (2 * mx + (1 - my), 2 * (1 - mx) + my, 2 * (1 - mx) + (1 - my))]
    pool_w3 = pool_w.reshape((-1,) + pool_w.shape[2:])
    shard_of = {"w_in": (ab_w_in, "col"), "w_out": (ab_w_out, "row"), "pool": (pool_w3, "row"), "w1": (ffn_w1, "col"),
                "w3": (ffn_w3, "col"), "w2": (ffn_w2, "row")}

    def layer_names(l):
        return (["w_in", "w_out"] if l % 2 == 0 else ["pool"]) + ["w1", "w3", "w2"]

    def layer_span(nm, l):
        if nm in ("w_in", "w_out"):
            return l // 2, 1
        if nm == "pool":
            return (l // 2) * n_pool, n_pool
        return l, 1

    owned, w_kinds, w_shapes = [], [], []
    for l in range(depth):
        names = layer_names(l)
        owned.append([_cast_into_full(shard_of[nm][0], shard_of[nm][1], s_vec, *layer_span(nm, l)) for nm in names])
        w_kinds.append([shard_of[nm][1] for nm in names])
        w_shapes.append([(layer_span(nm, l)[1],) + shard_of[nm][0].shape[1:] for nm in names])
    layer_w = [None] * depth
    layer_w[0] = dict(zip(layer_names(0), _allgather_weights(owned[0], w_kinds[0], w_shapes[0])))
    gather_prev = layer_w[0]["w2"]

    pre = _allgather_small(_pack([c, a_conv_w, pool_scale], 128)).reshape(N_DEV, -1)
    n_cw, n_ps = a_conv_w.size, pool_scale.size
    c_all = pre[:, :d]
    cw_chips = pre[0::2, d:d + n_cw].reshape((N_CHIP,) + a_conv_w.shape)
    conv_w_full = jnp.concatenate([cw_chips[k] for k in range(N_CHIP)], axis=-1)
    ps_chips = pre[0::2, d + n_cw:d + n_cw + n_ps].reshape((N_CHIP,) + pool_scale.shape)
    pool_scale_full = jnp.concatenate([ps_chips[k] for k in range(N_CHIP)], axis=-1)
    c_pad = jnp.pad(c_all, ((0, 8), (0, 0)))
    n_ada = ada_w.shape[2]
    ada_b_mine = lax.dynamic_slice_in_dim(ada_b, chip * n_ada, n_ada, axis=1)[:, None, :]
    mod_part = _ada_fwd(c_pad, ada_w, ada_b_mine)[:, :N_DEV, :]
    mod_all = _allgather_small(mod_part.reshape(depth * N_DEV, n_ada))
    mod_chips = mod_all[0::2].reshape(N_CHIP, depth, N_DEV, n_ada)
    mod_mine = lax.dynamic_index_in_dim(mod_chips, dev, axis=2, keepdims=False)
    mod = jnp.transpose(mod_mine, (1, 0, 2)).reshape(depth, n_mod, 1, d)

    causal = jnp.tril(jnp.ones((CHUNK, CHUNK), dtype=bool))
    w_c = jnp.where(causal[None, None], b_w_s, 0.0).astype(BF16)
    bias_full = jnp.repeat(jnp.swapaxes(b_bias, 1, 2), CHUNK, axis=2)

    saved = []
    xs = x2
    for l in range(depth):
        sh1, sc1, g1, sh2, sc2, g2 = [mod[l, k] for k in range(n_mod)]
        i = l // 2
        st = {"x1": xs}
        gain1 = norm_mix_g[l][None]
        if l + 1 < depth:
            in_flight = _gather_start(f"gather_start_{l + 1}", owned[l + 1], w_kinds[l + 1], w_shapes[l + 1], gather_prev)
            gain1 = gain1 + in_flight[3][0:1, 0:1]
        wl = layer_w[l]
        if l % 2 == 0:
            h = _norm_mod(xs, gain1, sh1, sc1, tm_row)
            proj = _mm("ab_proj", [h], [(wl["w_in"], 0)], [BF16], _ep_store, tk=2048)[0]
            cat, a1 = _ab_mid_fwd(proj, conv_w_full[i], a_conv_b[i][None], a_norm_g[i][None], a_norm_b[i][None],
                                  b_norm_g[i][None], b_norm_b[i][None], w_c[i], bias_full[i], tm_row)
            xs, y1 = _mm("ab_out", [cat], [(wl["w_out"], 0)], [F32, BF16], _ep_residual, extras=[xs, g1], extra_kinds=["tile", "row"], tk=2048)
            st.update(h=h, proj=proj, a1=a1, cat=cat, y=y1)
        else:
            p = _pool_fwd(xs, gain1, sh1, sc1, tm_row)
            gate = g1 * pool_scale_full[i][None]
            xs, ymm = _grouped_fwd(p, wl["pool"][None], 0, xs, gate, 1024)
            st.update(p=p, y=ymm, gate=gate)
        st["x2"] = xs
        h2 = _norm_mod(xs, norm_ffn_g[l][None], sh2, sc2, tm_row)
        u, t, z = _mm("ffn_up", [h2, h2], [(wl["w1"], 0), (wl["w3"], 0)], [BF16, BF16, BF16], _ep_swiglu, tn=512, tk=2048)
        xs, y2 = _mm("ffn_down", [z], [(wl["w2"], 0)], [F32, BF16], _ep_residual, extras=[xs, g2], extra_kinds=["tile", "row"],
                     tn=512, tk=ffn_w2.shape[1] * N_CHIP)
        if l + 1 < depth:
            landed = _gather_wait(f"gather_wait_{l + 1}", in_flight[0], in_flight[1], in_flight[2], w_kinds[l + 1], w_shapes[l + 1], xs)
            layer_w[l + 1] = dict(zip(layer_names(l + 1), landed))
            gather_prev = landed[-1]
        st.update(h2=h2, u=u, t=t, z=z, y2=y2)
        saved.append(st)

    def below_of(l, which):
        if which == "ffn":
            return saved[l]["y2"], mod[l, n_mod - 1]
        return saved[l]["y"], (mod[l, 2] if l % 2 == 0 else saved[l]["gate"])

    dx, fin_acc, loss_blk, dyb, gacc = _final_loss_bwd(xs, final_g[None], target, tm_row, below_of(depth - 1, "ffn"))
    loss = lax.psum(loss_blk[0, 0], MESH_AXES)
    d_final_g = fin_acc[0]
    dmod_rows = [None] * depth
    d_norm_mix, d_norm_ffn = [None] * depth, [None] * depth
    even_small = [None] * n_even
    d_pool_scale = [None] * (depth // 2)
    grad_names = ["w_in", "w_out", "pool", "w1", "w3", "w2"]
    g_shapes = {"w_in": ab_w_in.shape, "w_out": ab_w_out.shape, "pool": (pool_w.shape[0], n_pool * pool_w.shape[2], pool_w.shape[3]),
                "w1": ffn_w1.shape, "w3": ffn_w3.shape, "w2": ffn_w2.shape}
    shard_grads = {nm: None for nm in grad_names}
    pipe = {"pair": None, "chip": None, "prev": None, "token": None}

    def finish_chip(after):
        tag, pl_, names_, kinds_, sends, recvs, cps_f, slabs_f = pipe["chip"]
        cps_d, slabs_d = _reduce_wait(f"reduce_wait_{tag}", sends, recvs, cps_f, slabs_f, kinds_, after)
        for nm, kind, cp, sl in zip(names_, kinds_, cps_d, slabs_d):
            rpg = pool_w.shape[2] if nm == "pool" else None
            cp2 = cp.reshape(-1, cp.shape[-1])
            shard_grads[nm] = _chip_sum_into(sl.reshape(N_CHIP, -1, sl.shape[-1]), cp2, shard_grads[nm], g_shapes[nm],
                                             layer_span(nm, pl_)[0] // (n_pool if nm == "pool" else 1), kind, sc_vec, rpg)
        pipe["chip"] = None
        pipe["prev"] = slabs_d[-1]

    def advance(after, new=None):
        if pipe["pair"] is not None:
            tag, lyr, names, kinds_l, sends, recvs, dws_f, sibs_f = pipe["pair"]
            dws_d, sibs_d = _pair_wait(f"pair_wait_{tag}", sends, recvs, dws_f, sibs_f, kinds_l, after)
            cps = []
            for dw, sib, kind in zip(dws_d, sibs_d, kinds_l):
                cp = _pair_add(dw.reshape(-1, dw.shape[-1]), sib.reshape(-1, sib.shape[-1]), kind, c_vec)
                cps.append(cp.reshape(sib.shape))
            if pipe["chip"] is not None:
                finish_chip(after)
            prev = cps[-1] if pipe["prev"] is None else pipe["prev"]
            sends, recvs, cps_f, slabs_f, token = _reduce_start(f"reduce_start_{tag}", cps, kinds_l, prev)
            pipe["chip"] = (tag, lyr, names, kinds_l, sends, recvs, cps_f, slabs_f)
            pipe["pair"] = None
            pipe["token"] = token
        elif pipe["chip"] is not None:
            finish_chip(after)
        if new is not None:
            tag, lyr, names, big_ = new
            kinds_l = [shard_of[nm][1] for nm in names]
            dws = [big_[nm] for nm in names]
            prev = dws[-1] if pipe["token"] is None else pipe["token"]
            sends, recvs, dws_f, sibs_f, token = _pair_start(f"pair_start_{tag}", dws, kinds_l, prev)
            pipe["pair"] = (tag, lyr, names, kinds_l, sends, recvs, dws_f, sibs_f)
            pipe["token"] = token

    def behind(row):
        return row + pipe["token"][0:1, 0:1]

    for l in reversed(range(depth)):
        sh1, sc1, g1, sh2, sc2, g2 = [mod[l, k] for k in range(n_mod)]
        st = saved[l]
        wl = layer_w[l]
        i = l // 2
        big = {}
        d_g2 = gacc[0]
        du, dt = _mm("ffn_dz", [dyb], [(wl["w2"], 0)], [BF16, BF16], _ep_swiglu_bwd, trans_b=True, extras=[st["u"], st["t"]],
                     extra_kinds=["tile", "tile"], tm=2048, tn=512, tk=2048)
        big["w2"] = _mm("ffn_dw2", [st["z"]], [dyb], [BF16], _ep_store, trans_a=True, tm=512, tn=512, tk=s)[0]
        big["w1"] = _mm("ffn_dw1", [st["h2"]], [du], [BF16], _ep_store, trans_a=True, tm=512, tn=512, tk=s)[0]
        big["w3"] = _mm("ffn_dw3", [st["h2"]], [dt], [BF16], _ep_store, trans_a=True, tm=512, tn=512, tk=s)[0]
        advance(big["w3"], (f"ffn{l}", l, ["w1", "w3", "w2"], big))
        dh = _mm("ffn_dh", [du, dt], [(wl["w1"], 0), (wl["w3"], 0)], [F32], _ep_sum, trans_b=True, tn=512, tk=ffn_w2.shape[1] * 2)[0]
        dx, nacc, dyb, gacc = _norm_mod_bwd(dh, st["x2"], dx, behind(norm_ffn_g[l][None]), sc2, tm_row, below_of(l, "mix"))
        d_sh2, d_sc2, d_norm_ffn[l] = nacc[0], nacc[1], nacc[2]
        if l % 2 == 0:
            d_g1 = gacc[0]
            big["w_out"] = _mm("ab_dw_out", [st["cat"]], [dyb], [BF16], _ep_store, trans_a=True, tm=512, tn=512, tk=s)[0]
            dcat = _mm("ab_dcat", [dyb], [(wl["w_out"], 0)], [F32], _ep_store, trans_b=True, tk=2048)[0]
            dproj, dcw, vecs, dws, dbias = _ab_mid_bwd(dcat, st["proj"], st["a1"], conv_w_full[i], a_norm_g[i][None],
                                                       a_norm_b[i][None], b_norm_g[i][None], b_norm_b[i][None], w_c[i],
                                                       bias_full[i], tm_row)
            even_small[i] = dict(conv_w=dcw, a_norm_g=vecs[0, :da], a_norm_b=vecs[1, :da], conv_b=vecs[2, :da],
                                 b_norm_g=vecs[3, :db], b_norm_b=vecs[4, :db], w_s=dws, bias=dbias[:, :, 0])
            big["w_in"] = _mm("ab_dw_in", [st["h"]], [dproj], [BF16], _ep_store, trans_a=True, tm=512, tn=512, tk=s)[0]
            advance(big["w_in"], (f"mix{l}", l, ["w_in", "w_out"], big))
            dh = _mm("ab_dh", [dproj], [(wl["w_in"], 0)], [F32], _ep_store, trans_b=True, tn=512, tk=2 * da + 2 * db)[0]
            below = below_of(l - 1, "ffn") if l > 0 else None
            outs = _norm_mod_bwd(dh, st["x1"], dx, behind(norm_mix_g[l][None]), sc1, tm_row, below)
            dx, nacc = outs[0], outs[1]
            if below is not None:
                dyb, gacc = outs[2], outs[3]
        else:
            d_g1 = gacc[0] * pool_scale_full[i]
            d_pool_scale[i] = gacc[0] * g1[0]
            big["pool"] = _grouped_dw(st["p"], dyb, n_pool, 1024)
            advance(big["pool"], (f"mix{l}", l, ["pool"], big))
            dp = _grouped_dx(dyb, wl["pool"][None], 0, 1024)
            dx, nacc, dyb, gacc = _pool_bwd(dp, st["x1"], dx, behind(norm_mix_g[l][None]), sc1, tm_row, below_of(l - 1, "ffn"))
        d_sh1, d_sc1, d_norm_mix[l] = nacc[0], nacc[1], nacc[2]
        dmod_rows[l] = jnp.concatenate([d_sh1, d_sc1, d_g1, d_sh2, d_sc2, d_g2])
    grad_x = dx[None]

    dmod = jnp.stack(dmod_rows)
    small = [dmod, jnp.stack(d_norm_mix), jnp.stack(d_norm_ffn),
             jnp.stack([e["conv_w"] for e in even_small]), jnp.stack([e["conv_b"] for e in even_small]),
             jnp.stack([e["a_norm_g"] for e in even_small]), jnp.stack([e["a_norm_b"] for e in even_small]),
             jnp.stack([e["b_norm_g"] for e in even_small]), jnp.stack([e["b_norm_b"] for e in even_small]),
             jnp.stack([e["w_s"] for e in even_small]), jnp.stack([e["bias"] for e in even_small]),
             jnp.stack(d_pool_scale), d_final_g]
    small_shapes = [a.shape for a in small]
    width = 1024 if d >= 1024 else 128
    gathered = _allgather_small(_pack(small, width))
    summed = _unpack(_sum_leading(gathered), small_shapes)
    (g_ada_b, g_norm_mix, g_norm_ffn, g_conv_w_full, g_conv_b, g_a_norm_g, g_a_norm_b, g_b_norm_g, g_b_norm_b, g_w_s, g_bias,
     g_pool_scale_full, g_final_g) = summed
    cw_cols = a_conv_w.shape[2]
    g_conv_w = lax.dynamic_slice_in_dim(g_conv_w_full, chip * cw_cols, cw_cols, axis=2)
    ps_cols = pool_scale.shape[1]
    g_pool_scale = lax.dynamic_slice_in_dim(g_pool_scale_full, chip * ps_cols, ps_cols, axis=1)

    dmod_all = gathered.reshape(N_DEV, -1)[:, :dmod.size].reshape(N_DEV, depth, n_mod * d)
    dmod_cols = lax.dynamic_slice_in_dim(dmod_all, chip * n_ada, n_ada, axis=2)
    dmod_cols = jnp.pad(jnp.transpose(dmod_cols, (1, 0, 2)), ((0, 0), (0, 8), (0, 0)))
    g_ada_w = _ada_bwd(c_pad, dmod_cols)

    weights = [ada_w, ada_b, norm_mix_g, norm_ffn_g, ab_w_in, a_conv_w, a_conv_b, a_norm_g, a_norm_b, b_norm_g, b_norm_b, b_w_s,
               b_bias, ab_w_out, pool_w, pool_scale, ffn_w1, ffn_w3, ffn_w2, final_g]
    ms = [m_ada_w, m_ada_b, m_norm_mix_g, m_norm_ffn_g, m_ab_w_in, m_a_conv_w, m_a_conv_b, m_a_norm_g, m_a_norm_b, m_b_norm_g,
          m_b_norm_b, m_b_w_s, m_b_bias, m_ab_w_out, m_pool_w, m_pool_scale, m_ffn_w1, m_ffn_w3, m_ffn_w2, m_final_g]
    vs = [v_ada_w, v_ada_b, v_norm_mix_g, v_norm_ffn_g, v_ab_w_in, v_a_conv_w, v_a_conv_b, v_a_norm_g, v_a_norm_b, v_b_norm_g,
          v_b_norm_b, v_b_w_s, v_b_bias, v_ab_w_out, v_pool_w, v_pool_scale, v_ffn_w1, v_ffn_w3, v_ffn_w2, v_final_g]
    grads = [g_ada_w, g_ada_b, g_norm_mix, g_norm_ffn, None, g_conv_w, g_conv_b, g_a_norm_g, g_a_norm_b, g_b_norm_g,
             g_b_norm_b, g_w_s, g_bias, None, None, g_pool_scale, None, None, None, g_final_g]
    updates = [None] * len(weights)

    def update(k):
        grads[k] = grads[k].reshape(weights[k].shape)
        updates[k] = _adamw(weights[k], grads[k], ms[k], vs[k])

    advance(g_ada_w)
    for k in range(1, len(weights)):
        if grads[k] is not None:
            update(k)
    update(0)
    advance(updates[0][0])
    big_at = {"w_in": 4, "w_out": 13, "pool": 14, "w1": 16, "w3": 17, "w2": 18}
    assembled = _pair_assemble([shard_grads[nm] for nm in grad_names], [shard_of[nm][1] for nm in grad_names])
    for nm, g in zip(grad_names, assembled):
        grads[big_at[nm]] = g
        update(big_at[nm])
    return (loss, grad_x, *grads, *[u[0] for u in updates], *[u[1] for u in updates], *[u[2] for u in updates])
```

```python
import functools

import jax
import jax.numpy as jnp
from jax import lax
from jax.experimental import pallas as pl
from jax.experimental.pallas import tpu as pltpu

F32 = jnp.float32
BF16 = jnp.bfloat16
EPS = 1e-6
N_DEV = 8
N_CHIP = 4
MESH_AXES = ("x", "y", "c")
MESH = pl.DeviceIdType.MESH
V7X_VMEM_LIMIT_BYTES = 56 * 1024 * 1024
SUBLANES = 8
CONV_HALO = 32
POOL_HALO = 16
POOL_WINDOWS = (2, 4, 8, 16)
CHUNK = 128
ADAM_LR, ADAM_B1, ADAM_B2, ADAM_EPS, ADAM_WD, ADAM_STEP = 0.001, 0.9, 0.999, 1e-08, 0.01, 10
HBM = pl.BlockSpec(memory_space=pltpu.HBM)


def _params(sem=None):
    return pltpu.CompilerParams(dimension_semantics=sem, vmem_limit_bytes=V7X_VMEM_LIMIT_BYTES)


def _pick(n, prefs):
    for p in prefs:
        if p <= n and n % p == 0:
            return p
    return n


def _row_tile(rows, cols, target):
    if rows * cols <= target:
        return rows
    best = None
    for d in range(16, rows, 16):
        if rows % d == 0 and d * cols <= target:
            best = d
    return best if best is not None else rows


def _sds(shape, dtype):
    return jax.ShapeDtypeStruct(tuple(shape), dtype)


def _mm_core(name, grid, nk, a_list, b_list, a_spec, b_spec, dn, extras, extra_specs, out_shapes, out_specs,
             acc_shape, epilogue):
    n, n_ex, n_out = len(a_list), len(extras), len(out_shapes)

    def body(*refs):
        a_refs, b_refs = refs[:n], refs[n:2 * n]
        ex = refs[2 * n:2 * n + n_ex]
        outs = refs[2 * n + n_ex:2 * n + n_ex + n_out]
        accs = refs[2 * n + n_ex + n_out:]
        ps = [lax.dot_general(a[...], b[...], dn, preferred_element_type=F32) for a, b in zip(a_refs, b_refs)]
        if nk == 1:
            epilogue(ps, ex, outs)
            return
        k = pl.program_id(2)

        @pl.when(k == 0)
        def _():
            for acc, p in zip(accs, ps):
                acc[...] = p

        @pl.when(k > 0)
        def _():
            for acc, p in zip(accs, ps):
                acc[...] += p

        @pl.when(k == nk - 1)
        def _():
            epilogue([acc[...] for acc in accs], ex, outs)

    scratch = [] if nk == 1 else [pltpu.VMEM(acc_shape, F32) for _ in range(n)]
    return pl.pallas_call(
        body, grid=grid, name=name,
        in_specs=[a_spec] * n + [b_spec] * n + list(extra_specs),
        out_specs=list(out_specs), out_shape=list(out_shapes), scratch_shapes=scratch,
        compiler_params=_params(("parallel", "parallel", "arbitrary")),
    )(*a_list, *b_list, *extras)


def _mm(name, a_list, b_list, out_dtypes, epilogue, *, trans_a=False, trans_b=False, extras=(), extra_kinds=(),
        tm=1024, tn=1024, tk=1024):
    layer = None
    if isinstance(b_list[0], tuple):
        layer = b_list[0][1]
        b_list = [b for b, _ in b_list]
    a0 = a_list[0]
    b_shape = b_list[0].shape[-2:]
    m, kk = (a0.shape[1], a0.shape[0]) if trans_a else a0.shape
    nn = b_shape[0] if trans_b else b_shape[1]
    tm, tn, tk = _pick(m, (tm, 512, 256, 128)), _pick(nn, (tn, 512, 256, 128)), _pick(kk, (tk, 512, 256, 128))
    nk = kk // tk
    a_spec = pl.BlockSpec((tk, tm), lambda i, j, k: (k, i)) if trans_a else pl.BlockSpec((tm, tk), lambda i, j, k: (i, k))
    if layer is None:
        b_spec = pl.BlockSpec((tn, tk), lambda i, j, k: (j, k)) if trans_b else pl.BlockSpec((tk, tn), lambda i, j, k: (k, j))
    elif trans_b:
        b_spec = pl.BlockSpec((None, tn, tk), lambda i, j, k: (layer, j, k))
    else:
        b_spec = pl.BlockSpec((None, tk, tn), lambda i, j, k: (layer, k, j))
    dn = (((0 if trans_a else 1,), (1 if trans_b else 0,)), ((), ()))
    tile = pl.BlockSpec((tm, tn), lambda i, j, k: (i, j))
    row = pl.BlockSpec((1, tn), lambda i, j, k: (0, j))
    return _mm_core(name, (m // tm, nn // tn, nk), nk, a_list, b_list, a_spec, b_spec, dn, extras,
                    [tile if kd == "tile" else row for kd in extra_kinds],
                    [_sds((m, nn), dt) for dt in out_dtypes], [tile] * len(out_dtypes), (tm, tn), epilogue)


def _ep_store(ps, ex, outs):
    outs[0][...] = ps[0].astype(outs[0].dtype)


def _ep_sum(ps, ex, outs):
    outs[0][...] = (ps[0] + ps[1]).astype(outs[0].dtype)


def _ep_swiglu(ps, ex, outs):
    u, t = ps
    sg = jax.nn.sigmoid(u)
    su = u * sg
    outs[0][...] = (t * (sg * (1.0 + u * (1.0 - sg)))).astype(BF16)
    outs[1][...] = su.astype(BF16)
    outs[2][...] = (su * t).astype(BF16)


def _ep_residual(ps, ex, outs):
    x_ref, gate_ref = ex
    y = ps[0]
    outs[0][...] = x_ref[...] + gate_ref[...] * y
    outs[1][...] = y.astype(BF16)


def _ep_swiglu_bwd(ps, ex, outs):
    dz = ps[0]
    outs[0][...] = (dz * ex[0][...].astype(F32)).astype(BF16)
    outs[1][...] = (dz * ex[1][...].astype(F32)).astype(BF16)


def _grouped_fwd(p, w, layer, x, gate, tm):
    s, d = p.shape
    _, g, kg, ng = w.shape
    tm = _pick(s, (tm, 512, 256, 128))
    tile_a = pl.BlockSpec((tm, kg), lambda i, j, k: (i, j))
    tile_o = pl.BlockSpec((tm, ng), lambda i, j, k: (i, j))
    return _mm_core("pool_mm_fwd", (s // tm, g, 1), 1, [p], [w], tile_a,
                    pl.BlockSpec((None, None, kg, ng), lambda i, j, k: (layer, j, 0, 0)), (((1,), (0,)), ((), ())),
                    [x, gate], [tile_o, pl.BlockSpec((1, ng), lambda i, j, k: (0, j))],
                    [_sds((s, g * ng), F32), _sds((s, g * ng), BF16)], [tile_o, tile_o], None, _ep_residual)


def _grouped_dx(dy, w, layer, tm):
    s, _ = dy.shape
    _, g, kg, ng = w.shape
    tm = _pick(s, (tm, 512, 256, 128))
    return _mm_core("pool_mm_dx", (s // tm, g, 1), 1, [dy], [w], pl.BlockSpec((tm, ng), lambda i, j, k: (i, j)),
                    pl.BlockSpec((None, None, kg, ng), lambda i, j, k: (layer, j, 0, 0)), (((1,), (1,)), ((), ())), [], [],
                    [_sds((s, g * kg), F32)], [pl.BlockSpec((tm, kg), lambda i, j, k: (i, j))], None, _ep_store)[0]


def _grouped_dw(p, dy, groups, tk):
    s, d = p.shape
    kg = d // groups
    ng = dy.shape[1] // groups
    tk = _pick(s, (tk, 512, 256, 128))
    nk = s // tk

    def ep(ps, ex, outs):
        outs[0][...] = ps[0].astype(BF16)

    return _mm_core("pool_mm_dw", (groups, 1, nk), nk, [p], [dy], pl.BlockSpec((tk, kg), lambda i, j, k: (k, i)),
                    pl.BlockSpec((tk, ng), lambda i, j, k: (k, i)), (((0,), (0,)), ((), ())), [], [],
                    [_sds((groups, kg, ng), BF16)], [pl.BlockSpec((None, kg, ng), lambda i, j, k: (i, 0, 0))],
                    (kg, ng), ep)[0]


def _rms_rstd(xv):
    return lax.rsqrt(jnp.mean(xv * xv, axis=-1, keepdims=True) + EPS)


def _norm_mod_math(xv, g, sh, sc):
    return ((xv * _rms_rstd(xv)) * g) * (1.0 + sc) + sh


def _norm_mod_bwd_math(dh, xv, dxo, g, sc, acc_ref):
    r = _rms_rstd(xv)
    xhat = xv * r
    acc_ref[0:1, :] += jnp.sum(dh, axis=0, keepdims=True)
    acc_ref[1:2, :] += jnp.sum(dh * (xhat * g), axis=0, keepdims=True)
    dhn = dh * (1.0 + sc)
    acc_ref[2:3, :] += jnp.sum(dhn * xhat, axis=0, keepdims=True)
    dxh = dhn * g
    return dxo + r * (dxh - xhat * jnp.mean(dxh * xhat, axis=-1, keepdims=True))


def _vec_spec(d):
    return pl.BlockSpec((1, d), lambda i: (0, 0))


def _acc_spec(d):
    return pl.BlockSpec((8, d), lambda i: (0, 0))


def _norm_mod(x, g, sh, sc, tm):
    s, d = x.shape

    def body(x_ref, g_ref, sh_ref, sc_ref, h_ref):
        h_ref[...] = _norm_mod_math(x_ref[...], g_ref[...], sh_ref[...], sc_ref[...]).astype(BF16)

    row = pl.BlockSpec((tm, d), lambda i: (i, 0))
    return pl.pallas_call(body, grid=(s // tm,), name="norm_mod", in_specs=[row] + [_vec_spec(d)] * 3, out_specs=row,
                          out_shape=_sds((s, d), BF16), compiler_params=_params(("parallel",)))(x, g, sh, sc)


def _gate_step(dxv, y_ref, gate_ref, dy_ref, gacc_ref):
    dy_ref[...] = (dxv * gate_ref[...]).astype(BF16)
    gacc_ref[0:1, :] += jnp.sum(dxv * y_ref[...].astype(F32), axis=0, keepdims=True)


def _norm_mod_bwd(dh, x, dxo, g, sc, tm, below=None):
    s, d = x.shape

    def body(dh_ref, x_ref, dxo_ref, g_ref, sc_ref, *rest):
        if below is None:
            dx_ref, acc_ref = rest
        else:
            y_ref, gate_ref, dx_ref, acc_ref, dy_ref, gacc_ref = rest

        @pl.when(pl.program_id(0) == 0)
        def _():
            acc_ref[...] = jnp.zeros_like(acc_ref)
            if below is not None:
                gacc_ref[...] = jnp.zeros_like(gacc_ref)

        dxv = _norm_mod_bwd_math(dh_ref[...], x_ref[...], dxo_ref[...], g_ref[...], sc_ref[...], acc_ref)
        dx_ref[...] = dxv
        if below is not None:
            _gate_step(dxv, y_ref, gate_ref, dy_ref, gacc_ref)

    row = pl.BlockSpec((tm, d), lambda i: (i, 0))
    extra_in = [] if below is None else [row, _vec_spec(d)]
    extra_out = [] if below is None else [row, _acc_spec(d)]
    extra_shape = [] if below is None else [_sds((s, d), BF16), _sds((8, d), F32)]
    return pl.pallas_call(body, grid=(s // tm,), name="norm_mod_bwd", in_specs=[row, row, row, _vec_spec(d), _vec_spec(d)] + extra_in,
                          out_specs=[row, _acc_spec(d)] + extra_out, out_shape=[_sds((s, d), F32), _sds((8, d), F32)] + extra_shape,
                          compiler_params=_params(("arbitrary",)))(dh, x, dxo, g, sc, *([] if below is None else below))


def _gate_bwd(dx, y, gate, tm):
    s, d = dx.shape

    def body(dx_ref, y_ref, gate_ref, dy_ref, acc_ref):
        @pl.when(pl.program_id(0) == 0)
        def _():
            acc_ref[...] = jnp.zeros_like(acc_ref)

        dxv = dx_ref[...]
        dy_ref[...] = (dxv * gate_ref[...]).astype(BF16)
        acc_ref[0:1, :] += jnp.sum(dxv * y_ref[...].astype(F32), axis=0, keepdims=True)

    row = pl.BlockSpec((tm, d), lambda i: (i, 0))
    return pl.pallas_call(body, grid=(s // tm,), name="gate_bwd", in_specs=[row, row, _vec_spec(d)],
                          out_specs=[row, _acc_spec(d)], out_shape=[_sds((s, d), BF16), _sds((8, d), F32)],
                          compiler_params=_params(("arbitrary",)))(dx, y, gate)


def _final_loss_bwd(x, g, target, tm, below):
    s, d = x.shape

    def body(x_ref, g_ref, t_ref, y_ref, gate_ref, dx_ref, acc_ref, loss_ref, dy_ref, gacc_ref):
        @pl.when(pl.program_id(0) == 0)
        def _():
            acc_ref[...] = jnp.zeros_like(acc_ref)
            loss_ref[...] = jnp.zeros_like(loss_ref)
            gacc_ref[...] = jnp.zeros_like(gacc_ref)

        xv = x_ref[...]
        gv = g_ref[...]
        r = _rms_rstd(xv)
        xhat = xv * r
        err = xhat * gv - t_ref[...]
        loss_ref[...] += (0.5 / d) * jnp.sum(err * err)
        dy = err * (1.0 / d)
        acc_ref[0:1, :] += jnp.sum(dy * xhat, axis=0, keepdims=True)
        dxh = dy * gv
        dxv = r * (dxh - xhat * jnp.mean(dxh * xhat, axis=-1, keepdims=True))
        dx_ref[...] = dxv
        _gate_step(dxv, y_ref, gate_ref, dy_ref, gacc_ref)

    row = pl.BlockSpec((tm, d), lambda i: (i, 0))
    return pl.pallas_call(body, grid=(s // tm,), name="final_loss_bwd", in_specs=[row, _vec_spec(d), row, row, _vec_spec(d)],
                          out_specs=[row, _acc_spec(d), pl.BlockSpec((8, 128), lambda i: (0, 0)), row, _acc_spec(d)],
                          out_shape=[_sds((s, d), F32), _sds((8, d), F32), _sds((8, 128), F32), _sds((s, d), BF16), _sds((8, d), F32)],
                          compiler_params=_params(("arbitrary",)))(x, g, target, *below)


def _chunks(tm, width, rb, cb):
    rb, cb = min(rb, tm), min(cb, width)
    return [(r0, c0, rb, cb) for r0 in range(0, tm, rb) for c0 in range(0, width, cb)]


def _prev_halo_map(tm, halo):
    return lambda i: (jnp.maximum(i * (tm // halo) - 1, 0), 0)


def _next_halo_map(tm, halo, s):
    return lambda i: (jnp.minimum((i + 1) * (tm // halo), s // halo - 1), 0)


def _pool_fwd(x, g, sh, sc, tm):
    s, d = x.shape
    dg = d // len(POOL_WINDOWS)

    def body(x_ref, xh_ref, g_ref, sh_ref, sc_ref, p_ref, ext_ref):
        i = pl.program_id(0)
        gv, shv, scv = g_ref[...], sh_ref[...], sc_ref[...]
        ext_ref[POOL_HALO:, :] = _norm_mod_math(x_ref[...], gv, shv, scv)
        ext_ref[0:POOL_HALO, :] = jnp.where(i == 0, 0.0, _norm_mod_math(xh_ref[...], gv, shv, scv))
        for gi, w in enumerate(POOL_WINDOWS):
            for r0, c0, rb, cb in _chunks(tm, dg, 64, 256):
                cols = pl.ds(gi * dg + c0, cb)
                tok = ext_ref[pl.ds(POOL_HALO + r0, rb), cols]
                acc = tok
                for j in range(1, w):
                    acc = acc + ext_ref[pl.ds(POOL_HALO + r0 - j, rb), cols]
                t_glob = i * tm + r0 + lax.broadcasted_iota(jnp.int32, (rb, 1), 0)
                cnt = jnp.minimum(t_glob + 1, w).astype(F32)
                p_ref[pl.ds(r0, rb), cols] = (acc / cnt - tok).astype(BF16)

    row = pl.BlockSpec((tm, d), lambda i: (i, 0))
    halo = pl.BlockSpec((POOL_HALO, d), _prev_halo_map(tm, POOL_HALO))
    return pl.pallas_call(body, grid=(s // tm,), name="pool_fwd", in_specs=[row, halo] + [_vec_spec(d)] * 3, out_specs=row,
                          out_shape=_sds((s, d), BF16), scratch_shapes=[pltpu.VMEM((tm + POOL_HALO, d), F32)],
                          compiler_params=_params(("parallel",)))(x, x, g, sh, sc)


def _pool_bwd(dp, x, dxo, g, sc, tm, below):
    s, d = x.shape
    dg = d // len(POOL_WINDOWS)
    n_tiles = s // tm

    def body(dp_ref, dph_ref, x_ref, dxo_ref, g_ref, sc_ref, y_ref, gate_ref, dx_ref, acc_ref, dy_ref, gacc_ref, ext_ref, dh_ref):
        i = pl.program_id(0)

        @pl.when(i == 0)
        def _():
            acc_ref[...] = jnp.zeros_like(acc_ref)
            gacc_ref[...] = jnp.zeros_like(gacc_ref)

        for gi, w in enumerate(POOL_WINDOWS):
            cols = pl.ds(gi * dg, dg)
            t_main = i * tm + lax.broadcasted_iota(jnp.int32, (tm, 1), 0)
            ext_ref[0:tm, cols] = dp_ref[:, cols] / jnp.minimum(t_main + 1, w).astype(F32)
            ext_ref[tm:, cols] = jnp.where(i == n_tiles - 1, 0.0, dph_ref[:, cols] * (1.0 / w))
            for r0, c0, rb, cb in _chunks(tm, dg, 64, 256):
                cc = pl.ds(gi * dg + c0, cb)
                acc = ext_ref[pl.ds(r0, rb), cc]
                for j in range(1, w):
                    acc = acc + ext_ref[pl.ds(r0 + j, rb), cc]
                dh_ref[pl.ds(r0, rb), cc] = acc - dp_ref[pl.ds(r0, rb), cc]
        dxv = _norm_mod_bwd_math(dh_ref[...], x_ref[...], dxo_ref[...], g_ref[...], sc_ref[...], acc_ref)
        dx_ref[...] = dxv
        _gate_step(dxv, y_ref, gate_ref, dy_ref, gacc_ref)

    row = pl.BlockSpec((tm, d), lambda i: (i, 0))
    halo = pl.BlockSpec((POOL_HALO, d), _next_halo_map(tm, POOL_HALO, s))
    return pl.pallas_call(body, grid=(n_tiles,), name="pool_bwd",
                          in_specs=[row, halo, row, row, _vec_spec(d), _vec_spec(d), row, _vec_spec(d)],
                          out_specs=[row, _acc_spec(d), row, _acc_spec(d)],
                          out_shape=[_sds((s, d), F32), _sds((8, d), F32), _sds((s, d), BF16), _sds((8, d), F32)],
                          scratch_shapes=[pltpu.VMEM((tm + POOL_HALO, d), F32), pltpu.VMEM((tm, d), F32)],
                          compiler_params=_params(("arbitrary",)))(dp, dp, x, dxo, g, sc, *below)


def _layernorm_fwd(v, g, b):
    mu = jnp.mean(v, axis=-1, keepdims=True)
    xc = v - mu
    rstd = lax.rsqrt(jnp.mean(xc * xc, axis=-1, keepdims=True) + EPS)
    yn = xc * rstd
    return yn * g + b, yn, rstd


def _layernorm_bwd(dz, yn, rstd, g):
    dyn = dz * g
    dv = rstd * (dyn - jnp.mean(dyn, axis=-1, keepdims=True) - yn * jnp.mean(dyn * yn, axis=-1, keepdims=True))
    return dv, jnp.sum(dz * yn, axis=0, keepdims=True), jnp.sum(dz, axis=0, keepdims=True)


def _fill_shifted(ext_ref, sh_ref):
    n = ext_ref.shape[0]
    for r in range(1, SUBLANES):
        sh_ref[r - 1, 0:n - SUBLANES, :] = ext_ref[pl.ds(r, n - SUBLANES), :]


def _shifted(ext_ref, sh_ref, start, rows, cols):
    q, r = divmod(start, SUBLANES)
    if r == 0:
        return ext_ref[pl.ds(start, rows), cols]
    return sh_ref[r - 1, pl.ds(q * SUBLANES, rows), cols]


def _ab_mid_fwd(proj, conv_w, conv_b, a_g, a_b, v_g, v_b, w_c, bias_full, tm):
    s = proj.shape[0]
    da = conv_b.shape[1]
    db = v_g.shape[1]
    nh = w_c.shape[0]
    kw = conv_w.shape[0]
    lead = CONV_HALO - (kw - 1)

    def body(p_ref, ph_ref, cw_ref, cb_ref, ag_ref, ab_ref, vg_ref, vb_ref, wc_ref, bias_ref, cat_ref, a1_ref, ext_ref,
             a1s_ref, sh_ref):
        i = pl.program_id(0)
        val = p_ref[:, 0:da].astype(F32)
        gat = p_ref[:, da:2 * da].astype(F32)
        ext_ref[CONV_HALO:, :] = val * jax.nn.sigmoid(gat)
        hv = ph_ref[:, 0:da].astype(F32)
        hg = ph_ref[:, da:2 * da].astype(F32)
        ext_ref[0:CONV_HALO, :] = jnp.where(i == 0, 0.0, hv * jax.nn.sigmoid(hg))
        _fill_shifted(ext_ref, sh_ref)
        for r0, c0, rb, cb in _chunks(tm, da, 64, 256):
            cols = pl.ds(c0, cb)
            acc = jnp.broadcast_to(cb_ref[:, cols], (rb, cb))
            for k in range(kw):
                acc = acc + cw_ref[k:k + 1, cols] * _shifted(ext_ref, sh_ref, r0 + lead + k, rb, cols)
            a1s_ref[pl.ds(r0, rb), cols] = acc
        a1 = a1s_ref[...]
        a1_ref[...] = a1.astype(BF16)
        z, _, _ = _layernorm_fwd(a1, ag_ref[...], ab_ref[...])
        cat_ref[:, 0:da] = (z * jax.nn.sigmoid(z)).astype(BF16)

        bu = p_ref[:, 2 * da:2 * da + db].astype(F32)
        bv = p_ref[:, 2 * da + db:].astype(F32)
        vn, _, _ = _layernorm_fwd(bv, vg_ref[...], vb_ref[...])
        vnb = vn.astype(BF16)
        for n in range(tm // CHUNK):
            rows = slice(n * CHUNK, (n + 1) * CHUNK)
            for h in range(nh):
                hc = slice(h * CHUNK, (h + 1) * CHUNK)
                vo = jnp.dot(wc_ref[h], vnb[rows, hc], preferred_element_type=F32) + bias_ref[:, hc]
                cat_ref[rows, da + h * CHUNK:da + (h + 1) * CHUNK] = (bu[rows, hc] * vo).astype(BF16)

    full = lambda a: pl.BlockSpec(a.shape, lambda i: (0,) * a.ndim)
    return pl.pallas_call(
        body, grid=(s // tm,), name="ab_mid_fwd",
        in_specs=[pl.BlockSpec((tm, 2 * da + 2 * db), lambda i: (i, 0)),
                  pl.BlockSpec((CONV_HALO, 2 * da), _prev_halo_map(tm, CONV_HALO)),
                  full(conv_w), full(conv_b), full(a_g), full(a_b), full(v_g), full(v_b), full(w_c), full(bias_full)],
        out_specs=[pl.BlockSpec((tm, da + db), lambda i: (i, 0)), pl.BlockSpec((tm, da), lambda i: (i, 0))],
        out_shape=[_sds((s, da + db), BF16), _sds((s, da), BF16)],
        scratch_shapes=[pltpu.VMEM((tm + CONV_HALO, da), F32), pltpu.VMEM((tm, da), F32),
                        pltpu.VMEM((SUBLANES - 1, tm + CONV_HALO, da), F32)],
        compiler_params=_params(("parallel",)),
    )(proj, proj, conv_w, conv_b, a_g, a_b, v_g, v_b, w_c, bias_full)


def _ab_mid_bwd(dcat, proj, a1, conv_w, a_g, a_b, v_g, v_b, w_c, bias_full, tm):
    s = proj.shape[0]
    da = a_g.shape[1]
    db = v_g.shape[1]
    nh = w_c.shape[0]
    kw = conv_w.shape[0]
    lead = CONV_HALO - (kw - 1)
    n_tiles = s // tm

    def body(dc_ref, dch_ref, p_ref, ph_ref, a1_ref, a1h_ref, cw_ref, ag_ref, ab_ref, vg_ref, vb_ref, wc_ref, bias_ref,
             dp_ref, dcw_ref, vec_ref, dws_ref, dbias_ref, ext_ref, dext_ref, dcw_acc, dvn_ref, sh_ref, dsh_ref):
        i = pl.program_id(0)

        @pl.when(i == 0)
        def _():
            dcw_acc[...] = jnp.zeros_like(dcw_acc)
            vec_ref[...] = jnp.zeros_like(vec_ref)
            dws_ref[...] = jnp.zeros_like(dws_ref)
            dbias_ref[...] = jnp.zeros_like(dbias_ref)

        agv, abv = ag_ref[...], ab_ref[...]

        def silu_ln_bwd(a1v, d_a2):
            z, yn, rstd = _layernorm_fwd(a1v, agv, abv)
            sg = jax.nn.sigmoid(z)
            return _layernorm_bwd(d_a2 * (sg * (1.0 + z * (1.0 - sg))), yn, rstd, agv)

        d_a1, dga, dba = silu_ln_bwd(a1_ref[...].astype(F32), dc_ref[:, 0:da])
        dext_ref[0:tm, :] = d_a1
        d_a1h, _, _ = silu_ln_bwd(a1h_ref[...].astype(F32), dch_ref[...])
        dext_ref[tm:, :] = jnp.where(i == n_tiles - 1, 0.0, d_a1h)
        vec_ref[0:1, 0:da] += dga
        vec_ref[1:2, 0:da] += dba
        vec_ref[2:3, 0:da] += jnp.sum(d_a1, axis=0, keepdims=True)

        val = p_ref[:, 0:da].astype(F32)
        sgg = jax.nn.sigmoid(p_ref[:, da:2 * da].astype(F32))
        ext_ref[CONV_HALO:, :] = val * sgg
        hv = ph_ref[:, 0:da].astype(F32)
        hg = ph_ref[:, da:2 * da].astype(F32)
        ext_ref[0:CONV_HALO, :] = jnp.where(i == 0, 0.0, hv * jax.nn.sigmoid(hg))

        _fill_shifted(ext_ref, sh_ref)
        _fill_shifted(dext_ref, dsh_ref)
        for r0, c0, rb, cb in _chunks(tm, da, 64, 256):
            cols = pl.ds(c0, cb)
            rows = pl.ds(r0, rb)
            d1 = dext_ref[rows, cols]
            acc = jnp.zeros((rb, cb), F32)
            for k in range(kw):
                acc = acc + cw_ref[k:k + 1, cols] * _shifted(dext_ref, dsh_ref, r0 + (kw - 1) - k, rb, cols)
                prod = d1 * _shifted(ext_ref, sh_ref, r0 + lead + k, rb, cols)
                dcw_acc[k, :, cols] += jnp.sum(prod.reshape(rb // SUBLANES, SUBLANES, cb), axis=0)
            v = p_ref[rows, pl.ds(c0, cb)].astype(F32)
            sg = jax.nn.sigmoid(p_ref[rows, pl.ds(da + c0, cb)].astype(F32))
            dp_ref[rows, pl.ds(c0, cb)] = (acc * sg).astype(BF16)
            dp_ref[rows, pl.ds(da + c0, cb)] = (acc * v * sg * (1.0 - sg)).astype(BF16)

        vgv = vg_ref[...]
        bu = p_ref[:, 2 * da:2 * da + db].astype(F32)
        bv = p_ref[:, 2 * da + db:].astype(F32)
        vn, yn_v, rstd_v = _layernorm_fwd(bv, vgv, vb_ref[...])
        vnb = vn.astype(BF16)
        for n in range(tm // CHUNK):
            rows = slice(n * CHUNK, (n + 1) * CHUNK)
            for h in range(nh):
                hc = slice(h * CHUNK, (h + 1) * CHUNK)
                wch = wc_ref[h]
                blk = vnb[rows, hc]
                vo = jnp.dot(wch, blk, preferred_element_type=F32) + bias_ref[:, hc]
                d_bout = dc_ref[rows, da + h * CHUNK:da + (h + 1) * CHUNK]
                dp_ref[rows, 2 * da + h * CHUNK:2 * da + (h + 1) * CHUNK] = (d_bout * vo).astype(BF16)
                d_vo = d_bout * bu[rows, hc]
                dbias_ref[h] += jnp.sum(d_vo, axis=1, keepdims=True)
                d_vob = d_vo.astype(BF16)
                dws_ref[h] += lax.dot_general(d_vob, blk, (((1,), (1,)), ((), ())), preferred_element_type=F32)
                dvn_ref[rows, hc] = lax.dot_general(wch, d_vob, (((0,), (0,)), ((), ())), preferred_element_type=F32)
        d_bv, dgv, dbv = _layernorm_bwd(dvn_ref[...], yn_v, rstd_v, vgv)
        dp_ref[:, 2 * da + db:] = d_bv.astype(BF16)
        vec_ref[3:4, 0:db] += dgv
        vec_ref[4:5, 0:db] += dbv

        @pl.when(i == n_tiles - 1)
        def _():
            dcw_ref[...] = jnp.sum(dcw_acc[...], axis=1)
            causal = lax.broadcasted_iota(jnp.int32, (CHUNK, CHUNK), 0) >= lax.broadcasted_iota(jnp.int32, (CHUNK, CHUNK), 1)
            for h in range(nh):
                dws_ref[h] = jnp.where(causal, dws_ref[h], 0.0)

    full = lambda a: pl.BlockSpec(a.shape, lambda i: (0,) * a.ndim)
    wide = max(da, db)
    return pl.pallas_call(
        body, grid=(n_tiles,), name="ab_mid_bwd",
        in_specs=[pl.BlockSpec((tm, da + db), lambda i: (i, 0)),
                  pl.BlockSpec((CONV_HALO, da), _next_halo_map(tm, CONV_HALO, s)),
                  pl.BlockSpec((tm, 2 * da + 2 * db), lambda i: (i, 0)),
                  pl.BlockSpec((CONV_HALO, 2 * da), _prev_halo_map(tm, CONV_HALO)),
                  pl.BlockSpec((tm, da), lambda i: (i, 0)),
                  pl.BlockSpec((CONV_HALO, da), _next_halo_map(tm, CONV_HALO, s)),
                  full(conv_w), full(a_g), full(a_b), full(v_g), full(v_b), full(w_c), full(bias_full)],
        out_specs=[pl.BlockSpec((tm, 2 * da + 2 * db), lambda i: (i, 0)),
                   pl.BlockSpec((kw, da), lambda i: (0, 0)),
                   pl.BlockSpec((8, wide), lambda i: (0, 0)),
                   pl.BlockSpec((nh, CHUNK, CHUNK), lambda i: (0, 0, 0)),
                   pl.BlockSpec((nh, CHUNK, 1), lambda i: (0, 0, 0))],
        out_shape=[_sds((s, 2 * da + 2 * db), BF16), _sds((kw, da), F32), _sds((8, wide), F32),
                   _sds((nh, CHUNK, CHUNK), F32), _sds((nh, CHUNK, 1), F32)],
        scratch_shapes=[pltpu.VMEM((tm + CONV_HALO, da), F32), pltpu.VMEM((tm + CONV_HALO, da), F32),
                        pltpu.VMEM((kw, SUBLANES, da), F32), pltpu.VMEM((tm, db), F32),
                        pltpu.VMEM((SUBLANES - 1, tm + CONV_HALO, da), F32), pltpu.VMEM((SUBLANES - 1, tm + CONV_HALO, da), F32)],
        compiler_params=_params(("arbitrary",)),
    )(dcat, dcat, proj, proj, a1, a1, conv_w, a_g, a_b, v_g, v_b, w_c, bias_full)


def _ada_fwd(c_all, w, b):
    nl, d, n = w.shape
    tn = _pick(n, (512, 256, 128))

    def body(c_ref, w_ref, b_ref, o_ref):
        cv = c_ref[...]
        cond = (cv * jax.nn.sigmoid(cv)).astype(BF16)
        o_ref[...] = jnp.dot(cond, w_ref[...].astype(BF16), preferred_element_type=F32) + b_ref[...]

    return pl.pallas_call(
        body, grid=(nl, n // tn), name="ada_fwd",
        in_specs=[pl.BlockSpec(c_all.shape, lambda l, j: (0, 0)), pl.BlockSpec((None, d, tn), lambda l, j: (l, 0, j)),
                  pl.BlockSpec((None, 1, tn), lambda l, j: (l, 0, j))],
        out_specs=pl.BlockSpec((None, c_all.shape[0], tn), lambda l, j: (l, 0, j)),
        out_shape=_sds((nl, c_all.shape[0], n), F32), compiler_params=_params(("parallel", "parallel")),
    )(c_all, w, b)


def _ada_bwd(c_all, dmod):
    nl, nb, n = dmod.shape
    d = c_all.shape[1]
    tn = _pick(n, (512, 256, 128))

    def body(c_ref, g_ref, o_ref):
        cv = c_ref[...]
        cond = (cv * jax.nn.sigmoid(cv)).astype(BF16)
        o_ref[...] = lax.dot_general(cond, g_ref[...].astype(BF16), (((0,), (0,)), ((), ())), preferred_element_type=F32)

    return pl.pallas_call(
        body, grid=(nl, n // tn), name="ada_bwd",
        in_specs=[pl.BlockSpec(c_all.shape, lambda l, j: (0, 0)), pl.BlockSpec((None, nb, tn), lambda l, j: (l, 0, j))],
        out_specs=pl.BlockSpec((None, d, tn), lambda l, j: (l, 0, j)),
        out_shape=_sds((nl, d, n), F32), compiler_params=_params(("parallel", "parallel")),
    )(c_all, dmod)


def _sum_leading(a, out_dtype=F32, name="sum_leading"):
    n, r, c = a.shape
    tr = _row_tile(r, c, 256 * 1024)

    def body(a_ref, o_ref):
        acc = a_ref[0].astype(F32)
        for k in range(1, n):
            acc = acc + a_ref[k].astype(F32)
        o_ref[...] = acc.astype(out_dtype)

    return pl.pallas_call(body, grid=(r // tr,), name=name, in_specs=[pl.BlockSpec((n, tr, c), lambda i: (0, i, 0))],
                          out_specs=pl.BlockSpec((tr, c), lambda i: (i, 0)), out_shape=_sds((r, c), out_dtype),
                          compiler_params=_params(("parallel",)))(a)


def _cast_into_full(w, kind, s_vec, l0, nl):
    _, r, c = w.shape
    tr = _row_tile(r, c, 512 * 1024)
    nb = r // tr
    if kind == "col":
        out_shape, out_spec = (nl, r, N_CHIP * c), pl.BlockSpec((None, tr, c), lambda l, i, sv: (l, i, sv[0]))
    else:
        out_shape, out_spec = (nl, N_CHIP * r, c), pl.BlockSpec((None, tr, c), lambda l, i, sv: (l, sv[0] * nb + i, 0))

    def body(sv_ref, w_ref, o_ref):
        o_ref[...] = w_ref[...].astype(BF16)

    return pl.pallas_call(
        body, name="cast_into_full",
        grid_spec=pltpu.PrefetchScalarGridSpec(num_scalar_prefetch=1, grid=(nl, nb),
                                               in_specs=[pl.BlockSpec((None, tr, c), lambda l, i, sv: (l0 + l, i, 0))], out_specs=out_spec),
        out_shape=_sds(out_shape, BF16), compiler_params=_params(("parallel", "parallel")),
    )(s_vec, w)


def _chip_sum_into(slab, cp, g, g_shape, layer, kind, sc_vec, rows_per_group=None):
    n, r, c = slab.shape
    rg = r if rows_per_group is None else rows_per_group
    tr = _row_tile(rg, c, 256 * 1024)
    groups = r // rg
    nbg = rg // tr
    nb = groups * nbg
    n_sc = len(sc_vec)
    if kind == "col":
        out_spec = pl.BlockSpec((None, tr, c), lambda gi, i, *sc: (layer, sc[1][0] * nb + gi * nbg + i, 0))
        own_spec = pl.BlockSpec((tr, c), lambda gi, i, *sc: (gi * nbg + i, sc[0][0]))
    else:
        out_spec = pl.BlockSpec((None, tr, c), lambda gi, i, *sc: (layer, gi * nbg + i, sc[1][0]))
        own_spec = pl.BlockSpec((tr, c), lambda gi, i, *sc: (gi * (n * nbg) + sc[0][0] * nbg + i, 0))

    def other(k):
        return pl.BlockSpec((None, tr, c), lambda gi, i, *sc: (sc[1 + k][0], gi * nbg + i, 0))

    in_specs = [own_spec] + [other(k) for k in range(1, n)]
    args = list(sc_vec) + [cp] + [slab] * (n - 1)
    aliases = {}
    if g is not None:
        in_specs.append(pl.BlockSpec(memory_space=pl.ANY))
        args.append(g)
        aliases = {len(args) - 1: 0}

    def body(*refs):
        own_ref, rest = refs[n_sc], refs[n_sc + 1:]
        o_ref = rest[-1]
        acc = own_ref[...].astype(F32)
        for k in range(n - 1):
            acc = acc + rest[k][...].astype(F32)
        o_ref[...] = acc

    return pl.pallas_call(
        body, name="chip_sum",
        grid_spec=pltpu.PrefetchScalarGridSpec(num_scalar_prefetch=n_sc, grid=(groups, nbg), in_specs=in_specs, out_specs=out_spec),
        out_shape=_sds(g_shape, F32), input_output_aliases=aliases, compiler_params=_params(("parallel", "parallel")),
    )(*args)


def _pair_add(dw, sib, kind, c_vec):
    r, c = sib.shape
    tr = _row_tile(r, c, 512 * 1024)
    nb = r // tr
    if kind == "col":
        dw_spec = pl.BlockSpec((tr, c), lambda i, cv: (cv[0] * nb + i, 0))
    else:
        dw_spec = pl.BlockSpec((tr, c), lambda i, cv: (i, cv[0]))

    def body(cv_ref, dw_ref, sib_ref, o_ref):
        o_ref[...] = (dw_ref[...].astype(F32) + sib_ref[...].astype(F32)).astype(BF16)

    return pl.pallas_call(
        body, name="pair_add",
        grid_spec=pltpu.PrefetchScalarGridSpec(num_scalar_prefetch=1, grid=(nb,), in_specs=[dw_spec, pl.BlockSpec((tr, c), lambda i, cv: (i, 0))],
                                               out_specs=pl.BlockSpec((tr, c), lambda i, cv: (i, 0))),
        out_shape=_sds((r, c), BF16), compiler_params=_params(("parallel",)),
    )(c_vec, dw, sib)


def _adamw(w, g, m, v):
    shape = w.shape
    cols = shape[-1]
    rows = w.size // cols
    tr = _row_tile(rows, cols, 256 * 1024)
    bc1 = 1.0 - ADAM_B1 ** ADAM_STEP
    bc2 = 1.0 - ADAM_B2 ** ADAM_STEP

    def body(w_ref, g_ref, m_ref, v_ref, d_ref, mo_ref, vo_ref):
        gv = g_ref[...]
        mn = ADAM_B1 * m_ref[...] + (1.0 - ADAM_B1) * gv
        vn = ADAM_B2 * v_ref[...] + (1.0 - ADAM_B2) * (gv * gv)
        d_ref[...] = -ADAM_LR * ((mn / bc1) / (jnp.sqrt(vn / bc2) + ADAM_EPS) + ADAM_WD * w_ref[...])
        mo_ref[...] = mn
        vo_ref[...] = vn

    spec = pl.BlockSpec((tr, cols), lambda i: (i, 0))
    outs = pl.pallas_call(body, grid=(rows // tr,), name="adamw", in_specs=[spec] * 4, out_specs=[spec] * 3,
                          out_shape=[_sds((rows, cols), F32)] * 3, compiler_params=_params(("parallel",)))(
        *[a.reshape(rows, cols) for a in (w, g, m, v)])
    return [o.reshape(shape) for o in outs]


def _mesh_pos():
    return lax.axis_index("x"), lax.axis_index("y"), lax.axis_index("c")


def _other_chips(x, y):
    return [(1 - x, y), (x, 1 - y), (1 - x, 1 - y)]


def _allgather_small(a):
    r, c = a.shape

    def body(x_ref, out_ref, send_sems, recv_sems, local_sem):
        x, y, cc = _mesh_pos()
        me, sibling = (x, y, cc), (x, y, 1 - cc)
        chips = _other_chips(x, y)

        def slab(px, py, pc):
            return out_ref.at[4 * px + 2 * py + pc]

        def copy(k, block, to, src=None):
            return pltpu.make_async_remote_copy(src_ref=slab(*block) if src is None else src, dst_ref=slab(*block),
                                                send_sem=send_sems.at[k], recv_sem=recv_sems.at[k], device_id=to,
                                                device_id_type=MESH)

        mine = pltpu.make_async_copy(x_ref, slab(*me), local_sem)
        mine.start()
        first = [copy(0, me, sibling, src=x_ref)]
        first += [copy(1 + j, me, (*chip, cc), src=x_ref) for j, chip in enumerate(chips)]
        for cp in first:
            cp.start()
        passed = [copy(4 + j, (*chip, cc), sibling) for j, chip in enumerate(chips)]
        for j, chip in enumerate(chips):
            copy(1 + j, (*chip, cc), me).wait_recv()
            passed[j].start()
        copy(0, sibling, me).wait_recv()
        for j, chip in enumerate(chips):
            copy(4 + j, (*chip, 1 - cc), me).wait_recv()
        for cp in first + passed:
            cp.wait_send()
        mine.wait()

    return pl.pallas_call(
        body, name="allgather_small", out_shape=_sds((N_DEV, r, c), F32),
        in_specs=[pl.BlockSpec(memory_space=pltpu.VMEM)], out_specs=pl.BlockSpec(memory_space=pltpu.VMEM),
        scratch_shapes=[pltpu.SemaphoreType.DMA((7,)), pltpu.SemaphoreType.DMA((7,)), pltpu.SemaphoreType.DMA],
        compiler_params=pltpu.CompilerParams(vmem_limit_bytes=V7X_VMEM_LIMIT_BYTES),
    )(a)


def _idx(ref, rows=None, cols=None):
    lead = (slice(None),) * (len(ref.shape) - 2)
    return ref.at[lead + (slice(None) if rows is None else rows, slice(None) if cols is None else cols)]


def _half(ref, kind, c):
    r, cdim = ref.shape[-2:]
    if kind == "col":
        return _idx(ref, rows=pl.ds(c * (r // 2), r // 2))
    return _idx(ref, cols=pl.ds(c * (cdim // 2), cdim // 2))


def _shard_region(full, kind, shard_shape, s):
    r, cdim = shard_shape[-2:]
    if kind == "col":
        return _idx(full, cols=pl.ds(s * cdim, cdim))
    return _idx(full, rows=pl.ds(s * r, r))


def _allgather_weights(fulls, kinds, shard_shapes):
    nt = len(fulls)

    def body(*refs):
        fu = refs[nt:2 * nt]
        send_sems, recv_sems = refs[2 * nt:]
        x, y, c = _mesh_pos()
        chips = _other_chips(x, y)

        def part(t, chip, cc):
            return _half(_shard_region(fu[t], kinds[t], shard_shapes[t], 2 * chip[0] + chip[1]), kinds[t], cc)

        def copy(t, k, blk, to):
            return pltpu.make_async_remote_copy(src_ref=blk, dst_ref=blk, send_sem=send_sems.at[6 * t + k],
                                                recv_sem=recv_sems.at[6 * t + k], device_id=to, device_id_type=MESH)

        first, passed = [], []
        for t in range(nt):
            for j, chip in enumerate(chips):
                cp = copy(t, j, part(t, (x, y), c), (*chip, c))
                cp.start()
                first.append(cp)
        for t in range(nt):
            for j, chip in enumerate(chips):
                copy(t, j, part(t, chip, c), (x, y, c)).wait_recv()
                fw = copy(t, 3 + j, part(t, chip, c), (x, y, 1 - c))
                fw.start()
                passed.append(fw)
        for t in range(nt):
            for j, chip in enumerate(chips):
                copy(t, 3 + j, part(t, chip, 1 - c), (x, y, c)).wait_recv()
        for cp in first + passed:
            cp.wait_send()

    return pl.pallas_call(
        body, name="allgather_weights", out_shape=[_sds(f.shape, BF16) for f in fulls],
        in_specs=[HBM] * nt, out_specs=[HBM] * nt, input_output_aliases={t: t for t in range(nt)},
        scratch_shapes=[pltpu.SemaphoreType.DMA((6 * nt,)), pltpu.SemaphoreType.DMA((6 * nt,))],
    )(*fulls)


SEM = pl.BlockSpec(memory_space=pltpu.SEMAPHORE)
ANY = pl.BlockSpec(memory_space=pl.ANY)
EFFECT = pltpu.SideEffectType.DATAFLOW_SIDE_EFFECTING


def _gather_start(name, fulls, kinds, shard_shapes, prev):
    nt = len(fulls)

    def body(*refs):
        send_sems, recv_sems = refs[nt + 1], refs[nt + 2]
        fu = refs[nt + 3:2 * nt + 3]
        token = refs[2 * nt + 3]
        x, y, c = _mesh_pos()
        for t in range(nt):
            mine = _half(_shard_region(fu[t], kinds[t], shard_shapes[t], 2 * x + y), kinds[t], c)
            for j, chip in enumerate(_other_chips(x, y)):
                for e in range(2):
                    pltpu.make_async_remote_copy(src_ref=mine, dst_ref=mine, send_sem=send_sems.at[6 * t + 2 * j + e],
                                                 recv_sem=recv_sems.at[6 * t + 2 * j + c], device_id=(*chip, e),
                                                 device_id_type=MESH).start()
        token[...] = jnp.zeros_like(token)

    outs = pl.pallas_call(
        body, name=name,
        out_shape=(pltpu.SemaphoreType.DMA((6 * nt,)), pltpu.SemaphoreType.DMA((6 * nt,)), *[pltpu.HBM(f.shape, f.dtype) for f in fulls],
                   _sds((8, 128), F32)),
        in_specs=[HBM] * nt + [ANY], out_specs=(SEM, SEM, *[HBM] * nt, pl.BlockSpec(memory_space=pltpu.VMEM)),
        input_output_aliases={t: 2 + t for t in range(nt)}, compiler_params=pltpu.CompilerParams(has_side_effects=EFFECT),
    )(*fulls, prev)
    return outs[0], outs[1], list(outs[2:2 + nt]), outs[2 + nt]


def _gather_wait(name, send_sems, recv_sems, fulls, kinds, shard_shapes, after):
    nt = len(fulls)

    def body(*refs):
        fu = refs[:nt]
        send_sems, recv_sems = refs[nt], refs[nt + 1]
        x, y, c = _mesh_pos()

        def part(t, chip, cc):
            return _half(_shard_region(fu[t], kinds[t], shard_shapes[t], 2 * chip[0] + chip[1]), kinds[t], cc)

        for t in range(nt):
            for j, chip in enumerate(_other_chips(x, y)):
                for e in range(2):
                    mine = part(t, (x, y), c)
                    pltpu.make_async_remote_copy(src_ref=mine, dst_ref=mine, send_sem=send_sems.at[6 * t + 2 * j + e],
                                                 recv_sem=recv_sems.at[6 * t + 2 * j + e], device_id=(x, y, c),
                                                 device_id_type=MESH).wait_send()
                    landed = part(t, chip, e)
                    pltpu.make_async_remote_copy(src_ref=landed, dst_ref=landed, send_sem=send_sems.at[6 * t + 2 * j + e],
                                                 recv_sem=recv_sems.at[6 * t + 2 * j + e], device_id=(x, y, c),
                                                 device_id_type=MESH).wait_recv()

    return pl.pallas_call(
        body, name=name, out_shape=[pltpu.HBM(f.shape, f.dtype) for f in fulls], in_specs=[HBM] * nt + [SEM, SEM, ANY],
        out_specs=[HBM] * nt, input_output_aliases={t: t for t in range(nt)},
        compiler_params=pltpu.CompilerParams(has_side_effects=EFFECT),
    )(*fulls, send_sems, recv_sems, after)


def _reduce_start(name, cps, kinds, prev):
    nt = len(cps)
    slab_shapes = []
    for cp, kind in zip(cps, kinds):
        shp = list(cp.shape)
        shp[-1 if kind == "col" else -2] //= N_CHIP
        slab_shapes.append((N_CHIP,) + tuple(shp))

    def body(*refs):
        send_sems, recv_sems = refs[nt + 1], refs[nt + 2]
        src = refs[nt + 3:2 * nt + 3]
        dst = refs[2 * nt + 3:3 * nt + 3]
        token = refs[3 * nt + 3]
        x, y, c = _mesh_pos()
        s = 2 * x + y
        for t in range(nt):
            for j, chip in enumerate(_other_chips(x, y)):
                pltpu.make_async_remote_copy(src_ref=_chip_block(src[t], kinds[t], 2 * chip[0] + chip[1]), dst_ref=dst[t].at[s],
                                             send_sem=send_sems.at[3 * t + j], recv_sem=recv_sems.at[3 * t + j],
                                             device_id=(*chip, c), device_id_type=MESH).start()
        token[...] = jnp.zeros_like(token)

    outs = pl.pallas_call(
        body, name=name,
        out_shape=(pltpu.SemaphoreType.DMA((3 * nt,)), pltpu.SemaphoreType.DMA((3 * nt,)), *[pltpu.HBM(a.shape, a.dtype) for a in cps],
                   *[pltpu.HBM(shp, BF16) for shp in slab_shapes], _sds((8, 128), F32)),
        in_specs=[HBM] * nt + [ANY], out_specs=(SEM, SEM, *[HBM] * (2 * nt), pl.BlockSpec(memory_space=pltpu.VMEM)),
        input_output_aliases={t: 2 + t for t in range(nt)}, compiler_params=pltpu.CompilerParams(has_side_effects=EFFECT),
    )(*cps, prev)
    return outs[0], outs[1], list(outs[2:2 + nt]), list(outs[2 + nt:2 + 2 * nt]), outs[2 + 2 * nt]


def _reduce_wait(name, send_sems, recv_sems, cps, slabs, kinds, after):
    nt = len(cps)

    def body(*refs):
        src, dst = refs[:nt], refs[nt:2 * nt]
        send_sems, recv_sems = refs[2 * nt], refs[2 * nt + 1]
        x, y, c = _mesh_pos()
        for t in range(nt):
            for j, chip in enumerate(_other_chips(x, y)):
                sj = 2 * chip[0] + chip[1]
                pltpu.make_async_remote_copy(src_ref=_chip_block(src[t], kinds[t], sj), dst_ref=dst[t].at[sj],
                                             send_sem=send_sems.at[3 * t + j], recv_sem=recv_sems.at[3 * t + j],
                                             device_id=(x, y, c), device_id_type=MESH).wait()

    outs = pl.pallas_call(
        body, name=name, out_shape=[pltpu.HBM(a.shape, a.dtype) for a in cps] + [pltpu.HBM(a.shape, a.dtype) for a in slabs],
        in_specs=[HBM] * (2 * nt) + [SEM, SEM, ANY], out_specs=[HBM] * (2 * nt), input_output_aliases={t: t for t in range(2 * nt)},
        compiler_params=pltpu.CompilerParams(has_side_effects=EFFECT),
    )(*cps, *slabs, send_sems, recv_sems, after)
    return list(outs[:nt]), list(outs[nt:])


def _pair_start(name, dws, kinds, prev):
    nt = len(dws)
    sib_shapes = []
    for dw, kind in zip(dws, kinds):
        shp = list(dw.shape)
        shp[-2 if kind == "col" else -1] //= 2
        sib_shapes.append(tuple(shp))

    def body(*refs):
        send_sems, recv_sems = refs[nt + 1], refs[nt + 2]
        src = refs[nt + 3:2 * nt + 3]
        dst = refs[2 * nt + 3:3 * nt + 3]
        token = refs[3 * nt + 3]
        x, y, c = _mesh_pos()
        for t in range(nt):
            pltpu.make_async_remote_copy(src_ref=_half(src[t], kinds[t], 1 - c), dst_ref=dst[t], send_sem=send_sems.at[t],
                                         recv_sem=recv_sems.at[t], device_id=(x, y, 1 - c), device_id_type=MESH).start()
        token[...] = jnp.zeros_like(token)

    outs = pl.pallas_call(
        body, name=name,
        out_shape=(pltpu.SemaphoreType.DMA((nt,)), pltpu.SemaphoreType.DMA((nt,)), *[pltpu.HBM(a.shape, a.dtype) for a in dws],
                   *[pltpu.HBM(shp, BF16) for shp in sib_shapes], _sds((8, 128), F32)),
        in_specs=[HBM] * nt + [ANY], out_specs=(SEM, SEM, *[HBM] * (2 * nt), pl.BlockSpec(memory_space=pltpu.VMEM)),
        input_output_aliases={t: 2 + t for t in range(nt)}, compiler_params=pltpu.CompilerParams(has_side_effects=EFFECT),
    )(*dws, prev)
    return outs[0], outs[1], list(outs[2:2 + nt]), list(outs[2 + nt:2 + 2 * nt]), outs[2 + 2 * nt]


def _pair_wait(name, send_sems, recv_sems, dws, sibs, kinds, after):
    nt = len(dws)

    def body(*refs):
        src, dst = refs[:nt], refs[nt:2 * nt]
        send_sems, recv_sems = refs[2 * nt], refs[2 * nt + 1]
        x, y, c = _mesh_pos()
        for t in range(nt):
            pltpu.make_async_remote_copy(src_ref=_half(src[t], kinds[t], 1 - c), dst_ref=dst[t], send_sem=send_sems.at[t],
                                         recv_sem=recv_sems.at[t], device_id=(x, y, c), device_id_type=MESH).wait()

    outs = pl.pallas_call(
        body, name=name, out_shape=[pltpu.HBM(a.shape, a.dtype) for a in dws] + [pltpu.HBM(a.shape, a.dtype) for a in sibs],
        in_specs=[HBM] * (2 * nt) + [SEM, SEM, ANY], out_specs=[HBM] * (2 * nt), input_output_aliases={t: t for t in range(2 * nt)},
        compiler_params=pltpu.CompilerParams(has_side_effects=EFFECT),
    )(*dws, *sibs, send_sems, recv_sems, after)
    return list(outs[:nt]), list(outs[nt:])


def _pair_exchange(dws, kinds):
    nt = len(dws)
    out_shapes = []
    for dw, kind in zip(dws, kinds):
        shp = list(dw.shape)
        shp[-2 if kind == "col" else -1] //= 2
        out_shapes.append(tuple(shp))

    def body(*refs):
        src, dst = refs[:nt], refs[nt:2 * nt]
        send_sems, recv_sems = refs[2 * nt:]
        x, y, c = _mesh_pos()
        copies = [pltpu.make_async_remote_copy(src_ref=_half(src[t], kinds[t], 1 - c), dst_ref=dst[t], send_sem=send_sems.at[t],
                                               recv_sem=recv_sems.at[t], device_id=(x, y, 1 - c), device_id_type=MESH)
                  for t in range(nt)]
        for cp in copies:
            cp.start()
        for cp in copies:
            cp.wait()

    return pl.pallas_call(
        body, name="grad_pair_exchange", out_shape=[_sds(shp, BF16) for shp in out_shapes], in_specs=[HBM] * nt,
        out_specs=[HBM] * nt, scratch_shapes=[pltpu.SemaphoreType.DMA((nt,)), pltpu.SemaphoreType.DMA((nt,))],
    )(*dws)


def _chip_block(ref, kind, s):
    r, cdim = ref.shape[-2:]
    if kind == "col":
        return _idx(ref, cols=pl.ds(s * (cdim // N_CHIP), cdim // N_CHIP))
    return _idx(ref, rows=pl.ds(s * (r // N_CHIP), r // N_CHIP))


def _chip_exchange(cps, kinds):
    nt = len(cps)
    out_shapes = []
    for cp, kind in zip(cps, kinds):
        shp = list(cp.shape)
        shp[-1 if kind == "col" else -2] //= N_CHIP
        out_shapes.append((N_CHIP,) + tuple(shp))

    def body(*refs):
        src, dst = refs[:nt], refs[nt:2 * nt]
        send_sems, recv_sems, local_sems = refs[2 * nt:]
        x, y, c = _mesh_pos()
        s = 2 * x + y
        chips = _other_chips(x, y)
        sends, locals_ = [], []
        for t in range(nt):
            own = pltpu.make_async_copy(_chip_block(src[t], kinds[t], s), dst[t].at[s], local_sems.at[t])
            own.start()
            locals_.append(own)
            for j, chip in enumerate(chips):
                cp = pltpu.make_async_remote_copy(src_ref=_chip_block(src[t], kinds[t], 2 * chip[0] + chip[1]), dst_ref=dst[t].at[s],
                                                  send_sem=send_sems.at[3 * t + j], recv_sem=recv_sems.at[3 * t + j],
                                                  device_id=(*chip, c), device_id_type=MESH)
                cp.start()
                sends.append(cp)
        for t in range(nt):
            for j, chip in enumerate(chips):
                landing = dst[t].at[2 * chip[0] + chip[1]]
                pltpu.make_async_remote_copy(src_ref=landing, dst_ref=landing, send_sem=send_sems.at[3 * t + j],
                                             recv_sem=recv_sems.at[3 * t + j], device_id=(x, y, c), device_id_type=MESH).wait_recv()
        for cp in sends:
            cp.wait_send()
        for own in locals_:
            own.wait()

    return pl.pallas_call(
        body, name="grad_chip_exchange", out_shape=[_sds(shp, BF16) for shp in out_shapes], in_specs=[HBM] * nt,
        out_specs=[HBM] * nt,
        scratch_shapes=[pltpu.SemaphoreType.DMA((3 * nt,)), pltpu.SemaphoreType.DMA((3 * nt,)), pltpu.SemaphoreType.DMA((nt,))],
    )(*cps)


def _pair_assemble(gs, kinds):
    nt = len(gs)

    def body(*refs):
        g = refs[nt:2 * nt]
        send_sems, recv_sems = refs[2 * nt:]
        x, y, c = _mesh_pos()
        copies = []
        for t in range(nt):
            mine = _half(g[t], kinds[t], c)
            cp = pltpu.make_async_remote_copy(src_ref=mine, dst_ref=mine, send_sem=send_sems.at[t], recv_sem=recv_sems.at[t],
                                              device_id=(x, y, 1 - c), device_id_type=MESH)
            cp.start()
            copies.append(cp)
        for t in range(nt):
            landing = _half(g[t], kinds[t], 1 - c)
            pltpu.make_async_remote_copy(src_ref=landing, dst_ref=landing, send_sem=send_sems.at[t], recv_sem=recv_sems.at[t],
                                         device_id=(x, y, c), device_id_type=MESH).wait_recv()
        for cp in copies:
            cp.wait_send()

    return pl.pallas_call(
        body, name="grad_pair_assemble", out_shape=[_sds(a.shape, F32) for a in gs], in_specs=[HBM] * nt,
        out_specs=[HBM] * nt, input_output_aliases={t: t for t in range(nt)},
        scratch_shapes=[pltpu.SemaphoreType.DMA((nt,)), pltpu.SemaphoreType.DMA((nt,))],
    )(*gs)


def _pack(arrays, width):
    flat = jnp.concatenate([a.reshape(-1) for a in arrays])
    pad = (-flat.size) % (8 * width)
    return jnp.pad(flat, (0, pad)).reshape(-1, width)


def _unpack(packed, shapes):
    flat = packed.reshape(-1)
    out, off = [], 0
    for shp in shapes:
        n = 1
        for dim in shp:
            n *= dim
        out.append(flat[off:off + n].reshape(shp))
        off += n
    return out


def kernel(x, c, ada_w, ada_b, norm_mix_g, norm_ffn_g, ab_w_in, a_conv_w, a_conv_b, a_norm_g, a_norm_b, b_norm_g, b_norm_b, b_w_s, b_bias, ab_w_out, pool_w, pool_scale, ffn_w1, ffn_w3, ffn_w2, final_g, loss_target, m_ada_w, m_ada_b, m_norm_mix_g, m_norm_ffn_g, m_ab_w_in, m_a_conv_w, m_a_conv_b, m_a_norm_g, m_a_norm_b, m_b_norm_g, m_b_norm_b, m_b_w_s, m_b_bias, m_ab_w_out, m_pool_w, m_pool_scale, m_ffn_w1, m_ffn_w3, m_ffn_w2, m_final_g, v_ada_w, v_ada_b, v_norm_mix_g, v_norm_ffn_g, v_ab_w_in, v_a_conv_w, v_a_conv_b, v_a_norm_g, v_a_norm_b, v_b_norm_g, v_b_norm_b, v_b_w_s, v_b_bias, v_ab_w_out, v_pool_w, v_pool_scale, v_ffn_w1, v_ffn_w3, v_ffn_w2, v_final_g):
    mx, my, mc = _mesh_pos()
    chip = 2 * mx + my
    dev = 4 * mx + 2 * my + mc
    x2 = x[0]
    target = loss_target[0]
    s, d = x2.shape
    depth = ada_w.shape[0]
    n_mod = ada_b.shape[1] // d
    n_even = ab_w_in.shape[0]
    da, db = a_conv_b.shape[1], b_norm_g.shape[1]
    nh = b_w_s.shape[1]
    kw = a_conv_w.shape[1]
    n_pool = pool_w.shape[1]
    tm_row = _pick(s, (256, 128))

    s_vec = jnp.reshape(chip, (1,)).astype(jnp.int32)
    c_vec = jnp.reshape(mc, (1,)).astype(jnp.int32)
    sc_vec = [s_vec, c_vec] + [jnp.reshape(v, (1,)).astype(jnp.int32)
                               for v in (2 * mx + (1 - my), 2 * (1 - mx) + my, 2 * (1 - mx) + (1 - my))]
    pool_w3 = pool_w.reshape((-1,) + pool_w.shape[2:])
    shard_of = {"w_in": (ab_w_in, "col"), "w_out": (ab_w_out, "row"), "pool": (pool_w3, "row"), "w1": (ffn_w1, "col"),
                "w3": (ffn_w3, "col"), "w2": (ffn_w2, "row")}

    def layer_names(l):
        return (["w_in", "w_out"] if l % 2 == 0 else ["pool"]) + ["w1", "w3", "w2"]

    def layer_span(nm, l):
        if nm in ("w_in", "w_out"):
            return l // 2, 1
        if nm == "pool":
            return (l // 2) * n_pool, n_pool
        return l, 1

    ffn_names = ["w1", "w3", "w2"]
    group_names = {"0a": layer_names(0)[:-3], "0b": ffn_names}
    group_layer = {"0a": 0, "0b": 0}
    for l in range(1, depth):
        group_names[l], group_layer[l] = layer_names(l), l
    owned, w_kinds, w_shapes = {}, {}, {}
    for key, names in group_names.items():
        l = group_layer[key]
        owned[key] = [_cast_into_full(shard_of[nm][0], shard_of[nm][1], s_vec, *layer_span(nm, l)) for nm in names]
        w_kinds[key] = [shard_of[nm][1] for nm in names]
        w_shapes[key] = [(layer_span(nm, l)[1],) + shard_of[nm][0].shape[1:] for nm in names]
    layer_w = [{} for _ in range(depth)]
    layer_w[0].update(zip(group_names["0a"], _allgather_weights(owned["0a"], w_kinds["0a"], w_shapes["0a"])))
    gathers = {"prev": layer_w[0][group_names["0a"][-1]], "token": None, "flying": {}}

    def gather_start(key):
        sends, recvs, fulls, token = _gather_start(f"gather_start_{key}", owned[key], w_kinds[key], w_shapes[key], gathers["prev"])
        gathers["flying"][key] = (sends, recvs, fulls)
        gathers["prev"] = gathers["token"] = token

    def gather_wait(key, after):
        sends, recvs, fulls = gathers["flying"].pop(key)
        landed = _gather_wait(f"gather_wait_{key}", sends, recvs, fulls, w_kinds[key], w_shapes[key], after)
        layer_w[group_layer[key]].update(zip(group_names[key], landed))
        gathers["prev"] = landed[-1]

    def after_starts(row):
        return row + gathers["token"][0:1, 0:1]

    gather_start("0b")

    pre = _allgather_small(_pack([c, a_conv_w, pool_scale], 128)).reshape(N_DEV, -1)
    n_cw, n_ps = a_conv_w.size, pool_scale.size
    c_all = pre[:, :d]
    cw_chips = pre[0::2, d:d + n_cw].reshape((N_CHIP,) + a_conv_w.shape)
    conv_w_full = jnp.concatenate([cw_chips[k] for k in range(N_CHIP)], axis=-1)
    ps_chips = pre[0::2, d + n_cw:d + n_cw + n_ps].reshape((N_CHIP,) + pool_scale.shape)
    pool_scale_full = jnp.concatenate([ps_chips[k] for k in range(N_CHIP)], axis=-1)
    c_pad = jnp.pad(c_all, ((0, 8), (0, 0)))
    n_ada = ada_w.shape[2]
    ada_b_mine = lax.dynamic_slice_in_dim(ada_b, chip * n_ada, n_ada, axis=1)[:, None, :]
    mod_part = _ada_fwd(c_pad, ada_w, ada_b_mine)[:, :N_DEV, :]
    mod_all = _allgather_small(mod_part.reshape(depth * N_DEV, n_ada))
    mod_chips = mod_all[0::2].reshape(N_CHIP, depth, N_DEV, n_ada)
    mod_mine = lax.dynamic_index_in_dim(mod_chips, dev, axis=2, keepdims=False)
    mod = jnp.transpose(mod_mine, (1, 0, 2)).reshape(depth, n_mod, 1, d)

    causal = jnp.tril(jnp.ones((CHUNK, CHUNK), dtype=bool))
    w_c = jnp.where(causal[None, None], b_w_s, 0.0).astype(BF16)
    bias_full = jnp.repeat(jnp.swapaxes(b_bias, 1, 2), CHUNK, axis=2)

    saved = []
    xs = x2
    for l in range(depth):
        sh1, sc1, g1, sh2, sc2, g2 = [mod[l, k] for k in range(n_mod)]
        i = l // 2
        st = {"x1": xs}
        if 1 <= l and l + 2 < depth:
            gather_start(l + 2)
        gain1 = after_starts(norm_mix_g[l][None])
        wl = layer_w[l]
        if l % 2 == 0:
            h = _norm_mod(xs, gain1, sh1, sc1, tm_row)
            proj = _mm("ab_proj", [h], [(wl["w_in"], 0)], [BF16], _ep_store, tk=2048)[0]
            cat, a1 = _ab_mid_fwd(proj, conv_w_full[i], a_conv_b[i][None], a_norm_g[i][None], a_norm_b[i][None],
                                  b_norm_g[i][None], b_norm_b[i][None], w_c[i], bias_full[i], tm_row)
            xs, y1 = _mm("ab_out", [cat], [(wl["w_out"], 0)], [F32, BF16], _ep_residual, extras=[xs, g1], extra_kinds=["tile", "row"], tk=2048)
            st.update(h=h, proj=proj, a1=a1, cat=cat, y=y1)
        else:
            p = _pool_fwd(xs, gain1, sh1, sc1, tm_row)
            gate = g1 * pool_scale_full[i][None]
            xs, ymm = _grouped_fwd(p, wl["pool"][None], 0, xs, gate, 1024)
            st.update(p=p, y=ymm, gate=gate)
        st["x2"] = xs
        gain2 = norm_ffn_g[l][None]
        if l == 0:
            gather_wait("0b", xs)
            for nxt in range(1, min(3, depth)):
                gather_start(nxt)
            gain2 = after_starts(gain2)
        h2 = _norm_mod(xs, gain2, sh2, sc2, tm_row)
        u, t, z = _mm("ffn_up", [h2, h2], [(wl["w1"], 0), (wl["w3"], 0)], [BF16, BF16, BF16], _ep_swiglu, tn=512, tk=2048)
        xs, y2 = _mm("ffn_down", [z], [(wl["w2"], 0)], [F32, BF16], _ep_residual, extras=[xs, g2], extra_kinds=["tile", "row"],
                     tn=512, tk=ffn_w2.shape[1] * N_CHIP)
        if l + 1 < depth:
            gather_wait(l + 1, xs)
        st.update(h2=h2, u=u, t=t, z=z, y2=y2)
        saved.append(st)

    def below_of(l, which):
        if which == "ffn":
            return saved[l]["y2"], mod[l, n_mod - 1]
        return saved[l]["y"], (mod[l, 2] if l % 2 == 0 else saved[l]["gate"])

    dx, fin_acc, loss_blk, dyb, gacc = _final_loss_bwd(xs, final_g[None], target, tm_row, below_of(depth - 1, "ffn"))
    loss = lax.psum(loss_blk[0, 0], MESH_AXES)
    d_final_g = fin_acc[0]
    dmod_rows = [None] * depth
    d_norm_mix, d_norm_ffn = [None] * depth, [None] * depth
    even_small = [None] * n_even
    d_pool_scale = [None] * (depth // 2)
    grad_names = ["w_in", "w_out", "pool", "w1", "w3", "w2"]
    g_shapes = {"w_in": ab_w_in.shape, "w_out": ab_w_out.shape, "pool": (pool_w.shape[0], n_pool * pool_w.shape[2], pool_w.shape[3]),
                "w1": ffn_w1.shape, "w3": ffn_w3.shape, "w2": ffn_w2.shape}
    shard_grads = {nm: None for nm in grad_names}
    pipe = {"pair": None, "chip": None, "prev": None, "token": None}

    def finish_chip(after):
        tag, pl_, names_, kinds_, sends, recvs, cps_f, slabs_f = pipe["chip"]
        cps_d, slabs_d = _reduce_wait(f"reduce_wait_{tag}", sends, recvs, cps_f, slabs_f, kinds_, after)
        for nm, kind, cp, sl in zip(names_, kinds_, cps_d, slabs_d):
            rpg = pool_w.shape[2] if nm == "pool" else None
            cp2 = cp.reshape(-1, cp.shape[-1])
            shard_grads[nm] = _chip_sum_into(sl.reshape(N_CHIP, -1, sl.shape[-1]), cp2, shard_grads[nm], g_shapes[nm],
                                             layer_span(nm, pl_)[0] // (n_pool if nm == "pool" else 1), kind, sc_vec, rpg)
        pipe["chip"] = None
        pipe["prev"] = slabs_d[-1]

    def settle(after):
        if pipe["pair"] is not None:
            tag, lyr, names, kinds_l, sends, recvs, dws_f, sibs_f = pipe["pair"]
            dws_d, sibs_d = _pair_wait(f"pair_wait_{tag}", sends, recvs, dws_f, sibs_f, kinds_l, after)
            cps = []
            for dw, sib, kind in zip(dws_d, sibs_d, kinds_l):
                cp = _pair_add(dw.reshape(-1, dw.shape[-1]), sib.reshape(-1, sib.shape[-1]), kind, c_vec)
                cps.append(cp.reshape(sib.shape))
            pipe["pair"] = None
            pipe["ready"] = (tag, lyr, names, kinds_l, cps)
        if pipe["chip"] is not None:
            finish_chip(after)

    def advance(after, new=None):
        settle(after)
        if pipe.get("ready") is not None:
            tag, lyr, names, kinds_l, cps = pipe.pop("ready")
            prev = cps[-1] if pipe["prev"] is None else pipe["prev"]
            sends, recvs, cps_f, slabs_f, token = _reduce_start(f"reduce_start_{tag}", cps, kinds_l, prev)
            pipe["chip"] = (tag, lyr, names, kinds_l, sends, recvs, cps_f, slabs_f)
            pipe["token"] = token
        if new is not None:
            tag, lyr, names, big_ = new
            kinds_l = [shard_of[nm][1] for nm in names]
            dws = [big_[nm] for nm in names]
            prev = dws[-1] if pipe["token"] is None else pipe["token"]
            sends, recvs, dws_f, sibs_f, token = _pair_start(f"pair_start_{tag}", dws, kinds_l, prev)
            pipe["pair"] = (tag, lyr, names, kinds_l, sends, recvs, dws_f, sibs_f)
            pipe["token"] = token

    def behind(row):
        return row + pipe["token"][0:1, 0:1]

    for l in reversed(range(depth)):
        sh1, sc1, g1, sh2, sc2, g2 = [mod[l, k] for k in range(n_mod)]
        st = saved[l]
        wl = layer_w[l]
        i = l // 2
        big = {}
        d_g2 = gacc[0]
        du, dt = _mm("ffn_dz", [dyb], [(wl["w2"], 0)], [BF16, BF16], _ep_swiglu_bwd, trans_b=True, extras=[st["u"], st["t"]],
                     extra_kinds=["tile", "tile"], tm=2048, tn=512, tk=2048)
        big["w2"] = _mm("ffn_dw2", [st["z"]], [dyb], [BF16], _ep_store, trans_a=True, tm=512, tn=512, tk=s)[0]
        big["w1"] = _mm("ffn_dw1", [st["h2"]], [du], [BF16], _ep_store, trans_a=True, tm=512, tn=512, tk=s)[0]
        big["w3"] = _mm("ffn_dw3", [st["h2"]], [dt], [BF16], _ep_store, trans_a=True, tm=512, tn=512, tk=s)[0]
        advance(big["w3"], (f"ffn{l}", l, ["w1", "w3", "w2"], big))
        dh = _mm("ffn_dh", [du, dt], [(wl["w1"], 0), (wl["w3"], 0)], [F32], _ep_sum, trans_b=True, tn=512, tk=ffn_w2.shape[1] * 2)[0]
        dx, nacc, dyb, gacc = _norm_mod_bwd(dh, st["x2"], dx, behind(norm_ffn_g[l][None]), sc2, tm_row, below_of(l, "mix"))
        d_sh2, d_sc2, d_norm_ffn[l] = nacc[0], nacc[1], nacc[2]
        if l % 2 == 0:
            d_g1 = gacc[0]
            big["w_out"] = _mm("ab_dw_out", [st["cat"]], [dyb], [BF16], _ep_store, trans_a=True, tm=512, tn=512, tk=s)[0]
            dcat = _mm("ab_dcat", [dyb], [(wl["w_out"], 0)], [F32], _ep_store, trans_b=True, tk=2048)[0]
            dproj, dcw, vecs, dws, dbias = _ab_mid_bwd(dcat, st["proj"], st["a1"], conv_w_full[i], a_norm_g[i][None],
                                                       a_norm_b[i][None], b_norm_g[i][None], b_norm_b[i][None], w_c[i],
                                                       bias_full[i], tm_row)
            even_small[i] = dict(conv_w=dcw, a_norm_g=vecs[0, :da], a_norm_b=vecs[1, :da], conv_b=vecs[2, :da],
                                 b_norm_g=vecs[3, :db], b_norm_b=vecs[4, :db], w_s=dws, bias=dbias[:, :, 0])
            big["w_in"] = _mm("ab_dw_in", [st["h"]], [dproj], [BF16], _ep_store, trans_a=True, tm=512, tn=512, tk=s)[0]
            advance(big["w_in"], (f"mix{l}", l, ["w_in", "w_out"], big))
            dh = _mm("ab_dh", [dproj], [(wl["w_in"], 0)], [F32], _ep_store, trans_b=True, tn=512, tk=2 * da + 2 * db)[0]
            below = below_of(l - 1, "ffn") if l > 0 else None
            outs = _norm_mod_bwd(dh, st["x1"], dx, behind(norm_mix_g[l][None]), sc1, tm_row, below)
            dx, nacc = outs[0], outs[1]
            if below is not None:
                dyb, gacc = outs[2], outs[3]
        else:
            d_g1 = gacc[0] * pool_scale_full[i]
            d_pool_scale[i] = gacc[0] * g1[0]
            big["pool"] = _grouped_dw(st["p"], dyb, n_pool, 1024)
            advance(big["pool"], (f"mix{l}", l, ["pool"], big))
            dp = _grouped_dx(dyb, wl["pool"][None], 0, 1024)
            dx, nacc, dyb, gacc = _pool_bwd(dp, st["x1"], dx, behind(norm_mix_g[l][None]), sc1, tm_row, below_of(l - 1, "ffn"))
        d_sh1, d_sc1, d_norm_mix[l] = nacc[0], nacc[1], nacc[2]
        dmod_rows[l] = jnp.concatenate([d_sh1, d_sc1, d_g1, d_sh2, d_sc2, d_g2])
    grad_x = dx[None]
    settle(dx)

    dmod = jnp.stack(dmod_rows)
    small = [dmod, jnp.stack(d_norm_mix), jnp.stack(d_norm_ffn),
             jnp.stack([e["conv_w"] for e in even_small]), jnp.stack([e["conv_b"] for e in even_small]),
             jnp.stack([e["a_norm_g"] for e in even_small]), jnp.stack([e["a_norm_b"] for e in even_small]),
             jnp.stack([e["b_norm_g"] for e in even_small]), jnp.stack([e["b_norm_b"] for e in even_small]),
             jnp.stack([e["w_s"] for e in even_small]), jnp.stack([e["bias"] for e in even_small]),
             jnp.stack(d_pool_scale), d_final_g]
    small_shapes = [a.shape for a in small]
    width = 1024 if d >= 1024 else 128
    gathered = _allgather_small(_pack(small, width))
    summed = _unpack(_sum_leading(gathered), small_shapes)
    (g_ada_b, g_norm_mix, g_norm_ffn, g_conv_w_full, g_conv_b, g_a_norm_g, g_a_norm_b, g_b_norm_g, g_b_norm_b, g_w_s, g_bias,
     g_pool_scale_full, g_final_g) = summed
    cw_cols = a_conv_w.shape[2]
    g_conv_w = lax.dynamic_slice_in_dim(g_conv_w_full, chip * cw_cols, cw_cols, axis=2)
    ps_cols = pool_scale.shape[1]
    g_pool_scale = lax.dynamic_slice_in_dim(g_pool_scale_full, chip * ps_cols, ps_cols, axis=1)

    dmod_all = gathered.reshape(N_DEV, -1)[:, :dmod.size].reshape(N_DEV, depth, n_mod * d)
    dmod_cols = lax.dynamic_slice_in_dim(dmod_all, chip * n_ada, n_ada, axis=2)
    dmod_cols = jnp.pad(jnp.transpose(dmod_cols, (1, 0, 2)), ((0, 0), (0, 8), (0, 0)))
    g_ada_w = _ada_bwd(c_pad, dmod_cols)

    weights = [ada_w, ada_b, norm_mix_g, norm_ffn_g, ab_w_in, a_conv_w, a_conv_b, a_norm_g, a_norm_b, b_norm_g, b_norm_b, b_w_s,
               b_bias, ab_w_out, pool_w, pool_scale, ffn_w1, ffn_w3, ffn_w2, final_g]
    ms = [m_ada_w, m_ada_b, m_norm_mix_g, m_norm_ffn_g, m_ab_w_in, m_a_conv_w, m_a_conv_b, m_a_norm_g, m_a_norm_b, m_b_norm_g,
          m_b_norm_b, m_b_w_s, m_b_bias, m_ab_w_out, m_pool_w, m_pool_scale, m_ffn_w1, m_ffn_w3, m_ffn_w2, m_final_g]
    vs = [v_ada_w, v_ada_b, v_norm_mix_g, v_norm_ffn_g, v_ab_w_in, v_a_conv_w, v_a_conv_b, v_a_norm_g, v_a_norm_b, v_b_norm_g,
          v_b_norm_b, v_b_w_s, v_b_bias, v_ab_w_out, v_pool_w, v_pool_scale, v_ffn_w1, v_ffn_w3, v_ffn_w2, v_final_g]
    grads = [g_ada_w, g_ada_b, g_norm_mix, g_norm_ffn, None, g_conv_w, g_conv_b, g_a_norm_g, g_a_norm_b, g_b_norm_g,
             g_b_norm_b, g_w_s, g_bias, None, None, g_pool_scale, None, None, None, g_final_g]
    updates = [None] * len(weights)

    def update(k):
        grads[k] = grads[k].reshape(weights[k].shape)
        updates[k] = _adamw(weights[k], grads[k], ms[k], vs[k])

    advance(g_ada_w)
    for k in range(1, len(weights)):
        if grads[k] is not None:
            update(k)
    update(0)
    advance(updates[0][0])
    big_at = {"w_in": 4, "w_out": 13, "pool": 14, "w1": 16, "w3": 17, "w2": 18}
    assembled = _pair_assemble([shard_grads[nm] for nm in grad_names], [shard_of[nm][1] for nm in grad_names])
    for nm, g in zip(grad_names, assembled):
        grads[big_at[nm]] = g
        update(big_at[nm])
    return (loss, grad_x, *grads, *[u[0] for u in updates], *[u[1] for u in updates], *[u[2] for u in updates])
```

```python
import functools

import jax
import jax.numpy as jnp
from jax import lax
from jax.experimental import pallas as pl
from jax.experimental.pallas import tpu as pltpu

F32 = jnp.float32
BF16 = jnp.bfloat16
EPS = 1e-6
N_DEV = 8
N_CHIP = 4
MESH_AXES = ("x", "y", "c")
MESH = pl.DeviceIdType.MESH
V7X_VMEM_LIMIT_BYTES = 56 * 1024 * 1024
SUBLANES = 8
CONV_HALO = 32
POOL_HALO = 16
POOL_WINDOWS = (2, 4, 8, 16)
CHUNK = 128
ADAM_LR, ADAM_B1, ADAM_B2, ADAM_EPS, ADAM_WD, ADAM_STEP = 0.001, 0.9, 0.999, 1e-08, 0.01, 10
HBM = pl.BlockSpec(memory_space=pltpu.HBM)


def _params(sem=None):
    return pltpu.CompilerParams(dimension_semantics=sem, vmem_limit_bytes=V7X_VMEM_LIMIT_BYTES)


def _pick(n, prefs):
    for p in prefs:
        if p <= n and n % p == 0:
            return p
    return n


def _row_tile(rows, cols, target):
    if rows * cols <= target:
        return rows
    best = None
    for d in range(16, rows, 16):
        if rows % d == 0 and d * cols <= target:
            best = d
    return best if best is not None else rows


def _sds(shape, dtype):
    return jax.ShapeDtypeStruct(tuple(shape), dtype)


def _mm_core(name, grid, nk, a_list, b_list, a_spec, b_spec, dn, extras, extra_specs, out_shapes, out_specs,
             acc_shape, epilogue):
    n, n_ex, n_out = len(a_list), len(extras), len(out_shapes)

    def body(*refs):
        a_refs, b_refs = refs[:n], refs[n:2 * n]
        ex = refs[2 * n:2 * n + n_ex]
        outs = refs[2 * n + n_ex:2 * n + n_ex + n_out]
        accs = refs[2 * n + n_ex + n_out:]
        ps = [lax.dot_general(a[...], b[...], dn, preferred_element_type=F32) for a, b in zip(a_refs, b_refs)]
        if nk == 1:
            epilogue(ps, ex, outs)
            return
        k = pl.program_id(2)

        @pl.when(k == 0)
        def _():
            for acc, p in zip(accs, ps):
                acc[...] = p

        @pl.when(k > 0)
        def _():
            for acc, p in zip(accs, ps):
                acc[...] += p

        @pl.when(k == nk - 1)
        def _():
            epilogue([acc[...] for acc in accs], ex, outs)

    scratch = [] if nk == 1 else [pltpu.VMEM(acc_shape, F32) for _ in range(n)]
    return pl.pallas_call(
        body, grid=grid, name=name,
        in_specs=[a_spec] * n + [b_spec] * n + list(extra_specs),
        out_specs=list(out_specs), out_shape=list(out_shapes), scratch_shapes=scratch,
        compiler_params=_params(("parallel", "parallel", "arbitrary")),
    )(*a_list, *b_list, *extras)


def _mm(name, a_list, b_list, out_dtypes, epilogue, *, trans_a=False, trans_b=False, extras=(), extra_kinds=(),
        tm=1024, tn=1024, tk=1024):
    layer = None
    if isinstance(b_list[0], tuple):
        layer = b_list[0][1]
        b_list = [b for b, _ in b_list]
    a0 = a_list[0]
    b_shape = b_list[0].shape[-2:]
    m, kk = (a0.shape[1], a0.shape[0]) if trans_a else a0.shape
    nn = b_shape[0] if trans_b else b_shape[1]
    tm, tn, tk = _pick(m, (tm, 512, 256, 128)), _pick(nn, (tn, 512, 256, 128)), _pick(kk, (tk, 512, 256, 128))
    nk = kk // tk
    a_spec = pl.BlockSpec((tk, tm), lambda i, j, k: (k, i)) if trans_a else pl.BlockSpec((tm, tk), lambda i, j, k: (i, k))
    if layer is None:
        b_spec = pl.BlockSpec((tn, tk), lambda i, j, k: (j, k)) if trans_b else pl.BlockSpec((tk, tn), lambda i, j, k: (k, j))
    elif trans_b:
        b_spec = pl.BlockSpec((None, tn, tk), lambda i, j, k: (layer, j, k))
    else:
        b_spec = pl.BlockSpec((None, tk, tn), lambda i, j, k: (layer, k, j))
    dn = (((0 if trans_a else 1,), (1 if trans_b else 0,)), ((), ()))
    tile = pl.BlockSpec((tm, tn), lambda i, j, k: (i, j))
    row = pl.BlockSpec((1, tn), lambda i, j, k: (0, j))
    return _mm_core(name, (m // tm, nn // tn, nk), nk, a_list, b_list, a_spec, b_spec, dn, extras,
                    [tile if kd == "tile" else row for kd in extra_kinds],
                    [_sds((m, nn), dt) for dt in out_dtypes], [tile] * len(out_dtypes), (tm, tn), epilogue)


def _ep_store(ps, ex, outs):
    outs[0][...] = ps[0].astype(outs[0].dtype)


def _ep_sum(ps, ex, outs):
    outs[0][...] = (ps[0] + ps[1]).astype(outs[0].dtype)


def _ep_swiglu(ps, ex, outs):
    u, t = ps
    sg = jax.nn.sigmoid(u)
    su = u * sg
    outs[0][...] = (t * (sg * (1.0 + u * (1.0 - sg)))).astype(BF16)
    outs[1][...] = su.astype(BF16)
    outs[2][...] = (su * t).astype(BF16)


def _ep_residual(ps, ex, outs):
    x_ref, gate_ref = ex
    y = ps[0]
    outs[0][...] = x_ref[...] + gate_ref[...] * y
    outs[1][...] = y.astype(BF16)


def _ep_swiglu_bwd(ps, ex, outs):
    dz = ps[0]
    outs[0][...] = (dz * ex[0][...].astype(F32)).astype(BF16)
    outs[1][...] = (dz * ex[1][...].astype(F32)).astype(BF16)


def _grouped_fwd(p, w, layer, x, gate, tm):
    s, d = p.shape
    _, g, kg, ng = w.shape
    tm = _pick(s, (tm, 512, 256, 128))
    tile_a = pl.BlockSpec((tm, kg), lambda i, j, k: (i, j))
    tile_o = pl.BlockSpec((tm, ng), lambda i, j, k: (i, j))
    return _mm_core("pool_mm_fwd", (s // tm, g, 1), 1, [p], [w], tile_a,
                    pl.BlockSpec((None, None, kg, ng), lambda i, j, k: (layer, j, 0, 0)), (((1,), (0,)), ((), ())),
                    [x, gate], [tile_o, pl.BlockSpec((1, ng), lambda i, j, k: (0, j))],
                    [_sds((s, g * ng), F32), _sds((s, g * ng), BF16)], [tile_o, tile_o], None, _ep_residual)


def _grouped_dx(dy, w, layer, tm):
    s, _ = dy.shape
    _, g, kg, ng = w.shape
    tm = _pick(s, (tm, 512, 256, 128))
    return _mm_core("pool_mm_dx", (s // tm, g, 1), 1, [dy], [w], pl.BlockSpec((tm, ng), lambda i, j, k: (i, j)),
                    pl.BlockSpec((None, None, kg, ng), lambda i, j, k: (layer, j, 0, 0)), (((1,), (1,)), ((), ())), [], [],
                    [_sds((s, g * kg), F32)], [pl.BlockSpec((tm, kg), lambda i, j, k: (i, j))], None, _ep_store)[0]


def _grouped_dw(p, dy, groups, tk):
    s, d = p.shape
    kg = d // groups
    ng = dy.shape[1] // groups
    tk = _pick(s, (tk, 512, 256, 128))
    nk = s // tk

    def ep(ps, ex, outs):
        outs[0][...] = ps[0].astype(BF16)

    return _mm_core("pool_mm_dw", (groups, 1, nk), nk, [p], [dy], pl.BlockSpec((tk, kg), lambda i, j, k: (k, i)),
                    pl.BlockSpec((tk, ng), lambda i, j, k: (k, i)), (((0,), (0,)), ((), ())), [], [],
                    [_sds((groups, kg, ng), BF16)], [pl.BlockSpec((None, kg, ng), lambda i, j, k: (i, 0, 0))],
                    (kg, ng), ep)[0]


def _rms_rstd(xv):
    return lax.rsqrt(jnp.mean(xv * xv, axis=-1, keepdims=True) + EPS)


def _norm_mod_math(xv, g, sh, sc):
    return ((xv * _rms_rstd(xv)) * g) * (1.0 + sc) + sh


def _norm_mod_bwd_math(dh, xv, dxo, g, sc, acc_ref):
    r = _rms_rstd(xv)
    xhat = xv * r
    acc_ref[0:1, :] += jnp.sum(dh, axis=0, keepdims=True)
    acc_ref[1:2, :] += jnp.sum(dh * (xhat * g), axis=0, keepdims=True)
    dhn = dh * (1.0 + sc)
    acc_ref[2:3, :] += jnp.sum(dhn * xhat, axis=0, keepdims=True)
    dxh = dhn * g
    return dxo + r * (dxh - xhat * jnp.mean(dxh * xhat, axis=-1, keepdims=True))


def _vec_spec(d):
    return pl.BlockSpec((1, d), lambda i: (0, 0))


def _acc_spec(d):
    return pl.BlockSpec((8, d), lambda i: (0, 0))


def _norm_mod(x, g, sh, sc, tm):
    s, d = x.shape

    def body(x_ref, g_ref, sh_ref, sc_ref, h_ref):
        h_ref[...] = _norm_mod_math(x_ref[...], g_ref[...], sh_ref[...], sc_ref[...]).astype(BF16)

    row = pl.BlockSpec((tm, d), lambda i: (i, 0))
    return pl.pallas_call(body, grid=(s // tm,), name="norm_mod", in_specs=[row] + [_vec_spec(d)] * 3, out_specs=row,
                          out_shape=_sds((s, d), BF16), compiler_params=_params(("parallel",)))(x, g, sh, sc)


def _gate_step(dxv, y_ref, gate_ref, dy_ref, gacc_ref):
    dy_ref[...] = (dxv * gate_ref[...]).astype(BF16)
    gacc_ref[0:1, :] += jnp.sum(dxv * y_ref[...].astype(F32), axis=0, keepdims=True)


def _norm_mod_bwd(dh, x, dxo, g, sc, tm, below=None):
    s, d = x.shape

    def body(dh_ref, x_ref, dxo_ref, g_ref, sc_ref, *rest):
        if below is None:
            dx_ref, acc_ref = rest
        else:
            y_ref, gate_ref, dx_ref, acc_ref, dy_ref, gacc_ref = rest

        @pl.when(pl.program_id(0) == 0)
        def _():
            acc_ref[...] = jnp.zeros_like(acc_ref)
            if below is not None:
                gacc_ref[...] = jnp.zeros_like(gacc_ref)

        dxv = _norm_mod_bwd_math(dh_ref[...], x_ref[...], dxo_ref[...], g_ref[...], sc_ref[...], acc_ref)
        dx_ref[...] = dxv
        if below is not None:
            _gate_step(dxv, y_ref, gate_ref, dy_ref, gacc_ref)

    row = pl.BlockSpec((tm, d), lambda i: (i, 0))
    extra_in = [] if below is None else [row, _vec_spec(d)]
    extra_out = [] if below is None else [row, _acc_spec(d)]
    extra_shape = [] if below is None else [_sds((s, d), BF16), _sds((8, d), F32)]
    return pl.pallas_call(body, grid=(s // tm,), name="norm_mod_bwd", in_specs=[row, row, row, _vec_spec(d), _vec_spec(d)] + extra_in,
                          out_specs=[row, _acc_spec(d)] + extra_out, out_shape=[_sds((s, d), F32), _sds((8, d), F32)] + extra_shape,
                          compiler_params=_params(("arbitrary",)))(dh, x, dxo, g, sc, *([] if below is None else below))


def _gate_bwd(dx, y, gate, tm):
    s, d = dx.shape

    def body(dx_ref, y_ref, gate_ref, dy_ref, acc_ref):
        @pl.when(pl.program_id(0) == 0)
        def _():
            acc_ref[...] = jnp.zeros_like(acc_ref)

        dxv = dx_ref[...]
        dy_ref[...] = (dxv * gate_ref[...]).astype(BF16)
        acc_ref[0:1, :] += jnp.sum(dxv * y_ref[...].astype(F32), axis=0, keepdims=True)

    row = pl.BlockSpec((tm, d), lambda i: (i, 0))
    return pl.pallas_call(body, grid=(s // tm,), name="gate_bwd", in_specs=[row, row, _vec_spec(d)],
                          out_specs=[row, _acc_spec(d)], out_shape=[_sds((s, d), BF16), _sds((8, d), F32)],
                          compiler_params=_params(("arbitrary",)))(dx, y, gate)


def _final_loss_bwd(x, g, target, tm, below):
    s, d = x.shape

    def body(x_ref, g_ref, t_ref, y_ref, gate_ref, dx_ref, acc_ref, loss_ref, dy_ref, gacc_ref):
        @pl.when(pl.program_id(0) == 0)
        def _():
            acc_ref[...] = jnp.zeros_like(acc_ref)
            loss_ref[...] = jnp.zeros_like(loss_ref)
            gacc_ref[...] = jnp.zeros_like(gacc_ref)

        xv = x_ref[...]
        gv = g_ref[...]
        r = _rms_rstd(xv)
        xhat = xv * r
        err = xhat * gv - t_ref[...]
        loss_ref[...] += (0.5 / d) * jnp.sum(err * err)
        dy = err * (1.0 / d)
        acc_ref[0:1, :] += jnp.sum(dy * xhat, axis=0, keepdims=True)
        dxh = dy * gv
        dxv = r * (dxh - xhat * jnp.mean(dxh * xhat, axis=-1, keepdims=True))
        dx_ref[...] = dxv
        _gate_step(dxv, y_ref, gate_ref, dy_ref, gacc_ref)

    row = pl.BlockSpec((tm, d), lambda i: (i, 0))
    return pl.pallas_call(body, grid=(s // tm,), name="final_loss_bwd", in_specs=[row, _vec_spec(d), row, row, _vec_spec(d)],
                          out_specs=[row, _acc_spec(d), pl.BlockSpec((8, 128), lambda i: (0, 0)), row, _acc_spec(d)],
                          out_shape=[_sds((s, d), F32), _sds((8, d), F32), _sds((8, 128), F32), _sds((s, d), BF16), _sds((8, d), F32)],
                          compiler_params=_params(("arbitrary",)))(x, g, target, *below)


def _chunks(tm, width, rb, cb):
    rb, cb = min(rb, tm), min(cb, width)
    return [(r0, c0, rb, cb) for r0 in range(0, tm, rb) for c0 in range(0, width, cb)]


def _prev_halo_map(tm, halo):
    return lambda i: (jnp.maximum(i * (tm // halo) - 1, 0), 0)


def _next_halo_map(tm, halo, s):
    return lambda i: (jnp.minimum((i + 1) * (tm // halo), s // halo - 1), 0)


def _pool_fwd(x, g, sh, sc, tm):
    s, d = x.shape
    dg = d // len(POOL_WINDOWS)

    def body(x_ref, xh_ref, g_ref, sh_ref, sc_ref, p_ref, ext_ref):
        i = pl.program_id(0)
        gv, shv, scv = g_ref[...], sh_ref[...], sc_ref[...]
        ext_ref[POOL_HALO:, :] = _norm_mod_math(x_ref[...], gv, shv, scv)
        ext_ref[0:POOL_HALO, :] = jnp.where(i == 0, 0.0, _norm_mod_math(xh_ref[...], gv, shv, scv))
        for gi, w in enumerate(POOL_WINDOWS):
            for r0, c0, rb, cb in _chunks(tm, dg, 64, 256):
                cols = pl.ds(gi * dg + c0, cb)
                tok = ext_ref[pl.ds(POOL_HALO + r0, rb), cols]
                acc = tok
                for j in range(1, w):
                    acc = acc + ext_ref[pl.ds(POOL_HALO + r0 - j, rb), cols]
                t_glob = i * tm + r0 + lax.broadcasted_iota(jnp.int32, (rb, 1), 0)
                cnt = jnp.minimum(t_glob + 1, w).astype(F32)
                p_ref[pl.ds(r0, rb), cols] = (acc / cnt - tok).astype(BF16)

    row = pl.BlockSpec((tm, d), lambda i: (i, 0))
    halo = pl.BlockSpec((POOL_HALO, d), _prev_halo_map(tm, POOL_HALO))
    return pl.pallas_call(body, grid=(s // tm,), name="pool_fwd", in_specs=[row, halo] + [_vec_spec(d)] * 3, out_specs=row,
                          out_shape=_sds((s, d), BF16), scratch_shapes=[pltpu.VMEM((tm + POOL_HALO, d), F32)],
                          compiler_params=_params(("parallel",)))(x, x, g, sh, sc)


def _pool_bwd(dp, x, dxo, g, sc, tm, below):
    s, d = x.shape
    dg = d // len(POOL_WINDOWS)
    n_tiles = s // tm

    def body(dp_ref, dph_ref, x_ref, dxo_ref, g_ref, sc_ref, y_ref, gate_ref, dx_ref, acc_ref, dy_ref, gacc_ref, ext_ref, dh_ref):
        i = pl.program_id(0)

        @pl.when(i == 0)
        def _():
            acc_ref[...] = jnp.zeros_like(acc_ref)
            gacc_ref[...] = jnp.zeros_like(gacc_ref)

        for gi, w in enumerate(POOL_WINDOWS):
            cols = pl.ds(gi * dg, dg)
            t_main = i * tm + lax.broadcasted_iota(jnp.int32, (tm, 1), 0)
            ext_ref[0:tm, cols] = dp_ref[:, cols] / jnp.minimum(t_main + 1, w).astype(F32)
            ext_ref[tm:, cols] = jnp.where(i == n_tiles - 1, 0.0, dph_ref[:, cols] * (1.0 / w))
            for r0, c0, rb, cb in _chunks(tm, dg, 64, 256):
                cc = pl.ds(gi * dg + c0, cb)
                acc = ext_ref[pl.ds(r0, rb), cc]
                for j in range(1, w):
                    acc = acc + ext_ref[pl.ds(r0 + j, rb), cc]
                dh_ref[pl.ds(r0, rb), cc] = acc - dp_ref[pl.ds(r0, rb), cc]
        dxv = _norm_mod_bwd_math(dh_ref[...], x_ref[...], dxo_ref[...], g_ref[...], sc_ref[...], acc_ref)
        dx_ref[...] = dxv
        _gate_step(dxv, y_ref, gate_ref, dy_ref, gacc_ref)

    row = pl.BlockSpec((tm, d), lambda i: (i, 0))
    halo = pl.BlockSpec((POOL_HALO, d), _next_halo_map(tm, POOL_HALO, s))
    return pl.pallas_call(body, grid=(n_tiles,), name="pool_bwd",
                          in_specs=[row, halo, row, row, _vec_spec(d), _vec_spec(d), row, _vec_spec(d)],
                          out_specs=[row, _acc_spec(d), row, _acc_spec(d)],
                          out_shape=[_sds((s, d), F32), _sds((8, d), F32), _sds((s, d), BF16), _sds((8, d), F32)],
                          scratch_shapes=[pltpu.VMEM((tm + POOL_HALO, d), F32), pltpu.VMEM((tm, d), F32)],
                          compiler_params=_params(("arbitrary",)))(dp, dp, x, dxo, g, sc, *below)


def _layernorm_fwd(v, g, b):
    mu = jnp.mean(v, axis=-1, keepdims=True)
    xc = v - mu
    rstd = lax.rsqrt(jnp.mean(xc * xc, axis=-1, keepdims=True) + EPS)
    yn = xc * rstd
    return yn * g + b, yn, rstd


def _layernorm_bwd(dz, yn, rstd, g):
    dyn = dz * g
    dv = rstd * (dyn - jnp.mean(dyn, axis=-1, keepdims=True) - yn * jnp.mean(dyn * yn, axis=-1, keepdims=True))
    return dv, jnp.sum(dz * yn, axis=0, keepdims=True), jnp.sum(dz, axis=0, keepdims=True)


def _fill_shifted(ext_ref, sh_ref):
    n = ext_ref.shape[0]
    for r in range(1, SUBLANES):
        sh_ref[r - 1, 0:n - SUBLANES, :] = ext_ref[pl.ds(r, n - SUBLANES), :]


def _shifted(ext_ref, sh_ref, start, rows, cols):
    q, r = divmod(start, SUBLANES)
    if r == 0:
        return ext_ref[pl.ds(start, rows), cols]
    return sh_ref[r - 1, pl.ds(q * SUBLANES, rows), cols]


def _ab_mid_fwd(proj, conv_w, conv_b, a_g, a_b, v_g, v_b, w_c, bias_full, tm):
    s = proj.shape[0]
    da = conv_b.shape[1]
    db = v_g.shape[1]
    nh = w_c.shape[0]
    kw = conv_w.shape[0]
    lead = CONV_HALO - (kw - 1)

    def body(p_ref, ph_ref, cw_ref, cb_ref, ag_ref, ab_ref, vg_ref, vb_ref, wc_ref, bias_ref, cat_ref, a1_ref, ext_ref,
             a1s_ref, sh_ref):
        i = pl.program_id(0)
        val = p_ref[:, 0:da].astype(F32)
        gat = p_ref[:, da:2 * da].astype(F32)
        ext_ref[CONV_HALO:, :] = val * jax.nn.sigmoid(gat)
        hv = ph_ref[:, 0:da].astype(F32)
        hg = ph_ref[:, da:2 * da].astype(F32)
        ext_ref[0:CONV_HALO, :] = jnp.where(i == 0, 0.0, hv * jax.nn.sigmoid(hg))
        _fill_shifted(ext_ref, sh_ref)
        for r0, c0, rb, cb in _chunks(tm, da, 64, 256):
            cols = pl.ds(c0, cb)
            acc = jnp.broadcast_to(cb_ref[:, cols], (rb, cb))
            for k in range(kw):
                acc = acc + cw_ref[k:k + 1, cols] * _shifted(ext_ref, sh_ref, r0 + lead + k, rb, cols)
            a1s_ref[pl.ds(r0, rb), cols] = acc
        a1 = a1s_ref[...]
        a1_ref[...] = a1.astype(BF16)
        z, _, _ = _layernorm_fwd(a1, ag_ref[...], ab_ref[...])
        cat_ref[:, 0:da] = (z * jax.nn.sigmoid(z)).astype(BF16)

        bu = p_ref[:, 2 * da:2 * da + db].astype(F32)
        bv = p_ref[:, 2 * da + db:].astype(F32)
        vn, _, _ = _layernorm_fwd(bv, vg_ref[...], vb_ref[...])
        vnb = vn.astype(BF16)
        for n in range(tm // CHUNK):
            rows = slice(n * CHUNK, (n + 1) * CHUNK)
            for h in range(nh):
                hc = slice(h * CHUNK, (h + 1) * CHUNK)
                vo = jnp.dot(wc_ref[h], vnb[rows, hc], preferred_element_type=F32) + bias_ref[:, hc]
                cat_ref[rows, da + h * CHUNK:da + (h + 1) * CHUNK] = (bu[rows, hc] * vo).astype(BF16)

    full = lambda a: pl.BlockSpec(a.shape, lambda i: (0,) * a.ndim)
    return pl.pallas_call(
        body, grid=(s // tm,), name="ab_mid_fwd",
        in_specs=[pl.BlockSpec((tm, 2 * da + 2 * db), lambda i: (i, 0)),
                  pl.BlockSpec((CONV_HALO, 2 * da), _prev_halo_map(tm, CONV_HALO)),
                  full(conv_w), full(conv_b), full(a_g), full(a_b), full(v_g), full(v_b), full(w_c), full(bias_full)],
        out_specs=[pl.BlockSpec((tm, da + db), lambda i: (i, 0)), pl.BlockSpec((tm, da), lambda i: (i, 0))],
        out_shape=[_sds((s, da + db), BF16), _sds((s, da), BF16)],
        scratch_shapes=[pltpu.VMEM((tm + CONV_HALO, da), F32), pltpu.VMEM((tm, da), F32),
                        pltpu.VMEM((SUBLANES - 1, tm + CONV_HALO, da), F32)],
        compiler_params=_params(("parallel",)),
    )(proj, proj, conv_w, conv_b, a_g, a_b, v_g, v_b, w_c, bias_full)


def _ab_mid_bwd(dcat, proj, a1, conv_w, a_g, a_b, v_g, v_b, w_c, bias_full, tm):
    s = proj.shape[0]
    da = a_g.shape[1]
    db = v_g.shape[1]
    nh = w_c.shape[0]
    kw = conv_w.shape[0]
    lead = CONV_HALO - (kw - 1)
    n_tiles = s // tm

    def body(dc_ref, dch_ref, p_ref, ph_ref, a1_ref, a1h_ref, cw_ref, ag_ref, ab_ref, vg_ref, vb_ref, wc_ref, bias_ref,
             dp_ref, dcw_ref, vec_ref, dws_ref, dbias_ref, ext_ref, dext_ref, dcw_acc, dvn_ref, sh_ref, dsh_ref):
        i = pl.program_id(0)

        @pl.when(i == 0)
        def _():
            dcw_acc[...] = jnp.zeros_like(dcw_acc)
            vec_ref[...] = jnp.zeros_like(vec_ref)
            dws_ref[...] = jnp.zeros_like(dws_ref)
            dbias_ref[...] = jnp.zeros_like(dbias_ref)

        agv, abv = ag_ref[...], ab_ref[...]

        def silu_ln_bwd(a1v, d_a2):
            z, yn, rstd = _layernorm_fwd(a1v, agv, abv)
            sg = jax.nn.sigmoid(z)
            return _layernorm_bwd(d_a2 * (sg * (1.0 + z * (1.0 - sg))), yn, rstd, agv)

        d_a1, dga, dba = silu_ln_bwd(a1_ref[...].astype(F32), dc_ref[:, 0:da])
        dext_ref[0:tm, :] = d_a1
        d_a1h, _, _ = silu_ln_bwd(a1h_ref[...].astype(F32), dch_ref[...])
        dext_ref[tm:, :] = jnp.where(i == n_tiles - 1, 0.0, d_a1h)
        vec_ref[0:1, 0:da] += dga
        vec_ref[1:2, 0:da] += dba
        vec_ref[2:3, 0:da] += jnp.sum(d_a1, axis=0, keepdims=True)

        val = p_ref[:, 0:da].astype(F32)
        sgg = jax.nn.sigmoid(p_ref[:, da:2 * da].astype(F32))
        ext_ref[CONV_HALO:, :] = val * sgg
        hv = ph_ref[:, 0:da].astype(F32)
        hg = ph_ref[:, da:2 * da].astype(F32)
        ext_ref[0:CONV_HALO, :] = jnp.where(i == 0, 0.0, hv * jax.nn.sigmoid(hg))

        _fill_shifted(ext_ref, sh_ref)
        _fill_shifted(dext_ref, dsh_ref)
        for r0, c0, rb, cb in _chunks(tm, da, 64, 256):
            cols = pl.ds(c0, cb)
            rows = pl.ds(r0, rb)
            d1 = dext_ref[rows, cols]
            acc = jnp.zeros((rb, cb), F32)
            for k in range(kw):
                acc = acc + cw_ref[k:k + 1, cols] * _shifted(dext_ref, dsh_ref, r0 + (kw - 1) - k, rb, cols)
                prod = d1 * _shifted(ext_ref, sh_ref, r0 + lead + k, rb, cols)
                dcw_acc[k, :, cols] += jnp.sum(prod.reshape(rb // SUBLANES, SUBLANES, cb), axis=0)
            v = p_ref[rows, pl.ds(c0, cb)].astype(F32)
            sg = jax.nn.sigmoid(p_ref[rows, pl.ds(da + c0, cb)].astype(F32))
            dp_ref[rows, pl.ds(c0, cb)] = (acc * sg).astype(BF16)
            dp_ref[rows, pl.ds(da + c0, cb)] = (acc * v * sg * (1.0 - sg)).astype(BF16)

        vgv = vg_ref[...]
        bu = p_ref[:, 2 * da:2 * da + db].astype(F32)
        bv = p_ref[:, 2 * da + db:].astype(F32)
        vn, yn_v, rstd_v = _layernorm_fwd(bv, vgv, vb_ref[...])
        vnb = vn.astype(BF16)
        for n in range(tm // CHUNK):
            rows = slice(n * CHUNK, (n + 1) * CHUNK)
            for h in range(nh):
                hc = slice(h * CHUNK, (h + 1) * CHUNK)
                wch = wc_ref[h]
                blk = vnb[rows, hc]
                vo = jnp.dot(wch, blk, preferred_element_type=F32) + bias_ref[:, hc]
                d_bout = dc_ref[rows, da + h * CHUNK:da + (h + 1) * CHUNK]
                dp_ref[rows, 2 * da + h * CHUNK:2 * da + (h + 1) * CHUNK] = (d_bout * vo).astype(BF16)
                d_vo = d_bout * bu[rows, hc]
                dbias_ref[h] += jnp.sum(d_vo, axis=1, keepdims=True)
                d_vob = d_vo.astype(BF16)
                dws_ref[h] += lax.dot_general(d_vob, blk, (((1,), (1,)), ((), ())), preferred_element_type=F32)
                dvn_ref[rows, hc] = lax.dot_general(wch, d_vob, (((0,), (0,)), ((), ())), preferred_element_type=F32)
        d_bv, dgv, dbv = _layernorm_bwd(dvn_ref[...], yn_v, rstd_v, vgv)
        dp_ref[:, 2 * da + db:] = d_bv.astype(BF16)
        vec_ref[3:4, 0:db] += dgv
        vec_ref[4:5, 0:db] += dbv

        @pl.when(i == n_tiles - 1)
        def _():
            dcw_ref[...] = jnp.sum(dcw_acc[...], axis=1)
            causal = lax.broadcasted_iota(jnp.int32, (CHUNK, CHUNK), 0) >= lax.broadcasted_iota(jnp.int32, (CHUNK, CHUNK), 1)
            for h in range(nh):
                dws_ref[h] = jnp.where(causal, dws_ref[h], 0.0)

    full = lambda a: pl.BlockSpec(a.shape, lambda i: (0,) * a.ndim)
    wide = max(da, db)
    return pl.pallas_call(
        body, grid=(n_tiles,), name="ab_mid_bwd",
        in_specs=[pl.BlockSpec((tm, da + db), lambda i: (i, 0)),
                  pl.BlockSpec((CONV_HALO, da), _next_halo_map(tm, CONV_HALO, s)),
                  pl.BlockSpec((tm, 2 * da + 2 * db), lambda i: (i, 0)),
                  pl.BlockSpec((CONV_HALO, 2 * da), _prev_halo_map(tm, CONV_HALO)),
                  pl.BlockSpec((tm, da), lambda i: (i, 0)),
                  pl.BlockSpec((CONV_HALO, da), _next_halo_map(tm, CONV_HALO, s)),
                  full(conv_w), full(a_g), full(a_b), full(v_g), full(v_b), full(w_c), full(bias_full)],
        out_specs=[pl.BlockSpec((tm, 2 * da + 2 * db), lambda i: (i, 0)),
                   pl.BlockSpec((kw, da), lambda i: (0, 0)),
                   pl.BlockSpec((8, wide), lambda i: (0, 0)),
                   pl.BlockSpec((nh, CHUNK, CHUNK), lambda i: (0, 0, 0)),
                   pl.BlockSpec((nh, CHUNK, 1), lambda i: (0, 0, 0))],
        out_shape=[_sds((s, 2 * da + 2 * db), BF16), _sds((kw, da), F32), _sds((8, wide), F32),
                   _sds((nh, CHUNK, CHUNK), F32), _sds((nh, CHUNK, 1), F32)],
        scratch_shapes=[pltpu.VMEM((tm + CONV_HALO, da), F32), pltpu.VMEM((tm + CONV_HALO, da), F32),
                        pltpu.VMEM((kw, SUBLANES, da), F32), pltpu.VMEM((tm, db), F32),
                        pltpu.VMEM((SUBLANES - 1, tm + CONV_HALO, da), F32), pltpu.VMEM((SUBLANES - 1, tm + CONV_HALO, da), F32)],
        compiler_params=_params(("arbitrary",)),
    )(dcat, dcat, proj, proj, a1, a1, conv_w, a_g, a_b, v_g, v_b, w_c, bias_full)


def _ada_fwd(c_all, w, b):
    nl, d, n = w.shape
    tn = _pick(n, (512, 256, 128))

    def body(c_ref, w_ref, b_ref, o_ref):
        cv = c_ref[...]
        cond = (cv * jax.nn.sigmoid(cv)).astype(BF16)
        o_ref[...] = jnp.dot(cond, w_ref[...].astype(BF16), preferred_element_type=F32) + b_ref[...]

    return pl.pallas_call(
        body, grid=(nl, n // tn), name="ada_fwd",
        in_specs=[pl.BlockSpec(c_all.shape, lambda l, j: (0, 0)), pl.BlockSpec((None, d, tn), lambda l, j: (l, 0, j)),
                  pl.BlockSpec((None, 1, tn), lambda l, j: (l, 0, j))],
        out_specs=pl.BlockSpec((None, c_all.shape[0], tn), lambda l, j: (l, 0, j)),
        out_shape=_sds((nl, c_all.shape[0], n), F32), compiler_params=_params(("parallel", "parallel")),
    )(c_all, w, b)


def _ada_bwd(c_all, dmod):
    nl, nb, n = dmod.shape
    d = c_all.shape[1]
    tn = _pick(n, (512, 256, 128))

    def body(c_ref, g_ref, o_ref):
        cv = c_ref[...]
        cond = (cv * jax.nn.sigmoid(cv)).astype(BF16)
        o_ref[...] = lax.dot_general(cond, g_ref[...].astype(BF16), (((0,), (0,)), ((), ())), preferred_element_type=F32)

    return pl.pallas_call(
        body, grid=(nl, n // tn), name="ada_bwd",
        in_specs=[pl.BlockSpec(c_all.shape, lambda l, j: (0, 0)), pl.BlockSpec((None, nb, tn), lambda l, j: (l, 0, j))],
        out_specs=pl.BlockSpec((None, d, tn), lambda l, j: (l, 0, j)),
        out_shape=_sds((nl, d, n), F32), compiler_params=_params(("parallel", "parallel")),
    )(c_all, dmod)


def _sum_leading(a, out_dtype=F32, name="sum_leading"):
    n, r, c = a.shape
    tr = _row_tile(r, c, 256 * 1024)

    def body(a_ref, o_ref):
        acc = a_ref[0].astype(F32)
        for k in range(1, n):
            acc = acc + a_ref[k].astype(F32)
        o_ref[...] = acc.astype(out_dtype)

    return pl.pallas_call(body, grid=(r // tr,), name=name, in_specs=[pl.BlockSpec((n, tr, c), lambda i: (0, i, 0))],
                          out_specs=pl.BlockSpec((tr, c), lambda i: (i, 0)), out_shape=_sds((r, c), out_dtype),
                          compiler_params=_params(("parallel",)))(a)


def _cast_into_full(w, kind, s_vec, l0, nl):
    _, r, c = w.shape
    tr = _row_tile(r, c, 512 * 1024)
    nb = r // tr
    if kind == "col":
        out_shape, out_spec = (nl, r, N_CHIP * c), pl.BlockSpec((None, tr, c), lambda l, i, sv: (l, i, sv[0]))
    else:
        out_shape, out_spec = (nl, N_CHIP * r, c), pl.BlockSpec((None, tr, c), lambda l, i, sv: (l, sv[0] * nb + i, 0))

    def body(sv_ref, w_ref, o_ref):
        o_ref[...] = w_ref[...].astype(BF16)

    return pl.pallas_call(
        body, name="cast_into_full",
        grid_spec=pltpu.PrefetchScalarGridSpec(num_scalar_prefetch=1, grid=(nl, nb),
                                               in_specs=[pl.BlockSpec((None, tr, c), lambda l, i, sv: (l0 + l, i, 0))], out_specs=out_spec),
        out_shape=_sds(out_shape, BF16), compiler_params=_params(("parallel", "parallel")),
    )(s_vec, w)


def _chip_sum_into(slab, cp, g, g_shape, layer, kind, sc_vec, rows_per_group=None):
    n, r, c = slab.shape
    rg = r if rows_per_group is None else rows_per_group
    tr = _row_tile(rg, c, 256 * 1024)
    groups = r // rg
    nbg = rg // tr
    nb = groups * nbg
    n_sc = len(sc_vec)
    if kind == "col":
        out_spec = pl.BlockSpec((None, tr, c), lambda gi, i, *sc: (layer, sc[1][0] * nb + gi * nbg + i, 0))
        own_spec = pl.BlockSpec((tr, c), lambda gi, i, *sc: (gi * nbg + i, sc[0][0]))
    else:
        out_spec = pl.BlockSpec((None, tr, c), lambda gi, i, *sc: (layer, gi * nbg + i, sc[1][0]))
        own_spec = pl.BlockSpec((tr, c), lambda gi, i, *sc: (gi * (n * nbg) + sc[0][0] * nbg + i, 0))

    def other(k):
        return pl.BlockSpec((None, tr, c), lambda gi, i, *sc: (sc[1 + k][0], gi * nbg + i, 0))

    in_specs = [own_spec] + [other(k) for k in range(1, n)]
    args = list(sc_vec) + [cp] + [slab] * (n - 1)
    aliases = {}
    if g is not None:
        in_specs.append(pl.BlockSpec(memory_space=pl.ANY))
        args.append(g)
        aliases = {len(args) - 1: 0}

    def body(*refs):
        own_ref, rest = refs[n_sc], refs[n_sc + 1:]
        o_ref = rest[-1]
        acc = own_ref[...].astype(F32)
        for k in range(n - 1):
            acc = acc + rest[k][...].astype(F32)
        o_ref[...] = acc

    return pl.pallas_call(
        body, name="chip_sum",
        grid_spec=pltpu.PrefetchScalarGridSpec(num_scalar_prefetch=n_sc, grid=(groups, nbg), in_specs=in_specs, out_specs=out_spec),
        out_shape=_sds(g_shape, F32), input_output_aliases=aliases, compiler_params=_params(("parallel", "parallel")),
    )(*args)


def _pair_add(dw, sib, kind, c_vec):
    r, c = sib.shape
    tr = _row_tile(r, c, 512 * 1024)
    nb = r // tr
    if kind == "col":
        dw_spec = pl.BlockSpec((tr, c), lambda i, cv: (cv[0] * nb + i, 0))
    else:
        dw_spec = pl.BlockSpec((tr, c), lambda i, cv: (i, cv[0]))

    def body(cv_ref, dw_ref, sib_ref, o_ref):
        o_ref[...] = (dw_ref[...].astype(F32) + sib_ref[...].astype(F32)).astype(BF16)

    return pl.pallas_call(
        body, name="pair_add",
        grid_spec=pltpu.PrefetchScalarGridSpec(num_scalar_prefetch=1, grid=(nb,), in_specs=[dw_spec, pl.BlockSpec((tr, c), lambda i, cv: (i, 0))],
                                               out_specs=pl.BlockSpec((tr, c), lambda i, cv: (i, 0))),
        out_shape=_sds((r, c), BF16), compiler_params=_params(("parallel",)),
    )(c_vec, dw, sib)


def _adamw(w, g, m, v):
    shape = w.shape
    cols = shape[-1]
    rows = w.size // cols
    tr = _row_tile(rows, cols, 256 * 1024)
    bc1 = 1.0 - ADAM_B1 ** ADAM_STEP
    bc2 = 1.0 - ADAM_B2 ** ADAM_STEP

    def body(w_ref, g_ref, m_ref, v_ref, d_ref, mo_ref, vo_ref):
        gv = g_ref[...]
        mn = ADAM_B1 * m_ref[...] + (1.0 - ADAM_B1) * gv
        vn = ADAM_B2 * v_ref[...] + (1.0 - ADAM_B2) * (gv * gv)
        d_ref[...] = -ADAM_LR * ((mn / bc1) / (jnp.sqrt(vn / bc2) + ADAM_EPS) + ADAM_WD * w_ref[...])
        mo_ref[...] = mn
        vo_ref[...] = vn

    spec = pl.BlockSpec((tr, cols), lambda i: (i, 0))
    outs = pl.pallas_call(body, grid=(rows // tr,), name="adamw", in_specs=[spec] * 4, out_specs=[spec] * 3,
                          out_shape=[_sds((rows, cols), F32)] * 3, compiler_params=_params(("parallel",)))(
        *[a.reshape(rows, cols) for a in (w, g, m, v)])
    return [o.reshape(shape) for o in outs]


def _mesh_pos():
    return lax.axis_index("x"), lax.axis_index("y"), lax.axis_index("c")


def _other_chips(x, y):
    return [(1 - x, y), (x, 1 - y), (1 - x, 1 - y)]


def _allgather_small(a, after=()):
    r, c = a.shape

    def body(x_ref, *rest):
        out_ref, send_sems, recv_sems, local_sem = rest[len(after):]
        x, y, cc = _mesh_pos()
        me, sibling = (x, y, cc), (x, y, 1 - cc)
        chips = _other_chips(x, y)

        def slab(px, py, pc):
            return out_ref.at[4 * px + 2 * py + pc]

        def copy(k, block, to, src=None):
            return pltpu.make_async_remote_copy(src_ref=slab(*block) if src is None else src, dst_ref=slab(*block),
                                                send_sem=send_sems.at[k], recv_sem=recv_sems.at[k], device_id=to,
                                                device_id_type=MESH)

        mine = pltpu.make_async_copy(x_ref, slab(*me), local_sem)
        mine.start()
        first = [copy(0, me, sibling, src=x_ref)]
        first += [copy(1 + j, me, (*chip, cc), src=x_ref) for j, chip in enumerate(chips)]
        for cp in first:
            cp.start()
        passed = [copy(4 + j, (*chip, cc), sibling) for j, chip in enumerate(chips)]
        for j, chip in enumerate(chips):
            copy(1 + j, (*chip, cc), me).wait_recv()
            passed[j].start()
        copy(0, sibling, me).wait_recv()
        for j, chip in enumerate(chips):
            copy(4 + j, (*chip, 1 - cc), me).wait_recv()
        for cp in first + passed:
            cp.wait_send()
        mine.wait()

    return pl.pallas_call(
        body, name="allgather_small", out_shape=_sds((N_DEV, r, c), F32),
        in_specs=[pl.BlockSpec(memory_space=pltpu.VMEM)] + [pl.BlockSpec(memory_space=pl.ANY)] * len(after),
        out_specs=pl.BlockSpec(memory_space=pltpu.VMEM),
        scratch_shapes=[pltpu.SemaphoreType.DMA((7,)), pltpu.SemaphoreType.DMA((7,)), pltpu.SemaphoreType.DMA],
        compiler_params=pltpu.CompilerParams(vmem_limit_bytes=V7X_VMEM_LIMIT_BYTES),
    )(a, *after)


def _idx(ref, rows=None, cols=None):
    lead = (slice(None),) * (len(ref.shape) - 2)
    return ref.at[lead + (slice(None) if rows is None else rows, slice(None) if cols is None else cols)]


def _half(ref, kind, c):
    r, cdim = ref.shape[-2:]
    if kind == "col":
        return _idx(ref, rows=pl.ds(c * (r // 2), r // 2))
    return _idx(ref, cols=pl.ds(c * (cdim // 2), cdim // 2))


def _shard_region(full, kind, shard_shape, s):
    r, cdim = shard_shape[-2:]
    if kind == "col":
        return _idx(full, cols=pl.ds(s * cdim, cdim))
    return _idx(full, rows=pl.ds(s * r, r))


def _allgather_weights(fulls, kinds, shard_shapes):
    nt = len(fulls)

    def body(*refs):
        fu = refs[nt:2 * nt]
        send_sems, recv_sems = refs[2 * nt:]
        x, y, c = _mesh_pos()
        chips = _other_chips(x, y)

        def part(t, chip, cc):
            return _half(_shard_region(fu[t], kinds[t], shard_shapes[t], 2 * chip[0] + chip[1]), kinds[t], cc)

        def copy(t, k, blk, to):
            return pltpu.make_async_remote_copy(src_ref=blk, dst_ref=blk, send_sem=send_sems.at[6 * t + k],
                                                recv_sem=recv_sems.at[6 * t + k], device_id=to, device_id_type=MESH)

        first, passed = [], []
        for t in range(nt):
            for j, chip in enumerate(chips):
                cp = copy(t, j, part(t, (x, y), c), (*chip, c))
                cp.start()
                first.append(cp)
        for t in range(nt):
            for j, chip in enumerate(chips):
                copy(t, j, part(t, chip, c), (x, y, c)).wait_recv()
                fw = copy(t, 3 + j, part(t, chip, c), (x, y, 1 - c))
                fw.start()
                passed.append(fw)
        for t in range(nt):
            for j, chip in enumerate(chips):
                copy(t, 3 + j, part(t, chip, 1 - c), (x, y, c)).wait_recv()
        for cp in first + passed:
            cp.wait_send()

    return pl.pallas_call(
        body, name="allgather_weights", out_shape=[_sds(f.shape, BF16) for f in fulls],
        in_specs=[HBM] * nt, out_specs=[HBM] * nt, input_output_aliases={t: t for t in range(nt)},
        scratch_shapes=[pltpu.SemaphoreType.DMA((6 * nt,)), pltpu.SemaphoreType.DMA((6 * nt,))],
    )(*fulls)


SEM = pl.BlockSpec(memory_space=pltpu.SEMAPHORE)
ANY = pl.BlockSpec(memory_space=pl.ANY)
EFFECT = pltpu.SideEffectType.DATAFLOW_SIDE_EFFECTING


def _gather_start(name, fulls, kinds, shard_shapes, prev):
    nt = len(fulls)

    def body(*refs):
        send_sems, recv_sems = refs[nt + 1], refs[nt + 2]
        fu = refs[nt + 3:2 * nt + 3]
        token = refs[2 * nt + 3]
        x, y, c = _mesh_pos()
        for t in range(nt):
            mine = _half(_shard_region(fu[t], kinds[t], shard_shapes[t], 2 * x + y), kinds[t], c)
            for j, chip in enumerate(_other_chips(x, y)):
                for e in range(2):
                    pltpu.make_async_remote_copy(src_ref=mine, dst_ref=mine, send_sem=send_sems.at[6 * t + 2 * j + e],
                                                 recv_sem=recv_sems.at[6 * t + 2 * j + c], device_id=(*chip, e),
                                                 device_id_type=MESH).start()
        token[...] = jnp.zeros_like(token)

    outs = pl.pallas_call(
        body, name=name,
        out_shape=(pltpu.SemaphoreType.DMA((6 * nt,)), pltpu.SemaphoreType.DMA((6 * nt,)), *[pltpu.HBM(f.shape, f.dtype) for f in fulls],
                   _sds((8, 128), F32)),
        in_specs=[HBM] * nt + [ANY], out_specs=(SEM, SEM, *[HBM] * nt, pl.BlockSpec(memory_space=pltpu.VMEM)),
        input_output_aliases={t: 2 + t for t in range(nt)}, compiler_params=pltpu.CompilerParams(has_side_effects=EFFECT),
    )(*fulls, prev)
    return outs[0], outs[1], list(outs[2:2 + nt]), outs[2 + nt]


def _gather_wait(name, send_sems, recv_sems, fulls, kinds, shard_shapes, after):
    nt = len(fulls)

    def body(*refs):
        fu = refs[:nt]
        send_sems, recv_sems = refs[nt], refs[nt + 1]
        x, y, c = _mesh_pos()

        def part(t, chip, cc):
            return _half(_shard_region(fu[t], kinds[t], shard_shapes[t], 2 * chip[0] + chip[1]), kinds[t], cc)

        for t in range(nt):
            for j, chip in enumerate(_other_chips(x, y)):
                for e in range(2):
                    mine = part(t, (x, y), c)
                    pltpu.make_async_remote_copy(src_ref=mine, dst_ref=mine, send_sem=send_sems.at[6 * t + 2 * j + e],
                                                 recv_sem=recv_sems.at[6 * t + 2 * j + e], device_id=(x, y, c),
                                                 device_id_type=MESH).wait_send()
                    landed = part(t, chip, e)
                    pltpu.make_async_remote_copy(src_ref=landed, dst_ref=landed, send_sem=send_sems.at[6 * t + 2 * j + e],
                                                 recv_sem=recv_sems.at[6 * t + 2 * j + e], device_id=(x, y, c),
                                                 device_id_type=MESH).wait_recv()

    return pl.pallas_call(
        body, name=name, out_shape=[pltpu.HBM(f.shape, f.dtype) for f in fulls], in_specs=[HBM] * nt + [SEM, SEM, ANY],
        out_specs=[HBM] * nt, input_output_aliases={t: t for t in range(nt)},
        compiler_params=pltpu.CompilerParams(has_side_effects=EFFECT),
    )(*fulls, send_sems, recv_sems, after)


def _reduce_start(name, cps, kinds, prev):
    nt = len(cps)
    slab_shapes = []
    for cp, kind in zip(cps, kinds):
        shp = list(cp.shape)
        shp[-1 if kind == "col" else -2] //= N_CHIP
        slab_shapes.append((N_CHIP,) + tuple(shp))

    def body(*refs):
        send_sems, recv_sems = refs[nt + 1], refs[nt + 2]
        src = refs[nt + 3:2 * nt + 3]
        dst = refs[2 * nt + 3:3 * nt + 3]
        token = refs[3 * nt + 3]
        x, y, c = _mesh_pos()
        s = 2 * x + y
        for t in range(nt):
            for j, chip in enumerate(_other_chips(x, y)):
                pltpu.make_async_remote_copy(src_ref=_chip_block(src[t], kinds[t], 2 * chip[0] + chip[1]), dst_ref=dst[t].at[s],
                                             send_sem=send_sems.at[3 * t + j], recv_sem=recv_sems.at[3 * t + j],
                                             device_id=(*chip, c), device_id_type=MESH).start()
        token[...] = jnp.zeros_like(token)

    outs = pl.pallas_call(
        body, name=name,
        out_shape=(pltpu.SemaphoreType.DMA((3 * nt,)), pltpu.SemaphoreType.DMA((3 * nt,)), *[pltpu.HBM(a.shape, a.dtype) for a in cps],
                   *[pltpu.HBM(shp, BF16) for shp in slab_shapes], _sds((8, 128), F32)),
        in_specs=[HBM] * nt + [ANY], out_specs=(SEM, SEM, *[HBM] * (2 * nt), pl.BlockSpec(memory_space=pltpu.VMEM)),
        input_output_aliases={t: 2 + t for t in range(nt)}, compiler_params=pltpu.CompilerParams(has_side_effects=EFFECT),
    )(*cps, prev)
    return outs[0], outs[1], list(outs[2:2 + nt]), list(outs[2 + nt:2 + 2 * nt]), outs[2 + 2 * nt]


def _reduce_wait(name, send_sems, recv_sems, cps, slabs, kinds, after):
    nt = len(cps)

    def body(*refs):
        src, dst = refs[:nt], refs[nt:2 * nt]
        send_sems, recv_sems = refs[2 * nt], refs[2 * nt + 1]
        x, y, c = _mesh_pos()
        for t in range(nt):
            for j, chip in enumerate(_other_chips(x, y)):
                sj = 2 * chip[0] + chip[1]
                pltpu.make_async_remote_copy(src_ref=_chip_block(src[t], kinds[t], sj), dst_ref=dst[t].at[sj],
                                             send_sem=send_sems.at[3 * t + j], recv_sem=recv_sems.at[3 * t + j],
                                             device_id=(x, y, c), device_id_type=MESH).wait()

    outs = pl.pallas_call(
        body, name=name, out_shape=[pltpu.HBM(a.shape, a.dtype) for a in cps] + [pltpu.HBM(a.shape, a.dtype) for a in slabs],
        in_specs=[HBM] * (2 * nt) + [SEM, SEM, ANY], out_specs=[HBM] * (2 * nt), input_output_aliases={t: t for t in range(2 * nt)},
        compiler_params=pltpu.CompilerParams(has_side_effects=EFFECT),
    )(*cps, *slabs, send_sems, recv_sems, after)
    return list(outs[:nt]), list(outs[nt:])


def _pair_start(name, dws, kinds, prev):
    nt = len(dws)
    sib_shapes = []
    for dw, kind in zip(dws, kinds):
        shp = list(dw.shape)
        shp[-2 if kind == "col" else -1] //= 2
        sib_shapes.append(tuple(shp))

    def body(*refs):
        send_sems, recv_sems = refs[nt + 1], refs[nt + 2]
        src = refs[nt + 3:2 * nt + 3]
        dst = refs[2 * nt + 3:3 * nt + 3]
        token = refs[3 * nt + 3]
        x, y, c = _mesh_pos()
        for t in range(nt):
            pltpu.make_async_remote_copy(src_ref=_half(src[t], kinds[t], 1 - c), dst_ref=dst[t], send_sem=send_sems.at[t],
                                         recv_sem=recv_sems.at[t], device_id=(x, y, 1 - c), device_id_type=MESH).start()
        token[...] = jnp.zeros_like(token)

    outs = pl.pallas_call(
        body, name=name,
        out_shape=(pltpu.SemaphoreType.DMA((nt,)), pltpu.SemaphoreType.DMA((nt,)), *[pltpu.HBM(a.shape, a.dtype) for a in dws],
                   *[pltpu.HBM(shp, BF16) for shp in sib_shapes], _sds((8, 128), F32)),
        in_specs=[HBM] * nt + [ANY], out_specs=(SEM, SEM, *[HBM] * (2 * nt), pl.BlockSpec(memory_space=pltpu.VMEM)),
        input_output_aliases={t: 2 + t for t in range(nt)}, compiler_params=pltpu.CompilerParams(has_side_effects=EFFECT),
    )(*dws, prev)
    return outs[0], outs[1], list(outs[2:2 + nt]), list(outs[2 + nt:2 + 2 * nt]), outs[2 + 2 * nt]


def _pair_wait(name, send_sems, recv_sems, dws, sibs, kinds, after):
    nt = len(dws)

    def body(*refs):
        src, dst = refs[:nt], refs[nt:2 * nt]
        send_sems, recv_sems = refs[2 * nt], refs[2 * nt + 1]
        x, y, c = _mesh_pos()
        for t in range(nt):
            pltpu.make_async_remote_copy(src_ref=_half(src[t], kinds[t], 1 - c), dst_ref=dst[t], send_sem=send_sems.at[t],
                                         recv_sem=recv_sems.at[t], device_id=(x, y, c), device_id_type=MESH).wait()

    outs = pl.pallas_call(
        body, name=name, out_shape=[pltpu.HBM(a.shape, a.dtype) for a in dws] + [pltpu.HBM(a.shape, a.dtype) for a in sibs],
        in_specs=[HBM] * (2 * nt) + [SEM, SEM, ANY], out_specs=[HBM] * (2 * nt), input_output_aliases={t: t for t in range(2 * nt)},
        compiler_params=pltpu.CompilerParams(has_side_effects=EFFECT),
    )(*dws, *sibs, send_sems, recv_sems, after)
    return list(outs[:nt]), list(outs[nt:])


def _pair_exchange(dws, kinds):
    nt = len(dws)
    out_shapes = []
    for dw, kind in zip(dws, kinds):
        shp = list(dw.shape)
        shp[-2 if kind == "col" else -1] //= 2
        out_shapes.append(tuple(shp))

    def body(*refs):
        src, dst = refs[:nt], refs[nt:2 * nt]
        send_sems, recv_sems = refs[2 * nt:]
        x, y, c = _mesh_pos()
        copies = [pltpu.make_async_remote_copy(src_ref=_half(src[t], kinds[t], 1 - c), dst_ref=dst[t], send_sem=send_sems.at[t],
                                               recv_sem=recv_sems.at[t], device_id=(x, y, 1 - c), device_id_type=MESH)
                  for t in range(nt)]
        for cp in copies:
            cp.start()
        for cp in copies:
            cp.wait()

    return pl.pallas_call(
        body, name="grad_pair_exchange", out_shape=[_sds(shp, BF16) for shp in out_shapes], in_specs=[HBM] * nt,
        out_specs=[HBM] * nt, scratch_shapes=[pltpu.SemaphoreType.DMA((nt,)), pltpu.SemaphoreType.DMA((nt,))],
    )(*dws)


def _chip_block(ref, kind, s):
    r, cdim = ref.shape[-2:]
    if kind == "col":
        return _idx(ref, cols=pl.ds(s * (cdim // N_CHIP), cdim // N_CHIP))
    return _idx(ref, rows=pl.ds(s * (r // N_CHIP), r // N_CHIP))


def _chip_exchange(cps, kinds):
    nt = len(cps)
    out_shapes = []
    for cp, kind in zip(cps, kinds):
        shp = list(cp.shape)
        shp[-1 if kind == "col" else -2] //= N_CHIP
        out_shapes.append((N_CHIP,) + tuple(shp))

    def body(*refs):
        src, dst = refs[:nt], refs[nt:2 * nt]
        send_sems, recv_sems, local_sems = refs[2 * nt:]
        x, y, c = _mesh_pos()
        s = 2 * x + y
        chips = _other_chips(x, y)
        sends, locals_ = [], []
        for t in range(nt):
            own = pltpu.make_async_copy(_chip_block(src[t], kinds[t], s), dst[t].at[s], local_sems.at[t])
            own.start()
            locals_.append(own)
            for j, chip in enumerate(chips):
                cp = pltpu.make_async_remote_copy(src_ref=_chip_block(src[t], kinds[t], 2 * chip[0] + chip[1]), dst_ref=dst[t].at[s],
                                                  send_sem=send_sems.at[3 * t + j], recv_sem=recv_sems.at[3 * t + j],
                                                  device_id=(*chip, c), device_id_type=MESH)
                cp.start()
                sends.append(cp)
        for t in range(nt):
            for j, chip in enumerate(chips):
                landing = dst[t].at[2 * chip[0] + chip[1]]
                pltpu.make_async_remote_copy(src_ref=landing, dst_ref=landing, send_sem=send_sems.at[3 * t + j],
                                             recv_sem=recv_sems.at[3 * t + j], device_id=(x, y, c), device_id_type=MESH).wait_recv()
        for cp in sends:
            cp.wait_send()
        for own in locals_:
            own.wait()

    return pl.pallas_call(
        body, name="grad_chip_exchange", out_shape=[_sds(shp, BF16) for shp in out_shapes], in_specs=[HBM] * nt,
        out_specs=[HBM] * nt,
        scratch_shapes=[pltpu.SemaphoreType.DMA((3 * nt,)), pltpu.SemaphoreType.DMA((3 * nt,)), pltpu.SemaphoreType.DMA((nt,))],
    )(*cps)


def _pair_assemble(gs, kinds):
    nt = len(gs)

    def body(*refs):
        g = refs[nt:2 * nt]
        send_sems, recv_sems = refs[2 * nt:]
        x, y, c = _mesh_pos()
        copies = []
        for t in range(nt):
            mine = _half(g[t], kinds[t], c)
            cp = pltpu.make_async_remote_copy(src_ref=mine, dst_ref=mine, send_sem=send_sems.at[t], recv_sem=recv_sems.at[t],
                                              device_id=(x, y, 1 - c), device_id_type=MESH)
            cp.start()
            copies.append(cp)
        for t in range(nt):
            landing = _half(g[t], kinds[t], 1 - c)
            pltpu.make_async_remote_copy(src_ref=landing, dst_ref=landing, send_sem=send_sems.at[t], recv_sem=recv_sems.at[t],
                                         device_id=(x, y, c), device_id_type=MESH).wait_recv()
        for cp in copies:
            cp.wait_send()

    return pl.pallas_call(
        body, name="grad_pair_assemble", out_shape=[_sds(a.shape, F32) for a in gs], in_specs=[HBM] * nt,
        out_specs=[HBM] * nt, input_output_aliases={t: t for t in range(nt)},
        scratch_shapes=[pltpu.SemaphoreType.DMA((nt,)), pltpu.SemaphoreType.DMA((nt,))],
    )(*gs)


def _pack(arrays, width):
    flat = jnp.concatenate([a.reshape(-1) for a in arrays])
    pad = (-flat.size) % (8 * width)
    return jnp.pad(flat, (0, pad)).reshape(-1, width)


def _unpack(packed, shapes):
    flat = packed.reshape(-1)
    out, off = [], 0
    for shp in shapes:
        n = 1
        for dim in shp:
            n *= dim
        out.append(flat[off:off + n].reshape(shp))
        off += n
    return out


def kernel(x, c, ada_w, ada_b, norm_mix_g, norm_ffn_g, ab_w_in, a_conv_w, a_conv_b, a_norm_g, a_norm_b, b_norm_g, b_norm_b, b_w_s, b_bias, ab_w_out, pool_w, pool_scale, ffn_w1, ffn_w3, ffn_w2, final_g, loss_target, m_ada_w, m_ada_b, m_norm_mix_g, m_norm_ffn_g, m_ab_w_in, m_a_conv_w, m_a_conv_b, m_a_norm_g, m_a_norm_b, m_b_norm_g, m_b_norm_b, m_b_w_s, m_b_bias, m_ab_w_out, m_pool_w, m_pool_scale, m_ffn_w1, m_ffn_w3, m_ffn_w2, m_final_g, v_ada_w, v_ada_b, v_norm_mix_g, v_norm_ffn_g, v_ab_w_in, v_a_conv_w, v_a_conv_b, v_a_norm_g, v_a_norm_b, v_b_norm_g, v_b_norm_b, v_b_w_s, v_b_bias, v_ab_w_out, v_pool_w, v_pool_scale, v_ffn_w1, v_ffn_w3, v_ffn_w2, v_final_g):
    mx, my, mc = _mesh_pos()
    chip = 2 * mx + my
    dev = 4 * mx + 2 * my + mc
    x2 = x[0]
    target = loss_target[0]
    s, d = x2.shape
    depth = ada_w.shape[0]
    n_mod = ada_b.shape[1] // d
    n_even = ab_w_in.shape[0]
    da, db = a_conv_b.shape[1], b_norm_g.shape[1]
    nh = b_w_s.shape[1]
    kw = a_conv_w.shape[1]
    n_pool = pool_w.shape[1]
    tm_row = _pick(s, (256, 128))

    s_vec = jnp.reshape(chip, (1,)).astype(jnp.int32)
    c_vec = jnp.reshape(mc, (1,)).astype(jnp.int32)
    sc_vec = [s_vec, c_vec] + [jnp.reshape(v, (1,)).astype(jnp.int32)
                               for v in (2 * mx + (1 - my), 2 * (1 - mx) + my, 2 * (1 - mx) + (1 - my))]
    pool_w3 = pool_w.reshape((-1,) + pool_w.shape[2:])
    shard_of = {"w_in": (ab_w_in, "col"), "w_out": (ab_w_out, "row"), "pool": (pool_w3, "row"), "w1": (ffn_w1, "col"),
                "w3": (ffn_w3, "col"), "w2": (ffn_w2, "row")}

    def layer_names(l):
        return (["w_in", "w_out"] if l % 2 == 0 else ["pool"]) + ["w1", "w3", "w2"]

    def layer_span(nm, l):
        if nm in ("w_in", "w_out"):
            return l // 2, 1
        if nm == "pool":
            return (l // 2) * n_pool, n_pool
        return l, 1

    group_names = {"0a": ["w_in"], "0o": ["w_out"], "0b": ["w1", "w3"], "0c": ["w2"]}
    group_layer = {key: 0 for key in group_names}
    for l in range(1, depth):
        group_names[l], group_layer[l] = layer_names(l), l
    owned, w_kinds, w_shapes = {}, {}, {}
    for key, names in group_names.items():
        l = group_layer[key]
        owned[key] = [_cast_into_full(shard_of[nm][0], shard_of[nm][1], s_vec, *layer_span(nm, l)) for nm in names]
        w_kinds[key] = [shard_of[nm][1] for nm in names]
        w_shapes[key] = [(layer_span(nm, l)[1],) + shard_of[nm][0].shape[1:] for nm in names]
    layer_w = [{} for _ in range(depth)]
    layer_w[0].update(zip(group_names["0a"], _allgather_weights(owned["0a"], w_kinds["0a"], w_shapes["0a"])))
    gathers = {"prev": layer_w[0][group_names["0a"][-1]], "token": None, "flying": {}}

    def gather_start(key):
        sends, recvs, fulls, token = _gather_start(f"gather_start_{key}", owned[key], w_kinds[key], w_shapes[key], gathers["prev"])
        gathers["flying"][key] = (sends, recvs, fulls)
        gathers["prev"] = gathers["token"] = token

    def gather_wait(key, after):
        sends, recvs, fulls = gathers["flying"].pop(key)
        landed = _gather_wait(f"gather_wait_{key}", sends, recvs, fulls, w_kinds[key], w_shapes[key], after)
        layer_w[group_layer[key]].update(zip(group_names[key], landed))
        gathers["prev"] = landed[-1]

    def after_starts(row):
        return row + gathers["token"][0:1, 0:1]

    for key in ("0o", "0b", "0c"):
        gather_start(key)

    pre = _allgather_small(_pack([c, a_conv_w, pool_scale], 128)).reshape(N_DEV, -1)
    n_cw, n_ps = a_conv_w.size, pool_scale.size
    c_all = pre[:, :d]
    cw_chips = pre[0::2, d:d + n_cw].reshape((N_CHIP,) + a_conv_w.shape)
    conv_w_full = jnp.concatenate([cw_chips[k] for k in range(N_CHIP)], axis=-1)
    ps_chips = pre[0::2, d + n_cw:d + n_cw + n_ps].reshape((N_CHIP,) + pool_scale.shape)
    pool_scale_full = jnp.concatenate([ps_chips[k] for k in range(N_CHIP)], axis=-1)
    c_pad = jnp.pad(c_all, ((0, 8), (0, 0)))
    n_ada = ada_w.shape[2]
    ada_b_mine = lax.dynamic_slice_in_dim(ada_b, chip * n_ada, n_ada, axis=1)[:, None, :]
    mod_part = _ada_fwd(c_pad, ada_w, ada_b_mine)[:, :N_DEV, :]
    mod_all = _allgather_small(mod_part.reshape(depth * N_DEV, n_ada))
    mod_chips = mod_all[0::2].reshape(N_CHIP, depth, N_DEV, n_ada)
    mod_mine = lax.dynamic_index_in_dim(mod_chips, dev, axis=2, keepdims=False)
    mod = jnp.transpose(mod_mine, (1, 0, 2)).reshape(depth, n_mod, 1, d)

    causal = jnp.tril(jnp.ones((CHUNK, CHUNK), dtype=bool))
    w_c = jnp.where(causal[None, None], b_w_s, 0.0).astype(BF16)
    bias_full = jnp.repeat(jnp.swapaxes(b_bias, 1, 2), CHUNK, axis=2)

    saved = []
    xs = x2
    for l in range(depth):
        sh1, sc1, g1, sh2, sc2, g2 = [mod[l, k] for k in range(n_mod)]
        i = l // 2
        st = {"x1": xs}
        if 1 <= l and l + 2 < depth:
            gather_start(l + 2)
        gain1 = after_starts(norm_mix_g[l][None])
        wl = layer_w[l]
        if l % 2 == 0:
            h = _norm_mod(xs, gain1, sh1, sc1, tm_row)
            proj = _mm("ab_proj", [h], [(wl["w_in"], 0)], [BF16], _ep_store, tk=2048)[0]
            cat, a1 = _ab_mid_fwd(proj, conv_w_full[i], a_conv_b[i][None], a_norm_g[i][None], a_norm_b[i][None],
                                  b_norm_g[i][None], b_norm_b[i][None], w_c[i], bias_full[i], tm_row)
            if l == 0:
                gather_wait("0o", cat)
            xs, y1 = _mm("ab_out", [cat], [(wl["w_out"], 0)], [F32, BF16], _ep_residual, extras=[xs, g1], extra_kinds=["tile", "row"], tk=2048)
            st.update(h=h, proj=proj, a1=a1, cat=cat, y=y1)
        else:
            p = _pool_fwd(xs, gain1, sh1, sc1, tm_row)
            gate = g1 * pool_scale_full[i][None]
            xs, ymm = _grouped_fwd(p, wl["pool"][None], 0, xs, gate, 1024)
            st.update(p=p, y=ymm, gate=gate)
        st["x2"] = xs
        gain2 = norm_ffn_g[l][None]
        if l == 0:
            gather_wait("0b", xs)
            for nxt in range(1, min(3, depth)):
                gather_start(nxt)
            gain2 = after_starts(gain2)
        h2 = _norm_mod(xs, gain2, sh2, sc2, tm_row)
        u, t, z = _mm("ffn_up", [h2, h2], [(wl["w1"], 0), (wl["w3"], 0)], [BF16, BF16, BF16], _ep_swiglu, tn=512, tk=2048)
        if l == 0:
            gather_wait("0c", z)
        xs, y2 = _mm("ffn_down", [z], [(wl["w2"], 0)], [F32, BF16], _ep_residual, extras=[xs, g2], extra_kinds=["tile", "row"],
                     tn=512, tk=ffn_w2.shape[1] * N_CHIP)
        if l + 1 < depth:
            gather_wait(l + 1, xs)
        st.update(h2=h2, u=u, t=t, z=z, y2=y2)
        saved.append(st)

    def below_of(l, which):
        if which == "ffn":
            return saved[l]["y2"], mod[l, n_mod - 1]
        return saved[l]["y"], (mod[l, 2] if l % 2 == 0 else saved[l]["gate"])

    dx, fin_acc, loss_blk, dyb, gacc = _final_loss_bwd(xs, final_g[None], target, tm_row, below_of(depth - 1, "ffn"))
    loss = lax.psum(loss_blk[0, 0], MESH_AXES)
    d_final_g = fin_acc[0]
    dmod_rows = [None] * depth
    d_norm_mix, d_norm_ffn = [None] * depth, [None] * depth
    even_small = [None] * n_even
    d_pool_scale = [None] * (depth // 2)
    grad_names = ["w_in", "w_out", "pool", "w1", "w3", "w2"]
    g_shapes = {"w_in": ab_w_in.shape, "w_out": ab_w_out.shape, "pool": (pool_w.shape[0], n_pool * pool_w.shape[2], pool_w.shape[3]),
                "w1": ffn_w1.shape, "w3": ffn_w3.shape, "w2": ffn_w2.shape}
    shard_grads = {nm: None for nm in grad_names}
    pipe = {"pair": None, "chip": None, "prev": None, "token": None}

    def finish_chip(after):
        tag, pl_, names_, kinds_, sends, recvs, cps_f, slabs_f = pipe["chip"]
        cps_d, slabs_d = _reduce_wait(f"reduce_wait_{tag}", sends, recvs, cps_f, slabs_f, kinds_, after)
        for nm, kind, cp, sl in zip(names_, kinds_, cps_d, slabs_d):
            rpg = pool_w.shape[2] if nm == "pool" else None
            cp2 = cp.reshape(-1, cp.shape[-1])
            shard_grads[nm] = _chip_sum_into(sl.reshape(N_CHIP, -1, sl.shape[-1]), cp2, shard_grads[nm], g_shapes[nm],
                                             layer_span(nm, pl_)[0] // (n_pool if nm == "pool" else 1), kind, sc_vec, rpg)
        pipe["chip"] = None
        pipe["prev"] = slabs_d[-1]

    def settle(after):
        made = []
        if pipe["pair"] is not None:
            tag, lyr, names, kinds_l, sends, recvs, dws_f, sibs_f = pipe["pair"]
            dws_d, sibs_d = _pair_wait(f"pair_wait_{tag}", sends, recvs, dws_f, sibs_f, kinds_l, after)
            cps = []
            for dw, sib, kind in zip(dws_d, sibs_d, kinds_l):
                cp = _pair_add(dw.reshape(-1, dw.shape[-1]), sib.reshape(-1, sib.shape[-1]), kind, c_vec)
                cps.append(cp.reshape(sib.shape))
            pipe["pair"] = None
            pipe["ready"] = (tag, lyr, names, kinds_l, cps)
            made.append(cps[-1])
        if pipe["chip"] is not None:
            finish_chip(after)
            made.append(pipe["prev"])
        return made

    def advance(after, new=None):
        settle(after)
        if pipe.get("ready") is not None:
            tag, lyr, names, kinds_l, cps = pipe.pop("ready")
            prev = cps[-1] if pipe["prev"] is None else pipe["prev"]
            sends, recvs, cps_f, slabs_f, token = _reduce_start(f"reduce_start_{tag}", cps, kinds_l, prev)
            pipe["chip"] = (tag, lyr, names, kinds_l, sends, recvs, cps_f, slabs_f)
            pipe["token"] = token
        if new is not None:
            tag, lyr, names, big_ = new
            kinds_l = [shard_of[nm][1] for nm in names]
            dws = [big_[nm] for nm in names]
            prev = dws[-1] if pipe["token"] is None else pipe["token"]
            sends, recvs, dws_f, sibs_f, token = _pair_start(f"pair_start_{tag}", dws, kinds_l, prev)
            pipe["pair"] = (tag, lyr, names, kinds_l, sends, recvs, dws_f, sibs_f)
            pipe["token"] = token

    def behind(row):
        return row + pipe["token"][0:1, 0:1]

    for l in reversed(range(depth)):
        sh1, sc1, g1, sh2, sc2, g2 = [mod[l, k] for k in range(n_mod)]
        st = saved[l]
        wl = layer_w[l]
        i = l // 2
        big = {}
        d_g2 = gacc[0]
        du, dt = _mm("ffn_dz", [dyb], [(wl["w2"], 0)], [BF16, BF16], _ep_swiglu_bwd, trans_b=True, extras=[st["u"], st["t"]],
                     extra_kinds=["tile", "tile"], tm=2048, tn=512, tk=2048)
        big["w2"] = _mm("ffn_dw2", [st["z"]], [dyb], [BF16], _ep_store, trans_a=True, tm=512, tn=512, tk=s)[0]
        big["w1"] = _mm("ffn_dw1", [st["h2"]], [du], [BF16], _ep_store, trans_a=True, tm=512, tn=512, tk=s)[0]
        big["w3"] = _mm("ffn_dw3", [st["h2"]], [dt], [BF16], _ep_store, trans_a=True, tm=512, tn=512, tk=s)[0]
        advance(big["w3"], (f"ffn{l}", l, ["w1", "w3", "w2"], big))
        dh = _mm("ffn_dh", [du, dt], [(wl["w1"], 0), (wl["w3"], 0)], [F32], _ep_sum, trans_b=True, tn=512, tk=ffn_w2.shape[1] * 2)[0]
        dx, nacc, dyb, gacc = _norm_mod_bwd(dh, st["x2"], dx, behind(norm_ffn_g[l][None]), sc2, tm_row, below_of(l, "mix"))
        d_sh2, d_sc2, d_norm_ffn[l] = nacc[0], nacc[1], nacc[2]
        if l % 2 == 0:
            d_g1 = gacc[0]
            big["w_out"] = _mm("ab_dw_out", [st["cat"]], [dyb], [BF16], _ep_store, trans_a=True, tm=512, tn=512, tk=s)[0]
            dcat = _mm("ab_dcat", [dyb], [(wl["w_out"], 0)], [F32], _ep_store, trans_b=True, tk=2048)[0]
            dproj, dcw, vecs, dws, dbias = _ab_mid_bwd(dcat, st["proj"], st["a1"], conv_w_full[i], a_norm_g[i][None],
                                                       a_norm_b[i][None], b_norm_g[i][None], b_norm_b[i][None], w_c[i],
                                                       bias_full[i], tm_row)
            even_small[i] = dict(conv_w=dcw, a_norm_g=vecs[0, :da], a_norm_b=vecs[1, :da], conv_b=vecs[2, :da],
                                 b_norm_g=vecs[3, :db], b_norm_b=vecs[4, :db], w_s=dws, bias=dbias[:, :, 0])
            big["w_in"] = _mm("ab_dw_in", [st["h"]], [dproj], [BF16], _ep_store, trans_a=True, tm=512, tn=512, tk=s)[0]
            advance(big["w_in"], (f"mix{l}", l, ["w_in", "w_out"], big))
            dh = _mm("ab_dh", [dproj], [(wl["w_in"], 0)], [F32], _ep_store, trans_b=True, tn=512, tk=2 * da + 2 * db)[0]
            below = below_of(l - 1, "ffn") if l > 0 else None
            outs = _norm_mod_bwd(dh, st["x1"], dx, behind(norm_mix_g[l][None]), sc1, tm_row, below)
            dx, nacc = outs[0], outs[1]
            if below is not None:
                dyb, gacc = outs[2], outs[3]
        else:
            d_g1 = gacc[0] * pool_scale_full[i]
            d_pool_scale[i] = gacc[0] * g1[0]
            big["pool"] = _grouped_dw(st["p"], dyb, n_pool, 1024)
            advance(big["pool"], (f"mix{l}", l, ["pool"], big))
            dp = _grouped_dx(dyb, wl["pool"][None], 0, 1024)
            dx, nacc, dyb, gacc = _pool_bwd(dp, st["x1"], dx, behind(norm_mix_g[l][None]), sc1, tm_row, below_of(l - 1, "ffn"))
        d_sh1, d_sc1, d_norm_mix[l] = nacc[0], nacc[1], nacc[2]
        dmod_rows[l] = jnp.concatenate([d_sh1, d_sc1, d_g1, d_sh2, d_sc2, d_g2])
    grad_x = dx[None]
    settled = settle(dx)

    dmod = jnp.stack(dmod_rows)
    small = [dmod, jnp.stack(d_norm_mix), jnp.stack(d_norm_ffn),
             jnp.stack([e["conv_w"] for e in even_small]), jnp.stack([e["conv_b"] for e in even_small]),
             jnp.stack([e["a_norm_g"] for e in even_small]), jnp.stack([e["a_norm_b"] for e in even_small]),
             jnp.stack([e["b_norm_g"] for e in even_small]), jnp.stack([e["b_norm_b"] for e in even_small]),
             jnp.stack([e["w_s"] for e in even_small]), jnp.stack([e["bias"] for e in even_small]),
             jnp.stack(d_pool_scale), d_final_g]
    small_shapes = [a.shape for a in small]
    width = 1024 if d >= 1024 else 128
    gathered = _allgather_small(_pack(small, width), after=settled)
    summed = _unpack(_sum_leading(gathered), small_shapes)
    (g_ada_b, g_norm_mix, g_norm_ffn, g_conv_w_full, g_conv_b, g_a_norm_g, g_a_norm_b, g_b_norm_g, g_b_norm_b, g_w_s, g_bias,
     g_pool_scale_full, g_final_g) = summed
    cw_cols = a_conv_w.shape[2]
    g_conv_w = lax.dynamic_slice_in_dim(g_conv_w_full, chip * cw_cols, cw_cols, axis=2)
    ps_cols = pool_scale.shape[1]
    g_pool_scale = lax.dynamic_slice_in_dim(g_pool_scale_full, chip * ps_cols, ps_cols, axis=1)

    dmod_all = gathered.reshape(N_DEV, -1)[:, :dmod.size].reshape(N_DEV, depth, n_mod * d)
    dmod_cols = lax.dynamic_slice_in_dim(dmod_all, chip * n_ada, n_ada, axis=2)
    dmod_cols = jnp.pad(jnp.transpose(dmod_cols, (1, 0, 2)), ((0, 0), (0, 8), (0, 0)))
    g_ada_w = _ada_bwd(c_pad, dmod_cols)

    weights = [ada_w, ada_b, norm_mix_g, norm_ffn_g, ab_w_in, a_conv_w, a_conv_b, a_norm_g, a_norm_b, b_norm_g, b_norm_b, b_w_s,
               b_bias, ab_w_out, pool_w, pool_scale, ffn_w1, ffn_w3, ffn_w2, final_g]
    ms = [m_ada_w, m_ada_b, m_norm_mix_g, m_norm_ffn_g, m_ab_w_in, m_a_conv_w, m_a_conv_b, m_a_norm_g, m_a_norm_b, m_b_norm_g,
          m_b_norm_b, m_b_w_s, m_b_bias, m_ab_w_out, m_pool_w, m_pool_scale, m_ffn_w1, m_ffn_w3, m_ffn_w2, m_final_g]
    vs = [v_ada_w, v_ada_b, v_norm_mix_g, v_norm_ffn_g, v_ab_w_in, v_a_conv_w, v_a_conv_b, v_a_norm_g, v_a_norm_b, v_b_norm_g,
          v_b_norm_b, v_b_w_s, v_b_bias, v_ab_w_out, v_pool_w, v_pool_scale, v_ffn_w1, v_ffn_w3, v_ffn_w2, v_final_g]
    grads = [g_ada_w, g_ada_b, g_norm_mix, g_norm_ffn, None, g_conv_w, g_conv_b, g_a_norm_g, g_a_norm_b, g_b_norm_g,
             g_b_norm_b, g_w_s, g_bias, None, None, g_pool_scale, None, None, None, g_final_g]
    updates = [None] * len(weights)

    def update(k):
        grads[k] = grads[k].reshape(weights[k].shape)
        updates[k] = _adamw(weights[k], grads[k], ms[k], vs[k])

    advance(g_ada_w)
    for k in range(1, len(weights)):
        if grads[k] is not None:
            update(k)
    update(0)
    advance(updates[0][0])
    big_at = {"w_in": 4, "w_out": 13, "pool": 14, "w1": 16, "w3": 17, "w2": 18}
    assembled = _pair_assemble([shard_grads[nm] for nm in grad_names], [shard_of[nm][1] for nm in grad_names])
    for nm, g in zip(grad_names, assembled):
        grads[big_at[nm]] = g
        update(big_at[nm])
    return (loss, grad_x, *grads, *[u[0] for u in updates], *[u[1] for u in updates], *[u[2] for u in updates])
```

```python
import functools

import jax
import jax.numpy as jnp
from jax import lax
from jax.experimental import pallas as pl
from jax.experimental.pallas import tpu as pltpu

F32 = jnp.float32
BF16 = jnp.bfloat16
EPS = 1e-6
N_DEV = 8
N_CHIP = 4
MESH_AXES = ("x", "y", "c")
MESH = pl.DeviceIdType.MESH
V7X_VMEM_LIMIT_BYTES = 56 * 1024 * 1024
SUBLANES = 8
CONV_HALO = 32
POOL_HALO = 16
POOL_WINDOWS = (2, 4, 8, 16)
CHUNK = 128
ADAM_LR, ADAM_B1, ADAM_B2, ADAM_EPS, ADAM_WD, ADAM_STEP = 0.001, 0.9, 0.999, 1e-08, 0.01, 10
HBM = pl.BlockSpec(memory_space=pltpu.HBM)


def _params(sem=None):
    return pltpu.CompilerParams(dimension_semantics=sem, vmem_limit_bytes=V7X_VMEM_LIMIT_BYTES)


def _pick(n, prefs):
    for p in prefs:
        if p <= n and n % p == 0:
            return p
    return n


def _row_tile(rows, cols, target):
    if rows * cols <= target:
        return rows
    best = None
    for d in range(16, rows, 16):
        if rows % d == 0 and d * cols <= target:
            best = d
    return best if best is not None else rows


def _sds(shape, dtype):
    return jax.ShapeDtypeStruct(tuple(shape), dtype)


def _mm_core(name, grid, nk, a_list, b_list, a_spec, b_spec, dn, extras, extra_specs, out_shapes, out_specs,
             acc_shape, epilogue):
    n, n_ex, n_out = len(a_list), len(extras), len(out_shapes)

    def body(*refs):
        a_refs, b_refs = refs[:n], refs[n:2 * n]
        ex = refs[2 * n:2 * n + n_ex]
        outs = refs[2 * n + n_ex:2 * n + n_ex + n_out]
        accs = refs[2 * n + n_ex + n_out:]
        ps = [lax.dot_general(a[...], b[...], dn, preferred_element_type=F32) for a, b in zip(a_refs, b_refs)]
        if nk == 1:
            epilogue(ps, ex, outs)
            return
        k = pl.program_id(2)

        @pl.when(k == 0)
        def _():
            for acc, p in zip(accs, ps):
                acc[...] = p

        @pl.when(k > 0)
        def _():
            for acc, p in zip(accs, ps):
                acc[...] += p

        @pl.when(k == nk - 1)
        def _():
            epilogue([acc[...] for acc in accs], ex, outs)

    scratch = [] if nk == 1 else [pltpu.VMEM(acc_shape, F32) for _ in range(n)]
    return pl.pallas_call(
        body, grid=grid, name=name,
        in_specs=[a_spec] * n + [b_spec] * n + list(extra_specs),
        out_specs=list(out_specs), out_shape=list(out_shapes), scratch_shapes=scratch,
        compiler_params=_params(("parallel", "parallel", "arbitrary")),
    )(*a_list, *b_list, *extras)


def _mm(name, a_list, b_list, out_dtypes, epilogue, *, trans_a=False, trans_b=False, extras=(), extra_kinds=(),
        tm=1024, tn=1024, tk=1024):
    layer = None
    if isinstance(b_list[0], tuple):
        layer = b_list[0][1]
        b_list = [b for b, _ in b_list]
    a0 = a_list[0]
    b_shape = b_list[0].shape[-2:]
    m, kk = (a0.shape[1], a0.shape[0]) if trans_a else a0.shape
    nn = b_shape[0] if trans_b else b_shape[1]
    tm, tn, tk = _pick(m, (tm, 512, 256, 128)), _pick(nn, (tn, 512, 256, 128)), _pick(kk, (tk, 512, 256, 128))
    nk = kk // tk
    a_spec = pl.BlockSpec((tk, tm), lambda i, j, k: (k, i)) if trans_a else pl.BlockSpec((tm, tk), lambda i, j, k: (i, k))
    if layer is None:
        b_spec = pl.BlockSpec((tn, tk), lambda i, j, k: (j, k)) if trans_b else pl.BlockSpec((tk, tn), lambda i, j, k: (k, j))
    elif trans_b:
        b_spec = pl.BlockSpec((None, tn, tk), lambda i, j, k: (layer, j, k))
    else:
        b_spec = pl.BlockSpec((None, tk, tn), lambda i, j, k: (layer, k, j))
    dn = (((0 if trans_a else 1,), (1 if trans_b else 0,)), ((), ()))
    tile = pl.BlockSpec((tm, tn), lambda i, j, k: (i, j))
    row = pl.BlockSpec((1, tn), lambda i, j, k: (0, j))
    return _mm_core(name, (m // tm, nn // tn, nk), nk, a_list, b_list, a_spec, b_spec, dn, extras,
                    [tile if kd == "tile" else row for kd in extra_kinds],
                    [_sds((m, nn), dt) for dt in out_dtypes], [tile] * len(out_dtypes), (tm, tn), epilogue)


def _ep_store(ps, ex, outs):
    outs[0][...] = ps[0].astype(outs[0].dtype)


def _ep_sum(ps, ex, outs):
    outs[0][...] = (ps[0] + ps[1]).astype(outs[0].dtype)


def _ep_swiglu(ps, ex, outs):
    u, t = ps
    sg = jax.nn.sigmoid(u)
    su = u * sg
    outs[0][...] = (t * (sg * (1.0 + u * (1.0 - sg)))).astype(BF16)
    outs[1][...] = su.astype(BF16)
    outs[2][...] = (su * t).astype(BF16)


def _ep_residual(ps, ex, outs):
    x_ref, gate_ref = ex
    y = ps[0]
    outs[0][...] = x_ref[...] + gate_ref[...] * y
    outs[1][...] = y.astype(BF16)


def _ep_swiglu_bwd(ps, ex, outs):
    dz = ps[0]
    outs[0][...] = (dz * ex[0][...].astype(F32)).astype(BF16)
    outs[1][...] = (dz * ex[1][...].astype(F32)).astype(BF16)


def _grouped_fwd(p, w, layer, x, gate, tm):
    s, d = p.shape
    _, g, kg, ng = w.shape
    tm = _pick(s, (tm, 512, 256, 128))
    tile_a = pl.BlockSpec((tm, kg), lambda i, j, k: (i, j))
    tile_o = pl.BlockSpec((tm, ng), lambda i, j, k: (i, j))
    return _mm_core("pool_mm_fwd", (s // tm, g, 1), 1, [p], [w], tile_a,
                    pl.BlockSpec((None, None, kg, ng), lambda i, j, k: (layer, j, 0, 0)), (((1,), (0,)), ((), ())),
                    [x, gate], [tile_o, pl.BlockSpec((1, ng), lambda i, j, k: (0, j))],
                    [_sds((s, g * ng), F32), _sds((s, g * ng), BF16)], [tile_o, tile_o], None, _ep_residual)


def _grouped_dx(dy, w, layer, tm):
    s, _ = dy.shape
    _, g, kg, ng = w.shape
    tm = _pick(s, (tm, 512, 256, 128))
    return _mm_core("pool_mm_dx", (s // tm, g, 1), 1, [dy], [w], pl.BlockSpec((tm, ng), lambda i, j, k: (i, j)),
                    pl.BlockSpec((None, None, kg, ng), lambda i, j, k: (layer, j, 0, 0)), (((1,), (1,)), ((), ())), [], [],
                    [_sds((s, g * kg), BF16)], [pl.BlockSpec((tm, kg), lambda i, j, k: (i, j))], None, _ep_store)[0]


def _grouped_dw(p, dy, groups, tk):
    s, d = p.shape
    kg = d // groups
    ng = dy.shape[1] // groups
    tk = _pick(s, (tk, 512, 256, 128))
    nk = s // tk

    def ep(ps, ex, outs):
        outs[0][...] = ps[0].astype(BF16)

    return _mm_core("pool_mm_dw", (groups, 1, nk), nk, [p], [dy], pl.BlockSpec((tk, kg), lambda i, j, k: (k, i)),
                    pl.BlockSpec((tk, ng), lambda i, j, k: (k, i)), (((0,), (0,)), ((), ())), [], [],
                    [_sds((groups, kg, ng), BF16)], [pl.BlockSpec((None, kg, ng), lambda i, j, k: (i, 0, 0))],
                    (kg, ng), ep)[0]


def _rms_rstd(xv):
    return lax.rsqrt(jnp.mean(xv * xv, axis=-1, keepdims=True) + EPS)


def _norm_mod_math(xv, g, sh, sc):
    return ((xv * _rms_rstd(xv)) * g) * (1.0 + sc) + sh


def _norm_mod_bwd_math(dh, xv, dxo, g, sc, acc_ref):
    r = _rms_rstd(xv)
    xhat = xv * r
    acc_ref[0:1, :] += jnp.sum(dh, axis=0, keepdims=True)
    acc_ref[1:2, :] += jnp.sum(dh * (xhat * g), axis=0, keepdims=True)
    dhn = dh * (1.0 + sc)
    acc_ref[2:3, :] += jnp.sum(dhn * xhat, axis=0, keepdims=True)
    dxh = dhn * g
    return dxo + r * (dxh - xhat * jnp.mean(dxh * xhat, axis=-1, keepdims=True))


def _vec_spec(d):
    return pl.BlockSpec((1, d), lambda i: (0, 0))


def _acc_spec(d):
    return pl.BlockSpec((8, d), lambda i: (0, 0))


def _norm_mod(x, g, sh, sc, tm):
    s, d = x.shape

    def body(x_ref, g_ref, sh_ref, sc_ref, h_ref):
        h_ref[...] = _norm_mod_math(x_ref[...], g_ref[...], sh_ref[...], sc_ref[...]).astype(BF16)

    row = pl.BlockSpec((tm, d), lambda i: (i, 0))
    return pl.pallas_call(body, grid=(s // tm,), name="norm_mod", in_specs=[row] + [_vec_spec(d)] * 3, out_specs=row,
                          out_shape=_sds((s, d), BF16), compiler_params=_params(("parallel",)))(x, g, sh, sc)


def _gate_step(dxv, y_ref, gate_ref, dy_ref, gacc_ref):
    dy_ref[...] = (dxv * gate_ref[...]).astype(BF16)
    gacc_ref[0:1, :] += jnp.sum(dxv * y_ref[...].astype(F32), axis=0, keepdims=True)


def _norm_mod_bwd(dh, x, dxo, g, sc, tm, below=None):
    s, d = x.shape

    def body(dh_ref, x_ref, dxo_ref, g_ref, sc_ref, *rest):
        if below is None:
            dx_ref, acc_ref = rest
        else:
            y_ref, gate_ref, dx_ref, acc_ref, dy_ref, gacc_ref = rest

        @pl.when(pl.program_id(0) == 0)
        def _():
            acc_ref[...] = jnp.zeros_like(acc_ref)
            if below is not None:
                gacc_ref[...] = jnp.zeros_like(gacc_ref)

        dxv = _norm_mod_bwd_math(dh_ref[...].astype(F32), x_ref[...], dxo_ref[...], g_ref[...], sc_ref[...], acc_ref)
        dx_ref[...] = dxv
        if below is not None:
            _gate_step(dxv, y_ref, gate_ref, dy_ref, gacc_ref)

    row = pl.BlockSpec((tm, d), lambda i: (i, 0))
    extra_in = [] if below is None else [row, _vec_spec(d)]
    extra_out = [] if below is None else [row, _acc_spec(d)]
    extra_shape = [] if below is None else [_sds((s, d), BF16), _sds((8, d), F32)]
    return pl.pallas_call(body, grid=(s // tm,), name="norm_mod_bwd", in_specs=[row, row, row, _vec_spec(d), _vec_spec(d)] + extra_in,
                          out_specs=[row, _acc_spec(d)] + extra_out, out_shape=[_sds((s, d), F32), _sds((8, d), F32)] + extra_shape,
                          compiler_params=_params(("arbitrary",)))(dh, x, dxo, g, sc, *([] if below is None else below))


def _gate_bwd(dx, y, gate, tm):
    s, d = dx.shape

    def body(dx_ref, y_ref, gate_ref, dy_ref, acc_ref):
        @pl.when(pl.program_id(0) == 0)
        def _():
            acc_ref[...] = jnp.zeros_like(acc_ref)

        dxv = dx_ref[...]
        dy_ref[...] = (dxv * gate_ref[...]).astype(BF16)
        acc_ref[0:1, :] += jnp.sum(dxv * y_ref[...].astype(F32), axis=0, keepdims=True)

    row = pl.BlockSpec((tm, d), lambda i: (i, 0))
    return pl.pallas_call(body, grid=(s // tm,), name="gate_bwd", in_specs=[row, row, _vec_spec(d)],
                          out_specs=[row, _acc_spec(d)], out_shape=[_sds((s, d), BF16), _sds((8, d), F32)],
                          compiler_params=_params(("arbitrary",)))(dx, y, gate)


def _final_loss_bwd(x, g, target, tm, below):
    s, d = x.shape

    def body(x_ref, g_ref, t_ref, y_ref, gate_ref, dx_ref, acc_ref, loss_ref, dy_ref, gacc_ref):
        @pl.when(pl.program_id(0) == 0)
        def _():
            acc_ref[...] = jnp.zeros_like(acc_ref)
            loss_ref[...] = jnp.zeros_like(loss_ref)
            gacc_ref[...] = jnp.zeros_like(gacc_ref)

        xv = x_ref[...]
        gv = g_ref[...]
        r = _rms_rstd(xv)
        xhat = xv * r
        err = xhat * gv - t_ref[...]
        loss_ref[...] += (0.5 / d) * jnp.sum(err * err)
        dy = err * (1.0 / d)
        acc_ref[0:1, :] += jnp.sum(dy * xhat, axis=0, keepdims=True)
        dxh = dy * gv
        dxv = r * (dxh - xhat * jnp.mean(dxh * xhat, axis=-1, keepdims=True))
        dx_ref[...] = dxv
        _gate_step(dxv, y_ref, gate_ref, dy_ref, gacc_ref)

    row = pl.BlockSpec((tm, d), lambda i: (i, 0))
    return pl.pallas_call(body, grid=(s // tm,), name="final_loss_bwd", in_specs=[row, _vec_spec(d), row, row, _vec_spec(d)],
                          out_specs=[row, _acc_spec(d), pl.BlockSpec((8, 128), lambda i: (0, 0)), row, _acc_spec(d)],
                          out_shape=[_sds((s, d), F32), _sds((8, d), F32), _sds((8, 128), F32), _sds((s, d), BF16), _sds((8, d), F32)],
                          compiler_params=_params(("arbitrary",)))(x, g, target, *below)


def _chunks(tm, width, rb, cb):
    rb, cb = min(rb, tm), min(cb, width)
    return [(r0, c0, rb, cb) for r0 in range(0, tm, rb) for c0 in range(0, width, cb)]


def _prev_halo_map(tm, halo):
    return lambda i: (jnp.maximum(i * (tm // halo) - 1, 0), 0)


def _next_halo_map(tm, halo, s):
    return lambda i: (jnp.minimum((i + 1) * (tm // halo), s // halo - 1), 0)


def _pool_fwd(x, g, sh, sc, tm):
    s, d = x.shape
    dg = d // len(POOL_WINDOWS)

    def body(x_ref, xh_ref, g_ref, sh_ref, sc_ref, p_ref, ext_ref):
        i = pl.program_id(0)
        gv, shv, scv = g_ref[...], sh_ref[...], sc_ref[...]
        ext_ref[POOL_HALO:, :] = _norm_mod_math(x_ref[...], gv, shv, scv)
        ext_ref[0:POOL_HALO, :] = jnp.where(i == 0, 0.0, _norm_mod_math(xh_ref[...], gv, shv, scv))
        for gi, w in enumerate(POOL_WINDOWS):
            for r0, c0, rb, cb in _chunks(tm, dg, 64, 256):
                cols = pl.ds(gi * dg + c0, cb)
                tok = ext_ref[pl.ds(POOL_HALO + r0, rb), cols]
                acc = tok
                for j in range(1, w):
                    acc = acc + ext_ref[pl.ds(POOL_HALO + r0 - j, rb), cols]
                t_glob = i * tm + r0 + lax.broadcasted_iota(jnp.int32, (rb, 1), 0)
                cnt = jnp.minimum(t_glob + 1, w).astype(F32)
                p_ref[pl.ds(r0, rb), cols] = (acc / cnt - tok).astype(BF16)

    row = pl.BlockSpec((tm, d), lambda i: (i, 0))
    halo = pl.BlockSpec((POOL_HALO, d), _prev_halo_map(tm, POOL_HALO))
    return pl.pallas_call(body, grid=(s // tm,), name="pool_fwd", in_specs=[row, halo] + [_vec_spec(d)] * 3, out_specs=row,
                          out_shape=_sds((s, d), BF16), scratch_shapes=[pltpu.VMEM((tm + POOL_HALO, d), F32)],
                          compiler_params=_params(("parallel",)))(x, x, g, sh, sc)


def _pool_bwd(dp, x, dxo, g, sc, tm, below):
    s, d = x.shape
    dg = d // len(POOL_WINDOWS)
    n_tiles = s // tm

    def body(dp_ref, dph_ref, x_ref, dxo_ref, g_ref, sc_ref, y_ref, gate_ref, dx_ref, acc_ref, dy_ref, gacc_ref, ext_ref, dh_ref):
        i = pl.program_id(0)

        @pl.when(i == 0)
        def _():
            acc_ref[...] = jnp.zeros_like(acc_ref)
            gacc_ref[...] = jnp.zeros_like(gacc_ref)

        for gi, w in enumerate(POOL_WINDOWS):
            cols = pl.ds(gi * dg, dg)
            t_main = i * tm + lax.broadcasted_iota(jnp.int32, (tm, 1), 0)
            ext_ref[0:tm, cols] = dp_ref[:, cols].astype(F32) / jnp.minimum(t_main + 1, w).astype(F32)
            ext_ref[tm:, cols] = jnp.where(i == n_tiles - 1, 0.0, dph_ref[:, cols].astype(F32) * (1.0 / w))
            for r0, c0, rb, cb in _chunks(tm, dg, 64, 256):
                cc = pl.ds(gi * dg + c0, cb)
                acc = ext_ref[pl.ds(r0, rb), cc]
                for j in range(1, w):
                    acc = acc + ext_ref[pl.ds(r0 + j, rb), cc]
                dh_ref[pl.ds(r0, rb), cc] = acc - dp_ref[pl.ds(r0, rb), cc].astype(F32)
        dxv = _norm_mod_bwd_math(dh_ref[...], x_ref[...], dxo_ref[...], g_ref[...], sc_ref[...], acc_ref)
        dx_ref[...] = dxv
        _gate_step(dxv, y_ref, gate_ref, dy_ref, gacc_ref)

    row = pl.BlockSpec((tm, d), lambda i: (i, 0))
    halo = pl.BlockSpec((POOL_HALO, d), _next_halo_map(tm, POOL_HALO, s))
    return pl.pallas_call(body, grid=(n_tiles,), name="pool_bwd",
                          in_specs=[row, halo, row, row, _vec_spec(d), _vec_spec(d), row, _vec_spec(d)],
                          out_specs=[row, _acc_spec(d), row, _acc_spec(d)],
                          out_shape=[_sds((s, d), F32), _sds((8, d), F32), _sds((s, d), BF16), _sds((8, d), F32)],
                          scratch_shapes=[pltpu.VMEM((tm + POOL_HALO, d), F32), pltpu.VMEM((tm, d), F32)],
                          compiler_params=_params(("arbitrary",)))(dp, dp, x, dxo, g, sc, *below)


def _layernorm_fwd(v, g, b):
    mu = jnp.mean(v, axis=-1, keepdims=True)
    xc = v - mu
    rstd = lax.rsqrt(jnp.mean(xc * xc, axis=-1, keepdims=True) + EPS)
    yn = xc * rstd
    return yn * g + b, yn, rstd


def _layernorm_bwd(dz, yn, rstd, g):
    dyn = dz * g
    dv = rstd * (dyn - jnp.mean(dyn, axis=-1, keepdims=True) - yn * jnp.mean(dyn * yn, axis=-1, keepdims=True))
    return dv, jnp.sum(dz * yn, axis=0, keepdims=True), jnp.sum(dz, axis=0, keepdims=True)


def _fill_shifted(ext_ref, sh_ref):
    n = ext_ref.shape[0]
    for r in range(1, SUBLANES):
        sh_ref[r - 1, 0:n - SUBLANES, :] = ext_ref[pl.ds(r, n - SUBLANES), :]


def _shifted(ext_ref, sh_ref, start, rows, cols):
    q, r = divmod(start, SUBLANES)
    if r == 0:
        return ext_ref[pl.ds(start, rows), cols]
    return sh_ref[r - 1, pl.ds(q * SUBLANES, rows), cols]


def _ab_mid_fwd(proj, conv_w, conv_b, a_g, a_b, v_g, v_b, w_c, bias_full, tm):
    s = proj.shape[0]
    da = conv_b.shape[1]
    db = v_g.shape[1]
    nh = w_c.shape[0]
    kw = conv_w.shape[0]
    lead = CONV_HALO - (kw - 1)

    def body(p_ref, ph_ref, cw_ref, cb_ref, ag_ref, ab_ref, vg_ref, vb_ref, wc_ref, bias_ref, cat_ref, a1_ref, ext_ref,
             a1s_ref, sh_ref):
        i = pl.program_id(0)
        val = p_ref[:, 0:da].astype(F32)
        gat = p_ref[:, da:2 * da].astype(F32)
        ext_ref[CONV_HALO:, :] = val * jax.nn.sigmoid(gat)
        hv = ph_ref[:, 0:da].astype(F32)
        hg = ph_ref[:, da:2 * da].astype(F32)
        ext_ref[0:CONV_HALO, :] = jnp.where(i == 0, 0.0, hv * jax.nn.sigmoid(hg))
        _fill_shifted(ext_ref, sh_ref)
        for r0, c0, rb, cb in _chunks(tm, da, 64, 256):
            cols = pl.ds(c0, cb)
            acc = jnp.broadcast_to(cb_ref[:, cols], (rb, cb))
            for k in range(kw):
                acc = acc + cw_ref[k:k + 1, cols] * _shifted(ext_ref, sh_ref, r0 + lead + k, rb, cols)
            a1s_ref[pl.ds(r0, rb), cols] = acc
        a1 = a1s_ref[...]
        a1_ref[...] = a1.astype(BF16)
        z, _, _ = _layernorm_fwd(a1, ag_ref[...], ab_ref[...])
        cat_ref[:, 0:da] = (z * jax.nn.sigmoid(z)).astype(BF16)

        bu = p_ref[:, 2 * da:2 * da + db].astype(F32)
        bv = p_ref[:, 2 * da + db:].astype(F32)
        vn, _, _ = _layernorm_fwd(bv, vg_ref[...], vb_ref[...])
        vnb = vn.astype(BF16)
        for n in range(tm // CHUNK):
            rows = slice(n * CHUNK, (n + 1) * CHUNK)
            for h in range(nh):
                hc = slice(h * CHUNK, (h + 1) * CHUNK)
                vo = jnp.dot(wc_ref[h], vnb[rows, hc], preferred_element_type=F32) + bias_ref[:, hc]
                cat_ref[rows, da + h * CHUNK:da + (h + 1) * CHUNK] = (bu[rows, hc] * vo).astype(BF16)

    full = lambda a: pl.BlockSpec(a.shape, lambda i: (0,) * a.ndim)
    return pl.pallas_call(
        body, grid=(s // tm,), name="ab_mid_fwd",
        in_specs=[pl.BlockSpec((tm, 2 * da + 2 * db), lambda i: (i, 0)),
                  pl.BlockSpec((CONV_HALO, 2 * da), _prev_halo_map(tm, CONV_HALO)),
                  full(conv_w), full(conv_b), full(a_g), full(a_b), full(v_g), full(v_b), full(w_c), full(bias_full)],
        out_specs=[pl.BlockSpec((tm, da + db), lambda i: (i, 0)), pl.BlockSpec((tm, da), lambda i: (i, 0))],
        out_shape=[_sds((s, da + db), BF16), _sds((s, da), BF16)],
        scratch_shapes=[pltpu.VMEM((tm + CONV_HALO, da), F32), pltpu.VMEM((tm, da), F32),
                        pltpu.VMEM((SUBLANES - 1, tm + CONV_HALO, da), F32)],
        compiler_params=_params(("parallel",)),
    )(proj, proj, conv_w, conv_b, a_g, a_b, v_g, v_b, w_c, bias_full)


def _ab_mid_bwd(dcat, proj, a1, conv_w, a_g, a_b, v_g, v_b, w_c, bias_full, tm):
    s = proj.shape[0]
    da = a_g.shape[1]
    db = v_g.shape[1]
    nh = w_c.shape[0]
    kw = conv_w.shape[0]
    lead = CONV_HALO - (kw - 1)
    n_tiles = s // tm

    def body(dc_ref, dch_ref, p_ref, ph_ref, a1_ref, a1h_ref, cw_ref, ag_ref, ab_ref, vg_ref, vb_ref, wc_ref, bias_ref,
             dp_ref, dcw_ref, vec_ref, dws_ref, dbias_ref, ext_ref, dext_ref, dcw_acc, dvn_ref, sh_ref, dsh_ref):
        i = pl.program_id(0)

        @pl.when(i == 0)
        def _():
            dcw_acc[...] = jnp.zeros_like(dcw_acc)
            vec_ref[...] = jnp.zeros_like(vec_ref)
            dws_ref[...] = jnp.zeros_like(dws_ref)
            dbias_ref[...] = jnp.zeros_like(dbias_ref)

        agv, abv = ag_ref[...], ab_ref[...]

        def silu_ln_bwd(a1v, d_a2):
            z, yn, rstd = _layernorm_fwd(a1v, agv, abv)
            sg = jax.nn.sigmoid(z)
            return _layernorm_bwd(d_a2 * (sg * (1.0 + z * (1.0 - sg))), yn, rstd, agv)

        d_a1, dga, dba = silu_ln_bwd(a1_ref[...].astype(F32), dc_ref[:, 0:da].astype(F32))
        dext_ref[0:tm, :] = d_a1
        d_a1h, _, _ = silu_ln_bwd(a1h_ref[...].astype(F32), dch_ref[...].astype(F32))
        dext_ref[tm:, :] = jnp.where(i == n_tiles - 1, 0.0, d_a1h)
        vec_ref[0:1, 0:da] += dga
        vec_ref[1:2, 0:da] += dba
        vec_ref[2:3, 0:da] += jnp.sum(d_a1, axis=0, keepdims=True)

        val = p_ref[:, 0:da].astype(F32)
        sgg = jax.nn.sigmoid(p_ref[:, da:2 * da].astype(F32))
        ext_ref[CONV_HALO:, :] = val * sgg
        hv = ph_ref[:, 0:da].astype(F32)
        hg = ph_ref[:, da:2 * da].astype(F32)
        ext_ref[0:CONV_HALO, :] = jnp.where(i == 0, 0.0, hv * jax.nn.sigmoid(hg))

        _fill_shifted(ext_ref, sh_ref)
        _fill_shifted(dext_ref, dsh_ref)
        for r0, c0, rb, cb in _chunks(tm, da, 64, 256):
            cols = pl.ds(c0, cb)
            rows = pl.ds(r0, rb)
            d1 = dext_ref[rows, cols]
            acc = jnp.zeros((rb, cb), F32)
            for k in range(kw):
                acc = acc + cw_ref[k:k + 1, cols] * _shifted(dext_ref, dsh_ref, r0 + (kw - 1) - k, rb, cols)
                prod = d1 * _shifted(ext_ref, sh_ref, r0 + lead + k, rb, cols)
                dcw_acc[k, :, cols] += jnp.sum(prod.reshape(rb // SUBLANES, SUBLANES, cb), axis=0)
            v = p_ref[rows, pl.ds(c0, cb)].astype(F32)
            sg = jax.nn.sigmoid(p_ref[rows, pl.ds(da + c0, cb)].astype(F32))
            dp_ref[rows, pl.ds(c0, cb)] = (acc * sg).astype(BF16)
            dp_ref[rows, pl.ds(da + c0, cb)] = (acc * v * sg * (1.0 - sg)).astype(BF16)

        vgv = vg_ref[...]
        bu = p_ref[:, 2 * da:2 * da + db].astype(F32)
        bv = p_ref[:, 2 * da + db:].astype(F32)
        vn, yn_v, rstd_v = _layernorm_fwd(bv, vgv, vb_ref[...])
        vnb = vn.astype(BF16)
        for n in range(tm // CHUNK):
            rows = slice(n * CHUNK, (n + 1) * CHUNK)
            for h in range(nh):
                hc = slice(h * CHUNK, (h + 1) * CHUNK)
                wch = wc_ref[h]
                blk = vnb[rows, hc]
                vo = jnp.dot(wch, blk, preferred_element_type=F32) + bias_ref[:, hc]
                d_bout = dc_ref[rows, da + h * CHUNK:da + (h + 1) * CHUNK].astype(F32)
                dp_ref[rows, 2 * da + h * CHUNK:2 * da + (h + 1) * CHUNK] = (d_bout * vo).astype(BF16)
                d_vo = d_bout * bu[rows, hc]
                dbias_ref[h] += jnp.sum(d_vo, axis=1, keepdims=True)
                d_vob = d_vo.astype(BF16)
                dws_ref[h] += lax.dot_general(d_vob, blk, (((1,), (1,)), ((), ())), preferred_element_type=F32)
                dvn_ref[rows, hc] = lax.dot_general(wch, d_vob, (((0,), (0,)), ((), ())), preferred_element_type=F32)
        d_bv, dgv, dbv = _layernorm_bwd(dvn_ref[...], yn_v, rstd_v, vgv)
        dp_ref[:, 2 * da + db:] = d_bv.astype(BF16)
        vec_ref[3:4, 0:db] += dgv
        vec_ref[4:5, 0:db] += dbv

        @pl.when(i == n_tiles - 1)
        def _():
            dcw_ref[...] = jnp.sum(dcw_acc[...], axis=1)
            causal = lax.broadcasted_iota(jnp.int32, (CHUNK, CHUNK), 0) >= lax.broadcasted_iota(jnp.int32, (CHUNK, CHUNK), 1)
            for h in range(nh):
                dws_ref[h] = jnp.where(causal, dws_ref[h], 0.0)

    full = lambda a: pl.BlockSpec(a.shape, lambda i: (0,) * a.ndim)
    wide = max(da, db)
    return pl.pallas_call(
        body, grid=(n_tiles,), name="ab_mid_bwd",
        in_specs=[pl.BlockSpec((tm, da + db), lambda i: (i, 0)),
                  pl.BlockSpec((CONV_HALO, da), _next_halo_map(tm, CONV_HALO, s)),
                  pl.BlockSpec((tm, 2 * da + 2 * db), lambda i: (i, 0)),
                  pl.BlockSpec((CONV_HALO, 2 * da), _prev_halo_map(tm, CONV_HALO)),
                  pl.BlockSpec((tm, da), lambda i: (i, 0)),
                  pl.BlockSpec((CONV_HALO, da), _next_halo_map(tm, CONV_HALO, s)),
                  full(conv_w), full(a_g), full(a_b), full(v_g), full(v_b), full(w_c), full(bias_full)],
        out_specs=[pl.BlockSpec((tm, 2 * da + 2 * db), lambda i: (i, 0)),
                   pl.BlockSpec((kw, da), lambda i: (0, 0)),
                   pl.BlockSpec((8, wide), lambda i: (0, 0)),
                   pl.BlockSpec((nh, CHUNK, CHUNK), lambda i: (0, 0, 0)),
                   pl.BlockSpec((nh, CHUNK, 1), lambda i: (0, 0, 0))],
        out_shape=[_sds((s, 2 * da + 2 * db), BF16), _sds((kw, da), F32), _sds((8, wide), F32),
                   _sds((nh, CHUNK, CHUNK), F32), _sds((nh, CHUNK, 1), F32)],
        scratch_shapes=[pltpu.VMEM((tm + CONV_HALO, da), F32), pltpu.VMEM((tm + CONV_HALO, da), F32),
                        pltpu.VMEM((kw, SUBLANES, da), F32), pltpu.VMEM((tm, db), F32),
                        pltpu.VMEM((SUBLANES - 1, tm + CONV_HALO, da), F32), pltpu.VMEM((SUBLANES - 1, tm + CONV_HALO, da), F32)],
        compiler_params=_params(("arbitrary",)),
    )(dcat, dcat, proj, proj, a1, a1, conv_w, a_g, a_b, v_g, v_b, w_c, bias_full)


def _ada_fwd(c_all, w, b):
    nl, d, n = w.shape
    tn = _pick(n, (512, 256, 128))

    def body(c_ref, w_ref, b_ref, o_ref):
        cv = c_ref[...]
        cond = (cv * jax.nn.sigmoid(cv)).astype(BF16)
        o_ref[...] = jnp.dot(cond, w_ref[...].astype(BF16), preferred_element_type=F32) + b_ref[...]

    return pl.pallas_call(
        body, grid=(nl, n // tn), name="ada_fwd",
        in_specs=[pl.BlockSpec(c_all.shape, lambda l, j: (0, 0)), pl.BlockSpec((None, d, tn), lambda l, j: (l, 0, j)),
                  pl.BlockSpec((None, 1, tn), lambda l, j: (l, 0, j))],
        out_specs=pl.BlockSpec((None, c_all.shape[0], tn), lambda l, j: (l, 0, j)),
        out_shape=_sds((nl, c_all.shape[0], n), F32), compiler_params=_params(("parallel", "parallel")),
    )(c_all, w, b)


def _ada_bwd(c_all, dmod):
    nl, nb, n = dmod.shape
    d = c_all.shape[1]
    tn = _pick(n, (512, 256, 128))

    def body(c_ref, g_ref, o_ref):
        cv = c_ref[...]
        cond = (cv * jax.nn.sigmoid(cv)).astype(BF16)
        o_ref[...] = lax.dot_general(cond, g_ref[...].astype(BF16), (((0,), (0,)), ((), ())), preferred_element_type=F32)

    return pl.pallas_call(
        body, grid=(nl, n // tn), name="ada_bwd",
        in_specs=[pl.BlockSpec(c_all.shape, lambda l, j: (0, 0)), pl.BlockSpec((None, nb, tn), lambda l, j: (l, 0, j))],
        out_specs=pl.BlockSpec((None, d, tn), lambda l, j: (l, 0, j)),
        out_shape=_sds((nl, d, n), F32), compiler_params=_params(("parallel", "parallel")),
    )(c_all, dmod)


def _sum_leading(a, out_dtype=F32, name="sum_leading"):
    n, r, c = a.shape
    tr = _row_tile(r, c, 256 * 1024)

    def body(a_ref, o_ref):
        acc = a_ref[0].astype(F32)
        for k in range(1, n):
            acc = acc + a_ref[k].astype(F32)
        o_ref[...] = acc.astype(out_dtype)

    return pl.pallas_call(body, grid=(r // tr,), name=name, in_specs=[pl.BlockSpec((n, tr, c), lambda i: (0, i, 0))],
                          out_specs=pl.BlockSpec((tr, c), lambda i: (i, 0)), out_shape=_sds((r, c), out_dtype),
                          compiler_params=_params(("parallel",)))(a)


def _cast_into_full(w, kind, s_vec, l0, nl):
    _, r, c = w.shape
    tr = _row_tile(r, c, 512 * 1024)
    nb = r // tr
    if kind == "col":
        out_shape, out_spec = (nl, r, N_CHIP * c), pl.BlockSpec((None, tr, c), lambda l, i, sv: (l, i, sv[0]))
    else:
        out_shape, out_spec = (nl, N_CHIP * r, c), pl.BlockSpec((None, tr, c), lambda l, i, sv: (l, sv[0] * nb + i, 0))

    def body(sv_ref, w_ref, o_ref):
        o_ref[...] = w_ref[...].astype(BF16)

    return pl.pallas_call(
        body, name="cast_into_full",
        grid_spec=pltpu.PrefetchScalarGridSpec(num_scalar_prefetch=1, grid=(nl, nb),
                                               in_specs=[pl.BlockSpec((None, tr, c), lambda l, i, sv: (l0 + l, i, 0))], out_specs=out_spec),
        out_shape=_sds(out_shape, BF16), compiler_params=_params(("parallel", "parallel")),
    )(s_vec, w)


def _chip_sum_into(slab, cp, g, g_shape, layer, kind, sc_vec, rows_per_group=None):
    n, r, c = slab.shape
    rg = r if rows_per_group is None else rows_per_group
    tr = _row_tile(rg, c, 256 * 1024)
    groups = r // rg
    nbg = rg // tr
    nb = groups * nbg
    n_sc = len(sc_vec)
    if kind == "col":
        out_spec = pl.BlockSpec((None, tr, c), lambda gi, i, *sc: (layer, sc[1][0] * nb + gi * nbg + i, 0))
        own_spec = pl.BlockSpec((tr, c), lambda gi, i, *sc: (gi * nbg + i, sc[0][0]))
    else:
        out_spec = pl.BlockSpec((None, tr, c), lambda gi, i, *sc: (layer, gi * nbg + i, sc[1][0]))
        own_spec = pl.BlockSpec((tr, c), lambda gi, i, *sc: (gi * (n * nbg) + sc[0][0] * nbg + i, 0))

    def other(k):
        return pl.BlockSpec((None, tr, c), lambda gi, i, *sc: (sc[1 + k][0], gi * nbg + i, 0))

    in_specs = [own_spec] + [other(k) for k in range(1, n)]
    args = list(sc_vec) + [cp] + [slab] * (n - 1)
    aliases = {}
    if g is not None:
        in_specs.append(pl.BlockSpec(memory_space=pl.ANY))
        args.append(g)
        aliases = {len(args) - 1: 0}

    def body(*refs):
        own_ref, rest = refs[n_sc], refs[n_sc + 1:]
        o_ref = rest[-1]
        acc = own_ref[...].astype(F32)
        for k in range(n - 1):
            acc = acc + rest[k][...].astype(F32)
        o_ref[...] = acc

    return pl.pallas_call(
        body, name="chip_sum",
        grid_spec=pltpu.PrefetchScalarGridSpec(num_scalar_prefetch=n_sc, grid=(groups, nbg), in_specs=in_specs, out_specs=out_spec),
        out_shape=_sds(g_shape, F32), input_output_aliases=aliases, compiler_params=_params(("parallel", "parallel")),
    )(*args)


def _pair_add(dw, sib, kind, c_vec):
    r, c = sib.shape
    tr = _row_tile(r, c, 512 * 1024)
    nb = r // tr
    if kind == "col":
        dw_spec = pl.BlockSpec((tr, c), lambda i, cv: (cv[0] * nb + i, 0))
    else:
        dw_spec = pl.BlockSpec((tr, c), lambda i, cv: (i, cv[0]))

    def body(cv_ref, dw_ref, sib_ref, o_ref):
        o_ref[...] = (dw_ref[...].astype(F32) + sib_ref[...].astype(F32)).astype(BF16)

    return pl.pallas_call(
        body, name="pair_add",
        grid_spec=pltpu.PrefetchScalarGridSpec(num_scalar_prefetch=1, grid=(nb,), in_specs=[dw_spec, pl.BlockSpec((tr, c), lambda i, cv: (i, 0))],
                                               out_specs=pl.BlockSpec((tr, c), lambda i, cv: (i, 0))),
        out_shape=_sds((r, c), BF16), compiler_params=_params(("parallel",)),
    )(c_vec, dw, sib)


def _adamw(w, g, m, v):
    shape = w.shape
    cols = shape[-1]
    rows = w.size // cols
    tr = _row_tile(rows, cols, 256 * 1024)
    bc1 = 1.0 - ADAM_B1 ** ADAM_STEP
    bc2 = 1.0 - ADAM_B2 ** ADAM_STEP

    def body(w_ref, g_ref, m_ref, v_ref, d_ref, mo_ref, vo_ref):
        gv = g_ref[...]
        mn = ADAM_B1 * m_ref[...] + (1.0 - ADAM_B1) * gv
        vn = ADAM_B2 * v_ref[...] + (1.0 - ADAM_B2) * (gv * gv)
        d_ref[...] = -ADAM_LR * ((mn / bc1) / (jnp.sqrt(vn / bc2) + ADAM_EPS) + ADAM_WD * w_ref[...])
        mo_ref[...] = mn
        vo_ref[...] = vn

    spec = pl.BlockSpec((tr, cols), lambda i: (i, 0))
    outs = pl.pallas_call(body, grid=(rows // tr,), name="adamw", in_specs=[spec] * 4, out_specs=[spec] * 3,
                          out_shape=[_sds((rows, cols), F32)] * 3, compiler_params=_params(("parallel",)))(
        *[a.reshape(rows, cols) for a in (w, g, m, v)])
    return [o.reshape(shape) for o in outs]


def _mesh_pos():
    return lax.axis_index("x"), lax.axis_index("y"), lax.axis_index("c")


def _other_chips(x, y):
    return [(1 - x, y), (x, 1 - y), (1 - x, 1 - y)]


def _allgather_small(a, after=()):
    r, c = a.shape

    def body(x_ref, *rest):
        out_ref, send_sems, recv_sems, local_sem = rest[len(after):]
        x, y, cc = _mesh_pos()
        me, sibling = (x, y, cc), (x, y, 1 - cc)
        chips = _other_chips(x, y)

        def slab(px, py, pc):
            return out_ref.at[4 * px + 2 * py + pc]

        def copy(k, block, to, src=None):
            return pltpu.make_async_remote_copy(src_ref=slab(*block) if src is None else src, dst_ref=slab(*block),
                                                send_sem=send_sems.at[k], recv_sem=recv_sems.at[k], device_id=to,
                                                device_id_type=MESH)

        mine = pltpu.make_async_copy(x_ref, slab(*me), local_sem)
        mine.start()
        first = [copy(0, me, sibling, src=x_ref)]
        first += [copy(1 + j, me, (*chip, cc), src=x_ref) for j, chip in enumerate(chips)]
        for cp in first:
            cp.start()
        passed = [copy(4 + j, (*chip, cc), sibling) for j, chip in enumerate(chips)]
        for j, chip in enumerate(chips):
            copy(1 + j, (*chip, cc), me).wait_recv()
            passed[j].start()
        copy(0, sibling, me).wait_recv()
        for j, chip in enumerate(chips):
            copy(4 + j, (*chip, 1 - cc), me).wait_recv()
        for cp in first + passed:
            cp.wait_send()
        mine.wait()

    return pl.pallas_call(
        body, name="allgather_small", out_shape=_sds((N_DEV, r, c), F32),
        in_specs=[pl.BlockSpec(memory_space=pltpu.VMEM)] + [pl.BlockSpec(memory_space=pl.ANY)] * len(after),
        out_specs=pl.BlockSpec(memory_space=pltpu.VMEM),
        scratch_shapes=[pltpu.SemaphoreType.DMA((7,)), pltpu.SemaphoreType.DMA((7,)), pltpu.SemaphoreType.DMA],
        compiler_params=pltpu.CompilerParams(vmem_limit_bytes=V7X_VMEM_LIMIT_BYTES),
    )(a, *after)


def _idx(ref, rows=None, cols=None):
    lead = (slice(None),) * (len(ref.shape) - 2)
    return ref.at[lead + (slice(None) if rows is None else rows, slice(None) if cols is None else cols)]


def _half(ref, kind, c):
    r, cdim = ref.shape[-2:]
    if kind == "col":
        return _idx(ref, rows=pl.ds(c * (r // 2), r // 2))
    return _idx(ref, cols=pl.ds(c * (cdim // 2), cdim // 2))


def _shard_region(full, kind, shard_shape, s):
    r, cdim = shard_shape[-2:]
    if kind == "col":
        return _idx(full, cols=pl.ds(s * cdim, cdim))
    return _idx(full, rows=pl.ds(s * r, r))


def _allgather_weights(fulls, kinds, shard_shapes):
    nt = len(fulls)

    def body(*refs):
        fu = refs[nt:2 * nt]
        send_sems, recv_sems = refs[2 * nt:]
        x, y, c = _mesh_pos()
        chips = _other_chips(x, y)

        def part(t, chip, cc):
            return _half(_shard_region(fu[t], kinds[t], shard_shapes[t], 2 * chip[0] + chip[1]), kinds[t], cc)

        def copy(t, k, blk, to):
            return pltpu.make_async_remote_copy(src_ref=blk, dst_ref=blk, send_sem=send_sems.at[6 * t + k],
                                                recv_sem=recv_sems.at[6 * t + k], device_id=to, device_id_type=MESH)

        first, passed = [], []
        for t in range(nt):
            for j, chip in enumerate(chips):
                cp = copy(t, j, part(t, (x, y), c), (*chip, c))
                cp.start()
                first.append(cp)
        for t in range(nt):
            for j, chip in enumerate(chips):
                copy(t, j, part(t, chip, c), (x, y, c)).wait_recv()
                fw = copy(t, 3 + j, part(t, chip, c), (x, y, 1 - c))
                fw.start()
                passed.append(fw)
        for t in range(nt):
            for j, chip in enumerate(chips):
                copy(t, 3 + j, part(t, chip, 1 - c), (x, y, c)).wait_recv()
        for cp in first + passed:
            cp.wait_send()

    return pl.pallas_call(
        body, name="allgather_weights", out_shape=[_sds(f.shape, BF16) for f in fulls],
        in_specs=[HBM] * nt, out_specs=[HBM] * nt, input_output_aliases={t: t for t in range(nt)},
        scratch_shapes=[pltpu.SemaphoreType.DMA((6 * nt,)), pltpu.SemaphoreType.DMA((6 * nt,))],
    )(*fulls)


SEM = pl.BlockSpec(memory_space=pltpu.SEMAPHORE)
ANY = pl.BlockSpec(memory_space=pl.ANY)
EFFECT = pltpu.SideEffectType.DATAFLOW_SIDE_EFFECTING


def _gather_start(name, fulls, kinds, shard_shapes, prev):
    nt = len(fulls)

    def body(*refs):
        send_sems, recv_sems = refs[nt + 1], refs[nt + 2]
        fu = refs[nt + 3:2 * nt + 3]
        token = refs[2 * nt + 3]
        x, y, c = _mesh_pos()
        for t in range(nt):
            mine = _half(_shard_region(fu[t], kinds[t], shard_shapes[t], 2 * x + y), kinds[t], c)
            for j, chip in enumerate(_other_chips(x, y)):
                for e in range(2):
                    pltpu.make_async_remote_copy(src_ref=mine, dst_ref=mine, send_sem=send_sems.at[6 * t + 2 * j + e],
                                                 recv_sem=recv_sems.at[6 * t + 2 * j + c], device_id=(*chip, e),
                                                 device_id_type=MESH).start()
        token[...] = jnp.zeros_like(token)

    outs = pl.pallas_call(
        body, name=name,
        out_shape=(pltpu.SemaphoreType.DMA((6 * nt,)), pltpu.SemaphoreType.DMA((6 * nt,)), *[pltpu.HBM(f.shape, f.dtype) for f in fulls],
                   _sds((8, 128), F32)),
        in_specs=[HBM] * nt + [ANY], out_specs=(SEM, SEM, *[HBM] * nt, pl.BlockSpec(memory_space=pltpu.VMEM)),
        input_output_aliases={t: 2 + t for t in range(nt)}, compiler_params=pltpu.CompilerParams(has_side_effects=EFFECT),
    )(*fulls, prev)
    return outs[0], outs[1], list(outs[2:2 + nt]), outs[2 + nt]


def _gather_wait(name, send_sems, recv_sems, fulls, kinds, shard_shapes, after):
    nt = len(fulls)

    def body(*refs):
        fu = refs[:nt]
        send_sems, recv_sems = refs[nt], refs[nt + 1]
        x, y, c = _mesh_pos()

        def part(t, chip, cc):
            return _half(_shard_region(fu[t], kinds[t], shard_shapes[t], 2 * chip[0] + chip[1]), kinds[t], cc)

        for t in range(nt):
            for j, chip in enumerate(_other_chips(x, y)):
                for e in range(2):
                    mine = part(t, (x, y), c)
                    pltpu.make_async_remote_copy(src_ref=mine, dst_ref=mine, send_sem=send_sems.at[6 * t + 2 * j + e],
                                                 recv_sem=recv_sems.at[6 * t + 2 * j + e], device_id=(x, y, c),
                                                 device_id_type=MESH).wait_send()
                    landed = part(t, chip, e)
                    pltpu.make_async_remote_copy(src_ref=landed, dst_ref=landed, send_sem=send_sems.at[6 * t + 2 * j + e],
                                                 recv_sem=recv_sems.at[6 * t + 2 * j + e], device_id=(x, y, c),
                                                 device_id_type=MESH).wait_recv()

    return pl.pallas_call(
        body, name=name, out_shape=[pltpu.HBM(f.shape, f.dtype) for f in fulls], in_specs=[HBM] * nt + [SEM, SEM, ANY],
        out_specs=[HBM] * nt, input_output_aliases={t: t for t in range(nt)},
        compiler_params=pltpu.CompilerParams(has_side_effects=EFFECT),
    )(*fulls, send_sems, recv_sems, after)


def _reduce_start(name, cps, kinds, prev):
    nt = len(cps)
    slab_shapes = []
    for cp, kind in zip(cps, kinds):
        shp = list(cp.shape)
        shp[-1 if kind == "col" else -2] //= N_CHIP
        slab_shapes.append((N_CHIP,) + tuple(shp))

    def body(*refs):
        send_sems, recv_sems = refs[nt + 1], refs[nt + 2]
        src = refs[nt + 3:2 * nt + 3]
        dst = refs[2 * nt + 3:3 * nt + 3]
        token = refs[3 * nt + 3]
        x, y, c = _mesh_pos()
        s = 2 * x + y
        for t in range(nt):
            for j, chip in enumerate(_other_chips(x, y)):
                pltpu.make_async_remote_copy(src_ref=_chip_block(src[t], kinds[t], 2 * chip[0] + chip[1]), dst_ref=dst[t].at[s],
                                             send_sem=send_sems.at[3 * t + j], recv_sem=recv_sems.at[3 * t + j],
                                             device_id=(*chip, c), device_id_type=MESH).start()
        token[...] = jnp.zeros_like(token)

    outs = pl.pallas_call(
        body, name=name,
        out_shape=(pltpu.SemaphoreType.DMA((3 * nt,)), pltpu.SemaphoreType.DMA((3 * nt,)), *[pltpu.HBM(a.shape, a.dtype) for a in cps],
                   *[pltpu.HBM(shp, BF16) for shp in slab_shapes], _sds((8, 128), F32)),
        in_specs=[HBM] * nt + [ANY], out_specs=(SEM, SEM, *[HBM] * (2 * nt), pl.BlockSpec(memory_space=pltpu.VMEM)),
        input_output_aliases={t: 2 + t for t in range(nt)}, compiler_params=pltpu.CompilerParams(has_side_effects=EFFECT),
    )(*cps, prev)
    return outs[0], outs[1], list(outs[2:2 + nt]), list(outs[2 + nt:2 + 2 * nt]), outs[2 + 2 * nt]


def _reduce_wait(name, send_sems, recv_sems, cps, slabs, kinds, after):
    nt = len(cps)

    def body(*refs):
        src, dst = refs[:nt], refs[nt:2 * nt]
        send_sems, recv_sems = refs[2 * nt], refs[2 * nt + 1]
        x, y, c = _mesh_pos()
        for t in range(nt):
            for j, chip in enumerate(_other_chips(x, y)):
                sj = 2 * chip[0] + chip[1]
                pltpu.make_async_remote_copy(src_ref=_chip_block(src[t], kinds[t], sj), dst_ref=dst[t].at[sj],
                                             send_sem=send_sems.at[3 * t + j], recv_sem=recv_sems.at[3 * t + j],
                                             device_id=(x, y, c), device_id_type=MESH).wait()

    outs = pl.pallas_call(
        body, name=name, out_shape=[pltpu.HBM(a.shape, a.dtype) for a in cps] + [pltpu.HBM(a.shape, a.dtype) for a in slabs],
        in_specs=[HBM] * (2 * nt) + [SEM, SEM, ANY], out_specs=[HBM] * (2 * nt), input_output_aliases={t: t for t in range(2 * nt)},
        compiler_params=pltpu.CompilerParams(has_side_effects=EFFECT),
    )(*cps, *slabs, send_sems, recv_sems, after)
    return list(outs[:nt]), list(outs[nt:])


def _pair_start(name, dws, kinds, prev):
    nt = len(dws)
    sib_shapes = []
    for dw, kind in zip(dws, kinds):
        shp = list(dw.shape)
        shp[-2 if kind == "col" else -1] //= 2
        sib_shapes.append(tuple(shp))

    def body(*refs):
        send_sems, recv_sems = refs[nt + 1], refs[nt + 2]
        src = refs[nt + 3:2 * nt + 3]
        dst = refs[2 * nt + 3:3 * nt + 3]
        token = refs[3 * nt + 3]
        x, y, c = _mesh_pos()
        for t in range(nt):
            pltpu.make_async_remote_copy(src_ref=_half(src[t], kinds[t], 1 - c), dst_ref=dst[t], send_sem=send_sems.at[t],
                                         recv_sem=recv_sems.at[t], device_id=(x, y, 1 - c), device_id_type=MESH).start()
        token[...] = jnp.zeros_like(token)

    outs = pl.pallas_call(
        body, name=name,
        out_shape=(pltpu.SemaphoreType.DMA((nt,)), pltpu.SemaphoreType.DMA((nt,)), *[pltpu.HBM(a.shape, a.dtype) for a in dws],
                   *[pltpu.HBM(shp, BF16) for shp in sib_shapes], _sds((8, 128), F32)),
        in_specs=[HBM] * nt + [ANY], out_specs=(SEM, SEM, *[HBM] * (2 * nt), pl.BlockSpec(memory_space=pltpu.VMEM)),
        input_output_aliases={t: 2 + t for t in range(nt)}, compiler_params=pltpu.CompilerParams(has_side_effects=EFFECT),
    )(*dws, prev)
    return outs[0], outs[1], list(outs[2:2 + nt]), list(outs[2 + nt:2 + 2 * nt]), outs[2 + 2 * nt]


def _pair_wait(name, send_sems, recv_sems, dws, sibs, kinds, after):
    nt = len(dws)

    def body(*refs):
        src, dst = refs[:nt], refs[nt:2 * nt]
        send_sems, recv_sems = refs[2 * nt], refs[2 * nt + 1]
        x, y, c = _mesh_pos()
        for t in range(nt):
            pltpu.make_async_remote_copy(src_ref=_half(src[t], kinds[t], 1 - c), dst_ref=dst[t], send_sem=send_sems.at[t],
                                         recv_sem=recv_sems.at[t], device_id=(x, y, c), device_id_type=MESH).wait()

    outs = pl.pallas_call(
        body, name=name, out_shape=[pltpu.HBM(a.shape, a.dtype) for a in dws] + [pltpu.HBM(a.shape, a.dtype) for a in sibs],
        in_specs=[HBM] * (2 * nt) + [SEM, SEM, ANY], out_specs=[HBM] * (2 * nt), input_output_aliases={t: t for t in range(2 * nt)},
        compiler_params=pltpu.CompilerParams(has_side_effects=EFFECT),
    )(*dws, *sibs, send_sems, recv_sems, after)
    return list(outs[:nt]), list(outs[nt:])


def _pair_exchange(dws, kinds):
    nt = len(dws)
    out_shapes = []
    for dw, kind in zip(dws, kinds):
        shp = list(dw.shape)
        shp[-2 if kind == "col" else -1] //= 2
        out_shapes.append(tuple(shp))

    def body(*refs):
        src, dst = refs[:nt], refs[nt:2 * nt]
        send_sems, recv_sems = refs[2 * nt:]
        x, y, c = _mesh_pos()
        copies = [pltpu.make_async_remote_copy(src_ref=_half(src[t], kinds[t], 1 - c), dst_ref=dst[t], send_sem=send_sems.at[t],
                                               recv_sem=recv_sems.at[t], device_id=(x, y, 1 - c), device_id_type=MESH)
                  for t in range(nt)]
        for cp in copies:
            cp.start()
        for cp in copies:
            cp.wait()

    return pl.pallas_call(
        body, name="grad_pair_exchange", out_shape=[_sds(shp, BF16) for shp in out_shapes], in_specs=[HBM] * nt,
        out_specs=[HBM] * nt, scratch_shapes=[pltpu.SemaphoreType.DMA((nt,)), pltpu.SemaphoreType.DMA((nt,))],
    )(*dws)


def _chip_block(ref, kind, s):
    r, cdim = ref.shape[-2:]
    if kind == "col":
        return _idx(ref, cols=pl.ds(s * (cdim // N_CHIP), cdim // N_CHIP))
    return _idx(ref, rows=pl.ds(s * (r // N_CHIP), r // N_CHIP))


def _chip_exchange(cps, kinds):
    nt = len(cps)
    out_shapes = []
    for cp, kind in zip(cps, kinds):
        shp = list(cp.shape)
        shp[-1 if kind == "col" else -2] //= N_CHIP
        out_shapes.append((N_CHIP,) + tuple(shp))

    def body(*refs):
        src, dst = refs[:nt], refs[nt:2 * nt]
        send_sems, recv_sems, local_sems = refs[2 * nt:]
        x, y, c = _mesh_pos()
        s = 2 * x + y
        chips = _other_chips(x, y)
        sends, locals_ = [], []
        for t in range(nt):
            own = pltpu.make_async_copy(_chip_block(src[t], kinds[t], s), dst[t].at[s], local_sems.at[t])
            own.start()
            locals_.append(own)
            for j, chip in enumerate(chips):
                cp = pltpu.make_async_remote_copy(src_ref=_chip_block(src[t], kinds[t], 2 * chip[0] + chip[1]), dst_ref=dst[t].at[s],
                                                  send_sem=send_sems.at[3 * t + j], recv_sem=recv_sems.at[3 * t + j],
                                                  device_id=(*chip, c), device_id_type=MESH)
                cp.start()
                sends.append(cp)
        for t in range(nt):
            for j, chip in enumerate(chips):
                landing = dst[t].at[2 * chip[0] + chip[1]]
                pltpu.make_async_remote_copy(src_ref=landing, dst_ref=landing, send_sem=send_sems.at[3 * t + j],
                                             recv_sem=recv_sems.at[3 * t + j], device_id=(x, y, c), device_id_type=MESH).wait_recv()
        for cp in sends:
            cp.wait_send()
        for own in locals_:
            own.wait()

    return pl.pallas_call(
        body, name="grad_chip_exchange", out_shape=[_sds(shp, BF16) for shp in out_shapes], in_specs=[HBM] * nt,
        out_specs=[HBM] * nt,
        scratch_shapes=[pltpu.SemaphoreType.DMA((3 * nt,)), pltpu.SemaphoreType.DMA((3 * nt,)), pltpu.SemaphoreType.DMA((nt,))],
    )(*cps)


def _pair_assemble(gs, kinds):
    nt = len(gs)

    def body(*refs):
        g = refs[nt:2 * nt]
        send_sems, recv_sems = refs[2 * nt:]
        x, y, c = _mesh_pos()
        copies = []
        for t in range(nt):
            mine = _half(g[t], kinds[t], c)
            cp = pltpu.make_async_remote_copy(src_ref=mine, dst_ref=mine, send_sem=send_sems.at[t], recv_sem=recv_sems.at[t],
                                              device_id=(x, y, 1 - c), device_id_type=MESH)
            cp.start()
            copies.append(cp)
        for t in range(nt):
            landing = _half(g[t], kinds[t], 1 - c)
            pltpu.make_async_remote_copy(src_ref=landing, dst_ref=landing, send_sem=send_sems.at[t], recv_sem=recv_sems.at[t],
                                         device_id=(x, y, c), device_id_type=MESH).wait_recv()
        for cp in copies:
            cp.wait_send()

    return pl.pallas_call(
        body, name="grad_pair_assemble", out_shape=[_sds(a.shape, F32) for a in gs], in_specs=[HBM] * nt,
        out_specs=[HBM] * nt, input_output_aliases={t: t for t in range(nt)},
        scratch_shapes=[pltpu.SemaphoreType.DMA((nt,)), pltpu.SemaphoreType.DMA((nt,))],
    )(*gs)


def _pack(arrays, width):
    flat = jnp.concatenate([a.reshape(-1) for a in arrays])
    pad = (-flat.size) % (8 * width)
    return jnp.pad(flat, (0, pad)).reshape(-1, width)


def _unpack(packed, shapes):
    flat = packed.reshape(-1)
    out, off = [], 0
    for shp in shapes:
        n = 1
        for dim in shp:
            n *= dim
        out.append(flat[off:off + n].reshape(shp))
        off += n
    return out


def kernel(x, c, ada_w, ada_b, norm_mix_g, norm_ffn_g, ab_w_in, a_conv_w, a_conv_b, a_norm_g, a_norm_b, b_norm_g, b_norm_b, b_w_s, b_bias, ab_w_out, pool_w, pool_scale, ffn_w1, ffn_w3, ffn_w2, final_g, loss_target, m_ada_w, m_ada_b, m_norm_mix_g, m_norm_ffn_g, m_ab_w_in, m_a_conv_w, m_a_conv_b, m_a_norm_g, m_a_norm_b, m_b_norm_g, m_b_norm_b, m_b_w_s, m_b_bias, m_ab_w_out, m_pool_w, m_pool_scale, m_ffn_w1, m_ffn_w3, m_ffn_w2, m_final_g, v_ada_w, v_ada_b, v_norm_mix_g, v_norm_ffn_g, v_ab_w_in, v_a_conv_w, v_a_conv_b, v_a_norm_g, v_a_norm_b, v_b_norm_g, v_b_norm_b, v_b_w_s, v_b_bias, v_ab_w_out, v_pool_w, v_pool_scale, v_ffn_w1, v_ffn_w3, v_ffn_w2, v_final_g):
    mx, my, mc = _mesh_pos()
    chip = 2 * mx + my
    dev = 4 * mx + 2 * my + mc
    x2 = x[0]
    target = loss_target[0]
    s, d = x2.shape
    depth = ada_w.shape[0]
    n_mod = ada_b.shape[1] // d
    n_even = ab_w_in.shape[0]
    da, db = a_conv_b.shape[1], b_norm_g.shape[1]
    nh = b_w_s.shape[1]
    kw = a_conv_w.shape[1]
    n_pool = pool_w.shape[1]
    tm_row = _pick(s, (256, 128))

    s_vec = jnp.reshape(chip, (1,)).astype(jnp.int32)
    c_vec = jnp.reshape(mc, (1,)).astype(jnp.int32)
    sc_vec = [s_vec, c_vec] + [jnp.reshape(v, (1,)).astype(jnp.int32)
                               for v in (2 * mx + (1 - my), 2 * (1 - mx) + my, 2 * (1 - mx) + (1 - my))]
    pool_w3 = pool_w.reshape((-1,) + pool_w.shape[2:])
    shard_of = {"w_in": (ab_w_in, "col"), "w_out": (ab_w_out, "row"), "pool": (pool_w3, "row"), "w1": (ffn_w1, "col"),
                "w3": (ffn_w3, "col"), "w2": (ffn_w2, "row")}

    def layer_names(l):
        return (["w_in", "w_out"] if l % 2 == 0 else ["pool"]) + ["w1", "w3", "w2"]

    def layer_span(nm, l):
        if nm in ("w_in", "w_out"):
            return l // 2, 1
        if nm == "pool":
            return (l // 2) * n_pool, n_pool
        return l, 1

    group_names = {"0a": ["w_in"], "0o": ["w_out"], "0b": ["w1", "w3"], "0c": ["w2"]}
    group_layer = {key: 0 for key in group_names}
    for l in range(1, depth):
        group_names[l], group_layer[l] = layer_names(l), l
    owned, w_kinds, w_shapes = {}, {}, {}
    for key, names in group_names.items():
        l = group_layer[key]
        owned[key] = [_cast_into_full(shard_of[nm][0], shard_of[nm][1], s_vec, *layer_span(nm, l)) for nm in names]
        w_kinds[key] = [shard_of[nm][1] for nm in names]
        w_shapes[key] = [(layer_span(nm, l)[1],) + shard_of[nm][0].shape[1:] for nm in names]
    layer_w = [{} for _ in range(depth)]
    layer_w[0].update(zip(group_names["0a"], _allgather_weights(owned["0a"], w_kinds["0a"], w_shapes["0a"])))
    gathers = {"prev": layer_w[0][group_names["0a"][-1]], "token": None, "flying": {}}

    def gather_start(key):
        sends, recvs, fulls, token = _gather_start(f"gather_start_{key}", owned[key], w_kinds[key], w_shapes[key], gathers["prev"])
        gathers["flying"][key] = (sends, recvs, fulls)
        gathers["prev"] = gathers["token"] = token

    def gather_wait(key, after):
        sends, recvs, fulls = gathers["flying"].pop(key)
        landed = _gather_wait(f"gather_wait_{key}", sends, recvs, fulls, w_kinds[key], w_shapes[key], after)
        layer_w[group_layer[key]].update(zip(group_names[key], landed))
        gathers["prev"] = landed[-1]

    def after_starts(row):
        return row + gathers["token"][0:1, 0:1]

    for key in ("0o", "0b", "0c"):
        gather_start(key)

    pre = _allgather_small(_pack([c, a_conv_w, pool_scale], 128)).reshape(N_DEV, -1)
    n_cw, n_ps = a_conv_w.size, pool_scale.size
    c_all = pre[:, :d]
    cw_chips = pre[0::2, d:d + n_cw].reshape((N_CHIP,) + a_conv_w.shape)
    conv_w_full = jnp.concatenate([cw_chips[k] for k in range(N_CHIP)], axis=-1)
    ps_chips = pre[0::2, d + n_cw:d + n_cw + n_ps].reshape((N_CHIP,) + pool_scale.shape)
    pool_scale_full = jnp.concatenate([ps_chips[k] for k in range(N_CHIP)], axis=-1)
    c_pad = jnp.pad(c_all, ((0, 8), (0, 0)))
    n_ada = ada_w.shape[2]
    ada_b_mine = lax.dynamic_slice_in_dim(ada_b, chip * n_ada, n_ada, axis=1)[:, None, :]
    mod_part = _ada_fwd(c_pad, ada_w, ada_b_mine)[:, :N_DEV, :]
    mod_all = _allgather_small(mod_part.reshape(depth * N_DEV, n_ada))
    mod_chips = mod_all[0::2].reshape(N_CHIP, depth, N_DEV, n_ada)
    mod_mine = lax.dynamic_index_in_dim(mod_chips, dev, axis=2, keepdims=False)
    mod = jnp.transpose(mod_mine, (1, 0, 2)).reshape(depth, n_mod, 1, d)

    causal = jnp.tril(jnp.ones((CHUNK, CHUNK), dtype=bool))
    w_c = jnp.where(causal[None, None], b_w_s, 0.0).astype(BF16)
    bias_full = jnp.repeat(jnp.swapaxes(b_bias, 1, 2), CHUNK, axis=2)

    saved = []
    xs = x2
    for l in range(depth):
        sh1, sc1, g1, sh2, sc2, g2 = [mod[l, k] for k in range(n_mod)]
        i = l // 2
        st = {"x1": xs}
        if 1 <= l and l + 2 < depth:
            gather_start(l + 2)
        gain1 = after_starts(norm_mix_g[l][None])
        wl = layer_w[l]
        if l % 2 == 0:
            h = _norm_mod(xs, gain1, sh1, sc1, tm_row)
            proj = _mm("ab_proj", [h], [(wl["w_in"], 0)], [BF16], _ep_store, tk=2048)[0]
            cat, a1 = _ab_mid_fwd(proj, conv_w_full[i], a_conv_b[i][None], a_norm_g[i][None], a_norm_b[i][None],
                                  b_norm_g[i][None], b_norm_b[i][None], w_c[i], bias_full[i], tm_row)
            if l == 0:
                gather_wait("0o", cat)
            xs, y1 = _mm("ab_out", [cat], [(wl["w_out"], 0)], [F32, BF16], _ep_residual, extras=[xs, g1], extra_kinds=["tile", "row"], tk=2048)
            st.update(h=h, proj=proj, a1=a1, cat=cat, y=y1)
        else:
            p = _pool_fwd(xs, gain1, sh1, sc1, tm_row)
            gate = g1 * pool_scale_full[i][None]
            xs, ymm = _grouped_fwd(p, wl["pool"][None], 0, xs, gate, 1024)
            st.update(p=p, y=ymm, gate=gate)
        st["x2"] = xs
        gain2 = norm_ffn_g[l][None]
        if l == 0:
            gather_wait("0b", xs)
            for nxt in range(1, min(3, depth)):
                gather_start(nxt)
            gain2 = after_starts(gain2)
        h2 = _norm_mod(xs, gain2, sh2, sc2, tm_row)
        u, t, z = _mm("ffn_up", [h2, h2], [(wl["w1"], 0), (wl["w3"], 0)], [BF16, BF16, BF16], _ep_swiglu, tn=512, tk=2048)
        if l == 0:
            gather_wait("0c", z)
        xs, y2 = _mm("ffn_down", [z], [(wl["w2"], 0)], [F32, BF16], _ep_residual, extras=[xs, g2], extra_kinds=["tile", "row"],
                     tn=512, tk=ffn_w2.shape[1] * N_CHIP)
        if l + 1 < depth:
            gather_wait(l + 1, xs)
        st.update(h2=h2, u=u, t=t, z=z, y2=y2)
        saved.append(st)

    def below_of(l, which):
        if which == "ffn":
            return saved[l]["y2"], mod[l, n_mod - 1]
        return saved[l]["y"], (mod[l, 2] if l % 2 == 0 else saved[l]["gate"])

    dx, fin_acc, loss_blk, dyb, gacc = _final_loss_bwd(xs, final_g[None], target, tm_row, below_of(depth - 1, "ffn"))
    loss = lax.psum(loss_blk[0, 0], MESH_AXES)
    d_final_g = fin_acc[0]
    dmod_rows = [None] * depth
    d_norm_mix, d_norm_ffn = [None] * depth, [None] * depth
    even_small = [None] * n_even
    d_pool_scale = [None] * (depth // 2)
    grad_names = ["w_in", "w_out", "pool", "w1", "w3", "w2"]
    g_shapes = {"w_in": ab_w_in.shape, "w_out": ab_w_out.shape, "pool": (pool_w.shape[0], n_pool * pool_w.shape[2], pool_w.shape[3]),
                "w1": ffn_w1.shape, "w3": ffn_w3.shape, "w2": ffn_w2.shape}
    shard_grads = {nm: None for nm in grad_names}
    pipe = {"pair": None, "chip": None, "prev": None, "token": None}

    def finish_chip(after):
        tag, pl_, names_, kinds_, sends, recvs, cps_f, slabs_f = pipe["chip"]
        cps_d, slabs_d = _reduce_wait(f"reduce_wait_{tag}", sends, recvs, cps_f, slabs_f, kinds_, after)
        for nm, kind, cp, sl in zip(names_, kinds_, cps_d, slabs_d):
            rpg = pool_w.shape[2] if nm == "pool" else None
            cp2 = cp.reshape(-1, cp.shape[-1])
            shard_grads[nm] = _chip_sum_into(sl.reshape(N_CHIP, -1, sl.shape[-1]), cp2, shard_grads[nm], g_shapes[nm],
                                             layer_span(nm, pl_)[0] // (n_pool if nm == "pool" else 1), kind, sc_vec, rpg)
        pipe["chip"] = None
        pipe["prev"] = slabs_d[-1]

    def settle(after):
        made = []
        if pipe["pair"] is not None:
            tag, lyr, names, kinds_l, sends, recvs, dws_f, sibs_f = pipe["pair"]
            dws_d, sibs_d = _pair_wait(f"pair_wait_{tag}", sends, recvs, dws_f, sibs_f, kinds_l, after)
            cps = []
            for dw, sib, kind in zip(dws_d, sibs_d, kinds_l):
                cp = _pair_add(dw.reshape(-1, dw.shape[-1]), sib.reshape(-1, sib.shape[-1]), kind, c_vec)
                cps.append(cp.reshape(sib.shape))
            pipe["pair"] = None
            pipe["ready"] = (tag, lyr, names, kinds_l, cps)
            made.append(cps[-1])
        if pipe["chip"] is not None:
            finish_chip(after)
            made.append(pipe["prev"])
        return made

    def advance(after, new=None):
        settle(after)
        if pipe.get("ready") is not None:
            tag, lyr, names, kinds_l, cps = pipe.pop("ready")
            prev = cps[-1] if pipe["prev"] is None else pipe["prev"]
            sends, recvs, cps_f, slabs_f, token = _reduce_start(f"reduce_start_{tag}", cps, kinds_l, prev)
            pipe["chip"] = (tag, lyr, names, kinds_l, sends, recvs, cps_f, slabs_f)
            pipe["token"] = token
        if new is not None:
            tag, lyr, names, big_ = new
            kinds_l = [shard_of[nm][1] for nm in names]
            dws = [big_[nm] for nm in names]
            prev = dws[-1] if pipe["token"] is None else pipe["token"]
            sends, recvs, dws_f, sibs_f, token = _pair_start(f"pair_start_{tag}", dws, kinds_l, prev)
            pipe["pair"] = (tag, lyr, names, kinds_l, sends, recvs, dws_f, sibs_f)
            pipe["token"] = token

    def behind(row):
        return row + pipe["token"][0:1, 0:1]

    for l in reversed(range(depth)):
        sh1, sc1, g1, sh2, sc2, g2 = [mod[l, k] for k in range(n_mod)]
        st = saved[l]
        wl = layer_w[l]
        i = l // 2
        big = {}
        d_g2 = gacc[0]
        du, dt = _mm("ffn_dz", [dyb], [(wl["w2"], 0)], [BF16, BF16], _ep_swiglu_bwd, trans_b=True, extras=[st["u"], st["t"]],
                     extra_kinds=["tile", "tile"], tm=2048, tn=512, tk=2048)
        big["w2"] = _mm("ffn_dw2", [st["z"]], [dyb], [BF16], _ep_store, trans_a=True, tm=512, tn=512, tk=s)[0]
        big["w1"] = _mm("ffn_dw1", [st["h2"]], [du], [BF16], _ep_store, trans_a=True, tm=512, tn=512, tk=s)[0]
        big["w3"] = _mm("ffn_dw3", [st["h2"]], [dt], [BF16], _ep_store, trans_a=True, tm=512, tn=512, tk=s)[0]
        advance(big["w3"], (f"ffn{l}", l, ["w1", "w3", "w2"], big))
        dh = _mm("ffn_dh", [du, dt], [(wl["w1"], 0), (wl["w3"], 0)], [BF16], _ep_sum, trans_b=True, tn=512, tk=ffn_w2.shape[1] * 2)[0]
        dx, nacc, dyb, gacc = _norm_mod_bwd(dh, st["x2"], dx, behind(norm_ffn_g[l][None]), sc2, tm_row, below_of(l, "mix"))
        d_sh2, d_sc2, d_norm_ffn[l] = nacc[0], nacc[1], nacc[2]
        if l % 2 == 0:
            d_g1 = gacc[0]
            big["w_out"] = _mm("ab_dw_out", [st["cat"]], [dyb], [BF16], _ep_store, trans_a=True, tm=512, tn=512, tk=s)[0]
            dcat = _mm("ab_dcat", [dyb], [(wl["w_out"], 0)], [BF16], _ep_store, trans_b=True, tk=2048)[0]
            advance(dcat)
            dproj, dcw, vecs, dws, dbias = _ab_mid_bwd(dcat, st["proj"], st["a1"], conv_w_full[i], behind(a_norm_g[i][None]),
                                                       a_norm_b[i][None], b_norm_g[i][None], b_norm_b[i][None], w_c[i],
                                                       bias_full[i], tm_row)
            even_small[i] = dict(conv_w=dcw, a_norm_g=vecs[0, :da], a_norm_b=vecs[1, :da], conv_b=vecs[2, :da],
                                 b_norm_g=vecs[3, :db], b_norm_b=vecs[4, :db], w_s=dws, bias=dbias[:, :, 0])
            big["w_in"] = _mm("ab_dw_in", [st["h"]], [dproj], [BF16], _ep_store, trans_a=True, tm=512, tn=512, tk=s)[0]
            advance(big["w_in"], (f"mix{l}", l, ["w_in", "w_out"], big))
            dh = _mm("ab_dh", [dproj], [(wl["w_in"], 0)], [BF16], _ep_store, trans_b=True, tn=512, tk=2 * da + 2 * db)[0]
            below = below_of(l - 1, "ffn") if l > 0 else None
            outs = _norm_mod_bwd(dh, st["x1"], dx, behind(norm_mix_g[l][None]), sc1, tm_row, below)
            dx, nacc = outs[0], outs[1]
            if below is not None:
                dyb, gacc = outs[2], outs[3]
        else:
            d_g1 = gacc[0] * pool_scale_full[i]
            d_pool_scale[i] = gacc[0] * g1[0]
            big["pool"] = _grouped_dw(st["p"], dyb, n_pool, 1024)
            advance(big["pool"], (f"mix{l}", l, ["pool"], big))
            dp = _grouped_dx(dyb, wl["pool"][None], 0, 1024)
            dx, nacc, dyb, gacc = _pool_bwd(dp, st["x1"], dx, behind(norm_mix_g[l][None]), sc1, tm_row, below_of(l - 1, "ffn"))
        d_sh1, d_sc1, d_norm_mix[l] = nacc[0], nacc[1], nacc[2]
        dmod_rows[l] = jnp.concatenate([d_sh1, d_sc1, d_g1, d_sh2, d_sc2, d_g2])
    grad_x = dx[None]
    settled = settle(dx)

    dmod = jnp.stack(dmod_rows)
    small = [dmod, jnp.stack(d_norm_mix), jnp.stack(d_norm_ffn),
             jnp.stack([e["conv_w"] for e in even_small]), jnp.stack([e["conv_b"] for e in even_small]),
             jnp.stack([e["a_norm_g"] for e in even_small]), jnp.stack([e["a_norm_b"] for e in even_small]),
             jnp.stack([e["b_norm_g"] for e in even_small]), jnp.stack([e["b_norm_b"] for e in even_small]),
             jnp.stack([e["w_s"] for e in even_small]), jnp.stack([e["bias"] for e in even_small]),
             jnp.stack(d_pool_scale), d_final_g]
    small_shapes = [a.shape for a in small]
    width = 1024 if d >= 1024 else 128
    gathered = _allgather_small(_pack(small, width), after=settled)
    summed = _unpack(_sum_leading(gathered), small_shapes)
    (g_ada_b, g_norm_mix, g_norm_ffn, g_conv_w_full, g_conv_b, g_a_norm_g, g_a_norm_b, g_b_norm_g, g_b_norm_b, g_w_s, g_bias,
     g_pool_scale_full, g_final_g) = summed
    cw_cols = a_conv_w.shape[2]
    g_conv_w = lax.dynamic_slice_in_dim(g_conv_w_full, chip * cw_cols, cw_cols, axis=2)
    ps_cols = pool_scale.shape[1]
    g_pool_scale = lax.dynamic_slice_in_dim(g_pool_scale_full, chip * ps_cols, ps_cols, axis=1)

    dmod_all = gathered.reshape(N_DEV, -1)[:, :dmod.size].reshape(N_DEV, depth, n_mod * d)
    dmod_cols = lax.dynamic_slice_in_dim(dmod_all, chip * n_ada, n_ada, axis=2)
    dmod_cols = jnp.pad(jnp.transpose(dmod_cols, (1, 0, 2)), ((0, 0), (0, 8), (0, 0)))
    g_ada_w = _ada_bwd(c_pad, dmod_cols)

    weights = [ada_w, ada_b, norm_mix_g, norm_ffn_g, ab_w_in, a_conv_w, a_conv_b, a_norm_g, a_norm_b, b_norm_g, b_norm_b, b_w_s,
               b_bias, ab_w_out, pool_w, pool_scale, ffn_w1, ffn_w3, ffn_w2, final_g]
    ms = [m_ada_w, m_ada_b, m_norm_mix_g, m_norm_ffn_g, m_ab_w_in, m_a_conv_w, m_a_conv_b, m_a_norm_g, m_a_norm_b, m_b_norm_g,
          m_b_norm_b, m_b_w_s, m_b_bias, m_ab_w_out, m_pool_w, m_pool_scale, m_ffn_w1, m_ffn_w3, m_ffn_w2, m_final_g]
    vs = [v_ada_w, v_ada_b, v_norm_mix_g, v_norm_ffn_g, v_ab_w_in, v_a_conv_w, v_a_conv_b, v_a_norm_g, v_a_norm_b, v_b_norm_g,
          v_b_norm_b, v_b_w_s, v_b_bias, v_ab_w_out, v_pool_w, v_pool_scale, v_ffn_w1, v_ffn_w3, v_ffn_w2, v_final_g]
    grads = [g_ada_w, g_ada_b, g_norm_mix, g_norm_ffn, None, g_conv_w, g_conv_b, g_a_norm_g, g_a_norm_b, g_b_norm_g,
             g_b_norm_b, g_w_s, g_bias, None, None, g_pool_scale, None, None, None, g_final_g]
    updates = [None] * len(weights)

    def update(k):
        grads[k] = grads[k].reshape(weights[k].shape)
        updates[k] = _adamw(weights[k], grads[k], ms[k], vs[k])

    advance(g_ada_w)
    for k in range(1, len(weights)):
        if grads[k] is not None:
            update(k)
    update(0)
    advance(updates[0][0])
    big_at = {"w_in": 4, "w_out": 13, "pool": 14, "w1": 16, "w3": 17, "w2": 18}
    assembled = _pair_assemble([shard_grads[nm] for nm in grad_names], [shard_of[nm][1] for nm in grad_names])
    for nm, g in zip(grad_names, assembled):
        grads[big_at[nm]] = g
        update(big_at[nm])
    return (loss, grad_x, *grads, *[u[0] for u in updates], *[u[1] for u in updates], *[u[2] for u in updates])
```

```python
import functools

import jax
import jax.numpy as jnp
from jax import lax
from jax.experimental import pallas as pl
from jax.experimental.pallas import tpu as pltpu

F32 = jnp.float32
BF16 = jnp.bfloat16
EPS = 1e-6
N_DEV = 8
N_CHIP = 4
MESH_AXES = ("x", "y", "c")
MESH = pl.DeviceIdType.MESH
V7X_VMEM_LIMIT_BYTES = 56 * 1024 * 1024
SUBLANES = 8
CONV_HALO = 32
POOL_HALO = 16
POOL_WINDOWS = (2, 4, 8, 16)
CHUNK = 128
ADAM_LR, ADAM_B1, ADAM_B2, ADAM_EPS, ADAM_WD, ADAM_STEP = 0.001, 0.9, 0.999, 1e-08, 0.01, 10
HBM = pl.BlockSpec(memory_space=pltpu.HBM)


def _params(sem=None):
    return pltpu.CompilerParams(dimension_semantics=sem, vmem_limit_bytes=V7X_VMEM_LIMIT_BYTES)


def _pick(n, prefs):
    for p in prefs:
        if p <= n and n % p == 0:
            return p
    return n


def _row_tile(rows, cols, target):
    if rows * cols <= target:
        return rows
    best = None
    for d in range(16, rows, 16):
        if rows % d == 0 and d * cols <= target:
            best = d
    return best if best is not None else rows


def _sds(shape, dtype):
    return jax.ShapeDtypeStruct(tuple(shape), dtype)


def _mm_core(name, grid, nk, a_list, b_list, a_spec, b_spec, dn, extras, extra_specs, out_shapes, out_specs,
             acc_shape, epilogue):
    n, n_ex, n_out = len(a_list), len(extras), len(out_shapes)

    def body(*refs):
        a_refs, b_refs = refs[:n], refs[n:2 * n]
        ex = refs[2 * n:2 * n + n_ex]
        outs = refs[2 * n + n_ex:2 * n + n_ex + n_out]
        accs = refs[2 * n + n_ex + n_out:]
        ps = [lax.dot_general(a[...], b[...], dn, preferred_element_type=F32) for a, b in zip(a_refs, b_refs)]
        if nk == 1:
            epilogue(ps, ex, outs)
            return
        k = pl.program_id(2)

        @pl.when(k == 0)
        def _():
            for acc, p in zip(accs, ps):
                acc[...] = p

        @pl.when(k > 0)
        def _():
            for acc, p in zip(accs, ps):
                acc[...] += p

        @pl.when(k == nk - 1)
        def _():
            epilogue([acc[...] for acc in accs], ex, outs)

    scratch = [] if nk == 1 else [pltpu.VMEM(acc_shape, F32) for _ in range(n)]
    return pl.pallas_call(
        body, grid=grid, name=name,
        in_specs=[a_spec] * n + [b_spec] * n + list(extra_specs),
        out_specs=list(out_specs), out_shape=list(out_shapes), scratch_shapes=scratch,
        compiler_params=_params(("parallel", "parallel", "arbitrary")),
    )(*a_list, *b_list, *extras)


def _mm(name, a_list, b_list, out_dtypes, epilogue, *, trans_a=False, trans_b=False, extras=(), extra_kinds=(),
        tm=1024, tn=1024, tk=1024):
    layer = None
    if isinstance(b_list[0], tuple):
        layer = b_list[0][1]
        b_list = [b for b, _ in b_list]
    a0 = a_list[0]
    b_shape = b_list[0].shape[-2:]
    m, kk = (a0.shape[1], a0.shape[0]) if trans_a else a0.shape
    nn = b_shape[0] if trans_b else b_shape[1]
    tm, tn, tk = _pick(m, (tm, 512, 256, 128)), _pick(nn, (tn, 512, 256, 128)), _pick(kk, (tk, 512, 256, 128))
    nk = kk // tk
    a_spec = pl.BlockSpec((tk, tm), lambda i, j, k: (k, i)) if trans_a else pl.BlockSpec((tm, tk), lambda i, j, k: (i, k))
    if layer is None:
        b_spec = pl.BlockSpec((tn, tk), lambda i, j, k: (j, k)) if trans_b else pl.BlockSpec((tk, tn), lambda i, j, k: (k, j))
    elif trans_b:
        b_spec = pl.BlockSpec((None, tn, tk), lambda i, j, k: (layer, j, k))
    else:
        b_spec = pl.BlockSpec((None, tk, tn), lambda i, j, k: (layer, k, j))
    dn = (((0 if trans_a else 1,), (1 if trans_b else 0,)), ((), ()))
    tile = pl.BlockSpec((tm, tn), lambda i, j, k: (i, j))
    row = pl.BlockSpec((1, tn), lambda i, j, k: (0, j))
    return _mm_core(name, (m // tm, nn // tn, nk), nk, a_list, b_list, a_spec, b_spec, dn, extras,
                    [tile if kd == "tile" else row for kd in extra_kinds],
                    [_sds((m, nn), dt) for dt in out_dtypes], [tile] * len(out_dtypes), (tm, tn), epilogue)


def _ep_store(ps, ex, outs):
    outs[0][...] = ps[0].astype(outs[0].dtype)


def _ep_sum(ps, ex, outs):
    outs[0][...] = (ps[0] + ps[1]).astype(outs[0].dtype)


def _ep_swiglu(ps, ex, outs):
    u, t = ps
    sg = jax.nn.sigmoid(u)
    su = u * sg
    outs[0][...] = (t * (sg * (1.0 + u * (1.0 - sg)))).astype(BF16)
    outs[1][...] = su.astype(BF16)
    outs[2][...] = (su * t).astype(BF16)


def _ep_residual(ps, ex, outs):
    x_ref, gate_ref = ex
    y = ps[0]
    outs[0][...] = x_ref[...] + gate_ref[...] * y
    outs[1][...] = y.astype(BF16)


def _ep_swiglu_bwd(ps, ex, outs):
    dz = ps[0]
    outs[0][...] = (dz * ex[0][...].astype(F32)).astype(BF16)
    outs[1][...] = (dz * ex[1][...].astype(F32)).astype(BF16)


def _grouped_fwd(p, w, layer, x, gate, tm):
    s, d = p.shape
    _, g, kg, ng = w.shape
    tm = _pick(s, (tm, 512, 256, 128))
    tile_a = pl.BlockSpec((tm, kg), lambda i, j, k: (i, j))
    tile_o = pl.BlockSpec((tm, ng), lambda i, j, k: (i, j))
    return _mm_core("pool_mm_fwd", (s // tm, g, 1), 1, [p], [w], tile_a,
                    pl.BlockSpec((None, None, kg, ng), lambda i, j, k: (layer, j, 0, 0)), (((1,), (0,)), ((), ())),
                    [x, gate], [tile_o, pl.BlockSpec((1, ng), lambda i, j, k: (0, j))],
                    [_sds((s, g * ng), F32), _sds((s, g * ng), BF16)], [tile_o, tile_o], None, _ep_residual)


def _grouped_dx(dy, w, layer, tm):
    s, _ = dy.shape
    _, g, kg, ng = w.shape
    tm = _pick(s, (tm, 512, 256, 128))
    return _mm_core("pool_mm_dx", (s // tm, g, 1), 1, [dy], [w], pl.BlockSpec((tm, ng), lambda i, j, k: (i, j)),
                    pl.BlockSpec((None, None, kg, ng), lambda i, j, k: (layer, j, 0, 0)), (((1,), (1,)), ((), ())), [], [],
                    [_sds((s, g * kg), BF16)], [pl.BlockSpec((tm, kg), lambda i, j, k: (i, j))], None, _ep_store)[0]


def _grouped_dw(p, dy, groups, tk):
    s, d = p.shape
    kg = d // groups
    ng = dy.shape[1] // groups
    tk = _pick(s, (tk, 512, 256, 128))
    nk = s // tk

    def ep(ps, ex, outs):
        outs[0][...] = ps[0].astype(BF16)

    return _mm_core("pool_mm_dw", (groups, 1, nk), nk, [p], [dy], pl.BlockSpec((tk, kg), lambda i, j, k: (k, i)),
                    pl.BlockSpec((tk, ng), lambda i, j, k: (k, i)), (((0,), (0,)), ((), ())), [], [],
                    [_sds((groups, kg, ng), BF16)], [pl.BlockSpec((None, kg, ng), lambda i, j, k: (i, 0, 0))],
                    (kg, ng), ep)[0]


def _rms_rstd(xv):
    return lax.rsqrt(jnp.mean(xv * xv, axis=-1, keepdims=True) + EPS)


def _norm_mod_math(xv, g, sh, sc):
    return ((xv * _rms_rstd(xv)) * g) * (1.0 + sc) + sh


def _norm_mod_bwd_math(dh, xv, dxo, g, sc, acc_ref):
    r = _rms_rstd(xv)
    xhat = xv * r
    acc_ref[0:1, :] += jnp.sum(dh, axis=0, keepdims=True)
    acc_ref[1:2, :] += jnp.sum(dh * (xhat * g), axis=0, keepdims=True)
    dhn = dh * (1.0 + sc)
    acc_ref[2:3, :] += jnp.sum(dhn * xhat, axis=0, keepdims=True)
    dxh = dhn * g
    return dxo + r * (dxh - xhat * jnp.mean(dxh * xhat, axis=-1, keepdims=True))


def _vec_spec(d):
    return pl.BlockSpec((1, d), lambda i: (0, 0))


def _acc_spec(d):
    return pl.BlockSpec((8, d), lambda i: (0, 0))


def _norm_mod(x, g, sh, sc, tm):
    s, d = x.shape

    def body(x_ref, g_ref, sh_ref, sc_ref, h_ref):
        h_ref[...] = _norm_mod_math(x_ref[...], g_ref[...], sh_ref[...], sc_ref[...]).astype(BF16)

    row = pl.BlockSpec((tm, d), lambda i: (i, 0))
    return pl.pallas_call(body, grid=(s // tm,), name="norm_mod", in_specs=[row] + [_vec_spec(d)] * 3, out_specs=row,
                          out_shape=_sds((s, d), BF16), compiler_params=_params(("parallel",)))(x, g, sh, sc)


def _gate_step(dxv, y_ref, gate_ref, dy_ref, gacc_ref):
    dy_ref[...] = (dxv * gate_ref[...]).astype(BF16)
    gacc_ref[0:1, :] += jnp.sum(dxv * y_ref[...].astype(F32), axis=0, keepdims=True)


def _norm_mod_bwd(dh, x, dxo, g, sc, tm, below=None):
    s, d = x.shape

    def body(dh_ref, x_ref, dxo_ref, g_ref, sc_ref, *rest):
        if below is None:
            dx_ref, acc_ref = rest
        else:
            y_ref, gate_ref, dx_ref, acc_ref, dy_ref, gacc_ref = rest

        @pl.when(pl.program_id(0) == 0)
        def _():
            acc_ref[...] = jnp.zeros_like(acc_ref)
            if below is not None:
                gacc_ref[...] = jnp.zeros_like(gacc_ref)

        dxv = _norm_mod_bwd_math(dh_ref[...].astype(F32), x_ref[...], dxo_ref[...], g_ref[...], sc_ref[...], acc_ref)
        dx_ref[...] = dxv
        if below is not None:
            _gate_step(dxv, y_ref, gate_ref, dy_ref, gacc_ref)

    row = pl.BlockSpec((tm, d), lambda i: (i, 0))
    extra_in = [] if below is None else [row, _vec_spec(d)]
    extra_out = [] if below is None else [row, _acc_spec(d)]
    extra_shape = [] if below is None else [_sds((s, d), BF16), _sds((8, d), F32)]
    return pl.pallas_call(body, grid=(s // tm,), name="norm_mod_bwd", in_specs=[row, row, row, _vec_spec(d), _vec_spec(d)] + extra_in,
                          out_specs=[row, _acc_spec(d)] + extra_out, out_shape=[_sds((s, d), F32), _sds((8, d), F32)] + extra_shape,
                          compiler_params=_params(("arbitrary",)))(dh, x, dxo, g, sc, *([] if below is None else below))


def _gate_bwd(dx, y, gate, tm):
    s, d = dx.shape

    def body(dx_ref, y_ref, gate_ref, dy_ref, acc_ref):
        @pl.when(pl.program_id(0) == 0)
        def _():
            acc_ref[...] = jnp.zeros_like(acc_ref)

        dxv = dx_ref[...]
        dy_ref[...] = (dxv * gate_ref[...]).astype(BF16)
        acc_ref[0:1, :] += jnp.sum(dxv * y_ref[...].astype(F32), axis=0, keepdims=True)

    row = pl.BlockSpec((tm, d), lambda i: (i, 0))
    return pl.pallas_call(body, grid=(s // tm,), name="gate_bwd", in_specs=[row, row, _vec_spec(d)],
                          out_specs=[row, _acc_spec(d)], out_shape=[_sds((s, d), BF16), _sds((8, d), F32)],
                          compiler_params=_params(("arbitrary",)))(dx, y, gate)


def _final_loss_bwd(x, g, target, tm, below):
    s, d = x.shape

    def body(x_ref, g_ref, t_ref, y_ref, gate_ref, dx_ref, acc_ref, loss_ref, dy_ref, gacc_ref):
        @pl.when(pl.program_id(0) == 0)
        def _():
            acc_ref[...] = jnp.zeros_like(acc_ref)
            loss_ref[...] = jnp.zeros_like(loss_ref)
            gacc_ref[...] = jnp.zeros_like(gacc_ref)

        xv = x_ref[...]
        gv = g_ref[...]
        r = _rms_rstd(xv)
        xhat = xv * r
        err = xhat * gv - t_ref[...]
        loss_ref[...] += (0.5 / d) * jnp.sum(err * err)
        dy = err * (1.0 / d)
        acc_ref[0:1, :] += jnp.sum(dy * xhat, axis=0, keepdims=True)
        dxh = dy * gv
        dxv = r * (dxh - xhat * jnp.mean(dxh * xhat, axis=-1, keepdims=True))
        dx_ref[...] = dxv
        _gate_step(dxv, y_ref, gate_ref, dy_ref, gacc_ref)

    row = pl.BlockSpec((tm, d), lambda i: (i, 0))
    return pl.pallas_call(body, grid=(s // tm,), name="final_loss_bwd", in_specs=[row, _vec_spec(d), row, row, _vec_spec(d)],
                          out_specs=[row, _acc_spec(d), pl.BlockSpec((8, 128), lambda i: (0, 0)), row, _acc_spec(d)],
                          out_shape=[_sds((s, d), F32), _sds((8, d), F32), _sds((8, 128), F32), _sds((s, d), BF16), _sds((8, d), F32)],
                          compiler_params=_params(("arbitrary",)))(x, g, target, *below)


def _chunks(tm, width, rb, cb):
    rb, cb = min(rb, tm), min(cb, width)
    return [(r0, c0, rb, cb) for r0 in range(0, tm, rb) for c0 in range(0, width, cb)]


def _prev_halo_map(tm, halo):
    return lambda i: (jnp.maximum(i * (tm // halo) - 1, 0), 0)


def _next_halo_map(tm, halo, s):
    return lambda i: (jnp.minimum((i + 1) * (tm // halo), s // halo - 1), 0)


def _pool_fwd(x, g, sh, sc, tm):
    s, d = x.shape
    dg = d // len(POOL_WINDOWS)

    def body(x_ref, xh_ref, g_ref, sh_ref, sc_ref, p_ref, ext_ref):
        i = pl.program_id(0)
        gv, shv, scv = g_ref[...], sh_ref[...], sc_ref[...]
        ext_ref[POOL_HALO:, :] = _norm_mod_math(x_ref[...], gv, shv, scv)
        ext_ref[0:POOL_HALO, :] = jnp.where(i == 0, 0.0, _norm_mod_math(xh_ref[...], gv, shv, scv))
        for gi, w in enumerate(POOL_WINDOWS):
            for r0, c0, rb, cb in _chunks(tm, dg, 64, 256):
                cols = pl.ds(gi * dg + c0, cb)
                tok = ext_ref[pl.ds(POOL_HALO + r0, rb), cols]
                acc = tok
                for j in range(1, w):
                    acc = acc + ext_ref[pl.ds(POOL_HALO + r0 - j, rb), cols]
                t_glob = i * tm + r0 + lax.broadcasted_iota(jnp.int32, (rb, 1), 0)
                cnt = jnp.minimum(t_glob + 1, w).astype(F32)
                p_ref[pl.ds(r0, rb), cols] = (acc / cnt - tok).astype(BF16)

    row = pl.BlockSpec((tm, d), lambda i: (i, 0))
    halo = pl.BlockSpec((POOL_HALO, d), _prev_halo_map(tm, POOL_HALO))
    return pl.pallas_call(body, grid=(s // tm,), name="pool_fwd", in_specs=[row, halo] + [_vec_spec(d)] * 3, out_specs=row,
                          out_shape=_sds((s, d), BF16), scratch_shapes=[pltpu.VMEM((tm + POOL_HALO, d), F32)],
                          compiler_params=_params(("parallel",)))(x, x, g, sh, sc)


def _pool_bwd(dp, x, dxo, g, sc, tm, below):
    s, d = x.shape
    dg = d // len(POOL_WINDOWS)
    n_tiles = s // tm

    def body(dp_ref, dph_ref, x_ref, dxo_ref, g_ref, sc_ref, y_ref, gate_ref, dx_ref, acc_ref, dy_ref, gacc_ref, ext_ref, dh_ref):
        i = pl.program_id(0)

        @pl.when(i == 0)
        def _():
            acc_ref[...] = jnp.zeros_like(acc_ref)
            gacc_ref[...] = jnp.zeros_like(gacc_ref)

        for gi, w in enumerate(POOL_WINDOWS):
            cols = pl.ds(gi * dg, dg)
            t_main = i * tm + lax.broadcasted_iota(jnp.int32, (tm, 1), 0)
            ext_ref[0:tm, cols] = dp_ref[:, cols].astype(F32) / jnp.minimum(t_main + 1, w).astype(F32)
            ext_ref[tm:, cols] = jnp.where(i == n_tiles - 1, 0.0, dph_ref[:, cols].astype(F32) * (1.0 / w))
            for r0, c0, rb, cb in _chunks(tm, dg, 64, 256):
                cc = pl.ds(gi * dg + c0, cb)
                acc = ext_ref[pl.ds(r0, rb), cc]
                for j in range(1, w):
                    acc = acc + ext_ref[pl.ds(r0 + j, rb), cc]
                dh_ref[pl.ds(r0, rb), cc] = acc - dp_ref[pl.ds(r0, rb), cc].astype(F32)
        dxv = _norm_mod_bwd_math(dh_ref[...], x_ref[...], dxo_ref[...], g_ref[...], sc_ref[...], acc_ref)
        dx_ref[...] = dxv
        _gate_step(dxv, y_ref, gate_ref, dy_ref, gacc_ref)

    row = pl.BlockSpec((tm, d), lambda i: (i, 0))
    halo = pl.BlockSpec((POOL_HALO, d), _next_halo_map(tm, POOL_HALO, s))
    return pl.pallas_call(body, grid=(n_tiles,), name="pool_bwd",
                          in_specs=[row, halo, row, row, _vec_spec(d), _vec_spec(d), row, _vec_spec(d)],
                          out_specs=[row, _acc_spec(d), row, _acc_spec(d)],
                          out_shape=[_sds((s, d), F32), _sds((8, d), F32), _sds((s, d), BF16), _sds((8, d), F32)],
                          scratch_shapes=[pltpu.VMEM((tm + POOL_HALO, d), F32), pltpu.VMEM((tm, d), F32)],
                          compiler_params=_params(("arbitrary",)))(dp, dp, x, dxo, g, sc, *below)


def _layernorm_fwd(v, g, b):
    mu = jnp.mean(v, axis=-1, keepdims=True)
    xc = v - mu
    rstd = lax.rsqrt(jnp.mean(xc * xc, axis=-1, keepdims=True) + EPS)
    yn = xc * rstd
    return yn * g + b, yn, rstd


def _layernorm_bwd(dz, yn, rstd, g):
    dyn = dz * g
    dv = rstd * (dyn - jnp.mean(dyn, axis=-1, keepdims=True) - yn * jnp.mean(dyn * yn, axis=-1, keepdims=True))
    return dv, jnp.sum(dz * yn, axis=0, keepdims=True), jnp.sum(dz, axis=0, keepdims=True)


def _fill_shifted(ext_ref, sh_ref):
    n = ext_ref.shape[0]
    for r in range(1, SUBLANES):
        sh_ref[r - 1, 0:n - SUBLANES, :] = ext_ref[pl.ds(r, n - SUBLANES), :]


def _shifted(ext_ref, sh_ref, start, rows, cols):
    q, r = divmod(start, SUBLANES)
    if r == 0:
        return ext_ref[pl.ds(start, rows), cols]
    return sh_ref[r - 1, pl.ds(q * SUBLANES, rows), cols]


def _ab_mid_fwd(proj, conv_w, conv_b, a_g, a_b, v_g, v_b, w_c, bias_full, tm):
    s = proj.shape[0]
    da = conv_b.shape[1]
    db = v_g.shape[1]
    nh = w_c.shape[0]
    kw = conv_w.shape[0]
    lead = CONV_HALO - (kw - 1)

    def body(p_ref, ph_ref, cw_ref, cb_ref, ag_ref, ab_ref, vg_ref, vb_ref, wc_ref, bias_ref, cat_ref, a1_ref, ext_ref,
             a1s_ref, sh_ref):
        i = pl.program_id(0)
        val = p_ref[:, 0:da].astype(F32)
        gat = p_ref[:, da:2 * da].astype(F32)
        ext_ref[CONV_HALO:, :] = val * jax.nn.sigmoid(gat)
        hv = ph_ref[:, 0:da].astype(F32)
        hg = ph_ref[:, da:2 * da].astype(F32)
        ext_ref[0:CONV_HALO, :] = jnp.where(i == 0, 0.0, hv * jax.nn.sigmoid(hg))
        _fill_shifted(ext_ref, sh_ref)
        for r0, c0, rb, cb in _chunks(tm, da, 64, 256):
            cols = pl.ds(c0, cb)
            acc = jnp.broadcast_to(cb_ref[:, cols], (rb, cb))
            for k in range(kw):
                acc = acc + cw_ref[k:k + 1, cols] * _shifted(ext_ref, sh_ref, r0 + lead + k, rb, cols)
            a1s_ref[pl.ds(r0, rb), cols] = acc
        a1 = a1s_ref[...]
        a1_ref[...] = a1.astype(BF16)
        z, _, _ = _layernorm_fwd(a1, ag_ref[...], ab_ref[...])
        cat_ref[:, 0:da] = (z * jax.nn.sigmoid(z)).astype(BF16)

        bu = p_ref[:, 2 * da:2 * da + db].astype(F32)
        bv = p_ref[:, 2 * da + db:].astype(F32)
        vn, _, _ = _layernorm_fwd(bv, vg_ref[...], vb_ref[...])
        vnb = vn.astype(BF16)
        for n in range(tm // CHUNK):
            rows = slice(n * CHUNK, (n + 1) * CHUNK)
            for h in range(nh):
                hc = slice(h * CHUNK, (h + 1) * CHUNK)
                vo = jnp.dot(wc_ref[h], vnb[rows, hc], preferred_element_type=F32) + bias_ref[:, hc]
                cat_ref[rows, da + h * CHUNK:da + (h + 1) * CHUNK] = (bu[rows, hc] * vo).astype(BF16)

    full = lambda a: pl.BlockSpec(a.shape, lambda i: (0,) * a.ndim)
    return pl.pallas_call(
        body, grid=(s // tm,), name="ab_mid_fwd",
        in_specs=[pl.BlockSpec((tm, 2 * da + 2 * db), lambda i: (i, 0)),
                  pl.BlockSpec((CONV_HALO, 2 * da), _prev_halo_map(tm, CONV_HALO)),
                  full(conv_w), full(conv_b), full(a_g), full(a_b), full(v_g), full(v_b), full(w_c), full(bias_full)],
        out_specs=[pl.BlockSpec((tm, da + db), lambda i: (i, 0)), pl.BlockSpec((tm, da), lambda i: (i, 0))],
        out_shape=[_sds((s, da + db), BF16), _sds((s, da), BF16)],
        scratch_shapes=[pltpu.VMEM((tm + CONV_HALO, da), F32), pltpu.VMEM((tm, da), F32),
                        pltpu.VMEM((SUBLANES - 1, tm + CONV_HALO, da), F32)],
        compiler_params=_params(("parallel",)),
    )(proj, proj, conv_w, conv_b, a_g, a_b, v_g, v_b, w_c, bias_full)


def _ab_mid_bwd(dcat, proj, a1, conv_w, a_g, a_b, v_g, v_b, w_c, bias_full, tm):
    s = proj.shape[0]
    da = a_g.shape[1]
    db = v_g.shape[1]
    nh = w_c.shape[0]
    kw = conv_w.shape[0]
    lead = CONV_HALO - (kw - 1)
    n_tiles = s // tm

    def body(dc_ref, dch_ref, p_ref, ph_ref, a1_ref, a1h_ref, cw_ref, ag_ref, ab_ref, vg_ref, vb_ref, wc_ref, bias_ref,
             dp_ref, dcw_ref, vec_ref, dws_ref, dbias_ref, ext_ref, dext_ref, dcw_acc, dvn_ref, sh_ref, dsh_ref):
        i = pl.program_id(0)

        @pl.when(i == 0)
        def _():
            dcw_acc[...] = jnp.zeros_like(dcw_acc)
            vec_ref[...] = jnp.zeros_like(vec_ref)
            dws_ref[...] = jnp.zeros_like(dws_ref)
            dbias_ref[...] = jnp.zeros_like(dbias_ref)

        agv, abv = ag_ref[...], ab_ref[...]

        def silu_ln_bwd(a1v, d_a2):
            z, yn, rstd = _layernorm_fwd(a1v, agv, abv)
            sg = jax.nn.sigmoid(z)
            return _layernorm_bwd(d_a2 * (sg * (1.0 + z * (1.0 - sg))), yn, rstd, agv)

        d_a1, dga, dba = silu_ln_bwd(a1_ref[...].astype(F32), dc_ref[:, 0:da].astype(F32))
        dext_ref[0:tm, :] = d_a1
        d_a1h, _, _ = silu_ln_bwd(a1h_ref[...].astype(F32), dch_ref[...].astype(F32))
        dext_ref[tm:, :] = jnp.where(i == n_tiles - 1, 0.0, d_a1h)
        vec_ref[0:1, 0:da] += dga
        vec_ref[1:2, 0:da] += dba
        vec_ref[2:3, 0:da] += jnp.sum(d_a1, axis=0, keepdims=True)

        val = p_ref[:, 0:da].astype(F32)
        sgg = jax.nn.sigmoid(p_ref[:, da:2 * da].astype(F32))
        ext_ref[CONV_HALO:, :] = val * sgg
        hv = ph_ref[:, 0:da].astype(F32)
        hg = ph_ref[:, da:2 * da].astype(F32)
        ext_ref[0:CONV_HALO, :] = jnp.where(i == 0, 0.0, hv * jax.nn.sigmoid(hg))

        _fill_shifted(ext_ref, sh_ref)
        _fill_shifted(dext_ref, dsh_ref)
        for r0, c0, rb, cb in _chunks(tm, da, 64, 256):
            cols = pl.ds(c0, cb)
            rows = pl.ds(r0, rb)
            d1 = dext_ref[rows, cols]
            acc = jnp.zeros((rb, cb), F32)
            for k in range(kw):
                acc = acc + cw_ref[k:k + 1, cols] * _shifted(dext_ref, dsh_ref, r0 + (kw - 1) - k, rb, cols)
                prod = d1 * _shifted(ext_ref, sh_ref, r0 + lead + k, rb, cols)
                dcw_acc[k, :, cols] += jnp.sum(prod.reshape(rb // SUBLANES, SUBLANES, cb), axis=0)
            v = p_ref[rows, pl.ds(c0, cb)].astype(F32)
            sg = jax.nn.sigmoid(p_ref[rows, pl.ds(da + c0, cb)].astype(F32))
            dp_ref[rows, pl.ds(c0, cb)] = (acc * sg).astype(BF16)
            dp_ref[rows, pl.ds(da + c0, cb)] = (acc * v * sg * (1.0 - sg)).astype(BF16)

        vgv = vg_ref[...]
        bu = p_ref[:, 2 * da:2 * da + db].astype(F32)
        bv = p_ref[:, 2 * da + db:].astype(F32)
        vn, yn_v, rstd_v = _layernorm_fwd(bv, vgv, vb_ref[...])
        vnb = vn.astype(BF16)
        for n in range(tm // CHUNK):
            rows = slice(n * CHUNK, (n + 1) * CHUNK)
            for h in range(nh):
                hc = slice(h * CHUNK, (h + 1) * CHUNK)
                wch = wc_ref[h]
                blk = vnb[rows, hc]
                vo = jnp.dot(wch, blk, preferred_element_type=F32) + bias_ref[:, hc]
                d_bout = dc_ref[rows, da + h * CHUNK:da + (h + 1) * CHUNK].astype(F32)
                dp_ref[rows, 2 * da + h * CHUNK:2 * da + (h + 1) * CHUNK] = (d_bout * vo).astype(BF16)
                d_vo = d_bout * bu[rows, hc]
                dbias_ref[h] += jnp.sum(d_vo, axis=1, keepdims=True)
                d_vob = d_vo.astype(BF16)
                dws_ref[h] += lax.dot_general(d_vob, blk, (((1,), (1,)), ((), ())), preferred_element_type=F32)
                dvn_ref[rows, hc] = lax.dot_general(wch, d_vob, (((0,), (0,)), ((), ())), preferred_element_type=F32)
        d_bv, dgv, dbv = _layernorm_bwd(dvn_ref[...], yn_v, rstd_v, vgv)
        dp_ref[:, 2 * da + db:] = d_bv.astype(BF16)
        vec_ref[3:4, 0:db] += dgv
        vec_ref[4:5, 0:db] += dbv

        @pl.when(i == n_tiles - 1)
        def _():
            dcw_ref[...] = jnp.sum(dcw_acc[...], axis=1)
            causal = lax.broadcasted_iota(jnp.int32, (CHUNK, CHUNK), 0) >= lax.broadcasted_iota(jnp.int32, (CHUNK, CHUNK), 1)
            for h in range(nh):
                dws_ref[h] = jnp.where(causal, dws_ref[h], 0.0)

    full = lambda a: pl.BlockSpec(a.shape, lambda i: (0,) * a.ndim)
    wide = max(da, db)
    return pl.pallas_call(
        body, grid=(n_tiles,), name="ab_mid_bwd",
        in_specs=[pl.BlockSpec((tm, da + db), lambda i: (i, 0)),
                  pl.BlockSpec((CONV_HALO, da), _next_halo_map(tm, CONV_HALO, s)),
                  pl.BlockSpec((tm, 2 * da + 2 * db), lambda i: (i, 0)),
                  pl.BlockSpec((CONV_HALO, 2 * da), _prev_halo_map(tm, CONV_HALO)),
                  pl.BlockSpec((tm, da), lambda i: (i, 0)),
                  pl.BlockSpec((CONV_HALO, da), _next_halo_map(tm, CONV_HALO, s)),
                  full(conv_w), full(a_g), full(a_b), full(v_g), full(v_b), full(w_c), full(bias_full)],
        out_specs=[pl.BlockSpec((tm, 2 * da + 2 * db), lambda i: (i, 0)),
                   pl.BlockSpec((kw, da), lambda i: (0, 0)),
                   pl.BlockSpec((8, wide), lambda i: (0, 0)),
                   pl.BlockSpec((nh, CHUNK, CHUNK), lambda i: (0, 0, 0)),
                   pl.BlockSpec((nh, CHUNK, 1), lambda i: (0, 0, 0))],
        out_shape=[_sds((s, 2 * da + 2 * db), BF16), _sds((kw, da), F32), _sds((8, wide), F32),
                   _sds((nh, CHUNK, CHUNK), F32), _sds((nh, CHUNK, 1), F32)],
        scratch_shapes=[pltpu.VMEM((tm + CONV_HALO, da), F32), pltpu.VMEM((tm + CONV_HALO, da), F32),
                        pltpu.VMEM((kw, SUBLANES, da), F32), pltpu.VMEM((tm, db), F32),
                        pltpu.VMEM((SUBLANES - 1, tm + CONV_HALO, da), F32), pltpu.VMEM((SUBLANES - 1, tm + CONV_HALO, da), F32)],
        compiler_params=_params(("arbitrary",)),
    )(dcat, dcat, proj, proj, a1, a1, conv_w, a_g, a_b, v_g, v_b, w_c, bias_full)


def _ada_fwd(c_all, w, b):
    nl, d, n = w.shape
    tn = _pick(n, (512, 256, 128))

    def body(c_ref, w_ref, b_ref, o_ref):
        cv = c_ref[...]
        cond = (cv * jax.nn.sigmoid(cv)).astype(BF16)
        o_ref[...] = jnp.dot(cond, w_ref[...].astype(BF16), preferred_element_type=F32) + b_ref[...]

    return pl.pallas_call(
        body, grid=(nl, n // tn), name="ada_fwd",
        in_specs=[pl.BlockSpec(c_all.shape, lambda l, j: (0, 0)), pl.BlockSpec((None, d, tn), lambda l, j: (l, 0, j)),
                  pl.BlockSpec((None, 1, tn), lambda l, j: (l, 0, j))],
        out_specs=pl.BlockSpec((None, c_all.shape[0], tn), lambda l, j: (l, 0, j)),
        out_shape=_sds((nl, c_all.shape[0], n), F32), compiler_params=_params(("parallel", "parallel")),
    )(c_all, w, b)


def _ada_bwd(c_all, dmod):
    nl, nb, n = dmod.shape
    d = c_all.shape[1]
    tn = _pick(n, (512, 256, 128))

    def body(c_ref, g_ref, o_ref):
        cv = c_ref[...]
        cond = (cv * jax.nn.sigmoid(cv)).astype(BF16)
        o_ref[...] = lax.dot_general(cond, g_ref[...].astype(BF16), (((0,), (0,)), ((), ())), preferred_element_type=F32)

    return pl.pallas_call(
        body, grid=(nl, n // tn), name="ada_bwd",
        in_specs=[pl.BlockSpec(c_all.shape, lambda l, j: (0, 0)), pl.BlockSpec((None, nb, tn), lambda l, j: (l, 0, j))],
        out_specs=pl.BlockSpec((None, d, tn), lambda l, j: (l, 0, j)),
        out_shape=_sds((nl, d, n), F32), compiler_params=_params(("parallel", "parallel")),
    )(c_all, dmod)


def _sum_leading(a, out_dtype=F32, name="sum_leading"):
    n, r, c = a.shape
    tr = _row_tile(r, c, 256 * 1024)

    def body(a_ref, o_ref):
        acc = a_ref[0].astype(F32)
        for k in range(1, n):
            acc = acc + a_ref[k].astype(F32)
        o_ref[...] = acc.astype(out_dtype)

    return pl.pallas_call(body, grid=(r // tr,), name=name, in_specs=[pl.BlockSpec((n, tr, c), lambda i: (0, i, 0))],
                          out_specs=pl.BlockSpec((tr, c), lambda i: (i, 0)), out_shape=_sds((r, c), out_dtype),
                          compiler_params=_params(("parallel",)))(a)


def _cast_into_full(w, kind, s_vec, l0, nl):
    _, r, c = w.shape
    tr = _row_tile(r, c, 512 * 1024)
    nb = r // tr
    if kind == "col":
        out_shape, out_spec = (nl, r, N_CHIP * c), pl.BlockSpec((None, tr, c), lambda l, i, sv: (l, i, sv[0]))
    else:
        out_shape, out_spec = (nl, N_CHIP * r, c), pl.BlockSpec((None, tr, c), lambda l, i, sv: (l, sv[0] * nb + i, 0))

    def body(sv_ref, w_ref, o_ref):
        o_ref[...] = w_ref[...].astype(BF16)

    return pl.pallas_call(
        body, name="cast_into_full",
        grid_spec=pltpu.PrefetchScalarGridSpec(num_scalar_prefetch=1, grid=(nl, nb),
                                               in_specs=[pl.BlockSpec((None, tr, c), lambda l, i, sv: (l0 + l, i, 0))], out_specs=out_spec),
        out_shape=_sds(out_shape, BF16), compiler_params=_params(("parallel", "parallel")),
    )(s_vec, w)


def _chip_sum_into(slab, cp, g, g_shape, layer, kind, sc_vec, rows_per_group=None):
    n, r, c = slab.shape
    rg = r if rows_per_group is None else rows_per_group
    tr = _row_tile(rg, c, 256 * 1024)
    groups = r // rg
    nbg = rg // tr
    nb = groups * nbg
    n_sc = len(sc_vec)
    if kind == "col":
        out_spec = pl.BlockSpec((None, tr, c), lambda gi, i, *sc: (layer, sc[1][0] * nb + gi * nbg + i, 0))
        own_spec = pl.BlockSpec((tr, c), lambda gi, i, *sc: (gi * nbg + i, sc[0][0]))
    else:
        out_spec = pl.BlockSpec((None, tr, c), lambda gi, i, *sc: (layer, gi * nbg + i, sc[1][0]))
        own_spec = pl.BlockSpec((tr, c), lambda gi, i, *sc: (gi * (n * nbg) + sc[0][0] * nbg + i, 0))

    def other(k):
        return pl.BlockSpec((None, tr, c), lambda gi, i, *sc: (sc[1 + k][0], gi * nbg + i, 0))

    in_specs = [own_spec] + [other(k) for k in range(1, n)]
    args = list(sc_vec) + [cp] + [slab] * (n - 1)
    aliases = {}
    if g is not None:
        in_specs.append(pl.BlockSpec(memory_space=pl.ANY))
        args.append(g)
        aliases = {len(args) - 1: 0}

    def body(*refs):
        own_ref, rest = refs[n_sc], refs[n_sc + 1:]
        o_ref = rest[-1]
        acc = own_ref[...].astype(F32)
        for k in range(n - 1):
            acc = acc + rest[k][...].astype(F32)
        o_ref[...] = acc

    return pl.pallas_call(
        body, name="chip_sum",
        grid_spec=pltpu.PrefetchScalarGridSpec(num_scalar_prefetch=n_sc, grid=(groups, nbg), in_specs=in_specs, out_specs=out_spec),
        out_shape=_sds(g_shape, F32), input_output_aliases=aliases, compiler_params=_params(("parallel", "parallel")),
    )(*args)


def _pair_add(dw, sib, kind, c_vec):
    r, c = sib.shape
    tr = _row_tile(r, c, 512 * 1024)
    nb = r // tr
    if kind == "col":
        dw_spec = pl.BlockSpec((tr, c), lambda i, cv: (cv[0] * nb + i, 0))
    else:
        dw_spec = pl.BlockSpec((tr, c), lambda i, cv: (i, cv[0]))

    def body(cv_ref, dw_ref, sib_ref, o_ref):
        o_ref[...] = (dw_ref[...].astype(F32) + sib_ref[...].astype(F32)).astype(BF16)

    return pl.pallas_call(
        body, name="pair_add",
        grid_spec=pltpu.PrefetchScalarGridSpec(num_scalar_prefetch=1, grid=(nb,), in_specs=[dw_spec, pl.BlockSpec((tr, c), lambda i, cv: (i, 0))],
                                               out_specs=pl.BlockSpec((tr, c), lambda i, cv: (i, 0))),
        out_shape=_sds((r, c), BF16), compiler_params=_params(("parallel",)),
    )(c_vec, dw, sib)


def _adamw(w, g, m, v):
    shape = w.shape
    cols = shape[-1]
    rows = w.size // cols
    tr = _row_tile(rows, cols, 256 * 1024)
    bc1 = 1.0 - ADAM_B1 ** ADAM_STEP
    bc2 = 1.0 - ADAM_B2 ** ADAM_STEP

    def body(w_ref, g_ref, m_ref, v_ref, d_ref, mo_ref, vo_ref):
        gv = g_ref[...]
        mn = ADAM_B1 * m_ref[...] + (1.0 - ADAM_B1) * gv
        vn = ADAM_B2 * v_ref[...] + (1.0 - ADAM_B2) * (gv * gv)
        d_ref[...] = -ADAM_LR * ((mn / bc1) / (jnp.sqrt(vn / bc2) + ADAM_EPS) + ADAM_WD * w_ref[...])
        mo_ref[...] = mn
        vo_ref[...] = vn

    spec = pl.BlockSpec((tr, cols), lambda i: (i, 0))
    outs = pl.pallas_call(body, grid=(rows // tr,), name="adamw", in_specs=[spec] * 4, out_specs=[spec] * 3,
                          out_shape=[_sds((rows, cols), F32)] * 3, compiler_params=_params(("parallel",)))(
        *[a.reshape(rows, cols) for a in (w, g, m, v)])
    return [o.reshape(shape) for o in outs]


def _mesh_pos():
    return lax.axis_index("x"), lax.axis_index("y"), lax.axis_index("c")


def _other_chips(x, y):
    return [(1 - x, y), (x, 1 - y), (1 - x, 1 - y)]


def _allgather_small(a, after=()):
    r, c = a.shape

    def body(x_ref, *rest):
        out_ref, send_sems, recv_sems, local_sem = rest[len(after):]
        x, y, cc = _mesh_pos()
        me, sibling = (x, y, cc), (x, y, 1 - cc)
        chips = _other_chips(x, y)

        def slab(px, py, pc):
            return out_ref.at[4 * px + 2 * py + pc]

        def copy(k, block, to, src=None):
            return pltpu.make_async_remote_copy(src_ref=slab(*block) if src is None else src, dst_ref=slab(*block),
                                                send_sem=send_sems.at[k], recv_sem=recv_sems.at[k], device_id=to,
                                                device_id_type=MESH)

        mine = pltpu.make_async_copy(x_ref, slab(*me), local_sem)
        mine.start()
        first = [copy(0, me, sibling, src=x_ref)]
        first += [copy(1 + j, me, (*chip, cc), src=x_ref) for j, chip in enumerate(chips)]
        for cp in first:
            cp.start()
        passed = [copy(4 + j, (*chip, cc), sibling) for j, chip in enumerate(chips)]
        for j, chip in enumerate(chips):
            copy(1 + j, (*chip, cc), me).wait_recv()
            passed[j].start()
        copy(0, sibling, me).wait_recv()
        for j, chip in enumerate(chips):
            copy(4 + j, (*chip, 1 - cc), me).wait_recv()
        for cp in first + passed:
            cp.wait_send()
        mine.wait()

    return pl.pallas_call(
        body, name="allgather_small", out_shape=_sds((N_DEV, r, c), F32),
        in_specs=[pl.BlockSpec(memory_space=pltpu.VMEM)] + [pl.BlockSpec(memory_space=pl.ANY)] * len(after),
        out_specs=pl.BlockSpec(memory_space=pltpu.VMEM),
        scratch_shapes=[pltpu.SemaphoreType.DMA((7,)), pltpu.SemaphoreType.DMA((7,)), pltpu.SemaphoreType.DMA],
        compiler_params=pltpu.CompilerParams(vmem_limit_bytes=V7X_VMEM_LIMIT_BYTES),
    )(a, *after)


def _idx(ref, rows=None, cols=None):
    lead = (slice(None),) * (len(ref.shape) - 2)
    return ref.at[lead + (slice(None) if rows is None else rows, slice(None) if cols is None else cols)]


def _half(ref, kind, c):
    r, cdim = ref.shape[-2:]
    if kind == "col":
        return _idx(ref, rows=pl.ds(c * (r // 2), r // 2))
    return _idx(ref, cols=pl.ds(c * (cdim // 2), cdim // 2))


def _shard_region(full, kind, shard_shape, s):
    r, cdim = shard_shape[-2:]
    if kind == "col":
        return _idx(full, cols=pl.ds(s * cdim, cdim))
    return _idx(full, rows=pl.ds(s * r, r))


def _allgather_weights(fulls, kinds, shard_shapes):
    nt = len(fulls)

    def body(*refs):
        fu = refs[nt:2 * nt]
        send_sems, recv_sems = refs[2 * nt:]
        x, y, c = _mesh_pos()
        chips = _other_chips(x, y)

        def part(t, chip, cc):
            return _half(_shard_region(fu[t], kinds[t], shard_shapes[t], 2 * chip[0] + chip[1]), kinds[t], cc)

        def copy(t, k, blk, to):
            return pltpu.make_async_remote_copy(src_ref=blk, dst_ref=blk, send_sem=send_sems.at[6 * t + k],
                                                recv_sem=recv_sems.at[6 * t + k], device_id=to, device_id_type=MESH)

        first, passed = [], []
        for t in range(nt):
            for j, chip in enumerate(chips):
                cp = copy(t, j, part(t, (x, y), c), (*chip, c))
                cp.start()
                first.append(cp)
        for t in range(nt):
            for j, chip in enumerate(chips):
                copy(t, j, part(t, chip, c), (x, y, c)).wait_recv()
                fw = copy(t, 3 + j, part(t, chip, c), (x, y, 1 - c))
                fw.start()
                passed.append(fw)
        for t in range(nt):
            for j, chip in enumerate(chips):
                copy(t, 3 + j, part(t, chip, 1 - c), (x, y, c)).wait_recv()
        for cp in first + passed:
            cp.wait_send()

    return pl.pallas_call(
        body, name="allgather_weights", out_shape=[_sds(f.shape, BF16) for f in fulls],
        in_specs=[HBM] * nt, out_specs=[HBM] * nt, input_output_aliases={t: t for t in range(nt)},
        scratch_shapes=[pltpu.SemaphoreType.DMA((6 * nt,)), pltpu.SemaphoreType.DMA((6 * nt,))],
    )(*fulls)


SEM = pl.BlockSpec(memory_space=pltpu.SEMAPHORE)
ANY = pl.BlockSpec(memory_space=pl.ANY)
EFFECT = pltpu.SideEffectType.DATAFLOW_SIDE_EFFECTING


def _gather_start(name, fulls, kinds, shard_shapes, prev, both=True):
    nt = len(fulls)

    def body(*refs):
        send_sems, recv_sems = refs[nt + 1], refs[nt + 2]
        fu = refs[nt + 3:2 * nt + 3]
        token = refs[2 * nt + 3]
        x, y, c = _mesh_pos()
        for t in range(nt):
            mine = _half(_shard_region(fu[t], kinds[t], shard_shapes[t], 2 * x + y), kinds[t], c)
            for j, chip in enumerate(_other_chips(x, y)):
                if both:
                    for e in range(2):
                        pltpu.make_async_remote_copy(src_ref=mine, dst_ref=mine, send_sem=send_sems.at[6 * t + 2 * j + e],
                                                     recv_sem=recv_sems.at[6 * t + 2 * j + c], device_id=(*chip, e),
                                                     device_id_type=MESH).start()
                else:
                    pltpu.make_async_remote_copy(src_ref=mine, dst_ref=mine, send_sem=send_sems.at[6 * t + 2 * j],
                                                 recv_sem=recv_sems.at[6 * t + 2 * j], device_id=(*chip, c),
                                                 device_id_type=MESH).start()
        token[...] = jnp.zeros_like(token)

    outs = pl.pallas_call(
        body, name=name,
        out_shape=(pltpu.SemaphoreType.DMA((6 * nt,)), pltpu.SemaphoreType.DMA((6 * nt,)), *[pltpu.HBM(f.shape, f.dtype) for f in fulls],
                   _sds((8, 128), F32)),
        in_specs=[HBM] * nt + [ANY], out_specs=(SEM, SEM, *[HBM] * nt, pl.BlockSpec(memory_space=pltpu.VMEM)),
        input_output_aliases={t: 2 + t for t in range(nt)}, compiler_params=pltpu.CompilerParams(has_side_effects=EFFECT),
    )(*fulls, prev)
    return outs[0], outs[1], list(outs[2:2 + nt]), outs[2 + nt]


def _forward_halves(fulls, kinds, shard_shapes):
    nt = len(fulls)

    def body(*refs):
        fu = refs[nt:2 * nt]
        send_sems, recv_sems = refs[2 * nt:]
        x, y, c = _mesh_pos()

        def part(t, chip, cc):
            return _half(_shard_region(fu[t], kinds[t], shard_shapes[t], 2 * chip[0] + chip[1]), kinds[t], cc)

        copies = []
        for t in range(nt):
            for j, chip in enumerate(_other_chips(x, y)):
                blk = part(t, chip, c)
                cp = pltpu.make_async_remote_copy(src_ref=blk, dst_ref=blk, send_sem=send_sems.at[3 * t + j],
                                                  recv_sem=recv_sems.at[3 * t + j], device_id=(x, y, 1 - c), device_id_type=MESH)
                cp.start()
                copies.append(cp)
        for t in range(nt):
            for j, chip in enumerate(_other_chips(x, y)):
                blk = part(t, chip, 1 - c)
                pltpu.make_async_remote_copy(src_ref=blk, dst_ref=blk, send_sem=send_sems.at[3 * t + j],
                                             recv_sem=recv_sems.at[3 * t + j], device_id=(x, y, c), device_id_type=MESH).wait_recv()
        for cp in copies:
            cp.wait_send()

    return pl.pallas_call(
        body, name="forward_halves", out_shape=[_sds(f.shape, BF16) for f in fulls], in_specs=[HBM] * nt, out_specs=[HBM] * nt,
        input_output_aliases={t: t for t in range(nt)},
        scratch_shapes=[pltpu.SemaphoreType.DMA((3 * nt,)), pltpu.SemaphoreType.DMA((3 * nt,))],
    )(*fulls)


def _gather_wait(name, send_sems, recv_sems, fulls, kinds, shard_shapes, after, both=True):
    nt = len(fulls)

    def body(*refs):
        fu = refs[:nt]
        send_sems, recv_sems = refs[nt], refs[nt + 1]
        x, y, c = _mesh_pos()

        def part(t, chip, cc):
            return _half(_shard_region(fu[t], kinds[t], shard_shapes[t], 2 * chip[0] + chip[1]), kinds[t], cc)

        for t in range(nt):
            for j, chip in enumerate(_other_chips(x, y)):
                if not both:
                    mine, landed = part(t, (x, y), c), part(t, chip, c)
                    pltpu.make_async_remote_copy(src_ref=mine, dst_ref=mine, send_sem=send_sems.at[6 * t + 2 * j],
                                                 recv_sem=recv_sems.at[6 * t + 2 * j], device_id=(x, y, c),
                                                 device_id_type=MESH).wait_send()
                    pltpu.make_async_remote_copy(src_ref=landed, dst_ref=landed, send_sem=send_sems.at[6 * t + 2 * j],
                                                 recv_sem=recv_sems.at[6 * t + 2 * j], device_id=(x, y, c),
                                                 device_id_type=MESH).wait_recv()
                    continue
                for e in range(2):
                    mine = part(t, (x, y), c)
                    pltpu.make_async_remote_copy(src_ref=mine, dst_ref=mine, send_sem=send_sems.at[6 * t + 2 * j + e],
                                                 recv_sem=recv_sems.at[6 * t + 2 * j + e], device_id=(x, y, c),
                                                 device_id_type=MESH).wait_send()
                    landed = part(t, chip, e)
                    pltpu.make_async_remote_copy(src_ref=landed, dst_ref=landed, send_sem=send_sems.at[6 * t + 2 * j + e],
                                                 recv_sem=recv_sems.at[6 * t + 2 * j + e], device_id=(x, y, c),
                                                 device_id_type=MESH).wait_recv()

    return pl.pallas_call(
        body, name=name, out_shape=[pltpu.HBM(f.shape, f.dtype) for f in fulls], in_specs=[HBM] * nt + [SEM, SEM, ANY],
        out_specs=[HBM] * nt, input_output_aliases={t: t for t in range(nt)},
        compiler_params=pltpu.CompilerParams(has_side_effects=EFFECT),
    )(*fulls, send_sems, recv_sems, after)


def _reduce_start(name, cps, kinds, prev):
    nt = len(cps)
    slab_shapes = []
    for cp, kind in zip(cps, kinds):
        shp = list(cp.shape)
        shp[-1 if kind == "col" else -2] //= N_CHIP
        slab_shapes.append((N_CHIP,) + tuple(shp))

    def body(*refs):
        send_sems, recv_sems = refs[nt + 1], refs[nt + 2]
        src = refs[nt + 3:2 * nt + 3]
        dst = refs[2 * nt + 3:3 * nt + 3]
        token = refs[3 * nt + 3]
        x, y, c = _mesh_pos()
        s = 2 * x + y
        for t in range(nt):
            for j, chip in enumerate(_other_chips(x, y)):
                pltpu.make_async_remote_copy(src_ref=_chip_block(src[t], kinds[t], 2 * chip[0] + chip[1]), dst_ref=dst[t].at[s],
                                             send_sem=send_sems.at[3 * t + j], recv_sem=recv_sems.at[3 * t + j],
                                             device_id=(*chip, c), device_id_type=MESH).start()
        token[...] = jnp.zeros_like(token)

    outs = pl.pallas_call(
        body, name=name,
        out_shape=(pltpu.SemaphoreType.DMA((3 * nt,)), pltpu.SemaphoreType.DMA((3 * nt,)), *[pltpu.HBM(a.shape, a.dtype) for a in cps],
                   *[pltpu.HBM(shp, BF16) for shp in slab_shapes], _sds((8, 128), F32)),
        in_specs=[HBM] * nt + [ANY], out_specs=(SEM, SEM, *[HBM] * (2 * nt), pl.BlockSpec(memory_space=pltpu.VMEM)),
        input_output_aliases={t: 2 + t for t in range(nt)}, compiler_params=pltpu.CompilerParams(has_side_effects=EFFECT),
    )(*cps, prev)
    return outs[0], outs[1], list(outs[2:2 + nt]), list(outs[2 + nt:2 + 2 * nt]), outs[2 + 2 * nt]


def _reduce_wait(name, send_sems, recv_sems, cps, slabs, kinds, after):
    nt = len(cps)

    def body(*refs):
        src, dst = refs[:nt], refs[nt:2 * nt]
        send_sems, recv_sems = refs[2 * nt], refs[2 * nt + 1]
        x, y, c = _mesh_pos()
        for t in range(nt):
            for j, chip in enumerate(_other_chips(x, y)):
                sj = 2 * chip[0] + chip[1]
                pltpu.make_async_remote_copy(src_ref=_chip_block(src[t], kinds[t], sj), dst_ref=dst[t].at[sj],
                                             send_sem=send_sems.at[3 * t + j], recv_sem=recv_sems.at[3 * t + j],
                                             device_id=(x, y, c), device_id_type=MESH).wait()

    outs = pl.pallas_call(
        body, name=name, out_shape=[pltpu.HBM(a.shape, a.dtype) for a in cps] + [pltpu.HBM(a.shape, a.dtype) for a in slabs],
        in_specs=[HBM] * (2 * nt) + [SEM, SEM, ANY], out_specs=[HBM] * (2 * nt), input_output_aliases={t: t for t in range(2 * nt)},
        compiler_params=pltpu.CompilerParams(has_side_effects=EFFECT),
    )(*cps, *slabs, send_sems, recv_sems, after)
    return list(outs[:nt]), list(outs[nt:])


def _pair_start(name, dws, kinds, prev):
    nt = len(dws)
    sib_shapes = []
    for dw, kind in zip(dws, kinds):
        shp = list(dw.shape)
        shp[-2 if kind == "col" else -1] //= 2
        sib_shapes.append(tuple(shp))

    def body(*refs):
        send_sems, recv_sems = refs[nt + 1], refs[nt + 2]
        src = refs[nt + 3:2 * nt + 3]
        dst = refs[2 * nt + 3:3 * nt + 3]
        token = refs[3 * nt + 3]
        x, y, c = _mesh_pos()
        for t in range(nt):
            pltpu.make_async_remote_copy(src_ref=_half(src[t], kinds[t], 1 - c), dst_ref=dst[t], send_sem=send_sems.at[t],
                                         recv_sem=recv_sems.at[t], device_id=(x, y, 1 - c), device_id_type=MESH).start()
        token[...] = jnp.zeros_like(token)

    outs = pl.pallas_call(
        body, name=name,
        out_shape=(pltpu.SemaphoreType.DMA((nt,)), pltpu.SemaphoreType.DMA((nt,)), *[pltpu.HBM(a.shape, a.dtype) for a in dws],
                   *[pltpu.HBM(shp, BF16) for shp in sib_shapes], _sds((8, 128), F32)),
        in_specs=[HBM] * nt + [ANY], out_specs=(SEM, SEM, *[HBM] * (2 * nt), pl.BlockSpec(memory_space=pltpu.VMEM)),
        input_output_aliases={t: 2 + t for t in range(nt)}, compiler_params=pltpu.CompilerParams(has_side_effects=EFFECT),
    )(*dws, prev)
    return outs[0], outs[1], list(outs[2:2 + nt]), list(outs[2 + nt:2 + 2 * nt]), outs[2 + 2 * nt]


def _pair_wait(name, send_sems, recv_sems, dws, sibs, kinds, after):
    nt = len(dws)

    def body(*refs):
        src, dst = refs[:nt], refs[nt:2 * nt]
        send_sems, recv_sems = refs[2 * nt], refs[2 * nt + 1]
        x, y, c = _mesh_pos()
        for t in range(nt):
            pltpu.make_async_remote_copy(src_ref=_half(src[t], kinds[t], 1 - c), dst_ref=dst[t], send_sem=send_sems.at[t],
                                         recv_sem=recv_sems.at[t], device_id=(x, y, c), device_id_type=MESH).wait()

    outs = pl.pallas_call(
        body, name=name, out_shape=[pltpu.HBM(a.shape, a.dtype) for a in dws] + [pltpu.HBM(a.shape, a.dtype) for a in sibs],
        in_specs=[HBM] * (2 * nt) + [SEM, SEM, ANY], out_specs=[HBM] * (2 * nt), input_output_aliases={t: t for t in range(2 * nt)},
        compiler_params=pltpu.CompilerParams(has_side_effects=EFFECT),
    )(*dws, *sibs, send_sems, recv_sems, after)
    return list(outs[:nt]), list(outs[nt:])


def _pair_exchange(dws, kinds):
    nt = len(dws)
    out_shapes = []
    for dw, kind in zip(dws, kinds):
        shp = list(dw.shape)
        shp[-2 if kind == "col" else -1] //= 2
        out_shapes.append(tuple(shp))

    def body(*refs):
        src, dst = refs[:nt], refs[nt:2 * nt]
        send_sems, recv_sems = refs[2 * nt:]
        x, y, c = _mesh_pos()
        copies = [pltpu.make_async_remote_copy(src_ref=_half(src[t], kinds[t], 1 - c), dst_ref=dst[t], send_sem=send_sems.at[t],
                                               recv_sem=recv_sems.at[t], device_id=(x, y, 1 - c), device_id_type=MESH)
                  for t in range(nt)]
        for cp in copies:
            cp.start()
        for cp in copies:
            cp.wait()

    return pl.pallas_call(
        body, name="grad_pair_exchange", out_shape=[_sds(shp, BF16) for shp in out_shapes], in_specs=[HBM] * nt,
        out_specs=[HBM] * nt, scratch_shapes=[pltpu.SemaphoreType.DMA((nt,)), pltpu.SemaphoreType.DMA((nt,))],
    )(*dws)


def _chip_block(ref, kind, s):
    r, cdim = ref.shape[-2:]
    if kind == "col":
        return _idx(ref, cols=pl.ds(s * (cdim // N_CHIP), cdim // N_CHIP))
    return _idx(ref, rows=pl.ds(s * (r // N_CHIP), r // N_CHIP))


def _chip_exchange(cps, kinds):
    nt = len(cps)
    out_shapes = []
    for cp, kind in zip(cps, kinds):
        shp = list(cp.shape)
        shp[-1 if kind == "col" else -2] //= N_CHIP
        out_shapes.append((N_CHIP,) + tuple(shp))

    def body(*refs):
        src, dst = refs[:nt], refs[nt:2 * nt]
        send_sems, recv_sems, local_sems = refs[2 * nt:]
        x, y, c = _mesh_pos()
        s = 2 * x + y
        chips = _other_chips(x, y)
        sends, locals_ = [], []
        for t in range(nt):
            own = pltpu.make_async_copy(_chip_block(src[t], kinds[t], s), dst[t].at[s], local_sems.at[t])
            own.start()
            locals_.append(own)
            for j, chip in enumerate(chips):
                cp = pltpu.make_async_remote_copy(src_ref=_chip_block(src[t], kinds[t], 2 * chip[0] + chip[1]), dst_ref=dst[t].at[s],
                                                  send_sem=send_sems.at[3 * t + j], recv_sem=recv_sems.at[3 * t + j],
                                                  device_id=(*chip, c), device_id_type=MESH)
                cp.start()
                sends.append(cp)
        for t in range(nt):
            for j, chip in enumerate(chips):
                landing = dst[t].at[2 * chip[0] + chip[1]]
                pltpu.make_async_remote_copy(src_ref=landing, dst_ref=landing, send_sem=send_sems.at[3 * t + j],
                                             recv_sem=recv_sems.at[3 * t + j], device_id=(x, y, c), device_id_type=MESH).wait_recv()
        for cp in sends:
            cp.wait_send()
        for own in locals_:
            own.wait()

    return pl.pallas_call(
        body, name="grad_chip_exchange", out_shape=[_sds(shp, BF16) for shp in out_shapes], in_specs=[HBM] * nt,
        out_specs=[HBM] * nt,
        scratch_shapes=[pltpu.SemaphoreType.DMA((3 * nt,)), pltpu.SemaphoreType.DMA((3 * nt,)), pltpu.SemaphoreType.DMA((nt,))],
    )(*cps)


def _pair_assemble(gs, kinds):
    nt = len(gs)

    def body(*refs):
        g = refs[nt:2 * nt]
        send_sems, recv_sems = refs[2 * nt:]
        x, y, c = _mesh_pos()
        copies = []
        for t in range(nt):
            mine = _half(g[t], kinds[t], c)
            cp = pltpu.make_async_remote_copy(src_ref=mine, dst_ref=mine, send_sem=send_sems.at[t], recv_sem=recv_sems.at[t],
                                              device_id=(x, y, 1 - c), device_id_type=MESH)
            cp.start()
            copies.append(cp)
        for t in range(nt):
            landing = _half(g[t], kinds[t], 1 - c)
            pltpu.make_async_remote_copy(src_ref=landing, dst_ref=landing, send_sem=send_sems.at[t], recv_sem=recv_sems.at[t],
                                         device_id=(x, y, c), device_id_type=MESH).wait_recv()
        for cp in copies:
            cp.wait_send()

    return pl.pallas_call(
        body, name="grad_pair_assemble", out_shape=[_sds(a.shape, F32) for a in gs], in_specs=[HBM] * nt,
        out_specs=[HBM] * nt, input_output_aliases={t: t for t in range(nt)},
        scratch_shapes=[pltpu.SemaphoreType.DMA((nt,)), pltpu.SemaphoreType.DMA((nt,))],
    )(*gs)


def _pack(arrays, width):
    flat = jnp.concatenate([a.reshape(-1) for a in arrays])
    pad = (-flat.size) % (8 * width)
    return jnp.pad(flat, (0, pad)).reshape(-1, width)


def _unpack(packed, shapes):
    flat = packed.reshape(-1)
    out, off = [], 0
    for shp in shapes:
        n = 1
        for dim in shp:
            n *= dim
        out.append(flat[off:off + n].reshape(shp))
        off += n
    return out


def kernel(x, c, ada_w, ada_b, norm_mix_g, norm_ffn_g, ab_w_in, a_conv_w, a_conv_b, a_norm_g, a_norm_b, b_norm_g, b_norm_b, b_w_s, b_bias, ab_w_out, pool_w, pool_scale, ffn_w1, ffn_w3, ffn_w2, final_g, loss_target, m_ada_w, m_ada_b, m_norm_mix_g, m_norm_ffn_g, m_ab_w_in, m_a_conv_w, m_a_conv_b, m_a_norm_g, m_a_norm_b, m_b_norm_g, m_b_norm_b, m_b_w_s, m_b_bias, m_ab_w_out, m_pool_w, m_pool_scale, m_ffn_w1, m_ffn_w3, m_ffn_w2, m_final_g, v_ada_w, v_ada_b, v_norm_mix_g, v_norm_ffn_g, v_ab_w_in, v_a_conv_w, v_a_conv_b, v_a_norm_g, v_a_norm_b, v_b_norm_g, v_b_norm_b, v_b_w_s, v_b_bias, v_ab_w_out, v_pool_w, v_pool_scale, v_ffn_w1, v_ffn_w3, v_ffn_w2, v_final_g):
    mx, my, mc = _mesh_pos()
    chip = 2 * mx + my
    dev = 4 * mx + 2 * my + mc
    x2 = x[0]
    target = loss_target[0]
    s, d = x2.shape
    depth = ada_w.shape[0]
    n_mod = ada_b.shape[1] // d
    n_even = ab_w_in.shape[0]
    da, db = a_conv_b.shape[1], b_norm_g.shape[1]
    nh = b_w_s.shape[1]
    kw = a_conv_w.shape[1]
    n_pool = pool_w.shape[1]
    tm_row = _pick(s, (256, 128))

    s_vec = jnp.reshape(chip, (1,)).astype(jnp.int32)
    c_vec = jnp.reshape(mc, (1,)).astype(jnp.int32)
    sc_vec = [s_vec, c_vec] + [jnp.reshape(v, (1,)).astype(jnp.int32)
                               for v in (2 * mx + (1 - my), 2 * (1 - mx) + my, 2 * (1 - mx) + (1 - my))]
    pool_w3 = pool_w.reshape((-1,) + pool_w.shape[2:])
    shard_of = {"w_in": (ab_w_in, "col"), "w_out": (ab_w_out, "row"), "pool": (pool_w3, "row"), "w1": (ffn_w1, "col"),
                "w3": (ffn_w3, "col"), "w2": (ffn_w2, "row")}

    def layer_names(l):
        return (["w_in", "w_out"] if l % 2 == 0 else ["pool"]) + ["w1", "w3", "w2"]

    def layer_span(nm, l):
        if nm in ("w_in", "w_out"):
            return l // 2, 1
        if nm == "pool":
            return (l // 2) * n_pool, n_pool
        return l, 1

    group_names = {"0a": ["w_in"], "0o": ["w_out"], "0b": ["w1", "w3"], "0c": ["w2"]}
    group_layer = {key: 0 for key in group_names}
    for l in range(1, depth):
        group_names[l], group_layer[l] = layer_names(l), l
    owned, w_kinds, w_shapes = {}, {}, {}
    for key, names in group_names.items():
        l = group_layer[key]
        owned[key] = [_cast_into_full(shard_of[nm][0], shard_of[nm][1], s_vec, *layer_span(nm, l)) for nm in names]
        w_kinds[key] = [shard_of[nm][1] for nm in names]
        w_shapes[key] = [(layer_span(nm, l)[1],) + shard_of[nm][0].shape[1:] for nm in names]
    layer_w = [{} for _ in range(depth)]
    layer_w[0].update(zip(group_names["0a"], _allgather_weights(owned["0a"], w_kinds["0a"], w_shapes["0a"])))
    gathers = {"prev": layer_w[0][group_names["0a"][-1]], "token": None, "flying": {}}

    passed_on = {1, 2}

    def gather_start(key):
        sends, recvs, fulls, token = _gather_start(f"gather_start_{key}", owned[key], w_kinds[key], w_shapes[key], gathers["prev"],
                                                   both=key not in passed_on)
        gathers["flying"][key] = (sends, recvs, fulls)
        gathers["prev"] = gathers["token"] = token

    def gather_wait(key, after):
        sends, recvs, fulls = gathers["flying"].pop(key)
        landed = _gather_wait(f"gather_wait_{key}", sends, recvs, fulls, w_kinds[key], w_shapes[key], after, both=key not in passed_on)
        if key in passed_on:
            landed = _forward_halves(landed, w_kinds[key], w_shapes[key])
        layer_w[group_layer[key]].update(zip(group_names[key], landed))
        gathers["prev"] = landed[-1]

    def after_starts(row):
        return row + gathers["token"][0:1, 0:1]

    for key in ("0o", "0b", "0c"):
        gather_start(key)

    pre = _allgather_small(_pack([c, a_conv_w, pool_scale], 128)).reshape(N_DEV, -1)
    n_cw, n_ps = a_conv_w.size, pool_scale.size
    c_all = pre[:, :d]
    cw_chips = pre[0::2, d:d + n_cw].reshape((N_CHIP,) + a_conv_w.shape)
    conv_w_full = jnp.concatenate([cw_chips[k] for k in range(N_CHIP)], axis=-1)
    ps_chips = pre[0::2, d + n_cw:d + n_cw + n_ps].reshape((N_CHIP,) + pool_scale.shape)
    pool_scale_full = jnp.concatenate([ps_chips[k] for k in range(N_CHIP)], axis=-1)
    c_pad = jnp.pad(c_all, ((0, 8), (0, 0)))
    n_ada = ada_w.shape[2]
    ada_b_mine = lax.dynamic_slice_in_dim(ada_b, chip * n_ada, n_ada, axis=1)[:, None, :]
    mod_part = _ada_fwd(c_pad, ada_w, ada_b_mine)[:, :N_DEV, :]
    mod_all = _allgather_small(mod_part.reshape(depth * N_DEV, n_ada))
    mod_chips = mod_all[0::2].reshape(N_CHIP, depth, N_DEV, n_ada)
    mod_mine = lax.dynamic_index_in_dim(mod_chips, dev, axis=2, keepdims=False)
    mod = jnp.transpose(mod_mine, (1, 0, 2)).reshape(depth, n_mod, 1, d)

    causal = jnp.tril(jnp.ones((CHUNK, CHUNK), dtype=bool))
    w_c = jnp.where(causal[None, None], b_w_s, 0.0).astype(BF16)
    bias_full = jnp.repeat(jnp.swapaxes(b_bias, 1, 2), CHUNK, axis=2)

    saved = []
    xs = x2
    for l in range(depth):
        sh1, sc1, g1, sh2, sc2, g2 = [mod[l, k] for k in range(n_mod)]
        i = l // 2
        st = {"x1": xs}
        if 1 <= l and l + 2 < depth:
            gather_start(l + 2)
        gain1 = after_starts(norm_mix_g[l][None])
        wl = layer_w[l]
        if l % 2 == 0:
            h = _norm_mod(xs, gain1, sh1, sc1, tm_row)
            proj = _mm("ab_proj", [h], [(wl["w_in"], 0)], [BF16], _ep_store, tk=2048)[0]
            cat, a1 = _ab_mid_fwd(proj, conv_w_full[i], a_conv_b[i][None], a_norm_g[i][None], a_norm_b[i][None],
                                  b_norm_g[i][None], b_norm_b[i][None], w_c[i], bias_full[i], tm_row)
            if l == 0:
                gather_wait("0o", cat)
            xs, y1 = _mm("ab_out", [cat], [(wl["w_out"], 0)], [F32, BF16], _ep_residual, extras=[xs, g1], extra_kinds=["tile", "row"], tk=2048)
            st.update(h=h, proj=proj, a1=a1, cat=cat, y=y1)
        else:
            p = _pool_fwd(xs, gain1, sh1, sc1, tm_row)
            gate = g1 * pool_scale_full[i][None]
            xs, ymm = _grouped_fwd(p, wl["pool"][None], 0, xs, gate, 1024)
            st.update(p=p, y=ymm, gate=gate)
        st["x2"] = xs
        gain2 = norm_ffn_g[l][None]
        if l == 0:
            gather_wait("0b", xs)
            for nxt in range(1, min(3, depth)):
                gather_start(nxt)
            gain2 = after_starts(gain2)
        h2 = _norm_mod(xs, gain2, sh2, sc2, tm_row)
        u, t, z = _mm("ffn_up", [h2, h2], [(wl["w1"], 0), (wl["w3"], 0)], [BF16, BF16, BF16], _ep_swiglu, tn=512, tk=2048)
        if l == 0:
            gather_wait("0c", z)
        xs, y2 = _mm("ffn_down", [z], [(wl["w2"], 0)], [F32, BF16], _ep_residual, extras=[xs, g2], extra_kinds=["tile", "row"],
                     tn=512, tk=ffn_w2.shape[1] * N_CHIP)
        if l + 1 < depth:
            gather_wait(l + 1, xs)
        st.update(h2=h2, u=u, t=t, z=z, y2=y2)
        saved.append(st)

    def below_of(l, which):
        if which == "ffn":
            return saved[l]["y2"], mod[l, n_mod - 1]
        return saved[l]["y"], (mod[l, 2] if l % 2 == 0 else saved[l]["gate"])

    dx, fin_acc, loss_blk, dyb, gacc = _final_loss_bwd(xs, final_g[None], target, tm_row, below_of(depth - 1, "ffn"))
    loss = lax.psum(loss_blk[0, 0], MESH_AXES)
    d_final_g = fin_acc[0]
    dmod_rows = [None] * depth
    d_norm_mix, d_norm_ffn = [None] * depth, [None] * depth
    even_small = [None] * n_even
    d_pool_scale = [None] * (depth // 2)
    grad_names = ["w_in", "w_out", "pool", "w1", "w3", "w2"]
    g_shapes = {"w_in": ab_w_in.shape, "w_out": ab_w_out.shape, "pool": (pool_w.shape[0], n_pool * pool_w.shape[2], pool_w.shape[3]),
                "w1": ffn_w1.shape, "w3": ffn_w3.shape, "w2": ffn_w2.shape}
    shard_grads = {nm: None for nm in grad_names}
    pipe = {"pair": None, "chip": None, "prev": None, "token": None}

    def finish_chip(after):
        tag, pl_, names_, kinds_, sends, recvs, cps_f, slabs_f = pipe["chip"]
        cps_d, slabs_d = _reduce_wait(f"reduce_wait_{tag}", sends, recvs, cps_f, slabs_f, kinds_, after)
        for nm, kind, cp, sl in zip(names_, kinds_, cps_d, slabs_d):
            rpg = pool_w.shape[2] if nm == "pool" else None
            cp2 = cp.reshape(-1, cp.shape[-1])
            shard_grads[nm] = _chip_sum_into(sl.reshape(N_CHIP, -1, sl.shape[-1]), cp2, shard_grads[nm], g_shapes[nm],
                                             layer_span(nm, pl_)[0] // (n_pool if nm == "pool" else 1), kind, sc_vec, rpg)
        pipe["chip"] = None
        pipe["prev"] = slabs_d[-1]

    def settle(after):
        made = []
        if pipe["pair"] is not None:
            tag, lyr, names, kinds_l, sends, recvs, dws_f, sibs_f = pipe["pair"]
            dws_d, sibs_d = _pair_wait(f"pair_wait_{tag}", sends, recvs, dws_f, sibs_f, kinds_l, after)
            cps = []
            for dw, sib, kind in zip(dws_d, sibs_d, kinds_l):
                cp = _pair_add(dw.reshape(-1, dw.shape[-1]), sib.reshape(-1, sib.shape[-1]), kind, c_vec)
                cps.append(cp.reshape(sib.shape))
            pipe["pair"] = None
            pipe["ready"] = (tag, lyr, names, kinds_l, cps)
            made.append(cps[-1])
        if pipe["chip"] is not None:
            finish_chip(after)
            made.append(pipe["prev"])
        return made

    def advance(after, new=None):
        settle(after)
        if pipe.get("ready") is not None:
            tag, lyr, names, kinds_l, cps = pipe.pop("ready")
            prev = cps[-1] if pipe["prev"] is None else pipe["prev"]
            sends, recvs, cps_f, slabs_f, token = _reduce_start(f"reduce_start_{tag}", cps, kinds_l, prev)
            pipe["chip"] = (tag, lyr, names, kinds_l, sends, recvs, cps_f, slabs_f)
            pipe["token"] = token
        if new is not None:
            tag, lyr, names, big_ = new
            kinds_l = [shard_of[nm][1] for nm in names]
            dws = [big_[nm] for nm in names]
            prev = dws[-1] if pipe["token"] is None else pipe["token"]
            sends, recvs, dws_f, sibs_f, token = _pair_start(f"pair_start_{tag}", dws, kinds_l, prev)
            pipe["pair"] = (tag, lyr, names, kinds_l, sends, recvs, dws_f, sibs_f)
            pipe["token"] = token

    def behind(row):
        return row + pipe["token"][0:1, 0:1]

    for l in reversed(range(depth)):
        sh1, sc1, g1, sh2, sc2, g2 = [mod[l, k] for k in range(n_mod)]
        st = saved[l]
        wl = layer_w[l]
        i = l // 2
        big = {}
        d_g2 = gacc[0]
        du, dt = _mm("ffn_dz", [dyb], [(wl["w2"], 0)], [BF16, BF16], _ep_swiglu_bwd, trans_b=True, extras=[st["u"], st["t"]],
                     extra_kinds=["tile", "tile"], tm=2048, tn=512, tk=2048)
        big["w2"] = _mm("ffn_dw2", [st["z"]], [dyb], [BF16], _ep_store, trans_a=True, tm=512, tn=512, tk=s)[0]
        big["w1"] = _mm("ffn_dw1", [st["h2"]], [du], [BF16], _ep_store, trans_a=True, tm=512, tn=512, tk=s)[0]
        big["w3"] = _mm("ffn_dw3", [st["h2"]], [dt], [BF16], _ep_store, trans_a=True, tm=512, tn=512, tk=s)[0]
        advance(big["w3"], (f"ffn{l}", l, ["w1", "w3", "w2"], big))
        dh = _mm("ffn_dh", [du, dt], [(wl["w1"], 0), (wl["w3"], 0)], [BF16], _ep_sum, trans_b=True, tn=512, tk=ffn_w2.shape[1] * 2)[0]
        dx, nacc, dyb, gacc = _norm_mod_bwd(dh, st["x2"], dx, behind(norm_ffn_g[l][None]), sc2, tm_row, below_of(l, "mix"))
        d_sh2, d_sc2, d_norm_ffn[l] = nacc[0], nacc[1], nacc[2]
        if l % 2 == 0:
            d_g1 = gacc[0]
            big["w_out"] = _mm("ab_dw_out", [st["cat"]], [dyb], [BF16], _ep_store, trans_a=True, tm=512, tn=512, tk=s)[0]
            dcat = _mm("ab_dcat", [dyb], [(wl["w_out"], 0)], [BF16], _ep_store, trans_b=True, tk=2048)[0]
            advance(dcat)
            dproj, dcw, vecs, dws, dbias = _ab_mid_bwd(dcat, st["proj"], st["a1"], conv_w_full[i], behind(a_norm_g[i][None]),
                                                       a_norm_b[i][None], b_norm_g[i][None], b_norm_b[i][None], w_c[i],
                                                       bias_full[i], tm_row)
            even_small[i] = dict(conv_w=dcw, a_norm_g=vecs[0, :da], a_norm_b=vecs[1, :da], conv_b=vecs[2, :da],
                                 b_norm_g=vecs[3, :db], b_norm_b=vecs[4, :db], w_s=dws, bias=dbias[:, :, 0])
            big["w_in"] = _mm("ab_dw_in", [st["h"]], [dproj], [BF16], _ep_store, trans_a=True, tm=512, tn=512, tk=s)[0]
            advance(big["w_in"], (f"mix{l}", l, ["w_in", "w_out"], big))
            dh = _mm("ab_dh", [dproj], [(wl["w_in"], 0)], [BF16], _ep_store, trans_b=True, tn=512, tk=2 * da + 2 * db)[0]
            below = below_of(l - 1, "ffn") if l > 0 else None
            outs = _norm_mod_bwd(dh, st["x1"], dx, behind(norm_mix_g[l][None]), sc1, tm_row, below)
            dx, nacc = outs[0], outs[1]
            if below is not None:
                dyb, gacc = outs[2], outs[3]
        else:
            d_g1 = gacc[0] * pool_scale_full[i]
            d_pool_scale[i] = gacc[0] * g1[0]
            big["pool"] = _grouped_dw(st["p"], dyb, n_pool, 1024)
            advance(big["pool"], (f"mix{l}", l, ["pool"], big))
            dp = _grouped_dx(dyb, wl["pool"][None], 0, 1024)
            dx, nacc, dyb, gacc = _pool_bwd(dp, st["x1"], dx, behind(norm_mix_g[l][None]), sc1, tm_row, below_of(l - 1, "ffn"))
        d_sh1, d_sc1, d_norm_mix[l] = nacc[0], nacc[1], nacc[2]
        dmod_rows[l] = jnp.concatenate([d_sh1, d_sc1, d_g1, d_sh2, d_sc2, d_g2])
    grad_x = dx[None]
    settled = settle(dx)

    dmod = jnp.stack(dmod_rows)
    small = [dmod, jnp.stack(d_norm_mix), jnp.stack(d_norm_ffn),
             jnp.stack([e["conv_w"] for e in even_small]), jnp.stack([e["conv_b"] for e in even_small]),
             jnp.stack([e["a_norm_g"] for e in even_small]), jnp.stack([e["a_norm_b"] for e in even_small]),
             jnp.stack([e["b_norm_g"] for e in even_small]), jnp.stack([e["b_norm_b"] for e in even_small]),
             jnp.stack([e["w_s"] for e in even_small]), jnp.stack([e["bias"] for e in even_small]),
             jnp.stack(d_pool_scale), d_final_g]
    small_shapes = [a.shape for a in small]
    width = 1024 if d >= 1024 else 128
    gathered = _allgather_small(_pack(small, width), after=settled)
    summed = _unpack(_sum_leading(gathered), small_shapes)
    (g_ada_b, g_norm_mix, g_norm_ffn, g_conv_w_full, g_conv_b, g_a_norm_g, g_a_norm_b, g_b_norm_g, g_b_norm_b, g_w_s, g_bias,
     g_pool_scale_full, g_final_g) = summed
    cw_cols = a_conv_w.shape[2]
    g_conv_w = lax.dynamic_slice_in_dim(g_conv_w_full, chip * cw_cols, cw_cols, axis=2)
    ps_cols = pool_scale.shape[1]
    g_pool_scale = lax.dynamic_slice_in_dim(g_pool_scale_full, chip * ps_cols, ps_cols, axis=1)

    dmod_all = gathered.reshape(N_DEV, -1)[:, :dmod.size].reshape(N_DEV, depth, n_mod * d)
    dmod_cols = lax.dynamic_slice_in_dim(dmod_all, chip * n_ada, n_ada, axis=2)
    dmod_cols = jnp.pad(jnp.transpose(dmod_cols, (1, 0, 2)), ((0, 0), (0, 8), (0, 0)))
    g_ada_w = _ada_bwd(c_pad, dmod_cols)

    weights = [ada_w, ada_b, norm_mix_g, norm_ffn_g, ab_w_in, a_conv_w, a_conv_b, a_norm_g, a_norm_b, b_norm_g, b_norm_b, b_w_s,
               b_bias, ab_w_out, pool_w, pool_scale, ffn_w1, ffn_w3, ffn_w2, final_g]
    ms = [m_ada_w, m_ada_b, m_norm_mix_g, m_norm_ffn_g, m_ab_w_in, m_a_conv_w, m_a_conv_b, m_a_norm_g, m_a_norm_b, m_b_norm_g,
          m_b_norm_b, m_b_w_s, m_b_bias, m_ab_w_out, m_pool_w, m_pool_scale, m_ffn_w1, m_ffn_w3, m_ffn_w2, m_final_g]
    vs = [v_ada_w, v_ada_b, v_norm_mix_g, v_norm_ffn_g, v_ab_w_in, v_a_conv_w, v_a_conv_b, v_a_norm_g, v_a_norm_b, v_b_norm_g,
          v_b_norm_b, v_b_w_s, v_b_bias, v_ab_w_out, v_pool_w, v_pool_scale, v_ffn_w1, v_ffn_w3, v_ffn_w2, v_final_g]
    grads = [g_ada_w, g_ada_b, g_norm_mix, g_norm_ffn, None, g_conv_w, g_conv_b, g_a_norm_g, g_a_norm_b, g_b_norm_g,
             g_b_norm_b, g_w_s, g_bias, None, None, g_pool_scale, None, None, None, g_final_g]
    updates = [None] * len(weights)

    def update(k):
        grads[k] = grads[k].reshape(weights[k].shape)
        updates[k] = _adamw(weights[k], grads[k], ms[k], vs[k])

    advance(g_ada_w)
    for k in range(1, len(weights)):
        if grads[k] is not None:
            update(k)
    update(0)
    advance(updates[0][0])
    big_at = {"w_in": 4, "w_out": 13, "pool": 14, "w1": 16, "w3": 17, "w2": 18}
    assembled = _pair_assemble([shard_grads[nm] for nm in grad_names], [shard_of[nm][1] for nm in grad_names])
    for nm, g in zip(grad_names, assembled):
        grads[big_at[nm]] = g
        update(big_at[nm])
    return (loss, grad_x, *grads, *[u[0] for u in updates], *[u[1] for u in updates], *[u[2] for u in updates])
```

```python
import functools

import jax
import jax.numpy as jnp
from jax import lax
from jax.experimental import pallas as pl
from jax.experimental.pallas import tpu as pltpu

F32 = jnp.float32
BF16 = jnp.bfloat16
EPS = 1e-6
N_DEV = 8
N_CHIP = 4
MESH_AXES = ("x", "y", "c")
MESH = pl.DeviceIdType.MESH
V7X_VMEM_LIMIT_BYTES = 56 * 1024 * 1024
SUBLANES = 8
CONV_HALO = 32
POOL_HALO = 16
POOL_WINDOWS = (2, 4, 8, 16)
CHUNK = 128
ADAM_LR, ADAM_B1, ADAM_B2, ADAM_EPS, ADAM_WD, ADAM_STEP = 0.001, 0.9, 0.999, 1e-08, 0.01, 10
HBM = pl.BlockSpec(memory_space=pltpu.HBM)


def _params(sem=None):
    return pltpu.CompilerParams(dimension_semantics=sem, vmem_limit_bytes=V7X_VMEM_LIMIT_BYTES)


def _pick(n, prefs):
    for p in prefs:
        if p <= n and n % p == 0:
            return p
    return n


def _row_tile(rows, cols, target):
    if rows * cols <= target:
        return rows
    best = None
    for d in range(16, rows, 16):
        if rows % d == 0 and d * cols <= target:
            best = d
    return best if best is not None else rows


def _sds(shape, dtype):
    return jax.ShapeDtypeStruct(tuple(shape), dtype)


def _mm_core(name, grid, nk, a_list, b_list, a_spec, b_spec, dn, extras, extra_specs, out_shapes, out_specs,
             acc_shape, epilogue):
    n, n_ex, n_out = len(a_list), len(extras), len(out_shapes)

    def body(*refs):
        a_refs, b_refs = refs[:n], refs[n:2 * n]
        ex = refs[2 * n:2 * n + n_ex]
        outs = refs[2 * n + n_ex:2 * n + n_ex + n_out]
        accs = refs[2 * n + n_ex + n_out:]
        ps = [lax.dot_general(a[...], b[...], dn, preferred_element_type=F32) for a, b in zip(a_refs, b_refs)]
        if nk == 1:
            epilogue(ps, ex, outs)
            return
        k = pl.program_id(2)

        @pl.when(k == 0)
        def _():
            for acc, p in zip(accs, ps):
                acc[...] = p

        @pl.when(k > 0)
        def _():
            for acc, p in zip(accs, ps):
                acc[...] += p

        @pl.when(k == nk - 1)
        def _():
            epilogue([acc[...] for acc in accs], ex, outs)

    scratch = [] if nk == 1 else [pltpu.VMEM(acc_shape, F32) for _ in range(n)]
    return pl.pallas_call(
        body, grid=grid, name=name,
        in_specs=[a_spec] * n + [b_spec] * n + list(extra_specs),
        out_specs=list(out_specs), out_shape=list(out_shapes), scratch_shapes=scratch,
        compiler_params=_params(("parallel", "parallel", "arbitrary")),
    )(*a_list, *b_list, *extras)


def _mm(name, a_list, b_list, out_dtypes, epilogue, *, trans_a=False, trans_b=False, extras=(), extra_kinds=(),
        tm=1024, tn=1024, tk=1024):
    layer = None
    if isinstance(b_list[0], tuple):
        layer = b_list[0][1]
        b_list = [b for b, _ in b_list]
    a0 = a_list[0]
    b_shape = b_list[0].shape[-2:]
    m, kk = (a0.shape[1], a0.shape[0]) if trans_a else a0.shape
    nn = b_shape[0] if trans_b else b_shape[1]
    tm, tn, tk = _pick(m, (tm, 512, 256, 128)), _pick(nn, (tn, 512, 256, 128)), _pick(kk, (tk, 512, 256, 128))
    nk = kk // tk
    a_spec = pl.BlockSpec((tk, tm), lambda i, j, k: (k, i)) if trans_a else pl.BlockSpec((tm, tk), lambda i, j, k: (i, k))
    if layer is None:
        b_spec = pl.BlockSpec((tn, tk), lambda i, j, k: (j, k)) if trans_b else pl.BlockSpec((tk, tn), lambda i, j, k: (k, j))
    elif trans_b:
        b_spec = pl.BlockSpec((None, tn, tk), lambda i, j, k: (layer, j, k))
    else:
        b_spec = pl.BlockSpec((None, tk, tn), lambda i, j, k: (layer, k, j))
    dn = (((0 if trans_a else 1,), (1 if trans_b else 0,)), ((), ()))
    tile = pl.BlockSpec((tm, tn), lambda i, j, k: (i, j))
    row = pl.BlockSpec((1, tn), lambda i, j, k: (0, j))
    return _mm_core(name, (m // tm, nn // tn, nk), nk, a_list, b_list, a_spec, b_spec, dn, extras,
                    [tile if kd == "tile" else row for kd in extra_kinds],
                    [_sds((m, nn), dt) for dt in out_dtypes], [tile] * len(out_dtypes), (tm, tn), epilogue)


def _ep_store(ps, ex, outs):
    outs[0][...] = ps[0].astype(outs[0].dtype)


def _ep_sum(ps, ex, outs):
    outs[0][...] = (ps[0] + ps[1]).astype(outs[0].dtype)


def _ep_swiglu(ps, ex, outs):
    u, t = ps
    sg = jax.nn.sigmoid(u)
    su = u * sg
    outs[0][...] = (t * (sg * (1.0 + u * (1.0 - sg)))).astype(BF16)
    outs[1][...] = su.astype(BF16)
    outs[2][...] = (su * t).astype(BF16)


def _ep_residual(ps, ex, outs):
    x_ref, gate_ref = ex
    y = ps[0]
    outs[0][...] = x_ref[...] + gate_ref[...] * y
    outs[1][...] = y.astype(BF16)


def _ep_swiglu_bwd(ps, ex, outs):
    dz = ps[0]
    outs[0][...] = (dz * ex[0][...].astype(F32)).astype(BF16)
    outs[1][...] = (dz * ex[1][...].astype(F32)).astype(BF16)


def _grouped_fwd(p, w, layer, x, gate, tm):
    s, d = p.shape
    _, g, kg, ng = w.shape
    tm = _pick(s, (tm, 512, 256, 128))
    tile_a = pl.BlockSpec((tm, kg), lambda i, j, k: (i, j))
    tile_o = pl.BlockSpec((tm, ng), lambda i, j, k: (i, j))
    return _mm_core("pool_mm_fwd", (s // tm, g, 1), 1, [p], [w], tile_a,
                    pl.BlockSpec((None, None, kg, ng), lambda i, j, k: (layer, j, 0, 0)), (((1,), (0,)), ((), ())),
                    [x, gate], [tile_o, pl.BlockSpec((1, ng), lambda i, j, k: (0, j))],
                    [_sds((s, g * ng), F32), _sds((s, g * ng), BF16)], [tile_o, tile_o], None, _ep_residual)


def _grouped_dx(dy, w, layer, tm):
    s, _ = dy.shape
    _, g, kg, ng = w.shape
    tm = _pick(s, (tm, 512, 256, 128))
    return _mm_core("pool_mm_dx", (s // tm, g, 1), 1, [dy], [w], pl.BlockSpec((tm, ng), lambda i, j, k: (i, j)),
                    pl.BlockSpec((None, None, kg, ng), lambda i, j, k: (layer, j, 0, 0)), (((1,), (1,)), ((), ())), [], [],
                    [_sds((s, g * kg), BF16)], [pl.BlockSpec((tm, kg), lambda i, j, k: (i, j))], None, _ep_store)[0]


def _grouped_dw(p, dy, groups, tk):
    s, d = p.shape
    kg = d // groups
    ng = dy.shape[1] // groups
    tk = _pick(s, (tk, 512, 256, 128))
    nk = s // tk

    def ep(ps, ex, outs):
        outs[0][...] = ps[0].astype(BF16)

    return _mm_core("pool_mm_dw", (groups, 1, nk), nk, [p], [dy], pl.BlockSpec((tk, kg), lambda i, j, k: (k, i)),
                    pl.BlockSpec((tk, ng), lambda i, j, k: (k, i)), (((0,), (0,)), ((), ())), [], [],
                    [_sds((groups, kg, ng), BF16)], [pl.BlockSpec((None, kg, ng), lambda i, j, k: (i, 0, 0))],
                    (kg, ng), ep)[0]


def _rms_rstd(xv):
    return lax.rsqrt(jnp.mean(xv * xv, axis=-1, keepdims=True) + EPS)


def _norm_mod_math(xv, g, sh, sc):
    return ((xv * _rms_rstd(xv)) * g) * (1.0 + sc) + sh


def _norm_mod_bwd_math(dh, xv, dxo, g, sc, acc_ref):
    r = _rms_rstd(xv)
    xhat = xv * r
    acc_ref[0:1, :] += jnp.sum(dh, axis=0, keepdims=True)
    acc_ref[1:2, :] += jnp.sum(dh * (xhat * g), axis=0, keepdims=True)
    dhn = dh * (1.0 + sc)
    acc_ref[2:3, :] += jnp.sum(dhn * xhat, axis=0, keepdims=True)
    dxh = dhn * g
    return dxo + r * (dxh - xhat * jnp.mean(dxh * xhat, axis=-1, keepdims=True))


def _vec_spec(d):
    return pl.BlockSpec((1, d), lambda i: (0, 0))


def _acc_spec(d):
    return pl.BlockSpec((8, d), lambda i: (0, 0))


def _norm_mod(x, g, sh, sc, tm):
    s, d = x.shape

    def body(x_ref, g_ref, sh_ref, sc_ref, h_ref):
        h_ref[...] = _norm_mod_math(x_ref[...], g_ref[...], sh_ref[...], sc_ref[...]).astype(BF16)

    row = pl.BlockSpec((tm, d), lambda i: (i, 0))
    return pl.pallas_call(body, grid=(s // tm,), name="norm_mod", in_specs=[row] + [_vec_spec(d)] * 3, out_specs=row,
                          out_shape=_sds((s, d), BF16), compiler_params=_params(("parallel",)))(x, g, sh, sc)


def _gate_step(dxv, y_ref, gate_ref, dy_ref, gacc_ref):
    dy_ref[...] = (dxv * gate_ref[...]).astype(BF16)
    gacc_ref[0:1, :] += jnp.sum(dxv * y_ref[...].astype(F32), axis=0, keepdims=True)


def _norm_mod_bwd(dh, x, dxo, g, sc, tm, below=None):
    s, d = x.shape

    def body(dh_ref, x_ref, dxo_ref, g_ref, sc_ref, *rest):
        if below is None:
            dx_ref, acc_ref = rest
        else:
            y_ref, gate_ref, dx_ref, acc_ref, dy_ref, gacc_ref = rest

        @pl.when(pl.program_id(0) == 0)
        def _():
            acc_ref[...] = jnp.zeros_like(acc_ref)
            if below is not None:
                gacc_ref[...] = jnp.zeros_like(gacc_ref)

        dxv = _norm_mod_bwd_math(dh_ref[...].astype(F32), x_ref[...], dxo_ref[...], g_ref[...], sc_ref[...], acc_ref)
        dx_ref[...] = dxv
        if below is not None:
            _gate_step(dxv, y_ref, gate_ref, dy_ref, gacc_ref)

    row = pl.BlockSpec((tm, d), lambda i: (i, 0))
    extra_in = [] if below is None else [row, _vec_spec(d)]
    extra_out = [] if below is None else [row, _acc_spec(d)]
    extra_shape = [] if below is None else [_sds((s, d), BF16), _sds((8, d), F32)]
    return pl.pallas_call(body, grid=(s // tm,), name="norm_mod_bwd", in_specs=[row, row, row, _vec_spec(d), _vec_spec(d)] + extra_in,
                          out_specs=[row, _acc_spec(d)] + extra_out, out_shape=[_sds((s, d), F32), _sds((8, d), F32)] + extra_shape,
                          compiler_params=_params(("arbitrary",)))(dh, x, dxo, g, sc, *([] if below is None else below))


def _gate_bwd(dx, y, gate, tm):
    s, d = dx.shape

    def body(dx_ref, y_ref, gate_ref, dy_ref, acc_ref):
        @pl.when(pl.program_id(0) == 0)
        def _():
            acc_ref[...] = jnp.zeros_like(acc_ref)

        dxv = dx_ref[...]
        dy_ref[...] = (dxv * gate_ref[...]).astype(BF16)
        acc_ref[0:1, :] += jnp.sum(dxv * y_ref[...].astype(F32), axis=0, keepdims=True)

    row = pl.BlockSpec((tm, d), lambda i: (i, 0))
    return pl.pallas_call(body, grid=(s // tm,), name="gate_bwd", in_specs=[row, row, _vec_spec(d)],
                          out_specs=[row, _acc_spec(d)], out_shape=[_sds((s, d), BF16), _sds((8, d), F32)],
                          compiler_params=_params(("arbitrary",)))(dx, y, gate)


def _final_loss_bwd(x, g, target, tm, below):
    s, d = x.shape

    def body(x_ref, g_ref, t_ref, y_ref, gate_ref, dx_ref, acc_ref, loss_ref, dy_ref, gacc_ref):
        @pl.when(pl.program_id(0) == 0)
        def _():
            acc_ref[...] = jnp.zeros_like(acc_ref)
            loss_ref[...] = jnp.zeros_like(loss_ref)
            gacc_ref[...] = jnp.zeros_like(gacc_ref)

        xv = x_ref[...]
        gv = g_ref[...]
        r = _rms_rstd(xv)
        xhat = xv * r
        err = xhat * gv - t_ref[...]
        loss_ref[...] += (0.5 / d) * jnp.sum(err * err)
        dy = err * (1.0 / d)
        acc_ref[0:1, :] += jnp.sum(dy * xhat, axis=0, keepdims=True)
        dxh = dy * gv
        dxv = r * (dxh - xhat * jnp.mean(dxh * xhat, axis=-1, keepdims=True))
        dx_ref[...] = dxv
        _gate_step(dxv, y_ref, gate_ref, dy_ref, gacc_ref)

    row = pl.BlockSpec((tm, d), lambda i: (i, 0))
    return pl.pallas_call(body, grid=(s // tm,), name="final_loss_bwd", in_specs=[row, _vec_spec(d), row, row, _vec_spec(d)],
                          out_specs=[row, _acc_spec(d), pl.BlockSpec((8, 128), lambda i: (0, 0)), row, _acc_spec(d)],
                          out_shape=[_sds((s, d), F32), _sds((8, d), F32), _sds((8, 128), F32), _sds((s, d), BF16), _sds((8, d), F32)],
                          compiler_params=_params(("arbitrary",)))(x, g, target, *below)


def _chunks(tm, width, rb, cb):
    rb, cb = min(rb, tm), min(cb, width)
    return [(r0, c0, rb, cb) for r0 in range(0, tm, rb) for c0 in range(0, width, cb)]


def _prev_halo_map(tm, halo):
    return lambda i: (jnp.maximum(i * (tm // halo) - 1, 0), 0)


def _next_halo_map(tm, halo, s):
    return lambda i: (jnp.minimum((i + 1) * (tm // halo), s // halo - 1), 0)


def _pool_fwd(x, g, sh, sc, tm):
    s, d = x.shape
    dg = d // len(POOL_WINDOWS)

    def body(x_ref, xh_ref, g_ref, sh_ref, sc_ref, p_ref, ext_ref):
        i = pl.program_id(0)
        gv, shv, scv = g_ref[...], sh_ref[...], sc_ref[...]
        ext_ref[POOL_HALO:, :] = _norm_mod_math(x_ref[...], gv, shv, scv)
        ext_ref[0:POOL_HALO, :] = jnp.where(i == 0, 0.0, _norm_mod_math(xh_ref[...], gv, shv, scv))
        for gi, w in enumerate(POOL_WINDOWS):
            for r0, c0, rb, cb in _chunks(tm, dg, 64, 256):
                cols = pl.ds(gi * dg + c0, cb)
                tok = ext_ref[pl.ds(POOL_HALO + r0, rb), cols]
                acc = tok
                for j in range(1, w):
                    acc = acc + ext_ref[pl.ds(POOL_HALO + r0 - j, rb), cols]
                t_glob = i * tm + r0 + lax.broadcasted_iota(jnp.int32, (rb, 1), 0)
                cnt = jnp.minimum(t_glob + 1, w).astype(F32)
                p_ref[pl.ds(r0, rb), cols] = (acc / cnt - tok).astype(BF16)

    row = pl.BlockSpec((tm, d), lambda i: (i, 0))
    halo = pl.BlockSpec((POOL_HALO, d), _prev_halo_map(tm, POOL_HALO))
    return pl.pallas_call(body, grid=(s // tm,), name="pool_fwd", in_specs=[row, halo] + [_vec_spec(d)] * 3, out_specs=row,
                          out_shape=_sds((s, d), BF16), scratch_shapes=[pltpu.VMEM((tm + POOL_HALO, d), F32)],
                          compiler_params=_params(("parallel",)))(x, x, g, sh, sc)


def _pool_bwd(dp, x, dxo, g, sc, tm, below):
    s, d = x.shape
    dg = d // len(POOL_WINDOWS)
    n_tiles = s // tm

    def body(dp_ref, dph_ref, x_ref, dxo_ref, g_ref, sc_ref, y_ref, gate_ref, dx_ref, acc_ref, dy_ref, gacc_ref, ext_ref, dh_ref):
        i = pl.program_id(0)

        @pl.when(i == 0)
        def _():
            acc_ref[...] = jnp.zeros_like(acc_ref)
            gacc_ref[...] = jnp.zeros_like(gacc_ref)

        for gi, w in enumerate(POOL_WINDOWS):
            cols = pl.ds(gi * dg, dg)
            t_main = i * tm + lax.broadcasted_iota(jnp.int32, (tm, 1), 0)
            ext_ref[0:tm, cols] = dp_ref[:, cols].astype(F32) / jnp.minimum(t_main + 1, w).astype(F32)
            ext_ref[tm:, cols] = jnp.where(i == n_tiles - 1, 0.0, dph_ref[:, cols].astype(F32) * (1.0 / w))
            for r0, c0, rb, cb in _chunks(tm, dg, 64, 256):
                cc = pl.ds(gi * dg + c0, cb)
                acc = ext_ref[pl.ds(r0, rb), cc]
                for j in range(1, w):
                    acc = acc + ext_ref[pl.ds(r0 + j, rb), cc]
                dh_ref[pl.ds(r0, rb), cc] = acc - dp_ref[pl.ds(r0, rb), cc].astype(F32)
        dxv = _norm_mod_bwd_math(dh_ref[...], x_ref[...], dxo_ref[...], g_ref[...], sc_ref[...], acc_ref)
        dx_ref[...] = dxv
        _gate_step(dxv, y_ref, gate_ref, dy_ref, gacc_ref)

    row = pl.BlockSpec((tm, d), lambda i: (i, 0))
    halo = pl.BlockSpec((POOL_HALO, d), _next_halo_map(tm, POOL_HALO, s))
    return pl.pallas_call(body, grid=(n_tiles,), name="pool_bwd",
                          in_specs=[row, halo, row, row, _vec_spec(d), _vec_spec(d), row, _vec_spec(d)],
                          out_specs=[row, _acc_spec(d), row, _acc_spec(d)],
                          out_shape=[_sds((s, d), F32), _sds((8, d), F32), _sds((s, d), BF16), _sds((8, d), F32)],
                          scratch_shapes=[pltpu.VMEM((tm + POOL_HALO, d), F32), pltpu.VMEM((tm, d), F32)],
                          compiler_params=_params(("arbitrary",)))(dp, dp, x, dxo, g, sc, *below)


def _layernorm_fwd(v, g, b):
    mu = jnp.mean(v, axis=-1, keepdims=True)
    xc = v - mu
    rstd = lax.rsqrt(jnp.mean(xc * xc, axis=-1, keepdims=True) + EPS)
    yn = xc * rstd
    return yn * g + b, yn, rstd


def _layernorm_bwd(dz, yn, rstd, g):
    dyn = dz * g
    dv = rstd * (dyn - jnp.mean(dyn, axis=-1, keepdims=True) - yn * jnp.mean(dyn * yn, axis=-1, keepdims=True))
    return dv, jnp.sum(dz * yn, axis=0, keepdims=True), jnp.sum(dz, axis=0, keepdims=True)


def _fill_shifted(ext_ref, sh_ref):
    n = ext_ref.shape[0]
    for r in range(1, SUBLANES):
        sh_ref[r - 1, 0:n - SUBLANES, :] = ext_ref[pl.ds(r, n - SUBLANES), :]


def _shifted(ext_ref, sh_ref, start, rows, cols):
    q, r = divmod(start, SUBLANES)
    if r == 0:
        return ext_ref[pl.ds(start, rows), cols]
    return sh_ref[r - 1, pl.ds(q * SUBLANES, rows), cols]


def _ab_mid_fwd(proj, conv_w, conv_b, a_g, a_b, v_g, v_b, w_c, bias_full, tm):
    s = proj.shape[0]
    da = conv_b.shape[1]
    db = v_g.shape[1]
    nh = w_c.shape[0]
    kw = conv_w.shape[0]
    lead = CONV_HALO - (kw - 1)

    def body(p_ref, ph_ref, cw_ref, cb_ref, ag_ref, ab_ref, vg_ref, vb_ref, wc_ref, bias_ref, cat_ref, a1_ref, ext_ref,
             a1s_ref, sh_ref):
        i = pl.program_id(0)
        val = p_ref[:, 0:da].astype(F32)
        gat = p_ref[:, da:2 * da].astype(F32)
        ext_ref[CONV_HALO:, :] = val * jax.nn.sigmoid(gat)
        hv = ph_ref[:, 0:da].astype(F32)
        hg = ph_ref[:, da:2 * da].astype(F32)
        ext_ref[0:CONV_HALO, :] = jnp.where(i == 0, 0.0, hv * jax.nn.sigmoid(hg))
        _fill_shifted(ext_ref, sh_ref)
        for r0, c0, rb, cb in _chunks(tm, da, 64, 256):
            cols = pl.ds(c0, cb)
            acc = jnp.broadcast_to(cb_ref[:, cols], (rb, cb))
            for k in range(kw):
                acc = acc + cw_ref[k:k + 1, cols] * _shifted(ext_ref, sh_ref, r0 + lead + k, rb, cols)
            a1s_ref[pl.ds(r0, rb), cols] = acc
        a1 = a1s_ref[...]
        a1_ref[...] = a1.astype(BF16)
        z, _, _ = _layernorm_fwd(a1, ag_ref[...], ab_ref[...])
        cat_ref[:, 0:da] = (z * jax.nn.sigmoid(z)).astype(BF16)

        bu = p_ref[:, 2 * da:2 * da + db].astype(F32)
        bv = p_ref[:, 2 * da + db:].astype(F32)
        vn, _, _ = _layernorm_fwd(bv, vg_ref[...], vb_ref[...])
        vnb = vn.astype(BF16)
        for n in range(tm // CHUNK):
            rows = slice(n * CHUNK, (n + 1) * CHUNK)
            for h in range(nh):
                hc = slice(h * CHUNK, (h + 1) * CHUNK)
                vo = jnp.dot(wc_ref[h], vnb[rows, hc], preferred_element_type=F32) + bias_ref[:, hc]
                cat_ref[rows, da + h * CHUNK:da + (h + 1) * CHUNK] = (bu[rows, hc] * vo).astype(BF16)

    full = lambda a: pl.BlockSpec(a.shape, lambda i: (0,) * a.ndim)
    return pl.pallas_call(
        body, grid=(s // tm,), name="ab_mid_fwd",
        in_specs=[pl.BlockSpec((tm, 2 * da + 2 * db), lambda i: (i, 0)),
                  pl.BlockSpec((CONV_HALO, 2 * da), _prev_halo_map(tm, CONV_HALO)),
                  full(conv_w), full(conv_b), full(a_g), full(a_b), full(v_g), full(v_b), full(w_c), full(bias_full)],
        out_specs=[pl.BlockSpec((tm, da + db), lambda i: (i, 0)), pl.BlockSpec((tm, da), lambda i: (i, 0))],
        out_shape=[_sds((s, da + db), BF16), _sds((s, da), BF16)],
        scratch_shapes=[pltpu.VMEM((tm + CONV_HALO, da), F32), pltpu.VMEM((tm, da), F32),
                        pltpu.VMEM((SUBLANES - 1, tm + CONV_HALO, da), F32)],
        compiler_params=_params(("parallel",)),
    )(proj, proj, conv_w, conv_b, a_g, a_b, v_g, v_b, w_c, bias_full)


def _ab_mid_bwd(dcat, proj, a1, conv_w, a_g, a_b, v_g, v_b, w_c, bias_full, tm):
    s = proj.shape[0]
    da = a_g.shape[1]
    db = v_g.shape[1]
    nh = w_c.shape[0]
    kw = conv_w.shape[0]
    lead = CONV_HALO - (kw - 1)
    n_tiles = s // tm

    def body(dc_ref, dch_ref, p_ref, ph_ref, a1_ref, a1h_ref, cw_ref, ag_ref, ab_ref, vg_ref, vb_ref, wc_ref, bias_ref,
             dp_ref, dcw_ref, vec_ref, dws_ref, dbias_ref, ext_ref, dext_ref, dcw_acc, dvn_ref, sh_ref, dsh_ref):
        i = pl.program_id(0)

        @pl.when(i == 0)
        def _():
            dcw_acc[...] = jnp.zeros_like(dcw_acc)
            vec_ref[...] = jnp.zeros_like(vec_ref)
            dws_ref[...] = jnp.zeros_like(dws_ref)
            dbias_ref[...] = jnp.zeros_like(dbias_ref)

        agv, abv = ag_ref[...], ab_ref[...]

        def silu_ln_bwd(a1v, d_a2):
            z, yn, rstd = _layernorm_fwd(a1v, agv, abv)
            sg = jax.nn.sigmoid(z)
            return _layernorm_bwd(d_a2 * (sg * (1.0 + z * (1.0 - sg))), yn, rstd, agv)

        d_a1, dga, dba = silu_ln_bwd(a1_ref[...].astype(F32), dc_ref[:, 0:da].astype(F32))
        dext_ref[0:tm, :] = d_a1
        d_a1h, _, _ = silu_ln_bwd(a1h_ref[...].astype(F32), dch_ref[...].astype(F32))
        dext_ref[tm:, :] = jnp.where(i == n_tiles - 1, 0.0, d_a1h)
        vec_ref[0:1, 0:da] += dga
        vec_ref[1:2, 0:da] += dba
        vec_ref[2:3, 0:da] += jnp.sum(d_a1, axis=0, keepdims=True)

        val = p_ref[:, 0:da].astype(F32)
        sgg = jax.nn.sigmoid(p_ref[:, da:2 * da].astype(F32))
        ext_ref[CONV_HALO:, :] = val * sgg
        hv = ph_ref[:, 0:da].astype(F32)
        hg = ph_ref[:, da:2 * da].astype(F32)
        ext_ref[0:CONV_HALO, :] = jnp.where(i == 0, 0.0, hv * jax.nn.sigmoid(hg))

        _fill_shifted(ext_ref, sh_ref)
        _fill_shifted(dext_ref, dsh_ref)
        for r0, c0, rb, cb in _chunks(tm, da, 64, 256):
            cols = pl.ds(c0, cb)
            rows = pl.ds(r0, rb)
            d1 = dext_ref[rows, cols]
            acc = jnp.zeros((rb, cb), F32)
            for k in range(kw):
                acc = acc + cw_ref[k:k + 1, cols] * _shifted(dext_ref, dsh_ref, r0 + (kw - 1) - k, rb, cols)
                prod = d1 * _shifted(ext_ref, sh_ref, r0 + lead + k, rb, cols)
                dcw_acc[k, :, cols] += jnp.sum(prod.reshape(rb // SUBLANES, SUBLANES, cb), axis=0)
            v = p_ref[rows, pl.ds(c0, cb)].astype(F32)
            sg = jax.nn.sigmoid(p_ref[rows, pl.ds(da + c0, cb)].astype(F32))
            dp_ref[rows, pl.ds(c0, cb)] = (acc * sg).astype(BF16)
            dp_ref[rows, pl.ds(da + c0, cb)] = (acc * v * sg * (1.0 - sg)).astype(BF16)

        vgv = vg_ref[...]
        bu = p_ref[:, 2 * da:2 * da + db].astype(F32)
        bv = p_ref[:, 2 * da + db:].astype(F32)
        vn, yn_v, rstd_v = _layernorm_fwd(bv, vgv, vb_ref[...])
        vnb = vn.astype(BF16)
        for n in range(tm // CHUNK):
            rows = slice(n * CHUNK, (n + 1) * CHUNK)
            for h in range(nh):
                hc = slice(h * CHUNK, (h + 1) * CHUNK)
                wch = wc_ref[h]
                blk = vnb[rows, hc]
                vo = jnp.dot(wch, blk, preferred_element_type=F32) + bias_ref[:, hc]
                d_bout = dc_ref[rows, da + h * CHUNK:da + (h + 1) * CHUNK].astype(F32)
                dp_ref[rows, 2 * da + h * CHUNK:2 * da + (h + 1) * CHUNK] = (d_bout * vo).astype(BF16)
                d_vo = d_bout * bu[rows, hc]
                dbias_ref[h] += jnp.sum(d_vo, axis=1, keepdims=True)
                d_vob = d_vo.astype(BF16)
                dws_ref[h] += lax.dot_general(d_vob, blk, (((1,), (1,)), ((), ())), preferred_element_type=F32)
                dvn_ref[rows, hc] = lax.dot_general(wch, d_vob, (((0,), (0,)), ((), ())), preferred_element_type=F32)
        d_bv, dgv, dbv = _layernorm_bwd(dvn_ref[...], yn_v, rstd_v, vgv)
        dp_ref[:, 2 * da + db:] = d_bv.astype(BF16)
        vec_ref[3:4, 0:db] += dgv
        vec_ref[4:5, 0:db] += dbv

        @pl.when(i == n_tiles - 1)
        def _():
            dcw_ref[...] = jnp.sum(dcw_acc[...], axis=1)
            causal = lax.broadcasted_iota(jnp.int32, (CHUNK, CHUNK), 0) >= lax.broadcasted_iota(jnp.int32, (CHUNK, CHUNK), 1)
            for h in range(nh):
                dws_ref[h] = jnp.where(causal, dws_ref[h], 0.0)

    full = lambda a: pl.BlockSpec(a.shape, lambda i: (0,) * a.ndim)
    wide = max(da, db)
    return pl.pallas_call(
        body, grid=(n_tiles,), name="ab_mid_bwd",
        in_specs=[pl.BlockSpec((tm, da + db), lambda i: (i, 0)),
                  pl.BlockSpec((CONV_HALO, da), _next_halo_map(tm, CONV_HALO, s)),
                  pl.BlockSpec((tm, 2 * da + 2 * db), lambda i: (i, 0)),
                  pl.BlockSpec((CONV_HALO, 2 * da), _prev_halo_map(tm, CONV_HALO)),
                  pl.BlockSpec((tm, da), lambda i: (i, 0)),
                  pl.BlockSpec((CONV_HALO, da), _next_halo_map(tm, CONV_HALO, s)),
                  full(conv_w), full(a_g), full(a_b), full(v_g), full(v_b), full(w_c), full(bias_full)],
        out_specs=[pl.BlockSpec((tm, 2 * da + 2 * db), lambda i: (i, 0)),
                   pl.BlockSpec((kw, da), lambda i: (0, 0)),
                   pl.BlockSpec((8, wide), lambda i: (0, 0)),
                   pl.BlockSpec((nh, CHUNK, CHUNK), lambda i: (0, 0, 0)),
                   pl.BlockSpec((nh, CHUNK, 1), lambda i: (0, 0, 0))],
        out_shape=[_sds((s, 2 * da + 2 * db), BF16), _sds((kw, da), F32), _sds((8, wide), F32),
                   _sds((nh, CHUNK, CHUNK), F32), _sds((nh, CHUNK, 1), F32)],
        scratch_shapes=[pltpu.VMEM((tm + CONV_HALO, da), F32), pltpu.VMEM((tm + CONV_HALO, da), F32),
                        pltpu.VMEM((kw, SUBLANES, da), F32), pltpu.VMEM((tm, db), F32),
                        pltpu.VMEM((SUBLANES - 1, tm + CONV_HALO, da), F32), pltpu.VMEM((SUBLANES - 1, tm + CONV_HALO, da), F32)],
        compiler_params=_params(("arbitrary",)),
    )(dcat, dcat, proj, proj, a1, a1, conv_w, a_g, a_b, v_g, v_b, w_c, bias_full)


def _ada_fwd(c_all, w, b):
    nl, d, n = w.shape
    tn = _pick(n, (512, 256, 128))

    def body(c_ref, w_ref, b_ref, o_ref):
        cv = c_ref[...]
        cond = (cv * jax.nn.sigmoid(cv)).astype(BF16)
        o_ref[...] = jnp.dot(cond, w_ref[...].astype(BF16), preferred_element_type=F32) + b_ref[...]

    return pl.pallas_call(
        body, grid=(nl, n // tn), name="ada_fwd",
        in_specs=[pl.BlockSpec(c_all.shape, lambda l, j: (0, 0)), pl.BlockSpec((None, d, tn), lambda l, j: (l, 0, j)),
                  pl.BlockSpec((None, 1, tn), lambda l, j: (l, 0, j))],
        out_specs=pl.BlockSpec((None, c_all.shape[0], tn), lambda l, j: (l, 0, j)),
        out_shape=_sds((nl, c_all.shape[0], n), F32), compiler_params=_params(("parallel", "parallel")),
    )(c_all, w, b)


def _ada_bwd(c_all, dmod):
    nl, nb, n = dmod.shape
    d = c_all.shape[1]
    tn = _pick(n, (512, 256, 128))

    def body(c_ref, g_ref, o_ref):
        cv = c_ref[...]
        cond = (cv * jax.nn.sigmoid(cv)).astype(BF16)
        o_ref[...] = lax.dot_general(cond, g_ref[...].astype(BF16), (((0,), (0,)), ((), ())), preferred_element_type=F32)

    return pl.pallas_call(
        body, grid=(nl, n // tn), name="ada_bwd",
        in_specs=[pl.BlockSpec(c_all.shape, lambda l, j: (0, 0)), pl.BlockSpec((None, nb, tn), lambda l, j: (l, 0, j))],
        out_specs=pl.BlockSpec((None, d, tn), lambda l, j: (l, 0, j)),
        out_shape=_sds((nl, d, n), F32), compiler_params=_params(("parallel", "parallel")),
    )(c_all, dmod)


def _sum_leading(a, out_dtype=F32, name="sum_leading"):
    n, r, c = a.shape
    tr = _row_tile(r, c, 256 * 1024)

    def body(a_ref, o_ref):
        acc = a_ref[0].astype(F32)
        for k in range(1, n):
            acc = acc + a_ref[k].astype(F32)
        o_ref[...] = acc.astype(out_dtype)

    return pl.pallas_call(body, grid=(r // tr,), name=name, in_specs=[pl.BlockSpec((n, tr, c), lambda i: (0, i, 0))],
                          out_specs=pl.BlockSpec((tr, c), lambda i: (i, 0)), out_shape=_sds((r, c), out_dtype),
                          compiler_params=_params(("parallel",)))(a)


def _cast_into_full(w, kind, s_vec, l0, nl):
    _, r, c = w.shape
    tr = _row_tile(r, c, 512 * 1024)
    nb = r // tr
    if kind == "col":
        out_shape, out_spec = (nl, r, N_CHIP * c), pl.BlockSpec((None, tr, c), lambda l, i, sv: (l, i, sv[0]))
    else:
        out_shape, out_spec = (nl, N_CHIP * r, c), pl.BlockSpec((None, tr, c), lambda l, i, sv: (l, sv[0] * nb + i, 0))

    def body(sv_ref, w_ref, o_ref):
        o_ref[...] = w_ref[...].astype(BF16)

    return pl.pallas_call(
        body, name="cast_into_full",
        grid_spec=pltpu.PrefetchScalarGridSpec(num_scalar_prefetch=1, grid=(nl, nb),
                                               in_specs=[pl.BlockSpec((None, tr, c), lambda l, i, sv: (l0 + l, i, 0))], out_specs=out_spec),
        out_shape=_sds(out_shape, BF16), compiler_params=_params(("parallel", "parallel")),
    )(s_vec, w)


def _chip_sum_into(slab, cp, g, g_shape, layer, kind, sc_vec, rows_per_group=None):
    n, r, c = slab.shape
    rg = r if rows_per_group is None else rows_per_group
    tr = _row_tile(rg, c, 256 * 1024)
    groups = r // rg
    nbg = rg // tr
    nb = groups * nbg
    n_sc = len(sc_vec)
    if kind == "col":
        out_spec = pl.BlockSpec((None, tr, c), lambda gi, i, *sc: (layer, sc[1][0] * nb + gi * nbg + i, 0))
        own_spec = pl.BlockSpec((tr, c), lambda gi, i, *sc: (gi * nbg + i, sc[0][0]))
    else:
        out_spec = pl.BlockSpec((None, tr, c), lambda gi, i, *sc: (layer, gi * nbg + i, sc[1][0]))
        own_spec = pl.BlockSpec((tr, c), lambda gi, i, *sc: (gi * (n * nbg) + sc[0][0] * nbg + i, 0))

    def other(k):
        return pl.BlockSpec((None, tr, c), lambda gi, i, *sc: (sc[1 + k][0], gi * nbg + i, 0))

    in_specs = [own_spec] + [other(k) for k in range(1, n)]
    args = list(sc_vec) + [cp] + [slab] * (n - 1)
    aliases = {}
    if g is not None:
        in_specs.append(pl.BlockSpec(memory_space=pl.ANY))
        args.append(g)
        aliases = {len(args) - 1: 0}

    def body(*refs):
        own_ref, rest = refs[n_sc], refs[n_sc + 1:]
        o_ref = rest[-1]
        acc = own_ref[...].astype(F32)
        for k in range(n - 1):
            acc = acc + rest[k][...].astype(F32)
        o_ref[...] = acc

    return pl.pallas_call(
        body, name="chip_sum",
        grid_spec=pltpu.PrefetchScalarGridSpec(num_scalar_prefetch=n_sc, grid=(groups, nbg), in_specs=in_specs, out_specs=out_spec),
        out_shape=_sds(g_shape, F32), input_output_aliases=aliases, compiler_params=_params(("parallel", "parallel")),
    )(*args)


def _pair_add(dw, sib, kind, c_vec):
    r, c = sib.shape
    tr = _row_tile(r, c, 512 * 1024)
    nb = r // tr
    if kind == "col":
        dw_spec = pl.BlockSpec((tr, c), lambda i, cv: (cv[0] * nb + i, 0))
    else:
        dw_spec = pl.BlockSpec((tr, c), lambda i, cv: (i, cv[0]))

    def body(cv_ref, dw_ref, sib_ref, o_ref):
        o_ref[...] = (dw_ref[...].astype(F32) + sib_ref[...].astype(F32)).astype(BF16)

    return pl.pallas_call(
        body, name="pair_add",
        grid_spec=pltpu.PrefetchScalarGridSpec(num_scalar_prefetch=1, grid=(nb,), in_specs=[dw_spec, pl.BlockSpec((tr, c), lambda i, cv: (i, 0))],
                                               out_specs=pl.BlockSpec((tr, c), lambda i, cv: (i, 0))),
        out_shape=_sds((r, c), BF16), compiler_params=_params(("parallel",)),
    )(c_vec, dw, sib)


def _adamw(w, g, m, v):
    shape = w.shape
    cols = shape[-1]
    rows = w.size // cols
    tr = _row_tile(rows, cols, 256 * 1024)
    bc1 = 1.0 - ADAM_B1 ** ADAM_STEP
    bc2 = 1.0 - ADAM_B2 ** ADAM_STEP

    def body(w_ref, g_ref, m_ref, v_ref, d_ref, mo_ref, vo_ref):
        gv = g_ref[...]
        mn = ADAM_B1 * m_ref[...] + (1.0 - ADAM_B1) * gv
        vn = ADAM_B2 * v_ref[...] + (1.0 - ADAM_B2) * (gv * gv)
        d_ref[...] = -ADAM_LR * ((mn / bc1) / (jnp.sqrt(vn / bc2) + ADAM_EPS) + ADAM_WD * w_ref[...])
        mo_ref[...] = mn
        vo_ref[...] = vn

    spec = pl.BlockSpec((tr, cols), lambda i: (i, 0))
    outs = pl.pallas_call(body, grid=(rows // tr,), name="adamw", in_specs=[spec] * 4, out_specs=[spec] * 3,
                          out_shape=[_sds((rows, cols), F32)] * 3, compiler_params=_params(("parallel",)))(
        *[a.reshape(rows, cols) for a in (w, g, m, v)])
    return [o.reshape(shape) for o in outs]


def _mesh_pos():
    return lax.axis_index("x"), lax.axis_index("y"), lax.axis_index("c")


def _other_chips(x, y):
    return [(1 - x, y), (x, 1 - y), (1 - x, 1 - y)]


def _allgather_small(a, after=()):
    r, c = a.shape

    def body(x_ref, *rest):
        out_ref, send_sems, recv_sems, local_sem = rest[len(after):]
        x, y, cc = _mesh_pos()
        me, sibling = (x, y, cc), (x, y, 1 - cc)
        chips = _other_chips(x, y)

        def slab(px, py, pc):
            return out_ref.at[4 * px + 2 * py + pc]

        def copy(k, block, to, src=None):
            return pltpu.make_async_remote_copy(src_ref=slab(*block) if src is None else src, dst_ref=slab(*block),
                                                send_sem=send_sems.at[k], recv_sem=recv_sems.at[k], device_id=to,
                                                device_id_type=MESH)

        mine = pltpu.make_async_copy(x_ref, slab(*me), local_sem)
        mine.start()
        first = [copy(0, me, sibling, src=x_ref)]
        first += [copy(1 + j, me, (*chip, cc), src=x_ref) for j, chip in enumerate(chips)]
        for cp in first:
            cp.start()
        passed = [copy(4 + j, (*chip, cc), sibling) for j, chip in enumerate(chips)]
        for j, chip in enumerate(chips):
            copy(1 + j, (*chip, cc), me).wait_recv()
            passed[j].start()
        copy(0, sibling, me).wait_recv()
        for j, chip in enumerate(chips):
            copy(4 + j, (*chip, 1 - cc), me).wait_recv()
        for cp in first + passed:
            cp.wait_send()
        mine.wait()

    return pl.pallas_call(
        body, name="allgather_small", out_shape=_sds((N_DEV, r, c), F32),
        in_specs=[pl.BlockSpec(memory_space=pltpu.VMEM)] + [pl.BlockSpec(memory_space=pl.ANY)] * len(after),
        out_specs=pl.BlockSpec(memory_space=pltpu.VMEM),
        scratch_shapes=[pltpu.SemaphoreType.DMA((7,)), pltpu.SemaphoreType.DMA((7,)), pltpu.SemaphoreType.DMA],
        compiler_params=pltpu.CompilerParams(vmem_limit_bytes=V7X_VMEM_LIMIT_BYTES),
    )(a, *after)


def _idx(ref, rows=None, cols=None):
    lead = (slice(None),) * (len(ref.shape) - 2)
    return ref.at[lead + (slice(None) if rows is None else rows, slice(None) if cols is None else cols)]


def _half(ref, kind, c):
    r, cdim = ref.shape[-2:]
    if kind == "col":
        return _idx(ref, rows=pl.ds(c * (r // 2), r // 2))
    return _idx(ref, cols=pl.ds(c * (cdim // 2), cdim // 2))


def _shard_region(full, kind, shard_shape, s):
    r, cdim = shard_shape[-2:]
    if kind == "col":
        return _idx(full, cols=pl.ds(s * cdim, cdim))
    return _idx(full, rows=pl.ds(s * r, r))


def _allgather_weights(fulls, kinds, shard_shapes):
    nt = len(fulls)

    def body(*refs):
        fu = refs[nt:2 * nt]
        send_sems, recv_sems = refs[2 * nt:]
        x, y, c = _mesh_pos()
        chips = _other_chips(x, y)

        def part(t, chip, cc):
            return _half(_shard_region(fu[t], kinds[t], shard_shapes[t], 2 * chip[0] + chip[1]), kinds[t], cc)

        def copy(t, k, blk, to):
            return pltpu.make_async_remote_copy(src_ref=blk, dst_ref=blk, send_sem=send_sems.at[6 * t + k],
                                                recv_sem=recv_sems.at[6 * t + k], device_id=to, device_id_type=MESH)

        first, passed = [], []
        for t in range(nt):
            for j, chip in enumerate(chips):
                cp = copy(t, j, part(t, (x, y), c), (*chip, c))
                cp.start()
                first.append(cp)
        for t in range(nt):
            for j, chip in enumerate(chips):
                copy(t, j, part(t, chip, c), (x, y, c)).wait_recv()
                fw = copy(t, 3 + j, part(t, chip, c), (x, y, 1 - c))
                fw.start()
                passed.append(fw)
        for t in range(nt):
            for j, chip in enumerate(chips):
                copy(t, 3 + j, part(t, chip, 1 - c), (x, y, c)).wait_recv()
        for cp in first + passed:
            cp.wait_send()

    return pl.pallas_call(
        body, name="allgather_weights", out_shape=[_sds(f.shape, BF16) for f in fulls],
        in_specs=[HBM] * nt, out_specs=[HBM] * nt, input_output_aliases={t: t for t in range(nt)},
        scratch_shapes=[pltpu.SemaphoreType.DMA((6 * nt,)), pltpu.SemaphoreType.DMA((6 * nt,))],
    )(*fulls)


SEM = pl.BlockSpec(memory_space=pltpu.SEMAPHORE)
ANY = pl.BlockSpec(memory_space=pl.ANY)
EFFECT = pltpu.SideEffectType.DATAFLOW_SIDE_EFFECTING


def _gather_start(name, fulls, kinds, shard_shapes, prev, both=True):
    nt = len(fulls)

    def body(*refs):
        send_sems, recv_sems = refs[nt + 1], refs[nt + 2]
        fu = refs[nt + 3:2 * nt + 3]
        token = refs[2 * nt + 3]
        x, y, c = _mesh_pos()
        for t in range(nt):
            mine = _half(_shard_region(fu[t], kinds[t], shard_shapes[t], 2 * x + y), kinds[t], c)
            for j, chip in enumerate(_other_chips(x, y)):
                if both:
                    for e in range(2):
                        pltpu.make_async_remote_copy(src_ref=mine, dst_ref=mine, send_sem=send_sems.at[6 * t + 2 * j + e],
                                                     recv_sem=recv_sems.at[6 * t + 2 * j + c], device_id=(*chip, e),
                                                     device_id_type=MESH).start()
                else:
                    pltpu.make_async_remote_copy(src_ref=mine, dst_ref=mine, send_sem=send_sems.at[6 * t + 2 * j],
                                                 recv_sem=recv_sems.at[6 * t + 2 * j], device_id=(*chip, c),
                                                 device_id_type=MESH).start()
        token[...] = jnp.zeros_like(token)

    outs = pl.pallas_call(
        body, name=name,
        out_shape=(pltpu.SemaphoreType.DMA((6 * nt,)), pltpu.SemaphoreType.DMA((6 * nt,)), *[pltpu.HBM(f.shape, f.dtype) for f in fulls],
                   _sds((8, 128), F32)),
        in_specs=[HBM] * nt + [ANY], out_specs=(SEM, SEM, *[HBM] * nt, pl.BlockSpec(memory_space=pltpu.VMEM)),
        input_output_aliases={t: 2 + t for t in range(nt)}, compiler_params=pltpu.CompilerParams(has_side_effects=EFFECT),
    )(*fulls, prev)
    return outs[0], outs[1], list(outs[2:2 + nt]), outs[2 + nt]


def _forward_halves(fulls, kinds, shard_shapes):
    nt = len(fulls)

    def body(*refs):
        fu = refs[nt:2 * nt]
        send_sems, recv_sems = refs[2 * nt:]
        x, y, c = _mesh_pos()

        def part(t, chip, cc):
            return _half(_shard_region(fu[t], kinds[t], shard_shapes[t], 2 * chip[0] + chip[1]), kinds[t], cc)

        copies = []
        for t in range(nt):
            for j, chip in enumerate(_other_chips(x, y)):
                blk = part(t, chip, c)
                cp = pltpu.make_async_remote_copy(src_ref=blk, dst_ref=blk, send_sem=send_sems.at[3 * t + j],
                                                  recv_sem=recv_sems.at[3 * t + j], device_id=(x, y, 1 - c), device_id_type=MESH)
                cp.start()
                copies.append(cp)
        for t in range(nt):
            for j, chip in enumerate(_other_chips(x, y)):
                blk = part(t, chip, 1 - c)
                pltpu.make_async_remote_copy(src_ref=blk, dst_ref=blk, send_sem=send_sems.at[3 * t + j],
                                             recv_sem=recv_sems.at[3 * t + j], device_id=(x, y, c), device_id_type=MESH).wait_recv()
        for cp in copies:
            cp.wait_send()

    return pl.pallas_call(
        body, name="forward_halves", out_shape=[_sds(f.shape, BF16) for f in fulls], in_specs=[HBM] * nt, out_specs=[HBM] * nt,
        input_output_aliases={t: t for t in range(nt)},
        scratch_shapes=[pltpu.SemaphoreType.DMA((3 * nt,)), pltpu.SemaphoreType.DMA((3 * nt,))],
    )(*fulls)


def _gather_wait(name, send_sems, recv_sems, fulls, kinds, shard_shapes, after, both=True):
    nt = len(fulls)

    def body(*refs):
        fu = refs[:nt]
        send_sems, recv_sems = refs[nt], refs[nt + 1]
        x, y, c = _mesh_pos()

        def part(t, chip, cc):
            return _half(_shard_region(fu[t], kinds[t], shard_shapes[t], 2 * chip[0] + chip[1]), kinds[t], cc)

        for t in range(nt):
            for j, chip in enumerate(_other_chips(x, y)):
                if not both:
                    mine, landed = part(t, (x, y), c), part(t, chip, c)
                    pltpu.make_async_remote_copy(src_ref=mine, dst_ref=mine, send_sem=send_sems.at[6 * t + 2 * j],
                                                 recv_sem=recv_sems.at[6 * t + 2 * j], device_id=(x, y, c),
                                                 device_id_type=MESH).wait_send()
                    pltpu.make_async_remote_copy(src_ref=landed, dst_ref=landed, send_sem=send_sems.at[6 * t + 2 * j],
                                                 recv_sem=recv_sems.at[6 * t + 2 * j], device_id=(x, y, c),
                                                 device_id_type=MESH).wait_recv()
                    continue
                for e in range(2):
                    mine = part(t, (x, y), c)
                    pltpu.make_async_remote_copy(src_ref=mine, dst_ref=mine, send_sem=send_sems.at[6 * t + 2 * j + e],
                                                 recv_sem=recv_sems.at[6 * t + 2 * j + e], device_id=(x, y, c),
                                                 device_id_type=MESH).wait_send()
                    landed = part(t, chip, e)
                    pltpu.make_async_remote_copy(src_ref=landed, dst_ref=landed, send_sem=send_sems.at[6 * t + 2 * j + e],
                                                 recv_sem=recv_sems.at[6 * t + 2 * j + e], device_id=(x, y, c),
                                                 device_id_type=MESH).wait_recv()

    return pl.pallas_call(
        body, name=name, out_shape=[pltpu.HBM(f.shape, f.dtype) for f in fulls], in_specs=[HBM] * nt + [SEM, SEM, ANY],
        out_specs=[HBM] * nt, input_output_aliases={t: t for t in range(nt)},
        compiler_params=pltpu.CompilerParams(has_side_effects=EFFECT),
    )(*fulls, send_sems, recv_sems, after)


def _reduce_start(name, cps, kinds, prev):
    nt = len(cps)
    slab_shapes = []
    for cp, kind in zip(cps, kinds):
        shp = list(cp.shape)
        shp[-1 if kind == "col" else -2] //= N_CHIP
        slab_shapes.append((N_CHIP,) + tuple(shp))

    def body(*refs):
        send_sems, recv_sems = refs[nt + 1], refs[nt + 2]
        src = refs[nt + 3:2 * nt + 3]
        dst = refs[2 * nt + 3:3 * nt + 3]
        token = refs[3 * nt + 3]
        x, y, c = _mesh_pos()
        s = 2 * x + y
        for t in range(nt):
            for j, chip in enumerate(_other_chips(x, y)):
                pltpu.make_async_remote_copy(src_ref=_chip_block(src[t], kinds[t], 2 * chip[0] + chip[1]), dst_ref=dst[t].at[s],
                                             send_sem=send_sems.at[3 * t + j], recv_sem=recv_sems.at[3 * t + j],
                                             device_id=(*chip, c), device_id_type=MESH).start()
        token[...] = jnp.zeros_like(token)

    outs = pl.pallas_call(
        body, name=name,
        out_shape=(pltpu.SemaphoreType.DMA((3 * nt,)), pltpu.SemaphoreType.DMA((3 * nt,)), *[pltpu.HBM(a.shape, a.dtype) for a in cps],
                   *[pltpu.HBM(shp, BF16) for shp in slab_shapes], _sds((8, 128), F32)),
        in_specs=[HBM] * nt + [ANY], out_specs=(SEM, SEM, *[HBM] * (2 * nt), pl.BlockSpec(memory_space=pltpu.VMEM)),
        input_output_aliases={t: 2 + t for t in range(nt)}, compiler_params=pltpu.CompilerParams(has_side_effects=EFFECT),
    )(*cps, prev)
    return outs[0], outs[1], list(outs[2:2 + nt]), list(outs[2 + nt:2 + 2 * nt]), outs[2 + 2 * nt]


def _reduce_wait(name, send_sems, recv_sems, cps, slabs, kinds, after):
    nt = len(cps)

    def body(*refs):
        src, dst = refs[:nt], refs[nt:2 * nt]
        send_sems, recv_sems = refs[2 * nt], refs[2 * nt + 1]
        x, y, c = _mesh_pos()
        for t in range(nt):
            for j, chip in enumerate(_other_chips(x, y)):
                sj = 2 * chip[0] + chip[1]
                pltpu.make_async_remote_copy(src_ref=_chip_block(src[t], kinds[t], sj), dst_ref=dst[t].at[sj],
                                             send_sem=send_sems.at[3 * t + j], recv_sem=recv_sems.at[3 * t + j],
                                             device_id=(x, y, c), device_id_type=MESH).wait()

    outs = pl.pallas_call(
        body, name=name, out_shape=[pltpu.HBM(a.shape, a.dtype) for a in cps] + [pltpu.HBM(a.shape, a.dtype) for a in slabs],
        in_specs=[HBM] * (2 * nt) + [SEM, SEM, ANY], out_specs=[HBM] * (2 * nt), input_output_aliases={t: t for t in range(2 * nt)},
        compiler_params=pltpu.CompilerParams(has_side_effects=EFFECT),
    )(*cps, *slabs, send_sems, recv_sems, after)
    return list(outs[:nt]), list(outs[nt:])


def _pair_start(name, dws, kinds, prev):
    nt = len(dws)
    sib_shapes = []
    for dw, kind in zip(dws, kinds):
        shp = list(dw.shape)
        shp[-2 if kind == "col" else -1] //= 2
        sib_shapes.append(tuple(shp))

    def body(*refs):
        send_sems, recv_sems = refs[nt + 1], refs[nt + 2]
        src = refs[nt + 3:2 * nt + 3]
        dst = refs[2 * nt + 3:3 * nt + 3]
        token = refs[3 * nt + 3]
        x, y, c = _mesh_pos()
        for t in range(nt):
            pltpu.make_async_remote_copy(src_ref=_half(src[t], kinds[t], 1 - c), dst_ref=dst[t], send_sem=send_sems.at[t],
                                         recv_sem=recv_sems.at[t], device_id=(x, y, 1 - c), device_id_type=MESH).start()
        token[...] = jnp.zeros_like(token)

    outs = pl.pallas_call(
        body, name=name,
        out_shape=(pltpu.SemaphoreType.DMA((nt,)), pltpu.SemaphoreType.DMA((nt,)), *[pltpu.HBM(a.shape, a.dtype) for a in dws],
                   *[pltpu.HBM(shp, BF16) for shp in sib_shapes], _sds((8, 128), F32)),
        in_specs=[HBM] * nt + [ANY], out_specs=(SEM, SEM, *[HBM] * (2 * nt), pl.BlockSpec(memory_space=pltpu.VMEM)),
        input_output_aliases={t: 2 + t for t in range(nt)}, compiler_params=pltpu.CompilerParams(has_side_effects=EFFECT),
    )(*dws, prev)
    return outs[0], outs[1], list(outs[2:2 + nt]), list(outs[2 + nt:2 + 2 * nt]), outs[2 + 2 * nt]


def _pair_wait(name, send_sems, recv_sems, dws, sibs, kinds, after):
    nt = len(dws)

    def body(*refs):
        src, dst = refs[:nt], refs[nt:2 * nt]
        send_sems, recv_sems = refs[2 * nt], refs[2 * nt + 1]
        x, y, c = _mesh_pos()
        for t in range(nt):
            pltpu.make_async_remote_copy(src_ref=_half(src[t], kinds[t], 1 - c), dst_ref=dst[t], send_sem=send_sems.at[t],
                                         recv_sem=recv_sems.at[t], device_id=(x, y, c), device_id_type=MESH).wait()

    outs = pl.pallas_call(
        body, name=name, out_shape=[pltpu.HBM(a.shape, a.dtype) for a in dws] + [pltpu.HBM(a.shape, a.dtype) for a in sibs],
        in_specs=[HBM] * (2 * nt) + [SEM, SEM, ANY], out_specs=[HBM] * (2 * nt), input_output_aliases={t: t for t in range(2 * nt)},
        compiler_params=pltpu.CompilerParams(has_side_effects=EFFECT),
    )(*dws, *sibs, send_sems, recv_sems, after)
    return list(outs[:nt]), list(outs[nt:])


def _pair_exchange(dws, kinds):
    nt = len(dws)
    out_shapes = []
    for dw, kind in zip(dws, kinds):
        shp = list(dw.shape)
        shp[-2 if kind == "col" else -1] //= 2
        out_shapes.append(tuple(shp))

    def body(*refs):
        src, dst = refs[:nt], refs[nt:2 * nt]
        send_sems, recv_sems = refs[2 * nt:]
        x, y, c = _mesh_pos()
        copies = [pltpu.make_async_remote_copy(src_ref=_half(src[t], kinds[t], 1 - c), dst_ref=dst[t], send_sem=send_sems.at[t],
                                               recv_sem=recv_sems.at[t], device_id=(x, y, 1 - c), device_id_type=MESH)
                  for t in range(nt)]
        for cp in copies:
            cp.start()
        for cp in copies:
            cp.wait()

    return pl.pallas_call(
        body, name="grad_pair_exchange", out_shape=[_sds(shp, BF16) for shp in out_shapes], in_specs=[HBM] * nt,
        out_specs=[HBM] * nt, scratch_shapes=[pltpu.SemaphoreType.DMA((nt,)), pltpu.SemaphoreType.DMA((nt,))],
    )(*dws)


def _chip_block(ref, kind, s):
    r, cdim = ref.shape[-2:]
    if kind == "col":
        return _idx(ref, cols=pl.ds(s * (cdim // N_CHIP), cdim // N_CHIP))
    return _idx(ref, rows=pl.ds(s * (r // N_CHIP), r // N_CHIP))


def _chip_exchange(cps, kinds):
    nt = len(cps)
    out_shapes = []
    for cp, kind in zip(cps, kinds):
        shp = list(cp.shape)
        shp[-1 if kind == "col" else -2] //= N_CHIP
        out_shapes.append((N_CHIP,) + tuple(shp))

    def body(*refs):
        src, dst = refs[:nt], refs[nt:2 * nt]
        send_sems, recv_sems, local_sems = refs[2 * nt:]
        x, y, c = _mesh_pos()
        s = 2 * x + y
        chips = _other_chips(x, y)
        sends, locals_ = [], []
        for t in range(nt):
            own = pltpu.make_async_copy(_chip_block(src[t], kinds[t], s), dst[t].at[s], local_sems.at[t])
            own.start()
            locals_.append(own)
            for j, chip in enumerate(chips):
                cp = pltpu.make_async_remote_copy(src_ref=_chip_block(src[t], kinds[t], 2 * chip[0] + chip[1]), dst_ref=dst[t].at[s],
                                                  send_sem=send_sems.at[3 * t + j], recv_sem=recv_sems.at[3 * t + j],
                                                  device_id=(*chip, c), device_id_type=MESH)
                cp.start()
                sends.append(cp)
        for t in range(nt):
            for j, chip in enumerate(chips):
                landing = dst[t].at[2 * chip[0] + chip[1]]
                pltpu.make_async_remote_copy(src_ref=landing, dst_ref=landing, send_sem=send_sems.at[3 * t + j],
                                             recv_sem=recv_sems.at[3 * t + j], device_id=(x, y, c), device_id_type=MESH).wait_recv()
        for cp in sends:
            cp.wait_send()
        for own in locals_:
            own.wait()

    return pl.pallas_call(
        body, name="grad_chip_exchange", out_shape=[_sds(shp, BF16) for shp in out_shapes], in_specs=[HBM] * nt,
        out_specs=[HBM] * nt,
        scratch_shapes=[pltpu.SemaphoreType.DMA((3 * nt,)), pltpu.SemaphoreType.DMA((3 * nt,)), pltpu.SemaphoreType.DMA((nt,))],
    )(*cps)


def _pair_assemble(gs, kinds):
    nt = len(gs)

    def body(*refs):
        g = refs[nt:2 * nt]
        send_sems, recv_sems = refs[2 * nt:]
        x, y, c = _mesh_pos()
        copies = []
        for t in range(nt):
            mine = _half(g[t], kinds[t], c)
            cp = pltpu.make_async_remote_copy(src_ref=mine, dst_ref=mine, send_sem=send_sems.at[t], recv_sem=recv_sems.at[t],
                                              device_id=(x, y, 1 - c), device_id_type=MESH)
            cp.start()
            copies.append(cp)
        for t in range(nt):
            landing = _half(g[t], kinds[t], 1 - c)
            pltpu.make_async_remote_copy(src_ref=landing, dst_ref=landing, send_sem=send_sems.at[t], recv_sem=recv_sems.at[t],
                                         device_id=(x, y, c), device_id_type=MESH).wait_recv()
        for cp in copies:
            cp.wait_send()

    return pl.pallas_call(
        body, name="grad_pair_assemble", out_shape=[_sds(a.shape, F32) for a in gs], in_specs=[HBM] * nt,
        out_specs=[HBM] * nt, input_output_aliases={t: t for t in range(nt)},
        scratch_shapes=[pltpu.SemaphoreType.DMA((nt,)), pltpu.SemaphoreType.DMA((nt,))],
    )(*gs)


def _pack(arrays, width):
    flat = jnp.concatenate([a.reshape(-1) for a in arrays])
    pad = (-flat.size) % (8 * width)
    return jnp.pad(flat, (0, pad)).reshape(-1, width)


def _unpack(packed, shapes):
    flat = packed.reshape(-1)
    out, off = [], 0
    for shp in shapes:
        n = 1
        for dim in shp:
            n *= dim
        out.append(flat[off:off + n].reshape(shp))
        off += n
    return out


def kernel(x, c, ada_w, ada_b, norm_mix_g, norm_ffn_g, ab_w_in, a_conv_w, a_conv_b, a_norm_g, a_norm_b, b_norm_g, b_norm_b, b_w_s, b_bias, ab_w_out, pool_w, pool_scale, ffn_w1, ffn_w3, ffn_w2, final_g, loss_target, m_ada_w, m_ada_b, m_norm_mix_g, m_norm_ffn_g, m_ab_w_in, m_a_conv_w, m_a_conv_b, m_a_norm_g, m_a_norm_b, m_b_norm_g, m_b_norm_b, m_b_w_s, m_b_bias, m_ab_w_out, m_pool_w, m_pool_scale, m_ffn_w1, m_ffn_w3, m_ffn_w2, m_final_g, v_ada_w, v_ada_b, v_norm_mix_g, v_norm_ffn_g, v_ab_w_in, v_a_conv_w, v_a_conv_b, v_a_norm_g, v_a_norm_b, v_b_norm_g, v_b_norm_b, v_b_w_s, v_b_bias, v_ab_w_out, v_pool_w, v_pool_scale, v_ffn_w1, v_ffn_w3, v_ffn_w2, v_final_g):
    mx, my, mc = _mesh_pos()
    chip = 2 * mx + my
    dev = 4 * mx + 2 * my + mc
    x2 = x[0]
    target = loss_target[0]
    s, d = x2.shape
    depth = ada_w.shape[0]
    n_mod = ada_b.shape[1] // d
    n_even = ab_w_in.shape[0]
    da, db = a_conv_b.shape[1], b_norm_g.shape[1]
    nh = b_w_s.shape[1]
    kw = a_conv_w.shape[1]
    n_pool = pool_w.shape[1]
    tm_row = _pick(s, (256, 128))

    s_vec = jnp.reshape(chip, (1,)).astype(jnp.int32)
    c_vec = jnp.reshape(mc, (1,)).astype(jnp.int32)
    sc_vec = [s_vec, c_vec] + [jnp.reshape(v, (1,)).astype(jnp.int32)
                               for v in (2 * mx + (1 - my), 2 * (1 - mx) + my, 2 * (1 - mx) + (1 - my))]
    pool_w3 = pool_w.reshape((-1,) + pool_w.shape[2:])
    shard_of = {"w_in": (ab_w_in, "col"), "w_out": (ab_w_out, "row"), "pool": (pool_w3, "row"), "w1": (ffn_w1, "col"),
                "w3": (ffn_w3, "col"), "w2": (ffn_w2, "row")}

    def layer_names(l):
        return (["w_in", "w_out"] if l % 2 == 0 else ["pool"]) + ["w1", "w3", "w2"]

    def layer_span(nm, l):
        if nm in ("w_in", "w_out"):
            return l // 2, 1
        if nm == "pool":
            return (l // 2) * n_pool, n_pool
        return l, 1

    group_names = {"0a": ["w_in"], "0o": ["w_out"], "0b": ["w1", "w3"], "0c": ["w2"]}
    group_layer = {key: 0 for key in group_names}
    for l in range(1, depth):
        group_names[l], group_layer[l] = layer_names(l), l
    owned, w_kinds, w_shapes = {}, {}, {}
    for key, names in group_names.items():
        l = group_layer[key]
        owned[key] = [_cast_into_full(shard_of[nm][0], shard_of[nm][1], s_vec, *layer_span(nm, l)) for nm in names]
        w_kinds[key] = [shard_of[nm][1] for nm in names]
        w_shapes[key] = [(layer_span(nm, l)[1],) + shard_of[nm][0].shape[1:] for nm in names]
    layer_w = [{} for _ in range(depth)]
    layer_w[0].update(zip(group_names["0a"], _allgather_weights(owned["0a"], w_kinds["0a"], w_shapes["0a"])))
    gathers = {"prev": layer_w[0][group_names["0a"][-1]], "token": None, "flying": {}}

    passed_on = {"0b", 1, 2}

    def gather_start(key):
        sends, recvs, fulls, token = _gather_start(f"gather_start_{key}", owned[key], w_kinds[key], w_shapes[key], gathers["prev"],
                                                   both=key not in passed_on)
        gathers["flying"][key] = (sends, recvs, fulls)
        gathers["prev"] = gathers["token"] = token

    def gather_wait(key, after):
        sends, recvs, fulls = gathers["flying"].pop(key)
        landed = _gather_wait(f"gather_wait_{key}", sends, recvs, fulls, w_kinds[key], w_shapes[key], after, both=key not in passed_on)
        if key in passed_on:
            landed = _forward_halves(landed, w_kinds[key], w_shapes[key])
        layer_w[group_layer[key]].update(zip(group_names[key], landed))
        gathers["prev"] = landed[-1]

    def after_starts(row):
        return row + gathers["token"][0:1, 0:1]

    for key in ("0o", "0b", "0c"):
        gather_start(key)

    pre = _allgather_small(_pack([c, a_conv_w, pool_scale], 128)).reshape(N_DEV, -1)
    n_cw, n_ps = a_conv_w.size, pool_scale.size
    c_all = pre[:, :d]
    cw_chips = pre[0::2, d:d + n_cw].reshape((N_CHIP,) + a_conv_w.shape)
    conv_w_full = jnp.concatenate([cw_chips[k] for k in range(N_CHIP)], axis=-1)
    ps_chips = pre[0::2, d + n_cw:d + n_cw + n_ps].reshape((N_CHIP,) + pool_scale.shape)
    pool_scale_full = jnp.concatenate([ps_chips[k] for k in range(N_CHIP)], axis=-1)
    c_pad = jnp.pad(c_all, ((0, 8), (0, 0)))
    n_ada = ada_w.shape[2]
    ada_b_mine = lax.dynamic_slice_in_dim(ada_b, chip * n_ada, n_ada, axis=1)[:, None, :]
    mod_part = _ada_fwd(c_pad, ada_w, ada_b_mine)[:, :N_DEV, :]
    mod_all = _allgather_small(mod_part.reshape(depth * N_DEV, n_ada))
    mod_chips = mod_all[0::2].reshape(N_CHIP, depth, N_DEV, n_ada)
    mod_mine = lax.dynamic_index_in_dim(mod_chips, dev, axis=2, keepdims=False)
    mod = jnp.transpose(mod_mine, (1, 0, 2)).reshape(depth, n_mod, 1, d)

    causal = jnp.tril(jnp.ones((CHUNK, CHUNK), dtype=bool))
    w_c = jnp.where(causal[None, None], b_w_s, 0.0).astype(BF16)
    bias_full = jnp.repeat(jnp.swapaxes(b_bias, 1, 2), CHUNK, axis=2)

    saved = []
    xs = x2
    for l in range(depth):
        sh1, sc1, g1, sh2, sc2, g2 = [mod[l, k] for k in range(n_mod)]
        i = l // 2
        st = {"x1": xs}
        if 1 <= l and l + 2 < depth:
            gather_start(l + 2)
        gain1 = after_starts(norm_mix_g[l][None])
        wl = layer_w[l]
        if l % 2 == 0:
            h = _norm_mod(xs, gain1, sh1, sc1, tm_row)
            proj = _mm("ab_proj", [h], [(wl["w_in"], 0)], [BF16], _ep_store, tk=2048)[0]
            cat, a1 = _ab_mid_fwd(proj, conv_w_full[i], a_conv_b[i][None], a_norm_g[i][None], a_norm_b[i][None],
                                  b_norm_g[i][None], b_norm_b[i][None], w_c[i], bias_full[i], tm_row)
            if l == 0:
                gather_wait("0o", cat)
            xs, y1 = _mm("ab_out", [cat], [(wl["w_out"], 0)], [F32, BF16], _ep_residual, extras=[xs, g1], extra_kinds=["tile", "row"], tk=2048)
            st.update(h=h, proj=proj, a1=a1, cat=cat, y=y1)
        else:
            p = _pool_fwd(xs, gain1, sh1, sc1, tm_row)
            gate = g1 * pool_scale_full[i][None]
            xs, ymm = _grouped_fwd(p, wl["pool"][None], 0, xs, gate, 1024)
            st.update(p=p, y=ymm, gate=gate)
        st["x2"] = xs
        gain2 = norm_ffn_g[l][None]
        if l == 0:
            gather_wait("0b", xs)
            for nxt in range(1, min(3, depth)):
                gather_start(nxt)
            gain2 = after_starts(gain2)
        h2 = _norm_mod(xs, gain2, sh2, sc2, tm_row)
        u, t, z = _mm("ffn_up", [h2, h2], [(wl["w1"], 0), (wl["w3"], 0)], [BF16, BF16, BF16], _ep_swiglu, tn=512, tk=2048)
        if l == 0:
            gather_wait("0c", z)
        xs, y2 = _mm("ffn_down", [z], [(wl["w2"], 0)], [F32, BF16], _ep_residual, extras=[xs, g2], extra_kinds=["tile", "row"],
                     tn=512, tk=ffn_w2.shape[1] * N_CHIP)
        if l + 1 < depth:
            gather_wait(l + 1, xs)
        st.update(h2=h2, u=u, t=t, z=z, y2=y2)
        saved.append(st)

    def below_of(l, which):
        if which == "ffn":
            return saved[l]["y2"], mod[l, n_mod - 1]
        return saved[l]["y"], (mod[l, 2] if l % 2 == 0 else saved[l]["gate"])

    dx, fin_acc, loss_blk, dyb, gacc = _final_loss_bwd(xs, final_g[None], target, tm_row, below_of(depth - 1, "ffn"))
    loss = lax.psum(loss_blk[0, 0], MESH_AXES)
    d_final_g = fin_acc[0]
    dmod_rows = [None] * depth
    d_norm_mix, d_norm_ffn = [None] * depth, [None] * depth
    even_small = [None] * n_even
    d_pool_scale = [None] * (depth // 2)
    grad_names = ["w_in", "w_out", "pool", "w1", "w3", "w2"]
    g_shapes = {"w_in": ab_w_in.shape, "w_out": ab_w_out.shape, "pool": (pool_w.shape[0], n_pool * pool_w.shape[2], pool_w.shape[3]),
                "w1": ffn_w1.shape, "w3": ffn_w3.shape, "w2": ffn_w2.shape}
    shard_grads = {nm: None for nm in grad_names}
    pipe = {"pair": None, "chip": None, "prev": None, "token": None}

    def finish_chip(after):
        tag, pl_, names_, kinds_, sends, recvs, cps_f, slabs_f = pipe["chip"]
        cps_d, slabs_d = _reduce_wait(f"reduce_wait_{tag}", sends, recvs, cps_f, slabs_f, kinds_, after)
        for nm, kind, cp, sl in zip(names_, kinds_, cps_d, slabs_d):
            rpg = pool_w.shape[2] if nm == "pool" else None
            cp2 = cp.reshape(-1, cp.shape[-1])
            shard_grads[nm] = _chip_sum_into(sl.reshape(N_CHIP, -1, sl.shape[-1]), cp2, shard_grads[nm], g_shapes[nm],
                                             layer_span(nm, pl_)[0] // (n_pool if nm == "pool" else 1), kind, sc_vec, rpg)
        pipe["chip"] = None
        pipe["prev"] = slabs_d[-1]

    def settle(after):
        made = []
        if pipe["pair"] is not None:
            tag, lyr, names, kinds_l, sends, recvs, dws_f, sibs_f = pipe["pair"]
            dws_d, sibs_d = _pair_wait(f"pair_wait_{tag}", sends, recvs, dws_f, sibs_f, kinds_l, after)
            cps = []
            for dw, sib, kind in zip(dws_d, sibs_d, kinds_l):
                cp = _pair_add(dw.reshape(-1, dw.shape[-1]), sib.reshape(-1, sib.shape[-1]), kind, c_vec)
                cps.append(cp.reshape(sib.shape))
            pipe["pair"] = None
            pipe["ready"] = (tag, lyr, names, kinds_l, cps)
            made.append(cps[-1])
        if pipe["chip"] is not None:
            finish_chip(after)
            made.append(pipe["prev"])
        return made

    def advance(after, new=None):
        settle(after)
        if pipe.get("ready") is not None:
            tag, lyr, names, kinds_l, cps = pipe.pop("ready")
            prev = cps[-1] if pipe["prev"] is None else pipe["prev"]
            sends, recvs, cps_f, slabs_f, token = _reduce_start(f"reduce_start_{tag}", cps, kinds_l, prev)
            pipe["chip"] = (tag, lyr, names, kinds_l, sends, recvs, cps_f, slabs_f)
            pipe["token"] = token
        if new is not None:
            tag, lyr, names, big_ = new
            kinds_l = [shard_of[nm][1] for nm in names]
            dws = [big_[nm] for nm in names]
            prev = dws[-1] if pipe["token"] is None else pipe["token"]
            sends, recvs, dws_f, sibs_f, token = _pair_start(f"pair_start_{tag}", dws, kinds_l, prev)
            pipe["pair"] = (tag, lyr, names, kinds_l, sends, recvs, dws_f, sibs_f)
            pipe["token"] = token

    def behind(row):
        return row + pipe["token"][0:1, 0:1]

    for l in reversed(range(depth)):
        sh1, sc1, g1, sh2, sc2, g2 = [mod[l, k] for k in range(n_mod)]
        st = saved[l]
        wl = layer_w[l]
        i = l // 2
        big = {}
        d_g2 = gacc[0]
        du, dt = _mm("ffn_dz", [dyb], [(wl["w2"], 0)], [BF16, BF16], _ep_swiglu_bwd, trans_b=True, extras=[st["u"], st["t"]],
                     extra_kinds=["tile", "tile"], tm=2048, tn=512, tk=2048)
        big["w2"] = _mm("ffn_dw2", [st["z"]], [dyb], [BF16], _ep_store, trans_a=True, tm=512, tn=512, tk=s)[0]
        big["w1"] = _mm("ffn_dw1", [st["h2"]], [du], [BF16], _ep_store, trans_a=True, tm=512, tn=512, tk=s)[0]
        big["w3"] = _mm("ffn_dw3", [st["h2"]], [dt], [BF16], _ep_store, trans_a=True, tm=512, tn=512, tk=s)[0]
        advance(big["w3"], (f"ffn{l}", l, ["w1", "w3", "w2"], big))
        dh = _mm("ffn_dh", [du, dt], [(wl["w1"], 0), (wl["w3"], 0)], [BF16], _ep_sum, trans_b=True, tn=512, tk=ffn_w2.shape[1] * 2)[0]
        dx, nacc, dyb, gacc = _norm_mod_bwd(dh, st["x2"], dx, behind(norm_ffn_g[l][None]), sc2, tm_row, below_of(l, "mix"))
        d_sh2, d_sc2, d_norm_ffn[l] = nacc[0], nacc[1], nacc[2]
        if l % 2 == 0:
            d_g1 = gacc[0]
            big["w_out"] = _mm("ab_dw_out", [st["cat"]], [dyb], [BF16], _ep_store, trans_a=True, tm=512, tn=512, tk=s)[0]
            dcat = _mm("ab_dcat", [dyb], [(wl["w_out"], 0)], [BF16], _ep_store, trans_b=True, tk=2048)[0]
            advance(dcat)
            dproj, dcw, vecs, dws, dbias = _ab_mid_bwd(dcat, st["proj"], st["a1"], conv_w_full[i], behind(a_norm_g[i][None]),
                                                       a_norm_b[i][None], b_norm_g[i][None], b_norm_b[i][None], w_c[i],
                                                       bias_full[i], tm_row)
            even_small[i] = dict(conv_w=dcw, a_norm_g=vecs[0, :da], a_norm_b=vecs[1, :da], conv_b=vecs[2, :da],
                                 b_norm_g=vecs[3, :db], b_norm_b=vecs[4, :db], w_s=dws, bias=dbias[:, :, 0])
            big["w_in"] = _mm("ab_dw_in", [st["h"]], [dproj], [BF16], _ep_store, trans_a=True, tm=512, tn=512, tk=s)[0]
            advance(big["w_in"], (f"mix{l}", l, ["w_in", "w_out"], big))
            dh = _mm("ab_dh", [dproj], [(wl["w_in"], 0)], [BF16], _ep_store, trans_b=True, tn=512, tk=2 * da + 2 * db)[0]
            below = below_of(l - 1, "ffn") if l > 0 else None
            outs = _norm_mod_bwd(dh, st["x1"], dx, behind(norm_mix_g[l][None]), sc1, tm_row, below)
            dx, nacc = outs[0], outs[1]
            if below is not None:
                dyb, gacc = outs[2], outs[3]
        else:
            d_g1 = gacc[0] * pool_scale_full[i]
            d_pool_scale[i] = gacc[0] * g1[0]
            big["pool"] = _grouped_dw(st["p"], dyb, n_pool, 1024)
            advance(big["pool"], (f"mix{l}", l, ["pool"], big))
            dp = _grouped_dx(dyb, wl["pool"][None], 0, 1024)
            dx, nacc, dyb, gacc = _pool_bwd(dp, st["x1"], dx, behind(norm_mix_g[l][None]), sc1, tm_row, below_of(l - 1, "ffn"))
        d_sh1, d_sc1, d_norm_mix[l] = nacc[0], nacc[1], nacc[2]
        dmod_rows[l] = jnp.concatenate([d_sh1, d_sc1, d_g1, d_sh2, d_sc2, d_g2])
    grad_x = dx[None]
    settled = settle(dx)

    dmod = jnp.stack(dmod_rows)
    small = [dmod, jnp.stack(d_norm_mix), jnp.stack(d_norm_ffn),
             jnp.stack([e["conv_w"] for e in even_small]), jnp.stack([e["conv_b"] for e in even_small]),
             jnp.stack([e["a_norm_g"] for e in even_small]), jnp.stack([e["a_norm_b"] for e in even_small]),
             jnp.stack([e["b_norm_g"] for e in even_small]), jnp.stack([e["b_norm_b"] for e in even_small]),
             jnp.stack([e["w_s"] for e in even_small]), jnp.stack([e["bias"] for e in even_small]),
             jnp.stack(d_pool_scale), d_final_g]
    small_shapes = [a.shape for a in small]
    width = 1024 if d >= 1024 else 128
    gathered = _allgather_small(_pack(small, width), after=settled)
    summed = _unpack(_sum_leading(gathered), small_shapes)
    (g_ada_b, g_norm_mix, g_norm_ffn, g_conv_w_full, g_conv_b, g_a_norm_g, g_a_norm_b, g_b_norm_g, g_b_norm_b, g_w_s, g_bias,
     g_pool_scale_full, g_final_g) = summed
    cw_cols = a_conv_w.shape[2]
    g_conv_w = lax.dynamic_slice_in_dim(g_conv_w_full, chip * cw_cols, cw_cols, axis=2)
    ps_cols = pool_scale.shape[1]
    g_pool_scale = lax.dynamic_slice_in_dim(g_pool_scale_full, chip * ps_cols, ps_cols, axis=1)

    dmod_all = gathered.reshape(N_DEV, -1)[:, :dmod.size].reshape(N_DEV, depth, n_mod * d)
    dmod_cols = lax.dynamic_slice_in_dim(dmod_all, chip * n_ada, n_ada, axis=2)
    dmod_cols = jnp.pad(jnp.transpose(dmod_cols, (1, 0, 2)), ((0, 0), (0, 8), (0, 0)))
    g_ada_w = _ada_bwd(c_pad, dmod_cols)

    weights = [ada_w, ada_b, norm_mix_g, norm_ffn_g, ab_w_in, a_conv_w, a_conv_b, a_norm_g, a_norm_b, b_norm_g, b_norm_b, b_w_s,
               b_bias, ab_w_out, pool_w, pool_scale, ffn_w1, ffn_w3, ffn_w2, final_g]
    ms = [m_ada_w, m_ada_b, m_norm_mix_g, m_norm_ffn_g, m_ab_w_in, m_a_conv_w, m_a_conv_b, m_a_norm_g, m_a_norm_b, m_b_norm_g,
          m_b_norm_b, m_b_w_s, m_b_bias, m_ab_w_out, m_pool_w, m_pool_scale, m_ffn_w1, m_ffn_w3, m_ffn_w2, m_final_g]
    vs = [v_ada_w, v_ada_b, v_norm_mix_g, v_norm_ffn_g, v_ab_w_in, v_a_conv_w, v_a_conv_b, v_a_norm_g, v_a_norm_b, v_b_norm_g,
          v_b_norm_b, v_b_w_s, v_b_bias, v_ab_w_out, v_pool_w, v_pool_scale, v_ffn_w1, v_ffn_w3, v_ffn_w2, v_final_g]
    grads = [g_ada_w, g_ada_b, g_norm_mix, g_norm_ffn, None, g_conv_w, g_conv_b, g_a_norm_g, g_a_norm_b, g_b_norm_g,
             g_b_norm_b, g_w_s, g_bias, None, None, g_pool_scale, None, None, None, g_final_g]
    updates = [None] * len(weights)

    def update(k):
        grads[k] = grads[k].reshape(weights[k].shape)
        updates[k] = _adamw(weights[k], grads[k], ms[k], vs[k])

    advance(g_ada_w)
    for k in range(1, len(weights)):
        if grads[k] is not None:
            update(k)
    update(0)
    advance(updates[0][0])
    big_at = {"w_in": 4, "w_out": 13, "pool": 14, "w1": 16, "w3": 17, "w2": 18}
    assembled = _pair_assemble([shard_grads[nm] for nm in grad_names], [shard_of[nm][1] for nm in grad_names])
    for nm, g in zip(grad_names, assembled):
        grads[big_at[nm]] = g
        update(big_at[nm])
    return (loss, grad_x, *grads, *[u[0] for u in updates], *[u[1] for u in updates], *[u[2] for u in updates])
```

```python
import functools

import jax
import jax.numpy as jnp
from jax import lax
from jax.experimental import pallas as pl
from jax.experimental.pallas import tpu as pltpu

F32 = jnp.float32
BF16 = jnp.bfloat16
EPS = 1e-6
N_DEV = 8
N_CHIP = 4
MESH_AXES = ("x", "y", "c")
MESH = pl.DeviceIdType.MESH
V7X_VMEM_LIMIT_BYTES = 56 * 1024 * 1024
SUBLANES = 8
CONV_HALO = 32
POOL_HALO = 16
POOL_WINDOWS = (2, 4, 8, 16)
CHUNK = 128
ADAM_LR, ADAM_B1, ADAM_B2, ADAM_EPS, ADAM_WD, ADAM_STEP = 0.001, 0.9, 0.999, 1e-08, 0.01, 10
HBM = pl.BlockSpec(memory_space=pltpu.HBM)


def _params(sem=None):
    return pltpu.CompilerParams(dimension_semantics=sem, vmem_limit_bytes=V7X_VMEM_LIMIT_BYTES)


def _pick(n, prefs):
    for p in prefs:
        if p <= n and n % p == 0:
            return p
    return n


def _row_tile(rows, cols, target):
    if rows * cols <= target:
        return rows
    best = None
    for d in range(16, rows, 16):
        if rows % d == 0 and d * cols <= target:
            best = d
    return best if best is not None else rows


def _sds(shape, dtype):
    return jax.ShapeDtypeStruct(tuple(shape), dtype)


def _mm_core(name, grid, nk, a_list, b_list, a_spec, b_spec, dn, extras, extra_specs, out_shapes, out_specs,
             acc_shape, epilogue):
    n, n_ex, n_out = len(a_list), len(extras), len(out_shapes)

    def body(*refs):
        a_refs, b_refs = refs[:n], refs[n:2 * n]
        ex = refs[2 * n:2 * n + n_ex]
        outs = refs[2 * n + n_ex:2 * n + n_ex + n_out]
        accs = refs[2 * n + n_ex + n_out:]
        ps = [lax.dot_general(a[...], b[...], dn, preferred_element_type=F32) for a, b in zip(a_refs, b_refs)]
        if nk == 1:
            epilogue(ps, ex, outs)
            return
        k = pl.program_id(2)

        @pl.when(k == 0)
        def _():
            for acc, p in zip(accs, ps):
                acc[...] = p

        @pl.when(k > 0)
        def _():
            for acc, p in zip(accs, ps):
                acc[...] += p

        @pl.when(k == nk - 1)
        def _():
            epilogue([acc[...] for acc in accs], ex, outs)

    scratch = [] if nk == 1 else [pltpu.VMEM(acc_shape, F32) for _ in range(n)]
    return pl.pallas_call(
        body, grid=grid, name=name,
        in_specs=[a_spec] * n + [b_spec] * n + list(extra_specs),
        out_specs=list(out_specs), out_shape=list(out_shapes), scratch_shapes=scratch,
        compiler_params=_params(("parallel", "parallel", "arbitrary")),
    )(*a_list, *b_list, *extras)


def _mm(name, a_list, b_list, out_dtypes, epilogue, *, trans_a=False, trans_b=False, extras=(), extra_kinds=(),
        tm=1024, tn=1024, tk=1024):
    layer = None
    if isinstance(b_list[0], tuple):
        layer = b_list[0][1]
        b_list = [b for b, _ in b_list]
    a0 = a_list[0]
    b_shape = b_list[0].shape[-2:]
    m, kk = (a0.shape[1], a0.shape[0]) if trans_a else a0.shape
    nn = b_shape[0] if trans_b else b_shape[1]
    tm, tn, tk = _pick(m, (tm, 512, 256, 128)), _pick(nn, (tn, 512, 256, 128)), _pick(kk, (tk, 512, 256, 128))
    nk = kk // tk
    a_spec = pl.BlockSpec((tk, tm), lambda i, j, k: (k, i)) if trans_a else pl.BlockSpec((tm, tk), lambda i, j, k: (i, k))
    if layer is None:
        b_spec = pl.BlockSpec((tn, tk), lambda i, j, k: (j, k)) if trans_b else pl.BlockSpec((tk, tn), lambda i, j, k: (k, j))
    elif trans_b:
        b_spec = pl.BlockSpec((None, tn, tk), lambda i, j, k: (layer, j, k))
    else:
        b_spec = pl.BlockSpec((None, tk, tn), lambda i, j, k: (layer, k, j))
    dn = (((0 if trans_a else 1,), (1 if trans_b else 0,)), ((), ()))
    tile = pl.BlockSpec((tm, tn), lambda i, j, k: (i, j))
    row = pl.BlockSpec((1, tn), lambda i, j, k: (0, j))
    return _mm_core(name, (m // tm, nn // tn, nk), nk, a_list, b_list, a_spec, b_spec, dn, extras,
                    [tile if kd == "tile" else row for kd in extra_kinds],
                    [_sds((m, nn), dt) for dt in out_dtypes], [tile] * len(out_dtypes), (tm, tn), epilogue)


def _ep_store(ps, ex, outs):
    outs[0][...] = ps[0].astype(outs[0].dtype)


def _ep_sum(ps, ex, outs):
    outs[0][...] = (ps[0] + ps[1]).astype(outs[0].dtype)


def _ep_swiglu(ps, ex, outs):
    u, t = ps
    sg = jax.nn.sigmoid(u)
    su = u * sg
    outs[0][...] = (t * (sg * (1.0 + u * (1.0 - sg)))).astype(BF16)
    outs[1][...] = su.astype(BF16)
    outs[2][...] = (su * t).astype(BF16)


def _ep_residual(ps, ex, outs):
    x_ref, gate_ref = ex
    y = ps[0]
    outs[0][...] = x_ref[...] + gate_ref[...] * y
    outs[1][...] = y.astype(BF16)


def _ep_swiglu_bwd(ps, ex, outs):
    dz = ps[0]
    outs[0][...] = (dz * ex[0][...].astype(F32)).astype(BF16)
    outs[1][...] = (dz * ex[1][...].astype(F32)).astype(BF16)


def _grouped_fwd(p, w, layer, x, gate, tm):
    s, d = p.shape
    _, g, kg, ng = w.shape
    tm = _pick(s, (tm, 512, 256, 128))
    tile_a = pl.BlockSpec((tm, kg), lambda i, j, k: (i, j))
    tile_o = pl.BlockSpec((tm, ng), lambda i, j, k: (i, j))
    return _mm_core("pool_mm_fwd", (s // tm, g, 1), 1, [p], [w], tile_a,
                    pl.BlockSpec((None, None, kg, ng), lambda i, j, k: (layer, j, 0, 0)), (((1,), (0,)), ((), ())),
                    [x, gate], [tile_o, pl.BlockSpec((1, ng), lambda i, j, k: (0, j))],
                    [_sds((s, g * ng), F32), _sds((s, g * ng), BF16)], [tile_o, tile_o], None, _ep_residual)


def _grouped_dx(dy, w, layer, tm):
    s, _ = dy.shape
    _, g, kg, ng = w.shape
    tm = _pick(s, (tm, 512, 256, 128))
    return _mm_core("pool_mm_dx", (s // tm, g, 1), 1, [dy], [w], pl.BlockSpec((tm, ng), lambda i, j, k: (i, j)),
                    pl.BlockSpec((None, None, kg, ng), lambda i, j, k: (layer, j, 0, 0)), (((1,), (1,)), ((), ())), [], [],
                    [_sds((s, g * kg), BF16)], [pl.BlockSpec((tm, kg), lambda i, j, k: (i, j))], None, _ep_store)[0]


def _grouped_dw(p, dy, groups, tk):
    s, d = p.shape
    kg = d // groups
    ng = dy.shape[1] // groups
    tk = _pick(s, (tk, 512, 256, 128))
    nk = s // tk

    def ep(ps, ex, outs):
        outs[0][...] = ps[0].astype(BF16)

    return _mm_core("pool_mm_dw", (groups, 1, nk), nk, [p], [dy], pl.BlockSpec((tk, kg), lambda i, j, k: (k, i)),
                    pl.BlockSpec((tk, ng), lambda i, j, k: (k, i)), (((0,), (0,)), ((), ())), [], [],
                    [_sds((groups, kg, ng), BF16)], [pl.BlockSpec((None, kg, ng), lambda i, j, k: (i, 0, 0))],
                    (kg, ng), ep)[0]


def _rms_rstd(xv):
    return lax.rsqrt(jnp.mean(xv * xv, axis=-1, keepdims=True) + EPS)


def _norm_mod_math(xv, g, sh, sc):
    return ((xv * _rms_rstd(xv)) * g) * (1.0 + sc) + sh


def _norm_mod_bwd_math(dh, xv, dxo, g, sc, acc_ref):
    r = _rms_rstd(xv)
    xhat = xv * r
    acc_ref[0:1, :] += jnp.sum(dh, axis=0, keepdims=True)
    acc_ref[1:2, :] += jnp.sum(dh * (xhat * g), axis=0, keepdims=True)
    dhn = dh * (1.0 + sc)
    acc_ref[2:3, :] += jnp.sum(dhn * xhat, axis=0, keepdims=True)
    dxh = dhn * g
    return dxo + r * (dxh - xhat * jnp.mean(dxh * xhat, axis=-1, keepdims=True))


def _vec_spec(d):
    return pl.BlockSpec((1, d), lambda i: (0, 0))


def _acc_spec(d):
    return pl.BlockSpec((8, d), lambda i: (0, 0))


def _norm_mod(x, g, sh, sc, tm):
    s, d = x.shape

    def body(x_ref, g_ref, sh_ref, sc_ref, h_ref):
        h_ref[...] = _norm_mod_math(x_ref[...], g_ref[...], sh_ref[...], sc_ref[...]).astype(BF16)

    row = pl.BlockSpec((tm, d), lambda i: (i, 0))
    return pl.pallas_call(body, grid=(s // tm,), name="norm_mod", in_specs=[row] + [_vec_spec(d)] * 3, out_specs=row,
                          out_shape=_sds((s, d), BF16), compiler_params=_params(("parallel",)))(x, g, sh, sc)


def _gate_step(dxv, y_ref, gate_ref, dy_ref, gacc_ref):
    dy_ref[...] = (dxv * gate_ref[...]).astype(BF16)
    gacc_ref[0:1, :] += jnp.sum(dxv * y_ref[...].astype(F32), axis=0, keepdims=True)


def _norm_mod_bwd(dh, x, dxo, g, sc, tm, below=None):
    s, d = x.shape

    def body(dh_ref, x_ref, dxo_ref, g_ref, sc_ref, *rest):
        if below is None:
            dx_ref, acc_ref = rest
        else:
            y_ref, gate_ref, dx_ref, acc_ref, dy_ref, gacc_ref = rest

        @pl.when(pl.program_id(0) == 0)
        def _():
            acc_ref[...] = jnp.zeros_like(acc_ref)
            if below is not None:
                gacc_ref[...] = jnp.zeros_like(gacc_ref)

        dxv = _norm_mod_bwd_math(dh_ref[...].astype(F32), x_ref[...], dxo_ref[...], g_ref[...], sc_ref[...], acc_ref)
        dx_ref[...] = dxv
        if below is not None:
            _gate_step(dxv, y_ref, gate_ref, dy_ref, gacc_ref)

    row = pl.BlockSpec((tm, d), lambda i: (i, 0))
    extra_in = [] if below is None else [row, _vec_spec(d)]
    extra_out = [] if below is None else [row, _acc_spec(d)]
    extra_shape = [] if below is None else [_sds((s, d), BF16), _sds((8, d), F32)]
    return pl.pallas_call(body, grid=(s // tm,), name="norm_mod_bwd", in_specs=[row, row, row, _vec_spec(d), _vec_spec(d)] + extra_in,
                          out_specs=[row, _acc_spec(d)] + extra_out, out_shape=[_sds((s, d), F32), _sds((8, d), F32)] + extra_shape,
                          compiler_params=_params(("arbitrary",)))(dh, x, dxo, g, sc, *([] if below is None else below))


def _gate_bwd(dx, y, gate, tm):
    s, d = dx.shape

    def body(dx_ref, y_ref, gate_ref, dy_ref, acc_ref):
        @pl.when(pl.program_id(0) == 0)
        def _():
            acc_ref[...] = jnp.zeros_like(acc_ref)

        dxv = dx_ref[...]
        dy_ref[...] = (dxv * gate_ref[...]).astype(BF16)
        acc_ref[0:1, :] += jnp.sum(dxv * y_ref[...].astype(F32), axis=0, keepdims=True)

    row = pl.BlockSpec((tm, d), lambda i: (i, 0))
    return pl.pallas_call(body, grid=(s // tm,), name="gate_bwd", in_specs=[row, row, _vec_spec(d)],
                          out_specs=[row, _acc_spec(d)], out_shape=[_sds((s, d), BF16), _sds((8, d), F32)],
                          compiler_params=_params(("arbitrary",)))(dx, y, gate)


def _final_loss_bwd(x, g, target, tm, below):
    s, d = x.shape

    def body(x_ref, g_ref, t_ref, y_ref, gate_ref, dx_ref, acc_ref, loss_ref, dy_ref, gacc_ref):
        @pl.when(pl.program_id(0) == 0)
        def _():
            acc_ref[...] = jnp.zeros_like(acc_ref)
            loss_ref[...] = jnp.zeros_like(loss_ref)
            gacc_ref[...] = jnp.zeros_like(gacc_ref)

        xv = x_ref[...]
        gv = g_ref[...]
        r = _rms_rstd(xv)
        xhat = xv * r
        err = xhat * gv - t_ref[...]
        loss_ref[...] += (0.5 / d) * jnp.sum(err * err)
        dy = err * (1.0 / d)
        acc_ref[0:1, :] += jnp.sum(dy * xhat, axis=0, keepdims=True)
        dxh = dy * gv
        dxv = r * (dxh - xhat * jnp.mean(dxh * xhat, axis=-1, keepdims=True))
        dx_ref[...] = dxv
        _gate_step(dxv, y_ref, gate_ref, dy_ref, gacc_ref)

    row = pl.BlockSpec((tm, d), lambda i: (i, 0))
    return pl.pallas_call(body, grid=(s // tm,), name="final_loss_bwd", in_specs=[row, _vec_spec(d), row, row, _vec_spec(d)],
                          out_specs=[row, _acc_spec(d), pl.BlockSpec((8, 128), lambda i: (0, 0)), row, _acc_spec(d)],
                          out_shape=[_sds((s, d), F32), _sds((8, d), F32), _sds((8, 128), F32), _sds((s, d), BF16), _sds((8, d), F32)],
                          compiler_params=_params(("arbitrary",)))(x, g, target, *below)


def _chunks(tm, width, rb, cb):
    rb, cb = min(rb, tm), min(cb, width)
    return [(r0, c0, rb, cb) for r0 in range(0, tm, rb) for c0 in range(0, width, cb)]


def _prev_halo_map(tm, halo):
    return lambda i: (jnp.maximum(i * (tm // halo) - 1, 0), 0)


def _next_halo_map(tm, halo, s):
    return lambda i: (jnp.minimum((i + 1) * (tm // halo), s // halo - 1), 0)


def _pool_fwd(x, g, sh, sc, tm):
    s, d = x.shape
    dg = d // len(POOL_WINDOWS)

    def body(x_ref, xh_ref, g_ref, sh_ref, sc_ref, p_ref, ext_ref):
        i = pl.program_id(0)
        gv, shv, scv = g_ref[...], sh_ref[...], sc_ref[...]
        ext_ref[POOL_HALO:, :] = _norm_mod_math(x_ref[...], gv, shv, scv)
        ext_ref[0:POOL_HALO, :] = jnp.where(i == 0, 0.0, _norm_mod_math(xh_ref[...], gv, shv, scv))
        for gi, w in enumerate(POOL_WINDOWS):
            for r0, c0, rb, cb in _chunks(tm, dg, 64, 256):
                cols = pl.ds(gi * dg + c0, cb)
                tok = ext_ref[pl.ds(POOL_HALO + r0, rb), cols]
                acc = tok
                for j in range(1, w):
                    acc = acc + ext_ref[pl.ds(POOL_HALO + r0 - j, rb), cols]
                t_glob = i * tm + r0 + lax.broadcasted_iota(jnp.int32, (rb, 1), 0)
                cnt = jnp.minimum(t_glob + 1, w).astype(F32)
                p_ref[pl.ds(r0, rb), cols] = (acc / cnt - tok).astype(BF16)

    row = pl.BlockSpec((tm, d), lambda i: (i, 0))
    halo = pl.BlockSpec((POOL_HALO, d), _prev_halo_map(tm, POOL_HALO))
    return pl.pallas_call(body, grid=(s // tm,), name="pool_fwd", in_specs=[row, halo] + [_vec_spec(d)] * 3, out_specs=row,
                          out_shape=_sds((s, d), BF16), scratch_shapes=[pltpu.VMEM((tm + POOL_HALO, d), F32)],
                          compiler_params=_params(("parallel",)))(x, x, g, sh, sc)


def _pool_bwd(dp, x, dxo, g, sc, tm, below):
    s, d = x.shape
    dg = d // len(POOL_WINDOWS)
    n_tiles = s // tm

    def body(dp_ref, dph_ref, x_ref, dxo_ref, g_ref, sc_ref, y_ref, gate_ref, dx_ref, acc_ref, dy_ref, gacc_ref, ext_ref, dh_ref):
        i = pl.program_id(0)

        @pl.when(i == 0)
        def _():
            acc_ref[...] = jnp.zeros_like(acc_ref)
            gacc_ref[...] = jnp.zeros_like(gacc_ref)

        for gi, w in enumerate(POOL_WINDOWS):
            cols = pl.ds(gi * dg, dg)
            t_main = i * tm + lax.broadcasted_iota(jnp.int32, (tm, 1), 0)
            ext_ref[0:tm, cols] = dp_ref[:, cols].astype(F32) / jnp.minimum(t_main + 1, w).astype(F32)
            ext_ref[tm:, cols] = jnp.where(i == n_tiles - 1, 0.0, dph_ref[:, cols].astype(F32) * (1.0 / w))
            for r0, c0, rb, cb in _chunks(tm, dg, 64, 256):
                cc = pl.ds(gi * dg + c0, cb)
                acc = ext_ref[pl.ds(r0, rb), cc]
                for j in range(1, w):
                    acc = acc + ext_ref[pl.ds(r0 + j, rb), cc]
                dh_ref[pl.ds(r0, rb), cc] = acc - dp_ref[pl.ds(r0, rb), cc].astype(F32)
        dxv = _norm_mod_bwd_math(dh_ref[...], x_ref[...], dxo_ref[...], g_ref[...], sc_ref[...], acc_ref)
        dx_ref[...] = dxv
        _gate_step(dxv, y_ref, gate_ref, dy_ref, gacc_ref)

    row = pl.BlockSpec((tm, d), lambda i: (i, 0))
    halo = pl.BlockSpec((POOL_HALO, d), _next_halo_map(tm, POOL_HALO, s))
    return pl.pallas_call(body, grid=(n_tiles,), name="pool_bwd",
                          in_specs=[row, halo, row, row, _vec_spec(d), _vec_spec(d), row, _vec_spec(d)],
                          out_specs=[row, _acc_spec(d), row, _acc_spec(d)],
                          out_shape=[_sds((s, d), F32), _sds((8, d), F32), _sds((s, d), BF16), _sds((8, d), F32)],
                          scratch_shapes=[pltpu.VMEM((tm + POOL_HALO, d), F32), pltpu.VMEM((tm, d), F32)],
                          compiler_params=_params(("arbitrary",)))(dp, dp, x, dxo, g, sc, *below)


def _layernorm_fwd(v, g, b):
    mu = jnp.mean(v, axis=-1, keepdims=True)
    xc = v - mu
    rstd = lax.rsqrt(jnp.mean(xc * xc, axis=-1, keepdims=True) + EPS)
    yn = xc * rstd
    return yn * g + b, yn, rstd


def _layernorm_bwd(dz, yn, rstd, g):
    dyn = dz * g
    dv = rstd * (dyn - jnp.mean(dyn, axis=-1, keepdims=True) - yn * jnp.mean(dyn * yn, axis=-1, keepdims=True))
    return dv, jnp.sum(dz * yn, axis=0, keepdims=True), jnp.sum(dz, axis=0, keepdims=True)


def _fill_shifted(ext_ref, sh_ref):
    n = ext_ref.shape[0]
    for r in range(1, SUBLANES):
        sh_ref[r - 1, 0:n - SUBLANES, :] = ext_ref[pl.ds(r, n - SUBLANES), :]


def _shifted(ext_ref, sh_ref, start, rows, cols):
    q, r = divmod(start, SUBLANES)
    if r == 0:
        return ext_ref[pl.ds(start, rows), cols]
    return sh_ref[r - 1, pl.ds(q * SUBLANES, rows), cols]


def _ab_mid_fwd(proj, conv_w, conv_b, a_g, a_b, v_g, v_b, w_c, bias_full, tm):
    s = proj.shape[0]
    da = conv_b.shape[1]
    db = v_g.shape[1]
    nh = w_c.shape[0]
    kw = conv_w.shape[0]
    lead = CONV_HALO - (kw - 1)

    def body(p_ref, ph_ref, cw_ref, cb_ref, ag_ref, ab_ref, vg_ref, vb_ref, wc_ref, bias_ref, cat_ref, a1_ref, ext_ref,
             a1s_ref, sh_ref):
        i = pl.program_id(0)
        val = p_ref[:, 0:da].astype(F32)
        gat = p_ref[:, da:2 * da].astype(F32)
        ext_ref[CONV_HALO:, :] = val * jax.nn.sigmoid(gat)
        hv = ph_ref[:, 0:da].astype(F32)
        hg = ph_ref[:, da:2 * da].astype(F32)
        ext_ref[0:CONV_HALO, :] = jnp.where(i == 0, 0.0, hv * jax.nn.sigmoid(hg))
        _fill_shifted(ext_ref, sh_ref)
        for r0, c0, rb, cb in _chunks(tm, da, 64, 256):
            cols = pl.ds(c0, cb)
            acc = jnp.broadcast_to(cb_ref[:, cols], (rb, cb))
            for k in range(kw):
                acc = acc + cw_ref[k:k + 1, cols] * _shifted(ext_ref, sh_ref, r0 + lead + k, rb, cols)
            a1s_ref[pl.ds(r0, rb), cols] = acc
        a1 = a1s_ref[...]
        a1_ref[...] = a1.astype(BF16)
        z, _, _ = _layernorm_fwd(a1, ag_ref[...], ab_ref[...])
        cat_ref[:, 0:da] = (z * jax.nn.sigmoid(z)).astype(BF16)

        bu = p_ref[:, 2 * da:2 * da + db].astype(F32)
        bv = p_ref[:, 2 * da + db:].astype(F32)
        vn, _, _ = _layernorm_fwd(bv, vg_ref[...], vb_ref[...])
        vnb = vn.astype(BF16)
        for n in range(tm // CHUNK):
            rows = slice(n * CHUNK, (n + 1) * CHUNK)
            for h in range(nh):
                hc = slice(h * CHUNK, (h + 1) * CHUNK)
                vo = jnp.dot(wc_ref[h], vnb[rows, hc], preferred_element_type=F32) + bias_ref[:, hc]
                cat_ref[rows, da + h * CHUNK:da + (h + 1) * CHUNK] = (bu[rows, hc] * vo).astype(BF16)

    full = lambda a: pl.BlockSpec(a.shape, lambda i: (0,) * a.ndim)
    return pl.pallas_call(
        body, grid=(s // tm,), name="ab_mid_fwd",
        in_specs=[pl.BlockSpec((tm, 2 * da + 2 * db), lambda i: (i, 0)),
                  pl.BlockSpec((CONV_HALO, 2 * da), _prev_halo_map(tm, CONV_HALO)),
                  full(conv_w), full(conv_b), full(a_g), full(a_b), full(v_g), full(v_b), full(w_c), full(bias_full)],
        out_specs=[pl.BlockSpec((tm, da + db), lambda i: (i, 0)), pl.BlockSpec((tm, da), lambda i: (i, 0))],
        out_shape=[_sds((s, da + db), BF16), _sds((s, da), BF16)],
        scratch_shapes=[pltpu.VMEM((tm + CONV_HALO, da), F32), pltpu.VMEM((tm, da), F32),
                        pltpu.VMEM((SUBLANES - 1, tm + CONV_HALO, da), F32)],
        compiler_params=_params(("parallel",)),
    )(proj, proj, conv_w, conv_b, a_g, a_b, v_g, v_b, w_c, bias_full)


def _ab_mid_bwd(dcat, proj, a1, conv_w, a_g, a_b, v_g, v_b, w_c, bias_full, tm):
    s = proj.shape[0]
    da = a_g.shape[1]
    db = v_g.shape[1]
    nh = w_c.shape[0]
    kw = conv_w.shape[0]
    lead = CONV_HALO - (kw - 1)
    n_tiles = s // tm

    def body(dc_ref, dch_ref, p_ref, ph_ref, a1_ref, a1h_ref, cw_ref, ag_ref, ab_ref, vg_ref, vb_ref, wc_ref, bias_ref,
             dp_ref, dcw_ref, vec_ref, dws_ref, dbias_ref, ext_ref, dext_ref, dcw_acc, dvn_ref, sh_ref, dsh_ref):
        i = pl.program_id(0)

        @pl.when(i == 0)
        def _():
            dcw_acc[...] = jnp.zeros_like(dcw_acc)
            vec_ref[...] = jnp.zeros_like(vec_ref)
            dws_ref[...] = jnp.zeros_like(dws_ref)
            dbias_ref[...] = jnp.zeros_like(dbias_ref)

        agv, abv = ag_ref[...], ab_ref[...]

        def silu_ln_bwd(a1v, d_a2):
            z, yn, rstd = _layernorm_fwd(a1v, agv, abv)
            sg = jax.nn.sigmoid(z)
            return _layernorm_bwd(d_a2 * (sg * (1.0 + z * (1.0 - sg))), yn, rstd, agv)

        d_a1, dga, dba = silu_ln_bwd(a1_ref[...].astype(F32), dc_ref[:, 0:da].astype(F32))
        dext_ref[0:tm, :] = d_a1
        d_a1h, _, _ = silu_ln_bwd(a1h_ref[...].astype(F32), dch_ref[...].astype(F32))
        dext_ref[tm:, :] = jnp.where(i == n_tiles - 1, 0.0, d_a1h)
        vec_ref[0:1, 0:da] += dga
        vec_ref[1:2, 0:da] += dba
        vec_ref[2:3, 0:da] += jnp.sum(d_a1, axis=0, keepdims=True)

        val = p_ref[:, 0:da].astype(F32)
        sgg = jax.nn.sigmoid(p_ref[:, da:2 * da].astype(F32))
        ext_ref[CONV_HALO:, :] = val * sgg
        hv = ph_ref[:, 0:da].astype(F32)
        hg = ph_ref[:, da:2 * da].astype(F32)
        ext_ref[0:CONV_HALO, :] = jnp.where(i == 0, 0.0, hv * jax.nn.sigmoid(hg))

        _fill_shifted(ext_ref, sh_ref)
        _fill_shifted(dext_ref, dsh_ref)
        for r0, c0, rb, cb in _chunks(tm, da, 64, 256):
            cols = pl.ds(c0, cb)
            rows = pl.ds(r0, rb)
            d1 = dext_ref[rows, cols]
            acc = jnp.zeros((rb, cb), F32)
            for k in range(kw):
                acc = acc + cw_ref[k:k + 1, cols] * _shifted(dext_ref, dsh_ref, r0 + (kw - 1) - k, rb, cols)
                prod = d1 * _shifted(ext_ref, sh_ref, r0 + lead + k, rb, cols)
                dcw_acc[k, :, cols] += jnp.sum(prod.reshape(rb // SUBLANES, SUBLANES, cb), axis=0)
            v = p_ref[rows, pl.ds(c0, cb)].astype(F32)
            sg = jax.nn.sigmoid(p_ref[rows, pl.ds(da + c0, cb)].astype(F32))
            dp_ref[rows, pl.ds(c0, cb)] = (acc * sg).astype(BF16)
            dp_ref[rows, pl.ds(da + c0, cb)] = (acc * v * sg * (1.0 - sg)).astype(BF16)

        vgv = vg_ref[...]
        bu = p_ref[:, 2 * da:2 * da + db].astype(F32)
        bv = p_ref[:, 2 * da + db:].astype(F32)
        vn, yn_v, rstd_v = _layernorm_fwd(bv, vgv, vb_ref[...])
        vnb = vn.astype(BF16)
        for n in range(tm // CHUNK):
            rows = slice(n * CHUNK, (n + 1) * CHUNK)
            for h in range(nh):
                hc = slice(h * CHUNK, (h + 1) * CHUNK)
                wch = wc_ref[h]
                blk = vnb[rows, hc]
                vo = jnp.dot(wch, blk, preferred_element_type=F32) + bias_ref[:, hc]
                d_bout = dc_ref[rows, da + h * CHUNK:da + (h + 1) * CHUNK].astype(F32)
                dp_ref[rows, 2 * da + h * CHUNK:2 * da + (h + 1) * CHUNK] = (d_bout * vo).astype(BF16)
                d_vo = d_bout * bu[rows, hc]
                dbias_ref[h] += jnp.sum(d_vo, axis=1, keepdims=True)
                d_vob = d_vo.astype(BF16)
                dws_ref[h] += lax.dot_general(d_vob, blk, (((1,), (1,)), ((), ())), preferred_element_type=F32)
                dvn_ref[rows, hc] = lax.dot_general(wch, d_vob, (((0,), (0,)), ((), ())), preferred_element_type=F32)
        d_bv, dgv, dbv = _layernorm_bwd(dvn_ref[...], yn_v, rstd_v, vgv)
        dp_ref[:, 2 * da + db:] = d_bv.astype(BF16)
        vec_ref[3:4, 0:db] += dgv
        vec_ref[4:5, 0:db] += dbv

        @pl.when(i == n_tiles - 1)
        def _():
            dcw_ref[...] = jnp.sum(dcw_acc[...], axis=1)
            causal = lax.broadcasted_iota(jnp.int32, (CHUNK, CHUNK), 0) >= lax.broadcasted_iota(jnp.int32, (CHUNK, CHUNK), 1)
            for h in range(nh):
                dws_ref[h] = jnp.where(causal, dws_ref[h], 0.0)

    full = lambda a: pl.BlockSpec(a.shape, lambda i: (0,) * a.ndim)
    wide = max(da, db)
    return pl.pallas_call(
        body, grid=(n_tiles,), name="ab_mid_bwd",
        in_specs=[pl.BlockSpec((tm, da + db), lambda i: (i, 0)),
                  pl.BlockSpec((CONV_HALO, da), _next_halo_map(tm, CONV_HALO, s)),
                  pl.BlockSpec((tm, 2 * da + 2 * db), lambda i: (i, 0)),
                  pl.BlockSpec((CONV_HALO, 2 * da), _prev_halo_map(tm, CONV_HALO)),
                  pl.BlockSpec((tm, da), lambda i: (i, 0)),
                  pl.BlockSpec((CONV_HALO, da), _next_halo_map(tm, CONV_HALO, s)),
                  full(conv_w), full(a_g), full(a_b), full(v_g), full(v_b), full(w_c), full(bias_full)],
        out_specs=[pl.BlockSpec((tm, 2 * da + 2 * db), lambda i: (i, 0)),
                   pl.BlockSpec((kw, da), lambda i: (0, 0)),
                   pl.BlockSpec((8, wide), lambda i: (0, 0)),
                   pl.BlockSpec((nh, CHUNK, CHUNK), lambda i: (0, 0, 0)),
                   pl.BlockSpec((nh, CHUNK, 1), lambda i: (0, 0, 0))],
        out_shape=[_sds((s, 2 * da + 2 * db), BF16), _sds((kw, da), F32), _sds((8, wide), F32),
                   _sds((nh, CHUNK, CHUNK), F32), _sds((nh, CHUNK, 1), F32)],
        scratch_shapes=[pltpu.VMEM((tm + CONV_HALO, da), F32), pltpu.VMEM((tm + CONV_HALO, da), F32),
                        pltpu.VMEM((kw, SUBLANES, da), F32), pltpu.VMEM((tm, db), F32),
                        pltpu.VMEM((SUBLANES - 1, tm + CONV_HALO, da), F32), pltpu.VMEM((SUBLANES - 1, tm + CONV_HALO, da), F32)],
        compiler_params=_params(("arbitrary",)),
    )(dcat, dcat, proj, proj, a1, a1, conv_w, a_g, a_b, v_g, v_b, w_c, bias_full)


def _ada_fwd(c_all, w, b):
    nl, d, n = w.shape
    tn = _pick(n, (512, 256, 128))

    def body(c_ref, w_ref, b_ref, o_ref):
        cv = c_ref[...]
        cond = (cv * jax.nn.sigmoid(cv)).astype(BF16)
        o_ref[...] = jnp.dot(cond, w_ref[...].astype(BF16), preferred_element_type=F32) + b_ref[...]

    return pl.pallas_call(
        body, grid=(nl, n // tn), name="ada_fwd",
        in_specs=[pl.BlockSpec(c_all.shape, lambda l, j: (0, 0)), pl.BlockSpec((None, d, tn), lambda l, j: (l, 0, j)),
                  pl.BlockSpec((None, 1, tn), lambda l, j: (l, 0, j))],
        out_specs=pl.BlockSpec((None, c_all.shape[0], tn), lambda l, j: (l, 0, j)),
        out_shape=_sds((nl, c_all.shape[0], n), F32), compiler_params=_params(("parallel", "parallel")),
    )(c_all, w, b)


def _ada_bwd(c_all, dmod):
    nl, nb, n = dmod.shape
    d = c_all.shape[1]
    tn = _pick(n, (512, 256, 128))

    def body(c_ref, g_ref, o_ref):
        cv = c_ref[...]
        cond = (cv * jax.nn.sigmoid(cv)).astype(BF16)
        o_ref[...] = lax.dot_general(cond, g_ref[...].astype(BF16), (((0,), (0,)), ((), ())), preferred_element_type=F32)

    return pl.pallas_call(
        body, grid=(nl, n // tn), name="ada_bwd",
        in_specs=[pl.BlockSpec(c_all.shape, lambda l, j: (0, 0)), pl.BlockSpec((None, nb, tn), lambda l, j: (l, 0, j))],
        out_specs=pl.BlockSpec((None, d, tn), lambda l, j: (l, 0, j)),
        out_shape=_sds((nl, d, n), F32), compiler_params=_params(("parallel", "parallel")),
    )(c_all, dmod)


def _sum_leading(a, out_dtype=F32, name="sum_leading"):
    n, r, c = a.shape
    tr = _row_tile(r, c, 256 * 1024)

    def body(a_ref, o_ref):
        acc = a_ref[0].astype(F32)
        for k in range(1, n):
            acc = acc + a_ref[k].astype(F32)
        o_ref[...] = acc.astype(out_dtype)

    return pl.pallas_call(body, grid=(r // tr,), name=name, in_specs=[pl.BlockSpec((n, tr, c), lambda i: (0, i, 0))],
                          out_specs=pl.BlockSpec((tr, c), lambda i: (i, 0)), out_shape=_sds((r, c), out_dtype),
                          compiler_params=_params(("parallel",)))(a)


def _cast_into_full(w, kind, s_vec, l0, nl):
    _, r, c = w.shape
    tr = _row_tile(r, c, 512 * 1024)
    nb = r // tr
    if kind == "col":
        out_shape, out_spec = (nl, r, N_CHIP * c), pl.BlockSpec((None, tr, c), lambda l, i, sv: (l, i, sv[0]))
    else:
        out_shape, out_spec = (nl, N_CHIP * r, c), pl.BlockSpec((None, tr, c), lambda l, i, sv: (l, sv[0] * nb + i, 0))

    def body(sv_ref, w_ref, o_ref):
        o_ref[...] = w_ref[...].astype(BF16)

    return pl.pallas_call(
        body, name="cast_into_full",
        grid_spec=pltpu.PrefetchScalarGridSpec(num_scalar_prefetch=1, grid=(nl, nb),
                                               in_specs=[pl.BlockSpec((None, tr, c), lambda l, i, sv: (l0 + l, i, 0))], out_specs=out_spec),
        out_shape=_sds(out_shape, BF16), compiler_params=_params(("parallel", "parallel")),
    )(s_vec, w)


def _chip_sum_into(slab, cp, g, g_shape, layer, kind, sc_vec, rows_per_group=None):
    n, r, c = slab.shape
    rg = r if rows_per_group is None else rows_per_group
    tr = _row_tile(rg, c, 256 * 1024)
    groups = r // rg
    nbg = rg // tr
    nb = groups * nbg
    n_sc = len(sc_vec)
    if kind == "col":
        out_spec = pl.BlockSpec((None, tr, c), lambda gi, i, *sc: (layer, sc[1][0] * nb + gi * nbg + i, 0))
        own_spec = pl.BlockSpec((tr, c), lambda gi, i, *sc: (gi * nbg + i, sc[0][0]))
    else:
        out_spec = pl.BlockSpec((None, tr, c), lambda gi, i, *sc: (layer, gi * nbg + i, sc[1][0]))
        own_spec = pl.BlockSpec((tr, c), lambda gi, i, *sc: (gi * (n * nbg) + sc[0][0] * nbg + i, 0))

    def other(k):
        return pl.BlockSpec((None, tr, c), lambda gi, i, *sc: (sc[1 + k][0], gi * nbg + i, 0))

    in_specs = [own_spec] + [other(k) for k in range(1, n)]
    args = list(sc_vec) + [cp] + [slab] * (n - 1)
    aliases = {}
    if g is not None:
        in_specs.append(pl.BlockSpec(memory_space=pl.ANY))
        args.append(g)
        aliases = {len(args) - 1: 0}

    def body(*refs):
        own_ref, rest = refs[n_sc], refs[n_sc + 1:]
        o_ref = rest[-1]
        acc = own_ref[...].astype(F32)
        for k in range(n - 1):
            acc = acc + rest[k][...].astype(F32)
        o_ref[...] = acc

    return pl.pallas_call(
        body, name="chip_sum",
        grid_spec=pltpu.PrefetchScalarGridSpec(num_scalar_prefetch=n_sc, grid=(groups, nbg), in_specs=in_specs, out_specs=out_spec),
        out_shape=_sds(g_shape, F32), input_output_aliases=aliases, compiler_params=_params(("parallel", "parallel")),
    )(*args)


def _pair_add(dw, sib, kind, c_vec):
    r, c = sib.shape
    tr = _row_tile(r, c, 512 * 1024)
    nb = r // tr
    if kind == "col":
        dw_spec = pl.BlockSpec((tr, c), lambda i, cv: (cv[0] * nb + i, 0))
    else:
        dw_spec = pl.BlockSpec((tr, c), lambda i, cv: (i, cv[0]))

    def body(cv_ref, dw_ref, sib_ref, o_ref):
        o_ref[...] = (dw_ref[...].astype(F32) + sib_ref[...].astype(F32)).astype(BF16)

    return pl.pallas_call(
        body, name="pair_add",
        grid_spec=pltpu.PrefetchScalarGridSpec(num_scalar_prefetch=1, grid=(nb,), in_specs=[dw_spec, pl.BlockSpec((tr, c), lambda i, cv: (i, 0))],
                                               out_specs=pl.BlockSpec((tr, c), lambda i, cv: (i, 0))),
        out_shape=_sds((r, c), BF16), compiler_params=_params(("parallel",)),
    )(c_vec, dw, sib)


def _adamw(w, g, m, v, emit_grad=False):
    shape = w.shape
    cols = shape[-1]
    rows = w.size // cols
    tr = _row_tile(rows, cols, 256 * 1024)
    bc1 = 1.0 - ADAM_B1 ** ADAM_STEP
    bc2 = 1.0 - ADAM_B2 ** ADAM_STEP

    n_out = 4 if emit_grad else 3

    def body(w_ref, g_ref, m_ref, v_ref, d_ref, mo_ref, vo_ref, *go_ref):
        gv = g_ref[...]
        mn = ADAM_B1 * m_ref[...] + (1.0 - ADAM_B1) * gv
        vn = ADAM_B2 * v_ref[...] + (1.0 - ADAM_B2) * (gv * gv)
        d_ref[...] = -ADAM_LR * ((mn / bc1) / (jnp.sqrt(vn / bc2) + ADAM_EPS) + ADAM_WD * w_ref[...])
        mo_ref[...] = mn
        vo_ref[...] = vn
        if emit_grad:
            go_ref[0][...] = gv

    spec = pl.BlockSpec((tr, cols), lambda i: (i, 0))
    outs = pl.pallas_call(body, grid=(rows // tr,), name="adamw", in_specs=[spec] * 4, out_specs=[spec] * n_out,
                          out_shape=[_sds((rows, cols), F32)] * n_out, compiler_params=_params(("parallel",)))(
        *[a.reshape(rows, cols) for a in (w, g, m, v)])
    return [o.reshape(shape) for o in outs]


def _mesh_pos():
    return lax.axis_index("x"), lax.axis_index("y"), lax.axis_index("c")


def _other_chips(x, y):
    return [(1 - x, y), (x, 1 - y), (1 - x, 1 - y)]


def _allgather_small(a, after=()):
    r, c = a.shape

    def body(x_ref, *rest):
        out_ref, send_sems, recv_sems, local_sem = rest[len(after):]
        x, y, cc = _mesh_pos()
        me, sibling = (x, y, cc), (x, y, 1 - cc)
        chips = _other_chips(x, y)

        def slab(px, py, pc):
            return out_ref.at[4 * px + 2 * py + pc]

        def copy(k, block, to, src=None):
            return pltpu.make_async_remote_copy(src_ref=slab(*block) if src is None else src, dst_ref=slab(*block),
                                                send_sem=send_sems.at[k], recv_sem=recv_sems.at[k], device_id=to,
                                                device_id_type=MESH)

        mine = pltpu.make_async_copy(x_ref, slab(*me), local_sem)
        mine.start()
        first = [copy(0, me, sibling, src=x_ref)]
        first += [copy(1 + j, me, (*chip, cc), src=x_ref) for j, chip in enumerate(chips)]
        for cp in first:
            cp.start()
        passed = [copy(4 + j, (*chip, cc), sibling) for j, chip in enumerate(chips)]
        for j, chip in enumerate(chips):
            copy(1 + j, (*chip, cc), me).wait_recv()
            passed[j].start()
        copy(0, sibling, me).wait_recv()
        for j, chip in enumerate(chips):
            copy(4 + j, (*chip, 1 - cc), me).wait_recv()
        for cp in first + passed:
            cp.wait_send()
        mine.wait()

    return pl.pallas_call(
        body, name="allgather_small", out_shape=_sds((N_DEV, r, c), F32),
        in_specs=[pl.BlockSpec(memory_space=pltpu.VMEM)] + [pl.BlockSpec(memory_space=pl.ANY)] * len(after),
        out_specs=pl.BlockSpec(memory_space=pltpu.VMEM),
        scratch_shapes=[pltpu.SemaphoreType.DMA((7,)), pltpu.SemaphoreType.DMA((7,)), pltpu.SemaphoreType.DMA],
        compiler_params=pltpu.CompilerParams(vmem_limit_bytes=V7X_VMEM_LIMIT_BYTES),
    )(a, *after)


def _idx(ref, rows=None, cols=None):
    lead = (slice(None),) * (len(ref.shape) - 2)
    return ref.at[lead + (slice(None) if rows is None else rows, slice(None) if cols is None else cols)]


def _half(ref, kind, c):
    r, cdim = ref.shape[-2:]
    if kind == "col":
        return _idx(ref, rows=pl.ds(c * (r // 2), r // 2))
    return _idx(ref, cols=pl.ds(c * (cdim // 2), cdim // 2))


def _shard_region(full, kind, shard_shape, s):
    r, cdim = shard_shape[-2:]
    if kind == "col":
        return _idx(full, cols=pl.ds(s * cdim, cdim))
    return _idx(full, rows=pl.ds(s * r, r))


def _allgather_weights(fulls, kinds, shard_shapes):
    nt = len(fulls)

    def body(*refs):
        fu = refs[nt:2 * nt]
        send_sems, recv_sems = refs[2 * nt:]
        x, y, c = _mesh_pos()
        chips = _other_chips(x, y)

        def part(t, chip, cc):
            return _half(_shard_region(fu[t], kinds[t], shard_shapes[t], 2 * chip[0] + chip[1]), kinds[t], cc)

        def copy(t, k, blk, to):
            return pltpu.make_async_remote_copy(src_ref=blk, dst_ref=blk, send_sem=send_sems.at[6 * t + k],
                                                recv_sem=recv_sems.at[6 * t + k], device_id=to, device_id_type=MESH)

        first, passed = [], []
        for t in range(nt):
            for j, chip in enumerate(chips):
                cp = copy(t, j, part(t, (x, y), c), (*chip, c))
                cp.start()
                first.append(cp)
        for t in range(nt):
            for j, chip in enumerate(chips):
                copy(t, j, part(t, chip, c), (x, y, c)).wait_recv()
                fw = copy(t, 3 + j, part(t, chip, c), (x, y, 1 - c))
                fw.start()
                passed.append(fw)
        for t in range(nt):
            for j, chip in enumerate(chips):
                copy(t, 3 + j, part(t, chip, 1 - c), (x, y, c)).wait_recv()
        for cp in first + passed:
            cp.wait_send()

    return pl.pallas_call(
        body, name="allgather_weights", out_shape=[_sds(f.shape, BF16) for f in fulls],
        in_specs=[HBM] * nt, out_specs=[HBM] * nt, input_output_aliases={t: t for t in range(nt)},
        scratch_shapes=[pltpu.SemaphoreType.DMA((6 * nt,)), pltpu.SemaphoreType.DMA((6 * nt,))],
    )(*fulls)


SEM = pl.BlockSpec(memory_space=pltpu.SEMAPHORE)
ANY = pl.BlockSpec(memory_space=pl.ANY)
EFFECT = pltpu.SideEffectType.DATAFLOW_SIDE_EFFECTING


def _gather_start(name, fulls, kinds, shard_shapes, prev, both=True):
    nt = len(fulls)

    def body(*refs):
        send_sems, recv_sems = refs[nt + 1], refs[nt + 2]
        fu = refs[nt + 3:2 * nt + 3]
        token = refs[2 * nt + 3]
        x, y, c = _mesh_pos()
        for t in range(nt):
            mine = _half(_shard_region(fu[t], kinds[t], shard_shapes[t], 2 * x + y), kinds[t], c)
            for j, chip in enumerate(_other_chips(x, y)):
                if both:
                    for e in range(2):
                        pltpu.make_async_remote_copy(src_ref=mine, dst_ref=mine, send_sem=send_sems.at[6 * t + 2 * j + e],
                                                     recv_sem=recv_sems.at[6 * t + 2 * j + c], device_id=(*chip, e),
                                                     device_id_type=MESH).start()
                else:
                    pltpu.make_async_remote_copy(src_ref=mine, dst_ref=mine, send_sem=send_sems.at[6 * t + 2 * j],
                                                 recv_sem=recv_sems.at[6 * t + 2 * j], device_id=(*chip, c),
                                                 device_id_type=MESH).start()
        token[...] = jnp.zeros_like(token)

    outs = pl.pallas_call(
        body, name=name,
        out_shape=(pltpu.SemaphoreType.DMA((6 * nt,)), pltpu.SemaphoreType.DMA((6 * nt,)), *[pltpu.HBM(f.shape, f.dtype) for f in fulls],
                   _sds((8, 128), F32)),
        in_specs=[HBM] * nt + [ANY], out_specs=(SEM, SEM, *[HBM] * nt, pl.BlockSpec(memory_space=pltpu.VMEM)),
        input_output_aliases={t: 2 + t for t in range(nt)}, compiler_params=pltpu.CompilerParams(has_side_effects=EFFECT),
    )(*fulls, prev)
    return outs[0], outs[1], list(outs[2:2 + nt]), outs[2 + nt]


def _forward_halves(fulls, kinds, shard_shapes):
    nt = len(fulls)

    def body(*refs):
        fu = refs[nt:2 * nt]
        send_sems, recv_sems = refs[2 * nt:]
        x, y, c = _mesh_pos()

        def part(t, chip, cc):
            return _half(_shard_region(fu[t], kinds[t], shard_shapes[t], 2 * chip[0] + chip[1]), kinds[t], cc)

        copies = []
        for t in range(nt):
            for j, chip in enumerate(_other_chips(x, y)):
                blk = part(t, chip, c)
                cp = pltpu.make_async_remote_copy(src_ref=blk, dst_ref=blk, send_sem=send_sems.at[3 * t + j],
                                                  recv_sem=recv_sems.at[3 * t + j], device_id=(x, y, 1 - c), device_id_type=MESH)
                cp.start()
                copies.append(cp)
        for t in range(nt):
            for j, chip in enumerate(_other_chips(x, y)):
                blk = part(t, chip, 1 - c)
                pltpu.make_async_remote_copy(src_ref=blk, dst_ref=blk, send_sem=send_sems.at[3 * t + j],
                                             recv_sem=recv_sems.at[3 * t + j], device_id=(x, y, c), device_id_type=MESH).wait_recv()
        for cp in copies:
            cp.wait_send()

    return pl.pallas_call(
        body, name="forward_halves", out_shape=[_sds(f.shape, BF16) for f in fulls], in_specs=[HBM] * nt, out_specs=[HBM] * nt,
        input_output_aliases={t: t for t in range(nt)},
        scratch_shapes=[pltpu.SemaphoreType.DMA((3 * nt,)), pltpu.SemaphoreType.DMA((3 * nt,))],
    )(*fulls)


def _gather_wait(name, send_sems, recv_sems, fulls, kinds, shard_shapes, after, both=True):
    nt = len(fulls)

    def body(*refs):
        fu = refs[:nt]
        send_sems, recv_sems = refs[nt], refs[nt + 1]
        x, y, c = _mesh_pos()

        def part(t, chip, cc):
            return _half(_shard_region(fu[t], kinds[t], shard_shapes[t], 2 * chip[0] + chip[1]), kinds[t], cc)

        for t in range(nt):
            for j, chip in enumerate(_other_chips(x, y)):
                if not both:
                    mine, landed = part(t, (x, y), c), part(t, chip, c)
                    pltpu.make_async_remote_copy(src_ref=mine, dst_ref=mine, send_sem=send_sems.at[6 * t + 2 * j],
                                                 recv_sem=recv_sems.at[6 * t + 2 * j], device_id=(x, y, c),
                                                 device_id_type=MESH).wait_send()
                    pltpu.make_async_remote_copy(src_ref=landed, dst_ref=landed, send_sem=send_sems.at[6 * t + 2 * j],
                                                 recv_sem=recv_sems.at[6 * t + 2 * j], device_id=(x, y, c),
                                                 device_id_type=MESH).wait_recv()
                    continue
                for e in range(2):
                    mine = part(t, (x, y), c)
                    pltpu.make_async_remote_copy(src_ref=mine, dst_ref=mine, send_sem=send_sems.at[6 * t + 2 * j + e],
                                                 recv_sem=recv_sems.at[6 * t + 2 * j + e], device_id=(x, y, c),
                                                 device_id_type=MESH).wait_send()
                    landed = part(t, chip, e)
                    pltpu.make_async_remote_copy(src_ref=landed, dst_ref=landed, send_sem=send_sems.at[6 * t + 2 * j + e],
                                                 recv_sem=recv_sems.at[6 * t + 2 * j + e], device_id=(x, y, c),
                                                 device_id_type=MESH).wait_recv()

    return pl.pallas_call(
        body, name=name, out_shape=[pltpu.HBM(f.shape, f.dtype) for f in fulls], in_specs=[HBM] * nt + [SEM, SEM, ANY],
        out_specs=[HBM] * nt, input_output_aliases={t: t for t in range(nt)},
        compiler_params=pltpu.CompilerParams(has_side_effects=EFFECT),
    )(*fulls, send_sems, recv_sems, after)


def _reduce_start(name, cps, kinds, prev):
    nt = len(cps)
    slab_shapes = []
    for cp, kind in zip(cps, kinds):
        shp = list(cp.shape)
        shp[-1 if kind == "col" else -2] //= N_CHIP
        slab_shapes.append((N_CHIP,) + tuple(shp))

    def body(*refs):
        send_sems, recv_sems = refs[nt + 1], refs[nt + 2]
        src = refs[nt + 3:2 * nt + 3]
        dst = refs[2 * nt + 3:3 * nt + 3]
        token = refs[3 * nt + 3]
        x, y, c = _mesh_pos()
        s = 2 * x + y
        for t in range(nt):
            for j, chip in enumerate(_other_chips(x, y)):
                pltpu.make_async_remote_copy(src_ref=_chip_block(src[t], kinds[t], 2 * chip[0] + chip[1]), dst_ref=dst[t].at[s],
                                             send_sem=send_sems.at[3 * t + j], recv_sem=recv_sems.at[3 * t + j],
                                             device_id=(*chip, c), device_id_type=MESH).start()
        token[...] = jnp.zeros_like(token)

    outs = pl.pallas_call(
        body, name=name,
        out_shape=(pltpu.SemaphoreType.DMA((3 * nt,)), pltpu.SemaphoreType.DMA((3 * nt,)), *[pltpu.HBM(a.shape, a.dtype) for a in cps],
                   *[pltpu.HBM(shp, BF16) for shp in slab_shapes], _sds((8, 128), F32)),
        in_specs=[HBM] * nt + [ANY], out_specs=(SEM, SEM, *[HBM] * (2 * nt), pl.BlockSpec(memory_space=pltpu.VMEM)),
        input_output_aliases={t: 2 + t for t in range(nt)}, compiler_params=pltpu.CompilerParams(has_side_effects=EFFECT),
    )(*cps, prev)
    return outs[0], outs[1], list(outs[2:2 + nt]), list(outs[2 + nt:2 + 2 * nt]), outs[2 + 2 * nt]


def _reduce_wait(name, send_sems, recv_sems, cps, slabs, kinds, after):
    nt = len(cps)

    def body(*refs):
        src, dst = refs[:nt], refs[nt:2 * nt]
        send_sems, recv_sems = refs[2 * nt], refs[2 * nt + 1]
        x, y, c = _mesh_pos()
        for t in range(nt):
            for j, chip in enumerate(_other_chips(x, y)):
                sj = 2 * chip[0] + chip[1]
                pltpu.make_async_remote_copy(src_ref=_chip_block(src[t], kinds[t], sj), dst_ref=dst[t].at[sj],
                                             send_sem=send_sems.at[3 * t + j], recv_sem=recv_sems.at[3 * t + j],
                                             device_id=(x, y, c), device_id_type=MESH).wait()

    outs = pl.pallas_call(
        body, name=name, out_shape=[pltpu.HBM(a.shape, a.dtype) for a in cps] + [pltpu.HBM(a.shape, a.dtype) for a in slabs],
        in_specs=[HBM] * (2 * nt) + [SEM, SEM, ANY], out_specs=[HBM] * (2 * nt), input_output_aliases={t: t for t in range(2 * nt)},
        compiler_params=pltpu.CompilerParams(has_side_effects=EFFECT),
    )(*cps, *slabs, send_sems, recv_sems, after)
    return list(outs[:nt]), list(outs[nt:])


def _pair_start(name, dws, kinds, prev):
    nt = len(dws)
    sib_shapes = []
    for dw, kind in zip(dws, kinds):
        shp = list(dw.shape)
        shp[-2 if kind == "col" else -1] //= 2
        sib_shapes.append(tuple(shp))

    def body(*refs):
        send_sems, recv_sems = refs[nt + 1], refs[nt + 2]
        src = refs[nt + 3:2 * nt + 3]
        dst = refs[2 * nt + 3:3 * nt + 3]
        token = refs[3 * nt + 3]
        x, y, c = _mesh_pos()
        for t in range(nt):
            pltpu.make_async_remote_copy(src_ref=_half(src[t], kinds[t], 1 - c), dst_ref=dst[t], send_sem=send_sems.at[t],
                                         recv_sem=recv_sems.at[t], device_id=(x, y, 1 - c), device_id_type=MESH).start()
        token[...] = jnp.zeros_like(token)

    outs = pl.pallas_call(
        body, name=name,
        out_shape=(pltpu.SemaphoreType.DMA((nt,)), pltpu.SemaphoreType.DMA((nt,)), *[pltpu.HBM(a.shape, a.dtype) for a in dws],
                   *[pltpu.HBM(shp, BF16) for shp in sib_shapes], _sds((8, 128), F32)),
        in_specs=[HBM] * nt + [ANY], out_specs=(SEM, SEM, *[HBM] * (2 * nt), pl.BlockSpec(memory_space=pltpu.VMEM)),
        input_output_aliases={t: 2 + t for t in range(nt)}, compiler_params=pltpu.CompilerParams(has_side_effects=EFFECT),
    )(*dws, prev)
    return outs[0], outs[1], list(outs[2:2 + nt]), list(outs[2 + nt:2 + 2 * nt]), outs[2 + 2 * nt]


def _pair_wait(name, send_sems, recv_sems, dws, sibs, kinds, after):
    nt = len(dws)

    def body(*refs):
        src, dst = refs[:nt], refs[nt:2 * nt]
        send_sems, recv_sems = refs[2 * nt], refs[2 * nt + 1]
        x, y, c = _mesh_pos()
        for t in range(nt):
            pltpu.make_async_remote_copy(src_ref=_half(src[t], kinds[t], 1 - c), dst_ref=dst[t], send_sem=send_sems.at[t],
                                         recv_sem=recv_sems.at[t], device_id=(x, y, c), device_id_type=MESH).wait()

    outs = pl.pallas_call(
        body, name=name, out_shape=[pltpu.HBM(a.shape, a.dtype) for a in dws] + [pltpu.HBM(a.shape, a.dtype) for a in sibs],
        in_specs=[HBM] * (2 * nt) + [SEM, SEM, ANY], out_specs=[HBM] * (2 * nt), input_output_aliases={t: t for t in range(2 * nt)},
        compiler_params=pltpu.CompilerParams(has_side_effects=EFFECT),
    )(*dws, *sibs, send_sems, recv_sems, after)
    return list(outs[:nt]), list(outs[nt:])


def _pair_exchange(dws, kinds):
    nt = len(dws)
    out_shapes = []
    for dw, kind in zip(dws, kinds):
        shp = list(dw.shape)
        shp[-2 if kind == "col" else -1] //= 2
        out_shapes.append(tuple(shp))

    def body(*refs):
        src, dst = refs[:nt], refs[nt:2 * nt]
        send_sems, recv_sems = refs[2 * nt:]
        x, y, c = _mesh_pos()
        copies = [pltpu.make_async_remote_copy(src_ref=_half(src[t], kinds[t], 1 - c), dst_ref=dst[t], send_sem=send_sems.at[t],
                                               recv_sem=recv_sems.at[t], device_id=(x, y, 1 - c), device_id_type=MESH)
                  for t in range(nt)]
        for cp in copies:
            cp.start()
        for cp in copies:
            cp.wait()

    return pl.pallas_call(
        body, name="grad_pair_exchange", out_shape=[_sds(shp, BF16) for shp in out_shapes], in_specs=[HBM] * nt,
        out_specs=[HBM] * nt, scratch_shapes=[pltpu.SemaphoreType.DMA((nt,)), pltpu.SemaphoreType.DMA((nt,))],
    )(*dws)


def _chip_block(ref, kind, s):
    r, cdim = ref.shape[-2:]
    if kind == "col":
        return _idx(ref, cols=pl.ds(s * (cdim // N_CHIP), cdim // N_CHIP))
    return _idx(ref, rows=pl.ds(s * (r // N_CHIP), r // N_CHIP))


def _chip_exchange(cps, kinds):
    nt = len(cps)
    out_shapes = []
    for cp, kind in zip(cps, kinds):
        shp = list(cp.shape)
        shp[-1 if kind == "col" else -2] //= N_CHIP
        out_shapes.append((N_CHIP,) + tuple(shp))

    def body(*refs):
        src, dst = refs[:nt], refs[nt:2 * nt]
        send_sems, recv_sems, local_sems = refs[2 * nt:]
        x, y, c = _mesh_pos()
        s = 2 * x + y
        chips = _other_chips(x, y)
        sends, locals_ = [], []
        for t in range(nt):
            own = pltpu.make_async_copy(_chip_block(src[t], kinds[t], s), dst[t].at[s], local_sems.at[t])
            own.start()
            locals_.append(own)
            for j, chip in enumerate(chips):
                cp = pltpu.make_async_remote_copy(src_ref=_chip_block(src[t], kinds[t], 2 * chip[0] + chip[1]), dst_ref=dst[t].at[s],
                                                  send_sem=send_sems.at[3 * t + j], recv_sem=recv_sems.at[3 * t + j],
                                                  device_id=(*chip, c), device_id_type=MESH)
                cp.start()
                sends.append(cp)
        for t in range(nt):
            for j, chip in enumerate(chips):
                landing = dst[t].at[2 * chip[0] + chip[1]]
                pltpu.make_async_remote_copy(src_ref=landing, dst_ref=landing, send_sem=send_sems.at[3 * t + j],
                                             recv_sem=recv_sems.at[3 * t + j], device_id=(x, y, c), device_id_type=MESH).wait_recv()
        for cp in sends:
            cp.wait_send()
        for own in locals_:
            own.wait()

    return pl.pallas_call(
        body, name="grad_chip_exchange", out_shape=[_sds(shp, BF16) for shp in out_shapes], in_specs=[HBM] * nt,
        out_specs=[HBM] * nt,
        scratch_shapes=[pltpu.SemaphoreType.DMA((3 * nt,)), pltpu.SemaphoreType.DMA((3 * nt,)), pltpu.SemaphoreType.DMA((nt,))],
    )(*cps)


def _pair_assemble(gs, kinds):
    nt = len(gs)

    def body(*refs):
        g = refs[nt:2 * nt]
        send_sems, recv_sems = refs[2 * nt:]
        x, y, c = _mesh_pos()
        copies = []
        for t in range(nt):
            mine = _half(g[t], kinds[t], c)
            cp = pltpu.make_async_remote_copy(src_ref=mine, dst_ref=mine, send_sem=send_sems.at[t], recv_sem=recv_sems.at[t],
                                              device_id=(x, y, 1 - c), device_id_type=MESH)
            cp.start()
            copies.append(cp)
        for t in range(nt):
            landing = _half(g[t], kinds[t], 1 - c)
            pltpu.make_async_remote_copy(src_ref=landing, dst_ref=landing, send_sem=send_sems.at[t], recv_sem=recv_sems.at[t],
                                         device_id=(x, y, c), device_id_type=MESH).wait_recv()
        for cp in copies:
            cp.wait_send()

    return pl.pallas_call(
        body, name="grad_pair_assemble", out_shape=[_sds(a.shape, F32) for a in gs], in_specs=[HBM] * nt,
        out_specs=[HBM] * nt, input_output_aliases={t: t for t in range(nt)},
        scratch_shapes=[pltpu.SemaphoreType.DMA((nt,)), pltpu.SemaphoreType.DMA((nt,))],
    )(*gs)


def _pack(arrays, width):
    flat = jnp.concatenate([a.reshape(-1) for a in arrays])
    pad = (-flat.size) % (8 * width)
    return jnp.pad(flat, (0, pad)).reshape(-1, width)


def _unpack(packed, shapes):
    flat = packed.reshape(-1)
    out, off = [], 0
    for shp in shapes:
        n = 1
        for dim in shp:
            n *= dim
        out.append(flat[off:off + n].reshape(shp))
        off += n
    return out


def kernel(x, c, ada_w, ada_b, norm_mix_g, norm_ffn_g, ab_w_in, a_conv_w, a_conv_b, a_norm_g, a_norm_b, b_norm_g, b_norm_b, b_w_s, b_bias, ab_w_out, pool_w, pool_scale, ffn_w1, ffn_w3, ffn_w2, final_g, loss_target, m_ada_w, m_ada_b, m_norm_mix_g, m_norm_ffn_g, m_ab_w_in, m_a_conv_w, m_a_conv_b, m_a_norm_g, m_a_norm_b, m_b_norm_g, m_b_norm_b, m_b_w_s, m_b_bias, m_ab_w_out, m_pool_w, m_pool_scale, m_ffn_w1, m_ffn_w3, m_ffn_w2, m_final_g, v_ada_w, v_ada_b, v_norm_mix_g, v_norm_ffn_g, v_ab_w_in, v_a_conv_w, v_a_conv_b, v_a_norm_g, v_a_norm_b, v_b_norm_g, v_b_norm_b, v_b_w_s, v_b_bias, v_ab_w_out, v_pool_w, v_pool_scale, v_ffn_w1, v_ffn_w3, v_ffn_w2, v_final_g):
    mx, my, mc = _mesh_pos()
    chip = 2 * mx + my
    dev = 4 * mx + 2 * my + mc
    x2 = x[0]
    target = loss_target[0]
    s, d = x2.shape
    depth = ada_w.shape[0]
    n_mod = ada_b.shape[1] // d
    n_even = ab_w_in.shape[0]
    da, db = a_conv_b.shape[1], b_norm_g.shape[1]
    nh = b_w_s.shape[1]
    kw = a_conv_w.shape[1]
    n_pool = pool_w.shape[1]
    tm_row = _pick(s, (256, 128))

    s_vec = jnp.reshape(chip, (1,)).astype(jnp.int32)
    c_vec = jnp.reshape(mc, (1,)).astype(jnp.int32)
    sc_vec = [s_vec, c_vec] + [jnp.reshape(v, (1,)).astype(jnp.int32)
                               for v in (2 * mx + (1 - my), 2 * (1 - mx) + my, 2 * (1 - mx) + (1 - my))]
    pool_w3 = pool_w.reshape((-1,) + pool_w.shape[2:])
    shard_of = {"w_in": (ab_w_in, "col"), "w_out": (ab_w_out, "row"), "pool": (pool_w3, "row"), "w1": (ffn_w1, "col"),
                "w3": (ffn_w3, "col"), "w2": (ffn_w2, "row")}

    def layer_names(l):
        return (["w_in", "w_out"] if l % 2 == 0 else ["pool"]) + ["w1", "w3", "w2"]

    def layer_span(nm, l):
        if nm in ("w_in", "w_out"):
            return l // 2, 1
        if nm == "pool":
            return (l // 2) * n_pool, n_pool
        return l, 1

    group_names = {"0a": ["w_in"], "0o": ["w_out"], "0b": ["w1", "w3"], "0c": ["w2"]}
    group_layer = {key: 0 for key in group_names}
    for l in range(1, depth):
        group_names[l], group_layer[l] = layer_names(l), l
    owned, w_kinds, w_shapes = {}, {}, {}
    for key, names in group_names.items():
        l = group_layer[key]
        owned[key] = [_cast_into_full(shard_of[nm][0], shard_of[nm][1], s_vec, *layer_span(nm, l)) for nm in names]
        w_kinds[key] = [shard_of[nm][1] for nm in names]
        w_shapes[key] = [(layer_span(nm, l)[1],) + shard_of[nm][0].shape[1:] for nm in names]
    layer_w = [{} for _ in range(depth)]
    layer_w[0].update(zip(group_names["0a"], _allgather_weights(owned["0a"], w_kinds["0a"], w_shapes["0a"])))
    gathers = {"prev": layer_w[0][group_names["0a"][-1]], "token": None, "flying": {}}

    passed_on = {"0b", 1, 2}

    def gather_start(key):
        sends, recvs, fulls, token = _gather_start(f"gather_start_{key}", owned[key], w_kinds[key], w_shapes[key], gathers["prev"],
                                                   both=key not in passed_on)
        gathers["flying"][key] = (sends, recvs, fulls)
        gathers["prev"] = gathers["token"] = token

    def gather_wait(key, after):
        sends, recvs, fulls = gathers["flying"].pop(key)
        landed = _gather_wait(f"gather_wait_{key}", sends, recvs, fulls, w_kinds[key], w_shapes[key], after, both=key not in passed_on)
        if key in passed_on:
            landed = _forward_halves(landed, w_kinds[key], w_shapes[key])
        layer_w[group_layer[key]].update(zip(group_names[key], landed))
        gathers["prev"] = landed[-1]

    def after_starts(row):
        return row + gathers["token"][0:1, 0:1]

    for key in ("0o", "0b", "0c"):
        gather_start(key)

    pre = _allgather_small(_pack([c, a_conv_w, pool_scale], 128)).reshape(N_DEV, -1)
    n_cw, n_ps = a_conv_w.size, pool_scale.size
    c_all = pre[:, :d]
    cw_chips = pre[0::2, d:d + n_cw].reshape((N_CHIP,) + a_conv_w.shape)
    conv_w_full = jnp.concatenate([cw_chips[k] for k in range(N_CHIP)], axis=-1)
    ps_chips = pre[0::2, d + n_cw:d + n_cw + n_ps].reshape((N_CHIP,) + pool_scale.shape)
    pool_scale_full = jnp.concatenate([ps_chips[k] for k in range(N_CHIP)], axis=-1)
    c_pad = jnp.pad(c_all, ((0, 8), (0, 0)))
    n_ada = ada_w.shape[2]
    ada_b_mine = lax.dynamic_slice_in_dim(ada_b, chip * n_ada, n_ada, axis=1)[:, None, :]
    mod_part = _ada_fwd(c_pad, ada_w, ada_b_mine)[:, :N_DEV, :]
    mod_all = _allgather_small(mod_part.reshape(depth * N_DEV, n_ada))
    mod_chips = mod_all[0::2].reshape(N_CHIP, depth, N_DEV, n_ada)
    mod_mine = lax.dynamic_index_in_dim(mod_chips, dev, axis=2, keepdims=False)
    mod = jnp.transpose(mod_mine, (1, 0, 2)).reshape(depth, n_mod, 1, d)

    causal = jnp.tril(jnp.ones((CHUNK, CHUNK), dtype=bool))
    w_c = jnp.where(causal[None, None], b_w_s, 0.0).astype(BF16)
    bias_full = jnp.repeat(jnp.swapaxes(b_bias, 1, 2), CHUNK, axis=2)

    saved = []
    xs = x2
    for l in range(depth):
        sh1, sc1, g1, sh2, sc2, g2 = [mod[l, k] for k in range(n_mod)]
        i = l // 2
        st = {"x1": xs}
        if 1 <= l and l + 2 < depth:
            gather_start(l + 2)
        gain1 = after_starts(norm_mix_g[l][None])
        wl = layer_w[l]
        if l % 2 == 0:
            h = _norm_mod(xs, gain1, sh1, sc1, tm_row)
            proj = _mm("ab_proj", [h], [(wl["w_in"], 0)], [BF16], _ep_store, tk=2048)[0]
            cat, a1 = _ab_mid_fwd(proj, conv_w_full[i], a_conv_b[i][None], a_norm_g[i][None], a_norm_b[i][None],
                                  b_norm_g[i][None], b_norm_b[i][None], w_c[i], bias_full[i], tm_row)
            if l == 0:
                gather_wait("0o", cat)
            xs, y1 = _mm("ab_out", [cat], [(wl["w_out"], 0)], [F32, BF16], _ep_residual, extras=[xs, g1], extra_kinds=["tile", "row"], tk=2048)
            st.update(h=h, proj=proj, a1=a1, cat=cat, y=y1)
        else:
            p = _pool_fwd(xs, gain1, sh1, sc1, tm_row)
            gate = g1 * pool_scale_full[i][None]
            xs, ymm = _grouped_fwd(p, wl["pool"][None], 0, xs, gate, 1024)
            st.update(p=p, y=ymm, gate=gate)
        st["x2"] = xs
        gain2 = norm_ffn_g[l][None]
        if l == 0:
            gather_wait("0b", xs)
            for nxt in range(1, min(3, depth)):
                gather_start(nxt)
            gain2 = after_starts(gain2)
        h2 = _norm_mod(xs, gain2, sh2, sc2, tm_row)
        u, t, z = _mm("ffn_up", [h2, h2], [(wl["w1"], 0), (wl["w3"], 0)], [BF16, BF16, BF16], _ep_swiglu, tn=512, tk=2048)
        if l == 0:
            gather_wait("0c", z)
        xs, y2 = _mm("ffn_down", [z], [(wl["w2"], 0)], [F32, BF16], _ep_residual, extras=[xs, g2], extra_kinds=["tile", "row"],
                     tn=512, tk=ffn_w2.shape[1] * N_CHIP)
        if l + 1 < depth:
            gather_wait(l + 1, xs)
        st.update(h2=h2, u=u, t=t, z=z, y2=y2)
        saved.append(st)

    def below_of(l, which):
        if which == "ffn":
            return saved[l]["y2"], mod[l, n_mod - 1]
        return saved[l]["y"], (mod[l, 2] if l % 2 == 0 else saved[l]["gate"])

    dx, fin_acc, loss_blk, dyb, gacc = _final_loss_bwd(xs, final_g[None], target, tm_row, below_of(depth - 1, "ffn"))
    loss = lax.psum(loss_blk[0, 0], MESH_AXES)
    d_final_g = fin_acc[0]
    dmod_rows = [None] * depth
    d_norm_mix, d_norm_ffn = [None] * depth, [None] * depth
    even_small = [None] * n_even
    d_pool_scale = [None] * (depth // 2)
    grad_names = ["w_in", "w_out", "pool", "w1", "w3", "w2"]
    g_shapes = {"w_in": ab_w_in.shape, "w_out": ab_w_out.shape, "pool": (pool_w.shape[0], n_pool * pool_w.shape[2], pool_w.shape[3]),
                "w1": ffn_w1.shape, "w3": ffn_w3.shape, "w2": ffn_w2.shape}
    shard_grads = {nm: None for nm in grad_names}
    pipe = {"pair": None, "chip": None, "prev": None, "token": None}

    def finish_chip(after):
        tag, pl_, names_, kinds_, sends, recvs, cps_f, slabs_f = pipe["chip"]
        cps_d, slabs_d = _reduce_wait(f"reduce_wait_{tag}", sends, recvs, cps_f, slabs_f, kinds_, after)
        for nm, kind, cp, sl in zip(names_, kinds_, cps_d, slabs_d):
            rpg = pool_w.shape[2] if nm == "pool" else None
            cp2 = cp.reshape(-1, cp.shape[-1])
            shard_grads[nm] = _chip_sum_into(sl.reshape(N_CHIP, -1, sl.shape[-1]), cp2, shard_grads[nm], g_shapes[nm],
                                             layer_span(nm, pl_)[0] // (n_pool if nm == "pool" else 1), kind, sc_vec, rpg)
        pipe["chip"] = None
        pipe["prev"] = slabs_d[-1]

    def settle(after):
        made = []
        if pipe["pair"] is not None:
            tag, lyr, names, kinds_l, sends, recvs, dws_f, sibs_f = pipe["pair"]
            dws_d, sibs_d = _pair_wait(f"pair_wait_{tag}", sends, recvs, dws_f, sibs_f, kinds_l, after)
            cps = []
            for dw, sib, kind in zip(dws_d, sibs_d, kinds_l):
                cp = _pair_add(dw.reshape(-1, dw.shape[-1]), sib.reshape(-1, sib.shape[-1]), kind, c_vec)
                cps.append(cp.reshape(sib.shape))
            pipe["pair"] = None
            pipe["ready"] = (tag, lyr, names, kinds_l, cps)
            made.append(cps[-1])
        if pipe["chip"] is not None:
            finish_chip(after)
            made.append(pipe["prev"])
        return made

    def advance(after, new=None):
        settle(after)
        if pipe.get("ready") is not None:
            tag, lyr, names, kinds_l, cps = pipe.pop("ready")
            prev = cps[-1] if pipe["prev"] is None else pipe["prev"]
            sends, recvs, cps_f, slabs_f, token = _reduce_start(f"reduce_start_{tag}", cps, kinds_l, prev)
            pipe["chip"] = (tag, lyr, names, kinds_l, sends, recvs, cps_f, slabs_f)
            pipe["token"] = token
        if new is not None:
            tag, lyr, names, big_ = new
            kinds_l = [shard_of[nm][1] for nm in names]
            dws = [big_[nm] for nm in names]
            prev = dws[-1] if pipe["token"] is None else pipe["token"]
            sends, recvs, dws_f, sibs_f, token = _pair_start(f"pair_start_{tag}", dws, kinds_l, prev)
            pipe["pair"] = (tag, lyr, names, kinds_l, sends, recvs, dws_f, sibs_f)
            pipe["token"] = token

    def behind(row):
        return row + pipe["token"][0:1, 0:1]

    for l in reversed(range(depth)):
        sh1, sc1, g1, sh2, sc2, g2 = [mod[l, k] for k in range(n_mod)]
        st = saved[l]
        wl = layer_w[l]
        i = l // 2
        big = {}
        d_g2 = gacc[0]
        du, dt = _mm("ffn_dz", [dyb], [(wl["w2"], 0)], [BF16, BF16], _ep_swiglu_bwd, trans_b=True, extras=[st["u"], st["t"]],
                     extra_kinds=["tile", "tile"], tm=2048, tn=512, tk=2048)
        big["w2"] = _mm("ffn_dw2", [st["z"]], [dyb], [BF16], _ep_store, trans_a=True, tm=512, tn=512, tk=s)[0]
        big["w1"] = _mm("ffn_dw1", [st["h2"]], [du], [BF16], _ep_store, trans_a=True, tm=512, tn=512, tk=s)[0]
        big["w3"] = _mm("ffn_dw3", [st["h2"]], [dt], [BF16], _ep_store, trans_a=True, tm=512, tn=512, tk=s)[0]
        advance(big["w3"], (f"ffn{l}", l, ["w1", "w3", "w2"], big))
        dh = _mm("ffn_dh", [du, dt], [(wl["w1"], 0), (wl["w3"], 0)], [BF16], _ep_sum, trans_b=True, tn=512, tk=ffn_w2.shape[1] * 2)[0]
        dx, nacc, dyb, gacc = _norm_mod_bwd(dh, st["x2"], dx, behind(norm_ffn_g[l][None]), sc2, tm_row, below_of(l, "mix"))
        d_sh2, d_sc2, d_norm_ffn[l] = nacc[0], nacc[1], nacc[2]
        if l % 2 == 0:
            d_g1 = gacc[0]
            big["w_out"] = _mm("ab_dw_out", [st["cat"]], [dyb], [BF16], _ep_store, trans_a=True, tm=512, tn=512, tk=s)[0]
            dcat = _mm("ab_dcat", [dyb], [(wl["w_out"], 0)], [BF16], _ep_store, trans_b=True, tk=2048)[0]
            advance(dcat)
            dproj, dcw, vecs, dws, dbias = _ab_mid_bwd(dcat, st["proj"], st["a1"], conv_w_full[i], behind(a_norm_g[i][None]),
                                                       a_norm_b[i][None], b_norm_g[i][None], b_norm_b[i][None], w_c[i],
                                                       bias_full[i], tm_row)
            even_small[i] = dict(conv_w=dcw, a_norm_g=vecs[0, :da], a_norm_b=vecs[1, :da], conv_b=vecs[2, :da],
                                 b_norm_g=vecs[3, :db], b_norm_b=vecs[4, :db], w_s=dws, bias=dbias[:, :, 0])
            big["w_in"] = _mm("ab_dw_in", [st["h"]], [dproj], [BF16], _ep_store, trans_a=True, tm=512, tn=512, tk=s)[0]
            advance(big["w_in"], (f"mix{l}", l, ["w_in", "w_out"], big))
            dh = _mm("ab_dh", [dproj], [(wl["w_in"], 0)], [BF16], _ep_store, trans_b=True, tn=512, tk=2 * da + 2 * db)[0]
            below = below_of(l - 1, "ffn") if l > 0 else None
            outs = _norm_mod_bwd(dh, st["x1"], dx, behind(norm_mix_g[l][None]), sc1, tm_row, below)
            dx, nacc = outs[0], outs[1]
            if below is not None:
                dyb, gacc = outs[2], outs[3]
        else:
            d_g1 = gacc[0] * pool_scale_full[i]
            d_pool_scale[i] = gacc[0] * g1[0]
            big["pool"] = _grouped_dw(st["p"], dyb, n_pool, 1024)
            advance(big["pool"], (f"mix{l}", l, ["pool"], big))
            dp = _grouped_dx(dyb, wl["pool"][None], 0, 1024)
            dx, nacc, dyb, gacc = _pool_bwd(dp, st["x1"], dx, behind(norm_mix_g[l][None]), sc1, tm_row, below_of(l - 1, "ffn"))
        d_sh1, d_sc1, d_norm_mix[l] = nacc[0], nacc[1], nacc[2]
        dmod_rows[l] = jnp.concatenate([d_sh1, d_sc1, d_g1, d_sh2, d_sc2, d_g2])
    grad_x = dx[None]
    settled = settle(dx)

    dmod = jnp.stack(dmod_rows)
    small = [dmod, jnp.stack(d_norm_mix), jnp.stack(d_norm_ffn),
             jnp.stack([e["conv_w"] for e in even_small]), jnp.stack([e["conv_b"] for e in even_small]),
             jnp.stack([e["a_norm_g"] for e in even_small]), jnp.stack([e["a_norm_b"] for e in even_small]),
             jnp.stack([e["b_norm_g"] for e in even_small]), jnp.stack([e["b_norm_b"] for e in even_small]),
             jnp.stack([e["w_s"] for e in even_small]), jnp.stack([e["bias"] for e in even_small]),
             jnp.stack(d_pool_scale), d_final_g]
    small_shapes = [a.shape for a in small]
    width = 1024 if d >= 1024 else 128
    gathered = _allgather_small(_pack(small, width), after=settled)
    summed = _unpack(_sum_leading(gathered), small_shapes)
    (g_ada_b, g_norm_mix, g_norm_ffn, g_conv_w_full, g_conv_b, g_a_norm_g, g_a_norm_b, g_b_norm_g, g_b_norm_b, g_w_s, g_bias,
     g_pool_scale_full, g_final_g) = summed
    cw_cols = a_conv_w.shape[2]
    g_conv_w = lax.dynamic_slice_in_dim(g_conv_w_full, chip * cw_cols, cw_cols, axis=2)
    ps_cols = pool_scale.shape[1]
    g_pool_scale = lax.dynamic_slice_in_dim(g_pool_scale_full, chip * ps_cols, ps_cols, axis=1)

    dmod_all = gathered.reshape(N_DEV, -1)[:, :dmod.size].reshape(N_DEV, depth, n_mod * d)
    dmod_cols = lax.dynamic_slice_in_dim(dmod_all, chip * n_ada, n_ada, axis=2)
    dmod_cols = jnp.pad(jnp.transpose(dmod_cols, (1, 0, 2)), ((0, 0), (0, 8), (0, 0)))
    g_ada_w = _ada_bwd(c_pad, dmod_cols)

    weights = [ada_w, ada_b, norm_mix_g, norm_ffn_g, ab_w_in, a_conv_w, a_conv_b, a_norm_g, a_norm_b, b_norm_g, b_norm_b, b_w_s,
               b_bias, ab_w_out, pool_w, pool_scale, ffn_w1, ffn_w3, ffn_w2, final_g]
    ms = [m_ada_w, m_ada_b, m_norm_mix_g, m_norm_ffn_g, m_ab_w_in, m_a_conv_w, m_a_conv_b, m_a_norm_g, m_a_norm_b, m_b_norm_g,
          m_b_norm_b, m_b_w_s, m_b_bias, m_ab_w_out, m_pool_w, m_pool_scale, m_ffn_w1, m_ffn_w3, m_ffn_w2, m_final_g]
    vs = [v_ada_w, v_ada_b, v_norm_mix_g, v_norm_ffn_g, v_ab_w_in, v_a_conv_w, v_a_conv_b, v_a_norm_g, v_a_norm_b, v_b_norm_g,
          v_b_norm_b, v_b_w_s, v_b_bias, v_ab_w_out, v_pool_w, v_pool_scale, v_ffn_w1, v_ffn_w3, v_ffn_w2, v_final_g]
    grads = [g_ada_w, g_ada_b, g_norm_mix, g_norm_ffn, None, g_conv_w, g_conv_b, g_a_norm_g, g_a_norm_b, g_b_norm_g,
             g_b_norm_b, g_w_s, g_bias, None, None, g_pool_scale, None, None, None, g_final_g]
    updates = [None] * len(weights)

    def update(k, emit_grad=False):
        grads[k] = grads[k].reshape(weights[k].shape)
        updates[k] = _adamw(weights[k], grads[k], ms[k], vs[k], emit_grad)
        if emit_grad:
            grads[k] = updates[k][3]

    advance(g_ada_w)
    for k in range(1, len(weights)):
        if grads[k] is not None:
            update(k)
    update(0)
    advance(updates[0][0])
    big_at = {"w_in": 4, "w_out": 13, "pool": 14, "w1": 16, "w3": 17, "w2": 18}
    assembled = _pair_assemble([shard_grads[nm] for nm in grad_names], [shard_of[nm][1] for nm in grad_names])
    for nm, g in zip(grad_names, assembled):
        grads[big_at[nm]] = g
        update(big_at[nm], emit_grad=True)
    return (loss, grad_x, *grads, *[u[0] for u in updates], *[u[1] for u in updates], *[u[2] for u in updates])
```
